```python
import jax, jax.numpy as jnp
from jax import lax
import numpy as np

D_MODEL = 1024
BATCH = 8
SEQ = 4096
DEPTH = 2

RET_HEADS = 4
RET_QK_DIM = D_MODEL // RET_HEADS
RET_V_DIM = 2 * RET_QK_DIM
RET_QK_W = RET_HEADS * RET_QK_DIM
RET_V_W = RET_HEADS * RET_V_DIM
RET_CHUNK = 128
ROPE_BASE = 10000.0
SB_HEADS = 16
SB_DIM = 64
SB_W = SB_HEADS * SB_DIM
SB_BLOCK = 128
D_FF = 4 * D_MODEL
N_MOD = 6
EPS = 1e-6
GN_EPS = 1e-5
IN_SPLITS = (RET_QK_W, RET_QK_W, RET_V_W, RET_V_W, SB_W, SB_W, SB_W, D_MODEL, D_MODEL)
IN_W = sum(IN_SPLITS)

kernel_name = "hybrid_retention_stickbreaking_gated_block"


def _rmsnorm(x, g):
    xf = x.astype(jnp.float32)
    y = xf * lax.rsqrt(jnp.mean(xf * xf, axis=-1, keepdims=True) + EPS)
    return (y * g.astype(jnp.float32)).astype(x.dtype)


def _modulate(h, shift, scale):
    return h * (1.0 + scale[:, None, :]) + shift[:, None, :]


def _heads(t, n_heads):
    b, s, w = t.shape
    return t.reshape(b, s, n_heads, w // n_heads).transpose(0, 2, 1, 3)


def _merge_heads(t):
    b, h, s, d = t.shape
    return t.transpose(0, 2, 1, 3).reshape(b, s, h * d)


def _rotary(t):
    s, d = t.shape[2], t.shape[3]
    half = d // 2
    inv_freq = jnp.power(ROPE_BASE, -jnp.arange(half, dtype=jnp.float32) / half)
    ang = jnp.arange(s, dtype=jnp.float32)[:, None] * inv_freq[None, :]
    cos, sin = jnp.cos(ang), jnp.sin(ang)
    t1, t2 = t[..., :half], t[..., half:]
    return jnp.concatenate([t1 * cos - t2 * sin, t1 * sin + t2 * cos], axis=-1)


def _retention(q, k, v):
    out_dtype = v.dtype
    q = q.astype(jnp.float32)
    k = k.astype(jnp.float32) * (q.shape[-1] ** -0.5)
    v = v.astype(jnp.float32)
    b, h, s, dk = q.shape
    dv = v.shape[-1]
    c = RET_CHUNK
    n = s // c
    log_gamma = jnp.log1p(-jnp.power(2.0, -5.0 - jnp.arange(h, dtype=jnp.float32)))
    idx = jnp.arange(c, dtype=jnp.float32)
    rel = idx[:, None] - idx[None, :]
    decay = jnp.where(rel >= 0, jnp.exp(jnp.maximum(rel, 0.0) * log_gamma[:, None, None]), 0.0)
    xi = jnp.exp((idx + 1.0) * log_gamma[:, None])
    zeta = jnp.exp((c - 1.0 - idx) * log_gamma[:, None])
    gamma_c = jnp.exp(c * log_gamma)

    qc = q.reshape(b, h, n, c, dk)
    kc = k.reshape(b, h, n, c, dk)
    vc = v.reshape(b, h, n, c, dv)
    scores = jnp.einsum('bhncd,bhnmd->bhncm', qc, kc) * decay[None, :, None]
    o_inner = jnp.einsum('bhncm,bhnme->bhnce', scores, vc)

    def step(r, qkv):
        qn, kn, vn = qkv
        cross = jnp.einsum('bhcd,bhde->bhce', qn, r) * xi[None, :, :, None]
        r_new = r * gamma_c[None, :, None, None] + jnp.einsum(
            'bhcd,bhce->bhde', kn * zeta[None, :, :, None], vn)
        return r_new, cross

    r0 = jnp.zeros((b, h, dk, dv), jnp.float32)
    xs = (jnp.moveaxis(qc, 2, 0), jnp.moveaxis(kc, 2, 0), jnp.moveaxis(vc, 2, 0))
    _, o_cross = lax.scan(step, r0, xs)
    o = (o_inner + jnp.moveaxis(o_cross, 0, 2)).reshape(b, h, s, dv)
    mu = jnp.mean(o, axis=-1, keepdims=True)
    var = jnp.mean(jnp.square(o - mu), axis=-1, keepdims=True)
    return ((o - mu) * lax.rsqrt(var + GN_EPS)).astype(out_dtype)


def _stick_breaking(q, k, v):
    out_dtype = v.dtype
    b, h, s, d = q.shape
    nb = s // SB_BLOCK
    kf = k.astype(jnp.float32)
    vf = v.astype(jnp.float32)
    qb = jnp.moveaxis(q.astype(jnp.float32).reshape(b, h, nb, SB_BLOCK, d), 2, 0)
    kpos = jnp.arange(s)
    scale = d ** -0.5

    def block(args):
        qi, i = args
        qpos = i * SB_BLOCK + jnp.arange(SB_BLOCK)
        causal = kpos[None, :] < qpos[:, None]
        z = jnp.einsum('bhtd,bhsd->bhts', qi, kf) * scale
        log_1m_beta = jnp.where(causal, -jax.nn.softplus(z), 0.0)
        rev_incl = lax.cumsum(log_1m_beta, axis=3, reverse=True)
        rev_excl = jnp.concatenate([rev_incl[..., 1:], jnp.zeros_like(rev_incl[..., :1])], axis=-1)
        a = jnp.where(causal, jnp.exp(jax.nn.log_sigmoid(z) + rev_excl), 0.0)
        return jnp.einsum('bhts,bhsd->bhtd', a, vf)

    out = lax.map(block, (qb, jnp.arange(nb)))
    return jnp.moveaxis(out, 0, 2).reshape(b, h, s, d).astype(out_dtype)


def _mixer(h, w_in, w_ret_out, w_sb_out, w_mix_out):
    p = h @ w_in
    offsets = [int(o) for o in np.cumsum(IN_SPLITS)[:-1]]
    rq, rk, rv, rg, sq, sk, sv, ga, gb = jnp.split(p, offsets, axis=-1)
    y_ret = _retention(_rotary(_heads(rq, RET_HEADS)), _rotary(_heads(rk, RET_HEADS)), _heads(rv, RET_HEADS))
    y_a = (jax.nn.silu(rg) * _merge_heads(y_ret)) @ w_ret_out
    y_sb = _stick_breaking(_heads(sq, SB_HEADS), _heads(sk, SB_HEADS), _heads(sv, SB_HEADS))
    y_b = _merge_heads(y_sb) @ w_sb_out
    merged = jax.nn.sigmoid(ga) * y_a + jax.nn.sigmoid(gb) * y_b
    return merged @ w_mix_out


def _fwd_setup_inputs(seed: int = 0) -> dict:
    key = jax.random.key(seed)
    ks = jax.random.split(key, 16)
    f32 = jnp.float32

    def w(k, shape, fan_in):
        return jax.random.normal(k, shape, f32) * (fan_in ** -0.5)

    def gain(k, shape):
        return 1.0 + 0.02 * jax.random.normal(k, shape, f32)

    return {
        "x": jax.random.normal(ks[0], (BATCH, SEQ, D_MODEL), f32),
        "c": jax.random.normal(ks[1], (BATCH, D_MODEL), f32),
        "norm_mix_g": gain(ks[2], (DEPTH, D_MODEL)),
        "w_in": w(ks[3], (DEPTH, D_MODEL, IN_W), D_MODEL),
        "w_ret_out": w(ks[4], (DEPTH, RET_V_W, D_MODEL), RET_V_W),
        "w_sb_out": w(ks[5], (DEPTH, SB_W, D_MODEL), SB_W),
        "w_mix_out": w(ks[6], (DEPTH, D_MODEL, D_MODEL), D_MODEL),
        "norm_mlp_g": gain(ks[7], (DEPTH, D_MODEL)),
        "w_up": w(ks[8], (DEPTH, D_MODEL, D_FF), D_MODEL),
        "w_down": w(ks[9], (DEPTH, D_FF, D_MODEL), D_FF),
        "w_ada": w(ks[10], (DEPTH, D_MODEL, N_MOD * D_MODEL), D_MODEL),
        "b_ada": 0.02 * jax.random.normal(ks[11], (DEPTH, N_MOD * D_MODEL), f32),
        "final_g": gain(ks[12], (D_MODEL,)),
    }


def _fwd_reference(x, c, norm_mix_g, w_in, w_ret_out, w_sb_out, w_mix_out, norm_mlp_g,
              w_up, w_down, w_ada, b_ada, final_g):
    c_act = jax.nn.silu(c)
    for l in range(DEPTH):
        mod = c_act @ w_ada[l] + b_ada[l]
        shift1, scale1, gate1, shift2, scale2, gate2 = jnp.split(mod, N_MOD, axis=-1)
        h = _modulate(_rmsnorm(x, norm_mix_g[l]), shift1, scale1)
        x = x + gate1[:, None, :] * _mixer(h, w_in[l], w_ret_out[l], w_sb_out[l], w_mix_out[l])
        h2 = _modulate(_rmsnorm(x, norm_mlp_g[l]), shift2, scale2)
        u = jnp.square(jax.nn.relu(h2 @ w_up[l]))
        x = x + gate2[:, None, :] * (u @ w_down[l])
    return _rmsnorm(x, final_g)


import jax as _jax
import jax.numpy as _jnp

TWIN_FORMAT = 'train_step'
FWD_PARAMS = ['x', 'c', 'norm_mix_g', 'w_in', 'w_ret_out', 'w_sb_out', 'w_mix_out', 'norm_mlp_g', 'w_up', 'w_down', 'w_ada', 'b_ada', 'final_g']
TWIN_WEIGHTS = ['norm_mix_g', 'w_in', 'w_ret_out', 'w_sb_out', 'w_mix_out', 'norm_mlp_g', 'w_up', 'w_down', 'w_ada', 'b_ada', 'final_g']
TWIN_DIFF_INPUT = 'x'
TWIN_INPUTS = ['x', 'c', 'norm_mix_g', 'w_in', 'w_ret_out', 'w_sb_out', 'w_mix_out', 'norm_mlp_g', 'w_up', 'w_down', 'w_ada', 'b_ada', 'final_g', 'loss_target', 'm_norm_mix_g', 'm_w_in', 'm_w_ret_out', 'm_w_sb_out', 'm_w_mix_out', 'm_norm_mlp_g', 'm_w_up', 'm_w_down', 'm_w_ada', 'm_b_ada', 'm_final_g', 'v_norm_mix_g', 'v_w_in', 'v_w_ret_out', 'v_w_sb_out', 'v_w_mix_out', 'v_norm_mlp_g', 'v_w_up', 'v_w_down', 'v_w_ada', 'v_b_ada', 'v_final_g']
TWIN_OUTPUTS = ['loss', 'grad_x', 'grad_norm_mix_g', 'grad_w_in', 'grad_w_ret_out', 'grad_w_sb_out', 'grad_w_mix_out', 'grad_norm_mlp_g', 'grad_w_up', 'grad_w_down', 'grad_w_ada', 'grad_b_ada', 'grad_final_g', 'delta_norm_mix_g', 'delta_w_in', 'delta_w_ret_out', 'delta_w_sb_out', 'delta_w_mix_out', 'delta_norm_mlp_g', 'delta_w_up', 'delta_w_down', 'delta_w_ada', 'delta_b_ada', 'delta_final_g', 'new_m_norm_mix_g', 'new_m_w_in', 'new_m_w_ret_out', 'new_m_w_sb_out', 'new_m_w_mix_out', 'new_m_norm_mlp_g', 'new_m_w_up', 'new_m_w_down', 'new_m_w_ada', 'new_m_b_ada', 'new_m_final_g', 'new_v_norm_mix_g', 'new_v_w_in', 'new_v_w_ret_out', 'new_v_w_sb_out', 'new_v_w_mix_out', 'new_v_norm_mlp_g', 'new_v_w_up', 'new_v_w_down', 'new_v_w_ada', 'new_v_b_ada', 'new_v_final_g']
TWIN_LEAF_KINDS = {'loss': 'loss', 'grad_x': 'grad_x', 'grad_norm_mix_g': 'grad_w', 'grad_w_in': 'grad_w', 'grad_w_ret_out': 'grad_w', 'grad_w_sb_out': 'grad_w', 'grad_w_mix_out': 'grad_w', 'grad_norm_mlp_g': 'grad_w', 'grad_w_up': 'grad_w', 'grad_w_down': 'grad_w', 'grad_w_ada': 'grad_w', 'grad_b_ada': 'grad_w', 'grad_final_g': 'grad_w', 'delta_norm_mix_g': 'delta_w', 'delta_w_in': 'delta_w', 'delta_w_ret_out': 'delta_w', 'delta_w_sb_out': 'delta_w', 'delta_w_mix_out': 'delta_w', 'delta_norm_mlp_g': 'delta_w', 'delta_w_up': 'delta_w', 'delta_w_down': 'delta_w', 'delta_w_ada': 'delta_w', 'delta_b_ada': 'delta_w', 'delta_final_g': 'delta_w', 'new_m_norm_mix_g': 'new_m', 'new_m_w_in': 'new_m', 'new_m_w_ret_out': 'new_m', 'new_m_w_sb_out': 'new_m', 'new_m_w_mix_out': 'new_m', 'new_m_norm_mlp_g': 'new_m', 'new_m_w_up': 'new_m', 'new_m_w_down': 'new_m', 'new_m_w_ada': 'new_m', 'new_m_b_ada': 'new_m', 'new_m_final_g': 'new_m', 'new_v_norm_mix_g': 'new_v', 'new_v_w_in': 'new_v', 'new_v_w_ret_out': 'new_v', 'new_v_w_sb_out': 'new_v', 'new_v_w_mix_out': 'new_v', 'new_v_norm_mlp_g': 'new_v', 'new_v_w_up': 'new_v', 'new_v_w_down': 'new_v', 'new_v_w_ada': 'new_v', 'new_v_b_ada': 'new_v', 'new_v_final_g': 'new_v'}


def _forward(args):
    return _fwd_reference(*[args[k] for k in FWD_PARAMS])


def _output_shape():
    def fwd():
        inp = _fwd_setup_inputs(0)
        return _fwd_reference(*[inp[k] for k in FWD_PARAMS])
    out = _jax.eval_shape(fwd)
    return out.shape, out.dtype

N_MICROBATCH = 1
ADAM_LR = 0.001
ADAM_B1 = 0.9
ADAM_B2 = 0.999
ADAM_EPS = 1e-08
ADAM_WD = 0.01
ADAM_STEP = 10
PER_EXAMPLE_BATCH_AXIS = {'x': 0, 'c': 0, 'loss_target': 0}
SHARED_INPUTS = []
_WEIGHT_DTYPES = {'norm_mix_g': _jnp.float32, 'w_in': _jnp.float32, 'w_ret_out': _jnp.float32, 'w_sb_out': _jnp.float32, 'w_mix_out': _jnp.float32, 'norm_mlp_g': _jnp.float32, 'w_up': _jnp.float32, 'w_down': _jnp.float32, 'w_ada': _jnp.float32, 'b_ada': _jnp.float32, 'final_g': _jnp.float32}
MOMENT_SCALE = {'norm_mix_g': 9.933966e-02, 'w_in': 3.717541e-02, 'w_ret_out': 3.975362e-02, 'w_sb_out': 5.372902e-02, 'w_mix_out': 6.709700e-02, 'norm_mlp_g': 1.297442e-01, 'w_up': 7.834220e-02, 'w_down': 1.609566e-01, 'w_ada': 9.949054e-02, 'b_ada': 1.817921e-01, 'final_g': 3.362920e+01}


def _to_microbatches(a, axis):
    t = _jnp.moveaxis(a, axis, 0)
    t = t.reshape((N_MICROBATCH, t.shape[0] // N_MICROBATCH) + t.shape[1:])
    return _jnp.moveaxis(t, 1, axis + 1)


def setup_inputs(seed: int = 0) -> dict:
    inp = _fwd_setup_inputs(seed)
    key = _jax.random.fold_in(_jax.random.key(seed), 7919)
    shape, _ = _output_shape()
    out = dict(inp)
    out["loss_target"] = _jax.random.normal(_jax.random.fold_in(key, 0), shape, _jnp.float32)
    for i, name in enumerate(TWIN_WEIGHTS):
        w = inp[name].astype(_jnp.float32)
        if MOMENT_SCALE is None:
            s = _jnp.sqrt(_jnp.mean(_jnp.square(w)) + 1e-30)
        else:
            s = MOMENT_SCALE[name]
        km, kv = _jax.random.split(_jax.random.fold_in(key, i + 1))
        out[name] = w
        out["m_" + name] = s * _jax.random.normal(km, w.shape, _jnp.float32)
        out["v_" + name] = (s * s) * _jax.random.uniform(kv, w.shape, _jnp.float32, 0.5, 1.5)
    if N_MICROBATCH > 1:
        for name, axis in PER_EXAMPLE_BATCH_AXIS.items():
            out[name] = _to_microbatches(out[name], axis)
    return {'x': out['x'], 'c': out['c'], 'norm_mix_g': out['norm_mix_g'], 'w_in': out['w_in'], 'w_ret_out': out['w_ret_out'], 'w_sb_out': out['w_sb_out'], 'w_mix_out': out['w_mix_out'], 'norm_mlp_g': out['norm_mlp_g'], 'w_up': out['w_up'], 'w_down': out['w_down'], 'w_ada': out['w_ada'], 'b_ada': out['b_ada'], 'final_g': out['final_g'], 'loss_target': out['loss_target'], 'm_norm_mix_g': out['m_norm_mix_g'], 'm_w_in': out['m_w_in'], 'm_w_ret_out': out['m_w_ret_out'], 'm_w_sb_out': out['m_w_sb_out'], 'm_w_mix_out': out['m_w_mix_out'], 'm_norm_mlp_g': out['m_norm_mlp_g'], 'm_w_up': out['m_w_up'], 'm_w_down': out['m_w_down'], 'm_w_ada': out['m_w_ada'], 'm_b_ada': out['m_b_ada'], 'm_final_g': out['m_final_g'], 'v_norm_mix_g': out['v_norm_mix_g'], 'v_w_in': out['v_w_in'], 'v_w_ret_out': out['v_w_ret_out'], 'v_w_sb_out': out['v_w_sb_out'], 'v_w_mix_out': out['v_w_mix_out'], 'v_norm_mlp_g': out['v_norm_mlp_g'], 'v_w_up': out['v_w_up'], 'v_w_down': out['v_w_down'], 'v_w_ada': out['v_w_ada'], 'v_b_ada': out['v_b_ada'], 'v_final_g': out['v_final_g']}


def _loss(weights, diff, rest, loss_target):
    with _jax.named_scope("forward"):
        args = {**rest, TWIN_DIFF_INPUT: diff, **{k: w.astype(_WEIGHT_DTYPES[k]) for k, w in weights.items()}}
        y = _forward(args)
    with _jax.named_scope("loss_head"):
        err = _jnp.square(y.astype(_jnp.float32) - loss_target)
        return 0.5 * _jnp.sum(_jnp.mean(err, axis=-1)) if err.ndim else 0.5 * err


def _adamw(w, g, m, v):
    m = ADAM_B1 * m + (1.0 - ADAM_B1) * g
    v = ADAM_B2 * v + (1.0 - ADAM_B2) * _jnp.square(g)
    m_hat = m / (1.0 - ADAM_B1 ** ADAM_STEP)
    v_hat = v / (1.0 - ADAM_B2 ** ADAM_STEP)
    delta = -ADAM_LR * (m_hat / (_jnp.sqrt(v_hat) + ADAM_EPS) + ADAM_WD * w)
    return delta, m, v


def reference(x, c, norm_mix_g, w_in, w_ret_out, w_sb_out, w_mix_out, norm_mlp_g, w_up, w_down, w_ada, b_ada, final_g, loss_target, m_norm_mix_g, m_w_in, m_w_ret_out, m_w_sb_out, m_w_mix_out, m_norm_mlp_g, m_w_up, m_w_down, m_w_ada, m_b_ada, m_final_g, v_norm_mix_g, v_w_in, v_w_ret_out, v_w_sb_out, v_w_mix_out, v_norm_mlp_g, v_w_up, v_w_down, v_w_ada, v_b_ada, v_final_g):
    given = dict(x=x, c=c, norm_mix_g=norm_mix_g, w_in=w_in, w_ret_out=w_ret_out, w_sb_out=w_sb_out, w_mix_out=w_mix_out, norm_mlp_g=norm_mlp_g, w_up=w_up, w_down=w_down, w_ada=w_ada, b_ada=b_ada, final_g=final_g, loss_target=loss_target, m_norm_mix_g=m_norm_mix_g, m_w_in=m_w_in, m_w_ret_out=m_w_ret_out, m_w_sb_out=m_w_sb_out, m_w_mix_out=m_w_mix_out, m_norm_mlp_g=m_norm_mlp_g, m_w_up=m_w_up, m_w_down=m_w_down, m_w_ada=m_w_ada, m_b_ada=m_b_ada, m_final_g=m_final_g, v_norm_mix_g=v_norm_mix_g, v_w_in=v_w_in, v_w_ret_out=v_w_ret_out, v_w_sb_out=v_w_sb_out, v_w_mix_out=v_w_mix_out, v_norm_mlp_g=v_norm_mlp_g, v_w_up=v_w_up, v_w_down=v_w_down, v_w_ada=v_w_ada, v_b_ada=v_b_ada, v_final_g=v_final_g)
    weights = {n: given[n] for n in TWIN_WEIGHTS}
    shared = {n: given[n] for n in SHARED_INPUTS}
    per_example = {n: given[n] for n in ['x', 'c']}
    grad_fn = _jax.value_and_grad(_loss, argnums=(0, 1))

    def one_microbatch(ex, loss_target):
        ex = dict(ex)
        diff = ex.pop(TWIN_DIFF_INPUT)
        return grad_fn(weights, diff, {**shared, **ex}, loss_target)

    if N_MICROBATCH == 1:
        loss, (grad_w, grad_x) = one_microbatch(per_example, given["loss_target"])
    else:
        def body(carry, xs):
            loss_sum, grad_sum = carry
            l_k, (gw_k, gx_k) = one_microbatch(xs[0], xs[1])
            with _jax.named_scope("update"):
                return (loss_sum + l_k, _jax.tree.map(_jnp.add, grad_sum, gw_k)), gx_k

        init = (_jnp.zeros((), _jnp.float32), _jax.tree.map(_jnp.zeros_like, weights))
        (loss, grad_w), grad_x = _jax.lax.scan(body, init, (per_example, given["loss_target"]))
    with _jax.named_scope("update"):
        delta_w, new_m, new_v = {}, {}, {}
        for n in TWIN_WEIGHTS:
            delta_w[n], new_m[n], new_v[n] = _adamw(weights[n], grad_w[n], given["m_" + n], given["v_" + n])
    return (loss, grad_x, *[grad_w[n] for n in TWIN_WEIGHTS], *[delta_w[n] for n in TWIN_WEIGHTS],
            *[new_m[n] for n in TWIN_WEIGHTS], *[new_v[n] for n in TWIN_WEIGHTS])
```

```python
import functools
import math

import jax
import jax.numpy as jnp
import numpy as np
from jax import lax
from jax.experimental import pallas as pl
from jax.experimental.pallas import tpu as pltpu

F32 = jnp.float32
BF16 = jnp.bfloat16

N_DEV = 8
D_MODEL = 1024
DEPTH = 2
RET_HEADS = 4
RET_QK = 256
RET_V = 512
RET_CHUNK = 128
ROPE_BASE = 10000.0
SB_HEADS = 16
SB_DIM = 64
D_FF = 4096
N_MOD = 6
EPS = 1e-6
GN_EPS = 1e-5
O_RQ, O_RK, O_RV, O_RG, O_SQ, O_SK, O_SV, O_GA, O_GB = 0, 1024, 2048, 4096, 6144, 7168, 8192, 9216, 10240
IN_W = 11264

ADAM_LR, ADAM_B1, ADAM_B2, ADAM_EPS, ADAM_WD, ADAM_STEP = 0.001, 0.9, 0.999, 1e-08, 0.01, 10

VMEM_LIMIT = 56 * 1024 * 1024


def _cparams(sem):
    return pltpu.CompilerParams(dimension_semantics=sem, vmem_limit_bytes=VMEM_LIMIT)


def _mm(a, b, *, ta=False, tb=False, tm=512, tn=512, tk=512, a_ex=(), pro=None, o_ex=(), epi=None,
        outs=(F32,), name):
    if ta:
        K, M = a.shape
    else:
        M, K = a.shape
    N = b.shape[0] if tb else b.shape[1]
    tm, tn, tk = min(tm, M), min(tn, N), min(tk, K)
    assert M % tm == 0 and N % tn == 0 and K % tk == 0, (name, M, N, K, tm, tn, tk)
    nk = K // tk
    in_specs = [
        pl.BlockSpec((tk, tm), lambda i, j, k: (k, i)) if ta else pl.BlockSpec((tm, tk), lambda i, j, k: (i, k)),
        pl.BlockSpec((tn, tk), lambda i, j, k: (j, k)) if tb else pl.BlockSpec((tk, tn), lambda i, j, k: (k, j)),
    ]
    args = [a, b]
    for arr, kind in a_ex:
        if kind == 'a':
            in_specs.append(in_specs[0])
        elif kind == 'k':
            in_specs.append(pl.BlockSpec((tk, 1), lambda i, j, k: (k, 0)) if ta
                            else pl.BlockSpec((1, tk), lambda i, j, k: (0, k)))
        else:
            in_specs.append(pl.BlockSpec((1, tm), lambda i, j, k: (0, i)) if ta
                            else pl.BlockSpec((tm, 1), lambda i, j, k: (i, 0)))
        args.append(arr)
    for arr, kind in o_ex:
        if kind == 'o':
            in_specs.append(pl.BlockSpec((tm, tn), lambda i, j, k: (i, j)))
        elif kind == 'n':
            in_specs.append(pl.BlockSpec((1, tn), lambda i, j, k: (0, j)))
        else:
            in_specs.append(pl.BlockSpec((tm, 1), lambda i, j, k: (i, 0)))
        args.append(arr)
    na, no, nout = len(a_ex), len(o_ex), len(outs)
    dims = (((0 if ta else 1,), (1 if tb else 0,)), ((), ()))

    def body(*refs):
        a_ref, b_ref = refs[0], refs[1]
        aex = refs[2:2 + na]
        oex = refs[2 + na:2 + na + no]
        out_refs = refs[2 + na + no:2 + na + no + nout]
        acc = refs[-1]
        k = pl.program_id(2)

        @pl.when(k == 0)
        def _():
            acc[...] = jnp.zeros_like(acc)

        at = a_ref[...]
        if pro is not None:
            at = pro(at, *[r[...] for r in aex])
        acc[...] += lax.dot_general(at.astype(BF16), b_ref[...].astype(BF16), dims, preferred_element_type=F32)

        @pl.when(k == nk - 1)
        def _():
            res = acc[...]
            vals = epi(res, *[r[...] for r in oex]) if epi is not None else (res,)
            for o_ref, v in zip(out_refs, vals):
                o_ref[...] = v.astype(o_ref.dtype)

    res = pl.pallas_call(
        body, name=name, grid=(M // tm, N // tn, nk), in_specs=in_specs,
        out_specs=[pl.BlockSpec((tm, tn), lambda i, j, k: (i, j)) for _ in outs],
        out_shape=[jax.ShapeDtypeStruct((M, N), dt) for dt in outs],
        scratch_shapes=[pltpu.VMEM((tm, tn), F32)],
        compiler_params=_cparams(("parallel", "parallel", "arbitrary")),
    )(*args)
    return res


def _ew(fn, rows, vecs=(), cols=(), outs=(), tr=256, name=None):
    S = rows[0].shape[0]
    tr = min(tr, S)
    assert S % tr == 0
    in_specs, args = [], []
    for r in rows:
        in_specs.append(pl.BlockSpec((tr, r.shape[1]), lambda i: (i, 0)))
        args.append(r)
    for v in vecs:
        in_specs.append(pl.BlockSpec((1, v.shape[1]), lambda i: (0, 0)))
        args.append(v)
    for c in cols:
        in_specs.append(pl.BlockSpec((tr, 1), lambda i: (i, 0)))
        args.append(c)
    out_specs, out_shape = [], []
    for o in outs:
        if o[0] == 'row':
            out_specs.append(pl.BlockSpec((tr, o[1]), lambda i: (i, 0)))
            out_shape.append(jax.ShapeDtypeStruct((S, o[1]), o[2]))
        elif o[0] == 'sum':
            out_specs.append(pl.BlockSpec((1, o[1]), lambda i: (0, 0)))
            out_shape.append(jax.ShapeDtypeStruct((1, o[1]), F32))
        else:
            out_specs.append(pl.BlockSpec((tr, 1), lambda i: (i, 0)))
            out_shape.append(jax.ShapeDtypeStruct((S, 1), o[1]))
    nin = len(args)

    def body(*refs):
        i = pl.program_id(0)
        vals = fn(*[r[...] for r in refs[:nin]])
        for o, o_ref, v in zip(outs, refs[nin:], vals):
            if o[0] == 'sum':
                @pl.when(i == 0)
                def _():
                    o_ref[...] = jnp.zeros_like(o_ref)
                o_ref[...] += jnp.sum(v.astype(F32), axis=0, keepdims=True)
            else:
                o_ref[...] = v.astype(o_ref.dtype)

    return pl.pallas_call(
        body, name=name, grid=(S // tr,), in_specs=in_specs, out_specs=out_specs, out_shape=out_shape,
        compiler_params=_cparams(("arbitrary",)),
    )(*args)


def _sigmoid(x):
    return 1.0 / (1.0 + jnp.exp(-x))


def _ret_consts(S):
    h = np.arange(RET_HEADS, dtype=np.float64)
    log_gamma = np.log1p(-np.power(2.0, -5.0 - h))
    idx = np.arange(RET_CHUNK, dtype=np.float64)
    rel = idx[:, None] - idx[None, :]
    decay = np.where(rel >= 0, np.exp(np.maximum(rel, 0.0) * log_gamma[:, None, None]), 0.0)
    xi = np.exp((idx + 1.0) * log_gamma[:, None])[:, :, None]
    zeta = np.exp((RET_CHUNK - 1.0 - idx) * log_gamma[:, None])[:, :, None]
    gamma_c = np.exp(RET_CHUNK * log_gamma)[:, None, None]
    half = RET_QK // 2
    inv_freq = np.power(ROPE_BASE, -np.arange(half, dtype=np.float64) / half).astype(np.float32)
    ang = np.arange(S, dtype=np.float32)[:, None] * inv_freq[None, :]
    f = lambda t: jnp.asarray(t, F32)
    return dict(decay=f(decay), xi=f(xi), zeta=f(zeta), gc=f(gamma_c), cos=f(np.cos(ang)), sin=f(np.sin(ang)))


def _rot(t, cos, sin):
    half = RET_QK // 2
    t1, t2 = t[:, :half], t[:, half:]
    return jnp.concatenate([t1 * cos - t2 * sin, t1 * sin + t2 * cos], axis=-1)


def _rot_inv(t, cos, sin):
    half = RET_QK // 2
    t1, t2 = t[:, :half], t[:, half:]
    return jnp.concatenate([t1 * cos + t2 * sin, t2 * cos - t1 * sin], axis=-1)


_NT = (((1,), (1,)), ((), ()))
_TN = (((0,), (0,)), ((), ()))


def _dot(a, b):
    return jnp.dot(a, b, preferred_element_type=F32)


def _dot_nt(a, b):
    return lax.dot_general(a, b, _NT, preferred_element_type=F32)


def _dot_tn(a, b):
    return lax.dot_general(a, b, _TN, preferred_element_type=F32)


def _ret_in_specs(C, rev, NC):
    n_of = (lambda n: NC - 1 - n) if rev else (lambda n: n)
    qb, vb = O_RQ // RET_QK, O_RV // RET_V
    kb = O_RK // RET_QK
    return [
        pl.BlockSpec((C, RET_QK), lambda h, n: (n_of(n), qb + h)),
        pl.BlockSpec((C, RET_QK), lambda h, n: (n_of(n), kb + h)),
        pl.BlockSpec((C, RET_V), lambda h, n: (n_of(n), vb + h)),
        pl.BlockSpec((C, RET_QK // 2), lambda h, n: (n_of(n), 0)),
        pl.BlockSpec((C, RET_QK // 2), lambda h, n: (n_of(n), 0)),
        pl.BlockSpec((1, C, C), lambda h, n: (h, 0, 0)),
        pl.BlockSpec((1, C, 1), lambda h, n: (h, 0, 0)),
        pl.BlockSpec((1, C, 1), lambda h, n: (h, 0, 0)),
        pl.BlockSpec((1, 1, 1), lambda h, n: (h, 0, 0)),
    ]


def _ret_fwd(p, rc):
    S = p.shape[0]
    C = RET_CHUNK
    NC = S // C

    def body(q_ref, k_ref, v_ref, cos_ref, sin_ref, dec_ref, xi_ref, zeta_ref, gc_ref, y_ref, rs_ref, r_acc):
        n = pl.program_id(1)

        @pl.when(n == 0)
        def _():
            r_acc[...] = jnp.zeros_like(r_acc)

        cos, sin = cos_ref[...], sin_ref[...]
        q = _rot(q_ref[...].astype(F32), cos, sin).astype(BF16)
        kf = _rot(k_ref[...].astype(F32), cos, sin) * (RET_QK ** -0.5)
        k = kf.astype(BF16)
        v = v_ref[...]
        r = r_acc[...]
        rb = r.astype(BF16)
        rs_ref[0, 0] = rb
        s = (_dot_nt(q, k) * dec_ref[0]).astype(BF16)
        o = _dot(s, v) + _dot(q, rb) * xi_ref[0]
        mu = jnp.mean(o, axis=-1, keepdims=True)
        var = jnp.mean(jnp.square(o - mu), axis=-1, keepdims=True)
        y_ref[...] = ((o - mu) * lax.rsqrt(var + GN_EPS)).astype(y_ref.dtype)
        kz = (kf * zeta_ref[0]).astype(BF16)
        r_acc[...] = r * gc_ref[0] + _dot_tn(kz, v)

    return pl.pallas_call(
        body, name="ret_fwd", grid=(RET_HEADS, NC), in_specs=_ret_in_specs(C, False, NC),
        out_specs=[pl.BlockSpec((C, RET_V), lambda h, n: (n, h)),
                   pl.BlockSpec((1, 1, RET_QK, RET_V), lambda h, n: (h, n, 0, 0))],
        out_shape=[jax.ShapeDtypeStruct((S, RET_HEADS * RET_V), BF16),
                   jax.ShapeDtypeStruct((RET_HEADS, NC, RET_QK, RET_V), BF16)],
        scratch_shapes=[pltpu.VMEM((RET_QK, RET_V), F32)],
        compiler_params=_cparams(("parallel", "arbitrary")),
    )(p, p, p, rc['cos'], rc['sin'], rc['decay'], rc['xi'], rc['zeta'], rc['gc'])


def _ret_bwd(p, rstate, dy, rc):
    S = p.shape[0]
    C = RET_CHUNK
    NC = S // C

    def body(q_ref, k_ref, v_ref, cos_ref, sin_ref, dec_ref, xi_ref, zeta_ref, gc_ref, rs_ref, dy_ref,
             dq_ref, dk_ref, dv_ref, dr_acc):
        t = pl.program_id(1)

        @pl.when(t == 0)
        def _():
            dr_acc[...] = jnp.zeros_like(dr_acc)

        cos, sin = cos_ref[...], sin_ref[...]
        dec, xi, zeta = dec_ref[0], xi_ref[0], zeta_ref[0]
        q = _rot(q_ref[...].astype(F32), cos, sin).astype(BF16)
        kf = _rot(k_ref[...].astype(F32), cos, sin) * (RET_QK ** -0.5)
        k = kf.astype(BF16)
        kz = (kf * zeta).astype(BF16)
        v = v_ref[...]
        rb = rs_ref[0, 0]
        s = (_dot_nt(q, k) * dec).astype(BF16)
        o = _dot(s, v) + _dot(q, rb) * xi
        mu = jnp.mean(o, axis=-1, keepdims=True)
        var = jnp.mean(jnp.square(o - mu), axis=-1, keepdims=True)
        rstd = lax.rsqrt(var + GN_EPS)
        yh = (o - mu) * rstd
        dyf = dy_ref[...].astype(F32)
        do = (dyf - jnp.mean(dyf, axis=-1, keepdims=True) - yh * jnp.mean(dyf * yh, axis=-1, keepdims=True)) * rstd
        dob = do.astype(BF16)
        doxi = (do * xi).astype(BF16)
        dr = dr_acc[...]
        drb = dr.astype(BF16)
        ds = (_dot_nt(dob, v) * dec).astype(BF16)
        dq = _dot(ds, k) + _dot_nt(doxi, rb)
        dk = _dot_tn(ds, q) + _dot_nt(v, drb) * zeta
        dv = _dot_tn(s, dob) + _dot(kz, drb)
        dr_acc[...] = dr * gc_ref[0] + _dot_tn(q, doxi)
        dq_ref[...] = _rot_inv(dq, cos, sin).astype(dq_ref.dtype)
        dk_ref[...] = (_rot_inv(dk, cos, sin) * (RET_QK ** -0.5)).astype(dk_ref.dtype)
        dv_ref[...] = dv.astype(dv_ref.dtype)

    rn = lambda n: NC - 1 - n
    in_specs = _ret_in_specs(C, True, NC) + [
        pl.BlockSpec((1, 1, RET_QK, RET_V), lambda h, n: (h, rn(n), 0, 0)),
        pl.BlockSpec((C, RET_V), lambda h, n: (rn(n), h)),
    ]
    return pl.pallas_call(
        body, name="ret_bwd", grid=(RET_HEADS, NC), in_specs=in_specs,
        out_specs=[pl.BlockSpec((C, RET_QK), lambda h, n: (rn(n), h)),
                   pl.BlockSpec((C, RET_QK), lambda h, n: (rn(n), h)),
                   pl.BlockSpec((C, RET_V), lambda h, n: (rn(n), h))],
        out_shape=[jax.ShapeDtypeStruct((S, RET_HEADS * RET_QK), BF16),
                   jax.ShapeDtypeStruct((S, RET_HEADS * RET_QK), BF16),
                   jax.ShapeDtypeStruct((S, RET_HEADS * RET_V), BF16)],
        scratch_shapes=[pltpu.VMEM((RET_QK, RET_V), F32)],
        compiler_params=_cparams(("parallel", "arbitrary")),
    )(p, p, p, rc['cos'], rc['sin'], rc['decay'], rc['xi'], rc['zeta'], rc['gc'], rstate, dy)


SB_T = 256
SB_SCALE = SB_DIM ** -0.5


def _tri():
    j = np.arange(SB_T)
    return jnp.asarray((j[:, None] >= j[None, :]).astype(np.float32), BF16)


def _rev_cumsum(x, tri):
    hi = x.astype(BF16)
    lo = (x - hi.astype(F32)).astype(BF16)
    return _dot(hi, tri) + _dot(lo, tri)


def _sb_fwd(p, tri):
    S = p.shape[0]
    T = min(SB_T, S)
    NQ = S // T
    assert NQ <= 128
    qb, kb, vb = O_SQ // 128, O_SK // 128, O_SV // 128

    def body(q_ref, k_ref, v_ref, tri_ref, o_ref, cs_ref, o_acc, run):
        i = pl.program_id(1)
        lane = lax.broadcasted_iota(jnp.int32, (1, 128), 1)
        rowi = lax.broadcasted_iota(jnp.int32, (T, T), 0)
        coli = lax.broadcasted_iota(jnp.int32, (T, T), 1)
        tri_m = tri_ref[...]
        cs_ref[...] = jnp.zeros_like(cs_ref)
        for hh in range(2):
            hm = (lane >= 64) if hh else (lane < 64)
            q = jnp.where(hm, q_ref[...], jnp.zeros_like(q_ref[...]))
            csl = slice(hh * 128, (hh + 1) * 128)
            o_acc[hh] = jnp.zeros((T, 128), F32)
            run[...] = jnp.zeros_like(run)

            def step(t, carry):
                j = i - t
                ks = pl.multiple_of(j * T, T)
                kblk = k_ref[pl.ds(ks, T), :]
                vblk = v_ref[pl.ds(ks, T), :]
                z = _dot_nt(q, kblk) * SB_SCALE
                msk = coli < rowi + t * T
                sp = jnp.where(msk, jnp.maximum(z, 0.0) + jnp.log(1.0 + jnp.exp(-jnp.abs(z))), 0.0)
                cl = _rev_cumsum(sp, tri_m)
                cs = run[...]
                a = jnp.where(msk, jnp.exp(z - cl - cs), 0.0)
                o_acc[hh] += _dot(a.astype(BF16), vblk)
                cs_ref[:, csl] = jnp.where(lane == j, cs, cs_ref[:, csl])
                run[...] = cs + cl[:, 0:1]
                return carry

            lax.fori_loop(0, i + 1, step, 0)
        o_ref[...] = jnp.where(lane < 64, o_acc[0], o_acc[1]).astype(o_ref.dtype)

    return pl.pallas_call(
        body, name="sb_fwd", grid=(SB_HEADS // 2, NQ),
        scratch_shapes=[pltpu.VMEM((2, T, 128), F32), pltpu.VMEM((T, 1), F32)],
        in_specs=[pl.BlockSpec((T, 128), lambda h, i: (i, qb + h)),
                  pl.BlockSpec((S, 128), lambda h, i: (0, kb + h)),
                  pl.BlockSpec((S, 128), lambda h, i: (0, vb + h)),
                  pl.BlockSpec((T, T), lambda h, i: (0, 0))],
        out_specs=[pl.BlockSpec((T, 128), lambda h, i: (i, h)),
                   pl.BlockSpec((T, 256), lambda h, i: (i, h))],
        out_shape=[jax.ShapeDtypeStruct((S, SB_HEADS * SB_DIM), BF16),
                   jax.ShapeDtypeStruct((S, SB_HEADS * 128), F32)],
        compiler_params=_cparams(("parallel", "arbitrary")),
    )(p, p, p, tri)


def _sb_bwd(p, carries, dy, tri):
    S = p.shape[0]
    T = min(SB_T, S)
    NQ = S // T
    qb, kb, vb = O_SQ // 128, O_SK // 128, O_SV // 128

    def body(q_ref, k_ref, v_ref, cs_ref, dy_ref, tri_ref, dq_ref, dk_ref, dv_ref, dk_acc, dv_acc, dq_acc, run):
        i = pl.program_id(1)

        @pl.when(i == 0)
        def _():
            dk_acc[...] = jnp.zeros_like(dk_acc)
            dv_acc[...] = jnp.zeros_like(dv_acc)

        lane = lax.broadcasted_iota(jnp.int32, (1, 128), 1)
        rowi = lax.broadcasted_iota(jnp.int32, (T, T), 0)
        coli = lax.broadcasted_iota(jnp.int32, (T, T), 1)
        tri_m = tri_ref[...]
        for hh in range(2):
            hm = (lane >= 64) if hh else (lane < 64)
            q = jnp.where(hm, q_ref[...], jnp.zeros_like(q_ref[...]))
            do = jnp.where(hm, dy_ref[...], jnp.zeros_like(dy_ref[...]))
            csl = slice(hh * 128, (hh + 1) * 128)
            dq_acc[hh] = jnp.zeros((T, 128), F32)
            run[...] = jnp.zeros_like(run)

            def step(j, carry):
                cg = run[...]
                ks = pl.multiple_of(j * T, T)
                kblk = k_ref[pl.ds(ks, T), :]
                vblk = v_ref[pl.ds(ks, T), :]
                z = _dot_nt(q, kblk) * SB_SCALE
                msk = coli < rowi + (i - j) * T
                e = jnp.exp(-jnp.abs(z))
                sp = jnp.where(msk, jnp.maximum(z, 0.0) + jnp.log(1.0 + e), 0.0)
                sig = jnp.where(z >= 0, 1.0, e) / (1.0 + e)
                cs = jnp.sum(jnp.where(lane == j, cs_ref[:, csl], 0.0), axis=-1, keepdims=True)
                cl = _rev_cumsum(sp, tri_m)
                a = jnp.where(msk, jnp.exp(z - cl - cs), 0.0)
                g = a * _dot_nt(do, vblk)
                ghi = g.astype(BF16)
                glo = (g - ghi.astype(F32)).astype(BF16)
                pg = _dot_nt(ghi, tri_m) + _dot_nt(glo, tri_m)
                dz = jnp.where(msk, g - sig * (cg + pg), 0.0)
                dzb = (dz * SB_SCALE).astype(BF16)
                dq_acc[hh] += _dot(dzb, kblk)
                dk_acc[pl.ds(ks, T), :] += _dot_tn(dzb, q)
                dv_acc[pl.ds(ks, T), :] += _dot_tn(a.astype(BF16), do)
                run[...] = cg + pg[:, T - 1:T]
                return carry

            lax.fori_loop(0, i + 1, step, 0)
        dq_ref[...] = jnp.where(lane < 64, dq_acc[0], dq_acc[1]).astype(dq_ref.dtype)

        @pl.when(i == NQ - 1)
        def _():
            dk_ref[...] = dk_acc[...].astype(dk_ref.dtype)
            dv_ref[...] = dv_acc[...].astype(dv_ref.dtype)

    W = SB_HEADS * SB_DIM
    return pl.pallas_call(
        body, name="sb_bwd", grid=(SB_HEADS // 2, NQ),
        in_specs=[pl.BlockSpec((T, 128), lambda h, i: (i, qb + h)),
                  pl.BlockSpec((S, 128), lambda h, i: (0, kb + h)),
                  pl.BlockSpec((S, 128), lambda h, i: (0, vb + h)),
                  pl.BlockSpec((T, 256), lambda h, i: (i, h)),
                  pl.BlockSpec((T, 128), lambda h, i: (i, h)),
                  pl.BlockSpec((T, T), lambda h, i: (0, 0))],
        out_specs=[pl.BlockSpec((T, 128), lambda h, i: (i, h)),
                   pl.BlockSpec((S, 128), lambda h, i: (0, h)),
                   pl.BlockSpec((S, 128), lambda h, i: (0, h))],
        out_shape=[jax.ShapeDtypeStruct((S, W), BF16)] * 3,
        scratch_shapes=[pltpu.VMEM((S, 128), F32), pltpu.VMEM((S, 128), F32), pltpu.VMEM((2, T, 128), F32),
                        pltpu.VMEM((T, 1), F32)],
        compiler_params=_cparams(("parallel", "arbitrary")),
    )(p, p, p, carries, dy, tri)


def _exchange(srcs, out_shapes, src_slice, dst_slice, name):
    n = len(srcs)

    def body(*refs):
        ins, outs = refs[:n], refs[n:2 * n]
        send_sems, recv_sems, loc_sems = refs[2 * n:]
        x, y, c = lax.axis_index("x"), lax.axis_index("y"), lax.axis_index("c")
        me = 4 * x + 2 * y + c
        local = [pltpu.make_async_copy(src_slice(t, ins[t], me), dst_slice(t, outs[t], me), loc_sems.at[t])
                 for t in range(n)]
        for cp in local:
            cp.start()
        sends, recvs = [], []
        for k in (1, 2, 4, 6, 3, 5, 7):
            px = 1 - x if k & 4 else x
            py = 1 - y if k & 2 else y
            pc = 1 - c if k & 1 else c
            peer = 4 * px + 2 * py + pc
            for t in range(n):
                s = t * 7 + k - 1
                sends.append(pltpu.make_async_remote_copy(
                    src_ref=src_slice(t, ins[t], peer), dst_ref=dst_slice(t, outs[t], me),
                    send_sem=send_sems.at[s], recv_sem=recv_sems.at[s],
                    device_id=(px, py, pc), device_id_type=pl.DeviceIdType.MESH))
                recvs.append(pltpu.make_async_remote_copy(
                    src_ref=src_slice(t, ins[t], me), dst_ref=dst_slice(t, outs[t], peer),
                    send_sem=send_sems.at[s], recv_sem=recv_sems.at[s],
                    device_id=(px, py, pc), device_id_type=pl.DeviceIdType.MESH))
        for cp in sends:
            cp.start()
        for cp in recvs:
            cp.wait_recv()
        for cp in sends:
            cp.wait_send()
        for cp in local:
            cp.wait()

    anyspec = pl.BlockSpec(memory_space=pl.ANY)
    return pl.pallas_call(
        body, name=name, in_specs=[anyspec] * n, out_specs=[anyspec] * n,
        out_shape=[jax.ShapeDtypeStruct(s, d) for s, d in out_shapes],
        scratch_shapes=[pltpu.SemaphoreType.DMA((7 * n,)), pltpu.SemaphoreType.DMA((7 * n,)),
                        pltpu.SemaphoreType.DMA((n,))],
    )(*srcs)


def _all_gather_lead(xs, name):
    return _exchange(
        xs, [((N_DEV,) + x.shape, x.dtype) for x in xs],
        lambda t, ref, peer: ref, lambda t, ref, who: ref.at[who], name)


def _all_to_all_lead(xs, name):
    return _exchange(
        xs, [(x.shape, x.dtype) for x in xs],
        lambda t, ref, peer: ref.at[peer], lambda t, ref, who: ref.at[who], name)


_W_AXIS = {"w_in": 1, "w_ret_out": 0, "w_sb_out": 0, "w_mix_out": 0, "w_up": 1, "w_down": 0}
_W_NAMES = tuple(_W_AXIS)


def _shard_slice(axis, width):
    def f(ref, who):
        start = pl.multiple_of(who * width, width)
        if axis == 0:
            return ref.at[:, pl.ds(start, width), :]
        return ref.at[:, :, pl.ds(start, width)]
    return f


def _gather_weights(shards):
    xs = [shards[nm] for nm in _W_NAMES]
    shapes, dsts = [], []
    for nm, x in zip(_W_NAMES, xs):
        ax = _W_AXIS[nm]
        full = list(x.shape)
        full[1 + ax] *= N_DEV
        shapes.append((tuple(full), x.dtype))
        dsts.append(_shard_slice(ax, x.shape[1 + ax]))
    return _exchange(xs, shapes, lambda t, ref, peer: ref, lambda t, ref, who: dsts[t](ref, who), "gather_weights")


def _scatter_grads(grads, layer_tag):
    xs = [grads[nm] for nm in _W_NAMES]
    shapes, srcs = [], []
    for nm, x in zip(_W_NAMES, xs):
        ax = _W_AXIS[nm]
        sh = list(x.shape)
        sh[ax] //= N_DEV
        shapes.append(((N_DEV,) + tuple(sh), x.dtype))
        width = sh[ax]

        def src(ref, who, ax=ax, width=width):
            start = pl.multiple_of(who * width, width)
            return ref.at[pl.ds(start, width), :] if ax == 0 else ref.at[:, pl.ds(start, width)]
        srcs.append(src)
    return _exchange(xs, shapes, lambda t, ref, peer: srcs[t](ref, peer), lambda t, ref, who: ref.at[who],
                     "scatter_grads")


def _adam(parts, w, m, v, name, tr=256):
    P, R, C = parts.shape
    tr = min(tr, R)
    assert R % tr == 0
    bc1 = 1.0 / (1.0 - ADAM_B1 ** ADAM_STEP)
    bc2 = 1.0 / (1.0 - ADAM_B2 ** ADAM_STEP)

    def body(p_ref, w_ref, m_ref, v_ref, g_out, d_out, m_out, v_out):
        g = p_ref[0].astype(F32)
        for s in range(1, P):
            g = g + p_ref[s].astype(F32)
        mm = ADAM_B1 * m_ref[...] + (1.0 - ADAM_B1) * g
        vv = ADAM_B2 * v_ref[...] + (1.0 - ADAM_B2) * jnp.square(g)
        g_out[...] = g
        m_out[...] = mm
        v_out[...] = vv
        d_out[...] = -ADAM_LR * ((mm * bc1) / (jnp.sqrt(vv * bc2) + ADAM_EPS) + ADAM_WD * w_ref[...])

    spec = pl.BlockSpec((tr, C), lambda i: (i, 0))
    return pl.pallas_call(
        body, name=name, grid=(R // tr,),
        in_specs=[pl.BlockSpec((P, tr, C), lambda i: (0, i, 0)), spec, spec, spec],
        out_specs=[spec] * 4, out_shape=[jax.ShapeDtypeStruct((R, C), F32)] * 4,
        compiler_params=_cparams(("parallel",)),
    )(parts, w, m, v)


def _mod_partial(cact_all, w_ada_l, b_ada_l):
    def body(c_ref, w_ref, b_ref, o_ref):
        o_ref[...] = _dot(c_ref[...].astype(BF16), w_ref[...].astype(BF16)) + b_ref[...]

    return pl.pallas_call(
        body, name="mod_partial", out_shape=jax.ShapeDtypeStruct((cact_all.shape[0], w_ada_l.shape[1]), F32),
        compiler_params=pltpu.CompilerParams(vmem_limit_bytes=VMEM_LIMIT),
    )(cact_all, w_ada_l, b_ada_l)


def _ada_grad(cact_t, dmod):
    D, n = cact_t.shape[0], dmod.shape[1]

    def body(c_ref, d_ref, o_ref):
        ct = c_ref[...].astype(BF16).astype(F32)
        dm = d_ref[...].astype(BF16).astype(F32)
        acc = ct[:, 0:1] * dm[0:1, :]
        for b in range(1, N_DEV):
            acc = acc + ct[:, b:b + 1] * dm[b:b + 1, :]
        o_ref[0] = acc

    return pl.pallas_call(
        body, name="ada_grad", out_shape=jax.ShapeDtypeStruct((1, D, n), F32),
        compiler_params=pltpu.CompilerParams(vmem_limit_bytes=VMEM_LIMIT),
    )(cact_t, dmod)


def _norm_mod(x, r, gv, sh):
    return x * r * gv + sh


def _silu(x):
    return x * _sigmoid(x)


def _layer_fwd(x0, mod, gn1, gn2, W, rc, tri):
    S = x0.shape[0]
    sh1, sc1, g1m, sh2, sc2, g2m = [mod[i:i + 1] for i in range(N_MOD)]
    gv1 = gn1 * (1.0 + sc1)
    gv2 = gn2 * (1.0 + sc2)
    (r1,) = _ew(lambda x: (lax.rsqrt(jnp.mean(x * x, axis=-1, keepdims=True) + EPS),), [x0], outs=[('col', F32)],
                name="row_rstd")
    (p,) = _mm(x0, W["w_in"], tm=1024, tn=1024, tk=512, a_ex=[(r1, 'm'), (gv1, 'k'), (sh1, 'k')], pro=_norm_mod,
               outs=(BF16,), name="mm_in")
    yret, rstate = _ret_fwd(p, rc)
    ysb, sbc = _sb_fwd(p, tri)
    rg = (p, O_RG)
    (ya,) = _mm_cols(yret, W["w_ret_out"], a_cols=[rg], pro=lambda yr, g: _silu(g.astype(F32)) * yr.astype(F32),
                     outs=(BF16,), name="mm_ret_out")
    yb, mg = _mm_cols(ysb, W["w_sb_out"], o_cols=[(ya, 0), (p, O_GA), (p, O_GB)],
                      epi=lambda acc, a, ga, gb: (acc, _sigmoid(ga.astype(F32)) * a.astype(F32)
                                                  + _sigmoid(gb.astype(F32)) * acc),
                      outs=(BF16, BF16), name="mm_sb_out")
    mo, x1 = _mm(mg, W["w_mix_out"], tm=1024, tn=1024, tk=512, o_ex=[(x0, 'o'), (g1m, 'n')],
                 epi=lambda acc, x, g: (acc, x + g * acc), outs=(BF16, F32), name="mm_mix_out")
    (r2,) = _ew(lambda x: (lax.rsqrt(jnp.mean(x * x, axis=-1, keepdims=True) + EPS),), [x1], outs=[('col', F32)],
                name="row_rstd")
    (act,) = _mm(x1, W["w_up"], tm=1024, tn=1024, tk=512, a_ex=[(r2, 'm'), (gv2, 'k'), (sh2, 'k')], pro=_norm_mod,
                 epi=lambda acc: (jnp.maximum(acc, 0.0),), outs=(BF16,), name="mm_up")
    dn, x2 = _mm(act, W["w_down"], tm=1024, tn=1024, tk=512, pro=lambda a: jnp.square(a.astype(F32)),
                 o_ex=[(x1, 'o'), (g2m, 'n')], epi=lambda acc, x, g: (acc, x + g * acc), outs=(BF16, F32),
                 name="mm_down")
    saved = dict(x0=x0, r1=r1, p=p, yret=yret, rstate=rstate, ysb=ysb, sbc=sbc, ya=ya, yb=yb, mg=mg, mo=mo, x1=x1, r2=r2,
                 act=act, dn=dn, gv1=gv1, gv2=gv2, mod=mod, gn1=gn1, gn2=gn2)
    return x2, saved


def _mm_cols(a, b, *, a_cols=(), o_cols=(), pro=None, epi=None, outs, name, ta=False, tb=False):
    if ta:
        K, M = a.shape
    else:
        M, K = a.shape
    N = b.shape[0] if tb else b.shape[1]
    tm, tn, tk = min(1024, M), min(1024, N), min(512, K)
    nk = K // tk
    dims = (((0 if ta else 1,), (1 if tb else 0,)), ((), ()))
    in_specs = [
        pl.BlockSpec((tk, tm), lambda i, j, k: (k, i)) if ta else pl.BlockSpec((tm, tk), lambda i, j, k: (i, k)),
        pl.BlockSpec((tn, tk), lambda i, j, k: (j, k)) if tb else pl.BlockSpec((tk, tn), lambda i, j, k: (k, j)),
    ]
    args = [a, b]
    for arr, off in a_cols:
        if ta:
            assert off % tm == 0
            in_specs.append(pl.BlockSpec((tk, tm), lambda i, j, k, o=off // tm: (k, o + i)))
        else:
            assert off % tk == 0
            in_specs.append(pl.BlockSpec((tm, tk), lambda i, j, k, o=off // tk: (i, o + k)))
        args.append(arr)
    for arr, off in o_cols:
        assert off % tn == 0
        in_specs.append(pl.BlockSpec((tm, tn), lambda i, j, k, o=off // tn: (i, o + j)))
        args.append(arr)
    na, no, nout = len(a_cols), len(o_cols), len(outs)

    def body(*refs):
        a_ref, b_ref = refs[0], refs[1]
        aex = refs[2:2 + na]
        oex = refs[2 + na:2 + na + no]
        out_refs = refs[2 + na + no:2 + na + no + nout]
        acc = refs[-1]
        k = pl.program_id(2)

        @pl.when(k == 0)
        def _():
            acc[...] = jnp.zeros_like(acc)

        at = a_ref[...]
        if pro is not None:
            at = pro(at, *[r[...] for r in aex])
        acc[...] += lax.dot_general(at.astype(BF16), b_ref[...].astype(BF16), dims, preferred_element_type=F32)

        @pl.when(k == nk - 1)
        def _():
            res = acc[...]
            vals = epi(res, *[r[...] for r in oex]) if epi is not None else (res,)
            for o_ref, v in zip(out_refs, vals):
                o_ref[...] = v.astype(o_ref.dtype)

    return pl.pallas_call(
        body, name=name, grid=(M // tm, N // tn, nk), in_specs=in_specs,
        out_specs=[pl.BlockSpec((tm, tn), lambda i, j, k: (i, j)) for _ in outs],
        out_shape=[jax.ShapeDtypeStruct((M, N), dt) for dt in outs],
        scratch_shapes=[pltpu.VMEM((tm, tn), F32)],
        compiler_params=_cparams(("parallel", "parallel", "arbitrary")),
    )(*args)


def _norm_bwd(dh, x, r, dres, gv, gn, extra_rows=(), extra_vecs=(), extra_fn=None, extra_outs=(), name="norm_bwd"):
    D = x.shape[1]
    ne = len(extra_rows)

    def fn(dh_t, x_t, dres_t, *rest):
        er, rest = rest[:ne], rest[ne:]
        gv_t = rest[0]
        ev, r_t = rest[1:-1], rest[-1]
        xh = x_t * r_t
        dxh = dh_t * gv_t
        dx = r_t * (dxh - xh * jnp.mean(dxh * xh, axis=-1, keepdims=True)) + dres_t
        base = (dx, dh_t, dh_t * xh)
        if extra_fn is None:
            return base
        return base + tuple(extra_fn(dx, *er, *ev))

    return _ew(fn, [dh, x, dres] + list(extra_rows), vecs=[gv] + list(extra_vecs), cols=[r],
               outs=[('row', D, F32), ('sum', D), ('sum', D)] + list(extra_outs), name=name)


def _layer_bwd(dx2, sv, W, rc, tri):
    mod = sv['mod']
    sh1, sc1, g1m, sh2, sc2, g2m = [mod[i:i + 1] for i in range(N_MOD)]
    D = D_MODEL
    p = sv['p']
    d_g2m, d_dn = _ew(lambda dx, dn, g: (dx * dn.astype(F32), dx * g), [dx2, sv['dn']], vecs=[g2m],
                      outs=[('sum', D), ('row', D, BF16)], name="gate_bwd")
    (d_up,) = _mm(d_dn, W["w_down"], tb=True, tm=1024, tn=1024, tk=512, o_ex=[(sv['act'], 'o')],
                  epi=lambda acc, a: (acc * 2.0 * a.astype(F32),), outs=(BF16,), name="mm_down_dx")
    (gw_down,) = _mm(sv['act'], d_dn, ta=True, tm=1024, tn=1024, tk=512, pro=lambda a: jnp.square(a.astype(F32)),
                     outs=(BF16,), name="mm_down_dw")
    (gw_up,) = _mm(sv['x1'], d_up, ta=True, tm=1024, tn=1024, tk=512,
                   a_ex=[(sv['r2'], 'k'), (sv['gv2'].reshape(1, D), 'm'), (sh2, 'm')], pro=_norm_mod,
                   outs=(BF16,), name="mm_up_dw")
    (d_h2,) = _mm(d_up, W["w_up"], tb=True, tm=1024, tn=1024, tk=512, outs=(F32,), name="mm_up_dx")
    dx1, d_sh2, s_h2, d_g1m, d_mo = _norm_bwd(
        d_h2, sv['x1'], sv['r2'], dx2, sv['gv2'], sv['gn2'],
        extra_rows=[sv['mo']], extra_vecs=[g1m],
        extra_fn=lambda dx, mo, g: (dx * mo.astype(F32), dx * g),
        extra_outs=[('sum', D), ('row', D, BF16)], name="norm_bwd_mlp")
    d_sc2 = sv['gn2'] * s_h2
    d_gn2 = (1.0 + sc2) * s_h2
    def mix_epi(acc, ya, yb, ga, gb):
        sa, sb = _sigmoid(ga.astype(F32)), _sigmoid(gb.astype(F32))
        return (acc * sa, acc * sb, acc * ya.astype(F32) * sa * (1.0 - sa), acc * yb.astype(F32) * sb * (1.0 - sb))

    d_ya, d_yb, d_ga, d_gb = _mm_cols(d_mo, W["w_mix_out"], tb=True,
                                      o_cols=[(sv['ya'], 0), (sv['yb'], 0), (p, O_GA), (p, O_GB)], epi=mix_epi,
                                      outs=(BF16,) * 4, name="mm_mix_dx")
    (gw_mix,) = _mm(sv['mg'], d_mo, ta=True, tm=1024, tn=1024, tk=512, outs=(BF16,), name="mm_mix_dw")

    def ro_epi(acc, g, yr):
        gf = g.astype(F32)
        s = _sigmoid(gf)
        return (acc * yr.astype(F32) * s * (1.0 + gf * (1.0 - s)), acc * gf * s)

    d_rg, d_yret = _mm_cols(d_ya, W["w_ret_out"], tb=True, o_cols=[(p, O_RG), (sv['yret'], 0)], epi=ro_epi,
                            outs=(BF16, BF16), name="mm_ret_dx")
    (gw_ro,) = _mm_cols(sv['yret'], d_ya, ta=True, a_cols=[(p, O_RG)],
                        pro=lambda yr, g: _silu(g.astype(F32)) * yr.astype(F32), outs=(BF16,), name="mm_ret_dw")
    (d_ysb,) = _mm(d_yb, W["w_sb_out"], tb=True, tm=1024, tn=1024, tk=512, outs=(BF16,), name="mm_sb_dx")
    (gw_so,) = _mm(sv['ysb'], d_yb, ta=True, tm=1024, tn=1024, tk=512, outs=(BF16,), name="mm_sb_dw")
    d_sq, d_sk, d_sv = _sb_bwd(p, sv['sbc'], d_ysb, tri)
    d_rq, d_rk, d_rv = _ret_bwd(p, sv['rstate'], d_yret, rc)
    dp = jnp.concatenate([d_rq, d_rk, d_rv, d_rg, d_sq, d_sk, d_sv, d_ga, d_gb], axis=1)
    (gw_in,) = _mm(sv['x0'], dp, ta=True, tm=1024, tn=1024, tk=512,
                   a_ex=[(sv['r1'], 'k'), (sv['gv1'].reshape(1, D), 'm'), (sh1, 'm')], pro=_norm_mod,
                   outs=(BF16,), name="mm_in_dw")
    (d_h,) = _mm(dp, W["w_in"], tb=True, tm=1024, tn=1024, tk=512, outs=(F32,), name="mm_in_dx")
    dx0, d_sh1, s_h1 = _norm_bwd(d_h, sv['x0'], sv['r1'], dx1, sv['gv1'], sv['gn1'], name="norm_bwd_mix")
    d_sc1 = sv['gn1'] * s_h1
    d_gn1 = (1.0 + sc1) * s_h1
    d_mod = jnp.concatenate([d_sh1, d_sc1, d_g1m, d_sh2, d_sc2, d_g2m], axis=1)
    gw = dict(w_in=gw_in, w_ret_out=gw_ro, w_sb_out=gw_so, w_mix_out=gw_mix, w_up=gw_up, w_down=gw_down)
    return dx0, gw, d_mod, d_gn1, d_gn2


def kernel(x, c, norm_mix_g, w_in, w_ret_out, w_sb_out, w_mix_out, norm_mlp_g, w_up, w_down, w_ada, b_ada, final_g, loss_target, m_norm_mix_g, m_w_in, m_w_ret_out, m_w_sb_out, m_w_mix_out, m_norm_mlp_g, m_w_up, m_w_down, m_w_ada, m_b_ada, m_final_g, v_norm_mix_g, v_w_in, v_w_ret_out, v_w_sb_out, v_w_mix_out, v_norm_mlp_g, v_w_up, v_w_down, v_w_ada, v_b_ada, v_final_g):
    S, D = x.shape[1], x.shape[2]
    x0 = x.reshape(S, D)
    tgt = loss_target.reshape(S, D)
    me = 4 * lax.axis_index("x") + 2 * lax.axis_index("y") + lax.axis_index("c")
    wts = dict(w_in=w_in, w_ret_out=w_ret_out, w_sb_out=w_sb_out, w_mix_out=w_mix_out, w_up=w_up, w_down=w_down)
    mts = dict(w_in=m_w_in, w_ret_out=m_w_ret_out, w_sb_out=m_w_sb_out, w_mix_out=m_w_mix_out, w_up=m_w_up, w_down=m_w_down)
    vts = dict(w_in=v_w_in, w_ret_out=v_w_ret_out, w_sb_out=v_w_sb_out, w_mix_out=v_w_mix_out, w_up=v_w_up, w_down=v_w_down)
    rc = _ret_consts(S)
    tri = _tri()

    shards = {}
    for nm in _W_NAMES:
        w = wts[nm]
        (wb,) = _ew(lambda t: (t,), [w.reshape(-1, w.shape[-1])], outs=[('row', w.shape[-1], BF16)], name="cast_bf16")
        shards[nm] = wb.reshape(w.shape)
    full = dict(zip(_W_NAMES, _gather_weights(shards)))

    (cact,) = _ew(lambda t: (_silu(t),), [jnp.pad(c, ((0, 7), (0, 0)))], outs=[('row', D, F32)], name="silu_c")
    (cact_all,) = _all_gather_lead([cact[0:1]], "gather_c")
    cact_all = cact_all.reshape(N_DEV, D)
    cact16 = jnp.pad(cact_all, ((0, 8), (0, 0)))
    n_ada = w_ada.shape[2]
    b_loc = lax.dynamic_slice_in_dim(b_ada, me * n_ada, n_ada, axis=1)
    mods = [_mod_partial(cact16, w_ada[l], b_loc[l:l + 1])[:N_DEV] for l in range(DEPTH)]
    modp = jnp.stack(mods, axis=1)
    (modr,) = _all_to_all_lead([modp], "scatter_mod")
    mod_full = jnp.transpose(modr, (1, 0, 2)).reshape(DEPTH, N_MOD, D)

    xs = x0
    saved = []
    for l in range(DEPTH):
        W = {nm: full[nm][l] for nm in _W_NAMES}
        xs, sv = _layer_fwd(xs, mod_full[l], norm_mix_g[l:l + 1], norm_mlp_g[l:l + 1], W, rc, tri)
        sv['W'] = W
        saved.append(sv)

    fg = final_g.reshape(1, D)

    def head(xt, tg, g):
        r = lax.rsqrt(jnp.mean(xt * xt, axis=-1, keepdims=True) + EPS)
        xh = xt * r
        e = xh * g - tg
        dy = e * (1.0 / D)
        dxh = dy * g
        dx = r * (dxh - xh * jnp.mean(dxh * xh, axis=-1, keepdims=True))
        return dx, dy * xh, 0.5 * e * e * (1.0 / D)

    dxs, d_fg, loss_cols = _ew(head, [xs, tgt], vecs=[fg], outs=[('row', D, F32), ('sum', D), ('sum', D)], name="loss_head")

    small = [None] * DEPTH
    parts = [None] * DEPTH
    for l in reversed(range(DEPTH)):
        sv = saved[l]
        dxs, gw, d_mod, d_gn1, d_gn2 = _layer_bwd(dxs, sv, sv['W'], rc, tri)
        small[l] = (d_mod, d_gn1, d_gn2)
        parts[l] = dict(zip(_W_NAMES, _scatter_grads(gw, l)))
    grad_x = dxs.reshape(1, S, D)

    pack = jnp.concatenate([small[l][0] for l in range(DEPTH)] + [small[l][1] for l in range(DEPTH)]
                           + [small[l][2] for l in range(DEPTH)] + [d_fg, loss_cols], axis=1)
    (packs,) = _all_gather_lead([pack], "gather_small")
    packs = packs.reshape(N_DEV, -1)
    o = 0
    dmod_all = []
    for l in range(DEPTH):
        dmod_all.append(packs[:, o:o + N_MOD * D]); o += N_MOD * D
    gn1_parts = packs[:, o:o + DEPTH * D].reshape(N_DEV, DEPTH, D); o += DEPTH * D
    gn2_parts = packs[:, o:o + DEPTH * D].reshape(N_DEV, DEPTH, D); o += DEPTH * D
    fg_parts = packs[:, o:o + D].reshape(N_DEV, 1, D); o += D
    loss_parts = packs[:, o:o + D]
    (loss_sum,) = _ew(lambda t: (t,), [loss_parts], outs=[('sum', D)], name="loss_sum")
    loss = jnp.sum(loss_sum)

    res = {}
    for nm in _W_NAMES:
        per = [_adam(parts[l][nm], wts[nm][l], mts[nm][l], vts[nm][l], "adam") for l in range(DEPTH)]
        res[nm] = [jnp.stack([per[l][i] for l in range(DEPTH)]) for i in range(4)]
    res["norm_mix_g"] = _adam(gn1_parts, norm_mix_g, m_norm_mix_g, v_norm_mix_g, "adam")
    res["norm_mlp_g"] = _adam(gn2_parts, norm_mlp_g, m_norm_mlp_g, v_norm_mlp_g, "adam")
    fgr = _adam(fg_parts, fg, m_final_g.reshape(1, D), v_final_g.reshape(1, D), "adam")
    res["final_g"] = [t.reshape(D) for t in fgr]
    bparts = jnp.stack(dmod_all, axis=1)
    res["b_ada"] = _adam(bparts, b_ada, m_b_ada, v_b_ada, "adam")
    cact_t = cact_all.T
    ada = []
    for l in range(DEPTH):
        dm_loc = lax.dynamic_slice_in_dim(dmod_all[l], me * n_ada, n_ada, axis=1)
        ada.append(_adam(_ada_grad(cact_t, dm_loc), w_ada[l], m_w_ada[l], v_w_ada[l], "adam"))
    res["w_ada"] = [jnp.stack([ada[l][i] for l in range(DEPTH)]) for i in range(4)]

    order = ['norm_mix_g', 'w_in', 'w_ret_out', 'w_sb_out', 'w_mix_out', 'norm_mlp_g', 'w_up', 'w_down', 'w_ada', 'b_ada', 'final_g']
    out = [loss, grad_x]
    for i in range(4):
        out += [res[nm][i] for nm in order]
    return tuple(out)
```

```python
import functools
import math

import jax
import jax.numpy as jnp
import numpy as np
from jax import lax
from jax.experimental import pallas as pl
from jax.experimental.pallas import tpu as pltpu

F32 = jnp.float32
BF16 = jnp.bfloat16

N_DEV = 8
D_MODEL = 1024
DEPTH = 2
RET_HEADS = 4
RET_QK = 256
RET_V = 512
RET_CHUNK = 128
ROPE_BASE = 10000.0
SB_HEADS = 16
SB_DIM = 64
D_FF = 4096
N_MOD = 6
EPS = 1e-6
GN_EPS = 1e-5
O_RQ, O_RK, O_RV, O_RG, O_SQ, O_SK, O_SV, O_GA, O_GB = 0, 1024, 2048, 4096, 6144, 7168, 8192, 9216, 10240
IN_W = 11264

ADAM_LR, ADAM_B1, ADAM_B2, ADAM_EPS, ADAM_WD, ADAM_STEP = 0.001, 0.9, 0.999, 1e-08, 0.01, 10

VMEM_LIMIT = 56 * 1024 * 1024


def _cparams(sem):
    return pltpu.CompilerParams(dimension_semantics=sem, vmem_limit_bytes=VMEM_LIMIT)


def _mm(a, b, *, ta=False, tb=False, tm=512, tn=512, tk=512, a_ex=(), pro=None, o_ex=(), epi=None,
        outs=(F32,), name):
    if ta:
        K, M = a.shape
    else:
        M, K = a.shape
    N = b.shape[0] if tb else b.shape[1]
    tm, tn, tk = min(tm, M), min(tn, N), min(tk, K)
    assert M % tm == 0 and N % tn == 0 and K % tk == 0, (name, M, N, K, tm, tn, tk)
    nk = K // tk
    in_specs = [
        pl.BlockSpec((tk, tm), lambda i, j, k: (k, i)) if ta else pl.BlockSpec((tm, tk), lambda i, j, k: (i, k)),
        pl.BlockSpec((tn, tk), lambda i, j, k: (j, k)) if tb else pl.BlockSpec((tk, tn), lambda i, j, k: (k, j)),
    ]
    args = [a, b]
    for arr, kind in a_ex:
        if kind == 'a':
            in_specs.append(in_specs[0])
        elif kind == 'k':
            in_specs.append(pl.BlockSpec((tk, 1), lambda i, j, k: (k, 0)) if ta
                            else pl.BlockSpec((1, tk), lambda i, j, k: (0, k)))
        else:
            in_specs.append(pl.BlockSpec((1, tm), lambda i, j, k: (0, i)) if ta
                            else pl.BlockSpec((tm, 1), lambda i, j, k: (i, 0)))
        args.append(arr)
    for arr, kind in o_ex:
        if kind == 'o':
            in_specs.append(pl.BlockSpec((tm, tn), lambda i, j, k: (i, j)))
        elif kind == 'n':
            in_specs.append(pl.BlockSpec((1, tn), lambda i, j, k: (0, j)))
        else:
            in_specs.append(pl.BlockSpec((tm, 1), lambda i, j, k: (i, 0)))
        args.append(arr)
    na, no, nout = len(a_ex), len(o_ex), len(outs)
    dims = (((0 if ta else 1,), (1 if tb else 0,)), ((), ()))

    def body(*refs):
        a_ref, b_ref = refs[0], refs[1]
        aex = refs[2:2 + na]
        oex = refs[2 + na:2 + na + no]
        out_refs = refs[2 + na + no:2 + na + no + nout]
        acc = refs[-1]
        k = pl.program_id(2)

        @pl.when(k == 0)
        def _():
            acc[...] = jnp.zeros_like(acc)

        at = a_ref[...]
        if pro is not None:
            at = pro(at, *[r[...] for r in aex])
        acc[...] += lax.dot_general(at.astype(BF16), b_ref[...].astype(BF16), dims, preferred_element_type=F32)

        @pl.when(k == nk - 1)
        def _():
            res = acc[...]
            vals = epi(res, *[r[...] for r in oex]) if epi is not None else (res,)
            for o_ref, v in zip(out_refs, vals):
                o_ref[...] = v.astype(o_ref.dtype)

    res = pl.pallas_call(
        body, name=name, grid=(M // tm, N // tn, nk), in_specs=in_specs,
        out_specs=[pl.BlockSpec((tm, tn), lambda i, j, k: (i, j)) for _ in outs],
        out_shape=[jax.ShapeDtypeStruct((M, N), dt) for dt in outs],
        scratch_shapes=[pltpu.VMEM((tm, tn), F32)],
        compiler_params=_cparams(("parallel", "parallel", "arbitrary")),
    )(*args)
    return res


def _ew(fn, rows, vecs=(), cols=(), outs=(), tr=256, name=None):
    S = rows[0].shape[0]
    tr = min(tr, S)
    assert S % tr == 0
    in_specs, args = [], []
    for r in rows:
        in_specs.append(pl.BlockSpec((tr, r.shape[1]), lambda i: (i, 0)))
        args.append(r)
    for v in vecs:
        in_specs.append(pl.BlockSpec((1, v.shape[1]), lambda i: (0, 0)))
        args.append(v)
    for c in cols:
        in_specs.append(pl.BlockSpec((tr, 1), lambda i: (i, 0)))
        args.append(c)
    out_specs, out_shape = [], []
    for o in outs:
        if o[0] == 'row':
            out_specs.append(pl.BlockSpec((tr, o[1]), lambda i: (i, 0)))
            out_shape.append(jax.ShapeDtypeStruct((S, o[1]), o[2]))
        elif o[0] == 'sum':
            out_specs.append(pl.BlockSpec((1, o[1]), lambda i: (0, 0)))
            out_shape.append(jax.ShapeDtypeStruct((1, o[1]), F32))
        else:
            out_specs.append(pl.BlockSpec((tr, 1), lambda i: (i, 0)))
            out_shape.append(jax.ShapeDtypeStruct((S, 1), o[1]))
    nin = len(args)

    def body(*refs):
        i = pl.program_id(0)
        vals = fn(*[r[...] for r in refs[:nin]])
        for o, o_ref, v in zip(outs, refs[nin:], vals):
            if o[0] == 'sum':
                @pl.when(i == 0)
                def _():
                    o_ref[...] = jnp.zeros_like(o_ref)
                o_ref[...] += jnp.sum(v.astype(F32), axis=0, keepdims=True)
            else:
                o_ref[...] = v.astype(o_ref.dtype)

    return pl.pallas_call(
        body, name=name, grid=(S // tr,), in_specs=in_specs, out_specs=out_specs, out_shape=out_shape,
        compiler_params=_cparams(("arbitrary",)),
    )(*args)


def _sigmoid(x):
    return 1.0 / (1.0 + jnp.exp(-x))


def _ret_consts(S):
    h = np.arange(RET_HEADS, dtype=np.float64)
    log_gamma = np.log1p(-np.power(2.0, -5.0 - h))
    idx = np.arange(RET_CHUNK, dtype=np.float64)
    rel = idx[:, None] - idx[None, :]
    decay = np.where(rel >= 0, np.exp(np.maximum(rel, 0.0) * log_gamma[:, None, None]), 0.0)
    xi = np.exp((idx + 1.0) * log_gamma[:, None])[:, :, None]
    zeta = np.exp((RET_CHUNK - 1.0 - idx) * log_gamma[:, None])[:, :, None]
    gamma_c = np.exp(RET_CHUNK * log_gamma)[:, None, None]
    half = RET_QK // 2
    inv_freq = np.power(ROPE_BASE, -np.arange(half, dtype=np.float64) / half).astype(np.float32)
    ang = np.arange(S, dtype=np.float32)[:, None] * inv_freq[None, :]
    f = lambda t: jnp.asarray(t, F32)
    return dict(decay=f(decay), xi=f(xi), zeta=f(zeta), gc=f(gamma_c), cos=f(np.cos(ang)), sin=f(np.sin(ang)))


def _rot(t, cos, sin):
    half = RET_QK // 2
    t1, t2 = t[:, :half], t[:, half:]
    return jnp.concatenate([t1 * cos - t2 * sin, t1 * sin + t2 * cos], axis=-1)


def _rot_inv(t, cos, sin):
    half = RET_QK // 2
    t1, t2 = t[:, :half], t[:, half:]
    return jnp.concatenate([t1 * cos + t2 * sin, t2 * cos - t1 * sin], axis=-1)


_NT = (((1,), (1,)), ((), ()))
_TN = (((0,), (0,)), ((), ()))


def _dot(a, b):
    return jnp.dot(a, b, preferred_element_type=F32)


def _dot_nt(a, b):
    return lax.dot_general(a, b, _NT, preferred_element_type=F32)


def _dot_tn(a, b):
    return lax.dot_general(a, b, _TN, preferred_element_type=F32)


def _ret_in_specs(C, rev, NC):
    n_of = (lambda n: NC - 1 - n) if rev else (lambda n: n)
    qb, vb = O_RQ // RET_QK, O_RV // RET_V
    kb = O_RK // RET_QK
    return [
        pl.BlockSpec((C, RET_QK), lambda h, n: (n_of(n), qb + h)),
        pl.BlockSpec((C, RET_QK), lambda h, n: (n_of(n), kb + h)),
        pl.BlockSpec((C, RET_V), lambda h, n: (n_of(n), vb + h)),
        pl.BlockSpec((C, RET_QK // 2), lambda h, n: (n_of(n), 0)),
        pl.BlockSpec((C, RET_QK // 2), lambda h, n: (n_of(n), 0)),
        pl.BlockSpec((1, C, C), lambda h, n: (h, 0, 0)),
        pl.BlockSpec((1, C, 1), lambda h, n: (h, 0, 0)),
        pl.BlockSpec((1, C, 1), lambda h, n: (h, 0, 0)),
        pl.BlockSpec((1, 1, 1), lambda h, n: (h, 0, 0)),
    ]


def _ret_fwd(p, rc):
    S = p.shape[0]
    C = RET_CHUNK
    NC = S // C

    def body(q_ref, k_ref, v_ref, cos_ref, sin_ref, dec_ref, xi_ref, zeta_ref, gc_ref, y_ref, rs_ref, r_acc):
        n = pl.program_id(1)

        @pl.when(n == 0)
        def _():
            r_acc[...] = jnp.zeros_like(r_acc)

        cos, sin = cos_ref[...], sin_ref[...]
        q = _rot(q_ref[...].astype(F32), cos, sin).astype(BF16)
        kf = _rot(k_ref[...].astype(F32), cos, sin) * (RET_QK ** -0.5)
        k = kf.astype(BF16)
        v = v_ref[...]
        r = r_acc[...]
        rb = r.astype(BF16)
        rs_ref[0, 0] = rb
        s = (_dot_nt(q, k) * dec_ref[0]).astype(BF16)
        o = _dot(s, v) + _dot(q, rb) * xi_ref[0]
        mu = jnp.mean(o, axis=-1, keepdims=True)
        var = jnp.mean(jnp.square(o - mu), axis=-1, keepdims=True)
        y_ref[...] = ((o - mu) * lax.rsqrt(var + GN_EPS)).astype(y_ref.dtype)
        kz = (kf * zeta_ref[0]).astype(BF16)
        r_acc[...] = r * gc_ref[0] + _dot_tn(kz, v)

    return pl.pallas_call(
        body, name="ret_fwd", grid=(RET_HEADS, NC), in_specs=_ret_in_specs(C, False, NC),
        out_specs=[pl.BlockSpec((C, RET_V), lambda h, n: (n, h)),
                   pl.BlockSpec((1, 1, RET_QK, RET_V), lambda h, n: (h, n, 0, 0))],
        out_shape=[jax.ShapeDtypeStruct((S, RET_HEADS * RET_V), BF16),
                   jax.ShapeDtypeStruct((RET_HEADS, NC, RET_QK, RET_V), BF16)],
        scratch_shapes=[pltpu.VMEM((RET_QK, RET_V), F32)],
        compiler_params=_cparams(("parallel", "arbitrary")),
    )(p, p, p, rc['cos'], rc['sin'], rc['decay'], rc['xi'], rc['zeta'], rc['gc'])


def _ret_bwd(p, rstate, dy, rc):
    S = p.shape[0]
    C = RET_CHUNK
    NC = S // C

    def body(q_ref, k_ref, v_ref, cos_ref, sin_ref, dec_ref, xi_ref, zeta_ref, gc_ref, rs_ref, dy_ref,
             dq_ref, dk_ref, dv_ref, dr_acc):
        t = pl.program_id(1)

        @pl.when(t == 0)
        def _():
            dr_acc[...] = jnp.zeros_like(dr_acc)

        cos, sin = cos_ref[...], sin_ref[...]
        dec, xi, zeta = dec_ref[0], xi_ref[0], zeta_ref[0]
        q = _rot(q_ref[...].astype(F32), cos, sin).astype(BF16)
        kf = _rot(k_ref[...].astype(F32), cos, sin) * (RET_QK ** -0.5)
        k = kf.astype(BF16)
        kz = (kf * zeta).astype(BF16)
        v = v_ref[...]
        rb = rs_ref[0, 0]
        s = (_dot_nt(q, k) * dec).astype(BF16)
        o = _dot(s, v) + _dot(q, rb) * xi
        mu = jnp.mean(o, axis=-1, keepdims=True)
        var = jnp.mean(jnp.square(o - mu), axis=-1, keepdims=True)
        rstd = lax.rsqrt(var + GN_EPS)
        yh = (o - mu) * rstd
        dyf = dy_ref[...].astype(F32)
        do = (dyf - jnp.mean(dyf, axis=-1, keepdims=True) - yh * jnp.mean(dyf * yh, axis=-1, keepdims=True)) * rstd
        dob = do.astype(BF16)
        doxi = (do * xi).astype(BF16)
        dr = dr_acc[...]
        drb = dr.astype(BF16)
        ds = (_dot_nt(dob, v) * dec).astype(BF16)
        dq = _dot(ds, k) + _dot_nt(doxi, rb)
        dk = _dot_tn(ds, q) + _dot_nt(v, drb) * zeta
        dv = _dot_tn(s, dob) + _dot(kz, drb)
        dr_acc[...] = dr * gc_ref[0] + _dot_tn(q, doxi)
        dq_ref[...] = _rot_inv(dq, cos, sin).astype(dq_ref.dtype)
        dk_ref[...] = (_rot_inv(dk, cos, sin) * (RET_QK ** -0.5)).astype(dk_ref.dtype)
        dv_ref[...] = dv.astype(dv_ref.dtype)

    rn = lambda n: NC - 1 - n
    in_specs = _ret_in_specs(C, True, NC) + [
        pl.BlockSpec((1, 1, RET_QK, RET_V), lambda h, n: (h, rn(n), 0, 0)),
        pl.BlockSpec((C, RET_V), lambda h, n: (rn(n), h)),
    ]
    return pl.pallas_call(
        body, name="ret_bwd", grid=(RET_HEADS, NC), in_specs=in_specs,
        out_specs=[pl.BlockSpec((C, RET_QK), lambda h, n: (rn(n), h)),
                   pl.BlockSpec((C, RET_QK), lambda h, n: (rn(n), h)),
                   pl.BlockSpec((C, RET_V), lambda h, n: (rn(n), h))],
        out_shape=[jax.ShapeDtypeStruct((S, RET_HEADS * RET_QK), BF16),
                   jax.ShapeDtypeStruct((S, RET_HEADS * RET_QK), BF16),
                   jax.ShapeDtypeStruct((S, RET_HEADS * RET_V), BF16)],
        scratch_shapes=[pltpu.VMEM((RET_QK, RET_V), F32)],
        compiler_params=_cparams(("parallel", "arbitrary")),
    )(p, p, p, rc['cos'], rc['sin'], rc['decay'], rc['xi'], rc['zeta'], rc['gc'], rstate, dy)


SB_T = 256
SB_SCALE = SB_DIM ** -0.5


def _tri():
    j = np.arange(SB_T)
    rev = (j[:, None] >= j[None, :]).astype(np.float32)
    return jnp.asarray(np.stack([np.concatenate([rev, rev]), np.concatenate([rev.T, rev.T])]), BF16)


def _run_sum(x, tri2):
    hi = x.astype(BF16)
    lo = (x - hi.astype(F32)).astype(BF16)
    return _dot(jnp.concatenate([hi, lo], axis=1), tri2)


def _sb_fwd(p, tri):
    S = p.shape[0]
    T = min(SB_T, S)
    NQ = S // T
    assert NQ <= 128
    qb, kb, vb = O_SQ // 128, O_SK // 128, O_SV // 128

    def body(q_ref, k_ref, v_ref, tri_ref, o_ref, cs_ref, o_acc, run, zbuf, abuf):
        i = pl.program_id(1)
        lane = lax.broadcasted_iota(jnp.int32, (1, 128), 1)
        tri2 = tri_ref[0]
        qs = [jnp.where((lane >= 64) if hh else (lane < 64), q_ref[...], jnp.zeros_like(q_ref[...]))
              * jnp.asarray(SB_SCALE, BF16) for hh in range(2)]
        cs_ref[...] = jnp.zeros_like(cs_ref)
        o_acc[...] = jnp.zeros_like(o_acc)
        run[...] = jnp.zeros_like(run)

        def kv(ref, j):
            return ref[pl.ds(pl.multiple_of(j * T, T), T), :]

        for hh in range(2):
            zbuf[hh] = _dot_nt(qs[hh], kv(k_ref, i))

        def block(t, diagonal):
            j = i - t
            zn = [_dot_nt(qs[hh], kv(k_ref, jnp.maximum(j - 1, 0))) for hh in range(2)]
            if not diagonal:
                av = [_dot(abuf[hh], kv(v_ref, j + 1)) for hh in range(2)]
            else:
                msk = lax.broadcasted_iota(jnp.int32, (T, T), 1) < lax.broadcasted_iota(jnp.int32, (T, T), 0)
            zs = [zbuf[hh] for hh in range(2)]
            cls = []
            for hh in range(2):
                sp = jnp.maximum(zs[hh], 0.0) + jnp.log(1.0 + jnp.exp(-jnp.abs(zs[hh])))
                if diagonal:
                    sp = jnp.where(msk, sp, 0.0)
                cls.append(_run_sum(sp, tri2))
            for hh in range(2):
                csl = slice(hh * 128, (hh + 1) * 128)
                cs = run[hh]
                a = jnp.exp(zs[hh] - cls[hh] - cs)
                if diagonal:
                    a = jnp.where(msk, a, 0.0)
                abuf[hh] = a.astype(BF16)
                cs_ref[:, csl] = jnp.where(lane == j, cs, cs_ref[:, csl])
                run[hh] = cs + cls[hh][:, 0:1]
            for hh in range(2):
                if not diagonal:
                    o_acc[hh] += av[hh]
                zbuf[hh] = zn[hh]

        block(0, True)

        def step(t, carry):
            block(t, False)
            return carry

        lax.fori_loop(1, i + 1, step, 0)
        o_ref[...] = jnp.where(lane < 64, o_acc[0] + _dot(abuf[0], kv(v_ref, 0)),
                               o_acc[1] + _dot(abuf[1], kv(v_ref, 0))).astype(o_ref.dtype)

    return pl.pallas_call(
        body, name="sb_fwd", grid=(SB_HEADS // 2, NQ),
        scratch_shapes=[pltpu.VMEM((2, T, 128), F32), pltpu.VMEM((2, T, 1), F32), pltpu.VMEM((2, T, T), F32),
                        pltpu.VMEM((2, T, T), BF16)],
        in_specs=[pl.BlockSpec((T, 128), lambda h, i: (i, qb + h)),
                  pl.BlockSpec((S, 128), lambda h, i: (0, kb + h)),
                  pl.BlockSpec((S, 128), lambda h, i: (0, vb + h)),
                  pl.BlockSpec((1, 2 * T, T), lambda h, i: (0, 0, 0))],
        out_specs=[pl.BlockSpec((T, 128), lambda h, i: (i, h)),
                   pl.BlockSpec((T, 256), lambda h, i: (i, h))],
        out_shape=[jax.ShapeDtypeStruct((S, SB_HEADS * SB_DIM), BF16),
                   jax.ShapeDtypeStruct((S, SB_HEADS * 128), F32)],
        compiler_params=_cparams(("parallel", "arbitrary")),
    )(p, p, p, tri)


def _sb_bwd(p, carries, dy, tri):
    S = p.shape[0]
    T = min(SB_T, S)
    NQ = S // T
    qb, kb, vb = O_SQ // 128, O_SK // 128, O_SV // 128

    def body(q_ref, k_ref, v_ref, cs_ref, dy_ref, tri_ref, dq_ref, dk_ref, dv_ref, dk_acc, dv_acc, dq_acc, run,
             zbuf, dabuf, dzbuf, abuf):
        i = pl.program_id(1)

        @pl.when(i == 0)
        def _():
            dk_acc[...] = jnp.zeros_like(dk_acc)
            dv_acc[...] = jnp.zeros_like(dv_acc)

        lane = lax.broadcasted_iota(jnp.int32, (1, 128), 1)
        rev2, fwd2 = tri_ref[0], tri_ref[1]
        hms = [(lane >= 64) if hh else (lane < 64) for hh in range(2)]
        qs = [jnp.where(hm, q_ref[...], jnp.zeros_like(q_ref[...])) * jnp.asarray(SB_SCALE, BF16) for hm in hms]
        dos = [jnp.where(hm, dy_ref[...], jnp.zeros_like(dy_ref[...])) for hm in hms]
        dq_acc[...] = jnp.zeros_like(dq_acc)
        run[...] = jnp.zeros_like(run)
        dzbuf[...] = jnp.zeros_like(dzbuf)
        abuf[...] = jnp.zeros_like(abuf)

        def kv(ref, j):
            return ref[pl.ds(pl.multiple_of(j * T, T), T), :]

        def flush(jp):
            kp = kv(k_ref, jp)
            dq_add = [_dot(dzbuf[hh], kp) for hh in range(2)]
            dk_add = _dot_tn(dzbuf[0], qs[0]) + _dot_tn(dzbuf[1], qs[1])
            dv_add = _dot_tn(abuf[0], dos[0]) + _dot_tn(abuf[1], dos[1])
            return dq_add, dk_add, dv_add

        def apply(jp, adds):
            dq_add, dk_add, dv_add = adds
            rows = pl.ds(pl.multiple_of(jp * T, T), T)
            for hh in range(2):
                dq_acc[hh] += dq_add[hh]
            dk_acc[rows, :] += dk_add
            dv_acc[rows, :] += dv_add

        for hh in range(2):
            zbuf[hh] = _dot_nt(qs[hh], kv(k_ref, 0))
            dabuf[hh] = _dot_nt(dos[hh], kv(v_ref, 0))

        def block(j, diagonal):
            jp = jnp.maximum(j - 1, 0)
            if not diagonal:
                zn = [_dot_nt(qs[hh], kv(k_ref, j + 1)) for hh in range(2)]
                dan = [_dot_nt(dos[hh], kv(v_ref, j + 1)) for hh in range(2)]
            else:
                msk = lax.broadcasted_iota(jnp.int32, (T, T), 1) < lax.broadcasted_iota(jnp.int32, (T, T), 0)
            adds = flush(jp)
            zs = [zbuf[hh] for hh in range(2)]
            sigs, cls = [], []
            for hh in range(2):
                e = jnp.exp(-jnp.abs(zs[hh]))
                sp = jnp.maximum(zs[hh], 0.0) + jnp.log(1.0 + e)
                r = 1.0 / (1.0 + e)
                sigs.append(jnp.where(zs[hh] >= 0, r, e * r))
                if diagonal:
                    sp = jnp.where(msk, sp, 0.0)
                cls.append(_run_sum(sp, rev2))
            pgs, gs = [], []
            for hh in range(2):
                csl = slice(hh * 128, (hh + 1) * 128)
                cs = jnp.sum(jnp.where(lane == j, cs_ref[:, csl], 0.0), axis=-1, keepdims=True)
                a = jnp.exp(zs[hh] - cls[hh] - cs)
                if diagonal:
                    a = jnp.where(msk, a, 0.0)
                abuf_new = a.astype(BF16)
                g = a * dabuf[hh]
                gs.append((g, abuf_new))
                pgs.append(_run_sum(g, fwd2))
            for hh in range(2):
                g, abuf_new = gs[hh]
                cg = run[hh]
                dz = g - sigs[hh] * (cg + pgs[hh])
                if diagonal:
                    dz = jnp.where(msk, dz, 0.0)
                run[hh] = cg + pgs[hh][:, T - 1:T]
                dzbuf[hh] = dz.astype(BF16)
                abuf[hh] = abuf_new
            apply(jp, adds)
            if not diagonal:
                for hh in range(2):
                    zbuf[hh] = zn[hh]
                    dabuf[hh] = dan[hh]

        def step(j, carry):
            block(j, False)
            return carry

        lax.fori_loop(0, i, step, 0)
        block(i, True)
        apply(i, flush(i))
        dq_ref[...] = (jnp.where(lane < 64, dq_acc[0], dq_acc[1]) * SB_SCALE).astype(dq_ref.dtype)

        @pl.when(i == NQ - 1)
        def _():
            dk_ref[...] = dk_acc[...].astype(dk_ref.dtype)
            dv_ref[...] = dv_acc[...].astype(dv_ref.dtype)

    W = SB_HEADS * SB_DIM
    return pl.pallas_call(
        body, name="sb_bwd", grid=(SB_HEADS // 2, NQ),
        in_specs=[pl.BlockSpec((T, 128), lambda h, i: (i, qb + h)),
                  pl.BlockSpec((S, 128), lambda h, i: (0, kb + h)),
                  pl.BlockSpec((S, 128), lambda h, i: (0, vb + h)),
                  pl.BlockSpec((T, 256), lambda h, i: (i, h)),
                  pl.BlockSpec((T, 128), lambda h, i: (i, h)),
                  pl.BlockSpec((2, 2 * T, T), lambda h, i: (0, 0, 0))],
        out_specs=[pl.BlockSpec((T, 128), lambda h, i: (i, h)),
                   pl.BlockSpec((S, 128), lambda h, i: (0, h)),
                   pl.BlockSpec((S, 128), lambda h, i: (0, h))],
        out_shape=[jax.ShapeDtypeStruct((S, W), BF16)] * 3,
        scratch_shapes=[pltpu.VMEM((S, 128), F32), pltpu.VMEM((S, 128), F32), pltpu.VMEM((2, T, 128), F32),
                        pltpu.VMEM((2, T, 1), F32), pltpu.VMEM((2, T, T), F32), pltpu.VMEM((2, T, T), F32),
                        pltpu.VMEM((2, T, T), BF16), pltpu.VMEM((2, T, T), BF16)],
        compiler_params=_cparams(("parallel", "arbitrary")),
    )(p, p, p, carries, dy, tri)


def _exchange(srcs, out_shapes, src_slice, dst_slice, name):
    n = len(srcs)

    def body(*refs):
        ins, outs = refs[:n], refs[n:2 * n]
        send_sems, recv_sems, loc_sems = refs[2 * n:]
        x, y, c = lax.axis_index("x"), lax.axis_index("y"), lax.axis_index("c")
        me = 4 * x + 2 * y + c
        local = [pltpu.make_async_copy(src_slice(t, ins[t], me), dst_slice(t, outs[t], me), loc_sems.at[t])
                 for t in range(n)]
        for cp in local:
            cp.start()
        sends, recvs = [], []
        for k in (1, 2, 4, 6, 3, 5, 7):
            px = 1 - x if k & 4 else x
            py = 1 - y if k & 2 else y
            pc = 1 - c if k & 1 else c
            peer = 4 * px + 2 * py + pc
            for t in range(n):
                s = t * 7 + k - 1
                sends.append(pltpu.make_async_remote_copy(
                    src_ref=src_slice(t, ins[t], peer), dst_ref=dst_slice(t, outs[t], me),
                    send_sem=send_sems.at[s], recv_sem=recv_sems.at[s],
                    device_id=(px, py, pc), device_id_type=pl.DeviceIdType.MESH))
                recvs.append(pltpu.make_async_remote_copy(
                    src_ref=src_slice(t, ins[t], me), dst_ref=dst_slice(t, outs[t], peer),
                    send_sem=send_sems.at[s], recv_sem=recv_sems.at[s],
                    device_id=(px, py, pc), device_id_type=pl.DeviceIdType.MESH))
        for cp in sends:
            cp.start()
        for cp in recvs:
            cp.wait_recv()
        for cp in sends:
            cp.wait_send()
        for cp in local:
            cp.wait()

    anyspec = pl.BlockSpec(memory_space=pl.ANY)
    return pl.pallas_call(
        body, name=name, in_specs=[anyspec] * n, out_specs=[anyspec] * n,
        out_shape=[jax.ShapeDtypeStruct(s, d) for s, d in out_shapes],
        scratch_shapes=[pltpu.SemaphoreType.DMA((7 * n,)), pltpu.SemaphoreType.DMA((7 * n,)),
                        pltpu.SemaphoreType.DMA((n,))],
    )(*srcs)


def _all_gather_lead(xs, name):
    return _exchange(
        xs, [((N_DEV,) + x.shape, x.dtype) for x in xs],
        lambda t, ref, peer: ref, lambda t, ref, who: ref.at[who], name)


def _all_to_all_lead(xs, name):
    return _exchange(
        xs, [(x.shape, x.dtype) for x in xs],
        lambda t, ref, peer: ref.at[peer], lambda t, ref, who: ref.at[who], name)


_W_AXIS = {"w_in": 1, "w_ret_out": 0, "w_sb_out": 0, "w_mix_out": 0, "w_up": 1, "w_down": 0}
_W_NAMES = tuple(_W_AXIS)


def _shard_slice(axis, width):
    def f(ref, who):
        start = pl.multiple_of(who * width, width)
        if axis == 0:
            return ref.at[:, pl.ds(start, width), :]
        return ref.at[:, :, pl.ds(start, width)]
    return f


def _gather_weights(shards):
    xs = [shards[nm] for nm in _W_NAMES]
    shapes, dsts = [], []
    for nm, x in zip(_W_NAMES, xs):
        ax = _W_AXIS[nm]
        full = list(x.shape)
        full[1 + ax] *= N_DEV
        shapes.append((tuple(full), x.dtype))
        dsts.append(_shard_slice(ax, x.shape[1 + ax]))
    return _exchange(xs, shapes, lambda t, ref, peer: ref, lambda t, ref, who: dsts[t](ref, who), "gather_weights")


def _scatter_grads(grads, layer_tag):
    xs = [grads[nm] for nm in _W_NAMES]
    shapes, srcs = [], []
    for nm, x in zip(_W_NAMES, xs):
        ax = _W_AXIS[nm]
        sh = list(x.shape)
        sh[ax] //= N_DEV
        shapes.append(((N_DEV,) + tuple(sh), x.dtype))
        width = sh[ax]

        def src(ref, who, ax=ax, width=width):
            start = pl.multiple_of(who * width, width)
            return ref.at[pl.ds(start, width), :] if ax == 0 else ref.at[:, pl.ds(start, width)]
        srcs.append(src)
    return _exchange(xs, shapes, lambda t, ref, peer: srcs[t](ref, peer), lambda t, ref, who: ref.at[who],
                     "scatter_grads")


def _adam(parts, w, m, v, name, tr=256):
    P, R, C = parts.shape
    tr = min(tr, R)
    assert R % tr == 0
    bc1 = 1.0 / (1.0 - ADAM_B1 ** ADAM_STEP)
    bc2 = 1.0 / (1.0 - ADAM_B2 ** ADAM_STEP)

    def body(p_ref, w_ref, m_ref, v_ref, g_out, d_out, m_out, v_out):
        g = p_ref[0].astype(F32)
        for s in range(1, P):
            g = g + p_ref[s].astype(F32)
        mm = ADAM_B1 * m_ref[...] + (1.0 - ADAM_B1) * g
        vv = ADAM_B2 * v_ref[...] + (1.0 - ADAM_B2) * jnp.square(g)
        g_out[...] = g
        m_out[...] = mm
        v_out[...] = vv
        d_out[...] = -ADAM_LR * ((mm * bc1) / (jnp.sqrt(vv * bc2) + ADAM_EPS) + ADAM_WD * w_ref[...])

    spec = pl.BlockSpec((tr, C), lambda i: (i, 0))
    return pl.pallas_call(
        body, name=name, grid=(R // tr,),
        in_specs=[pl.BlockSpec((P, tr, C), lambda i: (0, i, 0)), spec, spec, spec],
        out_specs=[spec] * 4, out_shape=[jax.ShapeDtypeStruct((R, C), F32)] * 4,
        compiler_params=_cparams(("parallel",)),
    )(parts, w, m, v)


def _mod_partial(cact_all, w_ada_l, b_ada_l):
    def body(c_ref, w_ref, b_ref, o_ref):
        o_ref[...] = _dot(c_ref[...].astype(BF16), w_ref[...].astype(BF16)) + b_ref[...]

    return pl.pallas_call(
        body, name="mod_partial", out_shape=jax.ShapeDtypeStruct((cact_all.shape[0], w_ada_l.shape[1]), F32),
        compiler_params=pltpu.CompilerParams(vmem_limit_bytes=VMEM_LIMIT),
    )(cact_all, w_ada_l, b_ada_l)


def _ada_grad(cact_t, dmod):
    D, n = cact_t.shape[0], dmod.shape[1]

    def body(c_ref, d_ref, o_ref):
        ct = c_ref[...].astype(BF16).astype(F32)
        dm = d_ref[...].astype(BF16).astype(F32)
        acc = ct[:, 0:1] * dm[0:1, :]
        for b in range(1, N_DEV):
            acc = acc + ct[:, b:b + 1] * dm[b:b + 1, :]
        o_ref[0] = acc

    return pl.pallas_call(
        body, name="ada_grad", out_shape=jax.ShapeDtypeStruct((1, D, n), F32),
        compiler_params=pltpu.CompilerParams(vmem_limit_bytes=VMEM_LIMIT),
    )(cact_t, dmod)


def _norm_mod(x, r, gv, sh):
    return x * r * gv + sh


def _silu(x):
    return x * _sigmoid(x)


def _layer_fwd(x0, mod, gn1, gn2, W, rc, tri):
    S = x0.shape[0]
    sh1, sc1, g1m, sh2, sc2, g2m = [mod[i:i + 1] for i in range(N_MOD)]
    gv1 = gn1 * (1.0 + sc1)
    gv2 = gn2 * (1.0 + sc2)
    (r1,) = _ew(lambda x: (lax.rsqrt(jnp.mean(x * x, axis=-1, keepdims=True) + EPS),), [x0], outs=[('col', F32)],
                name="row_rstd")
    (p,) = _mm(x0, W["w_in"], tm=1024, tn=1024, tk=512, a_ex=[(r1, 'm'), (gv1, 'k'), (sh1, 'k')], pro=_norm_mod,
               outs=(BF16,), name="mm_in")
    yret, rstate = _ret_fwd(p, rc)
    ysb, sbc = _sb_fwd(p, tri)
    rg = (p, O_RG)
    (ya,) = _mm_cols(yret, W["w_ret_out"], a_cols=[rg], pro=lambda yr, g: _silu(g.astype(F32)) * yr.astype(F32),
                     outs=(BF16,), name="mm_ret_out")
    yb, mg = _mm_cols(ysb, W["w_sb_out"], o_cols=[(ya, 0), (p, O_GA), (p, O_GB)],
                      epi=lambda acc, a, ga, gb: (acc, _sigmoid(ga.astype(F32)) * a.astype(F32)
                                                  + _sigmoid(gb.astype(F32)) * acc),
                      outs=(BF16, BF16), name="mm_sb_out")
    mo, x1 = _mm(mg, W["w_mix_out"], tm=1024, tn=1024, tk=512, o_ex=[(x0, 'o'), (g1m, 'n')],
                 epi=lambda acc, x, g: (acc, x + g * acc), outs=(BF16, F32), name="mm_mix_out")
    (r2,) = _ew(lambda x: (lax.rsqrt(jnp.mean(x * x, axis=-1, keepdims=True) + EPS),), [x1], outs=[('col', F32)],
                name="row_rstd")
    (act,) = _mm(x1, W["w_up"], tm=1024, tn=1024, tk=512, a_ex=[(r2, 'm'), (gv2, 'k'), (sh2, 'k')], pro=_norm_mod,
                 epi=lambda acc: (jnp.maximum(acc, 0.0),), outs=(BF16,), name="mm_up")
    dn, x2 = _mm(act, W["w_down"], tm=1024, tn=1024, tk=512, pro=lambda a: jnp.square(a.astype(F32)),
                 o_ex=[(x1, 'o'), (g2m, 'n')], epi=lambda acc, x, g: (acc, x + g * acc), outs=(BF16, F32),
                 name="mm_down")
    saved = dict(x0=x0, r1=r1, p=p, yret=yret, rstate=rstate, ysb=ysb, sbc=sbc, ya=ya, yb=yb, mg=mg, mo=mo, x1=x1, r2=r2,
                 act=act, dn=dn, gv1=gv1, gv2=gv2, mod=mod, gn1=gn1, gn2=gn2)
    return x2, saved


def _mm_cols(a, b, *, a_cols=(), o_cols=(), pro=None, epi=None, outs, name, ta=False, tb=False):
    if ta:
        K, M = a.shape
    else:
        M, K = a.shape
    N = b.shape[0] if tb else b.shape[1]
    tm, tn, tk = min(1024, M), min(1024, N), min(512, K)
    nk = K // tk
    dims = (((0 if ta else 1,), (1 if tb else 0,)), ((), ()))
    in_specs = [
        pl.BlockSpec((tk, tm), lambda i, j, k: (k, i)) if ta else pl.BlockSpec((tm, tk), lambda i, j, k: (i, k)),
        pl.BlockSpec((tn, tk), lambda i, j, k: (j, k)) if tb else pl.BlockSpec((tk, tn), lambda i, j, k: (k, j)),
    ]
    args = [a, b]
    for arr, off in a_cols:
        if ta:
            assert off % tm == 0
            in_specs.append(pl.BlockSpec((tk, tm), lambda i, j, k, o=off // tm: (k, o + i)))
        else:
            assert off % tk == 0
            in_specs.append(pl.BlockSpec((tm, tk), lambda i, j, k, o=off // tk: (i, o + k)))
        args.append(arr)
    for arr, off in o_cols:
        assert off % tn == 0
        in_specs.append(pl.BlockSpec((tm, tn), lambda i, j, k, o=off // tn: (i, o + j)))
        args.append(arr)
    na, no, nout = len(a_cols), len(o_cols), len(outs)

    def body(*refs):
        a_ref, b_ref = refs[0], refs[1]
        aex = refs[2:2 + na]
        oex = refs[2 + na:2 + na + no]
        out_refs = refs[2 + na + no:2 + na + no + nout]
        acc = refs[-1]
        k = pl.program_id(2)

        @pl.when(k == 0)
        def _():
            acc[...] = jnp.zeros_like(acc)

        at = a_ref[...]
        if pro is not None:
            at = pro(at, *[r[...] for r in aex])
        acc[...] += lax.dot_general(at.astype(BF16), b_ref[...].astype(BF16), dims, preferred_element_type=F32)

        @pl.when(k == nk - 1)
        def _():
            res = acc[...]
            vals = epi(res, *[r[...] for r in oex]) if epi is not None else (res,)
            for o_ref, v in zip(out_refs, vals):
                o_ref[...] = v.astype(o_ref.dtype)

    return pl.pallas_call(
        body, name=name, grid=(M // tm, N // tn, nk), in_specs=in_specs,
        out_specs=[pl.BlockSpec((tm, tn), lambda i, j, k: (i, j)) for _ in outs],
        out_shape=[jax.ShapeDtypeStruct((M, N), dt) for dt in outs],
        scratch_shapes=[pltpu.VMEM((tm, tn), F32)],
        compiler_params=_cparams(("parallel", "parallel", "arbitrary")),
    )(*args)


def _norm_bwd(dh, x, r, dres, gv, gn, extra_rows=(), extra_vecs=(), extra_fn=None, extra_outs=(), name="norm_bwd"):
    D = x.shape[1]
    ne = len(extra_rows)

    def fn(dh_t, x_t, dres_t, *rest):
        er, rest = rest[:ne], rest[ne:]
        gv_t = rest[0]
        ev, r_t = rest[1:-1], rest[-1]
        xh = x_t * r_t
        dxh = dh_t * gv_t
        dx = r_t * (dxh - xh * jnp.mean(dxh * xh, axis=-1, keepdims=True)) + dres_t
        base = (dx, dh_t, dh_t * xh)
        if extra_fn is None:
            return base
        return base + tuple(extra_fn(dx, *er, *ev))

    return _ew(fn, [dh, x, dres] + list(extra_rows), vecs=[gv] + list(extra_vecs), cols=[r],
               outs=[('row', D, F32), ('sum', D), ('sum', D)] + list(extra_outs), name=name)


def _layer_bwd(dx2, sv, W, rc, tri):
    mod = sv['mod']
    sh1, sc1, g1m, sh2, sc2, g2m = [mod[i:i + 1] for i in range(N_MOD)]
    D = D_MODEL
    p = sv['p']
    d_g2m, d_dn = _ew(lambda dx, dn, g: (dx * dn.astype(F32), dx * g), [dx2, sv['dn']], vecs=[g2m],
                      outs=[('sum', D), ('row', D, BF16)], name="gate_bwd")
    (d_up,) = _mm(d_dn, W["w_down"], tb=True, tm=1024, tn=1024, tk=512, o_ex=[(sv['act'], 'o')],
                  epi=lambda acc, a: (acc * 2.0 * a.astype(F32),), outs=(BF16,), name="mm_down_dx")
    (gw_down,) = _mm(sv['act'], d_dn, ta=True, tm=1024, tn=1024, tk=512, pro=lambda a: jnp.square(a.astype(F32)),
                     outs=(BF16,), name="mm_down_dw")
    (gw_up,) = _mm(sv['x1'], d_up, ta=True, tm=1024, tn=1024, tk=512,
                   a_ex=[(sv['r2'], 'k'), (sv['gv2'].reshape(1, D), 'm'), (sh2, 'm')], pro=_norm_mod,
                   outs=(BF16,), name="mm_up_dw")
    (d_h2,) = _mm(d_up, W["w_up"], tb=True, tm=1024, tn=1024, tk=512, outs=(F32,), name="mm_up_dx")
    dx1, d_sh2, s_h2, d_g1m, d_mo = _norm_bwd(
        d_h2, sv['x1'], sv['r2'], dx2, sv['gv2'], sv['gn2'],
        extra_rows=[sv['mo']], extra_vecs=[g1m],
        extra_fn=lambda dx, mo, g: (dx * mo.astype(F32), dx * g),
        extra_outs=[('sum', D), ('row', D, BF16)], name="norm_bwd_mlp")
    d_sc2 = sv['gn2'] * s_h2
    d_gn2 = (1.0 + sc2) * s_h2
    def mix_epi(acc, ya, yb, ga, gb):
        sa, sb = _sigmoid(ga.astype(F32)), _sigmoid(gb.astype(F32))
        return (acc * sa, acc * sb, acc * ya.astype(F32) * sa * (1.0 - sa), acc * yb.astype(F32) * sb * (1.0 - sb))

    d_ya, d_yb, d_ga, d_gb = _mm_cols(d_mo, W["w_mix_out"], tb=True,
                                      o_cols=[(sv['ya'], 0), (sv['yb'], 0), (p, O_GA), (p, O_GB)], epi=mix_epi,
                                      outs=(BF16,) * 4, name="mm_mix_dx")
    (gw_mix,) = _mm(sv['mg'], d_mo, ta=True, tm=1024, tn=1024, tk=512, outs=(BF16,), name="mm_mix_dw")

    def ro_epi(acc, g, yr):
        gf = g.astype(F32)
        s = _sigmoid(gf)
        return (acc * yr.astype(F32) * s * (1.0 + gf * (1.0 - s)), acc * gf * s)

    d_rg, d_yret = _mm_cols(d_ya, W["w_ret_out"], tb=True, o_cols=[(p, O_RG), (sv['yret'], 0)], epi=ro_epi,
                            outs=(BF16, BF16), name="mm_ret_dx")
    (gw_ro,) = _mm_cols(sv['yret'], d_ya, ta=True, a_cols=[(p, O_RG)],
                        pro=lambda yr, g: _silu(g.astype(F32)) * yr.astype(F32), outs=(BF16,), name="mm_ret_dw")
    (d_ysb,) = _mm(d_yb, W["w_sb_out"], tb=True, tm=1024, tn=1024, tk=512, outs=(BF16,), name="mm_sb_dx")
    (gw_so,) = _mm(sv['ysb'], d_yb, ta=True, tm=1024, tn=1024, tk=512, outs=(BF16,), name="mm_sb_dw")
    d_sq, d_sk, d_sv = _sb_bwd(p, sv['sbc'], d_ysb, tri)
    d_rq, d_rk, d_rv = _ret_bwd(p, sv['rstate'], d_yret, rc)
    dp = jnp.concatenate([d_rq, d_rk, d_rv, d_rg, d_sq, d_sk, d_sv, d_ga, d_gb], axis=1)
    (gw_in,) = _mm(sv['x0'], dp, ta=True, tm=1024, tn=1024, tk=512,
                   a_ex=[(sv['r1'], 'k'), (sv['gv1'].reshape(1, D), 'm'), (sh1, 'm')], pro=_norm_mod,
                   outs=(BF16,), name="mm_in_dw")
    (d_h,) = _mm(dp, W["w_in"], tb=True, tm=1024, tn=1024, tk=512, outs=(F32,), name="mm_in_dx")
    dx0, d_sh1, s_h1 = _norm_bwd(d_h, sv['x0'], sv['r1'], dx1, sv['gv1'], sv['gn1'], name="norm_bwd_mix")
    d_sc1 = sv['gn1'] * s_h1
    d_gn1 = (1.0 + sc1) * s_h1
    d_mod = jnp.concatenate([d_sh1, d_sc1, d_g1m, d_sh2, d_sc2, d_g2m], axis=1)
    gw = dict(w_in=gw_in, w_ret_out=gw_ro, w_sb_out=gw_so, w_mix_out=gw_mix, w_up=gw_up, w_down=gw_down)
    return dx0, gw, d_mod, d_gn1, d_gn2


def kernel(x, c, norm_mix_g, w_in, w_ret_out, w_sb_out, w_mix_out, norm_mlp_g, w_up, w_down, w_ada, b_ada, final_g, loss_target, m_norm_mix_g, m_w_in, m_w_ret_out, m_w_sb_out, m_w_mix_out, m_norm_mlp_g, m_w_up, m_w_down, m_w_ada, m_b_ada, m_final_g, v_norm_mix_g, v_w_in, v_w_ret_out, v_w_sb_out, v_w_mix_out, v_norm_mlp_g, v_w_up, v_w_down, v_w_ada, v_b_ada, v_final_g):
    S, D = x.shape[1], x.shape[2]
    x0 = x.reshape(S, D)
    tgt = loss_target.reshape(S, D)
    me = 4 * lax.axis_index("x") + 2 * lax.axis_index("y") + lax.axis_index("c")
    wts = dict(w_in=w_in, w_ret_out=w_ret_out, w_sb_out=w_sb_out, w_mix_out=w_mix_out, w_up=w_up, w_down=w_down)
    mts = dict(w_in=m_w_in, w_ret_out=m_w_ret_out, w_sb_out=m_w_sb_out, w_mix_out=m_w_mix_out, w_up=m_w_up, w_down=m_w_down)
    vts = dict(w_in=v_w_in, w_ret_out=v_w_ret_out, w_sb_out=v_w_sb_out, w_mix_out=v_w_mix_out, w_up=v_w_up, w_down=v_w_down)
    rc = _ret_consts(S)
    tri = _tri()

    shards = {}
    for nm in _W_NAMES:
        w = wts[nm]
        (wb,) = _ew(lambda t: (t,), [w.reshape(-1, w.shape[-1])], outs=[('row', w.shape[-1], BF16)], name="cast_bf16")
        shards[nm] = wb.reshape(w.shape)
    full = dict(zip(_W_NAMES, _gather_weights(shards)))

    (cact,) = _ew(lambda t: (_silu(t),), [jnp.pad(c, ((0, 7), (0, 0)))], outs=[('row', D, F32)], name="silu_c")
    (cact_all,) = _all_gather_lead([cact[0:1]], "gather_c")
    cact_all = cact_all.reshape(N_DEV, D)
    cact16 = jnp.pad(cact_all, ((0, 8), (0, 0)))
    n_ada = w_ada.shape[2]
    b_loc = lax.dynamic_slice_in_dim(b_ada, me * n_ada, n_ada, axis=1)
    mods = [_mod_partial(cact16, w_ada[l], b_loc[l:l + 1])[:N_DEV] for l in range(DEPTH)]
    modp = jnp.stack(mods, axis=1)
    (modr,) = _all_to_all_lead([modp], "scatter_mod")
    mod_full = jnp.transpose(modr, (1, 0, 2)).reshape(DEPTH, N_MOD, D)

    xs = x0
    saved = []
    for l in range(DEPTH):
        W = {nm: full[nm][l] for nm in _W_NAMES}
        xs, sv = _layer_fwd(xs, mod_full[l], norm_mix_g[l:l + 1], norm_mlp_g[l:l + 1], W, rc, tri)
        sv['W'] = W
        saved.append(sv)

    fg = final_g.reshape(1, D)

    def head(xt, tg, g):
        r = lax.rsqrt(jnp.mean(xt * xt, axis=-1, keepdims=True) + EPS)
        xh = xt * r
        e = xh * g - tg
        dy = e * (1.0 / D)
        dxh = dy * g
        dx = r * (dxh - xh * jnp.mean(dxh * xh, axis=-1, keepdims=True))
        return dx, dy * xh, 0.5 * e * e * (1.0 / D)

    dxs, d_fg, loss_cols = _ew(head, [xs, tgt], vecs=[fg], outs=[('row', D, F32), ('sum', D), ('sum', D)], name="loss_head")

    small = [None] * DEPTH
    parts = [None] * DEPTH
    for l in reversed(range(DEPTH)):
        sv = saved[l]
        dxs, gw, d_mod, d_gn1, d_gn2 = _layer_bwd(dxs, sv, sv['W'], rc, tri)
        small[l] = (d_mod, d_gn1, d_gn2)
        parts[l] = dict(zip(_W_NAMES, _scatter_grads(gw, l)))
    grad_x = dxs.reshape(1, S, D)

    pack = jnp.concatenate([small[l][0] for l in range(DEPTH)] + [small[l][1] for l in range(DEPTH)]
                           + [small[l][2] for l in range(DEPTH)] + [d_fg, loss_cols], axis=1)
    (packs,) = _all_gather_lead([pack], "gather_small")
    packs = packs.reshape(N_DEV, -1)
    o = 0
    dmod_all = []
    for l in range(DEPTH):
        dmod_all.append(packs[:, o:o + N_MOD * D]); o += N_MOD * D
    gn1_parts = packs[:, o:o + DEPTH * D].reshape(N_DEV, DEPTH, D); o += DEPTH * D
    gn2_parts = packs[:, o:o + DEPTH * D].reshape(N_DEV, DEPTH, D); o += DEPTH * D
    fg_parts = packs[:, o:o + D].reshape(N_DEV, 1, D); o += D
    loss_parts = packs[:, o:o + D]
    (loss_sum,) = _ew(lambda t: (t,), [loss_parts], outs=[('sum', D)], name="loss_sum")
    loss = jnp.sum(loss_sum)

    res = {}
    for nm in _W_NAMES:
        per = [_adam(parts[l][nm], wts[nm][l], mts[nm][l], vts[nm][l], "adam") for l in range(DEPTH)]
        res[nm] = [jnp.stack([per[l][i] for l in range(DEPTH)]) for i in range(4)]
    res["norm_mix_g"] = _adam(gn1_parts, norm_mix_g, m_norm_mix_g, v_norm_mix_g, "adam")
    res["norm_mlp_g"] = _adam(gn2_parts, norm_mlp_g, m_norm_mlp_g, v_norm_mlp_g, "adam")
    fgr = _adam(fg_parts, fg, m_final_g.reshape(1, D), v_final_g.reshape(1, D), "adam")
    res["final_g"] = [t.reshape(D) for t in fgr]
    bparts = jnp.stack(dmod_all, axis=1)
    res["b_ada"] = _adam(bparts, b_ada, m_b_ada, v_b_ada, "adam")
    cact_t = cact_all.T
    ada = []
    for l in range(DEPTH):
        dm_loc = lax.dynamic_slice_in_dim(dmod_all[l], me * n_ada, n_ada, axis=1)
        ada.append(_adam(_ada_grad(cact_t, dm_loc), w_ada[l], m_w_ada[l], v_w_ada[l], "adam"))
    res["w_ada"] = [jnp.stack([ada[l][i] for l in range(DEPTH)]) for i in range(4)]

    order = ['norm_mix_g', 'w_in', 'w_ret_out', 'w_sb_out', 'w_mix_out', 'norm_mlp_g', 'w_up', 'w_down', 'w_ada', 'b_ada', 'final_g']
    out = [loss, grad_x]
    for i in range(4):
        out += [res[nm][i] for nm in order]
    return tuple(out)
```

```python
import functools
import math

import jax
import jax.numpy as jnp
import numpy as np
from jax import lax
from jax.experimental import pallas as pl
from jax.experimental.pallas import tpu as pltpu

F32 = jnp.float32
BF16 = jnp.bfloat16

N_DEV = 8
D_MODEL = 1024
DEPTH = 2
RET_HEADS = 4
RET_QK = 256
RET_V = 512
RET_CHUNK = 128
ROPE_BASE = 10000.0
SB_HEADS = 16
SB_DIM = 64
D_FF = 4096
N_MOD = 6
EPS = 1e-6
GN_EPS = 1e-5
O_RQ, O_RK, O_RV, O_RG, O_SQ, O_SK, O_SV, O_GA, O_GB = 0, 1024, 2048, 4096, 6144, 7168, 8192, 9216, 10240
IN_W = 11264

ADAM_LR, ADAM_B1, ADAM_B2, ADAM_EPS, ADAM_WD, ADAM_STEP = 0.001, 0.9, 0.999, 1e-08, 0.01, 10

VMEM_LIMIT = 56 * 1024 * 1024


def _cparams(sem):
    return pltpu.CompilerParams(dimension_semantics=sem, vmem_limit_bytes=VMEM_LIMIT)


def _mm(a, b, *, ta=False, tb=False, tm=512, tn=512, tk=512, a_ex=(), pro=None, o_ex=(), epi=None,
        outs=(F32,), name, deps=()):
    if ta:
        K, M = a.shape
    else:
        M, K = a.shape
    N = b.shape[0] if tb else b.shape[1]
    tm, tn, tk = min(tm, M), min(tn, N), min(tk, K)
    assert M % tm == 0 and N % tn == 0 and K % tk == 0, (name, M, N, K, tm, tn, tk)
    nk = K // tk
    in_specs = [
        pl.BlockSpec((tk, tm), lambda i, j, k: (k, i)) if ta else pl.BlockSpec((tm, tk), lambda i, j, k: (i, k)),
        pl.BlockSpec((tn, tk), lambda i, j, k: (j, k)) if tb else pl.BlockSpec((tk, tn), lambda i, j, k: (k, j)),
    ]
    args = [a, b]
    for arr, kind in a_ex:
        if kind == 'a':
            in_specs.append(in_specs[0])
        elif kind == 'k':
            in_specs.append(pl.BlockSpec((tk, 1), lambda i, j, k: (k, 0)) if ta
                            else pl.BlockSpec((1, tk), lambda i, j, k: (0, k)))
        else:
            in_specs.append(pl.BlockSpec((1, tm), lambda i, j, k: (0, i)) if ta
                            else pl.BlockSpec((tm, 1), lambda i, j, k: (i, 0)))
        args.append(arr)
    for arr, kind in o_ex:
        if kind == 'o':
            in_specs.append(pl.BlockSpec((tm, tn), lambda i, j, k: (i, j)))
        elif kind == 'n':
            in_specs.append(pl.BlockSpec((1, tn), lambda i, j, k: (0, j)))
        else:
            in_specs.append(pl.BlockSpec((tm, 1), lambda i, j, k: (i, 0)))
        args.append(arr)
    for arr in deps:
        in_specs.append(pl.BlockSpec(memory_space=pl.ANY))
        args.append(arr)
    na, no, nout, nd = len(a_ex), len(o_ex), len(outs), len(deps)
    dims = (((0 if ta else 1,), (1 if tb else 0,)), ((), ()))

    def body(*refs):
        a_ref, b_ref = refs[0], refs[1]
        aex = refs[2:2 + na]
        oex = refs[2 + na:2 + na + no]
        out_refs = refs[2 + na + no + nd:2 + na + no + nd + nout]
        acc = refs[-1]
        k = pl.program_id(2)

        @pl.when(k == 0)
        def _():
            acc[...] = jnp.zeros_like(acc)

        at = a_ref[...]
        if pro is not None:
            at = pro(at, *[r[...] for r in aex])
        acc[...] += lax.dot_general(at.astype(BF16), b_ref[...].astype(BF16), dims, preferred_element_type=F32)

        @pl.when(k == nk - 1)
        def _():
            res = acc[...]
            vals = epi(res, *[r[...] for r in oex]) if epi is not None else (res,)
            for o_ref, v in zip(out_refs, vals):
                o_ref[...] = v.astype(o_ref.dtype)

    res = pl.pallas_call(
        body, name=name, grid=(M // tm, N // tn, nk), in_specs=in_specs,
        out_specs=[pl.BlockSpec((tm, tn), lambda i, j, k: (i, j)) for _ in outs],
        out_shape=[jax.ShapeDtypeStruct((M, N), dt) for dt in outs],
        scratch_shapes=[pltpu.VMEM((tm, tn), F32)],
        compiler_params=_cparams(("parallel", "parallel", "arbitrary")),
    )(*args)
    return res


def _ew(fn, rows, vecs=(), cols=(), outs=(), tr=256, name=None):
    S = rows[0].shape[0]
    tr = min(tr, S)
    assert S % tr == 0
    in_specs, args = [], []
    for r in rows:
        in_specs.append(pl.BlockSpec((tr, r.shape[1]), lambda i: (i, 0)))
        args.append(r)
    for v in vecs:
        in_specs.append(pl.BlockSpec((1, v.shape[1]), lambda i: (0, 0)))
        args.append(v)
    for c in cols:
        in_specs.append(pl.BlockSpec((tr, 1), lambda i: (i, 0)))
        args.append(c)
    out_specs, out_shape = [], []
    for o in outs:
        if o[0] == 'row':
            out_specs.append(pl.BlockSpec((tr, o[1]), lambda i: (i, 0)))
            out_shape.append(jax.ShapeDtypeStruct((S, o[1]), o[2]))
        elif o[0] == 'sum':
            out_specs.append(pl.BlockSpec((1, o[1]), lambda i: (0, 0)))
            out_shape.append(jax.ShapeDtypeStruct((1, o[1]), F32))
        else:
            out_specs.append(pl.BlockSpec((tr, 1), lambda i: (i, 0)))
            out_shape.append(jax.ShapeDtypeStruct((S, 1), o[1]))
    nin = len(args)

    def body(*refs):
        i = pl.program_id(0)
        vals = fn(*[r[...] for r in refs[:nin]])
        for o, o_ref, v in zip(outs, refs[nin:], vals):
            if o[0] == 'sum':
                @pl.when(i == 0)
                def _():
                    o_ref[...] = jnp.zeros_like(o_ref)
                o_ref[...] += jnp.sum(v.astype(F32), axis=0, keepdims=True)
            else:
                o_ref[...] = v.astype(o_ref.dtype)

    return pl.pallas_call(
        body, name=name, grid=(S // tr,), in_specs=in_specs, out_specs=out_specs, out_shape=out_shape,
        compiler_params=_cparams(("arbitrary",)),
    )(*args)


def _sigmoid(x):
    return 1.0 / (1.0 + jnp.exp(-x))


def _ret_consts(S):
    h = np.arange(RET_HEADS, dtype=np.float64)
    log_gamma = np.log1p(-np.power(2.0, -5.0 - h))
    idx = np.arange(RET_CHUNK, dtype=np.float64)
    rel = idx[:, None] - idx[None, :]
    decay = np.where(rel >= 0, np.exp(np.maximum(rel, 0.0) * log_gamma[:, None, None]), 0.0)
    xi = np.exp((idx + 1.0) * log_gamma[:, None])[:, :, None]
    zeta = np.exp((RET_CHUNK - 1.0 - idx) * log_gamma[:, None])[:, :, None]
    gamma_c = np.exp(RET_CHUNK * log_gamma)[:, None, None]
    half = RET_QK // 2
    inv_freq = np.power(ROPE_BASE, -np.arange(half, dtype=np.float64) / half).astype(np.float32)
    ang = np.arange(S, dtype=np.float32)[:, None] * inv_freq[None, :]
    f = lambda t: jnp.asarray(t, F32)
    return dict(decay=f(decay), xi=f(xi), zeta=f(zeta), gc=f(gamma_c), cos=f(np.cos(ang)), sin=f(np.sin(ang)))


def _rot(t, cos, sin):
    half = RET_QK // 2
    t1, t2 = t[:, :half], t[:, half:]
    return jnp.concatenate([t1 * cos - t2 * sin, t1 * sin + t2 * cos], axis=-1)


def _rot_inv(t, cos, sin):
    half = RET_QK // 2
    t1, t2 = t[:, :half], t[:, half:]
    return jnp.concatenate([t1 * cos + t2 * sin, t2 * cos - t1 * sin], axis=-1)


_NT = (((1,), (1,)), ((), ()))
_TN = (((0,), (0,)), ((), ()))


def _dot(a, b):
    return jnp.dot(a, b, preferred_element_type=F32)


def _dot_nt(a, b):
    return lax.dot_general(a, b, _NT, preferred_element_type=F32)


def _dot_tn(a, b):
    return lax.dot_general(a, b, _TN, preferred_element_type=F32)


def _ret_in_specs(C, rev, NC):
    n_of = (lambda n: NC - 1 - n) if rev else (lambda n: n)
    qb, vb = O_RQ // RET_QK, O_RV // RET_V
    kb = O_RK // RET_QK
    return [
        pl.BlockSpec((C, RET_QK), lambda h, n: (n_of(n), qb + h)),
        pl.BlockSpec((C, RET_QK), lambda h, n: (n_of(n), kb + h)),
        pl.BlockSpec((C, RET_V), lambda h, n: (n_of(n), vb + h)),
        pl.BlockSpec((C, RET_QK // 2), lambda h, n: (n_of(n), 0)),
        pl.BlockSpec((C, RET_QK // 2), lambda h, n: (n_of(n), 0)),
        pl.BlockSpec((1, C, C), lambda h, n: (h, 0, 0)),
        pl.BlockSpec((1, C, 1), lambda h, n: (h, 0, 0)),
        pl.BlockSpec((1, C, 1), lambda h, n: (h, 0, 0)),
        pl.BlockSpec((1, 1, 1), lambda h, n: (h, 0, 0)),
    ]


def _ret_fwd(p, rc):
    S = p.shape[0]
    C = RET_CHUNK
    NC = S // C

    def body(q_ref, k_ref, v_ref, cos_ref, sin_ref, dec_ref, xi_ref, zeta_ref, gc_ref, y_ref, rs_ref, r_acc):
        n = pl.program_id(1)

        @pl.when(n == 0)
        def _():
            r_acc[...] = jnp.zeros_like(r_acc)

        cos, sin = cos_ref[...], sin_ref[...]
        q = _rot(q_ref[...].astype(F32), cos, sin).astype(BF16)
        kf = _rot(k_ref[...].astype(F32), cos, sin) * (RET_QK ** -0.5)
        k = kf.astype(BF16)
        v = v_ref[...]
        r = r_acc[...]
        rb = r.astype(BF16)
        rs_ref[0, 0] = rb
        s = (_dot_nt(q, k) * dec_ref[0]).astype(BF16)
        o = _dot(s, v) + _dot(q, rb) * xi_ref[0]
        mu = jnp.mean(o, axis=-1, keepdims=True)
        var = jnp.mean(jnp.square(o - mu), axis=-1, keepdims=True)
        y_ref[...] = ((o - mu) * lax.rsqrt(var + GN_EPS)).astype(y_ref.dtype)
        kz = (kf * zeta_ref[0]).astype(BF16)
        r_acc[...] = r * gc_ref[0] + _dot_tn(kz, v)

    return pl.pallas_call(
        body, name="ret_fwd", grid=(RET_HEADS, NC), in_specs=_ret_in_specs(C, False, NC),
        out_specs=[pl.BlockSpec((C, RET_V), lambda h, n: (n, h)),
                   pl.BlockSpec((1, 1, RET_QK, RET_V), lambda h, n: (h, n, 0, 0))],
        out_shape=[jax.ShapeDtypeStruct((S, RET_HEADS * RET_V), BF16),
                   jax.ShapeDtypeStruct((RET_HEADS, NC, RET_QK, RET_V), BF16)],
        scratch_shapes=[pltpu.VMEM((RET_QK, RET_V), F32)],
        compiler_params=_cparams(("parallel", "arbitrary")),
    )(p, p, p, rc['cos'], rc['sin'], rc['decay'], rc['xi'], rc['zeta'], rc['gc'])


def _ret_bwd(p, rstate, dy, rc):
    S = p.shape[0]
    C = RET_CHUNK
    NC = S // C

    def body(q_ref, k_ref, v_ref, cos_ref, sin_ref, dec_ref, xi_ref, zeta_ref, gc_ref, rs_ref, dy_ref,
             dq_ref, dk_ref, dv_ref, dr_acc):
        t = pl.program_id(1)

        @pl.when(t == 0)
        def _():
            dr_acc[...] = jnp.zeros_like(dr_acc)

        cos, sin = cos_ref[...], sin_ref[...]
        dec, xi, zeta = dec_ref[0], xi_ref[0], zeta_ref[0]
        q = _rot(q_ref[...].astype(F32), cos, sin).astype(BF16)
        kf = _rot(k_ref[...].astype(F32), cos, sin) * (RET_QK ** -0.5)
        k = kf.astype(BF16)
        kz = (kf * zeta).astype(BF16)
        v = v_ref[...]
        rb = rs_ref[0, 0]
        s = (_dot_nt(q, k) * dec).astype(BF16)
        o = _dot(s, v) + _dot(q, rb) * xi
        mu = jnp.mean(o, axis=-1, keepdims=True)
        var = jnp.mean(jnp.square(o - mu), axis=-1, keepdims=True)
        rstd = lax.rsqrt(var + GN_EPS)
        yh = (o - mu) * rstd
        dyf = dy_ref[...].astype(F32)
        do = (dyf - jnp.mean(dyf, axis=-1, keepdims=True) - yh * jnp.mean(dyf * yh, axis=-1, keepdims=True)) * rstd
        dob = do.astype(BF16)
        doxi = (do * xi).astype(BF16)
        dr = dr_acc[...]
        drb = dr.astype(BF16)
        ds = (_dot_nt(dob, v) * dec).astype(BF16)
        dq = _dot(ds, k) + _dot_nt(doxi, rb)
        dk = _dot_tn(ds, q) + _dot_nt(v, drb) * zeta
        dv = _dot_tn(s, dob) + _dot(kz, drb)
        dr_acc[...] = dr * gc_ref[0] + _dot_tn(q, doxi)
        dq_ref[...] = _rot_inv(dq, cos, sin).astype(dq_ref.dtype)
        dk_ref[...] = (_rot_inv(dk, cos, sin) * (RET_QK ** -0.5)).astype(dk_ref.dtype)
        dv_ref[...] = dv.astype(dv_ref.dtype)

    rn = lambda n: NC - 1 - n
    in_specs = _ret_in_specs(C, True, NC) + [
        pl.BlockSpec((1, 1, RET_QK, RET_V), lambda h, n: (h, rn(n), 0, 0)),
        pl.BlockSpec((C, RET_V), lambda h, n: (rn(n), h)),
    ]
    return pl.pallas_call(
        body, name="ret_bwd", grid=(RET_HEADS, NC), in_specs=in_specs,
        out_specs=[pl.BlockSpec((C, RET_QK), lambda h, n: (rn(n), h)),
                   pl.BlockSpec((C, RET_QK), lambda h, n: (rn(n), h)),
                   pl.BlockSpec((C, RET_V), lambda h, n: (rn(n), h))],
        out_shape=[jax.ShapeDtypeStruct((S, RET_HEADS * RET_QK), BF16),
                   jax.ShapeDtypeStruct((S, RET_HEADS * RET_QK), BF16),
                   jax.ShapeDtypeStruct((S, RET_HEADS * RET_V), BF16)],
        scratch_shapes=[pltpu.VMEM((RET_QK, RET_V), F32)],
        compiler_params=_cparams(("parallel", "arbitrary")),
    )(p, p, p, rc['cos'], rc['sin'], rc['decay'], rc['xi'], rc['zeta'], rc['gc'], rstate, dy)


SB_T = 256
SB_SCALE = SB_DIM ** -0.5


def _tri():
    j = np.arange(SB_T)
    rev = (j[:, None] >= j[None, :]).astype(np.float32)
    return jnp.asarray(np.stack([np.concatenate([rev, rev]), np.concatenate([rev.T, rev.T])]), BF16)


def _run_sum(x, tri2):
    hi = x.astype(BF16)
    lo = (x - hi.astype(F32)).astype(BF16)
    return _dot(jnp.concatenate([hi, lo], axis=1), tri2)


def _sb_fwd(p, tri):
    S = p.shape[0]
    T = min(SB_T, S)
    NQ = S // T
    assert NQ <= 128
    qb, kb, vb = O_SQ // 128, O_SK // 128, O_SV // 128

    def body(q_ref, k_ref, v_ref, tri_ref, o_ref, cs_ref, o_acc, run, zbuf, abuf):
        i = pl.program_id(1)
        lane = lax.broadcasted_iota(jnp.int32, (1, 128), 1)
        tri2 = tri_ref[0]
        qs = [jnp.where((lane >= 64) if hh else (lane < 64), q_ref[...], jnp.zeros_like(q_ref[...]))
              * jnp.asarray(SB_SCALE, BF16) for hh in range(2)]
        cs_ref[...] = jnp.zeros_like(cs_ref)
        o_acc[...] = jnp.zeros_like(o_acc)
        run[...] = jnp.zeros_like(run)

        def kv(ref, j):
            return ref[pl.ds(pl.multiple_of(j * T, T), T), :]

        for hh in range(2):
            zbuf[hh] = _dot_nt(qs[hh], kv(k_ref, i))

        def block(t, diagonal):
            j = i - t
            zn = [_dot_nt(qs[hh], kv(k_ref, jnp.maximum(j - 1, 0))) for hh in range(2)]
            if not diagonal:
                av = [_dot(abuf[hh], kv(v_ref, j + 1)) for hh in range(2)]
            else:
                msk = lax.broadcasted_iota(jnp.int32, (T, T), 1) < lax.broadcasted_iota(jnp.int32, (T, T), 0)
            zs = [zbuf[hh] for hh in range(2)]
            cls = []
            for hh in range(2):
                sp = jnp.maximum(zs[hh], 0.0) + jnp.log(1.0 + jnp.exp(-jnp.abs(zs[hh])))
                if diagonal:
                    sp = jnp.where(msk, sp, 0.0)
                cls.append(_run_sum(sp, tri2))
            for hh in range(2):
                csl = slice(hh * 128, (hh + 1) * 128)
                cs = run[hh]
                a = jnp.exp(zs[hh] - cls[hh] - cs)
                if diagonal:
                    a = jnp.where(msk, a, 0.0)
                abuf[hh] = a.astype(BF16)
                cs_ref[:, csl] = jnp.where(lane == j, cs, cs_ref[:, csl])
                run[hh] = cs + cls[hh][:, 0:1]
            for hh in range(2):
                if not diagonal:
                    o_acc[hh] += av[hh]
                zbuf[hh] = zn[hh]

        block(0, True)

        def step(t, carry):
            block(t, False)
            return carry

        lax.fori_loop(1, i + 1, step, 0)
        o_ref[...] = jnp.where(lane < 64, o_acc[0] + _dot(abuf[0], kv(v_ref, 0)),
                               o_acc[1] + _dot(abuf[1], kv(v_ref, 0))).astype(o_ref.dtype)

    return pl.pallas_call(
        body, name="sb_fwd", grid=(SB_HEADS // 2, NQ),
        scratch_shapes=[pltpu.VMEM((2, T, 128), F32), pltpu.VMEM((2, T, 1), F32), pltpu.VMEM((2, T, T), F32),
                        pltpu.VMEM((2, T, T), BF16)],
        in_specs=[pl.BlockSpec((T, 128), lambda h, i: (i, qb + h)),
                  pl.BlockSpec((S, 128), lambda h, i: (0, kb + h)),
                  pl.BlockSpec((S, 128), lambda h, i: (0, vb + h)),
                  pl.BlockSpec((1, 2 * T, T), lambda h, i: (0, 0, 0))],
        out_specs=[pl.BlockSpec((T, 128), lambda h, i: (i, h)),
                   pl.BlockSpec((T, 256), lambda h, i: (i, h))],
        out_shape=[jax.ShapeDtypeStruct((S, SB_HEADS * SB_DIM), BF16),
                   jax.ShapeDtypeStruct((S, SB_HEADS * 128), F32)],
        compiler_params=_cparams(("parallel", "arbitrary")),
    )(p, p, p, tri)


def _sb_bwd(p, carries, dy, tri):
    S = p.shape[0]
    T = min(SB_T, S)
    NQ = S // T
    qb, kb, vb = O_SQ // 128, O_SK // 128, O_SV // 128

    def body(q_ref, k_ref, v_ref, cs_ref, dy_ref, tri_ref, dq_ref, dk_ref, dv_ref, dk_acc, dv_acc, dq_acc, run,
             zbuf, dabuf, dzbuf, abuf):
        i = pl.program_id(1)

        @pl.when(i == 0)
        def _():
            dk_acc[...] = jnp.zeros_like(dk_acc)
            dv_acc[...] = jnp.zeros_like(dv_acc)

        lane = lax.broadcasted_iota(jnp.int32, (1, 128), 1)
        rev2, fwd2 = tri_ref[0], tri_ref[1]
        hms = [(lane >= 64) if hh else (lane < 64) for hh in range(2)]
        qs = [jnp.where(hm, q_ref[...], jnp.zeros_like(q_ref[...])) * jnp.asarray(SB_SCALE, BF16) for hm in hms]
        dos = [jnp.where(hm, dy_ref[...], jnp.zeros_like(dy_ref[...])) for hm in hms]
        dq_acc[...] = jnp.zeros_like(dq_acc)
        run[...] = jnp.zeros_like(run)
        dzbuf[...] = jnp.zeros_like(dzbuf)
        abuf[...] = jnp.zeros_like(abuf)

        def kv(ref, j):
            return ref[pl.ds(pl.multiple_of(j * T, T), T), :]

        def flush(jp):
            kp = kv(k_ref, jp)
            dq_add = [_dot(dzbuf[hh], kp) for hh in range(2)]
            dk_add = _dot_tn(dzbuf[0], qs[0]) + _dot_tn(dzbuf[1], qs[1])
            dv_add = _dot_tn(abuf[0], dos[0]) + _dot_tn(abuf[1], dos[1])
            return dq_add, dk_add, dv_add

        def apply(jp, adds):
            dq_add, dk_add, dv_add = adds
            rows = pl.ds(pl.multiple_of(jp * T, T), T)
            for hh in range(2):
                dq_acc[hh] += dq_add[hh]
            dk_acc[rows, :] += dk_add
            dv_acc[rows, :] += dv_add

        for hh in range(2):
            zbuf[hh] = _dot_nt(qs[hh], kv(k_ref, 0))
            dabuf[hh] = _dot_nt(dos[hh], kv(v_ref, 0))

        def block(j, diagonal):
            jp = jnp.maximum(j - 1, 0)
            if not diagonal:
                zn = [_dot_nt(qs[hh], kv(k_ref, j + 1)) for hh in range(2)]
                dan = [_dot_nt(dos[hh], kv(v_ref, j + 1)) for hh in range(2)]
            else:
                msk = lax.broadcasted_iota(jnp.int32, (T, T), 1) < lax.broadcasted_iota(jnp.int32, (T, T), 0)
            adds = flush(jp)
            zs = [zbuf[hh] for hh in range(2)]
            sigs, cls = [], []
            for hh in range(2):
                e = jnp.exp(-jnp.abs(zs[hh]))
                sp = jnp.maximum(zs[hh], 0.0) + jnp.log(1.0 + e)
                r = 1.0 / (1.0 + e)
                sigs.append(jnp.where(zs[hh] >= 0, r, e * r))
                if diagonal:
                    sp = jnp.where(msk, sp, 0.0)
                cls.append(_run_sum(sp, rev2))
            pgs, gs = [], []
            for hh in range(2):
                csl = slice(hh * 128, (hh + 1) * 128)
                cs = jnp.sum(jnp.where(lane == j, cs_ref[:, csl], 0.0), axis=-1, keepdims=True)
                a = jnp.exp(zs[hh] - cls[hh] - cs)
                if diagonal:
                    a = jnp.where(msk, a, 0.0)
                abuf_new = a.astype(BF16)
                g = a * dabuf[hh]
                gs.append((g, abuf_new))
                pgs.append(_run_sum(g, fwd2))
            for hh in range(2):
                g, abuf_new = gs[hh]
                cg = run[hh]
                dz = g - sigs[hh] * (cg + pgs[hh])
                if diagonal:
                    dz = jnp.where(msk, dz, 0.0)
                run[hh] = cg + pgs[hh][:, T - 1:T]
                dzbuf[hh] = dz.astype(BF16)
                abuf[hh] = abuf_new
            apply(jp, adds)
            if not diagonal:
                for hh in range(2):
                    zbuf[hh] = zn[hh]
                    dabuf[hh] = dan[hh]

        def step(j, carry):
            block(j, False)
            return carry

        lax.fori_loop(0, i, step, 0)
        block(i, True)
        apply(i, flush(i))
        dq_ref[...] = (jnp.where(lane < 64, dq_acc[0], dq_acc[1]) * SB_SCALE).astype(dq_ref.dtype)

        @pl.when(i == NQ - 1)
        def _():
            dk_ref[...] = dk_acc[...].astype(dk_ref.dtype)
            dv_ref[...] = dv_acc[...].astype(dv_ref.dtype)

    W = SB_HEADS * SB_DIM
    return pl.pallas_call(
        body, name="sb_bwd", grid=(SB_HEADS // 2, NQ),
        in_specs=[pl.BlockSpec((T, 128), lambda h, i: (i, qb + h)),
                  pl.BlockSpec((S, 128), lambda h, i: (0, kb + h)),
                  pl.BlockSpec((S, 128), lambda h, i: (0, vb + h)),
                  pl.BlockSpec((T, 256), lambda h, i: (i, h)),
                  pl.BlockSpec((T, 128), lambda h, i: (i, h)),
                  pl.BlockSpec((2, 2 * T, T), lambda h, i: (0, 0, 0))],
        out_specs=[pl.BlockSpec((T, 128), lambda h, i: (i, h)),
                   pl.BlockSpec((S, 128), lambda h, i: (0, h)),
                   pl.BlockSpec((S, 128), lambda h, i: (0, h))],
        out_shape=[jax.ShapeDtypeStruct((S, W), BF16)] * 3,
        scratch_shapes=[pltpu.VMEM((S, 128), F32), pltpu.VMEM((S, 128), F32), pltpu.VMEM((2, T, 128), F32),
                        pltpu.VMEM((2, T, 1), F32), pltpu.VMEM((2, T, T), F32), pltpu.VMEM((2, T, T), F32),
                        pltpu.VMEM((2, T, T), BF16), pltpu.VMEM((2, T, T), BF16)],
        compiler_params=_cparams(("parallel", "arbitrary")),
    )(p, p, p, carries, dy, tri)


def _exchange(srcs, out_shapes, src_slice, dst_slice, name):
    n = len(srcs)

    def body(*refs):
        ins, outs = refs[:n], refs[n:2 * n]
        send_sems, recv_sems, loc_sems = refs[2 * n:]
        x, y, c = lax.axis_index("x"), lax.axis_index("y"), lax.axis_index("c")
        me = 4 * x + 2 * y + c
        local = [pltpu.make_async_copy(src_slice(t, ins[t], me), dst_slice(t, outs[t], me), loc_sems.at[t])
                 for t in range(n)]
        for cp in local:
            cp.start()
        sends, recvs = [], []
        for k in (1, 2, 4, 6, 3, 5, 7):
            px = 1 - x if k & 4 else x
            py = 1 - y if k & 2 else y
            pc = 1 - c if k & 1 else c
            peer = 4 * px + 2 * py + pc
            for t in range(n):
                s = t * 7 + k - 1
                sends.append(pltpu.make_async_remote_copy(
                    src_ref=src_slice(t, ins[t], peer), dst_ref=dst_slice(t, outs[t], me),
                    send_sem=send_sems.at[s], recv_sem=recv_sems.at[s],
                    device_id=(px, py, pc), device_id_type=pl.DeviceIdType.MESH))
                recvs.append(pltpu.make_async_remote_copy(
                    src_ref=src_slice(t, ins[t], me), dst_ref=dst_slice(t, outs[t], peer),
                    send_sem=send_sems.at[s], recv_sem=recv_sems.at[s],
                    device_id=(px, py, pc), device_id_type=pl.DeviceIdType.MESH))
        for cp in sends:
            cp.start()
        for cp in recvs:
            cp.wait_recv()
        for cp in sends:
            cp.wait_send()
        for cp in local:
            cp.wait()

    anyspec = pl.BlockSpec(memory_space=pl.ANY)
    return pl.pallas_call(
        body, name=name, in_specs=[anyspec] * n, out_specs=[anyspec] * n,
        out_shape=[jax.ShapeDtypeStruct(s, d) for s, d in out_shapes],
        scratch_shapes=[pltpu.SemaphoreType.DMA((7 * n,)), pltpu.SemaphoreType.DMA((7 * n,)),
                        pltpu.SemaphoreType.DMA((n,))],
    )(*srcs)


def _all_gather_lead(xs, name):
    return _exchange(
        xs, [((N_DEV,) + x.shape, x.dtype) for x in xs],
        lambda t, ref, peer: ref, lambda t, ref, who: ref.at[who], name)


def _all_to_all_lead(xs, name):
    return _exchange(
        xs, [(x.shape, x.dtype) for x in xs],
        lambda t, ref, peer: ref.at[peer], lambda t, ref, who: ref.at[who], name)


_W_AXIS = {"w_in": 1, "w_ret_out": 0, "w_sb_out": 0, "w_mix_out": 0, "w_up": 1, "w_down": 0}
_W_NAMES = tuple(_W_AXIS)


def _window(ref, axis, who, width, count=1):
    start = pl.multiple_of(who * width, width)
    return ref.at[pl.ds(start, count * width), :] if axis == 0 else ref.at[:, pl.ds(start, count * width)]


_HBM = pl.BlockSpec(memory_space=pltpu.HBM)
_SEM = pl.BlockSpec(memory_space=pltpu.SEMAPHORE)
_EFFECT = pltpu.SideEffectType.DATAFLOW_SIDE_EFFECTING


def _place_own(srcs, shapes, src_slice, dst_slice, name, deps=()):
    n, nd = len(srcs), len(deps)

    def body(*refs):
        ins, outs, sems = refs[:n], refs[n + nd:2 * n + nd], refs[2 * n + nd]
        me = 4 * lax.axis_index("x") + 2 * lax.axis_index("y") + lax.axis_index("c")
        cps = [pltpu.make_async_copy(src_slice(t, ins[t], me), dst_slice(t, outs[t], me), sems.at[t]) for t in range(n)]
        for cp in cps:
            cp.start()
        for cp in cps:
            cp.wait()

    return pl.pallas_call(
        body, name=name, in_specs=[_HBM] * n + [pl.BlockSpec(memory_space=pl.ANY)] * nd, out_specs=[_HBM] * n,
        out_shape=[pltpu.HBM(s, d) for s, d in shapes],
        scratch_shapes=[pltpu.SemaphoreType.DMA((n,))],
    )(*[pltpu.with_memory_space_constraint(s, pltpu.HBM) for s in srcs], *deps)


def _exchange_start(srcs, lands, src_slice, dst_slice, name):
    n = len(srcs)

    def body(*refs):
        ins, lnd = refs[:n], refs[n:2 * n]
        sems = refs[2 * n:4 * n]
        token = refs[6 * n]
        x, y, c = lax.axis_index("x"), lax.axis_index("y"), lax.axis_index("c")
        me = 4 * x + 2 * y + c
        for k in (1, 2, 4, 6, 3, 5, 7):
            px = 1 - x if k & 4 else x
            py = 1 - y if k & 2 else y
            pc = 1 - c if k & 1 else c
            peer = 4 * px + 2 * py + pc
            for t in range(n):
                pltpu.make_async_remote_copy(
                    src_ref=src_slice(t, ins[t], peer), dst_ref=dst_slice(t, lnd[t], me),
                    send_sem=sems[2 * t], recv_sem=sems[2 * t + 1],
                    device_id=(px, py, pc), device_id_type=pl.DeviceIdType.MESH).start()
        token[...] = jnp.zeros_like(token)

    res = pl.pallas_call(
        body, name=name, in_specs=[_HBM] * (2 * n),
        out_specs=[_SEM] * (2 * n) + [_HBM] * (2 * n) + [pl.BlockSpec(memory_space=pltpu.VMEM)],
        out_shape=[pltpu.SemaphoreType.DMA(())] * (2 * n) + [pltpu.HBM(s.shape, s.dtype) for s in srcs]
        + [pltpu.HBM(s.shape, s.dtype) for s in lands] + [jax.ShapeDtypeStruct((8, 128), F32)],
        input_output_aliases={t: 2 * n + t for t in range(2 * n)},
        compiler_params=pltpu.CompilerParams(has_side_effects=_EFFECT),
    )(*[pltpu.with_memory_space_constraint(s, pltpu.HBM) for s in srcs], *lands)
    return dict(n=n, sems=res[:2 * n], srcs=res[2 * n:3 * n], lands=res[3 * n:4 * n], token=res[4 * n])


def _exchange_wait(h, span, after, name):
    n = h['n']

    def body(*refs):
        lnd = refs[n:2 * n]
        sems = refs[2 * n:4 * n]
        x, y, c = lax.axis_index("x"), lax.axis_index("y"), lax.axis_index("c")
        for t in range(n):
            w = span(t, lnd[t])
            cp = pltpu.make_async_remote_copy(src_ref=w, dst_ref=w, send_sem=sems[2 * t], recv_sem=sems[2 * t + 1],
                                              device_id=(x, y, 1 - c), device_id_type=pl.DeviceIdType.MESH)
            cp.wait_send()
            cp.wait_recv()

    after = list(after)
    res = pl.pallas_call(
        body, name=name,
        in_specs=[_HBM] * (2 * n) + [_SEM] * (2 * n) + [pl.BlockSpec(memory_space=pl.ANY)] * len(after),
        out_specs=[_HBM] * (2 * n),
        out_shape=[pltpu.HBM(s.shape, s.dtype) for s in h['srcs']] + [pltpu.HBM(s.shape, s.dtype) for s in h['lands']],
        input_output_aliases={t: t for t in range(2 * n)},
        compiler_params=pltpu.CompilerParams(has_side_effects=_EFFECT),
    )(*h['srcs'], *h['lands'], *h['sems'], *after)
    return list(res[n:])


def _gather_start(shards, names, tag, deps=()):
    xs = [shards[nm] for nm in names]
    axes = [_W_AXIS[nm] for nm in names]
    widths = [x.shape[ax] for x, ax in zip(xs, axes)]
    shapes = [(tuple(d * (N_DEV if a == ax else 1) for a, d in enumerate(x.shape)), x.dtype) for x, ax in zip(xs, axes)]
    src = lambda t, ref, peer: ref
    dst = lambda t, ref, who: _window(ref, axes[t], who, widths[t])
    lands = _place_own(xs, shapes, src, dst, "gw_own_" + tag, deps)
    h = _exchange_start(xs, lands, src, dst, "gw_start_" + tag)
    h['span'] = lambda t, ref: _window(ref, axes[t], 0, widths[t], N_DEV - 1)
    h['tag'] = "gw_wait_" + tag
    return h


def _scatter_start(grads, names, tag):
    xs = [grads[nm] for nm in names]
    axes = [_W_AXIS[nm] for nm in names]
    widths = [x.shape[ax] // N_DEV for x, ax in zip(xs, axes)]
    shapes = [((N_DEV,) + tuple(d // (N_DEV if a == ax else 1) for a, d in enumerate(x.shape)), x.dtype)
              for x, ax in zip(xs, axes)]
    src = lambda t, ref, peer: _window(ref, axes[t], peer, widths[t])
    dst = lambda t, ref, who: ref.at[who]
    lands = _place_own(xs, shapes, src, dst, "sg_own_" + tag)
    h = _exchange_start(xs, lands, src, dst, "sg_start_" + tag)
    h['span'] = lambda t, ref: ref.at[pl.ds(0, N_DEV - 1)]
    h['tag'] = "sg_wait_" + tag
    return h


def _finish(h, after):
    return _exchange_wait(h, h['span'], after, h['tag'])


class _LayerWeights:
    def __init__(self, groups, started):
        self.groups = groups
        self.started = started
        self.got = {}
        self.after = None

    def __getitem__(self, nm):
        if nm not in self.got:
            for names, h in self.groups:
                if nm in names:
                    self.got.update(zip(names, _finish(h, list(self.after) + self.started)))
        return self.got[nm]


def _adam(parts, w, m, v, name, tr=256):
    P, R, C = parts.shape
    tr = min(tr, R)
    assert R % tr == 0
    bc1 = 1.0 / (1.0 - ADAM_B1 ** ADAM_STEP)
    bc2 = 1.0 / (1.0 - ADAM_B2 ** ADAM_STEP)

    def body(p_ref, w_ref, m_ref, v_ref, g_out, d_out, m_out, v_out):
        g = p_ref[0].astype(F32)
        for s in range(1, P):
            g = g + p_ref[s].astype(F32)
        mm = ADAM_B1 * m_ref[...] + (1.0 - ADAM_B1) * g
        vv = ADAM_B2 * v_ref[...] + (1.0 - ADAM_B2) * jnp.square(g)
        g_out[...] = g
        m_out[...] = mm
        v_out[...] = vv
        d_out[...] = -ADAM_LR * ((mm * bc1) / (jnp.sqrt(vv * bc2) + ADAM_EPS) + ADAM_WD * w_ref[...])

    spec = pl.BlockSpec((tr, C), lambda i: (i, 0))
    return pl.pallas_call(
        body, name=name, grid=(R // tr,),
        in_specs=[pl.BlockSpec((P, tr, C), lambda i: (0, i, 0)), spec, spec, spec],
        out_specs=[spec] * 4, out_shape=[jax.ShapeDtypeStruct((R, C), F32)] * 4,
        compiler_params=_cparams(("parallel",)),
    )(parts, w, m, v)


def _mod_partial(cact_all, w_ada_l, b_ada_l):
    def body(c_ref, w_ref, b_ref, o_ref):
        o_ref[...] = _dot(c_ref[...].astype(BF16), w_ref[...].astype(BF16)) + b_ref[...]

    return pl.pallas_call(
        body, name="mod_partial", out_shape=jax.ShapeDtypeStruct((cact_all.shape[0], w_ada_l.shape[1]), F32),
        compiler_params=pltpu.CompilerParams(vmem_limit_bytes=VMEM_LIMIT),
    )(cact_all, w_ada_l, b_ada_l)


def _ada_grad(cact_t, dmod):
    D, n = cact_t.shape[0], dmod.shape[1]

    def body(c_ref, d_ref, o_ref):
        ct = c_ref[...].astype(BF16).astype(F32)
        dm = d_ref[...].astype(BF16).astype(F32)
        acc = ct[:, 0:1] * dm[0:1, :]
        for b in range(1, N_DEV):
            acc = acc + ct[:, b:b + 1] * dm[b:b + 1, :]
        o_ref[0] = acc

    return pl.pallas_call(
        body, name="ada_grad", out_shape=jax.ShapeDtypeStruct((1, D, n), F32),
        compiler_params=pltpu.CompilerParams(vmem_limit_bytes=VMEM_LIMIT),
    )(cact_t, dmod)


def _norm_mod(x, r, gv, sh):
    return x * r * gv + sh


def _silu(x):
    return x * _sigmoid(x)


def _layer_fwd(x0, mod, gn1, gn2, W, rc, tri):
    S = x0.shape[0]
    sh1, sc1, g1m, sh2, sc2, g2m = [mod[i:i + 1] for i in range(N_MOD)]
    gv1 = gn1 * (1.0 + sc1)
    gv2 = gn2 * (1.0 + sc2)
    (r1,) = _ew(lambda x: (lax.rsqrt(jnp.mean(x * x, axis=-1, keepdims=True) + EPS),), [x0], outs=[('col', F32)],
                name="row_rstd")
    W.after = [r1, gv1]
    (p,) = _mm(x0, W["w_in"], tm=1024, tn=1024, tk=512, a_ex=[(r1, 'm'), (gv1, 'k'), (sh1, 'k')], pro=_norm_mod,
               outs=(BF16,), name="mm_in")
    yret, rstate = _ret_fwd(p, rc)
    ysb, sbc = _sb_fwd(p, tri)
    W.after = [ysb]
    rg = (p, O_RG)
    (ya,) = _mm_cols(yret, W["w_ret_out"], a_cols=[rg], pro=lambda yr, g: _silu(g.astype(F32)) * yr.astype(F32),
                     outs=(BF16,), name="mm_ret_out")
    yb, mg = _mm_cols(ysb, W["w_sb_out"], o_cols=[(ya, 0), (p, O_GA), (p, O_GB)],
                      epi=lambda acc, a, ga, gb: (acc, _sigmoid(ga.astype(F32)) * a.astype(F32)
                                                  + _sigmoid(gb.astype(F32)) * acc),
                      outs=(BF16, BF16), name="mm_sb_out")
    mo, x1 = _mm(mg, W["w_mix_out"], tm=1024, tn=1024, tk=512, o_ex=[(x0, 'o'), (g1m, 'n')],
                 epi=lambda acc, x, g: (acc, x + g * acc), outs=(BF16, F32), name="mm_mix_out")
    (r2,) = _ew(lambda x: (lax.rsqrt(jnp.mean(x * x, axis=-1, keepdims=True) + EPS),), [x1], outs=[('col', F32)],
                name="row_rstd")
    (act,) = _mm(x1, W["w_up"], tm=1024, tn=1024, tk=512, a_ex=[(r2, 'm'), (gv2, 'k'), (sh2, 'k')], pro=_norm_mod,
                 epi=lambda acc: (jnp.maximum(acc, 0.0),), outs=(BF16,), name="mm_up")
    dn, x2 = _mm(act, W["w_down"], tm=1024, tn=1024, tk=512, pro=lambda a: jnp.square(a.astype(F32)),
                 o_ex=[(x1, 'o'), (g2m, 'n')], epi=lambda acc, x, g: (acc, x + g * acc), outs=(BF16, F32),
                 name="mm_down")
    saved = dict(x0=x0, r1=r1, p=p, yret=yret, rstate=rstate, ysb=ysb, sbc=sbc, ya=ya, yb=yb, mg=mg, mo=mo, x1=x1, r2=r2,
                 act=act, dn=dn, gv1=gv1, gv2=gv2, mod=mod, gn1=gn1, gn2=gn2)
    return x2, saved


def _mm_cols(a, b, *, a_cols=(), o_cols=(), pro=None, epi=None, outs, name, ta=False, tb=False):
    if ta:
        K, M = a.shape
    else:
        M, K = a.shape
    N = b.shape[0] if tb else b.shape[1]
    tm, tn, tk = min(1024, M), min(1024, N), min(512, K)
    nk = K // tk
    dims = (((0 if ta else 1,), (1 if tb else 0,)), ((), ()))
    in_specs = [
        pl.BlockSpec((tk, tm), lambda i, j, k: (k, i)) if ta else pl.BlockSpec((tm, tk), lambda i, j, k: (i, k)),
        pl.BlockSpec((tn, tk), lambda i, j, k: (j, k)) if tb else pl.BlockSpec((tk, tn), lambda i, j, k: (k, j)),
    ]
    args = [a, b]
    for arr, off in a_cols:
        if ta:
            assert off % tm == 0
            in_specs.append(pl.BlockSpec((tk, tm), lambda i, j, k, o=off // tm: (k, o + i)))
        else:
            assert off % tk == 0
            in_specs.append(pl.BlockSpec((tm, tk), lambda i, j, k, o=off // tk: (i, o + k)))
        args.append(arr)
    for arr, off in o_cols:
        assert off % tn == 0
        in_specs.append(pl.BlockSpec((tm, tn), lambda i, j, k, o=off // tn: (i, o + j)))
        args.append(arr)
    na, no, nout = len(a_cols), len(o_cols), len(outs)

    def body(*refs):
        a_ref, b_ref = refs[0], refs[1]
        aex = refs[2:2 + na]
        oex = refs[2 + na:2 + na + no]
        out_refs = refs[2 + na + no:2 + na + no + nout]
        acc = refs[-1]
        k = pl.program_id(2)

        @pl.when(k == 0)
        def _():
            acc[...] = jnp.zeros_like(acc)

        at = a_ref[...]
        if pro is not None:
            at = pro(at, *[r[...] for r in aex])
        acc[...] += lax.dot_general(at.astype(BF16), b_ref[...].astype(BF16), dims, preferred_element_type=F32)

        @pl.when(k == nk - 1)
        def _():
            res = acc[...]
            vals = epi(res, *[r[...] for r in oex]) if epi is not None else (res,)
            for o_ref, v in zip(out_refs, vals):
                o_ref[...] = v.astype(o_ref.dtype)

    return pl.pallas_call(
        body, name=name, grid=(M // tm, N // tn, nk), in_specs=in_specs,
        out_specs=[pl.BlockSpec((tm, tn), lambda i, j, k: (i, j)) for _ in outs],
        out_shape=[jax.ShapeDtypeStruct((M, N), dt) for dt in outs],
        scratch_shapes=[pltpu.VMEM((tm, tn), F32)],
        compiler_params=_cparams(("parallel", "parallel", "arbitrary")),
    )(*args)


def _norm_bwd(dh, x, r, dres, gv, gn, extra_rows=(), extra_vecs=(), extra_fn=None, extra_outs=(), name="norm_bwd"):
    D = x.shape[1]
    ne = len(extra_rows)

    def fn(dh_t, x_t, dres_t, *rest):
        er, rest = rest[:ne], rest[ne:]
        gv_t = rest[0]
        ev, r_t = rest[1:-1], rest[-1]
        xh = x_t * r_t
        dxh = dh_t * gv_t
        dx = r_t * (dxh - xh * jnp.mean(dxh * xh, axis=-1, keepdims=True)) + dres_t
        base = (dx, dh_t, dh_t * xh)
        if extra_fn is None:
            return base
        return base + tuple(extra_fn(dx, *er, *ev))

    return _ew(fn, [dh, x, dres] + list(extra_rows), vecs=[gv] + list(extra_vecs), cols=[r],
               outs=[('row', D, F32), ('sum', D), ('sum', D)] + list(extra_outs), name=name)


def _layer_bwd(dx2, sv, W, rc, tri, emit):
    mod = sv['mod']
    sh1, sc1, g1m, sh2, sc2, g2m = [mod[i:i + 1] for i in range(N_MOD)]
    D = D_MODEL
    p = sv['p']
    d_g2m, d_dn = _ew(lambda dx, dn, g: (dx * dn.astype(F32), dx * g), [dx2, sv['dn']], vecs=[g2m],
                      outs=[('sum', D), ('row', D, BF16)], name="gate_bwd")
    (d_up,) = _mm(d_dn, W["w_down"], tb=True, tm=1024, tn=1024, tk=512, o_ex=[(sv['act'], 'o')],
                  epi=lambda acc, a: (acc * 2.0 * a.astype(F32),), outs=(BF16,), name="mm_down_dx")
    (gw_down,) = _mm(sv['act'], d_dn, ta=True, tm=1024, tn=1024, tk=512, pro=lambda a: jnp.square(a.astype(F32)),
                     outs=(BF16,), name="mm_down_dw")
    (gw_up,) = _mm(sv['x1'], d_up, ta=True, tm=1024, tn=1024, tk=512,
                   a_ex=[(sv['r2'], 'k'), (sv['gv2'].reshape(1, D), 'm'), (sh2, 'm')], pro=_norm_mod,
                   outs=(BF16,), name="mm_up_dw")
    tok = emit(dict(w_down=gw_down, w_up=gw_up), "mlp")
    (d_h2,) = _mm(d_up, W["w_up"], tb=True, tm=1024, tn=1024, tk=512, outs=(F32,), name="mm_up_dx", deps=[tok])
    dx1, d_sh2, s_h2, d_g1m, d_mo = _norm_bwd(
        d_h2, sv['x1'], sv['r2'], dx2, sv['gv2'], sv['gn2'],
        extra_rows=[sv['mo']], extra_vecs=[g1m],
        extra_fn=lambda dx, mo, g: (dx * mo.astype(F32), dx * g),
        extra_outs=[('sum', D), ('row', D, BF16)], name="norm_bwd_mlp")
    d_sc2 = sv['gn2'] * s_h2
    d_gn2 = (1.0 + sc2) * s_h2
    def mix_epi(acc, ya, yb, ga, gb):
        sa, sb = _sigmoid(ga.astype(F32)), _sigmoid(gb.astype(F32))
        return (acc * sa, acc * sb, acc * ya.astype(F32) * sa * (1.0 - sa), acc * yb.astype(F32) * sb * (1.0 - sb))

    d_ya, d_yb, d_ga, d_gb = _mm_cols(d_mo, W["w_mix_out"], tb=True,
                                      o_cols=[(sv['ya'], 0), (sv['yb'], 0), (p, O_GA), (p, O_GB)], epi=mix_epi,
                                      outs=(BF16,) * 4, name="mm_mix_dx")
    (gw_mix,) = _mm(sv['mg'], d_mo, ta=True, tm=1024, tn=1024, tk=512, outs=(BF16,), name="mm_mix_dw")

    def ro_epi(acc, g, yr):
        gf = g.astype(F32)
        s = _sigmoid(gf)
        return (acc * yr.astype(F32) * s * (1.0 + gf * (1.0 - s)), acc * gf * s)

    d_rg, d_yret = _mm_cols(d_ya, W["w_ret_out"], tb=True, o_cols=[(p, O_RG), (sv['yret'], 0)], epi=ro_epi,
                            outs=(BF16, BF16), name="mm_ret_dx")
    (gw_ro,) = _mm_cols(sv['yret'], d_ya, ta=True, a_cols=[(p, O_RG)],
                        pro=lambda yr, g: _silu(g.astype(F32)) * yr.astype(F32), outs=(BF16,), name="mm_ret_dw")
    (gw_so,) = _mm(sv['ysb'], d_yb, ta=True, tm=1024, tn=1024, tk=512, outs=(BF16,), name="mm_sb_dw")
    tok = emit(dict(w_mix_out=gw_mix, w_ret_out=gw_ro, w_sb_out=gw_so), "mix")
    (d_ysb,) = _mm(d_yb, W["w_sb_out"], tb=True, tm=1024, tn=1024, tk=512, outs=(BF16,), name="mm_sb_dx", deps=[tok])
    d_sq, d_sk, d_sv = _sb_bwd(p, sv['sbc'], d_ysb, tri)
    d_rq, d_rk, d_rv = _ret_bwd(p, sv['rstate'], d_yret, rc)
    dp = jnp.concatenate([d_rq, d_rk, d_rv, d_rg, d_sq, d_sk, d_sv, d_ga, d_gb], axis=1)
    (gw_in,) = _mm(sv['x0'], dp, ta=True, tm=1024, tn=1024, tk=512,
                   a_ex=[(sv['r1'], 'k'), (sv['gv1'].reshape(1, D), 'm'), (sh1, 'm')], pro=_norm_mod,
                   outs=(BF16,), name="mm_in_dw")
    tok = emit(dict(w_in=gw_in), "in")
    (d_h,) = _mm(dp, W["w_in"], tb=True, tm=1024, tn=1024, tk=512, outs=(F32,), name="mm_in_dx", deps=[tok])
    dx0, d_sh1, s_h1 = _norm_bwd(d_h, sv['x0'], sv['r1'], dx1, sv['gv1'], sv['gn1'], name="norm_bwd_mix")
    d_sc1 = sv['gn1'] * s_h1
    d_gn1 = (1.0 + sc1) * s_h1
    d_mod = jnp.concatenate([d_sh1, d_sc1, d_g1m, d_sh2, d_sc2, d_g2m], axis=1)
    return dx0, d_mod, d_gn1, d_gn2


def kernel(x, c, norm_mix_g, w_in, w_ret_out, w_sb_out, w_mix_out, norm_mlp_g, w_up, w_down, w_ada, b_ada, final_g, loss_target, m_norm_mix_g, m_w_in, m_w_ret_out, m_w_sb_out, m_w_mix_out, m_norm_mlp_g, m_w_up, m_w_down, m_w_ada, m_b_ada, m_final_g, v_norm_mix_g, v_w_in, v_w_ret_out, v_w_sb_out, v_w_mix_out, v_norm_mlp_g, v_w_up, v_w_down, v_w_ada, v_b_ada, v_final_g):
    S, D = x.shape[1], x.shape[2]
    x0 = x.reshape(S, D)
    tgt = loss_target.reshape(S, D)
    me = 4 * lax.axis_index("x") + 2 * lax.axis_index("y") + lax.axis_index("c")
    wts = dict(w_in=w_in, w_ret_out=w_ret_out, w_sb_out=w_sb_out, w_mix_out=w_mix_out, w_up=w_up, w_down=w_down)
    mts = dict(w_in=m_w_in, w_ret_out=m_w_ret_out, w_sb_out=m_w_sb_out, w_mix_out=m_w_mix_out, w_up=m_w_up, w_down=m_w_down)
    vts = dict(w_in=v_w_in, w_ret_out=v_w_ret_out, w_sb_out=v_w_sb_out, w_mix_out=v_w_mix_out, w_up=v_w_up, w_down=v_w_down)
    rc = _ret_consts(S)
    tri = _tri()

    shards = {}
    for nm in _W_NAMES:
        w = wts[nm]
        (wb,) = _ew(lambda t: (t,), [w.reshape(-1, w.shape[-1])], outs=[('row', w.shape[-1], BF16)], name="cast_bf16")
        shards[nm] = wb.reshape(w.shape)
    rest = tuple(nm for nm in _W_NAMES if nm != "w_in")
    started, layer_groups = [], []
    for l in range(DEPTH):
        sh_l = {nm: shards[nm][l] for nm in _W_NAMES}
        groups = [(("w_in",), "%d_in" % l), (rest, "%d_rest" % l)] if l == 0 else [(_W_NAMES, "%d_all" % l)]
        layer_groups.append([])
        for names, tag in groups:
            layer_groups[-1].append((names, _gather_start(sh_l, names, tag, started[-1:])))
            started.append(layer_groups[-1][-1][1]['token'])
    layer_w = [_LayerWeights(g, started) for g in layer_groups]

    (cact,) = _ew(lambda t: (_silu(t),), [jnp.pad(c, ((0, 7), (0, 0)))], outs=[('row', D, F32)], name="silu_c")
    (cact_all,) = _all_gather_lead([cact[0:1]], "gather_c")
    cact_all = cact_all.reshape(N_DEV, D)
    cact16 = jnp.pad(cact_all, ((0, 8), (0, 0)))
    n_ada = w_ada.shape[2]
    b_loc = lax.dynamic_slice_in_dim(b_ada, me * n_ada, n_ada, axis=1)
    mods = [_mod_partial(cact16, w_ada[l], b_loc[l:l + 1])[:N_DEV] for l in range(DEPTH)]
    modp = jnp.stack(mods, axis=1)
    (modr,) = _all_to_all_lead([modp], "scatter_mod")
    mod_full = jnp.transpose(modr, (1, 0, 2)).reshape(DEPTH, N_MOD, D)

    xs = x0
    saved = []
    for l in range(DEPTH):
        xs, sv = _layer_fwd(xs, mod_full[l], norm_mix_g[l:l + 1], norm_mlp_g[l:l + 1], layer_w[l], rc, tri)
        sv['W'] = layer_w[l]
        saved.append(sv)

    fg = final_g.reshape(1, D)

    def head(xt, tg, g):
        r = lax.rsqrt(jnp.mean(xt * xt, axis=-1, keepdims=True) + EPS)
        xh = xt * r
        e = xh * g - tg
        dy = e * (1.0 / D)
        dxh = dy * g
        dx = r * (dxh - xh * jnp.mean(dxh * xh, axis=-1, keepdims=True))
        return dx, dy * xh, 0.5 * e * e * (1.0 / D)

    dxs, d_fg, loss_cols = _ew(head, [xs, tgt], vecs=[fg], outs=[('row', D, F32), ('sum', D), ('sum', D)], name="loss_head")

    small = [None] * DEPTH
    pending = []
    for l in reversed(range(DEPTH)):
        sv = saved[l]

        def emit(gw, tag, l=l):
            names = tuple(gw)
            pending.append((l, names, _scatter_start(gw, names, "%d_%s" % (l, tag))))
            return pending[-1][2]['token']

        dxs, d_mod, d_gn1, d_gn2 = _layer_bwd(dxs, sv, sv['W'], rc, tri, emit)
        small[l] = (d_mod, d_gn1, d_gn2)
    grad_x = dxs.reshape(1, S, D)

    pack = jnp.concatenate([small[l][0] for l in range(DEPTH)] + [small[l][1] for l in range(DEPTH)]
                           + [small[l][2] for l in range(DEPTH)] + [d_fg, loss_cols], axis=1)
    (packs,) = _all_gather_lead([pack], "gather_small")
    packs = packs.reshape(N_DEV, -1)
    o = 0
    dmod_all = []
    for l in range(DEPTH):
        dmod_all.append(packs[:, o:o + N_MOD * D]); o += N_MOD * D
    gn1_parts = packs[:, o:o + DEPTH * D].reshape(N_DEV, DEPTH, D); o += DEPTH * D
    gn2_parts = packs[:, o:o + DEPTH * D].reshape(N_DEV, DEPTH, D); o += DEPTH * D
    fg_parts = packs[:, o:o + D].reshape(N_DEV, 1, D); o += D
    loss_parts = packs[:, o:o + D]
    (loss_sum,) = _ew(lambda t: (t,), [loss_parts], outs=[('sum', D)], name="loss_sum")
    loss = jnp.sum(loss_sum)

    res = {}
    per = {nm: [None] * DEPTH for nm in _W_NAMES}
    after = [dxs, loss_sum]
    for l, names, h in pending:
        for nm, landed in zip(names, _finish(h, after)):
            per[nm][l] = _adam(landed, wts[nm][l], mts[nm][l], vts[nm][l], "adam")
        after = [per[names[-1]][l][0]]
    for nm in _W_NAMES:
        res[nm] = [jnp.stack([per[nm][l][i] for l in range(DEPTH)]) for i in range(4)]
    res["norm_mix_g"] = _adam(gn1_parts, norm_mix_g, m_norm_mix_g, v_norm_mix_g, "adam")
    res["norm_mlp_g"] = _adam(gn2_parts, norm_mlp_g, m_norm_mlp_g, v_norm_mlp_g, "adam")
    fgr = _adam(fg_parts, fg, m_final_g.reshape(1, D), v_final_g.reshape(1, D), "adam")
    res["final_g"] = [t.reshape(D) for t in fgr]
    bparts = jnp.stack(dmod_all, axis=1)
    res["b_ada"] = _adam(bparts, b_ada, m_b_ada, v_b_ada, "adam")
    cact_t = cact_all.T
    ada = []
    for l in range(DEPTH):
        dm_loc = lax.dynamic_slice_in_dim(dmod_all[l], me * n_ada, n_ada, axis=1)
        ada.append(_adam(_ada_grad(cact_t, dm_loc), w_ada[l], m_w_ada[l], v_w_ada[l], "adam"))
    res["w_ada"] = [jnp.stack([ada[l][i] for l in range(DEPTH)]) for i in range(4)]

    order = ['norm_mix_g', 'w_in', 'w_ret_out', 'w_sb_out', 'w_mix_out', 'norm_mlp_g', 'w_up', 'w_down', 'w_ada', 'b_ada', 'final_g']
    out = [loss, grad_x]
    for i in range(4):
        out += [res[nm][i] for nm in order]
    return tuple(out)
```

```python
import functools
import math

import jax
import jax.numpy as jnp
import numpy as np
from jax import lax
from jax.experimental import pallas as pl
from jax.experimental.pallas import tpu as pltpu

F32 = jnp.float32
BF16 = jnp.bfloat16

N_DEV = 8
D_MODEL = 1024
DEPTH = 2
RET_HEADS = 4
RET_QK = 256
RET_V = 512
RET_CHUNK = 128
ROPE_BASE = 10000.0
SB_HEADS = 16
SB_DIM = 64
D_FF = 4096
N_MOD = 6
EPS = 1e-6
GN_EPS = 1e-5
O_RQ, O_RK, O_RV, O_RG, O_SQ, O_SK, O_SV, O_GA, O_GB = 0, 1024, 2048, 4096, 6144, 7168, 8192, 9216, 10240
IN_W = 11264

ADAM_LR, ADAM_B1, ADAM_B2, ADAM_EPS, ADAM_WD, ADAM_STEP = 0.001, 0.9, 0.999, 1e-08, 0.01, 10

VMEM_LIMIT = 56 * 1024 * 1024


def _cparams(sem):
    return pltpu.CompilerParams(dimension_semantics=sem, vmem_limit_bytes=VMEM_LIMIT)


def _mm(a, b, *, ta=False, tb=False, tm=512, tn=512, tk=512, a_ex=(), pro=None, o_ex=(), epi=None,
        outs=(F32,), name, deps=()):
    if ta:
        K, M = a.shape
    else:
        M, K = a.shape
    N = b.shape[0] if tb else b.shape[1]
    tm, tn, tk = min(tm, M), min(tn, N), min(tk, K)
    assert M % tm == 0 and N % tn == 0 and K % tk == 0, (name, M, N, K, tm, tn, tk)
    nk = K // tk
    in_specs = [
        pl.BlockSpec((tk, tm), lambda i, j, k: (k, i)) if ta else pl.BlockSpec((tm, tk), lambda i, j, k: (i, k)),
        pl.BlockSpec((tn, tk), lambda i, j, k: (j, k)) if tb else pl.BlockSpec((tk, tn), lambda i, j, k: (k, j)),
    ]
    args = [a, b]
    for arr, kind in a_ex:
        if kind == 'a':
            in_specs.append(in_specs[0])
        elif kind == 'k':
            in_specs.append(pl.BlockSpec((tk, 1), lambda i, j, k: (k, 0)) if ta
                            else pl.BlockSpec((1, tk), lambda i, j, k: (0, k)))
        else:
            in_specs.append(pl.BlockSpec((1, tm), lambda i, j, k: (0, i)) if ta
                            else pl.BlockSpec((tm, 1), lambda i, j, k: (i, 0)))
        args.append(arr)
    for arr, kind in o_ex:
        if kind == 'o':
            in_specs.append(pl.BlockSpec((tm, tn), lambda i, j, k: (i, j)))
        elif kind == 'n':
            in_specs.append(pl.BlockSpec((1, tn), lambda i, j, k: (0, j)))
        else:
            in_specs.append(pl.BlockSpec((tm, 1), lambda i, j, k: (i, 0)))
        args.append(arr)
    for arr in deps:
        in_specs.append(pl.BlockSpec(memory_space=pl.ANY))
        args.append(arr)
    na, no, nout, nd = len(a_ex), len(o_ex), len(outs), len(deps)
    dims = (((0 if ta else 1,), (1 if tb else 0,)), ((), ()))

    def body(*refs):
        a_ref, b_ref = refs[0], refs[1]
        aex = refs[2:2 + na]
        oex = refs[2 + na:2 + na + no]
        out_refs = refs[2 + na + no + nd:2 + na + no + nd + nout]
        acc = refs[-1]
        k = pl.program_id(2)

        @pl.when(k == 0)
        def _():
            acc[...] = jnp.zeros_like(acc)

        at = a_ref[...]
        if pro is not None:
            at = pro(at, *[r[...] for r in aex])
        acc[...] += lax.dot_general(at.astype(BF16), b_ref[...].astype(BF16), dims, preferred_element_type=F32)

        @pl.when(k == nk - 1)
        def _():
            res = acc[...]
            vals = epi(res, *[r[...] for r in oex]) if epi is not None else (res,)
            for o_ref, v in zip(out_refs, vals):
                o_ref[...] = v.astype(o_ref.dtype)

    res = pl.pallas_call(
        body, name=name, grid=(M // tm, N // tn, nk), in_specs=in_specs,
        out_specs=[pl.BlockSpec((tm, tn), lambda i, j, k: (i, j)) for _ in outs],
        out_shape=[jax.ShapeDtypeStruct((M, N), dt) for dt in outs],
        scratch_shapes=[pltpu.VMEM((tm, tn), F32)],
        compiler_params=_cparams(("parallel", "parallel", "arbitrary")),
    )(*args)
    return res


def _ew(fn, rows, vecs=(), cols=(), outs=(), tr=256, name=None):
    S = rows[0].shape[0]
    tr = min(tr, S)
    assert S % tr == 0
    in_specs, args = [], []
    for r in rows:
        in_specs.append(pl.BlockSpec((tr, r.shape[1]), lambda i: (i, 0)))
        args.append(r)
    for v in vecs:
        in_specs.append(pl.BlockSpec((1, v.shape[1]), lambda i: (0, 0)))
        args.append(v)
    for c in cols:
        in_specs.append(pl.BlockSpec((tr, 1), lambda i: (i, 0)))
        args.append(c)
    out_specs, out_shape = [], []
    for o in outs:
        if o[0] == 'row':
            out_specs.append(pl.BlockSpec((tr, o[1]), lambda i: (i, 0)))
            out_shape.append(jax.ShapeDtypeStruct((S, o[1]), o[2]))
        elif o[0] == 'sum':
            out_specs.append(pl.BlockSpec((1, o[1]), lambda i: (0, 0)))
            out_shape.append(jax.ShapeDtypeStruct((1, o[1]), F32))
        else:
            out_specs.append(pl.BlockSpec((tr, 1), lambda i: (i, 0)))
            out_shape.append(jax.ShapeDtypeStruct((S, 1), o[1]))
    nin = len(args)

    def body(*refs):
        i = pl.program_id(0)
        vals = fn(*[r[...] for r in refs[:nin]])
        for o, o_ref, v in zip(outs, refs[nin:], vals):
            if o[0] == 'sum':
                @pl.when(i == 0)
                def _():
                    o_ref[...] = jnp.zeros_like(o_ref)
                o_ref[...] += jnp.sum(v.astype(F32), axis=0, keepdims=True)
            else:
                o_ref[...] = v.astype(o_ref.dtype)

    return pl.pallas_call(
        body, name=name, grid=(S // tr,), in_specs=in_specs, out_specs=out_specs, out_shape=out_shape,
        compiler_params=_cparams(("arbitrary",)),
    )(*args)


def _sigmoid(x):
    return 1.0 / (1.0 + jnp.exp(-x))


def _ret_consts(S):
    h = np.arange(RET_HEADS, dtype=np.float64)
    log_gamma = np.log1p(-np.power(2.0, -5.0 - h))
    idx = np.arange(RET_CHUNK, dtype=np.float64)
    rel = idx[:, None] - idx[None, :]
    decay = np.where(rel >= 0, np.exp(np.maximum(rel, 0.0) * log_gamma[:, None, None]), 0.0)
    xi = np.exp((idx + 1.0) * log_gamma[:, None])[:, :, None]
    zeta = np.exp((RET_CHUNK - 1.0 - idx) * log_gamma[:, None])[:, :, None]
    gamma_c = np.exp(RET_CHUNK * log_gamma)[:, None, None]
    half = RET_QK // 2
    inv_freq = np.power(ROPE_BASE, -np.arange(half, dtype=np.float64) / half).astype(np.float32)
    ang = np.arange(S, dtype=np.float32)[:, None] * inv_freq[None, :]
    f = lambda t: jnp.asarray(t, F32)
    return dict(decay=f(decay), xi=f(xi), zeta=f(zeta), gc=f(gamma_c), cos=f(np.cos(ang)), sin=f(np.sin(ang)))


def _rot(t, cos, sin):
    half = RET_QK // 2
    t1, t2 = t[:, :half], t[:, half:]
    return jnp.concatenate([t1 * cos - t2 * sin, t1 * sin + t2 * cos], axis=-1)


def _rot_inv(t, cos, sin):
    half = RET_QK // 2
    t1, t2 = t[:, :half], t[:, half:]
    return jnp.concatenate([t1 * cos + t2 * sin, t2 * cos - t1 * sin], axis=-1)


_NT = (((1,), (1,)), ((), ()))
_TN = (((0,), (0,)), ((), ()))


def _dot(a, b):
    return jnp.dot(a, b, preferred_element_type=F32)


def _dot_nt(a, b):
    return lax.dot_general(a, b, _NT, preferred_element_type=F32)


def _dot_tn(a, b):
    return lax.dot_general(a, b, _TN, preferred_element_type=F32)


def _ret_in_specs(C, rev, NC):
    n_of = (lambda n: NC - 1 - n) if rev else (lambda n: n)
    qb, vb = O_RQ // RET_QK, O_RV // RET_V
    kb = O_RK // RET_QK
    return [
        pl.BlockSpec((C, RET_QK), lambda h, n: (n_of(n), qb + h)),
        pl.BlockSpec((C, RET_QK), lambda h, n: (n_of(n), kb + h)),
        pl.BlockSpec((C, RET_V), lambda h, n: (n_of(n), vb + h)),
        pl.BlockSpec((C, RET_QK // 2), lambda h, n: (n_of(n), 0)),
        pl.BlockSpec((C, RET_QK // 2), lambda h, n: (n_of(n), 0)),
        pl.BlockSpec((1, C, C), lambda h, n: (h, 0, 0)),
        pl.BlockSpec((1, C, 1), lambda h, n: (h, 0, 0)),
        pl.BlockSpec((1, C, 1), lambda h, n: (h, 0, 0)),
        pl.BlockSpec((1, 1, 1), lambda h, n: (h, 0, 0)),
    ]


def _ret_fwd(p, rc):
    S = p.shape[0]
    C = RET_CHUNK
    NC = S // C

    def body(q_ref, k_ref, v_ref, cos_ref, sin_ref, dec_ref, xi_ref, zeta_ref, gc_ref, y_ref, rs_ref, r_acc):
        n = pl.program_id(1)

        @pl.when(n == 0)
        def _():
            r_acc[...] = jnp.zeros_like(r_acc)

        cos, sin = cos_ref[...], sin_ref[...]
        q = _rot(q_ref[...].astype(F32), cos, sin).astype(BF16)
        kf = _rot(k_ref[...].astype(F32), cos, sin) * (RET_QK ** -0.5)
        k = kf.astype(BF16)
        v = v_ref[...]
        r = r_acc[...]
        rb = r.astype(BF16)
        rs_ref[0, 0] = rb
        s = (_dot_nt(q, k) * dec_ref[0]).astype(BF16)
        o = _dot(s, v) + _dot(q, rb) * xi_ref[0]
        mu = jnp.mean(o, axis=-1, keepdims=True)
        var = jnp.mean(jnp.square(o - mu), axis=-1, keepdims=True)
        y_ref[...] = ((o - mu) * lax.rsqrt(var + GN_EPS)).astype(y_ref.dtype)
        kz = (kf * zeta_ref[0]).astype(BF16)
        r_acc[...] = r * gc_ref[0] + _dot_tn(kz, v)

    return pl.pallas_call(
        body, name="ret_fwd", grid=(RET_HEADS, NC), in_specs=_ret_in_specs(C, False, NC),
        out_specs=[pl.BlockSpec((C, RET_V), lambda h, n: (n, h)),
                   pl.BlockSpec((1, 1, RET_QK, RET_V), lambda h, n: (h, n, 0, 0))],
        out_shape=[jax.ShapeDtypeStruct((S, RET_HEADS * RET_V), BF16),
                   jax.ShapeDtypeStruct((RET_HEADS, NC, RET_QK, RET_V), BF16)],
        scratch_shapes=[pltpu.VMEM((RET_QK, RET_V), F32)],
        compiler_params=_cparams(("parallel", "arbitrary")),
    )(p, p, p, rc['cos'], rc['sin'], rc['decay'], rc['xi'], rc['zeta'], rc['gc'])


def _ret_bwd(p, rstate, dy, rc):
    S = p.shape[0]
    C = RET_CHUNK
    NC = S // C

    def body(q_ref, k_ref, v_ref, cos_ref, sin_ref, dec_ref, xi_ref, zeta_ref, gc_ref, rs_ref, dy_ref,
             dq_ref, dk_ref, dv_ref, dr_acc):
        t = pl.program_id(1)

        @pl.when(t == 0)
        def _():
            dr_acc[...] = jnp.zeros_like(dr_acc)

        cos, sin = cos_ref[...], sin_ref[...]
        dec, xi, zeta = dec_ref[0], xi_ref[0], zeta_ref[0]
        q = _rot(q_ref[...].astype(F32), cos, sin).astype(BF16)
        kf = _rot(k_ref[...].astype(F32), cos, sin) * (RET_QK ** -0.5)
        k = kf.astype(BF16)
        kz = (kf * zeta).astype(BF16)
        v = v_ref[...]
        rb = rs_ref[0, 0]
        s = (_dot_nt(q, k) * dec).astype(BF16)
        o = _dot(s, v) + _dot(q, rb) * xi
        mu = jnp.mean(o, axis=-1, keepdims=True)
        var = jnp.mean(jnp.square(o - mu), axis=-1, keepdims=True)
        rstd = lax.rsqrt(var + GN_EPS)
        yh = (o - mu) * rstd
        dyf = dy_ref[...].astype(F32)
        do = (dyf - jnp.mean(dyf, axis=-1, keepdims=True) - yh * jnp.mean(dyf * yh, axis=-1, keepdims=True)) * rstd
        dob = do.astype(BF16)
        doxi = (do * xi).astype(BF16)
        dr = dr_acc[...]
        drb = dr.astype(BF16)
        ds = (_dot_nt(dob, v) * dec).astype(BF16)
        dq = _dot(ds, k) + _dot_nt(doxi, rb)
        dk = _dot_tn(ds, q) + _dot_nt(v, drb) * zeta
        dv = _dot_tn(s, dob) + _dot(kz, drb)
        dr_acc[...] = dr * gc_ref[0] + _dot_tn(q, doxi)
        dq_ref[...] = _rot_inv(dq, cos, sin).astype(dq_ref.dtype)
        dk_ref[...] = (_rot_inv(dk, cos, sin) * (RET_QK ** -0.5)).astype(dk_ref.dtype)
        dv_ref[...] = dv.astype(dv_ref.dtype)

    rn = lambda n: NC - 1 - n
    in_specs = _ret_in_specs(C, True, NC) + [
        pl.BlockSpec((1, 1, RET_QK, RET_V), lambda h, n: (h, rn(n), 0, 0)),
        pl.BlockSpec((C, RET_V), lambda h, n: (rn(n), h)),
    ]
    return pl.pallas_call(
        body, name="ret_bwd", grid=(RET_HEADS, NC), in_specs=in_specs,
        out_specs=[pl.BlockSpec((C, RET_QK), lambda h, n: (rn(n), h)),
                   pl.BlockSpec((C, RET_QK), lambda h, n: (rn(n), h)),
                   pl.BlockSpec((C, RET_V), lambda h, n: (rn(n), h))],
        out_shape=[jax.ShapeDtypeStruct((S, RET_HEADS * RET_QK), BF16),
                   jax.ShapeDtypeStruct((S, RET_HEADS * RET_QK), BF16),
                   jax.ShapeDtypeStruct((S, RET_HEADS * RET_V), BF16)],
        scratch_shapes=[pltpu.VMEM((RET_QK, RET_V), F32)],
        compiler_params=_cparams(("parallel", "arbitrary")),
    )(p, p, p, rc['cos'], rc['sin'], rc['decay'], rc['xi'], rc['zeta'], rc['gc'], rstate, dy)


SB_T = 256
SB_SCALE = SB_DIM ** -0.5


def _tri():
    j = np.arange(SB_T)
    rev = (j[:, None] >= j[None, :]).astype(np.float32)
    return jnp.asarray(np.stack([np.concatenate([rev, rev]), np.concatenate([rev.T, rev.T])]), BF16)


def _run_sum(x, tri2):
    hi = x.astype(BF16)
    lo = (x - hi.astype(F32)).astype(BF16)
    return _dot(jnp.concatenate([hi, lo], axis=1), tri2)


def _sb_fwd(p, tri):
    S = p.shape[0]
    T = min(SB_T, S)
    NQ = S // T
    assert NQ <= 128
    qb, kb, vb = O_SQ // 128, O_SK // 128, O_SV // 128

    def body(q_ref, k_ref, v_ref, tri_ref, o_ref, cs_ref, o_acc, run, zbuf, abuf):
        i = pl.program_id(1)
        lane = lax.broadcasted_iota(jnp.int32, (1, 128), 1)
        tri2 = tri_ref[0]
        qs = [jnp.where((lane >= 64) if hh else (lane < 64), q_ref[...], jnp.zeros_like(q_ref[...]))
              * jnp.asarray(SB_SCALE, BF16) for hh in range(2)]
        cs_ref[...] = jnp.zeros_like(cs_ref)
        o_acc[...] = jnp.zeros_like(o_acc)
        run[...] = jnp.zeros_like(run)

        def kv(ref, j):
            return ref[pl.ds(pl.multiple_of(j * T, T), T), :]

        for hh in range(2):
            zbuf[hh] = _dot_nt(qs[hh], kv(k_ref, i))

        def block(t, diagonal):
            j = i - t
            zn = [_dot_nt(qs[hh], kv(k_ref, jnp.maximum(j - 1, 0))) for hh in range(2)]
            if not diagonal:
                av = [_dot(abuf[hh], kv(v_ref, j + 1)) for hh in range(2)]
            else:
                msk = lax.broadcasted_iota(jnp.int32, (T, T), 1) < lax.broadcasted_iota(jnp.int32, (T, T), 0)
            zs = [zbuf[hh] for hh in range(2)]
            cls = []
            for hh in range(2):
                sp = jnp.maximum(zs[hh], 0.0) + jnp.log(1.0 + jnp.exp(-jnp.abs(zs[hh])))
                if diagonal:
                    sp = jnp.where(msk, sp, 0.0)
                cls.append(_run_sum(sp, tri2))
            for hh in range(2):
                csl = slice(hh * 128, (hh + 1) * 128)
                cs = run[hh]
                a = jnp.exp(zs[hh] - cls[hh] - cs)
                if diagonal:
                    a = jnp.where(msk, a, 0.0)
                abuf[hh] = a.astype(BF16)
                cs_ref[:, csl] = jnp.where(lane == j, cs, cs_ref[:, csl])
                run[hh] = cs + cls[hh][:, 0:1]
            for hh in range(2):
                if not diagonal:
                    o_acc[hh] += av[hh]
                zbuf[hh] = zn[hh]

        block(0, True)

        def step(t, carry):
            block(t, False)
            return carry

        lax.fori_loop(1, i + 1, step, 0)
        o_ref[...] = jnp.where(lane < 64, o_acc[0] + _dot(abuf[0], kv(v_ref, 0)),
                               o_acc[1] + _dot(abuf[1], kv(v_ref, 0))).astype(o_ref.dtype)

    return pl.pallas_call(
        body, name="sb_fwd", grid=(SB_HEADS // 2, NQ),
        scratch_shapes=[pltpu.VMEM((2, T, 128), F32), pltpu.VMEM((2, T, 1), F32), pltpu.VMEM((2, T, T), F32),
                        pltpu.VMEM((2, T, T), BF16)],
        in_specs=[pl.BlockSpec((T, 128), lambda h, i: (i, qb + h)),
                  pl.BlockSpec((S, 128), lambda h, i: (0, kb + h)),
                  pl.BlockSpec((S, 128), lambda h, i: (0, vb + h)),
                  pl.BlockSpec((1, 2 * T, T), lambda h, i: (0, 0, 0))],
        out_specs=[pl.BlockSpec((T, 128), lambda h, i: (i, h)),
                   pl.BlockSpec((T, 256), lambda h, i: (i, h))],
        out_shape=[jax.ShapeDtypeStruct((S, SB_HEADS * SB_DIM), BF16),
                   jax.ShapeDtypeStruct((S, SB_HEADS * 128), F32)],
        compiler_params=_cparams(("parallel", "arbitrary")),
    )(p, p, p, tri)


def _sb_bwd(p, carries, dy, tri):
    S = p.shape[0]
    T = min(SB_T, S)
    NQ = S // T
    qb, kb, vb = O_SQ // 128, O_SK // 128, O_SV // 128

    def body(q_ref, k_ref, v_ref, cs_ref, dy_ref, tri_ref, dq_ref, dk_ref, dv_ref, dk_acc, dv_acc, dq_acc, run,
             zbuf, dabuf, dzbuf, abuf):
        i = pl.program_id(1)

        @pl.when(i == 0)
        def _():
            dk_acc[...] = jnp.zeros_like(dk_acc)
            dv_acc[...] = jnp.zeros_like(dv_acc)

        lane = lax.broadcasted_iota(jnp.int32, (1, 128), 1)
        rev2, fwd2 = tri_ref[0], tri_ref[1]
        hms = [(lane >= 64) if hh else (lane < 64) for hh in range(2)]
        qs = [jnp.where(hm, q_ref[...], jnp.zeros_like(q_ref[...])) * jnp.asarray(SB_SCALE, BF16) for hm in hms]
        dos = [jnp.where(hm, dy_ref[...], jnp.zeros_like(dy_ref[...])) for hm in hms]
        dq_acc[...] = jnp.zeros_like(dq_acc)
        run[...] = jnp.zeros_like(run)
        dzbuf[...] = jnp.zeros_like(dzbuf)
        abuf[...] = jnp.zeros_like(abuf)

        def kv(ref, j):
            return ref[pl.ds(pl.multiple_of(j * T, T), T), :]

        def flush(jp):
            kp = kv(k_ref, jp)
            dq_add = [_dot(dzbuf[hh], kp) for hh in range(2)]
            dk_add = _dot_tn(dzbuf[0], qs[0]) + _dot_tn(dzbuf[1], qs[1])
            dv_add = _dot_tn(abuf[0], dos[0]) + _dot_tn(abuf[1], dos[1])
            return dq_add, dk_add, dv_add

        def apply(jp, adds):
            dq_add, dk_add, dv_add = adds
            rows = pl.ds(pl.multiple_of(jp * T, T), T)
            for hh in range(2):
                dq_acc[hh] += dq_add[hh]
            dk_acc[rows, :] += dk_add
            dv_acc[rows, :] += dv_add

        for hh in range(2):
            zbuf[hh] = _dot_nt(qs[hh], kv(k_ref, 0))
            dabuf[hh] = _dot_nt(dos[hh], kv(v_ref, 0))

        def block(j, diagonal):
            jp = jnp.maximum(j - 1, 0)
            if not diagonal:
                zn = [_dot_nt(qs[hh], kv(k_ref, j + 1)) for hh in range(2)]
                dan = [_dot_nt(dos[hh], kv(v_ref, j + 1)) for hh in range(2)]
            else:
                msk = lax.broadcasted_iota(jnp.int32, (T, T), 1) < lax.broadcasted_iota(jnp.int32, (T, T), 0)
            adds = flush(jp)
            zs = [zbuf[hh] for hh in range(2)]
            sigs, cls = [], []
            for hh in range(2):
                e = jnp.exp(-jnp.abs(zs[hh]))
                sp = jnp.maximum(zs[hh], 0.0) + jnp.log(1.0 + e)
                r = 1.0 / (1.0 + e)
                sigs.append(jnp.where(zs[hh] >= 0, r, e * r))
                if diagonal:
                    sp = jnp.where(msk, sp, 0.0)
                cls.append(_run_sum(sp, rev2))
            pgs, gs = [], []
            for hh in range(2):
                csl = slice(hh * 128, (hh + 1) * 128)
                cs = jnp.sum(jnp.where(lane == j, cs_ref[:, csl], 0.0), axis=-1, keepdims=True)
                a = jnp.exp(zs[hh] - cls[hh] - cs)
                if diagonal:
                    a = jnp.where(msk, a, 0.0)
                abuf_new = a.astype(BF16)
                g = a * dabuf[hh]
                gs.append((g, abuf_new))
                pgs.append(_run_sum(g, fwd2))
            for hh in range(2):
                g, abuf_new = gs[hh]
                cg = run[hh]
                dz = g - sigs[hh] * (cg + pgs[hh])
                if diagonal:
                    dz = jnp.where(msk, dz, 0.0)
                run[hh] = cg + pgs[hh][:, T - 1:T]
                dzbuf[hh] = dz.astype(BF16)
                abuf[hh] = abuf_new
            apply(jp, adds)
            if not diagonal:
                for hh in range(2):
                    zbuf[hh] = zn[hh]
                    dabuf[hh] = dan[hh]

        def step(j, carry):
            block(j, False)
            return carry

        lax.fori_loop(0, i, step, 0)
        block(i, True)
        apply(i, flush(i))
        dq_ref[...] = (jnp.where(lane < 64, dq_acc[0], dq_acc[1]) * SB_SCALE).astype(dq_ref.dtype)

        @pl.when(i == NQ - 1)
        def _():
            dk_ref[...] = dk_acc[...].astype(dk_ref.dtype)
            dv_ref[...] = dv_acc[...].astype(dv_ref.dtype)

    W = SB_HEADS * SB_DIM
    return pl.pallas_call(
        body, name="sb_bwd", grid=(SB_HEADS // 2, NQ),
        in_specs=[pl.BlockSpec((T, 128), lambda h, i: (i, qb + h)),
                  pl.BlockSpec((S, 128), lambda h, i: (0, kb + h)),
                  pl.BlockSpec((S, 128), lambda h, i: (0, vb + h)),
                  pl.BlockSpec((T, 256), lambda h, i: (i, h)),
                  pl.BlockSpec((T, 128), lambda h, i: (i, h)),
                  pl.BlockSpec((2, 2 * T, T), lambda h, i: (0, 0, 0))],
        out_specs=[pl.BlockSpec((T, 128), lambda h, i: (i, h)),
                   pl.BlockSpec((S, 128), lambda h, i: (0, h)),
                   pl.BlockSpec((S, 128), lambda h, i: (0, h))],
        out_shape=[jax.ShapeDtypeStruct((S, W), BF16)] * 3,
        scratch_shapes=[pltpu.VMEM((S, 128), F32), pltpu.VMEM((S, 128), F32), pltpu.VMEM((2, T, 128), F32),
                        pltpu.VMEM((2, T, 1), F32), pltpu.VMEM((2, T, T), F32), pltpu.VMEM((2, T, T), F32),
                        pltpu.VMEM((2, T, T), BF16), pltpu.VMEM((2, T, T), BF16)],
        compiler_params=_cparams(("parallel", "arbitrary")),
    )(p, p, p, carries, dy, tri)


def _exchange(srcs, out_shapes, src_slice, dst_slice, name):
    n = len(srcs)

    def body(*refs):
        ins, outs = refs[:n], refs[n:2 * n]
        send_sems, recv_sems, loc_sems = refs[2 * n:]
        x, y, c = lax.axis_index("x"), lax.axis_index("y"), lax.axis_index("c")
        me = 4 * x + 2 * y + c
        local = [pltpu.make_async_copy(src_slice(t, ins[t], me), dst_slice(t, outs[t], me), loc_sems.at[t])
                 for t in range(n)]
        for cp in local:
            cp.start()
        sends, recvs = [], []
        for k in (1, 2, 4, 6, 3, 5, 7):
            px = 1 - x if k & 4 else x
            py = 1 - y if k & 2 else y
            pc = 1 - c if k & 1 else c
            peer = 4 * px + 2 * py + pc
            for t in range(n):
                s = t * 7 + k - 1
                sends.append(pltpu.make_async_remote_copy(
                    src_ref=src_slice(t, ins[t], peer), dst_ref=dst_slice(t, outs[t], me),
                    send_sem=send_sems.at[s], recv_sem=recv_sems.at[s],
                    device_id=(px, py, pc), device_id_type=pl.DeviceIdType.MESH))
                recvs.append(pltpu.make_async_remote_copy(
                    src_ref=src_slice(t, ins[t], me), dst_ref=dst_slice(t, outs[t], peer),
                    send_sem=send_sems.at[s], recv_sem=recv_sems.at[s],
                    device_id=(px, py, pc), device_id_type=pl.DeviceIdType.MESH))
        for cp in sends:
            cp.start()
        for cp in recvs:
            cp.wait_recv()
        for cp in sends:
            cp.wait_send()
        for cp in local:
            cp.wait()

    anyspec = pl.BlockSpec(memory_space=pl.ANY)
    return pl.pallas_call(
        body, name=name, in_specs=[anyspec] * n, out_specs=[anyspec] * n,
        out_shape=[jax.ShapeDtypeStruct(s, d) for s, d in out_shapes],
        scratch_shapes=[pltpu.SemaphoreType.DMA((7 * n,)), pltpu.SemaphoreType.DMA((7 * n,)),
                        pltpu.SemaphoreType.DMA((n,))],
    )(*srcs)


def _all_gather_lead(xs, name):
    return _exchange(
        xs, [((N_DEV,) + x.shape, x.dtype) for x in xs],
        lambda t, ref, peer: ref, lambda t, ref, who: ref.at[who], name)


def _all_to_all_lead(xs, name):
    return _exchange(
        xs, [(x.shape, x.dtype) for x in xs],
        lambda t, ref, peer: ref.at[peer], lambda t, ref, who: ref.at[who], name)


_W_AXIS = {"w_in": 1, "w_ret_out": 0, "w_sb_out": 0, "w_mix_out": 0, "w_up": 1, "w_down": 0}
_W_NAMES = tuple(_W_AXIS)


def _window(ref, axis, who, width, count=1):
    start = pl.multiple_of(who * width, width)
    return ref.at[pl.ds(start, count * width), :] if axis == 0 else ref.at[:, pl.ds(start, count * width)]


_HBM = pl.BlockSpec(memory_space=pltpu.HBM)
_SEM = pl.BlockSpec(memory_space=pltpu.SEMAPHORE)
_EFFECT = pltpu.SideEffectType.DATAFLOW_SIDE_EFFECTING


def _exchange_start(srcs, shapes, src_slice, dst_slice, name, deps=()):
    n, nd = len(srcs), len(deps)
    lands = [pltpu.with_memory_space_constraint(lax.empty(s, d), pltpu.HBM) for s, d in shapes]

    def body(*refs):
        ins, lnd = refs[:n], refs[n:2 * n]
        sems = refs[2 * n + nd:4 * n + nd]
        token = refs[6 * n + nd]
        x, y, c = lax.axis_index("x"), lax.axis_index("y"), lax.axis_index("c")
        me = 4 * x + 2 * y + c
        for k in (0, 1, 2, 4, 6, 3, 5, 7):
            px = 1 - x if k & 4 else x
            py = 1 - y if k & 2 else y
            pc = 1 - c if k & 1 else c
            peer = 4 * px + 2 * py + pc
            for t in range(n):
                pltpu.make_async_remote_copy(
                    src_ref=src_slice(t, ins[t], peer), dst_ref=dst_slice(t, lnd[t], me),
                    send_sem=sems[2 * t], recv_sem=sems[2 * t + 1],
                    device_id=(px, py, pc), device_id_type=pl.DeviceIdType.MESH).start()
        token[...] = jnp.zeros_like(token)

    res = pl.pallas_call(
        body, name=name, in_specs=[_HBM] * (2 * n) + [pl.BlockSpec(memory_space=pl.ANY)] * nd,
        out_specs=[_SEM] * (2 * n) + [_HBM] * (2 * n) + [pl.BlockSpec(memory_space=pltpu.VMEM)],
        out_shape=[pltpu.SemaphoreType.DMA(())] * (2 * n) + [pltpu.HBM(s.shape, s.dtype) for s in srcs]
        + [pltpu.HBM(s.shape, s.dtype) for s in lands] + [jax.ShapeDtypeStruct((8, 128), F32)],
        input_output_aliases={t: 2 * n + t for t in range(2 * n)},
        compiler_params=pltpu.CompilerParams(has_side_effects=_EFFECT),
    )(*[pltpu.with_memory_space_constraint(s, pltpu.HBM) for s in srcs], *lands, *deps)
    return dict(n=n, sems=res[:2 * n], srcs=res[2 * n:3 * n], lands=res[3 * n:4 * n], token=res[4 * n])


def _exchange_wait(h, after, name):
    n = h['n']

    def body(*refs):
        lnd = refs[n:2 * n]
        sems = refs[2 * n:4 * n]
        x, y, c = lax.axis_index("x"), lax.axis_index("y"), lax.axis_index("c")
        for t in range(n):
            w = lnd[t]
            cp = pltpu.make_async_remote_copy(src_ref=w, dst_ref=w, send_sem=sems[2 * t], recv_sem=sems[2 * t + 1],
                                              device_id=(x, y, 1 - c), device_id_type=pl.DeviceIdType.MESH)
            cp.wait_send()
            cp.wait_recv()

    after = list(after)
    res = pl.pallas_call(
        body, name=name,
        in_specs=[_HBM] * (2 * n) + [_SEM] * (2 * n) + [pl.BlockSpec(memory_space=pl.ANY)] * len(after),
        out_specs=[_HBM] * (2 * n),
        out_shape=[pltpu.HBM(s.shape, s.dtype) for s in h['srcs']] + [pltpu.HBM(s.shape, s.dtype) for s in h['lands']],
        input_output_aliases={t: t for t in range(2 * n)},
        compiler_params=pltpu.CompilerParams(has_side_effects=_EFFECT),
    )(*h['srcs'], *h['lands'], *h['sems'], *after)
    return list(res[n:])


def _gather_start(shards, names, tag, deps=()):
    xs = [shards[nm] for nm in names]
    axes = [_W_AXIS[nm] for nm in names]
    widths = [x.shape[ax] for x, ax in zip(xs, axes)]
    shapes = [(tuple(d * (N_DEV if a == ax else 1) for a, d in enumerate(x.shape)), x.dtype) for x, ax in zip(xs, axes)]
    src = lambda t, ref, peer: ref
    dst = lambda t, ref, who: _window(ref, axes[t], who, widths[t])
    h = _exchange_start(xs, shapes, src, dst, "gw_start_" + tag, deps)
    h['tag'] = "gw_wait_" + tag
    return h


def _scatter_start(grads, names, tag):
    xs = [grads[nm] for nm in names]
    axes = [_W_AXIS[nm] for nm in names]
    widths = [x.shape[ax] // N_DEV for x, ax in zip(xs, axes)]
    shapes = [((N_DEV,) + tuple(d // (N_DEV if a == ax else 1) for a, d in enumerate(x.shape)), x.dtype)
              for x, ax in zip(xs, axes)]
    src = lambda t, ref, peer: _window(ref, axes[t], peer, widths[t])
    dst = lambda t, ref, who: ref.at[who]
    h = _exchange_start(xs, shapes, src, dst, "sg_start_" + tag)
    h['tag'] = "sg_wait_" + tag
    return h


def _finish(h, after):
    return _exchange_wait(h, after, h['tag'])


class _LayerWeights:
    def __init__(self, groups, started):
        self.groups = groups
        self.started = started
        self.got = {}
        self.after = None

    def __getitem__(self, nm):
        if nm not in self.got:
            for names, h in self.groups:
                if nm in names:
                    self.got.update(zip(names, _finish(h, list(self.after) + self.started)))
        return self.got[nm]


def _adam(parts, w, m, v, name, tr=256):
    P, R, C = parts.shape
    tr = min(tr, R)
    assert R % tr == 0
    bc1 = 1.0 / (1.0 - ADAM_B1 ** ADAM_STEP)
    bc2 = 1.0 / (1.0 - ADAM_B2 ** ADAM_STEP)

    def body(p_ref, w_ref, m_ref, v_ref, g_out, d_out, m_out, v_out):
        g = p_ref[0].astype(F32)
        for s in range(1, P):
            g = g + p_ref[s].astype(F32)
        mm = ADAM_B1 * m_ref[...] + (1.0 - ADAM_B1) * g
        vv = ADAM_B2 * v_ref[...] + (1.0 - ADAM_B2) * jnp.square(g)
        g_out[...] = g
        m_out[...] = mm
        v_out[...] = vv
        d_out[...] = -ADAM_LR * ((mm * bc1) / (jnp.sqrt(vv * bc2) + ADAM_EPS) + ADAM_WD * w_ref[...])

    spec = pl.BlockSpec((tr, C), lambda i: (i, 0))
    return pl.pallas_call(
        body, name=name, grid=(R // tr,),
        in_specs=[pl.BlockSpec((P, tr, C), lambda i: (0, i, 0)), spec, spec, spec],
        out_specs=[spec] * 4, out_shape=[jax.ShapeDtypeStruct((R, C), F32)] * 4,
        compiler_params=_cparams(("parallel",)),
    )(parts, w, m, v)


def _mod_partial(cact_all, w_ada_l, b_ada_l):
    def body(c_ref, w_ref, b_ref, o_ref):
        o_ref[...] = _dot(c_ref[...].astype(BF16), w_ref[...].astype(BF16)) + b_ref[...]

    return pl.pallas_call(
        body, name="mod_partial", out_shape=jax.ShapeDtypeStruct((cact_all.shape[0], w_ada_l.shape[1]), F32),
        compiler_params=pltpu.CompilerParams(vmem_limit_bytes=VMEM_LIMIT),
    )(cact_all, w_ada_l, b_ada_l)


def _ada_grad(cact_t, dmod):
    D, n = cact_t.shape[0], dmod.shape[1]

    def body(c_ref, d_ref, o_ref):
        ct = c_ref[...].astype(BF16).astype(F32)
        dm = d_ref[...].astype(BF16).astype(F32)
        acc = ct[:, 0:1] * dm[0:1, :]
        for b in range(1, N_DEV):
            acc = acc + ct[:, b:b + 1] * dm[b:b + 1, :]
        o_ref[0] = acc

    return pl.pallas_call(
        body, name="ada_grad", out_shape=jax.ShapeDtypeStruct((1, D, n), F32),
        compiler_params=pltpu.CompilerParams(vmem_limit_bytes=VMEM_LIMIT),
    )(cact_t, dmod)


def _norm_mod(x, r, gv, sh):
    return x * r * gv + sh


def _silu(x):
    return x * _sigmoid(x)


def _layer_fwd(x0, mod, gn1, gn2, W, rc, tri):
    S = x0.shape[0]
    sh1, sc1, g1m, sh2, sc2, g2m = [mod[i:i + 1] for i in range(N_MOD)]
    gv1 = gn1 * (1.0 + sc1)
    gv2 = gn2 * (1.0 + sc2)
    (r1,) = _ew(lambda x: (lax.rsqrt(jnp.mean(x * x, axis=-1, keepdims=True) + EPS),), [x0], outs=[('col', F32)],
                name="row_rstd")
    W.after = [r1, gv1]
    (p,) = _mm(x0, W["w_in"], tm=1024, tn=1024, tk=512, a_ex=[(r1, 'm'), (gv1, 'k'), (sh1, 'k')], pro=_norm_mod,
               outs=(BF16,), name="mm_in")
    yret, rstate = _ret_fwd(p, rc)
    ysb, sbc = _sb_fwd(p, tri)
    W.after = [ysb]
    rg = (p, O_RG)
    (ya,) = _mm_cols(yret, W["w_ret_out"], a_cols=[rg], pro=lambda yr, g: _silu(g.astype(F32)) * yr.astype(F32),
                     outs=(BF16,), name="mm_ret_out")
    yb, mg = _mm_cols(ysb, W["w_sb_out"], o_cols=[(ya, 0), (p, O_GA), (p, O_GB)],
                      epi=lambda acc, a, ga, gb: (acc, _sigmoid(ga.astype(F32)) * a.astype(F32)
                                                  + _sigmoid(gb.astype(F32)) * acc),
                      outs=(BF16, BF16), name="mm_sb_out")
    mo, x1 = _mm(mg, W["w_mix_out"], tm=1024, tn=1024, tk=512, o_ex=[(x0, 'o'), (g1m, 'n')],
                 epi=lambda acc, x, g: (acc, x + g * acc), outs=(BF16, F32), name="mm_mix_out")
    (r2,) = _ew(lambda x: (lax.rsqrt(jnp.mean(x * x, axis=-1, keepdims=True) + EPS),), [x1], outs=[('col', F32)],
                name="row_rstd")
    (act,) = _mm(x1, W["w_up"], tm=1024, tn=1024, tk=512, a_ex=[(r2, 'm'), (gv2, 'k'), (sh2, 'k')], pro=_norm_mod,
                 epi=lambda acc: (jnp.maximum(acc, 0.0),), outs=(BF16,), name="mm_up")
    dn, x2 = _mm(act, W["w_down"], tm=1024, tn=1024, tk=512, pro=lambda a: jnp.square(a.astype(F32)),
                 o_ex=[(x1, 'o'), (g2m, 'n')], epi=lambda acc, x, g: (acc, x + g * acc), outs=(BF16, F32),
                 name="mm_down")
    saved = dict(x0=x0, r1=r1, p=p, yret=yret, rstate=rstate, ysb=ysb, sbc=sbc, ya=ya, yb=yb, mg=mg, mo=mo, x1=x1, r2=r2,
                 act=act, dn=dn, gv1=gv1, gv2=gv2, mod=mod, gn1=gn1, gn2=gn2)
    return x2, saved


def _mm_cols(a, b, *, a_cols=(), o_cols=(), pro=None, epi=None, outs, name, ta=False, tb=False):
    if ta:
        K, M = a.shape
    else:
        M, K = a.shape
    N = b.shape[0] if tb else b.shape[1]
    tm, tn, tk = min(1024, M), min(1024, N), min(512, K)
    nk = K // tk
    dims = (((0 if ta else 1,), (1 if tb else 0,)), ((), ()))
    in_specs = [
        pl.BlockSpec((tk, tm), lambda i, j, k: (k, i)) if ta else pl.BlockSpec((tm, tk), lambda i, j, k: (i, k)),
        pl.BlockSpec((tn, tk), lambda i, j, k: (j, k)) if tb else pl.BlockSpec((tk, tn), lambda i, j, k: (k, j)),
    ]
    args = [a, b]
    for arr, off in a_cols:
        if ta:
            assert off % tm == 0
            in_specs.append(pl.BlockSpec((tk, tm), lambda i, j, k, o=off // tm: (k, o + i)))
        else:
            assert off % tk == 0
            in_specs.append(pl.BlockSpec((tm, tk), lambda i, j, k, o=off // tk: (i, o + k)))
        args.append(arr)
    for arr, off in o_cols:
        assert off % tn == 0
        in_specs.append(pl.BlockSpec((tm, tn), lambda i, j, k, o=off // tn: (i, o + j)))
        args.append(arr)
    na, no, nout = len(a_cols), len(o_cols), len(outs)

    def body(*refs):
        a_ref, b_ref = refs[0], refs[1]
        aex = refs[2:2 + na]
        oex = refs[2 + na:2 + na + no]
        out_refs = refs[2 + na + no:2 + na + no + nout]
        acc = refs[-1]
        k = pl.program_id(2)

        @pl.when(k == 0)
        def _():
            acc[...] = jnp.zeros_like(acc)

        at = a_ref[...]
        if pro is not None:
            at = pro(at, *[r[...] for r in aex])
        acc[...] += lax.dot_general(at.astype(BF16), b_ref[...].astype(BF16), dims, preferred_element_type=F32)

        @pl.when(k == nk - 1)
        def _():
            res = acc[...]
            vals = epi(res, *[r[...] for r in oex]) if epi is not None else (res,)
            for o_ref, v in zip(out_refs, vals):
                o_ref[...] = v.astype(o_ref.dtype)

    return pl.pallas_call(
        body, name=name, grid=(M // tm, N // tn, nk), in_specs=in_specs,
        out_specs=[pl.BlockSpec((tm, tn), lambda i, j, k: (i, j)) for _ in outs],
        out_shape=[jax.ShapeDtypeStruct((M, N), dt) for dt in outs],
        scratch_shapes=[pltpu.VMEM((tm, tn), F32)],
        compiler_params=_cparams(("parallel", "parallel", "arbitrary")),
    )(*args)


def _norm_bwd(dh, x, r, dres, gv, gn, extra_rows=(), extra_vecs=(), extra_fn=None, extra_outs=(), name="norm_bwd"):
    D = x.shape[1]
    ne = len(extra_rows)

    def fn(dh_t, x_t, dres_t, *rest):
        er, rest = rest[:ne], rest[ne:]
        gv_t = rest[0]
        ev, r_t = rest[1:-1], rest[-1]
        xh = x_t * r_t
        dxh = dh_t * gv_t
        dx = r_t * (dxh - xh * jnp.mean(dxh * xh, axis=-1, keepdims=True)) + dres_t
        base = (dx, dh_t, dh_t * xh)
        if extra_fn is None:
            return base
        return base + tuple(extra_fn(dx, *er, *ev))

    return _ew(fn, [dh, x, dres] + list(extra_rows), vecs=[gv] + list(extra_vecs), cols=[r],
               outs=[('row', D, F32), ('sum', D), ('sum', D)] + list(extra_outs), name=name)


def _layer_bwd(dx2, sv, W, rc, tri, emit):
    mod = sv['mod']
    sh1, sc1, g1m, sh2, sc2, g2m = [mod[i:i + 1] for i in range(N_MOD)]
    D = D_MODEL
    p = sv['p']
    d_g2m, d_dn = _ew(lambda dx, dn, g: (dx * dn.astype(F32), dx * g), [dx2, sv['dn']], vecs=[g2m],
                      outs=[('sum', D), ('row', D, BF16)], name="gate_bwd")
    (d_up,) = _mm(d_dn, W["w_down"], tb=True, tm=1024, tn=1024, tk=512, o_ex=[(sv['act'], 'o')],
                  epi=lambda acc, a: (acc * 2.0 * a.astype(F32),), outs=(BF16,), name="mm_down_dx")
    (gw_down,) = _mm(sv['act'], d_dn, ta=True, tm=1024, tn=1024, tk=512, pro=lambda a: jnp.square(a.astype(F32)),
                     outs=(BF16,), name="mm_down_dw")
    (gw_up,) = _mm(sv['x1'], d_up, ta=True, tm=1024, tn=1024, tk=512,
                   a_ex=[(sv['r2'], 'k'), (sv['gv2'].reshape(1, D), 'm'), (sh2, 'm')], pro=_norm_mod,
                   outs=(BF16,), name="mm_up_dw")
    tok = emit(dict(w_down=gw_down, w_up=gw_up), "mlp")
    (d_h2,) = _mm(d_up, W["w_up"], tb=True, tm=1024, tn=1024, tk=512, outs=(F32,), name="mm_up_dx", deps=[tok])
    dx1, d_sh2, s_h2, d_g1m, d_mo = _norm_bwd(
        d_h2, sv['x1'], sv['r2'], dx2, sv['gv2'], sv['gn2'],
        extra_rows=[sv['mo']], extra_vecs=[g1m],
        extra_fn=lambda dx, mo, g: (dx * mo.astype(F32), dx * g),
        extra_outs=[('sum', D), ('row', D, BF16)], name="norm_bwd_mlp")
    d_sc2 = sv['gn2'] * s_h2
    d_gn2 = (1.0 + sc2) * s_h2
    def mix_epi(acc, ya, yb, ga, gb):
        sa, sb = _sigmoid(ga.astype(F32)), _sigmoid(gb.astype(F32))
        return (acc * sa, acc * sb, acc * ya.astype(F32) * sa * (1.0 - sa), acc * yb.astype(F32) * sb * (1.0 - sb))

    d_ya, d_yb, d_ga, d_gb = _mm_cols(d_mo, W["w_mix_out"], tb=True,
                                      o_cols=[(sv['ya'], 0), (sv['yb'], 0), (p, O_GA), (p, O_GB)], epi=mix_epi,
                                      outs=(BF16,) * 4, name="mm_mix_dx")
    (gw_mix,) = _mm(sv['mg'], d_mo, ta=True, tm=1024, tn=1024, tk=512, outs=(BF16,), name="mm_mix_dw")

    def ro_epi(acc, g, yr):
        gf = g.astype(F32)
        s = _sigmoid(gf)
        return (acc * yr.astype(F32) * s * (1.0 + gf * (1.0 - s)), acc * gf * s)

    d_rg, d_yret = _mm_cols(d_ya, W["w_ret_out"], tb=True, o_cols=[(p, O_RG), (sv['yret'], 0)], epi=ro_epi,
                            outs=(BF16, BF16), name="mm_ret_dx")
    (gw_ro,) = _mm_cols(sv['yret'], d_ya, ta=True, a_cols=[(p, O_RG)],
                        pro=lambda yr, g: _silu(g.astype(F32)) * yr.astype(F32), outs=(BF16,), name="mm_ret_dw")
    (gw_so,) = _mm(sv['ysb'], d_yb, ta=True, tm=1024, tn=1024, tk=512, outs=(BF16,), name="mm_sb_dw")
    tok = emit(dict(w_mix_out=gw_mix, w_ret_out=gw_ro, w_sb_out=gw_so), "mix")
    (d_ysb,) = _mm(d_yb, W["w_sb_out"], tb=True, tm=1024, tn=1024, tk=512, outs=(BF16,), name="mm_sb_dx", deps=[tok])
    d_sq, d_sk, d_sv = _sb_bwd(p, sv['sbc'], d_ysb, tri)
    d_rq, d_rk, d_rv = _ret_bwd(p, sv['rstate'], d_yret, rc)
    dp = jnp.concatenate([d_rq, d_rk, d_rv, d_rg, d_sq, d_sk, d_sv, d_ga, d_gb], axis=1)
    (gw_in,) = _mm(sv['x0'], dp, ta=True, tm=1024, tn=1024, tk=512,
                   a_ex=[(sv['r1'], 'k'), (sv['gv1'].reshape(1, D), 'm'), (sh1, 'm')], pro=_norm_mod,
                   outs=(BF16,), name="mm_in_dw")
    tok = emit(dict(w_in=gw_in), "in")
    (d_h,) = _mm(dp, W["w_in"], tb=True, tm=1024, tn=1024, tk=512, outs=(F32,), name="mm_in_dx", deps=[tok])
    dx0, d_sh1, s_h1 = _norm_bwd(d_h, sv['x0'], sv['r1'], dx1, sv['gv1'], sv['gn1'], name="norm_bwd_mix")
    d_sc1 = sv['gn1'] * s_h1
    d_gn1 = (1.0 + sc1) * s_h1
    d_mod = jnp.concatenate([d_sh1, d_sc1, d_g1m, d_sh2, d_sc2, d_g2m], axis=1)
    return dx0, d_mod, d_gn1, d_gn2


def kernel(x, c, norm_mix_g, w_in, w_ret_out, w_sb_out, w_mix_out, norm_mlp_g, w_up, w_down, w_ada, b_ada, final_g, loss_target, m_norm_mix_g, m_w_in, m_w_ret_out, m_w_sb_out, m_w_mix_out, m_norm_mlp_g, m_w_up, m_w_down, m_w_ada, m_b_ada, m_final_g, v_norm_mix_g, v_w_in, v_w_ret_out, v_w_sb_out, v_w_mix_out, v_norm_mlp_g, v_w_up, v_w_down, v_w_ada, v_b_ada, v_final_g):
    S, D = x.shape[1], x.shape[2]
    x0 = x.reshape(S, D)
    tgt = loss_target.reshape(S, D)
    me = 4 * lax.axis_index("x") + 2 * lax.axis_index("y") + lax.axis_index("c")
    wts = dict(w_in=w_in, w_ret_out=w_ret_out, w_sb_out=w_sb_out, w_mix_out=w_mix_out, w_up=w_up, w_down=w_down)
    mts = dict(w_in=m_w_in, w_ret_out=m_w_ret_out, w_sb_out=m_w_sb_out, w_mix_out=m_w_mix_out, w_up=m_w_up, w_down=m_w_down)
    vts = dict(w_in=v_w_in, w_ret_out=v_w_ret_out, w_sb_out=v_w_sb_out, w_mix_out=v_w_mix_out, w_up=v_w_up, w_down=v_w_down)
    rc = _ret_consts(S)
    tri = _tri()

    (cact,) = _ew(lambda t: (_silu(t),), [jnp.pad(c, ((0, 7), (0, 0)))], outs=[('row', D, F32)], name="silu_c")
    (cact_all,) = _all_gather_lead([cact[0:1]], "gather_c")
    cact_all = cact_all.reshape(N_DEV, D)
    cact16 = jnp.pad(cact_all, ((0, 8), (0, 0)))
    n_ada = w_ada.shape[2]
    b_loc = lax.dynamic_slice_in_dim(b_ada, me * n_ada, n_ada, axis=1)
    mods = [_mod_partial(cact16, w_ada[l], b_loc[l:l + 1])[:N_DEV] for l in range(DEPTH)]
    modp = jnp.stack(mods, axis=1)
    (modr,) = _all_to_all_lead([modp], "scatter_mod")
    mod_full = jnp.transpose(modr, (1, 0, 2)).reshape(DEPTH, N_MOD, D)

    shards = {}
    for nm in _W_NAMES:
        w = wts[nm]
        (wb,) = _ew(lambda t: (t,), [w.reshape(-1, w.shape[-1])], outs=[('row', w.shape[-1], BF16)], name="cast_bf16")
        shards[nm] = wb.reshape(w.shape)
    rest = tuple(nm for nm in _W_NAMES if nm != "w_in")
    started, layer_groups = [modr], []
    for l in range(DEPTH):
        sh_l = {nm: shards[nm][l] for nm in _W_NAMES}
        groups = [(("w_in",), "%d_in" % l), (rest, "%d_rest" % l)] if l == 0 else [(_W_NAMES, "%d_all" % l)]
        layer_groups.append([])
        for names, tag in groups:
            layer_groups[-1].append((names, _gather_start(sh_l, names, tag, started[-1:])))
            started.append(layer_groups[-1][-1][1]['token'])
    layer_w = [_LayerWeights(g, started) for g in layer_groups]

    xs = x0
    saved = []
    for l in range(DEPTH):
        xs, sv = _layer_fwd(xs, mod_full[l], norm_mix_g[l:l + 1], norm_mlp_g[l:l + 1], layer_w[l], rc, tri)
        sv['W'] = layer_w[l]
        saved.append(sv)

    fg = final_g.reshape(1, D)

    def head(xt, tg, g):
        r = lax.rsqrt(jnp.mean(xt * xt, axis=-1, keepdims=True) + EPS)
        xh = xt * r
        e = xh * g - tg
        dy = e * (1.0 / D)
        dxh = dy * g
        dx = r * (dxh - xh * jnp.mean(dxh * xh, axis=-1, keepdims=True))
        return dx, dy * xh, 0.5 * e * e * (1.0 / D)

    dxs, d_fg, loss_cols = _ew(head, [xs, tgt], vecs=[fg], outs=[('row', D, F32), ('sum', D), ('sum', D)], name="loss_head")

    small = [None] * DEPTH
    pending = []
    for l in reversed(range(DEPTH)):
        sv = saved[l]

        def emit(gw, tag, l=l):
            names = tuple(gw)
            pending.append((l, names, _scatter_start(gw, names, "%d_%s" % (l, tag))))
            return pending[-1][2]['token']

        dxs, d_mod, d_gn1, d_gn2 = _layer_bwd(dxs, sv, sv['W'], rc, tri, emit)
        small[l] = (d_mod, d_gn1, d_gn2)
    grad_x = dxs.reshape(1, S, D)

    pack = jnp.concatenate([small[l][0] for l in range(DEPTH)] + [small[l][1] for l in range(DEPTH)]
                           + [small[l][2] for l in range(DEPTH)] + [d_fg, loss_cols], axis=1)
    (packs,) = _all_gather_lead([pack], "gather_small")
    packs = packs.reshape(N_DEV, -1)
    o = 0
    dmod_all = []
    for l in range(DEPTH):
        dmod_all.append(packs[:, o:o + N_MOD * D]); o += N_MOD * D
    gn1_parts = packs[:, o:o + DEPTH * D].reshape(N_DEV, DEPTH, D); o += DEPTH * D
    gn2_parts = packs[:, o:o + DEPTH * D].reshape(N_DEV, DEPTH, D); o += DEPTH * D
    fg_parts = packs[:, o:o + D].reshape(N_DEV, 1, D); o += D
    loss_parts = packs[:, o:o + D]
    (loss_sum,) = _ew(lambda t: (t,), [loss_parts], outs=[('sum', D)], name="loss_sum")
    loss = jnp.sum(loss_sum)

    res = {}
    per = {nm: [None] * DEPTH for nm in _W_NAMES}
    after = [dxs, loss_sum]
    for l, names, h in pending:
        for nm, landed in zip(names, _finish(h, after)):
            per[nm][l] = _adam(landed, wts[nm][l], mts[nm][l], vts[nm][l], "adam")
        after = [per[names[-1]][l][0]]
    for nm in _W_NAMES:
        res[nm] = [jnp.stack([per[nm][l][i] for l in range(DEPTH)]) for i in range(4)]
    res["norm_mix_g"] = _adam(gn1_parts, norm_mix_g, m_norm_mix_g, v_norm_mix_g, "adam")
    res["norm_mlp_g"] = _adam(gn2_parts, norm_mlp_g, m_norm_mlp_g, v_norm_mlp_g, "adam")
    fgr = _adam(fg_parts, fg, m_final_g.reshape(1, D), v_final_g.reshape(1, D), "adam")
    res["final_g"] = [t.reshape(D) for t in fgr]
    bparts = jnp.stack(dmod_all, axis=1)
    res["b_ada"] = _adam(bparts, b_ada, m_b_ada, v_b_ada, "adam")
    cact_t = cact_all.T
    ada = []
    for l in range(DEPTH):
        dm_loc = lax.dynamic_slice_in_dim(dmod_all[l], me * n_ada, n_ada, axis=1)
        ada.append(_adam(_ada_grad(cact_t, dm_loc), w_ada[l], m_w_ada[l], v_w_ada[l], "adam"))
    res["w_ada"] = [jnp.stack([ada[l][i] for l in range(DEPTH)]) for i in range(4)]

    order = ['norm_mix_g', 'w_in', 'w_ret_out', 'w_sb_out', 'w_mix_out', 'norm_mlp_g', 'w_up', 'w_down', 'w_ada', 'b_ada', 'final_g']
    out = [loss, grad_x]
    for i in range(4):
        out += [res[nm][i] for nm in order]
    return tuple(out)
```

```python
import functools
import math

import jax
import jax.numpy as jnp
import numpy as np
from jax import lax
from jax.experimental import pallas as pl
from jax.experimental.pallas import tpu as pltpu

F32 = jnp.float32
BF16 = jnp.bfloat16

N_DEV = 8
D_MODEL = 1024
DEPTH = 2
RET_HEADS = 4
RET_QK = 256
RET_V = 512
RET_CHUNK = 128
ROPE_BASE = 10000.0
SB_HEADS = 16
SB_DIM = 64
D_FF = 4096
N_MOD = 6
EPS = 1e-6
GN_EPS = 1e-5
O_RQ, O_RK, O_RV, O_RG, O_SQ, O_SK, O_SV, O_GA, O_GB = 0, 1024, 2048, 4096, 6144, 7168, 8192, 9216, 10240
IN_W = 11264

ADAM_LR, ADAM_B1, ADAM_B2, ADAM_EPS, ADAM_WD, ADAM_STEP = 0.001, 0.9, 0.999, 1e-08, 0.01, 10

VMEM_LIMIT = 56 * 1024 * 1024
DW_TK = 2048


def _cparams(sem):
    return pltpu.CompilerParams(dimension_semantics=sem, vmem_limit_bytes=VMEM_LIMIT)


def _mm(a, b, *, ta=False, tb=False, tm=1024, tn=1024, tk=None, a_ex=(), pro=None, o_ex=(), epi=None,
        outs=(F32,), name, deps=()):
    if ta:
        K, M = a.shape
    else:
        M, K = a.shape
    N = b.shape[0] if tb else b.shape[1]
    tm, tn, tk = min(tm, M), min(tn, N), K if tk is None else min(tk, K)
    assert M % tm == 0 and N % tn == 0 and K % tk == 0, (name, M, N, K, tm, tn, tk)
    nk = K // tk
    in_specs = [
        pl.BlockSpec((tk, tm), lambda i, j, k: (k, i)) if ta else pl.BlockSpec((tm, tk), lambda i, j, k: (i, k)),
        pl.BlockSpec((tn, tk), lambda i, j, k: (j, k)) if tb else pl.BlockSpec((tk, tn), lambda i, j, k: (k, j)),
    ]
    args = [a, b]
    for arr, kind, *off in a_ex:
        off = off[0] if off else 0
        if kind == 'a' and ta:
            assert off % tm == 0
            in_specs.append(pl.BlockSpec((tk, tm), lambda i, j, k, o=off // tm: (k, o + i)))
        elif kind == 'a':
            assert off % tk == 0
            in_specs.append(pl.BlockSpec((tm, tk), lambda i, j, k, o=off // tk: (i, o + k)))
        elif kind == 'k':
            in_specs.append(pl.BlockSpec((tk, 1), lambda i, j, k: (k, 0)) if ta
                            else pl.BlockSpec((1, tk), lambda i, j, k: (0, k)))
        else:
            in_specs.append(pl.BlockSpec((1, tm), lambda i, j, k: (0, i)) if ta
                            else pl.BlockSpec((tm, 1), lambda i, j, k: (i, 0)))
        args.append(arr)
    for arr, kind, *off in o_ex:
        off = off[0] if off else 0
        if kind == 'o':
            assert off % tn == 0
            in_specs.append(pl.BlockSpec((tm, tn), lambda i, j, k, o=off // tn: (i, o + j)))
        elif kind == 'n':
            in_specs.append(pl.BlockSpec((1, tn), lambda i, j, k: (0, j)))
        else:
            in_specs.append(pl.BlockSpec((tm, 1), lambda i, j, k: (i, 0)))
        args.append(arr)
    for arr in deps:
        in_specs.append(pl.BlockSpec(memory_space=pl.ANY))
        args.append(arr)
    na, no, nout, nd = len(a_ex), len(o_ex), len(outs), len(deps)
    dims = (((0 if ta else 1,), (1 if tb else 0,)), ((), ()))

    def body(*refs):
        a_ref, b_ref = refs[0], refs[1]
        aex = refs[2:2 + na]
        oex = refs[2 + na:2 + na + no]
        out_refs = refs[2 + na + no + nd:2 + na + no + nd + nout]

        def product():
            at = a_ref[...]
            if pro is not None:
                at = pro(at, *[r[...] for r in aex])
            return lax.dot_general(at.astype(BF16), b_ref[...].astype(BF16), dims, preferred_element_type=F32)

        def finish(res):
            vals = epi(res, *[r[...] for r in oex]) if epi is not None else (res,)
            for o_ref, v in zip(out_refs, vals):
                o_ref[...] = v.astype(o_ref.dtype)

        if nk == 1:
            finish(product())
            return
        acc = refs[-1]
        k = pl.program_id(2)

        @pl.when(k == 0)
        def _():
            acc[...] = product()

        @pl.when(k > 0)
        def _():
            acc[...] += product()

        @pl.when(k == nk - 1)
        def _():
            finish(acc[...])

    res = pl.pallas_call(
        body, name=name, grid=(M // tm, N // tn, nk), in_specs=in_specs,
        out_specs=[pl.BlockSpec((tm, tn), lambda i, j, k: (i, j)) for _ in outs],
        out_shape=[jax.ShapeDtypeStruct((M, N), dt) for dt in outs],
        scratch_shapes=[pltpu.VMEM((tm, tn), F32)] if nk > 1 else [],
        compiler_params=_cparams(("parallel", "parallel", "arbitrary")),
    )(*args)
    return res


def _ew(fn, rows, vecs=(), cols=(), outs=(), tr=256, name=None):
    S = rows[0].shape[0]
    tr = min(tr, S)
    assert S % tr == 0
    in_specs, args = [], []
    for r in rows:
        in_specs.append(pl.BlockSpec((tr, r.shape[1]), lambda i: (i, 0)))
        args.append(r)
    for v in vecs:
        in_specs.append(pl.BlockSpec((1, v.shape[1]), lambda i: (0, 0)))
        args.append(v)
    for c in cols:
        in_specs.append(pl.BlockSpec((tr, 1), lambda i: (i, 0)))
        args.append(c)
    out_specs, out_shape = [], []
    for o in outs:
        if o[0] == 'row':
            out_specs.append(pl.BlockSpec((tr, o[1]), lambda i: (i, 0)))
            out_shape.append(jax.ShapeDtypeStruct((S, o[1]), o[2]))
        elif o[0] == 'sum':
            out_specs.append(pl.BlockSpec((1, o[1]), lambda i: (0, 0)))
            out_shape.append(jax.ShapeDtypeStruct((1, o[1]), F32))
        else:
            out_specs.append(pl.BlockSpec((tr, 1), lambda i: (i, 0)))
            out_shape.append(jax.ShapeDtypeStruct((S, 1), o[1]))
    nin = len(args)

    def body(*refs):
        i = pl.program_id(0)
        vals = fn(*[r[...] for r in refs[:nin]])
        for o, o_ref, v in zip(outs, refs[nin:], vals):
            if o[0] == 'sum':
                @pl.when(i == 0)
                def _():
                    o_ref[...] = jnp.zeros_like(o_ref)
                o_ref[...] += jnp.sum(v.astype(F32), axis=0, keepdims=True)
            else:
                o_ref[...] = v.astype(o_ref.dtype)

    return pl.pallas_call(
        body, name=name, grid=(S // tr,), in_specs=in_specs, out_specs=out_specs, out_shape=out_shape,
        compiler_params=_cparams(("arbitrary",)),
    )(*args)


def _sigmoid(x):
    return 1.0 / (1.0 + jnp.exp(-x))


def _ret_consts(S):
    h = np.arange(RET_HEADS, dtype=np.float64)
    log_gamma = np.log1p(-np.power(2.0, -5.0 - h))
    idx = np.arange(RET_CHUNK, dtype=np.float64)
    rel = idx[:, None] - idx[None, :]
    decay = np.where(rel >= 0, np.exp(np.maximum(rel, 0.0) * log_gamma[:, None, None]), 0.0)
    xi = np.exp((idx + 1.0) * log_gamma[:, None])[:, :, None]
    zeta = np.exp((RET_CHUNK - 1.0 - idx) * log_gamma[:, None])[:, :, None]
    gamma_c = np.exp(RET_CHUNK * log_gamma)[:, None, None]
    half = RET_QK // 2
    inv_freq = np.power(ROPE_BASE, -np.arange(half, dtype=np.float64) / half).astype(np.float32)
    ang = np.arange(S, dtype=np.float32)[:, None] * inv_freq[None, :]
    f = lambda t: jnp.asarray(t, F32)
    return dict(decay=f(decay), xi=f(xi), zeta=f(zeta), gc=f(gamma_c), cos=f(np.cos(ang)), sin=f(np.sin(ang)))


def _rot(t, cos, sin):
    half = RET_QK // 2
    t1, t2 = t[:, :half], t[:, half:]
    return jnp.concatenate([t1 * cos - t2 * sin, t1 * sin + t2 * cos], axis=-1)


def _rot_inv(t, cos, sin):
    half = RET_QK // 2
    t1, t2 = t[:, :half], t[:, half:]
    return jnp.concatenate([t1 * cos + t2 * sin, t2 * cos - t1 * sin], axis=-1)


_NT = (((1,), (1,)), ((), ()))
_TN = (((0,), (0,)), ((), ()))


def _dot(a, b):
    return jnp.dot(a, b, preferred_element_type=F32)


def _dot_nt(a, b):
    return lax.dot_general(a, b, _NT, preferred_element_type=F32)


def _dot_tn(a, b):
    return lax.dot_general(a, b, _TN, preferred_element_type=F32)


def _ret_in_specs(C, rev, NC):
    n_of = (lambda n: NC - 1 - n) if rev else (lambda n: n)
    qb, vb = O_RQ // RET_QK, O_RV // RET_V
    kb = O_RK // RET_QK
    return [
        pl.BlockSpec((C, RET_QK), lambda h, n: (n_of(n), qb + h)),
        pl.BlockSpec((C, RET_QK), lambda h, n: (n_of(n), kb + h)),
        pl.BlockSpec((C, RET_V), lambda h, n: (n_of(n), vb + h)),
        pl.BlockSpec((C, RET_QK // 2), lambda h, n: (n_of(n), 0)),
        pl.BlockSpec((C, RET_QK // 2), lambda h, n: (n_of(n), 0)),
        pl.BlockSpec((1, C, C), lambda h, n: (h, 0, 0)),
        pl.BlockSpec((1, C, 1), lambda h, n: (h, 0, 0)),
        pl.BlockSpec((1, C, 1), lambda h, n: (h, 0, 0)),
        pl.BlockSpec((1, 1, 1), lambda h, n: (h, 0, 0)),
    ]


def _ret_fwd(p, rc):
    S = p.shape[0]
    C = RET_CHUNK
    NC = S // C

    def body(q_ref, k_ref, v_ref, cos_ref, sin_ref, dec_ref, xi_ref, zeta_ref, gc_ref, y_ref, rs_ref, r_acc):
        n = pl.program_id(1)

        @pl.when(n == 0)
        def _():
            r_acc[...] = jnp.zeros_like(r_acc)

        cos, sin = cos_ref[...], sin_ref[...]
        q = _rot(q_ref[...].astype(F32), cos, sin).astype(BF16)
        kf = _rot(k_ref[...].astype(F32), cos, sin) * (RET_QK ** -0.5)
        k = kf.astype(BF16)
        v = v_ref[...]
        r = r_acc[...]
        rb = r.astype(BF16)
        rs_ref[0, 0] = rb
        s = (_dot_nt(q, k) * dec_ref[0]).astype(BF16)
        o = _dot(s, v) + _dot(q, rb) * xi_ref[0]
        mu = jnp.mean(o, axis=-1, keepdims=True)
        var = jnp.mean(jnp.square(o - mu), axis=-1, keepdims=True)
        y_ref[...] = ((o - mu) * lax.rsqrt(var + GN_EPS)).astype(y_ref.dtype)
        kz = (kf * zeta_ref[0]).astype(BF16)
        r_acc[...] = r * gc_ref[0] + _dot_tn(kz, v)

    return pl.pallas_call(
        body, name="ret_fwd", grid=(RET_HEADS, NC), in_specs=_ret_in_specs(C, False, NC),
        out_specs=[pl.BlockSpec((C, RET_V), lambda h, n: (n, h)),
                   pl.BlockSpec((1, 1, RET_QK, RET_V), lambda h, n: (h, n, 0, 0))],
        out_shape=[jax.ShapeDtypeStruct((S, RET_HEADS * RET_V), BF16),
                   jax.ShapeDtypeStruct((RET_HEADS, NC, RET_QK, RET_V), BF16)],
        scratch_shapes=[pltpu.VMEM((RET_QK, RET_V), F32)],
        compiler_params=_cparams(("parallel", "arbitrary")),
    )(p, p, p, rc['cos'], rc['sin'], rc['decay'], rc['xi'], rc['zeta'], rc['gc'])


def _ret_bwd(p, rstate, dy, rc):
    S = p.shape[0]
    C = RET_CHUNK
    NC = S // C

    def body(q_ref, k_ref, v_ref, cos_ref, sin_ref, dec_ref, xi_ref, zeta_ref, gc_ref, rs_ref, dy_ref,
             dq_ref, dk_ref, dv_ref, dr_acc):
        t = pl.program_id(1)

        @pl.when(t == 0)
        def _():
            dr_acc[...] = jnp.zeros_like(dr_acc)

        cos, sin = cos_ref[...], sin_ref[...]
        dec, xi, zeta = dec_ref[0], xi_ref[0], zeta_ref[0]
        q = _rot(q_ref[...].astype(F32), cos, sin).astype(BF16)
        kf = _rot(k_ref[...].astype(F32), cos, sin) * (RET_QK ** -0.5)
        k = kf.astype(BF16)
        kz = (kf * zeta).astype(BF16)
        v = v_ref[...]
        rb = rs_ref[0, 0]
        s = (_dot_nt(q, k) * dec).astype(BF16)
        o = _dot(s, v) + _dot(q, rb) * xi
        mu = jnp.mean(o, axis=-1, keepdims=True)
        var = jnp.mean(jnp.square(o - mu), axis=-1, keepdims=True)
        rstd = lax.rsqrt(var + GN_EPS)
        yh = (o - mu) * rstd
        dyf = dy_ref[...].astype(F32)
        do = (dyf - jnp.mean(dyf, axis=-1, keepdims=True) - yh * jnp.mean(dyf * yh, axis=-1, keepdims=True)) * rstd
        dob = do.astype(BF16)
        doxi = (do * xi).astype(BF16)
        dr = dr_acc[...]
        drb = dr.astype(BF16)
        ds = (_dot_nt(dob, v) * dec).astype(BF16)
        dq = _dot(ds, k) + _dot_nt(doxi, rb)
        dk = _dot_tn(ds, q) + _dot_nt(v, drb) * zeta
        dv = _dot_tn(s, dob) + _dot(kz, drb)
        dr_acc[...] = dr * gc_ref[0] + _dot_tn(q, doxi)
        dq_ref[...] = _rot_inv(dq, cos, sin).astype(dq_ref.dtype)
        dk_ref[...] = (_rot_inv(dk, cos, sin) * (RET_QK ** -0.5)).astype(dk_ref.dtype)
        dv_ref[...] = dv.astype(dv_ref.dtype)

    rn = lambda n: NC - 1 - n
    in_specs = _ret_in_specs(C, True, NC) + [
        pl.BlockSpec((1, 1, RET_QK, RET_V), lambda h, n: (h, rn(n), 0, 0)),
        pl.BlockSpec((C, RET_V), lambda h, n: (rn(n), h)),
    ]
    return pl.pallas_call(
        body, name="ret_bwd", grid=(RET_HEADS, NC), in_specs=in_specs,
        out_specs=[pl.BlockSpec((C, RET_QK), lambda h, n: (rn(n), h)),
                   pl.BlockSpec((C, RET_QK), lambda h, n: (rn(n), h)),
                   pl.BlockSpec((C, RET_V), lambda h, n: (rn(n), h))],
        out_shape=[jax.ShapeDtypeStruct((S, RET_HEADS * RET_QK), BF16),
                   jax.ShapeDtypeStruct((S, RET_HEADS * RET_QK), BF16),
                   jax.ShapeDtypeStruct((S, RET_HEADS * RET_V), BF16)],
        scratch_shapes=[pltpu.VMEM((RET_QK, RET_V), F32)],
        compiler_params=_cparams(("parallel", "arbitrary")),
    )(p, p, p, rc['cos'], rc['sin'], rc['decay'], rc['xi'], rc['zeta'], rc['gc'], rstate, dy)


SB_T = 256
SB_SCALE = SB_DIM ** -0.5


def _tri():
    j = np.arange(SB_T)
    rev = (j[:, None] >= j[None, :]).astype(np.float32)
    return jnp.asarray(np.stack([np.concatenate([rev, rev]), np.concatenate([rev.T, rev.T])]), BF16)


def _run_sum(x, tri2):
    hi = x.astype(BF16)
    lo = (x - hi.astype(F32)).astype(BF16)
    return _dot(jnp.concatenate([hi, lo], axis=1), tri2)


def _sb_fwd(p, tri):
    S = p.shape[0]
    T = min(SB_T, S)
    NQ = S // T
    assert NQ <= 128
    qb, kb, vb = O_SQ // 128, O_SK // 128, O_SV // 128

    def body(q_ref, k_ref, v_ref, tri_ref, o_ref, cs_ref, o_acc, run, zbuf, abuf):
        i = pl.program_id(1)
        lane = lax.broadcasted_iota(jnp.int32, (1, 128), 1)
        tri2 = tri_ref[0]
        qs = [jnp.where((lane >= 64) if hh else (lane < 64), q_ref[...], jnp.zeros_like(q_ref[...]))
              * jnp.asarray(SB_SCALE, BF16) for hh in range(2)]
        cs_ref[...] = jnp.zeros_like(cs_ref)
        o_acc[...] = jnp.zeros_like(o_acc)
        run[...] = jnp.zeros_like(run)

        def kv(ref, j):
            return ref[pl.ds(pl.multiple_of(j * T, T), T), :]

        for hh in range(2):
            zbuf[hh] = _dot_nt(qs[hh], kv(k_ref, i))

        def block(t, diagonal):
            j = i - t
            zn = [_dot_nt(qs[hh], kv(k_ref, jnp.maximum(j - 1, 0))) for hh in range(2)]
            if not diagonal:
                av = [_dot(abuf[hh], kv(v_ref, j + 1)) for hh in range(2)]
            else:
                msk = lax.broadcasted_iota(jnp.int32, (T, T), 1) < lax.broadcasted_iota(jnp.int32, (T, T), 0)
            zs = [zbuf[hh] for hh in range(2)]
            cls = []
            for hh in range(2):
                sp = jnp.maximum(zs[hh], 0.0) + jnp.log(1.0 + jnp.exp(-jnp.abs(zs[hh])))
                if diagonal:
                    sp = jnp.where(msk, sp, 0.0)
                cls.append(_run_sum(sp, tri2))
            for hh in range(2):
                csl = slice(hh * 128, (hh + 1) * 128)
                cs = run[hh]
                a = jnp.exp(zs[hh] - cls[hh] - cs)
                if diagonal:
                    a = jnp.where(msk, a, 0.0)
                abuf[hh] = a.astype(BF16)
                cs_ref[:, csl] = jnp.where(lane == j, cs, cs_ref[:, csl])
                run[hh] = cs + cls[hh][:, 0:1]
            for hh in range(2):
                if not diagonal:
                    o_acc[hh] += av[hh]
                zbuf[hh] = zn[hh]

        block(0, True)

        def step(t, carry):
            block(t, False)
            return carry

        lax.fori_loop(1, i + 1, step, 0)
        o_ref[...] = jnp.where(lane < 64, o_acc[0] + _dot(abuf[0], kv(v_ref, 0)),
                               o_acc[1] + _dot(abuf[1], kv(v_ref, 0))).astype(o_ref.dtype)

    return pl.pallas_call(
        body, name="sb_fwd", grid=(SB_HEADS // 2, NQ),
        scratch_shapes=[pltpu.VMEM((2, T, 128), F32), pltpu.VMEM((2, T, 1), F32), pltpu.VMEM((2, T, T), F32),
                        pltpu.VMEM((2, T, T), BF16)],
        in_specs=[pl.BlockSpec((T, 128), lambda h, i: (i, qb + h)),
                  pl.BlockSpec((S, 128), lambda h, i: (0, kb + h)),
                  pl.BlockSpec((S, 128), lambda h, i: (0, vb + h)),
                  pl.BlockSpec((1, 2 * T, T), lambda h, i: (0, 0, 0))],
        out_specs=[pl.BlockSpec((T, 128), lambda h, i: (i, h)),
                   pl.BlockSpec((T, 256), lambda h, i: (i, h))],
        out_shape=[jax.ShapeDtypeStruct((S, SB_HEADS * SB_DIM), BF16),
                   jax.ShapeDtypeStruct((S, SB_HEADS * 128), F32)],
        compiler_params=_cparams(("parallel", "arbitrary")),
    )(p, p, p, tri)


def _sb_bwd(p, carries, dy, tri):
    S = p.shape[0]
    T = min(SB_T, S)
    NQ = S // T
    qb, kb, vb = O_SQ // 128, O_SK // 128, O_SV // 128

    def body(q_ref, k_ref, v_ref, cs_ref, dy_ref, tri_ref, dq_ref, dk_ref, dv_ref, dk_acc, dv_acc, dq_acc, run,
             zbuf, dabuf, dzbuf, abuf):
        i = pl.program_id(1)

        @pl.when(i == 0)
        def _():
            dk_acc[...] = jnp.zeros_like(dk_acc)
            dv_acc[...] = jnp.zeros_like(dv_acc)

        lane = lax.broadcasted_iota(jnp.int32, (1, 128), 1)
        rev2, fwd1 = tri_ref[0], tri_ref[1, 0:T, :]
        hms = [(lane >= 64) if hh else (lane < 64) for hh in range(2)]
        qs = [jnp.where(hm, q_ref[...], jnp.zeros_like(q_ref[...])) * jnp.asarray(SB_SCALE, BF16) for hm in hms]
        dos = [jnp.where(hm, dy_ref[...], jnp.zeros_like(dy_ref[...])) for hm in hms]
        dq_acc[...] = jnp.zeros_like(dq_acc)
        run[...] = jnp.zeros_like(run)
        dzbuf[...] = jnp.zeros_like(dzbuf)
        abuf[...] = jnp.zeros_like(abuf)

        def kv(ref, j):
            return ref[pl.ds(pl.multiple_of(j * T, T), T), :]

        def flush(jp):
            kp = kv(k_ref, jp)
            dq_add = [_dot(dzbuf[hh], kp) for hh in range(2)]
            dk_add = _dot_tn(dzbuf[0], qs[0]) + _dot_tn(dzbuf[1], qs[1])
            dv_add = _dot_tn(abuf[0], dos[0]) + _dot_tn(abuf[1], dos[1])
            return dq_add, dk_add, dv_add

        def apply(jp, adds):
            dq_add, dk_add, dv_add = adds
            rows = pl.ds(pl.multiple_of(jp * T, T), T)
            for hh in range(2):
                dq_acc[hh] += dq_add[hh]
            dk_acc[rows, :] += dk_add
            dv_acc[rows, :] += dv_add

        for hh in range(2):
            zbuf[hh] = _dot_nt(qs[hh], kv(k_ref, 0))
            dabuf[hh] = _dot_nt(dos[hh], kv(v_ref, 0))

        def block(j, diagonal):
            jp = jnp.maximum(j - 1, 0)
            if not diagonal:
                zn = [_dot_nt(qs[hh], kv(k_ref, j + 1)) for hh in range(2)]
                dan = [_dot_nt(dos[hh], kv(v_ref, j + 1)) for hh in range(2)]
            else:
                msk = lax.broadcasted_iota(jnp.int32, (T, T), 1) < lax.broadcasted_iota(jnp.int32, (T, T), 0)
            adds = flush(jp)
            zs = [zbuf[hh] for hh in range(2)]
            sigs, cls = [], []
            for hh in range(2):
                e = jnp.exp(-jnp.abs(zs[hh]))
                sp = jnp.maximum(zs[hh], 0.0) + jnp.log(1.0 + e)
                r = 1.0 / (1.0 + e)
                sigs.append(jnp.where(zs[hh] >= 0, r, e * r))
                if diagonal:
                    sp = jnp.where(msk, sp, 0.0)
                cls.append(_run_sum(sp, rev2))
            pgs, gs = [], []
            for hh in range(2):
                csl = slice(hh * 128, (hh + 1) * 128)
                cs = jnp.sum(jnp.where(lane == j, cs_ref[:, csl], 0.0), axis=-1, keepdims=True)
                a = jnp.exp(zs[hh] - cls[hh] - cs)
                if diagonal:
                    a = jnp.where(msk, a, 0.0)
                abuf_new = a.astype(BF16)
                g = a * dabuf[hh]
                gs.append((g, abuf_new))
                pgs.append(_dot(g.astype(BF16), fwd1))
            for hh in range(2):
                g, abuf_new = gs[hh]
                cg = run[hh]
                dz = g - sigs[hh] * (cg + pgs[hh])
                if diagonal:
                    dz = jnp.where(msk, dz, 0.0)
                run[hh] = cg + pgs[hh][:, T - 1:T]
                dzbuf[hh] = dz.astype(BF16)
                abuf[hh] = abuf_new
            apply(jp, adds)
            if not diagonal:
                for hh in range(2):
                    zbuf[hh] = zn[hh]
                    dabuf[hh] = dan[hh]

        def step(j, carry):
            block(j, False)
            return carry

        lax.fori_loop(0, i, step, 0)
        block(i, True)
        apply(i, flush(i))
        dq_ref[...] = (jnp.where(lane < 64, dq_acc[0], dq_acc[1]) * SB_SCALE).astype(dq_ref.dtype)

        @pl.when(i == NQ - 1)
        def _():
            dk_ref[...] = dk_acc[...].astype(dk_ref.dtype)
            dv_ref[...] = dv_acc[...].astype(dv_ref.dtype)

    W = SB_HEADS * SB_DIM
    return pl.pallas_call(
        body, name="sb_bwd", grid=(SB_HEADS // 2, NQ),
        in_specs=[pl.BlockSpec((T, 128), lambda h, i: (i, qb + h)),
                  pl.BlockSpec((S, 128), lambda h, i: (0, kb + h)),
                  pl.BlockSpec((S, 128), lambda h, i: (0, vb + h)),
                  pl.BlockSpec((T, 256), lambda h, i: (i, h)),
                  pl.BlockSpec((T, 128), lambda h, i: (i, h)),
                  pl.BlockSpec((2, 2 * T, T), lambda h, i: (0, 0, 0))],
        out_specs=[pl.BlockSpec((T, 128), lambda h, i: (i, h)),
                   pl.BlockSpec((S, 128), lambda h, i: (0, h)),
                   pl.BlockSpec((S, 128), lambda h, i: (0, h))],
        out_shape=[jax.ShapeDtypeStruct((S, W), BF16)] * 3,
        scratch_shapes=[pltpu.VMEM((S, 128), F32), pltpu.VMEM((S, 128), F32), pltpu.VMEM((2, T, 128), F32),
                        pltpu.VMEM((2, T, 1), F32), pltpu.VMEM((2, T, T), F32), pltpu.VMEM((2, T, T), F32),
                        pltpu.VMEM((2, T, T), BF16), pltpu.VMEM((2, T, T), BF16)],
        compiler_params=_cparams(("parallel", "arbitrary")),
    )(p, p, p, carries, dy, tri)


def _exchange(srcs, out_shapes, src_slice, dst_slice, name):
    n = len(srcs)

    def body(*refs):
        ins, outs = refs[:n], refs[n:2 * n]
        send_sems, recv_sems, loc_sems = refs[2 * n:]
        x, y, c = lax.axis_index("x"), lax.axis_index("y"), lax.axis_index("c")
        me = 4 * x + 2 * y + c
        local = [pltpu.make_async_copy(src_slice(t, ins[t], me), dst_slice(t, outs[t], me), loc_sems.at[t])
                 for t in range(n)]
        for cp in local:
            cp.start()
        sends, recvs = [], []
        for k in (1, 2, 4, 6, 3, 5, 7):
            px = 1 - x if k & 4 else x
            py = 1 - y if k & 2 else y
            pc = 1 - c if k & 1 else c
            peer = 4 * px + 2 * py + pc
            for t in range(n):
                s = t * 7 + k - 1
                sends.append(pltpu.make_async_remote_copy(
                    src_ref=src_slice(t, ins[t], peer), dst_ref=dst_slice(t, outs[t], me),
                    send_sem=send_sems.at[s], recv_sem=recv_sems.at[s],
                    device_id=(px, py, pc), device_id_type=pl.DeviceIdType.MESH))
                recvs.append(pltpu.make_async_remote_copy(
                    src_ref=src_slice(t, ins[t], me), dst_ref=dst_slice(t, outs[t], peer),
                    send_sem=send_sems.at[s], recv_sem=recv_sems.at[s],
                    device_id=(px, py, pc), device_id_type=pl.DeviceIdType.MESH))
        for cp in sends:
            cp.start()
        for cp in recvs:
            cp.wait_recv()
        for cp in sends:
            cp.wait_send()
        for cp in local:
            cp.wait()

    anyspec = pl.BlockSpec(memory_space=pl.ANY)
    return pl.pallas_call(
        body, name=name, in_specs=[anyspec] * n, out_specs=[anyspec] * n,
        out_shape=[jax.ShapeDtypeStruct(s, d) for s, d in out_shapes],
        scratch_shapes=[pltpu.SemaphoreType.DMA((7 * n,)), pltpu.SemaphoreType.DMA((7 * n,)),
                        pltpu.SemaphoreType.DMA((n,))],
    )(*srcs)


def _all_gather_lead(xs, name):
    return _exchange(
        xs, [((N_DEV,) + x.shape, x.dtype) for x in xs],
        lambda t, ref, peer: ref, lambda t, ref, who: ref.at[who], name)


def _all_to_all_lead(xs, name):
    return _exchange(
        xs, [(x.shape, x.dtype) for x in xs],
        lambda t, ref, peer: ref.at[peer], lambda t, ref, who: ref.at[who], name)


_W_AXIS = {"w_in": 1, "w_ret_out": 0, "w_sb_out": 0, "w_mix_out": 0, "w_up": 1, "w_down": 0}
_W_NAMES = tuple(_W_AXIS)


def _window(ref, axis, who, width, count=1):
    start = pl.multiple_of(who * width, width)
    return ref.at[pl.ds(start, count * width), :] if axis == 0 else ref.at[:, pl.ds(start, count * width)]


_HBM = pl.BlockSpec(memory_space=pltpu.HBM)
_SEM = pl.BlockSpec(memory_space=pltpu.SEMAPHORE)
_EFFECT = pltpu.SideEffectType.DATAFLOW_SIDE_EFFECTING


def _exchange_start(srcs, shapes, src_slice, dst_slice, name, deps=()):
    n, nd = len(srcs), len(deps)
    lands = [pltpu.with_memory_space_constraint(lax.empty(s, d), pltpu.HBM) for s, d in shapes]

    def body(*refs):
        ins, lnd = refs[:n], refs[n:2 * n]
        sems = refs[2 * n + nd:4 * n + nd]
        token = refs[6 * n + nd]
        x, y, c = lax.axis_index("x"), lax.axis_index("y"), lax.axis_index("c")
        me = 4 * x + 2 * y + c
        for k in (0, 1, 2, 4, 6, 3, 5, 7):
            px = 1 - x if k & 4 else x
            py = 1 - y if k & 2 else y
            pc = 1 - c if k & 1 else c
            peer = 4 * px + 2 * py + pc
            for t in range(n):
                pltpu.make_async_remote_copy(
                    src_ref=src_slice(t, ins[t], peer), dst_ref=dst_slice(t, lnd[t], me),
                    send_sem=sems[2 * t], recv_sem=sems[2 * t + 1],
                    device_id=(px, py, pc), device_id_type=pl.DeviceIdType.MESH).start()
        token[...] = jnp.zeros_like(token)

    res = pl.pallas_call(
        body, name=name, in_specs=[_HBM] * (2 * n) + [pl.BlockSpec(memory_space=pl.ANY)] * nd,
        out_specs=[_SEM] * (2 * n) + [_HBM] * (2 * n) + [pl.BlockSpec(memory_space=pltpu.VMEM)],
        out_shape=[pltpu.SemaphoreType.DMA(())] * (2 * n) + [pltpu.HBM(s.shape, s.dtype) for s in srcs]
        + [pltpu.HBM(s.shape, s.dtype) for s in lands] + [jax.ShapeDtypeStruct((8, 128), F32)],
        input_output_aliases={t: 2 * n + t for t in range(2 * n)},
        compiler_params=pltpu.CompilerParams(has_side_effects=_EFFECT),
    )(*[pltpu.with_memory_space_constraint(s, pltpu.HBM) for s in srcs], *lands, *deps)
    return dict(n=n, sems=res[:2 * n], srcs=res[2 * n:3 * n], lands=res[3 * n:4 * n], token=res[4 * n])


def _exchange_wait(h, after, name):
    n = h['n']

    def body(*refs):
        lnd = refs[n:2 * n]
        sems = refs[2 * n:4 * n]
        x, y, c = lax.axis_index("x"), lax.axis_index("y"), lax.axis_index("c")
        for t in range(n):
            w = lnd[t]
            cp = pltpu.make_async_remote_copy(src_ref=w, dst_ref=w, send_sem=sems[2 * t], recv_sem=sems[2 * t + 1],
                                              device_id=(x, y, 1 - c), device_id_type=pl.DeviceIdType.MESH)
            cp.wait_send()
            cp.wait_recv()

    after = list(after)
    res = pl.pallas_call(
        body, name=name,
        in_specs=[_HBM] * (2 * n) + [_SEM] * (2 * n) + [pl.BlockSpec(memory_space=pl.ANY)] * len(after),
        out_specs=[_HBM] * (2 * n),
        out_shape=[pltpu.HBM(s.shape, s.dtype) for s in h['srcs']] + [pltpu.HBM(s.shape, s.dtype) for s in h['lands']],
        input_output_aliases={t: t for t in range(2 * n)},
        compiler_params=pltpu.CompilerParams(has_side_effects=_EFFECT),
    )(*h['srcs'], *h['lands'], *h['sems'], *after)
    return list(res[n:])


def _gather_start(shards, names, tag, deps=()):
    xs = [shards[nm] for nm in names]
    axes = [_W_AXIS[nm] for nm in names]
    widths = [x.shape[ax] for x, ax in zip(xs, axes)]
    shapes = [(tuple(d * (N_DEV if a == ax else 1) for a, d in enumerate(x.shape)), x.dtype) for x, ax in zip(xs, axes)]
    src = lambda t, ref, peer: ref
    dst = lambda t, ref, who: _window(ref, axes[t], who, widths[t])
    h = _exchange_start(xs, shapes, src, dst, "gw_start_" + tag, deps)
    h['tag'] = "gw_wait_" + tag
    return h


def _scatter_start(grads, names, tag):
    xs = [grads[nm] for nm in names]
    axes = [_W_AXIS[nm] for nm in names]
    widths = [x.shape[ax] // N_DEV for x, ax in zip(xs, axes)]
    shapes = [((N_DEV,) + tuple(d // (N_DEV if a == ax else 1) for a, d in enumerate(x.shape)), x.dtype)
              for x, ax in zip(xs, axes)]
    src = lambda t, ref, peer: _window(ref, axes[t], peer, widths[t])
    dst = lambda t, ref, who: ref.at[who]
    h = _exchange_start(xs, shapes, src, dst, "sg_start_" + tag)
    h['tag'] = "sg_wait_" + tag
    return h


def _finish(h, after):
    return _exchange_wait(h, after, h['tag'])


class _LayerWeights:
    def __init__(self, groups, started):
        self.groups = groups
        self.started = started
        self.got = {}
        self.after = None

    def __getitem__(self, nm):
        if nm not in self.got:
            for names, h in self.groups:
                if nm in names:
                    self.got.update(zip(names, _finish(h, list(self.after) + self.started)))
        return self.got[nm]


def _adam(parts, w, m, v, name, tr=256):
    P, R, C = parts.shape
    tr = min(tr, R)
    assert R % tr == 0
    bc1 = 1.0 / (1.0 - ADAM_B1 ** ADAM_STEP)
    bc2 = 1.0 / (1.0 - ADAM_B2 ** ADAM_STEP)

    def body(p_ref, w_ref, m_ref, v_ref, g_out, d_out, m_out, v_out):
        g = p_ref[0].astype(F32)
        for s in range(1, P):
            g = g + p_ref[s].astype(F32)
        mm = ADAM_B1 * m_ref[...] + (1.0 - ADAM_B1) * g
        vv = ADAM_B2 * v_ref[...] + (1.0 - ADAM_B2) * jnp.square(g)
        g_out[...] = g
        m_out[...] = mm
        v_out[...] = vv
        d_out[...] = -ADAM_LR * ((mm * bc1) / (jnp.sqrt(vv * bc2) + ADAM_EPS) + ADAM_WD * w_ref[...])

    spec = pl.BlockSpec((tr, C), lambda i: (i, 0))
    return pl.pallas_call(
        body, name=name, grid=(R // tr,),
        in_specs=[pl.BlockSpec((P, tr, C), lambda i: (0, i, 0)), spec, spec, spec],
        out_specs=[spec] * 4, out_shape=[jax.ShapeDtypeStruct((R, C), F32)] * 4,
        compiler_params=_cparams(("parallel",)),
    )(parts, w, m, v)


def _mod_partial(cact_all, w_ada_l, b_ada_l):
    def body(c_ref, w_ref, b_ref, o_ref):
        o_ref[...] = _dot(c_ref[...].astype(BF16), w_ref[...].astype(BF16)) + b_ref[...]

    return pl.pallas_call(
        body, name="mod_partial", out_shape=jax.ShapeDtypeStruct((cact_all.shape[0], w_ada_l.shape[1]), F32),
        compiler_params=pltpu.CompilerParams(vmem_limit_bytes=VMEM_LIMIT),
    )(cact_all, w_ada_l, b_ada_l)


def _ada_grad(cact_t, dmod):
    D, n = cact_t.shape[0], dmod.shape[1]

    def body(c_ref, d_ref, o_ref):
        ct = c_ref[...].astype(BF16).astype(F32)
        dm = d_ref[...].astype(BF16).astype(F32)
        acc = ct[:, 0:1] * dm[0:1, :]
        for b in range(1, N_DEV):
            acc = acc + ct[:, b:b + 1] * dm[b:b + 1, :]
        o_ref[0] = acc

    return pl.pallas_call(
        body, name="ada_grad", out_shape=jax.ShapeDtypeStruct((1, D, n), F32),
        compiler_params=pltpu.CompilerParams(vmem_limit_bytes=VMEM_LIMIT),
    )(cact_t, dmod)


def _norm_mod(x, r, gv, sh):
    return x * r * gv + sh


def _silu(x):
    return x * _sigmoid(x)


def _layer_fwd(x0, mod, gn1, gn2, W, rc, tri):
    S = x0.shape[0]
    sh1, sc1, g1m, sh2, sc2, g2m = [mod[i:i + 1] for i in range(N_MOD)]
    gv1 = gn1 * (1.0 + sc1)
    gv2 = gn2 * (1.0 + sc2)
    (r1,) = _ew(lambda x: (lax.rsqrt(jnp.mean(x * x, axis=-1, keepdims=True) + EPS),), [x0], outs=[('col', F32)],
                name="row_rstd")
    W.after = [r1, gv1]
    (p,) = _mm(x0, W["w_in"], a_ex=[(r1, 'm'), (gv1, 'k'), (sh1, 'k')], pro=_norm_mod, outs=(BF16,), name="mm_in")
    yret, rstate = _ret_fwd(p, rc)
    ysb, sbc = _sb_fwd(p, tri)
    W.after = [ysb]
    (ya,) = _mm(yret, W["w_ret_out"], tm=512, a_ex=[(p, 'a', O_RG)],
                pro=lambda yr, g: _silu(g.astype(F32)) * yr.astype(F32), outs=(BF16,), name="mm_ret_out")
    yb, mg = _mm(ysb, W["w_sb_out"], o_ex=[(ya, 'o'), (p, 'o', O_GA), (p, 'o', O_GB)],
                 epi=lambda acc, a, ga, gb: (acc, _sigmoid(ga.astype(F32)) * a.astype(F32)
                                             + _sigmoid(gb.astype(F32)) * acc),
                 outs=(BF16, BF16), name="mm_sb_out")
    mo, x1 = _mm(mg, W["w_mix_out"], o_ex=[(x0, 'o'), (g1m, 'n')],
                 epi=lambda acc, x, g: (acc, x + g * acc), outs=(BF16, F32), name="mm_mix_out")
    (r2,) = _ew(lambda x: (lax.rsqrt(jnp.mean(x * x, axis=-1, keepdims=True) + EPS),), [x1], outs=[('col', F32)],
                name="row_rstd")
    (act,) = _mm(x1, W["w_up"], a_ex=[(r2, 'm'), (gv2, 'k'), (sh2, 'k')], pro=_norm_mod,
                 epi=lambda acc: (jnp.maximum(acc, 0.0),), outs=(BF16,), name="mm_up")
    dn, x2 = _mm(act, W["w_down"], tm=512, pro=lambda a: a * a,
                 o_ex=[(x1, 'o'), (g2m, 'n')], epi=lambda acc, x, g: (acc, x + g * acc), outs=(BF16, F32),
                 name="mm_down")
    saved = dict(x0=x0, r1=r1, p=p, yret=yret, rstate=rstate, ysb=ysb, sbc=sbc, ya=ya, yb=yb, mg=mg, mo=mo, x1=x1, r2=r2,
                 act=act, dn=dn, gv1=gv1, gv2=gv2, mod=mod, gn1=gn1, gn2=gn2)
    return x2, saved


def _norm_bwd(dh, x, r, dres, gv, gn, extra_rows=(), extra_vecs=(), extra_fn=None, extra_outs=(), name="norm_bwd"):
    D = x.shape[1]
    ne = len(extra_rows)

    def fn(dh_t, x_t, dres_t, *rest):
        er, rest = rest[:ne], rest[ne:]
        gv_t = rest[0]
        ev, r_t = rest[1:-1], rest[-1]
        xh = x_t * r_t
        dxh = dh_t * gv_t
        dx = r_t * (dxh - xh * jnp.mean(dxh * xh, axis=-1, keepdims=True)) + dres_t
        base = (dx, dh_t, dh_t * xh)
        if extra_fn is None:
            return base
        return base + tuple(extra_fn(dx, *er, *ev))

    return _ew(fn, [dh, x, dres] + list(extra_rows), vecs=[gv] + list(extra_vecs), cols=[r],
               outs=[('row', D, F32), ('sum', D), ('sum', D)] + list(extra_outs), name=name)


def _layer_bwd(dx2, sv, W, rc, tri, emit):
    mod = sv['mod']
    sh1, sc1, g1m, sh2, sc2, g2m = [mod[i:i + 1] for i in range(N_MOD)]
    D = D_MODEL
    p = sv['p']
    d_g2m, d_dn = _ew(lambda dx, dn, g: (dx * dn.astype(F32), dx * g), [dx2, sv['dn']], vecs=[g2m],
                      outs=[('sum', D), ('row', D, BF16)], name="gate_bwd")
    (d_up,) = _mm(d_dn, W["w_down"], tb=True, o_ex=[(sv['act'], 'o')],
                  epi=lambda acc, a: (acc * 2.0 * a.astype(F32),), outs=(BF16,), name="mm_down_dx")
    (gw_down,) = _mm(sv['act'], d_dn, ta=True, tk=DW_TK, pro=lambda a: a * a, outs=(BF16,), name="mm_down_dw")
    (gw_up,) = _mm(sv['x1'], d_up, ta=True, tm=512, tk=DW_TK,
                   a_ex=[(sv['r2'], 'k'), (sv['gv2'].reshape(1, D), 'm'), (sh2, 'm')], pro=_norm_mod,
                   outs=(BF16,), name="mm_up_dw")
    tok = emit(dict(w_down=gw_down, w_up=gw_up), "mlp")
    (d_h2,) = _mm(d_up, W["w_up"], tb=True, tk=2048, outs=(F32,), name="mm_up_dx", deps=[tok])
    dx1, d_sh2, s_h2, d_g1m, d_mo = _norm_bwd(
        d_h2, sv['x1'], sv['r2'], dx2, sv['gv2'], sv['gn2'],
        extra_rows=[sv['mo']], extra_vecs=[g1m],
        extra_fn=lambda dx, mo, g: (dx * mo.astype(F32), dx * g),
        extra_outs=[('sum', D), ('row', D, BF16)], name="norm_bwd_mlp")
    d_sc2 = sv['gn2'] * s_h2
    d_gn2 = (1.0 + sc2) * s_h2
    def mix_epi(acc, ya, yb, ga, gb):
        sa, sb = _sigmoid(ga.astype(F32)), _sigmoid(gb.astype(F32))
        return (acc * sa, acc * sb, acc * ya.astype(F32) * sa * (1.0 - sa), acc * yb.astype(F32) * sb * (1.0 - sb))

    d_ya, d_yb, d_ga, d_gb = _mm(d_mo, W["w_mix_out"], tb=True, tm=512,
                                 o_ex=[(sv['ya'], 'o'), (sv['yb'], 'o'), (p, 'o', O_GA), (p, 'o', O_GB)], epi=mix_epi,
                                 outs=(BF16,) * 4, name="mm_mix_dx")
    (gw_mix,) = _mm(sv['mg'], d_mo, ta=True, tk=DW_TK, outs=(BF16,), name="mm_mix_dw")

    def ro_epi(acc, g, yr):
        gf = g.astype(F32)
        s = _sigmoid(gf)
        return (acc * yr.astype(F32) * s * (1.0 + gf * (1.0 - s)), acc * gf * s)

    d_rg, d_yret = _mm(d_ya, W["w_ret_out"], tb=True, tm=512, o_ex=[(p, 'o', O_RG), (sv['yret'], 'o')], epi=ro_epi,
                       outs=(BF16, BF16), name="mm_ret_dx")
    (gw_ro,) = _mm(sv['yret'], d_ya, ta=True, tm=512, tk=DW_TK, a_ex=[(p, 'a', O_RG)],
                   pro=lambda yr, g: _silu(g.astype(F32)) * yr.astype(F32), outs=(BF16,), name="mm_ret_dw")
    (gw_so,) = _mm(sv['ysb'], d_yb, ta=True, tk=DW_TK, outs=(BF16,), name="mm_sb_dw")
    tok = emit(dict(w_mix_out=gw_mix, w_ret_out=gw_ro, w_sb_out=gw_so), "mix")
    (d_ysb,) = _mm(d_yb, W["w_sb_out"], tb=True, outs=(BF16,), name="mm_sb_dx", deps=[tok])
    d_sq, d_sk, d_sv = _sb_bwd(p, sv['sbc'], d_ysb, tri)
    d_rq, d_rk, d_rv = _ret_bwd(p, sv['rstate'], d_yret, rc)
    dp = jnp.concatenate([d_rq, d_rk, d_rv, d_rg, d_sq, d_sk, d_sv, d_ga, d_gb], axis=1)
    (gw_in,) = _mm(sv['x0'], dp, ta=True, tm=512, tk=DW_TK,
                   a_ex=[(sv['r1'], 'k'), (sv['gv1'].reshape(1, D), 'm'), (sh1, 'm')], pro=_norm_mod,
                   outs=(BF16,), name="mm_in_dw")
    tok = emit(dict(w_in=gw_in), "in")
    (d_h,) = _mm(dp, W["w_in"], tb=True, tk=IN_W // 4, outs=(F32,), name="mm_in_dx", deps=[tok])
    dx0, d_sh1, s_h1 = _norm_bwd(d_h, sv['x0'], sv['r1'], dx1, sv['gv1'], sv['gn1'], name="norm_bwd_mix")
    d_sc1 = sv['gn1'] * s_h1
    d_gn1 = (1.0 + sc1) * s_h1
    d_mod = jnp.concatenate([d_sh1, d_sc1, d_g1m, d_sh2, d_sc2, d_g2m], axis=1)
    return dx0, d_mod, d_gn1, d_gn2


def kernel(x, c, norm_mix_g, w_in, w_ret_out, w_sb_out, w_mix_out, norm_mlp_g, w_up, w_down, w_ada, b_ada, final_g, loss_target, m_norm_mix_g, m_w_in, m_w_ret_out, m_w_sb_out, m_w_mix_out, m_norm_mlp_g, m_w_up, m_w_down, m_w_ada, m_b_ada, m_final_g, v_norm_mix_g, v_w_in, v_w_ret_out, v_w_sb_out, v_w_mix_out, v_norm_mlp_g, v_w_up, v_w_down, v_w_ada, v_b_ada, v_final_g):
    S, D = x.shape[1], x.shape[2]
    x0 = x.reshape(S, D)
    tgt = loss_target.reshape(S, D)
    me = 4 * lax.axis_index("x") + 2 * lax.axis_index("y") + lax.axis_index("c")
    wts = dict(w_in=w_in, w_ret_out=w_ret_out, w_sb_out=w_sb_out, w_mix_out=w_mix_out, w_up=w_up, w_down=w_down)
    mts = dict(w_in=m_w_in, w_ret_out=m_w_ret_out, w_sb_out=m_w_sb_out, w_mix_out=m_w_mix_out, w_up=m_w_up, w_down=m_w_down)
    vts = dict(w_in=v_w_in, w_ret_out=v_w_ret_out, w_sb_out=v_w_sb_out, w_mix_out=v_w_mix_out, w_up=v_w_up, w_down=v_w_down)
    rc = _ret_consts(S)
    tri = _tri()

    (cact,) = _ew(lambda t: (_silu(t),), [jnp.pad(c, ((0, 7), (0, 0)))], outs=[('row', D, F32)], name="silu_c")
    (cact_all,) = _all_gather_lead([cact[0:1]], "gather_c")
    cact_all = cact_all.reshape(N_DEV, D)
    cact16 = jnp.pad(cact_all, ((0, 8), (0, 0)))
    n_ada = w_ada.shape[2]
    b_loc = lax.dynamic_slice_in_dim(b_ada, me * n_ada, n_ada, axis=1)
    mods = [_mod_partial(cact16, w_ada[l], b_loc[l:l + 1])[:N_DEV] for l in range(DEPTH)]
    modp = jnp.stack(mods, axis=1)
    (modr,) = _all_to_all_lead([modp], "scatter_mod")
    mod_full = jnp.transpose(modr, (1, 0, 2)).reshape(DEPTH, N_MOD, D)

    shards = {}
    for nm in _W_NAMES:
        w = wts[nm]
        (wb,) = _ew(lambda t: (t,), [w.reshape(-1, w.shape[-1])], outs=[('row', w.shape[-1], BF16)], name="cast_bf16")
        shards[nm] = wb.reshape(w.shape)
    rest = tuple(nm for nm in _W_NAMES if nm != "w_in")
    started, layer_groups = [modr], []
    for l in range(DEPTH):
        sh_l = {nm: shards[nm][l] for nm in _W_NAMES}
        groups = [(("w_in",), "%d_in" % l), (rest, "%d_rest" % l)] if l == 0 else [(_W_NAMES, "%d_all" % l)]
        layer_groups.append([])
        for names, tag in groups:
            layer_groups[-1].append((names, _gather_start(sh_l, names, tag, started[-1:])))
            started.append(layer_groups[-1][-1][1]['token'])
    layer_w = [_LayerWeights(g, started) for g in layer_groups]

    xs = x0
    saved = []
    for l in range(DEPTH):
        xs, sv = _layer_fwd(xs, mod_full[l], norm_mix_g[l:l + 1], norm_mlp_g[l:l + 1], layer_w[l], rc, tri)
        sv['W'] = layer_w[l]
        saved.append(sv)

    fg = final_g.reshape(1, D)

    def head(xt, tg, g):
        r = lax.rsqrt(jnp.mean(xt * xt, axis=-1, keepdims=True) + EPS)
        xh = xt * r
        e = xh * g - tg
        dy = e * (1.0 / D)
        dxh = dy * g
        dx = r * (dxh - xh * jnp.mean(dxh * xh, axis=-1, keepdims=True))
        return dx, dy * xh, 0.5 * e * e * (1.0 / D)

    dxs, d_fg, loss_cols = _ew(head, [xs, tgt], vecs=[fg], outs=[('row', D, F32), ('sum', D), ('sum', D)], name="loss_head")

    small = [None] * DEPTH
    pending = []
    for l in reversed(range(DEPTH)):
        sv = saved[l]

        def emit(gw, tag, l=l):
            names = tuple(gw)
            pending.append((l, names, _scatter_start(gw, names, "%d_%s" % (l, tag))))
            return pending[-1][2]['token']

        dxs, d_mod, d_gn1, d_gn2 = _layer_bwd(dxs, sv, sv['W'], rc, tri, emit)
        small[l] = (d_mod, d_gn1, d_gn2)
    grad_x = dxs.reshape(1, S, D)

    pack = jnp.concatenate([small[l][0] for l in range(DEPTH)] + [small[l][1] for l in range(DEPTH)]
                           + [small[l][2] for l in range(DEPTH)] + [d_fg, loss_cols], axis=1)
    (packs,) = _all_gather_lead([pack], "gather_small")
    packs = packs.reshape(N_DEV, -1)
    o = 0
    dmod_all = []
    for l in range(DEPTH):
        dmod_all.append(packs[:, o:o + N_MOD * D]); o += N_MOD * D
    gn1_parts = packs[:, o:o + DEPTH * D].reshape(N_DEV, DEPTH, D); o += DEPTH * D
    gn2_parts = packs[:, o:o + DEPTH * D].reshape(N_DEV, DEPTH, D); o += DEPTH * D
    fg_parts = packs[:, o:o + D].reshape(N_DEV, 1, D); o += D
    loss_parts = packs[:, o:o + D]
    (loss_sum,) = _ew(lambda t: (t,), [loss_parts], outs=[('sum', D)], name="loss_sum")
    loss = jnp.sum(loss_sum)

    res = {}
    per = {nm: [None] * DEPTH for nm in _W_NAMES}
    after = [dxs, loss_sum]
    for l, names, h in pending:
        for nm, landed in zip(names, _finish(h, after)):
            per[nm][l] = _adam(landed, wts[nm][l], mts[nm][l], vts[nm][l], "adam")
        after = [per[names[-1]][l][0]]
    for nm in _W_NAMES:
        res[nm] = [jnp.stack([per[nm][l][i] for l in range(DEPTH)]) for i in range(4)]
    res["norm_mix_g"] = _adam(gn1_parts, norm_mix_g, m_norm_mix_g, v_norm_mix_g, "adam")
    res["norm_mlp_g"] = _adam(gn2_parts, norm_mlp_g, m_norm_mlp_g, v_norm_mlp_g, "adam")
    fgr = _adam(fg_parts, fg, m_final_g.reshape(1, D), v_final_g.reshape(1, D), "adam")
    res["final_g"] = [t.reshape(D) for t in fgr]
    bparts = jnp.stack(dmod_all, axis=1)
    res["b_ada"] = _adam(bparts, b_ada, m_b_ada, v_b_ada, "adam")
    cact_t = cact_all.T
    ada = []
    for l in range(DEPTH):
        dm_loc = lax.dynamic_slice_in_dim(dmod_all[l], me * n_ada, n_ada, axis=1)
        ada.append(_adam(_ada_grad(cact_t, dm_loc), w_ada[l], m_w_ada[l], v_w_ada[l], "adam"))
    res["w_ada"] = [jnp.stack([ada[l][i] for l in range(DEPTH)]) for i in range(4)]

    order = ['norm_mix_g', 'w_in', 'w_ret_out', 'w_sb_out', 'w_mix_out', 'norm_mlp_g', 'w_up', 'w_down', 'w_ada', 'b_ada', 'final_g']
    out = [loss, grad_x]
    for i in range(4):
        out += [res[nm][i] for nm in order]
    return tuple(out)
```

```python
import functools
import math

import jax
import jax.numpy as jnp
import numpy as np
from jax import lax
from jax.experimental import pallas as pl
from jax.experimental.pallas import tpu as pltpu

F32 = jnp.float32
BF16 = jnp.bfloat16

N_DEV = 8
D_MODEL = 1024
DEPTH = 2
RET_HEADS = 4
RET_QK = 256
RET_V = 512
RET_CHUNK = 128
ROPE_BASE = 10000.0
SB_HEADS = 16
SB_DIM = 64
D_FF = 4096
N_MOD = 6
EPS = 1e-6
GN_EPS = 1e-5
O_RQ, O_RK, O_RV, O_RG, O_SQ, O_SK, O_SV, O_GA, O_GB = 0, 1024, 2048, 4096, 6144, 7168, 8192, 9216, 10240
IN_W = 11264

ADAM_LR, ADAM_B1, ADAM_B2, ADAM_EPS, ADAM_WD, ADAM_STEP = 0.001, 0.9, 0.999, 1e-08, 0.01, 10

VMEM_LIMIT = 56 * 1024 * 1024
DW_TK = 2048


def _cparams(sem):
    return pltpu.CompilerParams(dimension_semantics=sem, vmem_limit_bytes=VMEM_LIMIT)


def _mm(a, b, *, ta=False, tb=False, tm=1024, tn=1024, tk=None, a_ex=(), pro=None, o_ex=(), epi=None,
        outs=(F32,), name, deps=()):
    if ta:
        K, M = a.shape
    else:
        M, K = a.shape
    N = b.shape[0] if tb else b.shape[1]
    tm, tn, tk = min(tm, M), min(tn, N), K if tk is None else min(tk, K)
    assert M % tm == 0 and N % tn == 0 and K % tk == 0, (name, M, N, K, tm, tn, tk)
    nk = K // tk
    in_specs = [
        pl.BlockSpec((tk, tm), lambda i, j, k: (k, i)) if ta else pl.BlockSpec((tm, tk), lambda i, j, k: (i, k)),
        pl.BlockSpec((tn, tk), lambda i, j, k: (j, k)) if tb else pl.BlockSpec((tk, tn), lambda i, j, k: (k, j)),
    ]
    args = [a, b]
    for arr, kind, *off in a_ex:
        off = off[0] if off else 0
        if kind == 'a' and ta:
            assert off % tm == 0
            in_specs.append(pl.BlockSpec((tk, tm), lambda i, j, k, o=off // tm: (k, o + i)))
        elif kind == 'a':
            assert off % tk == 0
            in_specs.append(pl.BlockSpec((tm, tk), lambda i, j, k, o=off // tk: (i, o + k)))
        elif kind == 'k':
            in_specs.append(pl.BlockSpec((tk, 1), lambda i, j, k: (k, 0)) if ta
                            else pl.BlockSpec((1, tk), lambda i, j, k: (0, k)))
        else:
            in_specs.append(pl.BlockSpec((1, tm), lambda i, j, k: (0, i)) if ta
                            else pl.BlockSpec((tm, 1), lambda i, j, k: (i, 0)))
        args.append(arr)
    for arr, kind, *off in o_ex:
        off = off[0] if off else 0
        if kind == 'o':
            assert off % tn == 0
            in_specs.append(pl.BlockSpec((tm, tn), lambda i, j, k, o=off // tn: (i, o + j)))
        elif kind == 'n':
            in_specs.append(pl.BlockSpec((1, tn), lambda i, j, k: (0, j)))
        else:
            in_specs.append(pl.BlockSpec((tm, 1), lambda i, j, k: (i, 0)))
        args.append(arr)
    for arr in deps:
        in_specs.append(pl.BlockSpec(memory_space=pl.ANY))
        args.append(arr)
    na, no, nout, nd = len(a_ex), len(o_ex), len(outs), len(deps)
    dims = (((0 if ta else 1,), (1 if tb else 0,)), ((), ()))

    def body(*refs):
        a_ref, b_ref = refs[0], refs[1]
        aex = refs[2:2 + na]
        oex = refs[2 + na:2 + na + no]
        out_refs = refs[2 + na + no + nd:2 + na + no + nd + nout]

        def product():
            at = a_ref[...]
            if pro is not None:
                at = pro(at, *[r[...] for r in aex])
            return lax.dot_general(at.astype(BF16), b_ref[...].astype(BF16), dims, preferred_element_type=F32)

        def finish(res):
            vals = epi(res, *[r[...] for r in oex]) if epi is not None else (res,)
            for o_ref, v in zip(out_refs, vals):
                o_ref[...] = v.astype(o_ref.dtype)

        if nk == 1:
            finish(product())
            return
        acc = refs[-1]
        k = pl.program_id(2)

        @pl.when(k == 0)
        def _():
            acc[...] = product()

        @pl.when(k > 0)
        def _():
            acc[...] += product()

        @pl.when(k == nk - 1)
        def _():
            finish(acc[...])

    res = pl.pallas_call(
        body, name=name, grid=(M // tm, N // tn, nk), in_specs=in_specs,
        out_specs=[pl.BlockSpec((tm, tn), lambda i, j, k: (i, j)) for _ in outs],
        out_shape=[jax.ShapeDtypeStruct((M, N), dt) for dt in outs],
        scratch_shapes=[pltpu.VMEM((tm, tn), F32)] if nk > 1 else [],
        compiler_params=_cparams(("parallel", "parallel", "arbitrary")),
    )(*args)
    return res


def _ew(fn, rows, vecs=(), cols=(), outs=(), tr=256, name=None):
    S = rows[0].shape[0]
    tr = min(tr, S)
    assert S % tr == 0
    in_specs, args = [], []
    for r in rows:
        in_specs.append(pl.BlockSpec((tr, r.shape[1]), lambda i: (i, 0)))
        args.append(r)
    for v in vecs:
        in_specs.append(pl.BlockSpec((1, v.shape[1]), lambda i: (0, 0)))
        args.append(v)
    for c in cols:
        in_specs.append(pl.BlockSpec((tr, 1), lambda i: (i, 0)))
        args.append(c)
    out_specs, out_shape = [], []
    for o in outs:
        if o[0] == 'row':
            out_specs.append(pl.BlockSpec((tr, o[1]), lambda i: (i, 0)))
            out_shape.append(jax.ShapeDtypeStruct((S, o[1]), o[2]))
        elif o[0] == 'sum':
            out_specs.append(pl.BlockSpec((1, o[1]), lambda i: (0, 0)))
            out_shape.append(jax.ShapeDtypeStruct((1, o[1]), F32))
        else:
            out_specs.append(pl.BlockSpec((tr, 1), lambda i: (i, 0)))
            out_shape.append(jax.ShapeDtypeStruct((S, 1), o[1]))
    nin = len(args)

    def body(*refs):
        i = pl.program_id(0)
        vals = fn(*[r[...] for r in refs[:nin]])
        for o, o_ref, v in zip(outs, refs[nin:], vals):
            if o[0] == 'sum':
                @pl.when(i == 0)
                def _():
                    o_ref[...] = jnp.zeros_like(o_ref)
                o_ref[...] += jnp.sum(v.astype(F32), axis=0, keepdims=True)
            else:
                o_ref[...] = v.astype(o_ref.dtype)

    return pl.pallas_call(
        body, name=name, grid=(S // tr,), in_specs=in_specs, out_specs=out_specs, out_shape=out_shape,
        compiler_params=_cparams(("arbitrary",)),
    )(*args)


def _sigmoid(x):
    return 1.0 / (1.0 + jnp.exp(-x))


def _ret_consts(S):
    h = np.arange(RET_HEADS, dtype=np.float64)
    log_gamma = np.log1p(-np.power(2.0, -5.0 - h))
    idx = np.arange(RET_CHUNK, dtype=np.float64)
    rel = idx[:, None] - idx[None, :]
    decay = np.where(rel >= 0, np.exp(np.maximum(rel, 0.0) * log_gamma[:, None, None]), 0.0)
    xi = np.exp((idx + 1.0) * log_gamma[:, None])[:, :, None]
    zeta = np.exp((RET_CHUNK - 1.0 - idx) * log_gamma[:, None])[:, :, None]
    gamma_c = np.exp(RET_CHUNK * log_gamma)[:, None, None]
    half = RET_QK // 2
    inv_freq = np.power(ROPE_BASE, -np.arange(half, dtype=np.float64) / half).astype(np.float32)
    ang = np.arange(S, dtype=np.float32)[:, None] * inv_freq[None, :]
    f = lambda t: jnp.asarray(t, F32)
    return dict(decay=f(decay), xi=f(xi), zeta=f(zeta), gc=f(gamma_c), cos=f(np.cos(ang)), sin=f(np.sin(ang)))


def _rot(t, cos, sin):
    half = RET_QK // 2
    t1, t2 = t[:, :half], t[:, half:]
    return jnp.concatenate([t1 * cos - t2 * sin, t1 * sin + t2 * cos], axis=-1)


def _rot_inv(t, cos, sin):
    half = RET_QK // 2
    t1, t2 = t[:, :half], t[:, half:]
    return jnp.concatenate([t1 * cos + t2 * sin, t2 * cos - t1 * sin], axis=-1)


_NT = (((1,), (1,)), ((), ()))
_TN = (((0,), (0,)), ((), ()))


def _dot(a, b):
    return jnp.dot(a, b, preferred_element_type=F32)


def _dot_nt(a, b):
    return lax.dot_general(a, b, _NT, preferred_element_type=F32)


def _dot_tn(a, b):
    return lax.dot_general(a, b, _TN, preferred_element_type=F32)


def _ret_in_specs(C, rev, NC):
    n_of = (lambda n: NC - 1 - n) if rev else (lambda n: n)
    qb, vb = O_RQ // RET_QK, O_RV // RET_V
    kb = O_RK // RET_QK
    return [
        pl.BlockSpec((C, RET_QK), lambda h, n: (n_of(n), qb + h)),
        pl.BlockSpec((C, RET_QK), lambda h, n: (n_of(n), kb + h)),
        pl.BlockSpec((C, RET_V), lambda h, n: (n_of(n), vb + h)),
        pl.BlockSpec((C, RET_QK // 2), lambda h, n: (n_of(n), 0)),
        pl.BlockSpec((C, RET_QK // 2), lambda h, n: (n_of(n), 0)),
        pl.BlockSpec((1, C, C), lambda h, n: (h, 0, 0)),
        pl.BlockSpec((1, C, 1), lambda h, n: (h, 0, 0)),
        pl.BlockSpec((1, C, 1), lambda h, n: (h, 0, 0)),
        pl.BlockSpec((1, 1, 1), lambda h, n: (h, 0, 0)),
    ]


def _ret_fwd(p, rc):
    S = p.shape[0]
    C = RET_CHUNK
    NC = S // C

    def body(q_ref, k_ref, v_ref, cos_ref, sin_ref, dec_ref, xi_ref, zeta_ref, gc_ref, y_ref, rs_ref, r_acc):
        n = pl.program_id(1)

        @pl.when(n == 0)
        def _():
            r_acc[...] = jnp.zeros_like(r_acc)

        cos, sin = cos_ref[...], sin_ref[...]
        q = _rot(q_ref[...].astype(F32), cos, sin).astype(BF16)
        kf = _rot(k_ref[...].astype(F32), cos, sin) * (RET_QK ** -0.5)
        k = kf.astype(BF16)
        v = v_ref[...]
        r = r_acc[...]
        rb = r.astype(BF16)
        rs_ref[0, 0] = rb
        s = (_dot_nt(q, k) * dec_ref[0]).astype(BF16)
        o = _dot(s, v) + _dot(q, rb) * xi_ref[0]
        mu = jnp.mean(o, axis=-1, keepdims=True)
        var = jnp.mean(jnp.square(o - mu), axis=-1, keepdims=True)
        y_ref[...] = ((o - mu) * lax.rsqrt(var + GN_EPS)).astype(y_ref.dtype)
        kz = (kf * zeta_ref[0]).astype(BF16)
        r_acc[...] = r * gc_ref[0] + _dot_tn(kz, v)

    return pl.pallas_call(
        body, name="ret_fwd", grid=(RET_HEADS, NC), in_specs=_ret_in_specs(C, False, NC),
        out_specs=[pl.BlockSpec((C, RET_V), lambda h, n: (n, h)),
                   pl.BlockSpec((1, 1, RET_QK, RET_V), lambda h, n: (h, n, 0, 0))],
        out_shape=[jax.ShapeDtypeStruct((S, RET_HEADS * RET_V), BF16),
                   jax.ShapeDtypeStruct((RET_HEADS, NC, RET_QK, RET_V), BF16)],
        scratch_shapes=[pltpu.VMEM((RET_QK, RET_V), F32)],
        compiler_params=_cparams(("parallel", "arbitrary")),
    )(p, p, p, rc['cos'], rc['sin'], rc['decay'], rc['xi'], rc['zeta'], rc['gc'])


def _ret_bwd(p, rstate, dy, rc):
    S = p.shape[0]
    C = RET_CHUNK
    NC = S // C

    def body(q_ref, k_ref, v_ref, cos_ref, sin_ref, dec_ref, xi_ref, zeta_ref, gc_ref, rs_ref, dy_ref,
             dq_ref, dk_ref, dv_ref, dr_acc):
        t = pl.program_id(1)

        @pl.when(t == 0)
        def _():
            dr_acc[...] = jnp.zeros_like(dr_acc)

        cos, sin = cos_ref[...], sin_ref[...]
        dec, xi, zeta = dec_ref[0], xi_ref[0], zeta_ref[0]
        q = _rot(q_ref[...].astype(F32), cos, sin).astype(BF16)
        kf = _rot(k_ref[...].astype(F32), cos, sin) * (RET_QK ** -0.5)
        k = kf.astype(BF16)
        kz = (kf * zeta).astype(BF16)
        v = v_ref[...]
        rb = rs_ref[0, 0]
        s = (_dot_nt(q, k) * dec).astype(BF16)
        o = _dot(s, v) + _dot(q, rb) * xi
        mu = jnp.mean(o, axis=-1, keepdims=True)
        var = jnp.mean(jnp.square(o - mu), axis=-1, keepdims=True)
        rstd = lax.rsqrt(var + GN_EPS)
        yh = (o - mu) * rstd
        dyf = dy_ref[...].astype(F32)
        do = (dyf - jnp.mean(dyf, axis=-1, keepdims=True) - yh * jnp.mean(dyf * yh, axis=-1, keepdims=True)) * rstd
        dob = do.astype(BF16)
        doxi = (do * xi).astype(BF16)
        dr = dr_acc[...]
        drb = dr.astype(BF16)
        ds = (_dot_nt(dob, v) * dec).astype(BF16)
        dq = _dot(ds, k) + _dot_nt(doxi, rb)
        dk = _dot_tn(ds, q) + _dot_nt(v, drb) * zeta
        dv = _dot_tn(s, dob) + _dot(kz, drb)
        dr_acc[...] = dr * gc_ref[0] + _dot_tn(q, doxi)
        dq_ref[...] = _rot_inv(dq, cos, sin).astype(dq_ref.dtype)
        dk_ref[...] = (_rot_inv(dk, cos, sin) * (RET_QK ** -0.5)).astype(dk_ref.dtype)
        dv_ref[...] = dv.astype(dv_ref.dtype)

    rn = lambda n: NC - 1 - n
    in_specs = _ret_in_specs(C, True, NC) + [
        pl.BlockSpec((1, 1, RET_QK, RET_V), lambda h, n: (h, rn(n), 0, 0)),
        pl.BlockSpec((C, RET_V), lambda h, n: (rn(n), h)),
    ]
    return pl.pallas_call(
        body, name="ret_bwd", grid=(RET_HEADS, NC), in_specs=in_specs,
        out_specs=[pl.BlockSpec((C, RET_QK), lambda h, n: (rn(n), h)),
                   pl.BlockSpec((C, RET_QK), lambda h, n: (rn(n), h)),
                   pl.BlockSpec((C, RET_V), lambda h, n: (rn(n), h))],
        out_shape=[jax.ShapeDtypeStruct((S, RET_HEADS * RET_QK), BF16),
                   jax.ShapeDtypeStruct((S, RET_HEADS * RET_QK), BF16),
                   jax.ShapeDtypeStruct((S, RET_HEADS * RET_V), BF16)],
        scratch_shapes=[pltpu.VMEM((RET_QK, RET_V), F32)],
        compiler_params=_cparams(("parallel", "arbitrary")),
    )(p, p, p, rc['cos'], rc['sin'], rc['decay'], rc['xi'], rc['zeta'], rc['gc'], rstate, dy)


SB_T = 256
SB_SCALE = SB_DIM ** -0.5


def _tri():
    j = np.arange(SB_T)
    after = (j[:, None] > j[None, :]).astype(np.float32)
    upto = (j[:, None] <= j[None, :]).astype(np.float32)
    return jnp.asarray(np.stack([after, upto]), BF16)


def _softplus_parts(z):
    neg_abs = lax.bitcast_convert_type(lax.bitcast_convert_type(z, jnp.uint32) | jnp.uint32(0x80000000), F32)
    e = jnp.exp(neg_abs)
    return jnp.maximum(z, 0.0) + jnp.log(1.0 + e), e


def _sb_fwd(p, tri):
    S = p.shape[0]
    T = min(SB_T, S)
    NQ = S // T
    assert NQ <= 128
    qb, kb, vb = O_SQ // 128, O_SK // 128, O_SV // 128

    def body(q_ref, k_ref, v_ref, tri_ref, o_ref, cs_ref, o_acc, run, zbuf, abuf):
        i = pl.program_id(1)
        lane = lax.broadcasted_iota(jnp.int32, (1, 128), 1)
        tri_after = tri_ref[0]
        qs = [jnp.where((lane >= 64) if hh else (lane < 64), q_ref[...], jnp.zeros_like(q_ref[...]))
              * jnp.asarray(SB_SCALE, BF16) for hh in range(2)]
        cs_ref[...] = jnp.zeros_like(cs_ref)
        o_acc[...] = jnp.zeros_like(o_acc)
        run[...] = jnp.zeros_like(run)

        def kv(ref, j):
            return ref[pl.ds(pl.multiple_of(j * T, T), T), :]

        for hh in range(2):
            zbuf[hh] = _dot_nt(qs[hh], kv(k_ref, i))

        def block(t, diagonal):
            j = i - t
            if diagonal:
                msk = lax.broadcasted_iota(jnp.int32, (T, T), 1) < lax.broadcasted_iota(jnp.int32, (T, T), 0)
            lss, exs, tot, zn = [], [], [], []
            for hh in range(2):
                z = zbuf[hh]
                sp, _ = _softplus_parts(z)
                lss.append(z - sp)
                if diagonal:
                    sp = jnp.where(msk, sp, 0.0)
                exs.append(_dot(sp.astype(BF16), tri_after))
                tot.append(sp[:, 0:1])
                zn.append(_dot_nt(qs[hh], kv(k_ref, jnp.maximum(j - 1, 0))))
            if not diagonal:
                av = [_dot(abuf[hh], kv(v_ref, j + 1)) for hh in range(2)]
            for hh in range(2):
                csl = slice(hh * 128, (hh + 1) * 128)
                cs = run[hh]
                a = jnp.exp(lss[hh] - exs[hh] - cs)
                if diagonal:
                    a = jnp.where(msk, a, 0.0)
                abuf[hh] = a.astype(BF16)
                cs_ref[:, csl] = jnp.where(lane == j, cs, cs_ref[:, csl])
                run[hh] = cs + exs[hh][:, 0:1] + tot[hh]
            for hh in range(2):
                if not diagonal:
                    o_acc[hh] += av[hh]
                zbuf[hh] = zn[hh]

        block(0, True)

        def step(t, carry):
            block(t, False)
            return carry

        lax.fori_loop(1, i + 1, step, 0)
        o_ref[...] = jnp.where(lane < 64, o_acc[0] + _dot(abuf[0], kv(v_ref, 0)),
                               o_acc[1] + _dot(abuf[1], kv(v_ref, 0))).astype(o_ref.dtype)

    return pl.pallas_call(
        body, name="sb_fwd", grid=(SB_HEADS // 2, NQ),
        scratch_shapes=[pltpu.VMEM((2, T, 128), F32), pltpu.VMEM((2, T, 1), F32), pltpu.VMEM((2, T, T), F32),
                        pltpu.VMEM((2, T, T), BF16)],
        in_specs=[pl.BlockSpec((T, 128), lambda h, i: (i, qb + h)),
                  pl.BlockSpec((S, 128), lambda h, i: (0, kb + h)),
                  pl.BlockSpec((S, 128), lambda h, i: (0, vb + h)),
                  pl.BlockSpec((1, T, T), lambda h, i: (0, 0, 0))],
        out_specs=[pl.BlockSpec((T, 128), lambda h, i: (i, h)),
                   pl.BlockSpec((T, 256), lambda h, i: (i, h))],
        out_shape=[jax.ShapeDtypeStruct((S, SB_HEADS * SB_DIM), BF16),
                   jax.ShapeDtypeStruct((S, SB_HEADS * 128), F32)],
        compiler_params=_cparams(("parallel", "arbitrary")),
    )(p, p, p, tri)


def _sb_bwd(p, carries, dy, tri):
    S = p.shape[0]
    T = min(SB_T, S)
    NQ = S // T
    qb, kb, vb = O_SQ // 128, O_SK // 128, O_SV // 128

    def body(q_ref, k_ref, v_ref, cs_ref, dy_ref, tri_ref, dq_ref, dk_ref, dv_ref, dk_acc, dv_acc, dq_acc, run,
             zbuf, dabuf, dzbuf, abuf):
        i = pl.program_id(1)

        @pl.when(i == 0)
        def _():
            dk_acc[...] = jnp.zeros_like(dk_acc)
            dv_acc[...] = jnp.zeros_like(dv_acc)

        lane = lax.broadcasted_iota(jnp.int32, (1, 128), 1)
        tri_after, tri_upto = tri_ref[0], tri_ref[1]
        hms = [(lane >= 64) if hh else (lane < 64) for hh in range(2)]
        qs = [jnp.where(hm, q_ref[...], jnp.zeros_like(q_ref[...])) * jnp.asarray(SB_SCALE, BF16) for hm in hms]
        dos = [jnp.where(hm, dy_ref[...], jnp.zeros_like(dy_ref[...])) for hm in hms]
        qst = [t.T for t in qs]
        dost = [t.T for t in dos]
        dq_acc[...] = jnp.zeros_like(dq_acc)
        run[...] = jnp.zeros_like(run)
        dzbuf[...] = jnp.zeros_like(dzbuf)
        abuf[...] = jnp.zeros_like(abuf)

        def kv(ref, j):
            return ref[pl.ds(pl.multiple_of(j * T, T), T), :]

        def flush(jp):
            kp = kv(k_ref, jp)
            dq_add = [_dot(dzbuf[hh], kp) for hh in range(2)]
            dk_add = _dot(qst[0], dzbuf[0]) + _dot(qst[1], dzbuf[1])
            dv_add = _dot(dost[0], abuf[0]) + _dot(dost[1], abuf[1])
            return dq_add, dk_add, dv_add

        def apply(jp, adds):
            dq_add, dk_add, dv_add = adds
            cols = pl.ds(pl.multiple_of(jp * T, T), T)
            for hh in range(2):
                dq_acc[hh] += dq_add[hh]
            dk_acc[:, cols] += dk_add
            dv_acc[:, cols] += dv_add

        for hh in range(2):
            zbuf[hh] = _dot_nt(qs[hh], kv(k_ref, 0))
            dabuf[hh] = _dot_nt(dos[hh], kv(v_ref, 0))

        def block(j, diagonal):
            jp = jnp.maximum(j - 1, 0)
            if diagonal:
                msk = lax.broadcasted_iota(jnp.int32, (T, T), 1) < lax.broadcasted_iota(jnp.int32, (T, T), 0)
            sigs, lss, exs, zn, dan = [], [], [], [], []
            for hh in range(2):
                z = zbuf[hh]
                sp, e = _softplus_parts(z)
                r = 1.0 / (1.0 + e)
                sigs.append(jnp.where(z >= 0, r, e * r))
                lss.append(z - sp)
                if diagonal:
                    sp = jnp.where(msk, sp, 0.0)
                exs.append(_dot(sp.astype(BF16), tri_after))
                if not diagonal:
                    zn.append(_dot_nt(qs[hh], kv(k_ref, j + 1)))
            adds = flush(jp)
            pgs, gs = [], []
            for hh in range(2):
                csl = slice(hh * 128, (hh + 1) * 128)
                cs = jnp.sum(jnp.where(lane == j, cs_ref[:, csl], 0.0), axis=-1, keepdims=True)
                a = jnp.exp(lss[hh] - exs[hh] - cs)
                if diagonal:
                    a = jnp.where(msk, a, 0.0)
                abuf_new = a.astype(BF16)
                g = a * dabuf[hh]
                gs.append((g, abuf_new))
                pgs.append(_dot(g.astype(BF16), tri_upto))
                if not diagonal:
                    dan.append(_dot_nt(dos[hh], kv(v_ref, j + 1)))
            for hh in range(2):
                g, abuf_new = gs[hh]
                cg = run[hh]
                dz = g - sigs[hh] * (cg + pgs[hh])
                if diagonal:
                    dz = jnp.where(msk, dz, 0.0)
                run[hh] = cg + pgs[hh][:, T - 1:T]
                dzbuf[hh] = dz.astype(BF16)
                abuf[hh] = abuf_new
            apply(jp, adds)
            if not diagonal:
                for hh in range(2):
                    zbuf[hh] = zn[hh]
                    dabuf[hh] = dan[hh]

        def step(j, carry):
            block(j, False)
            return carry

        lax.fori_loop(0, i, step, 0)
        block(i, True)
        apply(i, flush(i))
        dq_ref[...] = (jnp.where(lane < 64, dq_acc[0], dq_acc[1]) * SB_SCALE).astype(dq_ref.dtype)

        @pl.when(i == NQ - 1)
        def _():
            dk_ref[...] = dk_acc[...].T.astype(dk_ref.dtype)
            dv_ref[...] = dv_acc[...].T.astype(dv_ref.dtype)

    W = SB_HEADS * SB_DIM
    return pl.pallas_call(
        body, name="sb_bwd", grid=(SB_HEADS // 2, NQ),
        in_specs=[pl.BlockSpec((T, 128), lambda h, i: (i, qb + h)),
                  pl.BlockSpec((S, 128), lambda h, i: (0, kb + h)),
                  pl.BlockSpec((S, 128), lambda h, i: (0, vb + h)),
                  pl.BlockSpec((T, 256), lambda h, i: (i, h)),
                  pl.BlockSpec((T, 128), lambda h, i: (i, h)),
                  pl.BlockSpec((2, T, T), lambda h, i: (0, 0, 0))],
        out_specs=[pl.BlockSpec((T, 128), lambda h, i: (i, h)),
                   pl.BlockSpec((S, 128), lambda h, i: (0, h)),
                   pl.BlockSpec((S, 128), lambda h, i: (0, h))],
        out_shape=[jax.ShapeDtypeStruct((S, W), BF16)] * 3,
        scratch_shapes=[pltpu.VMEM((128, S), F32), pltpu.VMEM((128, S), F32), pltpu.VMEM((2, T, 128), F32),
                        pltpu.VMEM((2, T, 1), F32), pltpu.VMEM((2, T, T), F32), pltpu.VMEM((2, T, T), F32),
                        pltpu.VMEM((2, T, T), BF16), pltpu.VMEM((2, T, T), BF16)],
        compiler_params=_cparams(("parallel", "arbitrary")),
    )(p, p, p, carries, dy, tri)


def _exchange(srcs, out_shapes, src_slice, dst_slice, name):
    n = len(srcs)

    def body(*refs):
        ins, outs = refs[:n], refs[n:2 * n]
        send_sems, recv_sems, loc_sems = refs[2 * n:]
        x, y, c = lax.axis_index("x"), lax.axis_index("y"), lax.axis_index("c")
        me = 4 * x + 2 * y + c
        local = [pltpu.make_async_copy(src_slice(t, ins[t], me), dst_slice(t, outs[t], me), loc_sems.at[t])
                 for t in range(n)]
        for cp in local:
            cp.start()
        sends, recvs = [], []
        for k in (1, 2, 4, 6, 3, 5, 7):
            px = 1 - x if k & 4 else x
            py = 1 - y if k & 2 else y
            pc = 1 - c if k & 1 else c
            peer = 4 * px + 2 * py + pc
            for t in range(n):
                s = t * 7 + k - 1
                sends.append(pltpu.make_async_remote_copy(
                    src_ref=src_slice(t, ins[t], peer), dst_ref=dst_slice(t, outs[t], me),
                    send_sem=send_sems.at[s], recv_sem=recv_sems.at[s],
                    device_id=(px, py, pc), device_id_type=pl.DeviceIdType.MESH))
                recvs.append(pltpu.make_async_remote_copy(
                    src_ref=src_slice(t, ins[t], me), dst_ref=dst_slice(t, outs[t], peer),
                    send_sem=send_sems.at[s], recv_sem=recv_sems.at[s],
                    device_id=(px, py, pc), device_id_type=pl.DeviceIdType.MESH))
        for cp in sends:
            cp.start()
        for cp in recvs:
            cp.wait_recv()
        for cp in sends:
            cp.wait_send()
        for cp in local:
            cp.wait()

    anyspec = pl.BlockSpec(memory_space=pl.ANY)
    return pl.pallas_call(
        body, name=name, in_specs=[anyspec] * n, out_specs=[anyspec] * n,
        out_shape=[jax.ShapeDtypeStruct(s, d) for s, d in out_shapes],
        scratch_shapes=[pltpu.SemaphoreType.DMA((7 * n,)), pltpu.SemaphoreType.DMA((7 * n,)),
                        pltpu.SemaphoreType.DMA((n,))],
    )(*srcs)


def _all_gather_lead(xs, name):
    return _exchange(
        xs, [((N_DEV,) + x.shape, x.dtype) for x in xs],
        lambda t, ref, peer: ref, lambda t, ref, who: ref.at[who], name)


def _all_to_all_lead(xs, name):
    return _exchange(
        xs, [(x.shape, x.dtype) for x in xs],
        lambda t, ref, peer: ref.at[peer], lambda t, ref, who: ref.at[who], name)


_W_AXIS = {"w_in": 1, "w_ret_out": 0, "w_sb_out": 0, "w_mix_out": 0, "w_up": 1, "w_down": 0}
_W_NAMES = tuple(_W_AXIS)


def _window(ref, axis, who, width, count=1):
    start = pl.multiple_of(who * width, width)
    return ref.at[pl.ds(start, count * width), :] if axis == 0 else ref.at[:, pl.ds(start, count * width)]


_HBM = pl.BlockSpec(memory_space=pltpu.HBM)
_SEM = pl.BlockSpec(memory_space=pltpu.SEMAPHORE)
_EFFECT = pltpu.SideEffectType.DATAFLOW_SIDE_EFFECTING


def _exchange_start(srcs, shapes, src_slice, dst_slice, name, deps=()):
    n, nd = len(srcs), len(deps)
    lands = [pltpu.with_memory_space_constraint(lax.empty(s, d), pltpu.HBM) for s, d in shapes]

    def body(*refs):
        ins, lnd = refs[:n], refs[n:2 * n]
        sems = refs[2 * n + nd:4 * n + nd]
        token = refs[6 * n + nd]
        x, y, c = lax.axis_index("x"), lax.axis_index("y"), lax.axis_index("c")
        me = 4 * x + 2 * y + c
        for k in (0, 1, 2, 4, 6, 3, 5, 7):
            px = 1 - x if k & 4 else x
            py = 1 - y if k & 2 else y
            pc = 1 - c if k & 1 else c
            peer = 4 * px + 2 * py + pc
            for t in range(n):
                pltpu.make_async_remote_copy(
                    src_ref=src_slice(t, ins[t], peer), dst_ref=dst_slice(t, lnd[t], me),
                    send_sem=sems[2 * t], recv_sem=sems[2 * t + 1],
                    device_id=(px, py, pc), device_id_type=pl.DeviceIdType.MESH).start()
        token[...] = jnp.zeros_like(token)

    res = pl.pallas_call(
        body, name=name, in_specs=[_HBM] * (2 * n) + [pl.BlockSpec(memory_space=pl.ANY)] * nd,
        out_specs=[_SEM] * (2 * n) + [_HBM] * (2 * n) + [pl.BlockSpec(memory_space=pltpu.VMEM)],
        out_shape=[pltpu.SemaphoreType.DMA(())] * (2 * n) + [pltpu.HBM(s.shape, s.dtype) for s in srcs]
        + [pltpu.HBM(s.shape, s.dtype) for s in lands] + [jax.ShapeDtypeStruct((8, 128), F32)],
        input_output_aliases={t: 2 * n + t for t in range(2 * n)},
        compiler_params=pltpu.CompilerParams(has_side_effects=_EFFECT),
    )(*[pltpu.with_memory_space_constraint(s, pltpu.HBM) for s in srcs], *lands, *deps)
    return dict(n=n, sems=res[:2 * n], srcs=res[2 * n:3 * n], lands=res[3 * n:4 * n], token=res[4 * n])


def _exchange_wait(h, after, name):
    n = h['n']

    def body(*refs):
        lnd = refs[n:2 * n]
        sems = refs[2 * n:4 * n]
        x, y, c = lax.axis_index("x"), lax.axis_index("y"), lax.axis_index("c")
        for t in range(n):
            w = lnd[t]
            cp = pltpu.make_async_remote_copy(src_ref=w, dst_ref=w, send_sem=sems[2 * t], recv_sem=sems[2 * t + 1],
                                              device_id=(x, y, 1 - c), device_id_type=pl.DeviceIdType.MESH)
            cp.wait_send()
            cp.wait_recv()

    after = list(after)
    res = pl.pallas_call(
        body, name=name,
        in_specs=[_HBM] * (2 * n) + [_SEM] * (2 * n) + [pl.BlockSpec(memory_space=pl.ANY)] * len(after),
        out_specs=[_HBM] * (2 * n),
        out_shape=[pltpu.HBM(s.shape, s.dtype) for s in h['srcs']] + [pltpu.HBM(s.shape, s.dtype) for s in h['lands']],
        input_output_aliases={t: t for t in range(2 * n)},
        compiler_params=pltpu.CompilerParams(has_side_effects=_EFFECT),
    )(*h['srcs'], *h['lands'], *h['sems'], *after)
    return list(res[n:])


def _gather_start(shards, names, tag, deps=()):
    xs = [shards[nm] for nm in names]
    axes = [_W_AXIS[nm] for nm in names]
    widths = [x.shape[ax] for x, ax in zip(xs, axes)]
    shapes = [(tuple(d * (N_DEV if a == ax else 1) for a, d in enumerate(x.shape)), x.dtype) for x, ax in zip(xs, axes)]
    src = lambda t, ref, peer: ref
    dst = lambda t, ref, who: _window(ref, axes[t], who, widths[t])
    h = _exchange_start(xs, shapes, src, dst, "gw_start_" + tag, deps)
    h['tag'] = "gw_wait_" + tag
    return h


def _scatter_start(grads, names, tag):
    xs = [grads[nm] for nm in names]
    axes = [_W_AXIS[nm] for nm in names]
    widths = [x.shape[ax] // N_DEV for x, ax in zip(xs, axes)]
    shapes = [((N_DEV,) + tuple(d // (N_DEV if a == ax else 1) for a, d in enumerate(x.shape)), x.dtype)
              for x, ax in zip(xs, axes)]
    src = lambda t, ref, peer: _window(ref, axes[t], peer, widths[t])
    dst = lambda t, ref, who: ref.at[who]
    h = _exchange_start(xs, shapes, src, dst, "sg_start_" + tag)
    h['tag'] = "sg_wait_" + tag
    return h


def _finish(h, after):
    return _exchange_wait(h, after, h['tag'])


class _LayerWeights:
    def __init__(self, groups, started):
        self.groups = groups
        self.started = started
        self.got = {}
        self.after = None

    def __getitem__(self, nm):
        if nm not in self.got:
            for names, h in self.groups:
                if nm in names:
                    self.got.update(zip(names, _finish(h, list(self.after) + self.started)))
        return self.got[nm]


def _adam(parts, w, m, v, name, tr=256):
    P, R, C = parts.shape
    tr = min(tr, R)
    assert R % tr == 0
    bc1 = 1.0 / (1.0 - ADAM_B1 ** ADAM_STEP)
    bc2 = 1.0 / (1.0 - ADAM_B2 ** ADAM_STEP)

    def body(p_ref, w_ref, m_ref, v_ref, g_out, d_out, m_out, v_out):
        g = p_ref[0].astype(F32)
        for s in range(1, P):
            g = g + p_ref[s].astype(F32)
        mm = ADAM_B1 * m_ref[...] + (1.0 - ADAM_B1) * g
        vv = ADAM_B2 * v_ref[...] + (1.0 - ADAM_B2) * jnp.square(g)
        g_out[...] = g
        m_out[...] = mm
        v_out[...] = vv
        d_out[...] = -ADAM_LR * ((mm * bc1) / (jnp.sqrt(vv * bc2) + ADAM_EPS) + ADAM_WD * w_ref[...])

    spec = pl.BlockSpec((tr, C), lambda i: (i, 0))
    return pl.pallas_call(
        body, name=name, grid=(R // tr,),
        in_specs=[pl.BlockSpec((P, tr, C), lambda i: (0, i, 0)), spec, spec, spec],
        out_specs=[spec] * 4, out_shape=[jax.ShapeDtypeStruct((R, C), F32)] * 4,
        compiler_params=_cparams(("parallel",)),
    )(parts, w, m, v)


def _mod_partial(cact_all, w_ada_l, b_ada_l):
    def body(c_ref, w_ref, b_ref, o_ref):
        o_ref[...] = _dot(c_ref[...].astype(BF16), w_ref[...].astype(BF16)) + b_ref[...]

    return pl.pallas_call(
        body, name="mod_partial", out_shape=jax.ShapeDtypeStruct((cact_all.shape[0], w_ada_l.shape[1]), F32),
        compiler_params=pltpu.CompilerParams(vmem_limit_bytes=VMEM_LIMIT),
    )(cact_all, w_ada_l, b_ada_l)


def _ada_grad(cact_t, dmod):
    D, n = cact_t.shape[0], dmod.shape[1]

    def body(c_ref, d_ref, o_ref):
        ct = c_ref[...].astype(BF16).astype(F32)
        dm = d_ref[...].astype(BF16).astype(F32)
        acc = ct[:, 0:1] * dm[0:1, :]
        for b in range(1, N_DEV):
            acc = acc + ct[:, b:b + 1] * dm[b:b + 1, :]
        o_ref[0] = acc

    return pl.pallas_call(
        body, name="ada_grad", out_shape=jax.ShapeDtypeStruct((1, D, n), F32),
        compiler_params=pltpu.CompilerParams(vmem_limit_bytes=VMEM_LIMIT),
    )(cact_t, dmod)


def _norm_mod(x, r, gv, sh):
    return x * r * gv + sh


def _silu(x):
    return x * _sigmoid(x)


def _layer_fwd(x0, mod, gn1, gn2, W, rc, tri):
    S = x0.shape[0]
    sh1, sc1, g1m, sh2, sc2, g2m = [mod[i:i + 1] for i in range(N_MOD)]
    gv1 = gn1 * (1.0 + sc1)
    gv2 = gn2 * (1.0 + sc2)
    (r1,) = _ew(lambda x: (lax.rsqrt(jnp.mean(x * x, axis=-1, keepdims=True) + EPS),), [x0], outs=[('col', F32)],
                name="row_rstd")
    W.after = [r1, gv1]
    (p,) = _mm(x0, W["w_in"], a_ex=[(r1, 'm'), (gv1, 'k'), (sh1, 'k')], pro=_norm_mod, outs=(BF16,), name="mm_in")
    yret, rstate = _ret_fwd(p, rc)
    ysb, sbc = _sb_fwd(p, tri)
    W.after = [ysb]
    (ya,) = _mm(yret, W["w_ret_out"], tm=512, a_ex=[(p, 'a', O_RG)],
                pro=lambda yr, g: _silu(g.astype(F32)) * yr.astype(F32), outs=(BF16,), name="mm_ret_out")
    yb, mg = _mm(ysb, W["w_sb_out"], o_ex=[(ya, 'o'), (p, 'o', O_GA), (p, 'o', O_GB)],
                 epi=lambda acc, a, ga, gb: (acc, _sigmoid(ga.astype(F32)) * a.astype(F32)
                                             + _sigmoid(gb.astype(F32)) * acc),
                 outs=(BF16, BF16), name="mm_sb_out")
    mo, x1 = _mm(mg, W["w_mix_out"], o_ex=[(x0, 'o'), (g1m, 'n')],
                 epi=lambda acc, x, g: (acc, x + g * acc), outs=(BF16, F32), name="mm_mix_out")
    (r2,) = _ew(lambda x: (lax.rsqrt(jnp.mean(x * x, axis=-1, keepdims=True) + EPS),), [x1], outs=[('col', F32)],
                name="row_rstd")
    (act,) = _mm(x1, W["w_up"], a_ex=[(r2, 'm'), (gv2, 'k'), (sh2, 'k')], pro=_norm_mod,
                 epi=lambda acc: (jnp.maximum(acc, 0.0),), outs=(BF16,), name="mm_up")
    dn, x2 = _mm(act, W["w_down"], tm=512, pro=lambda a: a * a,
                 o_ex=[(x1, 'o'), (g2m, 'n')], epi=lambda acc, x, g: (acc, x + g * acc), outs=(BF16, F32),
                 name="mm_down")
    saved = dict(x0=x0, r1=r1, p=p, yret=yret, rstate=rstate, ysb=ysb, sbc=sbc, ya=ya, yb=yb, mg=mg, mo=mo, x1=x1, r2=r2,
                 act=act, dn=dn, gv1=gv1, gv2=gv2, mod=mod, gn1=gn1, gn2=gn2)
    return x2, saved


def _norm_bwd(dh, x, r, dres, gv, gn, extra_rows=(), extra_vecs=(), extra_fn=None, extra_outs=(), name="norm_bwd"):
    D = x.shape[1]
    ne = len(extra_rows)

    def fn(dh_t, x_t, dres_t, *rest):
        er, rest = rest[:ne], rest[ne:]
        gv_t = rest[0]
        ev, r_t = rest[1:-1], rest[-1]
        xh = x_t * r_t
        dxh = dh_t * gv_t
        dx = r_t * (dxh - xh * jnp.mean(dxh * xh, axis=-1, keepdims=True)) + dres_t
        base = (dx, dh_t, dh_t * xh)
        if extra_fn is None:
            return base
        return base + tuple(extra_fn(dx, *er, *ev))

    return _ew(fn, [dh, x, dres] + list(extra_rows), vecs=[gv] + list(extra_vecs), cols=[r],
               outs=[('row', D, F32), ('sum', D), ('sum', D)] + list(extra_outs), name=name)


def _layer_bwd(dx2, sv, W, rc, tri, emit):
    mod = sv['mod']
    sh1, sc1, g1m, sh2, sc2, g2m = [mod[i:i + 1] for i in range(N_MOD)]
    D = D_MODEL
    p = sv['p']
    d_g2m, d_dn = _ew(lambda dx, dn, g: (dx * dn.astype(F32), dx * g), [dx2, sv['dn']], vecs=[g2m],
                      outs=[('sum', D), ('row', D, BF16)], name="gate_bwd")
    (d_up,) = _mm(d_dn, W["w_down"], tb=True, o_ex=[(sv['act'], 'o')],
                  epi=lambda acc, a: (acc * 2.0 * a.astype(F32),), outs=(BF16,), name="mm_down_dx")
    (gw_down,) = _mm(sv['act'], d_dn, ta=True, tk=DW_TK, pro=lambda a: a * a, outs=(BF16,), name="mm_down_dw")
    (gw_up,) = _mm(sv['x1'], d_up, ta=True, tm=512, tk=DW_TK,
                   a_ex=[(sv['r2'], 'k'), (sv['gv2'].reshape(1, D), 'm'), (sh2, 'm')], pro=_norm_mod,
                   outs=(BF16,), name="mm_up_dw")
    tok = emit(dict(w_down=gw_down, w_up=gw_up), "mlp")
    (d_h2,) = _mm(d_up, W["w_up"], tb=True, tk=2048, outs=(F32,), name="mm_up_dx", deps=[tok])
    dx1, d_sh2, s_h2, d_g1m, d_mo = _norm_bwd(
        d_h2, sv['x1'], sv['r2'], dx2, sv['gv2'], sv['gn2'],
        extra_rows=[sv['mo']], extra_vecs=[g1m],
        extra_fn=lambda dx, mo, g: (dx * mo.astype(F32), dx * g),
        extra_outs=[('sum', D), ('row', D, BF16)], name="norm_bwd_mlp")
    d_sc2 = sv['gn2'] * s_h2
    d_gn2 = (1.0 + sc2) * s_h2
    def mix_epi(acc, ya, yb, ga, gb):
        sa, sb = _sigmoid(ga.astype(F32)), _sigmoid(gb.astype(F32))
        return (acc * sa, acc * sb, acc * ya.astype(F32) * sa * (1.0 - sa), acc * yb.astype(F32) * sb * (1.0 - sb))

    d_ya, d_yb, d_ga, d_gb = _mm(d_mo, W["w_mix_out"], tb=True, tm=512,
                                 o_ex=[(sv['ya'], 'o'), (sv['yb'], 'o'), (p, 'o', O_GA), (p, 'o', O_GB)], epi=mix_epi,
                                 outs=(BF16,) * 4, name="mm_mix_dx")
    (gw_mix,) = _mm(sv['mg'], d_mo, ta=True, tk=DW_TK, outs=(BF16,), name="mm_mix_dw")

    def ro_epi(acc, g, yr):
        gf = g.astype(F32)
        s = _sigmoid(gf)
        return (acc * yr.astype(F32) * s * (1.0 + gf * (1.0 - s)), acc * gf * s)

    d_rg, d_yret = _mm(d_ya, W["w_ret_out"], tb=True, tm=512, o_ex=[(p, 'o', O_RG), (sv['yret'], 'o')], epi=ro_epi,
                       outs=(BF16, BF16), name="mm_ret_dx")
    (gw_ro,) = _mm(sv['yret'], d_ya, ta=True, tm=512, tk=DW_TK, a_ex=[(p, 'a', O_RG)],
                   pro=lambda yr, g: _silu(g.astype(F32)) * yr.astype(F32), outs=(BF16,), name="mm_ret_dw")
    (gw_so,) = _mm(sv['ysb'], d_yb, ta=True, tk=DW_TK, outs=(BF16,), name="mm_sb_dw")
    tok = emit(dict(w_mix_out=gw_mix, w_ret_out=gw_ro, w_sb_out=gw_so), "mix")
    (d_ysb,) = _mm(d_yb, W["w_sb_out"], tb=True, outs=(BF16,), name="mm_sb_dx", deps=[tok])
    d_sq, d_sk, d_sv = _sb_bwd(p, sv['sbc'], d_ysb, tri)
    d_rq, d_rk, d_rv = _ret_bwd(p, sv['rstate'], d_yret, rc)
    dp = jnp.concatenate([d_rq, d_rk, d_rv, d_rg, d_sq, d_sk, d_sv, d_ga, d_gb], axis=1)
    (gw_in,) = _mm(sv['x0'], dp, ta=True, tm=512, tk=DW_TK,
                   a_ex=[(sv['r1'], 'k'), (sv['gv1'].reshape(1, D), 'm'), (sh1, 'm')], pro=_norm_mod,
                   outs=(BF16,), name="mm_in_dw")
    tok = emit(dict(w_in=gw_in), "in")
    (d_h,) = _mm(dp, W["w_in"], tb=True, tk=IN_W // 4, outs=(F32,), name="mm_in_dx", deps=[tok])
    dx0, d_sh1, s_h1 = _norm_bwd(d_h, sv['x0'], sv['r1'], dx1, sv['gv1'], sv['gn1'], name="norm_bwd_mix")
    d_sc1 = sv['gn1'] * s_h1
    d_gn1 = (1.0 + sc1) * s_h1
    d_mod = jnp.concatenate([d_sh1, d_sc1, d_g1m, d_sh2, d_sc2, d_g2m], axis=1)
    return dx0, d_mod, d_gn1, d_gn2


def kernel(x, c, norm_mix_g, w_in, w_ret_out, w_sb_out, w_mix_out, norm_mlp_g, w_up, w_down, w_ada, b_ada, final_g, loss_target, m_norm_mix_g, m_w_in, m_w_ret_out, m_w_sb_out, m_w_mix_out, m_norm_mlp_g, m_w_up, m_w_down, m_w_ada, m_b_ada, m_final_g, v_norm_mix_g, v_w_in, v_w_ret_out, v_w_sb_out, v_w_mix_out, v_norm_mlp_g, v_w_up, v_w_down, v_w_ada, v_b_ada, v_final_g):
    S, D = x.shape[1], x.shape[2]
    x0 = x.reshape(S, D)
    tgt = loss_target.reshape(S, D)
    me = 4 * lax.axis_index("x") + 2 * lax.axis_index("y") + lax.axis_index("c")
    wts = dict(w_in=w_in, w_ret_out=w_ret_out, w_sb_out=w_sb_out, w_mix_out=w_mix_out, w_up=w_up, w_down=w_down)
    mts = dict(w_in=m_w_in, w_ret_out=m_w_ret_out, w_sb_out=m_w_sb_out, w_mix_out=m_w_mix_out, w_up=m_w_up, w_down=m_w_down)
    vts = dict(w_in=v_w_in, w_ret_out=v_w_ret_out, w_sb_out=v_w_sb_out, w_mix_out=v_w_mix_out, w_up=v_w_up, w_down=v_w_down)
    rc = _ret_consts(S)
    tri = _tri()

    (cact,) = _ew(lambda t: (_silu(t),), [jnp.pad(c, ((0, 7), (0, 0)))], outs=[('row', D, F32)], name="silu_c")
    (cact_all,) = _all_gather_lead([cact[0:1]], "gather_c")
    cact_all = cact_all.reshape(N_DEV, D)
    cact16 = jnp.pad(cact_all, ((0, 8), (0, 0)))
    n_ada = w_ada.shape[2]
    b_loc = lax.dynamic_slice_in_dim(b_ada, me * n_ada, n_ada, axis=1)
    mods = [_mod_partial(cact16, w_ada[l], b_loc[l:l + 1])[:N_DEV] for l in range(DEPTH)]
    modp = jnp.stack(mods, axis=1)
    (modr,) = _all_to_all_lead([modp], "scatter_mod")
    mod_full = jnp.transpose(modr, (1, 0, 2)).reshape(DEPTH, N_MOD, D)

    shards = {}
    for nm in _W_NAMES:
        w = wts[nm]
        (wb,) = _ew(lambda t: (t,), [w.reshape(-1, w.shape[-1])], outs=[('row', w.shape[-1], BF16)], name="cast_bf16")
        shards[nm] = wb.reshape(w.shape)
    rest = tuple(nm for nm in _W_NAMES if nm != "w_in")
    started, layer_groups = [modr], []
    for l in range(DEPTH):
        sh_l = {nm: shards[nm][l] for nm in _W_NAMES}
        groups = [(("w_in",), "%d_in" % l), (rest, "%d_rest" % l)] if l == 0 else [(_W_NAMES, "%d_all" % l)]
        layer_groups.append([])
        for names, tag in groups:
            layer_groups[-1].append((names, _gather_start(sh_l, names, tag, started[-1:])))
            started.append(layer_groups[-1][-1][1]['token'])
    layer_w = [_LayerWeights(g, started) for g in layer_groups]

    xs = x0
    saved = []
    for l in range(DEPTH):
        xs, sv = _layer_fwd(xs, mod_full[l], norm_mix_g[l:l + 1], norm_mlp_g[l:l + 1], layer_w[l], rc, tri)
        sv['W'] = layer_w[l]
        saved.append(sv)

    fg = final_g.reshape(1, D)

    def head(xt, tg, g):
        r = lax.rsqrt(jnp.mean(xt * xt, axis=-1, keepdims=True) + EPS)
        xh = xt * r
        e = xh * g - tg
        dy = e * (1.0 / D)
        dxh = dy * g
        dx = r * (dxh - xh * jnp.mean(dxh * xh, axis=-1, keepdims=True))
        return dx, dy * xh, 0.5 * e * e * (1.0 / D)

    dxs, d_fg, loss_cols = _ew(head, [xs, tgt], vecs=[fg], outs=[('row', D, F32), ('sum', D), ('sum', D)], name="loss_head")

    small = [None] * DEPTH
    pending = []
    for l in reversed(range(DEPTH)):
        sv = saved[l]

        def emit(gw, tag, l=l):
            names = tuple(gw)
            pending.append((l, names, _scatter_start(gw, names, "%d_%s" % (l, tag))))
            return pending[-1][2]['token']

        dxs, d_mod, d_gn1, d_gn2 = _layer_bwd(dxs, sv, sv['W'], rc, tri, emit)
        small[l] = (d_mod, d_gn1, d_gn2)
    grad_x = dxs.reshape(1, S, D)

    pack = jnp.concatenate([small[l][0] for l in range(DEPTH)] + [small[l][1] for l in range(DEPTH)]
                           + [small[l][2] for l in range(DEPTH)] + [d_fg, loss_cols], axis=1)
    (packs,) = _all_gather_lead([pack], "gather_small")
    packs = packs.reshape(N_DEV, -1)
    o = 0
    dmod_all = []
    for l in range(DEPTH):
        dmod_all.append(packs[:, o:o + N_MOD * D]); o += N_MOD * D
    gn1_parts = packs[:, o:o + DEPTH * D].reshape(N_DEV, DEPTH, D); o += DEPTH * D
    gn2_parts = packs[:, o:o + DEPTH * D].reshape(N_DEV, DEPTH, D); o += DEPTH * D
    fg_parts = packs[:, o:o + D].reshape(N_DEV, 1, D); o += D
    loss_parts = packs[:, o:o + D]
    (loss_sum,) = _ew(lambda t: (t,), [loss_parts], outs=[('sum', D)], name="loss_sum")
    loss = jnp.sum(loss_sum)

    res = {}
    per = {nm: [None] * DEPTH for nm in _W_NAMES}
    after = [dxs, loss_sum]
    for l, names, h in pending:
        for nm, landed in zip(names, _finish(h, after)):
            per[nm][l] = _adam(landed, wts[nm][l], mts[nm][l], vts[nm][l], "adam")
        after = [per[names[-1]][l][0]]
    for nm in _W_NAMES:
        res[nm] = [jnp.stack([per[nm][l][i] for l in range(DEPTH)]) for i in range(4)]
    res["norm_mix_g"] = _adam(gn1_parts, norm_mix_g, m_norm_mix_g, v_norm_mix_g, "adam")
    res["norm_mlp_g"] = _adam(gn2_parts, norm_mlp_g, m_norm_mlp_g, v_norm_mlp_g, "adam")
    fgr = _adam(fg_parts, fg, m_final_g.reshape(1, D), v_final_g.reshape(1, D), "adam")
    res["final_g"] = [t.reshape(D) for t in fgr]
    bparts = jnp.stack(dmod_all, axis=1)
    res["b_ada"] = _adam(bparts, b_ada, m_b_ada, v_b_ada, "adam")
    cact_t = cact_all.T
    ada = []
    for l in range(DEPTH):
        dm_loc = lax.dynamic_slice_in_dim(dmod_all[l], me * n_ada, n_ada, axis=1)
        ada.append(_adam(_ada_grad(cact_t, dm_loc), w_ada[l], m_w_ada[l], v_w_ada[l], "adam"))
    res["w_ada"] = [jnp.stack([ada[l][i] for l in range(DEPTH)]) for i in range(4)]

    order = ['norm_mix_g', 'w_in', 'w_ret_out', 'w_sb_out', 'w_mix_out', 'norm_mlp_g', 'w_up', 'w_down', 'w_ada', 'b_ada', 'final_g']
    out = [loss, grad_x]
    for i in range(4):
        out += [res[nm][i] for nm in order]
    return tuple(out)
```

```python
import functools
import math

import jax
import jax.numpy as jnp
import numpy as np
from jax import lax
from jax.experimental import pallas as pl
from jax.experimental.pallas import tpu as pltpu

F32 = jnp.float32
BF16 = jnp.bfloat16

N_DEV = 8
D_MODEL = 1024
DEPTH = 2
RET_HEADS = 4
RET_QK = 256
RET_V = 512
RET_CHUNK = 128
ROPE_BASE = 10000.0
SB_HEADS = 16
SB_DIM = 64
D_FF = 4096
N_MOD = 6
EPS = 1e-6
GN_EPS = 1e-5
O_RQ, O_RK, O_RV, O_RG, O_SQ, O_SK, O_SV, O_GA, O_GB = 0, 1024, 2048, 4096, 6144, 7168, 8192, 9216, 10240
IN_W = 11264

ADAM_LR, ADAM_B1, ADAM_B2, ADAM_EPS, ADAM_WD, ADAM_STEP = 0.001, 0.9, 0.999, 1e-08, 0.01, 10

VMEM_LIMIT = 56 * 1024 * 1024
DW_TK = 2048


def _cparams(sem):
    return pltpu.CompilerParams(dimension_semantics=sem, vmem_limit_bytes=VMEM_LIMIT)


def _mm(a, b, *, ta=False, tb=False, tm=1024, tn=1024, tk=None, a_ex=(), pro=None, o_ex=(), epi=None,
        outs=(F32,), name, deps=()):
    if ta:
        K, M = a.shape
    else:
        M, K = a.shape
    N = b.shape[0] if tb else b.shape[1]
    tm, tn, tk = min(tm, M), min(tn, N), K if tk is None else min(tk, K)
    assert M % tm == 0 and N % tn == 0 and K % tk == 0, (name, M, N, K, tm, tn, tk)
    nk = K // tk
    in_specs = [
        pl.BlockSpec((tk, tm), lambda i, j, k: (k, i)) if ta else pl.BlockSpec((tm, tk), lambda i, j, k: (i, k)),
        pl.BlockSpec((tn, tk), lambda i, j, k: (j, k)) if tb else pl.BlockSpec((tk, tn), lambda i, j, k: (k, j)),
    ]
    args = [a, b]
    for arr, kind, *off in a_ex:
        off = off[0] if off else 0
        if kind == 'a' and ta:
            assert off % tm == 0
            in_specs.append(pl.BlockSpec((tk, tm), lambda i, j, k, o=off // tm: (k, o + i)))
        elif kind == 'a':
            assert off % tk == 0
            in_specs.append(pl.BlockSpec((tm, tk), lambda i, j, k, o=off // tk: (i, o + k)))
        elif kind == 'k':
            in_specs.append(pl.BlockSpec((tk, 1), lambda i, j, k: (k, 0)) if ta
                            else pl.BlockSpec((1, tk), lambda i, j, k: (0, k)))
        else:
            in_specs.append(pl.BlockSpec((1, tm), lambda i, j, k: (0, i)) if ta
                            else pl.BlockSpec((tm, 1), lambda i, j, k: (i, 0)))
        args.append(arr)
    for arr, kind, *off in o_ex:
        off = off[0] if off else 0
        if kind == 'o':
            assert off % tn == 0
            in_specs.append(pl.BlockSpec((tm, tn), lambda i, j, k, o=off // tn: (i, o + j)))
        elif kind == 'n':
            in_specs.append(pl.BlockSpec((1, tn), lambda i, j, k: (0, j)))
        else:
            in_specs.append(pl.BlockSpec((tm, 1), lambda i, j, k: (i, 0)))
        args.append(arr)
    for arr in deps:
        in_specs.append(pl.BlockSpec(memory_space=pl.ANY))
        args.append(arr)
    na, no, nout, nd = len(a_ex), len(o_ex), len(outs), len(deps)
    dims = (((0 if ta else 1,), (1 if tb else 0,)), ((), ()))

    def body(*refs):
        a_ref, b_ref = refs[0], refs[1]
        aex = refs[2:2 + na]
        oex = refs[2 + na:2 + na + no]
        out_refs = refs[2 + na + no + nd:2 + na + no + nd + nout]

        def product():
            at = a_ref[...]
            if pro is not None:
                at = pro(at, *[r[...] for r in aex])
            return lax.dot_general(at.astype(BF16), b_ref[...].astype(BF16), dims, preferred_element_type=F32)

        def finish(res):
            vals = epi(res, *[r[...] for r in oex]) if epi is not None else (res,)
            for o_ref, v in zip(out_refs, vals):
                o_ref[...] = v.astype(o_ref.dtype)

        if nk == 1:
            finish(product())
            return
        acc = refs[-1]
        k = pl.program_id(2)

        @pl.when(k == 0)
        def _():
            acc[...] = product()

        @pl.when(k > 0)
        def _():
            acc[...] += product()

        @pl.when(k == nk - 1)
        def _():
            finish(acc[...])

    res = pl.pallas_call(
        body, name=name, grid=(M // tm, N // tn, nk), in_specs=in_specs,
        out_specs=[pl.BlockSpec((tm, tn), lambda i, j, k: (i, j)) for _ in outs],
        out_shape=[jax.ShapeDtypeStruct((M, N), dt) for dt in outs],
        scratch_shapes=[pltpu.VMEM((tm, tn), F32)] if nk > 1 else [],
        compiler_params=_cparams(("parallel", "parallel", "arbitrary")),
    )(*args)
    return res


def _ew(fn, rows, vecs=(), cols=(), outs=(), tr=256, name=None):
    S = rows[0].shape[0]
    tr = min(tr, S)
    assert S % tr == 0
    in_specs, args = [], []
    for r in rows:
        in_specs.append(pl.BlockSpec((tr, r.shape[1]), lambda i: (i, 0)))
        args.append(r)
    for v in vecs:
        in_specs.append(pl.BlockSpec((1, v.shape[1]), lambda i: (0, 0)))
        args.append(v)
    for c in cols:
        in_specs.append(pl.BlockSpec((tr, 1), lambda i: (i, 0)))
        args.append(c)
    out_specs, out_shape = [], []
    for o in outs:
        if o[0] == 'row':
            out_specs.append(pl.BlockSpec((tr, o[1]), lambda i: (i, 0)))
            out_shape.append(jax.ShapeDtypeStruct((S, o[1]), o[2]))
        elif o[0] == 'sum':
            out_specs.append(pl.BlockSpec((1, o[1]), lambda i: (0, 0)))
            out_shape.append(jax.ShapeDtypeStruct((1, o[1]), F32))
        else:
            out_specs.append(pl.BlockSpec((tr, 1), lambda i: (i, 0)))
            out_shape.append(jax.ShapeDtypeStruct((S, 1), o[1]))
    nin = len(args)

    def body(*refs):
        i = pl.program_id(0)
        vals = fn(*[r[...] for r in refs[:nin]])
        for o, o_ref, v in zip(outs, refs[nin:], vals):
            if o[0] == 'sum':
                @pl.when(i == 0)
                def _():
                    o_ref[...] = jnp.zeros_like(o_ref)
                o_ref[...] += jnp.sum(v.astype(F32), axis=0, keepdims=True)
            else:
                o_ref[...] = v.astype(o_ref.dtype)

    return pl.pallas_call(
        body, name=name, grid=(S // tr,), in_specs=in_specs, out_specs=out_specs, out_shape=out_shape,
        compiler_params=_cparams(("arbitrary",)),
    )(*args)


def _sigmoid(x):
    return 1.0 / (1.0 + jnp.exp(-x))


def _ret_consts(S):
    h = np.arange(RET_HEADS, dtype=np.float64)
    log_gamma = np.log1p(-np.power(2.0, -5.0 - h))
    idx = np.arange(RET_CHUNK, dtype=np.float64)
    rel = idx[:, None] - idx[None, :]
    decay = np.where(rel >= 0, np.exp(np.maximum(rel, 0.0) * log_gamma[:, None, None]), 0.0)
    xi = np.exp((idx + 1.0) * log_gamma[:, None])[:, :, None]
    zeta = np.exp((RET_CHUNK - 1.0 - idx) * log_gamma[:, None])[:, :, None]
    gamma_c = np.exp(RET_CHUNK * log_gamma)[:, None, None]
    half = RET_QK // 2
    inv_freq = np.power(ROPE_BASE, -np.arange(half, dtype=np.float64) / half).astype(np.float32)
    ang = np.arange(S, dtype=np.float32)[:, None] * inv_freq[None, :]
    f = lambda t: jnp.asarray(t, F32)
    return dict(decay=f(decay), xi=f(xi), zeta=f(zeta), gc=f(gamma_c), cos=f(np.cos(ang)), sin=f(np.sin(ang)))


def _rot(t, cos, sin):
    half = RET_QK // 2
    t1, t2 = t[:, :half], t[:, half:]
    return jnp.concatenate([t1 * cos - t2 * sin, t1 * sin + t2 * cos], axis=-1)


def _rot_inv(t, cos, sin):
    half = RET_QK // 2
    t1, t2 = t[:, :half], t[:, half:]
    return jnp.concatenate([t1 * cos + t2 * sin, t2 * cos - t1 * sin], axis=-1)


_NT = (((1,), (1,)), ((), ()))
_TN = (((0,), (0,)), ((), ()))


def _dot(a, b):
    return jnp.dot(a, b, preferred_element_type=F32)


def _dot_nt(a, b):
    return lax.dot_general(a, b, _NT, preferred_element_type=F32)


def _dot_tn(a, b):
    return lax.dot_general(a, b, _TN, preferred_element_type=F32)


_QW, _VW = RET_HEADS * RET_QK, RET_HEADS * RET_V
_HEADS = range(RET_HEADS)


def _ret_in_specs(C, rev, NC):
    n_of = (lambda n: NC - 1 - n) if rev else (lambda n: n)
    whole3 = lambda n: (0, 0, 0)
    return [
        pl.BlockSpec((C, _QW), lambda n: (n_of(n), O_RQ // _QW)),
        pl.BlockSpec((C, _QW), lambda n: (n_of(n), O_RK // _QW)),
        pl.BlockSpec((C, _VW), lambda n: (n_of(n), O_RV // _VW)),
        pl.BlockSpec((C, RET_QK // 2), lambda n: (n_of(n), 0)),
        pl.BlockSpec((C, RET_QK // 2), lambda n: (n_of(n), 0)),
        pl.BlockSpec((RET_HEADS, C, C), whole3),
        pl.BlockSpec((RET_HEADS, C, 1), whole3),
        pl.BlockSpec((RET_HEADS, C, 1), whole3),
        pl.BlockSpec((RET_HEADS, 1, 1), whole3),
    ]


def _qk_heads(q_ref, k_ref, cos, sin):
    qs, kfs = [], []
    for h in _HEADS:
        cols = slice(h * RET_QK, (h + 1) * RET_QK)
        qs.append(_rot(q_ref[:, cols].astype(F32), cos, sin).astype(BF16))
        kfs.append(_rot(k_ref[:, cols].astype(F32), cos, sin) * (RET_QK ** -0.5))
    return qs, kfs


def _ret_fwd(p, rc):
    S = p.shape[0]
    C = RET_CHUNK
    NC = S // C

    def body(q_ref, k_ref, v_ref, cos_ref, sin_ref, dec_ref, xi_ref, zeta_ref, gc_ref, y_ref, rs_ref, r_acc):
        n = pl.program_id(0)

        @pl.when(n == 0)
        def _():
            r_acc[...] = jnp.zeros_like(r_acc)

        cos, sin = cos_ref[...], sin_ref[...]
        qs, kfs = _qk_heads(q_ref, k_ref, cos, sin)
        vs = [v_ref[:, h * RET_V:(h + 1) * RET_V] for h in _HEADS]
        rbs = [r_acc[h].astype(BF16) for h in _HEADS]
        for h in _HEADS:
            rs_ref[h, 0] = rbs[h]
        ss = [(_dot_nt(qs[h], kfs[h].astype(BF16)) * dec_ref[h]).astype(BF16) for h in _HEADS]
        os = [_dot(ss[h], vs[h]) + _dot(qs[h], rbs[h]) * xi_ref[h] for h in _HEADS]
        for h in _HEADS:
            o = os[h]
            mu = jnp.mean(o, axis=-1, keepdims=True)
            var = jnp.mean(jnp.square(o - mu), axis=-1, keepdims=True)
            y_ref[:, h * RET_V:(h + 1) * RET_V] = ((o - mu) * lax.rsqrt(var + GN_EPS)).astype(y_ref.dtype)
        for h in _HEADS:
            kz = (kfs[h] * zeta_ref[h]).astype(BF16)
            r_acc[h] = r_acc[h] * gc_ref[h] + _dot_tn(kz, vs[h])

    return pl.pallas_call(
        body, name="ret_fwd", grid=(NC,), in_specs=_ret_in_specs(C, False, NC),
        out_specs=[pl.BlockSpec((C, _VW), lambda n: (n, 0)),
                   pl.BlockSpec((RET_HEADS, 1, RET_QK, RET_V), lambda n: (0, n, 0, 0))],
        out_shape=[jax.ShapeDtypeStruct((S, _VW), BF16),
                   jax.ShapeDtypeStruct((RET_HEADS, NC, RET_QK, RET_V), BF16)],
        scratch_shapes=[pltpu.VMEM((RET_HEADS, RET_QK, RET_V), F32)],
        compiler_params=_cparams(("arbitrary",)),
    )(p, p, p, rc['cos'], rc['sin'], rc['decay'], rc['xi'], rc['zeta'], rc['gc'])


def _ret_bwd(p, rstate, dy, rc):
    S = p.shape[0]
    C = RET_CHUNK
    NC = S // C

    def body(q_ref, k_ref, v_ref, cos_ref, sin_ref, dec_ref, xi_ref, zeta_ref, gc_ref, rs_ref, dy_ref,
             dq_ref, dk_ref, dv_ref, dr_acc):
        t = pl.program_id(0)

        @pl.when(t == 0)
        def _():
            dr_acc[...] = jnp.zeros_like(dr_acc)

        cos, sin = cos_ref[...], sin_ref[...]
        qs, kfs = _qk_heads(q_ref, k_ref, cos, sin)
        ks = [kf.astype(BF16) for kf in kfs]
        vs = [v_ref[:, h * RET_V:(h + 1) * RET_V] for h in _HEADS]
        rbs = [rs_ref[h, 0] for h in _HEADS]
        ss = [(_dot_nt(qs[h], ks[h]) * dec_ref[h]).astype(BF16) for h in _HEADS]
        os = [_dot(ss[h], vs[h]) + _dot(qs[h], rbs[h]) * xi_ref[h] for h in _HEADS]
        dobs, doxis = [], []
        for h in _HEADS:
            o = os[h]
            mu = jnp.mean(o, axis=-1, keepdims=True)
            var = jnp.mean(jnp.square(o - mu), axis=-1, keepdims=True)
            rstd = lax.rsqrt(var + GN_EPS)
            yh = (o - mu) * rstd
            dyf = dy_ref[:, h * RET_V:(h + 1) * RET_V].astype(F32)
            do = (dyf - jnp.mean(dyf, axis=-1, keepdims=True)
                  - yh * jnp.mean(dyf * yh, axis=-1, keepdims=True)) * rstd
            dobs.append(do.astype(BF16))
            doxis.append((do * xi_ref[h]).astype(BF16))
        drbs = [dr_acc[h].astype(BF16) for h in _HEADS]
        dss = [(_dot_nt(dobs[h], vs[h]) * dec_ref[h]).astype(BF16) for h in _HEADS]
        for h in _HEADS:
            dq = _dot(dss[h], ks[h]) + _dot_nt(doxis[h], rbs[h])
            dq_ref[:, h * RET_QK:(h + 1) * RET_QK] = _rot_inv(dq, cos, sin).astype(dq_ref.dtype)
        for h in _HEADS:
            dk = _dot_tn(dss[h], qs[h]) + _dot_nt(vs[h], drbs[h]) * zeta_ref[h]
            dk_ref[:, h * RET_QK:(h + 1) * RET_QK] = (_rot_inv(dk, cos, sin) * (RET_QK ** -0.5)).astype(dk_ref.dtype)
        for h in _HEADS:
            kz = (kfs[h] * zeta_ref[h]).astype(BF16)
            dv = _dot_tn(ss[h], dobs[h]) + _dot(kz, drbs[h])
            dv_ref[:, h * RET_V:(h + 1) * RET_V] = dv.astype(dv_ref.dtype)
        for h in _HEADS:
            dr_acc[h] = dr_acc[h] * gc_ref[h] + _dot_tn(qs[h], doxis[h])

    rn = lambda n: NC - 1 - n
    in_specs = _ret_in_specs(C, True, NC) + [
        pl.BlockSpec((RET_HEADS, 1, RET_QK, RET_V), lambda n: (0, rn(n), 0, 0)),
        pl.BlockSpec((C, _VW), lambda n: (rn(n), 0)),
    ]
    return pl.pallas_call(
        body, name="ret_bwd", grid=(NC,), in_specs=in_specs,
        out_specs=[pl.BlockSpec((C, _QW), lambda n: (rn(n), 0)),
                   pl.BlockSpec((C, _QW), lambda n: (rn(n), 0)),
                   pl.BlockSpec((C, _VW), lambda n: (rn(n), 0))],
        out_shape=[jax.ShapeDtypeStruct((S, _QW), BF16),
                   jax.ShapeDtypeStruct((S, _QW), BF16),
                   jax.ShapeDtypeStruct((S, _VW), BF16)],
        scratch_shapes=[pltpu.VMEM((RET_HEADS, RET_QK, RET_V), F32)],
        compiler_params=_cparams(("arbitrary",)),
    )(p, p, p, rc['cos'], rc['sin'], rc['decay'], rc['xi'], rc['zeta'], rc['gc'], rstate, dy)


SB_T = 256
SB_SCALE = SB_DIM ** -0.5


def _tri():
    j = np.arange(SB_T)
    after = (j[:, None] > j[None, :]).astype(np.float32)
    upto = (j[:, None] <= j[None, :]).astype(np.float32)
    return jnp.asarray(np.stack([after, upto]), BF16)


def _softplus_parts(z):
    neg_abs = lax.bitcast_convert_type(lax.bitcast_convert_type(z, jnp.uint32) | jnp.uint32(0x80000000), F32)
    e = jnp.exp(neg_abs)
    return jnp.maximum(z, 0.0) + jnp.log(1.0 + e), e


def _sb_fwd(p, tri):
    S = p.shape[0]
    T = min(SB_T, S)
    NQ = S // T
    assert NQ <= 128
    qb, kb, vb = O_SQ // 128, O_SK // 128, O_SV // 128

    def body(q_ref, k_ref, v_ref, tri_ref, o_ref, cs_ref, o_acc, run, zbuf, abuf):
        i = pl.program_id(1)
        lane = lax.broadcasted_iota(jnp.int32, (1, 128), 1)
        tri_after = tri_ref[0]
        qs = [jnp.where((lane >= 64) if hh else (lane < 64), q_ref[...], jnp.zeros_like(q_ref[...]))
              * jnp.asarray(SB_SCALE, BF16) for hh in range(2)]
        cs_ref[...] = jnp.zeros_like(cs_ref)
        o_acc[...] = jnp.zeros_like(o_acc)
        run[...] = jnp.zeros_like(run)

        def kv(ref, j):
            return ref[pl.ds(pl.multiple_of(j * T, T), T), :]

        for hh in range(2):
            zbuf[hh] = _dot_nt(qs[hh], kv(k_ref, i))

        def block(t, diagonal):
            j = i - t
            if diagonal:
                msk = lax.broadcasted_iota(jnp.int32, (T, T), 1) < lax.broadcasted_iota(jnp.int32, (T, T), 0)
            if not diagonal:
                av = [_dot(abuf[hh], kv(v_ref, j + 1)) for hh in range(2)]
            lss, exs, tot, zn = [], [], [], []
            for hh in range(2):
                z = zbuf[hh]
                sp, _ = _softplus_parts(z)
                lss.append(z - sp)
                if diagonal:
                    sp = jnp.where(msk, sp, 0.0)
                exs.append(_dot(sp.astype(BF16), tri_after))
                tot.append(sp[:, 0:1])
                zn.append(_dot_nt(qs[hh], kv(k_ref, jnp.maximum(j - 1, 0))))
            for hh in range(2):
                csl = slice(hh * 128, (hh + 1) * 128)
                cs = run[hh]
                a = jnp.exp(lss[hh] - exs[hh] - cs)
                if diagonal:
                    a = jnp.where(msk, a, 0.0)
                abuf[hh] = a.astype(BF16)
                cs_ref[:, csl] = jnp.where(lane == j, cs, cs_ref[:, csl])
                run[hh] = cs + exs[hh][:, 0:1] + tot[hh]
            for hh in range(2):
                if not diagonal:
                    o_acc[hh] += av[hh]
                zbuf[hh] = zn[hh]

        block(0, True)

        def step(t, carry):
            block(t, False)
            return carry

        lax.fori_loop(1, i + 1, step, 0)
        o_ref[...] = jnp.where(lane < 64, o_acc[0] + _dot(abuf[0], kv(v_ref, 0)),
                               o_acc[1] + _dot(abuf[1], kv(v_ref, 0))).astype(o_ref.dtype)

    return pl.pallas_call(
        body, name="sb_fwd", grid=(SB_HEADS // 2, NQ),
        scratch_shapes=[pltpu.VMEM((2, T, 128), F32), pltpu.VMEM((2, T, 1), F32), pltpu.VMEM((2, T, T), F32),
                        pltpu.VMEM((2, T, T), BF16)],
        in_specs=[pl.BlockSpec((T, 128), lambda h, i: (i, qb + h)),
                  pl.BlockSpec((S, 128), lambda h, i: (0, kb + h)),
                  pl.BlockSpec((S, 128), lambda h, i: (0, vb + h)),
                  pl.BlockSpec((1, T, T), lambda h, i: (0, 0, 0))],
        out_specs=[pl.BlockSpec((T, 128), lambda h, i: (i, h)),
                   pl.BlockSpec((T, 256), lambda h, i: (i, h))],
        out_shape=[jax.ShapeDtypeStruct((S, SB_HEADS * SB_DIM), BF16),
                   jax.ShapeDtypeStruct((S, SB_HEADS * 128), F32)],
        compiler_params=_cparams(("parallel", "arbitrary")),
    )(p, p, p, tri)


def _sb_bwd(p, carries, dy, tri):
    S = p.shape[0]
    T = min(SB_T, S)
    NQ = S // T
    qb, kb, vb = O_SQ // 128, O_SK // 128, O_SV // 128

    def body(q_ref, k_ref, v_ref, cs_ref, dy_ref, tri_ref, dq_ref, dk_ref, dv_ref, dk_acc, dv_acc, dq_acc, run,
             zbuf, dabuf, dzbuf, abuf):
        i = pl.program_id(1)

        @pl.when(i == 0)
        def _():
            dk_acc[...] = jnp.zeros_like(dk_acc)
            dv_acc[...] = jnp.zeros_like(dv_acc)

        lane = lax.broadcasted_iota(jnp.int32, (1, 128), 1)
        tri_after, tri_upto = tri_ref[0], tri_ref[1]
        hms = [(lane >= 64) if hh else (lane < 64) for hh in range(2)]
        qs = [jnp.where(hm, q_ref[...], jnp.zeros_like(q_ref[...])) * jnp.asarray(SB_SCALE, BF16) for hm in hms]
        dos = [jnp.where(hm, dy_ref[...], jnp.zeros_like(dy_ref[...])) for hm in hms]
        qst = [t.T for t in qs]
        dost = [t.T for t in dos]
        dq_acc[...] = jnp.zeros_like(dq_acc)
        run[...] = jnp.zeros_like(run)
        dzbuf[...] = jnp.zeros_like(dzbuf)
        abuf[...] = jnp.zeros_like(abuf)

        def kv(ref, j):
            return ref[pl.ds(pl.multiple_of(j * T, T), T), :]

        def flush(jp):
            kp = kv(k_ref, jp)
            dq_add = [_dot(dzbuf[hh], kp) for hh in range(2)]
            dk_add = _dot(qst[0], dzbuf[0]) + _dot(qst[1], dzbuf[1])
            dv_add = _dot(dost[0], abuf[0]) + _dot(dost[1], abuf[1])
            return dq_add, dk_add, dv_add

        def apply(jp, adds):
            dq_add, dk_add, dv_add = adds
            cols = pl.ds(pl.multiple_of(jp * T, T), T)
            for hh in range(2):
                dq_acc[hh] += dq_add[hh]
            dk_acc[:, cols] += dk_add
            dv_acc[:, cols] += dv_add

        for hh in range(2):
            zbuf[hh] = _dot_nt(qs[hh], kv(k_ref, 0))
            dabuf[hh] = _dot_nt(dos[hh], kv(v_ref, 0))

        def block(j, diagonal):
            jp = jnp.maximum(j - 1, 0)
            if diagonal:
                msk = lax.broadcasted_iota(jnp.int32, (T, T), 1) < lax.broadcasted_iota(jnp.int32, (T, T), 0)
            kp = kv(k_ref, jp)
            dq_add = [_dot(dzbuf[hh], kp) for hh in range(2)]
            sigs, lss, exs, zn, dan, dk_part, dv_part = [], [], [], [], [], [], []
            for hh in range(2):
                z = zbuf[hh]
                sp, e = _softplus_parts(z)
                r = 1.0 / (1.0 + e)
                sigs.append(jnp.where(z >= 0, r, e * r))
                lss.append(z - sp)
                if diagonal:
                    sp = jnp.where(msk, sp, 0.0)
                exs.append(_dot(sp.astype(BF16), tri_after))
                if not diagonal:
                    zn.append(_dot_nt(qs[hh], kv(k_ref, j + 1)))
                dk_part.append(_dot(qst[hh], dzbuf[hh]))
            pgs, gs = [], []
            for hh in range(2):
                csl = slice(hh * 128, (hh + 1) * 128)
                cs = jnp.sum(jnp.where(lane == j, cs_ref[:, csl], 0.0), axis=-1, keepdims=True)
                a = jnp.exp(lss[hh] - exs[hh] - cs)
                if diagonal:
                    a = jnp.where(msk, a, 0.0)
                abuf_new = a.astype(BF16)
                g = a * dabuf[hh]
                gs.append((g, abuf_new))
                pgs.append(_dot(g.astype(BF16), tri_upto))
                if not diagonal:
                    dan.append(_dot_nt(dos[hh], kv(v_ref, j + 1)))
                dv_part.append(_dot(dost[hh], abuf[hh]))
            adds = (dq_add, dk_part[0] + dk_part[1], dv_part[0] + dv_part[1])
            for hh in range(2):
                g, abuf_new = gs[hh]
                cg = run[hh]
                dz = g - sigs[hh] * (cg + pgs[hh])
                if diagonal:
                    dz = jnp.where(msk, dz, 0.0)
                run[hh] = cg + pgs[hh][:, T - 1:T]
                dzbuf[hh] = dz.astype(BF16)
                abuf[hh] = abuf_new
            apply(jp, adds)
            if not diagonal:
                for hh in range(2):
                    zbuf[hh] = zn[hh]
                    dabuf[hh] = dan[hh]

        def step(j, carry):
            block(j, False)
            return carry

        lax.fori_loop(0, i, step, 0)
        block(i, True)
        apply(i, flush(i))
        dq_ref[...] = (jnp.where(lane < 64, dq_acc[0], dq_acc[1]) * SB_SCALE).astype(dq_ref.dtype)

        @pl.when(i == NQ - 1)
        def _():
            dk_ref[...] = dk_acc[...].T.astype(dk_ref.dtype)
            dv_ref[...] = dv_acc[...].T.astype(dv_ref.dtype)

    W = SB_HEADS * SB_DIM
    return pl.pallas_call(
        body, name="sb_bwd", grid=(SB_HEADS // 2, NQ),
        in_specs=[pl.BlockSpec((T, 128), lambda h, i: (i, qb + h)),
                  pl.BlockSpec((S, 128), lambda h, i: (0, kb + h)),
                  pl.BlockSpec((S, 128), lambda h, i: (0, vb + h)),
                  pl.BlockSpec((T, 256), lambda h, i: (i, h)),
                  pl.BlockSpec((T, 128), lambda h, i: (i, h)),
                  pl.BlockSpec((2, T, T), lambda h, i: (0, 0, 0))],
        out_specs=[pl.BlockSpec((T, 128), lambda h, i: (i, h)),
                   pl.BlockSpec((S, 128), lambda h, i: (0, h)),
                   pl.BlockSpec((S, 128), lambda h, i: (0, h))],
        out_shape=[jax.ShapeDtypeStruct((S, W), BF16)] * 3,
        scratch_shapes=[pltpu.VMEM((128, S), F32), pltpu.VMEM((128, S), F32), pltpu.VMEM((2, T, 128), F32),
                        pltpu.VMEM((2, T, 1), F32), pltpu.VMEM((2, T, T), F32), pltpu.VMEM((2, T, T), F32),
                        pltpu.VMEM((2, T, T), BF16), pltpu.VMEM((2, T, T), BF16)],
        compiler_params=_cparams(("parallel", "arbitrary")),
    )(p, p, p, carries, dy, tri)


def _exchange(srcs, out_shapes, src_slice, dst_slice, name):
    n = len(srcs)

    def body(*refs):
        ins, outs = refs[:n], refs[n:2 * n]
        send_sems, recv_sems, loc_sems = refs[2 * n:]
        x, y, c = lax.axis_index("x"), lax.axis_index("y"), lax.axis_index("c")
        me = 4 * x + 2 * y + c
        local = [pltpu.make_async_copy(src_slice(t, ins[t], me), dst_slice(t, outs[t], me), loc_sems.at[t])
                 for t in range(n)]
        for cp in local:
            cp.start()
        sends, recvs = [], []
        for k in (1, 2, 4, 6, 3, 5, 7):
            px = 1 - x if k & 4 else x
            py = 1 - y if k & 2 else y
            pc = 1 - c if k & 1 else c
            peer = 4 * px + 2 * py + pc
            for t in range(n):
                s = t * 7 + k - 1
                sends.append(pltpu.make_async_remote_copy(
                    src_ref=src_slice(t, ins[t], peer), dst_ref=dst_slice(t, outs[t], me),
                    send_sem=send_sems.at[s], recv_sem=recv_sems.at[s],
                    device_id=(px, py, pc), device_id_type=pl.DeviceIdType.MESH))
                recvs.append(pltpu.make_async_remote_copy(
                    src_ref=src_slice(t, ins[t], me), dst_ref=dst_slice(t, outs[t], peer),
                    send_sem=send_sems.at[s], recv_sem=recv_sems.at[s],
                    device_id=(px, py, pc), device_id_type=pl.DeviceIdType.MESH))
        for cp in sends:
            cp.start()
        for cp in recvs:
            cp.wait_recv()
        for cp in sends:
            cp.wait_send()
        for cp in local:
            cp.wait()

    anyspec = pl.BlockSpec(memory_space=pl.ANY)
    return pl.pallas_call(
        body, name=name, in_specs=[anyspec] * n, out_specs=[anyspec] * n,
        out_shape=[jax.ShapeDtypeStruct(s, d) for s, d in out_shapes],
        scratch_shapes=[pltpu.SemaphoreType.DMA((7 * n,)), pltpu.SemaphoreType.DMA((7 * n,)),
                        pltpu.SemaphoreType.DMA((n,))],
    )(*srcs)


def _all_gather_lead(xs, name):
    return _exchange(
        xs, [((N_DEV,) + x.shape, x.dtype) for x in xs],
        lambda t, ref, peer: ref, lambda t, ref, who: ref.at[who], name)


def _all_to_all_lead(xs, name):
    return _exchange(
        xs, [(x.shape, x.dtype) for x in xs],
        lambda t, ref, peer: ref.at[peer], lambda t, ref, who: ref.at[who], name)


_W_AXIS = {"w_in": 1, "w_ret_out": 0, "w_sb_out": 0, "w_mix_out": 0, "w_up": 1, "w_down": 0}
_W_NAMES = tuple(_W_AXIS)


def _window(ref, axis, who, width, count=1):
    start = pl.multiple_of(who * width, width)
    return ref.at[pl.ds(start, count * width), :] if axis == 0 else ref.at[:, pl.ds(start, count * width)]


_HBM = pl.BlockSpec(memory_space=pltpu.HBM)
_SEM = pl.BlockSpec(memory_space=pltpu.SEMAPHORE)
_EFFECT = pltpu.SideEffectType.DATAFLOW_SIDE_EFFECTING


def _exchange_start(srcs, shapes, src_slice, dst_slice, name, deps=()):
    n, nd = len(srcs), len(deps)
    lands = [pltpu.with_memory_space_constraint(lax.empty(s, d), pltpu.HBM) for s, d in shapes]

    def body(*refs):
        ins, lnd = refs[:n], refs[n:2 * n]
        sems = refs[2 * n + nd:4 * n + nd]
        token = refs[6 * n + nd]
        x, y, c = lax.axis_index("x"), lax.axis_index("y"), lax.axis_index("c")
        me = 4 * x + 2 * y + c
        for k in (0, 1, 2, 4, 6, 3, 5, 7):
            px = 1 - x if k & 4 else x
            py = 1 - y if k & 2 else y
            pc = 1 - c if k & 1 else c
            peer = 4 * px + 2 * py + pc
            for t in range(n):
                pltpu.make_async_remote_copy(
                    src_ref=src_slice(t, ins[t], peer), dst_ref=dst_slice(t, lnd[t], me),
                    send_sem=sems[2 * t], recv_sem=sems[2 * t + 1],
                    device_id=(px, py, pc), device_id_type=pl.DeviceIdType.MESH).start()
        token[...] = jnp.zeros_like(token)

    res = pl.pallas_call(
        body, name=name, in_specs=[_HBM] * (2 * n) + [pl.BlockSpec(memory_space=pl.ANY)] * nd,
        out_specs=[_SEM] * (2 * n) + [_HBM] * (2 * n) + [pl.BlockSpec(memory_space=pltpu.VMEM)],
        out_shape=[pltpu.SemaphoreType.DMA(())] * (2 * n) + [pltpu.HBM(s.shape, s.dtype) for s in srcs]
        + [pltpu.HBM(s.shape, s.dtype) for s in lands] + [jax.ShapeDtypeStruct((8, 128), F32)],
        input_output_aliases={t: 2 * n + t for t in range(2 * n)},
        compiler_params=pltpu.CompilerParams(has_side_effects=_EFFECT),
    )(*[pltpu.with_memory_space_constraint(s, pltpu.HBM) for s in srcs], *lands, *deps)
    return dict(n=n, sems=res[:2 * n], srcs=res[2 * n:3 * n], lands=res[3 * n:4 * n], token=res[4 * n])


def _exchange_wait(h, after, name):
    n = h['n']

    def body(*refs):
        lnd = refs[n:2 * n]
        sems = refs[2 * n:4 * n]
        x, y, c = lax.axis_index("x"), lax.axis_index("y"), lax.axis_index("c")
        for t in range(n):
            w = lnd[t]
            cp = pltpu.make_async_remote_copy(src_ref=w, dst_ref=w, send_sem=sems[2 * t], recv_sem=sems[2 * t + 1],
                                              device_id=(x, y, 1 - c), device_id_type=pl.DeviceIdType.MESH)
            cp.wait_send()
            cp.wait_recv()

    after = list(after)
    res = pl.pallas_call(
        body, name=name,
        in_specs=[_HBM] * (2 * n) + [_SEM] * (2 * n) + [pl.BlockSpec(memory_space=pl.ANY)] * len(after),
        out_specs=[_HBM] * (2 * n),
        out_shape=[pltpu.HBM(s.shape, s.dtype) for s in h['srcs']] + [pltpu.HBM(s.shape, s.dtype) for s in h['lands']],
        input_output_aliases={t: t for t in range(2 * n)},
        compiler_params=pltpu.CompilerParams(has_side_effects=_EFFECT),
    )(*h['srcs'], *h['lands'], *h['sems'], *after)
    return list(res[n:])


def _gather_start(shards, names, tag, deps=()):
    xs = [shards[nm] for nm in names]
    axes = [_W_AXIS[nm] for nm in names]
    widths = [x.shape[ax] for x, ax in zip(xs, axes)]
    shapes = [(tuple(d * (N_DEV if a == ax else 1) for a, d in enumerate(x.shape)), x.dtype) for x, ax in zip(xs, axes)]
    src = lambda t, ref, peer: ref
    dst = lambda t, ref, who: _window(ref, axes[t], who, widths[t])
    h = _exchange_start(xs, shapes, src, dst, "gw_start_" + tag, deps)
    h['tag'] = "gw_wait_" + tag
    return h


def _scatter_start(grads, names, tag):
    xs = [grads[nm] for nm in names]
    axes = [_W_AXIS[nm] for nm in names]
    widths = [x.shape[ax] // N_DEV for x, ax in zip(xs, axes)]
    shapes = [((N_DEV,) + tuple(d // (N_DEV if a == ax else 1) for a, d in enumerate(x.shape)), x.dtype)
              for x, ax in zip(xs, axes)]
    src = lambda t, ref, peer: _window(ref, axes[t], peer, widths[t])
    dst = lambda t, ref, who: ref.at[who]
    h = _exchange_start(xs, shapes, src, dst, "sg_start_" + tag)
    h['tag'] = "sg_wait_" + tag
    return h


def _finish(h, after):
    return _exchange_wait(h, after, h['tag'])


class _LayerWeights:
    def __init__(self, groups, started):
        self.groups = groups
        self.started = started
        self.got = {}
        self.after = None

    def __getitem__(self, nm):
        if nm not in self.got:
            for names, h in self.groups:
                if nm in names:
                    self.got.update(zip(names, _finish(h, list(self.after) + self.started)))
        return self.got[nm]


def _adam_math(p_ref, w, m, v):
    g = p_ref[0].astype(F32)
    for s in range(1, p_ref.shape[0]):
        g = g + p_ref[s].astype(F32)
    bc1 = 1.0 / (1.0 - ADAM_B1 ** ADAM_STEP)
    bc2 = 1.0 / (1.0 - ADAM_B2 ** ADAM_STEP)
    mm = ADAM_B1 * m + (1.0 - ADAM_B1) * g
    vv = ADAM_B2 * v + (1.0 - ADAM_B2) * jnp.square(g)
    return g, -ADAM_LR * ((mm * bc1) / (jnp.sqrt(vv * bc2) + ADAM_EPS) + ADAM_WD * w), mm, vv


def _adam(parts, w, m, v, name, tr=256):
    P, R, C = parts.shape
    tr = min(tr, R)
    assert R % tr == 0

    def body(p_ref, w_ref, m_ref, v_ref, *outs):
        for o_ref, val in zip(outs, _adam_math(p_ref, w_ref[...], m_ref[...], v_ref[...])):
            o_ref[...] = val

    spec = pl.BlockSpec((tr, C), lambda i: (i, 0))
    return pl.pallas_call(
        body, name=name, grid=(R // tr,),
        in_specs=[pl.BlockSpec((P, tr, C), lambda i: (0, i, 0)), spec, spec, spec],
        out_specs=[spec] * 4, out_shape=[jax.ShapeDtypeStruct((R, C), F32)] * 4,
        compiler_params=_cparams(("parallel",)),
    )(parts, w, m, v)


def _adam_layer(parts, w, m, v, l, prev, name, tr=256):
    P, R, C = parts.shape
    tr = min(tr, R)
    assert R % tr == 0 and w.shape == (DEPTH, R, C)
    npv = 0 if prev is None else 4

    def body(p_ref, w_ref, m_ref, v_ref, *rest):
        for o_ref, val in zip(rest[npv:], _adam_math(p_ref, w_ref[0], m_ref[0], v_ref[0])):
            o_ref[0] = val

    spec = pl.BlockSpec((1, tr, C), lambda i: (l, i, 0))
    return pl.pallas_call(
        body, name=name, grid=(R // tr,),
        in_specs=[pl.BlockSpec((P, tr, C), lambda i: (0, i, 0)), spec, spec, spec]
        + [pl.BlockSpec(memory_space=pl.ANY)] * npv,
        out_specs=[spec] * 4, out_shape=[jax.ShapeDtypeStruct((DEPTH, R, C), F32)] * 4,
        input_output_aliases={4 + t: t for t in range(npv)},
        compiler_params=_cparams(("parallel",)),
    )(parts, w, m, v, *([] if prev is None else prev))


def _mod_partial(cact_all, w_ada_l, b_ada_l):
    def body(c_ref, w_ref, b_ref, o_ref):
        o_ref[...] = _dot(c_ref[...].astype(BF16), w_ref[...].astype(BF16)) + b_ref[...]

    return pl.pallas_call(
        body, name="mod_partial", out_shape=jax.ShapeDtypeStruct((cact_all.shape[0], w_ada_l.shape[1]), F32),
        compiler_params=pltpu.CompilerParams(vmem_limit_bytes=VMEM_LIMIT),
    )(cact_all, w_ada_l, b_ada_l)


def _ada_grad(cact_t, dmod):
    D, n = cact_t.shape[0], dmod.shape[1]

    def body(c_ref, d_ref, o_ref):
        ct = c_ref[...].astype(BF16).astype(F32)
        dm = d_ref[...].astype(BF16).astype(F32)
        acc = ct[:, 0:1] * dm[0:1, :]
        for b in range(1, N_DEV):
            acc = acc + ct[:, b:b + 1] * dm[b:b + 1, :]
        o_ref[0] = acc

    return pl.pallas_call(
        body, name="ada_grad", out_shape=jax.ShapeDtypeStruct((1, D, n), F32),
        compiler_params=pltpu.CompilerParams(vmem_limit_bytes=VMEM_LIMIT),
    )(cact_t, dmod)


def _norm_mod(x, r, gv, sh):
    return x * r * gv + sh


def _silu(x):
    return x * _sigmoid(x)


def _layer_fwd(x0, mod, gn1, gn2, W, rc, tri):
    S = x0.shape[0]
    sh1, sc1, g1m, sh2, sc2, g2m = [mod[i:i + 1] for i in range(N_MOD)]
    gv1 = gn1 * (1.0 + sc1)
    gv2 = gn2 * (1.0 + sc2)
    (r1,) = _ew(lambda x: (lax.rsqrt(jnp.mean(x * x, axis=-1, keepdims=True) + EPS),), [x0], outs=[('col', F32)],
                name="row_rstd")
    W.after = [r1, gv1]
    (p,) = _mm(x0, W["w_in"], a_ex=[(r1, 'm'), (gv1, 'k'), (sh1, 'k')], pro=_norm_mod, outs=(BF16,), name="mm_in")
    yret, rstate = _ret_fwd(p, rc)
    ysb, sbc = _sb_fwd(p, tri)
    W.after = [ysb]
    (ya,) = _mm(yret, W["w_ret_out"], tm=512, a_ex=[(p, 'a', O_RG)],
                pro=lambda yr, g: _silu(g.astype(F32)) * yr.astype(F32), outs=(BF16,), name="mm_ret_out")
    yb, mg = _mm(ysb, W["w_sb_out"], o_ex=[(ya, 'o'), (p, 'o', O_GA), (p, 'o', O_GB)],
                 epi=lambda acc, a, ga, gb: (acc, _sigmoid(ga.astype(F32)) * a.astype(F32)
                                             + _sigmoid(gb.astype(F32)) * acc),
                 outs=(BF16, BF16), name="mm_sb_out")
    mo, x1 = _mm(mg, W["w_mix_out"], o_ex=[(x0, 'o'), (g1m, 'n')],
                 epi=lambda acc, x, g: (acc, x + g * acc), outs=(BF16, F32), name="mm_mix_out")
    (r2,) = _ew(lambda x: (lax.rsqrt(jnp.mean(x * x, axis=-1, keepdims=True) + EPS),), [x1], outs=[('col', F32)],
                name="row_rstd")
    (act,) = _mm(x1, W["w_up"], a_ex=[(r2, 'm'), (gv2, 'k'), (sh2, 'k')], pro=_norm_mod,
                 epi=lambda acc: (jnp.maximum(acc, 0.0),), outs=(BF16,), name="mm_up")
    dn, x2 = _mm(act, W["w_down"], tm=512, pro=lambda a: a * a,
                 o_ex=[(x1, 'o'), (g2m, 'n')], epi=lambda acc, x, g: (acc, x + g * acc), outs=(BF16, F32),
                 name="mm_down")
    saved = dict(x0=x0, r1=r1, p=p, yret=yret, rstate=rstate, ysb=ysb, sbc=sbc, ya=ya, yb=yb, mg=mg, mo=mo, x1=x1, r2=r2,
                 act=act, dn=dn, gv1=gv1, gv2=gv2, mod=mod, gn1=gn1, gn2=gn2)
    return x2, saved


def _norm_bwd(dh, x, r, dres, gv, gn, extra_rows=(), extra_vecs=(), extra_fn=None, extra_outs=(), name="norm_bwd"):
    D = x.shape[1]
    ne = len(extra_rows)

    def fn(dh_t, x_t, dres_t, *rest):
        er, rest = rest[:ne], rest[ne:]
        gv_t = rest[0]
        ev, r_t = rest[1:-1], rest[-1]
        xh = x_t * r_t
        dxh = dh_t * gv_t
        dx = r_t * (dxh - xh * jnp.mean(dxh * xh, axis=-1, keepdims=True)) + dres_t
        base = (dx, dh_t, dh_t * xh)
        if extra_fn is None:
            return base
        return base + tuple(extra_fn(dx, *er, *ev))

    return _ew(fn, [dh, x, dres] + list(extra_rows), vecs=[gv] + list(extra_vecs), cols=[r],
               outs=[('row', D, F32), ('sum', D), ('sum', D)] + list(extra_outs), name=name)


def _layer_bwd(dx2, sv, W, rc, tri, emit):
    mod = sv['mod']
    sh1, sc1, g1m, sh2, sc2, g2m = [mod[i:i + 1] for i in range(N_MOD)]
    D = D_MODEL
    p = sv['p']
    d_g2m, d_dn = _ew(lambda dx, dn, g: (dx * dn.astype(F32), dx * g), [dx2, sv['dn']], vecs=[g2m],
                      outs=[('sum', D), ('row', D, BF16)], name="gate_bwd")
    (d_up,) = _mm(d_dn, W["w_down"], tb=True, o_ex=[(sv['act'], 'o')],
                  epi=lambda acc, a: (acc * 2.0 * a.astype(F32),), outs=(BF16,), name="mm_down_dx")
    (gw_down,) = _mm(sv['act'], d_dn, ta=True, tk=DW_TK, pro=lambda a: a * a, outs=(BF16,), name="mm_down_dw")
    (gw_up,) = _mm(sv['x1'], d_up, ta=True, tm=512, tk=DW_TK,
                   a_ex=[(sv['r2'], 'k'), (sv['gv2'].reshape(1, D), 'm'), (sh2, 'm')], pro=_norm_mod,
                   outs=(BF16,), name="mm_up_dw")
    tok = emit(dict(w_down=gw_down, w_up=gw_up), "mlp")
    (d_h2,) = _mm(d_up, W["w_up"], tb=True, tk=2048, outs=(F32,), name="mm_up_dx", deps=[tok])
    dx1, d_sh2, s_h2, d_g1m, d_mo = _norm_bwd(
        d_h2, sv['x1'], sv['r2'], dx2, sv['gv2'], sv['gn2'],
        extra_rows=[sv['mo']], extra_vecs=[g1m],
        extra_fn=lambda dx, mo, g: (dx * mo.astype(F32), dx * g),
        extra_outs=[('sum', D), ('row', D, BF16)], name="norm_bwd_mlp")
    d_sc2 = sv['gn2'] * s_h2
    d_gn2 = (1.0 + sc2) * s_h2
    def mix_epi(acc, ya, yb, ga, gb):
        sa, sb = _sigmoid(ga.astype(F32)), _sigmoid(gb.astype(F32))
        return (acc * sa, acc * sb, acc * ya.astype(F32) * sa * (1.0 - sa), acc * yb.astype(F32) * sb * (1.0 - sb))

    d_ya, d_yb, d_ga, d_gb = _mm(d_mo, W["w_mix_out"], tb=True, tm=512,
                                 o_ex=[(sv['ya'], 'o'), (sv['yb'], 'o'), (p, 'o', O_GA), (p, 'o', O_GB)], epi=mix_epi,
                                 outs=(BF16,) * 4, name="mm_mix_dx")
    (gw_mix,) = _mm(sv['mg'], d_mo, ta=True, tk=DW_TK, outs=(BF16,), name="mm_mix_dw")

    def ro_epi(acc, g, yr):
        gf = g.astype(F32)
        s = _sigmoid(gf)
        return (acc * yr.astype(F32) * s * (1.0 + gf * (1.0 - s)), acc * gf * s)

    d_rg, d_yret = _mm(d_ya, W["w_ret_out"], tb=True, tm=512, o_ex=[(p, 'o', O_RG), (sv['yret'], 'o')], epi=ro_epi,
                       outs=(BF16, BF16), name="mm_ret_dx")
    (gw_ro,) = _mm(sv['yret'], d_ya, ta=True, tm=512, tk=DW_TK, a_ex=[(p, 'a', O_RG)],
                   pro=lambda yr, g: _silu(g.astype(F32)) * yr.astype(F32), outs=(BF16,), name="mm_ret_dw")
    (gw_so,) = _mm(sv['ysb'], d_yb, ta=True, tk=DW_TK, outs=(BF16,), name="mm_sb_dw")
    tok = emit(dict(w_mix_out=gw_mix, w_ret_out=gw_ro, w_sb_out=gw_so), "mix")
    (d_ysb,) = _mm(d_yb, W["w_sb_out"], tb=True, outs=(BF16,), name="mm_sb_dx", deps=[tok])
    d_sq, d_sk, d_sv = _sb_bwd(p, sv['sbc'], d_ysb, tri)
    d_rq, d_rk, d_rv = _ret_bwd(p, sv['rstate'], d_yret, rc)
    dp = jnp.concatenate([d_rq, d_rk, d_rv, d_rg, d_sq, d_sk, d_sv, d_ga, d_gb], axis=1)
    (gw_in,) = _mm(sv['x0'], dp, ta=True, tm=512, tk=DW_TK,
                   a_ex=[(sv['r1'], 'k'), (sv['gv1'].reshape(1, D), 'm'), (sh1, 'm')], pro=_norm_mod,
                   outs=(BF16,), name="mm_in_dw")
    tok = emit(dict(w_in=gw_in), "in")
    (d_h,) = _mm(dp, W["w_in"], tb=True, tk=IN_W // 4, outs=(F32,), name="mm_in_dx", deps=[tok])
    dx0, d_sh1, s_h1 = _norm_bwd(d_h, sv['x0'], sv['r1'], dx1, sv['gv1'], sv['gn1'], name="norm_bwd_mix")
    d_sc1 = sv['gn1'] * s_h1
    d_gn1 = (1.0 + sc1) * s_h1
    d_mod = jnp.concatenate([d_sh1, d_sc1, d_g1m, d_sh2, d_sc2, d_g2m], axis=1)
    return dx0, d_mod, d_gn1, d_gn2


def kernel(x, c, norm_mix_g, w_in, w_ret_out, w_sb_out, w_mix_out, norm_mlp_g, w_up, w_down, w_ada, b_ada, final_g, loss_target, m_norm_mix_g, m_w_in, m_w_ret_out, m_w_sb_out, m_w_mix_out, m_norm_mlp_g, m_w_up, m_w_down, m_w_ada, m_b_ada, m_final_g, v_norm_mix_g, v_w_in, v_w_ret_out, v_w_sb_out, v_w_mix_out, v_norm_mlp_g, v_w_up, v_w_down, v_w_ada, v_b_ada, v_final_g):
    S, D = x.shape[1], x.shape[2]
    x0 = x.reshape(S, D)
    tgt = loss_target.reshape(S, D)
    me = 4 * lax.axis_index("x") + 2 * lax.axis_index("y") + lax.axis_index("c")
    wts = dict(w_in=w_in, w_ret_out=w_ret_out, w_sb_out=w_sb_out, w_mix_out=w_mix_out, w_up=w_up, w_down=w_down)
    mts = dict(w_in=m_w_in, w_ret_out=m_w_ret_out, w_sb_out=m_w_sb_out, w_mix_out=m_w_mix_out, w_up=m_w_up, w_down=m_w_down)
    vts = dict(w_in=v_w_in, w_ret_out=v_w_ret_out, w_sb_out=v_w_sb_out, w_mix_out=v_w_mix_out, w_up=v_w_up, w_down=v_w_down)
    rc = _ret_consts(S)
    tri = _tri()

    (cact,) = _ew(lambda t: (_silu(t),), [jnp.pad(c, ((0, 7), (0, 0)))], outs=[('row', D, F32)], name="silu_c")
    (cact_all,) = _all_gather_lead([cact[0:1]], "gather_c")
    cact_all = cact_all.reshape(N_DEV, D)
    cact16 = jnp.pad(cact_all, ((0, 8), (0, 0)))
    n_ada = w_ada.shape[2]
    b_loc = lax.dynamic_slice_in_dim(b_ada, me * n_ada, n_ada, axis=1)
    mods = [_mod_partial(cact16, w_ada[l], b_loc[l:l + 1])[:N_DEV] for l in range(DEPTH)]
    modp = jnp.stack(mods, axis=1)
    (modr,) = _all_to_all_lead([modp], "scatter_mod")
    mod_full = jnp.transpose(modr, (1, 0, 2)).reshape(DEPTH, N_MOD, D)

    shards = {}
    for nm in _W_NAMES:
        w = wts[nm]
        (wb,) = _ew(lambda t: (t,), [w.reshape(-1, w.shape[-1])], outs=[('row', w.shape[-1], BF16)], name="cast_bf16")
        shards[nm] = wb.reshape(w.shape)
    rest = tuple(nm for nm in _W_NAMES if nm != "w_in")
    started, layer_groups = [modr], []
    for l in range(DEPTH):
        sh_l = {nm: shards[nm][l] for nm in _W_NAMES}
        groups = [(("w_in",), "%d_in" % l), (rest, "%d_rest" % l)] if l == 0 else [(_W_NAMES, "%d_all" % l)]
        layer_groups.append([])
        for names, tag in groups:
            layer_groups[-1].append((names, _gather_start(sh_l, names, tag, started[-1:])))
            started.append(layer_groups[-1][-1][1]['token'])
    layer_w = [_LayerWeights(g, started) for g in layer_groups]

    xs = x0
    saved = []
    for l in range(DEPTH):
        xs, sv = _layer_fwd(xs, mod_full[l], norm_mix_g[l:l + 1], norm_mlp_g[l:l + 1], layer_w[l], rc, tri)
        sv['W'] = layer_w[l]
        saved.append(sv)

    fg = final_g.reshape(1, D)

    def head(xt, tg, g):
        r = lax.rsqrt(jnp.mean(xt * xt, axis=-1, keepdims=True) + EPS)
        xh = xt * r
        e = xh * g - tg
        dy = e * (1.0 / D)
        dxh = dy * g
        dx = r * (dxh - xh * jnp.mean(dxh * xh, axis=-1, keepdims=True))
        return dx, dy * xh, 0.5 * e * e * (1.0 / D)

    dxs, d_fg, loss_cols = _ew(head, [xs, tgt], vecs=[fg], outs=[('row', D, F32), ('sum', D), ('sum', D)], name="loss_head")

    small = [None] * DEPTH
    pending = []
    for l in reversed(range(DEPTH)):
        sv = saved[l]

        def emit(gw, tag, l=l):
            names = tuple(gw)
            pending.append((l, names, _scatter_start(gw, names, "%d_%s" % (l, tag))))
            return pending[-1][2]['token']

        dxs, d_mod, d_gn1, d_gn2 = _layer_bwd(dxs, sv, sv['W'], rc, tri, emit)
        small[l] = (d_mod, d_gn1, d_gn2)
    grad_x = dxs.reshape(1, S, D)

    pack = jnp.concatenate([small[l][0] for l in range(DEPTH)] + [small[l][1] for l in range(DEPTH)]
                           + [small[l][2] for l in range(DEPTH)] + [d_fg, loss_cols], axis=1)
    (packs,) = _all_gather_lead([pack], "gather_small")
    packs = packs.reshape(N_DEV, -1)
    o = 0
    dmod_all = []
    for l in range(DEPTH):
        dmod_all.append(packs[:, o:o + N_MOD * D]); o += N_MOD * D
    gn1_parts = packs[:, o:o + DEPTH * D].reshape(N_DEV, DEPTH, D); o += DEPTH * D
    gn2_parts = packs[:, o:o + DEPTH * D].reshape(N_DEV, DEPTH, D); o += DEPTH * D
    fg_parts = packs[:, o:o + D].reshape(N_DEV, 1, D); o += D
    loss_parts = packs[:, o:o + D]
    (loss_sum,) = _ew(lambda t: (t,), [loss_parts], outs=[('sum', D)], name="loss_sum")
    loss = jnp.sum(loss_sum)

    res = {}
    after = [dxs, loss_sum]
    for l, names, h in pending:
        for nm, landed in zip(names, _finish(h, after)):
            res[nm] = _adam_layer(landed, wts[nm], mts[nm], vts[nm], l, res.get(nm), "adam_layer")
        after = [res[names[-1]][0]]
    res["norm_mix_g"] = _adam(gn1_parts, norm_mix_g, m_norm_mix_g, v_norm_mix_g, "adam")
    res["norm_mlp_g"] = _adam(gn2_parts, norm_mlp_g, m_norm_mlp_g, v_norm_mlp_g, "adam")
    fgr = _adam(fg_parts, fg, m_final_g.reshape(1, D), v_final_g.reshape(1, D), "adam")
    res["final_g"] = [t.reshape(D) for t in fgr]
    bparts = jnp.stack(dmod_all, axis=1)
    res["b_ada"] = _adam(bparts, b_ada, m_b_ada, v_b_ada, "adam")
    cact_t = cact_all.T
    for l in range(DEPTH):
        dm_loc = lax.dynamic_slice_in_dim(dmod_all[l], me * n_ada, n_ada, axis=1)
        res["w_ada"] = _adam_layer(_ada_grad(cact_t, dm_loc), w_ada, m_w_ada, v_w_ada, l, res.get("w_ada"),
                                   "adam_layer")

    order = ['norm_mix_g', 'w_in', 'w_ret_out', 'w_sb_out', 'w_mix_out', 'norm_mlp_g', 'w_up', 'w_down', 'w_ada', 'b_ada', 'final_g']
    out = [loss, grad_x]
    for i in range(4):
        out += [res[nm][i] for nm in order]
    return tuple(out)
```

```python
import functools
import math

import jax
import jax.numpy as jnp
import numpy as np
from jax import lax
from jax.experimental import pallas as pl
from jax.experimental.pallas import tpu as pltpu

F32 = jnp.float32
BF16 = jnp.bfloat16

N_DEV = 8
D_MODEL = 1024
DEPTH = 2
RET_HEADS = 4
RET_QK = 256
RET_V = 512
RET_CHUNK = 128
ROPE_BASE = 10000.0
SB_HEADS = 16
SB_DIM = 64
D_FF = 4096
N_MOD = 6
EPS = 1e-6
GN_EPS = 1e-5
O_RQ, O_RK, O_RV, O_RG, O_SQ, O_SK, O_SV, O_GA, O_GB = 0, 1024, 2048, 4096, 6144, 7168, 8192, 9216, 10240
IN_W = 11264

ADAM_LR, ADAM_B1, ADAM_B2, ADAM_EPS, ADAM_WD, ADAM_STEP = 0.001, 0.9, 0.999, 1e-08, 0.01, 10

VMEM_LIMIT = 56 * 1024 * 1024
DW_TK = 2048


def _cparams(sem):
    return pltpu.CompilerParams(dimension_semantics=sem, vmem_limit_bytes=VMEM_LIMIT)


def _mm(a, b, *, ta=False, tb=False, tm=1024, tn=1024, tk=None, a_ex=(), pro=None, o_ex=(), epi=None,
        outs=(F32,), name, deps=()):
    if ta:
        K, M = a.shape
    else:
        M, K = a.shape
    N = b.shape[0] if tb else b.shape[1]
    tm, tn, tk = min(tm, M), min(tn, N), K if tk is None else min(tk, K)
    assert M % tm == 0 and N % tn == 0 and K % tk == 0, (name, M, N, K, tm, tn, tk)
    nk = K // tk
    in_specs = [
        pl.BlockSpec((tk, tm), lambda i, j, k: (k, i)) if ta else pl.BlockSpec((tm, tk), lambda i, j, k: (i, k)),
        pl.BlockSpec((tn, tk), lambda i, j, k: (j, k)) if tb else pl.BlockSpec((tk, tn), lambda i, j, k: (k, j)),
    ]
    args = [a, b]
    for arr, kind, *off in a_ex:
        off = off[0] if off else 0
        if kind == 'a' and ta:
            assert off % tm == 0
            in_specs.append(pl.BlockSpec((tk, tm), lambda i, j, k, o=off // tm: (k, o + i)))
        elif kind == 'a':
            assert off % tk == 0
            in_specs.append(pl.BlockSpec((tm, tk), lambda i, j, k, o=off // tk: (i, o + k)))
        elif kind == 'k':
            in_specs.append(pl.BlockSpec((tk, 1), lambda i, j, k: (k, 0)) if ta
                            else pl.BlockSpec((1, tk), lambda i, j, k: (0, k)))
        else:
            in_specs.append(pl.BlockSpec((1, tm), lambda i, j, k: (0, i)) if ta
                            else pl.BlockSpec((tm, 1), lambda i, j, k: (i, 0)))
        args.append(arr)
    for arr, kind, *off in o_ex:
        off = off[0] if off else 0
        if kind == 'o':
            assert off % tn == 0
            in_specs.append(pl.BlockSpec((tm, tn), lambda i, j, k, o=off // tn: (i, o + j)))
        elif kind == 'n':
            in_specs.append(pl.BlockSpec((1, tn), lambda i, j, k: (0, j)))
        else:
            in_specs.append(pl.BlockSpec((tm, 1), lambda i, j, k: (i, 0)))
        args.append(arr)
    for arr in deps:
        in_specs.append(pl.BlockSpec(memory_space=pl.ANY))
        args.append(arr)
    na, no, nout, nd = len(a_ex), len(o_ex), len(outs), len(deps)
    dims = (((0 if ta else 1,), (1 if tb else 0,)), ((), ()))

    def body(*refs):
        a_ref, b_ref = refs[0], refs[1]
        aex = refs[2:2 + na]
        oex = refs[2 + na:2 + na + no]
        out_refs = refs[2 + na + no + nd:2 + na + no + nd + nout]

        def product():
            at = a_ref[...]
            if pro is not None:
                at = pro(at, *[r[...] for r in aex])
            return lax.dot_general(at.astype(BF16), b_ref[...].astype(BF16), dims, preferred_element_type=F32)

        def finish(res):
            vals = epi(res, *[r[...] for r in oex]) if epi is not None else (res,)
            for o_ref, v in zip(out_refs, vals):
                o_ref[...] = v.astype(o_ref.dtype)

        if nk == 1:
            finish(product())
            return
        acc = refs[-1]
        k = pl.program_id(2)

        @pl.when(k == 0)
        def _():
            acc[...] = product()

        @pl.when(k > 0)
        def _():
            acc[...] += product()

        @pl.when(k == nk - 1)
        def _():
            finish(acc[...])

    res = pl.pallas_call(
        body, name=name, grid=(M // tm, N // tn, nk), in_specs=in_specs,
        out_specs=[pl.BlockSpec((tm, tn), lambda i, j, k: (i, j)) for _ in outs],
        out_shape=[jax.ShapeDtypeStruct((M, N), dt) for dt in outs],
        scratch_shapes=[pltpu.VMEM((tm, tn), F32)] if nk > 1 else [],
        compiler_params=_cparams(("parallel", "parallel", "arbitrary")),
    )(*args)
    return res


def _ew(fn, rows, vecs=(), cols=(), outs=(), tr=256, name=None):
    S = rows[0].shape[0]
    tr = min(tr, S)
    assert S % tr == 0
    in_specs, args = [], []
    for r in rows:
        in_specs.append(pl.BlockSpec((tr, r.shape[1]), lambda i: (i, 0)))
        args.append(r)
    for v in vecs:
        in_specs.append(pl.BlockSpec((1, v.shape[1]), lambda i: (0, 0)))
        args.append(v)
    for c in cols:
        in_specs.append(pl.BlockSpec((tr, 1), lambda i: (i, 0)))
        args.append(c)
    out_specs, out_shape = [], []
    for o in outs:
        if o[0] == 'row':
            out_specs.append(pl.BlockSpec((tr, o[1]), lambda i: (i, 0)))
            out_shape.append(jax.ShapeDtypeStruct((S, o[1]), o[2]))
        elif o[0] == 'sum':
            out_specs.append(pl.BlockSpec((1, o[1]), lambda i: (0, 0)))
            out_shape.append(jax.ShapeDtypeStruct((1, o[1]), F32))
        else:
            out_specs.append(pl.BlockSpec((tr, 1), lambda i: (i, 0)))
            out_shape.append(jax.ShapeDtypeStruct((S, 1), o[1]))
    nin = len(args)

    def body(*refs):
        i = pl.program_id(0)
        vals = fn(*[r[...] for r in refs[:nin]])
        for o, o_ref, v in zip(outs, refs[nin:], vals):
            if o[0] == 'sum':
                @pl.when(i == 0)
                def _():
                    o_ref[...] = jnp.zeros_like(o_ref)
                o_ref[...] += jnp.sum(v.astype(F32), axis=0, keepdims=True)
            else:
                o_ref[...] = v.astype(o_ref.dtype)

    return pl.pallas_call(
        body, name=name, grid=(S // tr,), in_specs=in_specs, out_specs=out_specs, out_shape=out_shape,
        compiler_params=_cparams(("arbitrary",)),
    )(*args)


def _sigmoid(x):
    return 1.0 / (1.0 + jnp.exp(-x))


def _ret_consts(S):
    h = np.arange(RET_HEADS, dtype=np.float64)
    log_gamma = np.log1p(-np.power(2.0, -5.0 - h))
    idx = np.arange(RET_CHUNK, dtype=np.float64)
    rel = idx[:, None] - idx[None, :]
    decay = np.where(rel >= 0, np.exp(np.maximum(rel, 0.0) * log_gamma[:, None, None]), 0.0)
    xi = np.exp((idx + 1.0) * log_gamma[:, None])[:, :, None]
    zeta = np.exp((RET_CHUNK - 1.0 - idx) * log_gamma[:, None])[:, :, None]
    gamma_c = np.exp(RET_CHUNK * log_gamma)[:, None, None]
    half = RET_QK // 2
    inv_freq = np.power(ROPE_BASE, -np.arange(half, dtype=np.float64) / half).astype(np.float32)
    ang = np.arange(S, dtype=np.float32)[:, None] * inv_freq[None, :]
    f = lambda t: jnp.asarray(t, F32)
    return dict(decay=f(decay), xi=f(xi), zeta=f(zeta), gc=f(gamma_c), cos=f(np.cos(ang)), sin=f(np.sin(ang)))


def _rot(t, cos, sin):
    half = RET_QK // 2
    t1, t2 = t[:, :half], t[:, half:]
    return jnp.concatenate([t1 * cos - t2 * sin, t1 * sin + t2 * cos], axis=-1)


def _rot_inv(t, cos, sin):
    half = RET_QK // 2
    t1, t2 = t[:, :half], t[:, half:]
    return jnp.concatenate([t1 * cos + t2 * sin, t2 * cos - t1 * sin], axis=-1)


_NT = (((1,), (1,)), ((), ()))
_TN = (((0,), (0,)), ((), ()))


def _dot(a, b):
    return jnp.dot(a, b, preferred_element_type=F32)


def _dot_nt(a, b):
    return lax.dot_general(a, b, _NT, preferred_element_type=F32)


def _dot_tn(a, b):
    return lax.dot_general(a, b, _TN, preferred_element_type=F32)


_QW, _VW = RET_HEADS * RET_QK, RET_HEADS * RET_V
_HEADS = range(RET_HEADS)


def _ret_in_specs(C, rev, NC):
    n_of = (lambda n: NC - 1 - n) if rev else (lambda n: n)
    whole3 = lambda n: (0, 0, 0)
    return [
        pl.BlockSpec((C, _QW), lambda n: (n_of(n), O_RQ // _QW)),
        pl.BlockSpec((C, _QW), lambda n: (n_of(n), O_RK // _QW)),
        pl.BlockSpec((C, _VW), lambda n: (n_of(n), O_RV // _VW)),
        pl.BlockSpec((C, RET_QK // 2), lambda n: (n_of(n), 0)),
        pl.BlockSpec((C, RET_QK // 2), lambda n: (n_of(n), 0)),
        pl.BlockSpec((RET_HEADS, C, C), whole3),
        pl.BlockSpec((RET_HEADS, C, 1), whole3),
        pl.BlockSpec((RET_HEADS, C, 1), whole3),
        pl.BlockSpec((RET_HEADS, 1, 1), whole3),
    ]


def _qk_heads(q_ref, k_ref, cos, sin):
    qs, kfs = [], []
    for h in _HEADS:
        cols = slice(h * RET_QK, (h + 1) * RET_QK)
        qs.append(_rot(q_ref[:, cols].astype(F32), cos, sin).astype(BF16))
        kfs.append(_rot(k_ref[:, cols].astype(F32), cos, sin) * (RET_QK ** -0.5))
    return qs, kfs


def _ret_fwd(p, rc):
    S = p.shape[0]
    C = RET_CHUNK
    NC = S // C

    def body(q_ref, k_ref, v_ref, cos_ref, sin_ref, dec_ref, xi_ref, zeta_ref, gc_ref, y_ref, rs_ref, r_acc):
        n = pl.program_id(0)

        @pl.when(n == 0)
        def _():
            r_acc[...] = jnp.zeros_like(r_acc)

        cos, sin = cos_ref[...], sin_ref[...]
        qs, kfs = _qk_heads(q_ref, k_ref, cos, sin)
        vs = [v_ref[:, h * RET_V:(h + 1) * RET_V] for h in _HEADS]
        rbs = [r_acc[h].astype(BF16) for h in _HEADS]
        for h in _HEADS:
            rs_ref[h, 0] = rbs[h]
        ss = [(_dot_nt(qs[h], kfs[h].astype(BF16)) * dec_ref[h]).astype(BF16) for h in _HEADS]
        os = [_dot(ss[h], vs[h]) + _dot(qs[h], rbs[h]) * xi_ref[h] for h in _HEADS]
        for h in _HEADS:
            o = os[h]
            mu = jnp.mean(o, axis=-1, keepdims=True)
            var = jnp.mean(jnp.square(o - mu), axis=-1, keepdims=True)
            y_ref[:, h * RET_V:(h + 1) * RET_V] = ((o - mu) * lax.rsqrt(var + GN_EPS)).astype(y_ref.dtype)
        for h in _HEADS:
            kz = (kfs[h] * zeta_ref[h]).astype(BF16)
            r_acc[h] = r_acc[h] * gc_ref[h] + _dot_tn(kz, vs[h])

    return pl.pallas_call(
        body, name="ret_fwd", grid=(NC,), in_specs=_ret_in_specs(C, False, NC),
        out_specs=[pl.BlockSpec((C, _VW), lambda n: (n, 0)),
                   pl.BlockSpec((RET_HEADS, 1, RET_QK, RET_V), lambda n: (0, n, 0, 0))],
        out_shape=[jax.ShapeDtypeStruct((S, _VW), BF16),
                   jax.ShapeDtypeStruct((RET_HEADS, NC, RET_QK, RET_V), BF16)],
        scratch_shapes=[pltpu.VMEM((RET_HEADS, RET_QK, RET_V), F32)],
        compiler_params=_cparams(("arbitrary",)),
    )(p, p, p, rc['cos'], rc['sin'], rc['decay'], rc['xi'], rc['zeta'], rc['gc'])


def _ret_bwd(p, rstate, dy, rc):
    S = p.shape[0]
    C = RET_CHUNK
    NC = S // C

    def body(q_ref, k_ref, v_ref, cos_ref, sin_ref, dec_ref, xi_ref, zeta_ref, gc_ref, rs_ref, dy_ref,
             dq_ref, dk_ref, dv_ref, dr_acc):
        t = pl.program_id(0)

        @pl.when(t == 0)
        def _():
            dr_acc[...] = jnp.zeros_like(dr_acc)

        cos, sin = cos_ref[...], sin_ref[...]
        qs, kfs = _qk_heads(q_ref, k_ref, cos, sin)
        ks = [kf.astype(BF16) for kf in kfs]
        vs = [v_ref[:, h * RET_V:(h + 1) * RET_V] for h in _HEADS]
        rbs = [rs_ref[h, 0] for h in _HEADS]
        ss = [(_dot_nt(qs[h], ks[h]) * dec_ref[h]).astype(BF16) for h in _HEADS]
        os = [_dot(ss[h], vs[h]) + _dot(qs[h], rbs[h]) * xi_ref[h] for h in _HEADS]
        dobs, doxis = [], []
        for h in _HEADS:
            o = os[h]
            mu = jnp.mean(o, axis=-1, keepdims=True)
            var = jnp.mean(jnp.square(o - mu), axis=-1, keepdims=True)
            rstd = lax.rsqrt(var + GN_EPS)
            yh = (o - mu) * rstd
            dyf = dy_ref[:, h * RET_V:(h + 1) * RET_V].astype(F32)
            do = (dyf - jnp.mean(dyf, axis=-1, keepdims=True)
                  - yh * jnp.mean(dyf * yh, axis=-1, keepdims=True)) * rstd
            dobs.append(do.astype(BF16))
            doxis.append((do * xi_ref[h]).astype(BF16))
        drbs = [dr_acc[h].astype(BF16) for h in _HEADS]
        dss = [(_dot_nt(dobs[h], vs[h]) * dec_ref[h]).astype(BF16) for h in _HEADS]
        for h in _HEADS:
            dq = _dot(dss[h], ks[h]) + _dot_nt(doxis[h], rbs[h])
            dq_ref[:, h * RET_QK:(h + 1) * RET_QK] = _rot_inv(dq, cos, sin).astype(dq_ref.dtype)
        for h in _HEADS:
            dk = _dot_tn(dss[h], qs[h]) + _dot_nt(vs[h], drbs[h]) * zeta_ref[h]
            dk_ref[:, h * RET_QK:(h + 1) * RET_QK] = (_rot_inv(dk, cos, sin) * (RET_QK ** -0.5)).astype(dk_ref.dtype)
        for h in _HEADS:
            kz = (kfs[h] * zeta_ref[h]).astype(BF16)
            dv = _dot_tn(ss[h], dobs[h]) + _dot(kz, drbs[h])
            dv_ref[:, h * RET_V:(h + 1) * RET_V] = dv.astype(dv_ref.dtype)
        for h in _HEADS:
            dr_acc[h] = dr_acc[h] * gc_ref[h] + _dot_tn(qs[h], doxis[h])

    rn = lambda n: NC - 1 - n
    in_specs = _ret_in_specs(C, True, NC) + [
        pl.BlockSpec((RET_HEADS, 1, RET_QK, RET_V), lambda n: (0, rn(n), 0, 0)),
        pl.BlockSpec((C, _VW), lambda n: (rn(n), 0)),
    ]
    return pl.pallas_call(
        body, name="ret_bwd", grid=(NC,), in_specs=in_specs,
        out_specs=[pl.BlockSpec((C, _QW), lambda n: (rn(n), 0)),
                   pl.BlockSpec((C, _QW), lambda n: (rn(n), 0)),
                   pl.BlockSpec((C, _VW), lambda n: (rn(n), 0))],
        out_shape=[jax.ShapeDtypeStruct((S, _QW), BF16),
                   jax.ShapeDtypeStruct((S, _QW), BF16),
                   jax.ShapeDtypeStruct((S, _VW), BF16)],
        scratch_shapes=[pltpu.VMEM((RET_HEADS, RET_QK, RET_V), F32)],
        compiler_params=_cparams(("arbitrary",)),
    )(p, p, p, rc['cos'], rc['sin'], rc['decay'], rc['xi'], rc['zeta'], rc['gc'], rstate, dy)


SB_T = 256
SB_SCALE = SB_DIM ** -0.5


def _tri():
    j = np.arange(SB_T)
    after = (j[:, None] > j[None, :]).astype(np.float32)
    upto = (j[:, None] <= j[None, :]).astype(np.float32)
    return jnp.asarray(np.stack([after, upto]), BF16)


def _softplus_parts(z):
    neg_abs = lax.bitcast_convert_type(lax.bitcast_convert_type(z, jnp.uint32) | jnp.uint32(0x80000000), F32)
    e = jnp.exp(neg_abs)
    return jnp.maximum(z, 0.0) + jnp.log(1.0 + e), e


def _sb_fwd(p, tri):
    S = p.shape[0]
    T = min(SB_T, S)
    NQ = S // T
    qb, kb, vb = O_SQ // 128, O_SK // 128, O_SV // 128

    def body(q_ref, k_ref, v_ref, tri_ref, o_ref, a_hbm, o_acc, run, zbuf, abuf, a_sems):
        hp, i = pl.program_id(0), pl.program_id(1)
        lane = lax.broadcasted_iota(jnp.int32, (1, 128), 1)
        tri_after = tri_ref[0]
        qs = [jnp.where((lane >= 64) if hh else (lane < 64), q_ref[...], jnp.zeros_like(q_ref[...]))
              * jnp.asarray(SB_SCALE, BF16) for hh in range(2)]
        o_acc[...] = jnp.zeros_like(o_acc)
        run[...] = jnp.zeros_like(run)

        def kv(ref, j):
            return ref[pl.ds(pl.multiple_of(j * T, T), T), :]

        def save(t):
            return pltpu.make_async_copy(abuf.at[t % 2], a_hbm.at[hp, (i * (i + 1)) // 2 + i - t], a_sems.at[t % 2])

        for hh in range(2):
            zbuf[hh] = _dot_nt(qs[hh], kv(k_ref, i))

        def block(t, diagonal):
            j = i - t
            slot = t % 2
            if diagonal:
                msk = lax.broadcasted_iota(jnp.int32, (T, T), 1) < lax.broadcasted_iota(jnp.int32, (T, T), 0)
            else:
                @pl.when(t >= 2)
                def _():
                    save(t - 2).wait()
                av = [_dot(abuf[1 - slot, hh], kv(v_ref, j + 1)) for hh in range(2)]
            lss, exs, tot, zn = [], [], [], []
            for hh in range(2):
                z = zbuf[hh]
                sp, _ = _softplus_parts(z)
                lss.append(z - sp)
                if diagonal:
                    sp = jnp.where(msk, sp, 0.0)
                exs.append(_dot(sp.astype(BF16), tri_after))
                tot.append(sp[:, 0:1])
                zn.append(_dot_nt(qs[hh], kv(k_ref, jnp.maximum(j - 1, 0))))
            for hh in range(2):
                cs = run[hh]
                a = jnp.exp(lss[hh] - exs[hh] - cs)
                if diagonal:
                    a = jnp.where(msk, a, 0.0)
                abuf[slot, hh] = a.astype(BF16)
                run[hh] = cs + exs[hh][:, 0:1] + tot[hh]
            for hh in range(2):
                if not diagonal:
                    o_acc[hh] += av[hh]
                zbuf[hh] = zn[hh]
            save(t).start()

        block(0, True)

        def step(t, carry):
            block(t, False)
            return carry

        lax.fori_loop(1, i + 1, step, 0)
        last = i % 2
        o_ref[...] = jnp.where(lane < 64, o_acc[0] + _dot(abuf[last, 0], kv(v_ref, 0)),
                               o_acc[1] + _dot(abuf[last, 1], kv(v_ref, 0))).astype(o_ref.dtype)
        save(i).wait()

        @pl.when(i >= 1)
        def _():
            save(i - 1).wait()

    return pl.pallas_call(
        body, name="sb_fwd", grid=(SB_HEADS // 2, NQ),
        scratch_shapes=[pltpu.VMEM((2, T, 128), F32), pltpu.VMEM((2, T, 1), F32), pltpu.VMEM((2, T, T), F32),
                        pltpu.VMEM((2, 2, T, T), BF16), pltpu.SemaphoreType.DMA((2,))],
        in_specs=[pl.BlockSpec((T, 128), lambda h, i: (i, qb + h)),
                  pl.BlockSpec((S, 128), lambda h, i: (0, kb + h)),
                  pl.BlockSpec((S, 128), lambda h, i: (0, vb + h)),
                  pl.BlockSpec((1, T, T), lambda h, i: (0, 0, 0))],
        out_specs=[pl.BlockSpec((T, 128), lambda h, i: (i, h)),
                   pl.BlockSpec(memory_space=pl.ANY)],
        out_shape=[jax.ShapeDtypeStruct((S, SB_HEADS * SB_DIM), BF16),
                   jax.ShapeDtypeStruct((SB_HEADS // 2, (NQ * (NQ + 1)) // 2, 2, T, T), BF16)],
        compiler_params=_cparams(("parallel", "arbitrary")),
    )(p, p, p, tri)


def _sb_bwd(p, a_sav, dy, tri):
    S = p.shape[0]
    T = min(SB_T, S)
    NQ = S // T
    qb, kb, vb = O_SQ // 128, O_SK // 128, O_SV // 128

    def body(q_ref, k_ref, v_ref, a_hbm, dy_ref, tri_ref, dq_ref, dk_ref, dv_ref, dk_acc, dv_acc, dq_acc, run,
             zbuf, dabuf, dzbuf, abuf, a_sems):
        hp, i = pl.program_id(0), pl.program_id(1)

        @pl.when(i == 0)
        def _():
            dk_acc[...] = jnp.zeros_like(dk_acc)
            dv_acc[...] = jnp.zeros_like(dv_acc)

        def fetch(j):
            return pltpu.make_async_copy(a_hbm.at[hp, (i * (i + 1)) // 2 + j], abuf.at[j % 2], a_sems.at[j % 2])

        fetch(0).start()
        lane = lax.broadcasted_iota(jnp.int32, (1, 128), 1)
        tri_upto = tri_ref[0]
        hms = [(lane >= 64) if hh else (lane < 64) for hh in range(2)]
        qs = [jnp.where(hm, q_ref[...], jnp.zeros_like(q_ref[...])) * jnp.asarray(SB_SCALE, BF16) for hm in hms]
        dos = [jnp.where(hm, dy_ref[...], jnp.zeros_like(dy_ref[...])) for hm in hms]
        qst = [t.T for t in qs]
        dost = [t.T for t in dos]
        dq_acc[...] = jnp.zeros_like(dq_acc)
        run[...] = jnp.zeros_like(run)
        dzbuf[...] = jnp.zeros_like(dzbuf)

        def kv(ref, j):
            return ref[pl.ds(pl.multiple_of(j * T, T), T), :]

        def apply(jp, dq_add, dk_add):
            cols = pl.ds(pl.multiple_of(jp * T, T), T)
            for hh in range(2):
                dq_acc[hh] += dq_add[hh]
            dk_acc[:, cols] += dk_add

        for hh in range(2):
            zbuf[hh] = _dot_nt(qs[hh], kv(k_ref, 0))
            dabuf[hh] = _dot_nt(dos[hh], kv(v_ref, 0))

        def block(j, diagonal):
            jp = jnp.maximum(j - 1, 0)
            slot = j % 2
            fetch(j).wait()
            if diagonal:
                msk = lax.broadcasted_iota(jnp.int32, (T, T), 1) < lax.broadcasted_iota(jnp.int32, (T, T), 0)
            else:
                fetch(j + 1).start()
            kp = kv(k_ref, jp)
            dq_add = [_dot(dzbuf[hh], kp) for hh in range(2)]
            cols = pl.ds(pl.multiple_of(j * T, T), T)
            dv_acc[:, cols] += _dot(dost[0], abuf[slot, 0]) + _dot(dost[1], abuf[slot, 1])
            sigs, pgs, gs, zn, dan, dk_part = [], [], [], [], [], []
            for hh in range(2):
                z = zbuf[hh]
                neg_abs = lax.bitcast_convert_type(lax.bitcast_convert_type(z, jnp.uint32) | jnp.uint32(0x80000000), F32)
                e = jnp.exp(neg_abs)
                r = 1.0 / (1.0 + e)
                sigs.append(jnp.where(z >= 0, r, e * r))
                g = abuf[slot, hh].astype(F32) * dabuf[hh]
                gs.append(g)
                pgs.append(_dot(g.astype(BF16), tri_upto))
                if not diagonal:
                    zn.append(_dot_nt(qs[hh], kv(k_ref, j + 1)))
                    dan.append(_dot_nt(dos[hh], kv(v_ref, j + 1)))
                dk_part.append(_dot(qst[hh], dzbuf[hh]))
            for hh in range(2):
                cg = run[hh]
                dz = gs[hh] - sigs[hh] * (cg + pgs[hh])
                if diagonal:
                    dz = jnp.where(msk, dz, 0.0)
                run[hh] = cg + pgs[hh][:, T - 1:T]
                dzbuf[hh] = dz.astype(BF16)
            apply(jp, dq_add, dk_part[0] + dk_part[1])
            if not diagonal:
                for hh in range(2):
                    zbuf[hh] = zn[hh]
                    dabuf[hh] = dan[hh]

        def step(j, carry):
            block(j, False)
            return carry

        lax.fori_loop(0, i, step, 0)
        block(i, True)
        ki = kv(k_ref, i)
        apply(i, [_dot(dzbuf[hh], ki) for hh in range(2)], _dot(qst[0], dzbuf[0]) + _dot(qst[1], dzbuf[1]))
        dq_ref[...] = (jnp.where(lane < 64, dq_acc[0], dq_acc[1]) * SB_SCALE).astype(dq_ref.dtype)

        @pl.when(i == NQ - 1)
        def _():
            dk_ref[...] = dk_acc[...].T.astype(dk_ref.dtype)
            dv_ref[...] = dv_acc[...].T.astype(dv_ref.dtype)

    W = SB_HEADS * SB_DIM
    return pl.pallas_call(
        body, name="sb_bwd", grid=(SB_HEADS // 2, NQ),
        in_specs=[pl.BlockSpec((T, 128), lambda h, i: (i, qb + h)),
                  pl.BlockSpec((S, 128), lambda h, i: (0, kb + h)),
                  pl.BlockSpec((S, 128), lambda h, i: (0, vb + h)),
                  pl.BlockSpec(memory_space=pl.ANY),
                  pl.BlockSpec((T, 128), lambda h, i: (i, h)),
                  pl.BlockSpec((1, T, T), lambda h, i: (1, 0, 0))],
        out_specs=[pl.BlockSpec((T, 128), lambda h, i: (i, h)),
                   pl.BlockSpec((S, 128), lambda h, i: (0, h)),
                   pl.BlockSpec((S, 128), lambda h, i: (0, h))],
        out_shape=[jax.ShapeDtypeStruct((S, W), BF16)] * 3,
        scratch_shapes=[pltpu.VMEM((128, S), F32), pltpu.VMEM((128, S), F32), pltpu.VMEM((2, T, 128), F32),
                        pltpu.VMEM((2, T, 1), F32), pltpu.VMEM((2, T, T), F32), pltpu.VMEM((2, T, T), F32),
                        pltpu.VMEM((2, T, T), BF16), pltpu.VMEM((2, 2, T, T), BF16),
                        pltpu.SemaphoreType.DMA((2,))],
        compiler_params=_cparams(("parallel", "arbitrary")),
    )(p, p, p, a_sav, dy, tri)


def _exchange(srcs, out_shapes, src_slice, dst_slice, name):
    n = len(srcs)

    def body(*refs):
        ins, outs = refs[:n], refs[n:2 * n]
        send_sems, recv_sems, loc_sems = refs[2 * n:]
        x, y, c = lax.axis_index("x"), lax.axis_index("y"), lax.axis_index("c")
        me = 4 * x + 2 * y + c
        local = [pltpu.make_async_copy(src_slice(t, ins[t], me), dst_slice(t, outs[t], me), loc_sems.at[t])
                 for t in range(n)]
        for cp in local:
            cp.start()
        sends, recvs = [], []
        for k in (1, 2, 4, 6, 3, 5, 7):
            px = 1 - x if k & 4 else x
            py = 1 - y if k & 2 else y
            pc = 1 - c if k & 1 else c
            peer = 4 * px + 2 * py + pc
            for t in range(n):
                s = t * 7 + k - 1
                sends.append(pltpu.make_async_remote_copy(
                    src_ref=src_slice(t, ins[t], peer), dst_ref=dst_slice(t, outs[t], me),
                    send_sem=send_sems.at[s], recv_sem=recv_sems.at[s],
                    device_id=(px, py, pc), device_id_type=pl.DeviceIdType.MESH))
                recvs.append(pltpu.make_async_remote_copy(
                    src_ref=src_slice(t, ins[t], me), dst_ref=dst_slice(t, outs[t], peer),
                    send_sem=send_sems.at[s], recv_sem=recv_sems.at[s],
                    device_id=(px, py, pc), device_id_type=pl.DeviceIdType.MESH))
        for cp in sends:
            cp.start()
        for cp in recvs:
            cp.wait_recv()
        for cp in sends:
            cp.wait_send()
        for cp in local:
            cp.wait()

    anyspec = pl.BlockSpec(memory_space=pl.ANY)
    return pl.pallas_call(
        body, name=name, in_specs=[anyspec] * n, out_specs=[anyspec] * n,
        out_shape=[jax.ShapeDtypeStruct(s, d) for s, d in out_shapes],
        scratch_shapes=[pltpu.SemaphoreType.DMA((7 * n,)), pltpu.SemaphoreType.DMA((7 * n,)),
                        pltpu.SemaphoreType.DMA((n,))],
    )(*srcs)


def _all_gather_lead(xs, name):
    return _exchange(
        xs, [((N_DEV,) + x.shape, x.dtype) for x in xs],
        lambda t, ref, peer: ref, lambda t, ref, who: ref.at[who], name)


def _all_to_all_lead(xs, name):
    return _exchange(
        xs, [(x.shape, x.dtype) for x in xs],
        lambda t, ref, peer: ref.at[peer], lambda t, ref, who: ref.at[who], name)


_W_AXIS = {"w_in": 1, "w_ret_out": 0, "w_sb_out": 0, "w_mix_out": 0, "w_up": 1, "w_down": 0}
_W_NAMES = tuple(_W_AXIS)


def _window(ref, axis, who, width, count=1):
    start = pl.multiple_of(who * width, width)
    return ref.at[pl.ds(start, count * width), :] if axis == 0 else ref.at[:, pl.ds(start, count * width)]


_HBM = pl.BlockSpec(memory_space=pltpu.HBM)
_SEM = pl.BlockSpec(memory_space=pltpu.SEMAPHORE)
_EFFECT = pltpu.SideEffectType.DATAFLOW_SIDE_EFFECTING


def _exchange_start(srcs, shapes, src_slice, dst_slice, name, deps=()):
    n, nd = len(srcs), len(deps)
    lands = [pltpu.with_memory_space_constraint(lax.empty(s, d), pltpu.HBM) for s, d in shapes]

    def body(*refs):
        ins, lnd = refs[:n], refs[n:2 * n]
        sems = refs[2 * n + nd:4 * n + nd]
        token = refs[6 * n + nd]
        x, y, c = lax.axis_index("x"), lax.axis_index("y"), lax.axis_index("c")
        me = 4 * x + 2 * y + c
        for k in (0, 1, 2, 4, 6, 3, 5, 7):
            px = 1 - x if k & 4 else x
            py = 1 - y if k & 2 else y
            pc = 1 - c if k & 1 else c
            peer = 4 * px + 2 * py + pc
            for t in range(n):
                pltpu.make_async_remote_copy(
                    src_ref=src_slice(t, ins[t], peer), dst_ref=dst_slice(t, lnd[t], me),
                    send_sem=sems[2 * t], recv_sem=sems[2 * t + 1],
                    device_id=(px, py, pc), device_id_type=pl.DeviceIdType.MESH).start()
        token[...] = jnp.zeros_like(token)

    res = pl.pallas_call(
        body, name=name, in_specs=[_HBM] * (2 * n) + [pl.BlockSpec(memory_space=pl.ANY)] * nd,
        out_specs=[_SEM] * (2 * n) + [_HBM] * (2 * n) + [pl.BlockSpec(memory_space=pltpu.VMEM)],
        out_shape=[pltpu.SemaphoreType.DMA(())] * (2 * n) + [pltpu.HBM(s.shape, s.dtype) for s in srcs]
        + [pltpu.HBM(s.shape, s.dtype) for s in lands] + [jax.ShapeDtypeStruct((8, 128), F32)],
        input_output_aliases={t: 2 * n + t for t in range(2 * n)},
        compiler_params=pltpu.CompilerParams(has_side_effects=_EFFECT),
    )(*[pltpu.with_memory_space_constraint(s, pltpu.HBM) for s in srcs], *lands, *deps)
    return dict(n=n, sems=res[:2 * n], srcs=res[2 * n:3 * n], lands=res[3 * n:4 * n], token=res[4 * n])


def _exchange_wait(h, after, name):
    n = h['n']

    def body(*refs):
        lnd = refs[n:2 * n]
        sems = refs[2 * n:4 * n]
        x, y, c = lax.axis_index("x"), lax.axis_index("y"), lax.axis_index("c")
        for t in range(n):
            w = lnd[t]
            cp = pltpu.make_async_remote_copy(src_ref=w, dst_ref=w, send_sem=sems[2 * t], recv_sem=sems[2 * t + 1],
                                              device_id=(x, y, 1 - c), device_id_type=pl.DeviceIdType.MESH)
            cp.wait_send()
            cp.wait_recv()

    after = list(after)
    res = pl.pallas_call(
        body, name=name,
        in_specs=[_HBM] * (2 * n) + [_SEM] * (2 * n) + [pl.BlockSpec(memory_space=pl.ANY)] * len(after),
        out_specs=[_HBM] * (2 * n),
        out_shape=[pltpu.HBM(s.shape, s.dtype) for s in h['srcs']] + [pltpu.HBM(s.shape, s.dtype) for s in h['lands']],
        input_output_aliases={t: t for t in range(2 * n)},
        compiler_params=pltpu.CompilerParams(has_side_effects=_EFFECT),
    )(*h['srcs'], *h['lands'], *h['sems'], *after)
    return list(res[n:])


def _gather_start(shards, names, tag, deps=()):
    xs = [shards[nm] for nm in names]
    axes = [_W_AXIS[nm] for nm in names]
    widths = [x.shape[ax] for x, ax in zip(xs, axes)]
    shapes = [(tuple(d * (N_DEV if a == ax else 1) for a, d in enumerate(x.shape)), x.dtype) for x, ax in zip(xs, axes)]
    src = lambda t, ref, peer: ref
    dst = lambda t, ref, who: _window(ref, axes[t], who, widths[t])
    h = _exchange_start(xs, shapes, src, dst, "gw_start_" + tag, deps)
    h['tag'] = "gw_wait_" + tag
    return h


def _scatter_start(grads, names, tag):
    xs = [grads[nm] for nm in names]
    axes = [_W_AXIS[nm] for nm in names]
    widths = [x.shape[ax] // N_DEV for x, ax in zip(xs, axes)]
    shapes = [((N_DEV,) + tuple(d // (N_DEV if a == ax else 1) for a, d in enumerate(x.shape)), x.dtype)
              for x, ax in zip(xs, axes)]
    src = lambda t, ref, peer: _window(ref, axes[t], peer, widths[t])
    dst = lambda t, ref, who: ref.at[who]
    h = _exchange_start(xs, shapes, src, dst, "sg_start_" + tag)
    h['tag'] = "sg_wait_" + tag
    return h


def _finish(h, after):
    return _exchange_wait(h, after, h['tag'])


class _LayerWeights:
    def __init__(self, groups, started):
        self.groups = groups
        self.started = started
        self.got = {}
        self.after = None

    def __getitem__(self, nm):
        if nm not in self.got:
            for names, h in self.groups:
                if nm in names:
                    self.got.update(zip(names, _finish(h, list(self.after) + self.started)))
        return self.got[nm]


def _adam_math(p_ref, w, m, v):
    g = p_ref[0].astype(F32)
    for s in range(1, p_ref.shape[0]):
        g = g + p_ref[s].astype(F32)
    bc1 = 1.0 / (1.0 - ADAM_B1 ** ADAM_STEP)
    bc2 = 1.0 / (1.0 - ADAM_B2 ** ADAM_STEP)
    mm = ADAM_B1 * m + (1.0 - ADAM_B1) * g
    vv = ADAM_B2 * v + (1.0 - ADAM_B2) * jnp.square(g)
    return g, -ADAM_LR * ((mm * bc1) / (jnp.sqrt(vv * bc2) + ADAM_EPS) + ADAM_WD * w), mm, vv


def _adam(parts, w, m, v, name, tr=256):
    P, R, C = parts.shape
    tr = min(tr, R)
    assert R % tr == 0

    def body(p_ref, w_ref, m_ref, v_ref, *outs):
        for o_ref, val in zip(outs, _adam_math(p_ref, w_ref[...], m_ref[...], v_ref[...])):
            o_ref[...] = val

    spec = pl.BlockSpec((tr, C), lambda i: (i, 0))
    return pl.pallas_call(
        body, name=name, grid=(R // tr,),
        in_specs=[pl.BlockSpec((P, tr, C), lambda i: (0, i, 0)), spec, spec, spec],
        out_specs=[spec] * 4, out_shape=[jax.ShapeDtypeStruct((R, C), F32)] * 4,
        compiler_params=_cparams(("parallel",)),
    )(parts, w, m, v)


def _adam_layer(parts, w, m, v, l, prev, name, tr=256):
    P, R, C = parts.shape
    tr = min(tr, R)
    assert R % tr == 0 and w.shape == (DEPTH, R, C)
    npv = 0 if prev is None else 4

    def body(p_ref, w_ref, m_ref, v_ref, *rest):
        for o_ref, val in zip(rest[npv:], _adam_math(p_ref, w_ref[0], m_ref[0], v_ref[0])):
            o_ref[0] = val

    spec = pl.BlockSpec((1, tr, C), lambda i: (l, i, 0))
    return pl.pallas_call(
        body, name=name, grid=(R // tr,),
        in_specs=[pl.BlockSpec((P, tr, C), lambda i: (0, i, 0)), spec, spec, spec]
        + [pl.BlockSpec(memory_space=pl.ANY)] * npv,
        out_specs=[spec] * 4, out_shape=[jax.ShapeDtypeStruct((DEPTH, R, C), F32)] * 4,
        input_output_aliases={4 + t: t for t in range(npv)},
        compiler_params=_cparams(("parallel",)),
    )(parts, w, m, v, *([] if prev is None else prev))


def _mod_partial(cact_all, w_ada_l, b_ada_l):
    def body(c_ref, w_ref, b_ref, o_ref):
        o_ref[...] = _dot(c_ref[...].astype(BF16), w_ref[...].astype(BF16)) + b_ref[...]

    return pl.pallas_call(
        body, name="mod_partial", out_shape=jax.ShapeDtypeStruct((cact_all.shape[0], w_ada_l.shape[1]), F32),
        compiler_params=pltpu.CompilerParams(vmem_limit_bytes=VMEM_LIMIT),
    )(cact_all, w_ada_l, b_ada_l)


def _ada_grad(cact_t, dmod):
    D, n = cact_t.shape[0], dmod.shape[1]

    def body(c_ref, d_ref, o_ref):
        ct = c_ref[...].astype(BF16).astype(F32)
        dm = d_ref[...].astype(BF16).astype(F32)
        acc = ct[:, 0:1] * dm[0:1, :]
        for b in range(1, N_DEV):
            acc = acc + ct[:, b:b + 1] * dm[b:b + 1, :]
        o_ref[0] = acc

    return pl.pallas_call(
        body, name="ada_grad", out_shape=jax.ShapeDtypeStruct((1, D, n), F32),
        compiler_params=pltpu.CompilerParams(vmem_limit_bytes=VMEM_LIMIT),
    )(cact_t, dmod)


def _norm_mod(x, r, gv, sh):
    return x * r * gv + sh


def _silu(x):
    return x * _sigmoid(x)


def _layer_fwd(x0, mod, gn1, gn2, W, rc, tri):
    S = x0.shape[0]
    sh1, sc1, g1m, sh2, sc2, g2m = [mod[i:i + 1] for i in range(N_MOD)]
    gv1 = gn1 * (1.0 + sc1)
    gv2 = gn2 * (1.0 + sc2)
    (r1,) = _ew(lambda x: (lax.rsqrt(jnp.mean(x * x, axis=-1, keepdims=True) + EPS),), [x0], outs=[('col', F32)],
                name="row_rstd")
    W.after = [r1, gv1]
    (p,) = _mm(x0, W["w_in"], a_ex=[(r1, 'm'), (gv1, 'k'), (sh1, 'k')], pro=_norm_mod, outs=(BF16,), name="mm_in")
    yret, rstate = _ret_fwd(p, rc)
    ysb, sbc = _sb_fwd(p, tri)
    W.after = [ysb]
    (ya,) = _mm(yret, W["w_ret_out"], tm=512, a_ex=[(p, 'a', O_RG)],
                pro=lambda yr, g: _silu(g.astype(F32)) * yr.astype(F32), outs=(BF16,), name="mm_ret_out")
    yb, mg = _mm(ysb, W["w_sb_out"], o_ex=[(ya, 'o'), (p, 'o', O_GA), (p, 'o', O_GB)],
                 epi=lambda acc, a, ga, gb: (acc, _sigmoid(ga.astype(F32)) * a.astype(F32)
                                             + _sigmoid(gb.astype(F32)) * acc),
                 outs=(BF16, BF16), name="mm_sb_out")
    mo, x1 = _mm(mg, W["w_mix_out"], o_ex=[(x0, 'o'), (g1m, 'n')],
                 epi=lambda acc, x, g: (acc, x + g * acc), outs=(BF16, F32), name="mm_mix_out")
    (r2,) = _ew(lambda x: (lax.rsqrt(jnp.mean(x * x, axis=-1, keepdims=True) + EPS),), [x1], outs=[('col', F32)],
                name="row_rstd")
    (act,) = _mm(x1, W["w_up"], a_ex=[(r2, 'm'), (gv2, 'k'), (sh2, 'k')], pro=_norm_mod,
                 epi=lambda acc: (jnp.maximum(acc, 0.0),), outs=(BF16,), name="mm_up")
    dn, x2 = _mm(act, W["w_down"], tm=512, pro=lambda a: a * a,
                 o_ex=[(x1, 'o'), (g2m, 'n')], epi=lambda acc, x, g: (acc, x + g * acc), outs=(BF16, F32),
                 name="mm_down")
    saved = dict(x0=x0, r1=r1, p=p, yret=yret, rstate=rstate, ysb=ysb, sbc=sbc, ya=ya, yb=yb, mg=mg, mo=mo, x1=x1, r2=r2,
                 act=act, dn=dn, gv1=gv1, gv2=gv2, mod=mod, gn1=gn1, gn2=gn2)
    return x2, saved


def _norm_bwd(dh, x, r, dres, gv, gn, extra_rows=(), extra_vecs=(), extra_fn=None, extra_outs=(), name="norm_bwd"):
    D = x.shape[1]
    ne = len(extra_rows)

    def fn(dh_t, x_t, dres_t, *rest):
        er, rest = rest[:ne], rest[ne:]
        gv_t = rest[0]
        ev, r_t = rest[1:-1], rest[-1]
        xh = x_t * r_t
        dxh = dh_t * gv_t
        dx = r_t * (dxh - xh * jnp.mean(dxh * xh, axis=-1, keepdims=True)) + dres_t
        base = (dx, dh_t, dh_t * xh)
        if extra_fn is None:
            return base
        return base + tuple(extra_fn(dx, *er, *ev))

    return _ew(fn, [dh, x, dres] + list(extra_rows), vecs=[gv] + list(extra_vecs), cols=[r],
               outs=[('row', D, F32), ('sum', D), ('sum', D)] + list(extra_outs), name=name)


def _layer_bwd(dx2, sv, W, rc, tri, emit):
    mod = sv['mod']
    sh1, sc1, g1m, sh2, sc2, g2m = [mod[i:i + 1] for i in range(N_MOD)]
    D = D_MODEL
    p = sv['p']
    d_g2m, d_dn = _ew(lambda dx, dn, g: (dx * dn.astype(F32), dx * g), [dx2, sv['dn']], vecs=[g2m],
                      outs=[('sum', D), ('row', D, BF16)], name="gate_bwd")
    (d_up,) = _mm(d_dn, W["w_down"], tb=True, o_ex=[(sv['act'], 'o')],
                  epi=lambda acc, a: (acc * 2.0 * a.astype(F32),), outs=(BF16,), name="mm_down_dx")
    (gw_down,) = _mm(sv['act'], d_dn, ta=True, tk=DW_TK, pro=lambda a: a * a, outs=(BF16,), name="mm_down_dw")
    (gw_up,) = _mm(sv['x1'], d_up, ta=True, tm=512, tk=DW_TK,
                   a_ex=[(sv['r2'], 'k'), (sv['gv2'].reshape(1, D), 'm'), (sh2, 'm')], pro=_norm_mod,
                   outs=(BF16,), name="mm_up_dw")
    tok = emit(dict(w_down=gw_down, w_up=gw_up), "mlp")
    (d_h2,) = _mm(d_up, W["w_up"], tb=True, tk=2048, outs=(F32,), name="mm_up_dx", deps=[tok])
    dx1, d_sh2, s_h2, d_g1m, d_mo = _norm_bwd(
        d_h2, sv['x1'], sv['r2'], dx2, sv['gv2'], sv['gn2'],
        extra_rows=[sv['mo']], extra_vecs=[g1m],
        extra_fn=lambda dx, mo, g: (dx * mo.astype(F32), dx * g),
        extra_outs=[('sum', D), ('row', D, BF16)], name="norm_bwd_mlp")
    d_sc2 = sv['gn2'] * s_h2
    d_gn2 = (1.0 + sc2) * s_h2
    def mix_epi(acc, ya, yb, ga, gb):
        sa, sb = _sigmoid(ga.astype(F32)), _sigmoid(gb.astype(F32))
        return (acc * sa, acc * sb, acc * ya.astype(F32) * sa * (1.0 - sa), acc * yb.astype(F32) * sb * (1.0 - sb))

    d_ya, d_yb, d_ga, d_gb = _mm(d_mo, W["w_mix_out"], tb=True, tm=512,
                                 o_ex=[(sv['ya'], 'o'), (sv['yb'], 'o'), (p, 'o', O_GA), (p, 'o', O_GB)], epi=mix_epi,
                                 outs=(BF16,) * 4, name="mm_mix_dx")
    (gw_mix,) = _mm(sv['mg'], d_mo, ta=True, tk=DW_TK, outs=(BF16,), name="mm_mix_dw")

    def ro_epi(acc, g, yr):
        gf = g.astype(F32)
        s = _sigmoid(gf)
        return (acc * yr.astype(F32) * s * (1.0 + gf * (1.0 - s)), acc * gf * s)

    d_rg, d_yret = _mm(d_ya, W["w_ret_out"], tb=True, tm=512, o_ex=[(p, 'o', O_RG), (sv['yret'], 'o')], epi=ro_epi,
                       outs=(BF16, BF16), name="mm_ret_dx")
    (gw_ro,) = _mm(sv['yret'], d_ya, ta=True, tm=512, tk=DW_TK, a_ex=[(p, 'a', O_RG)],
                   pro=lambda yr, g: _silu(g.astype(F32)) * yr.astype(F32), outs=(BF16,), name="mm_ret_dw")
    (gw_so,) = _mm(sv['ysb'], d_yb, ta=True, tk=DW_TK, outs=(BF16,), name="mm_sb_dw")
    tok = emit(dict(w_mix_out=gw_mix, w_ret_out=gw_ro, w_sb_out=gw_so), "mix")
    (d_ysb,) = _mm(d_yb, W["w_sb_out"], tb=True, outs=(BF16,), name="mm_sb_dx", deps=[tok])
    d_sq, d_sk, d_sv = _sb_bwd(p, sv['sbc'], d_ysb, tri)
    d_rq, d_rk, d_rv = _ret_bwd(p, sv['rstate'], d_yret, rc)
    dp = jnp.concatenate([d_rq, d_rk, d_rv, d_rg, d_sq, d_sk, d_sv, d_ga, d_gb], axis=1)
    (gw_in,) = _mm(sv['x0'], dp, ta=True, tm=512, tk=DW_TK,
                   a_ex=[(sv['r1'], 'k'), (sv['gv1'].reshape(1, D), 'm'), (sh1, 'm')], pro=_norm_mod,
                   outs=(BF16,), name="mm_in_dw")
    tok = emit(dict(w_in=gw_in), "in")
    (d_h,) = _mm(dp, W["w_in"], tb=True, tk=IN_W // 4, outs=(F32,), name="mm_in_dx", deps=[tok])
    dx0, d_sh1, s_h1 = _norm_bwd(d_h, sv['x0'], sv['r1'], dx1, sv['gv1'], sv['gn1'], name="norm_bwd_mix")
    d_sc1 = sv['gn1'] * s_h1
    d_gn1 = (1.0 + sc1) * s_h1
    d_mod = jnp.concatenate([d_sh1, d_sc1, d_g1m, d_sh2, d_sc2, d_g2m], axis=1)
    return dx0, d_mod, d_gn1, d_gn2


def kernel(x, c, norm_mix_g, w_in, w_ret_out, w_sb_out, w_mix_out, norm_mlp_g, w_up, w_down, w_ada, b_ada, final_g, loss_target, m_norm_mix_g, m_w_in, m_w_ret_out, m_w_sb_out, m_w_mix_out, m_norm_mlp_g, m_w_up, m_w_down, m_w_ada, m_b_ada, m_final_g, v_norm_mix_g, v_w_in, v_w_ret_out, v_w_sb_out, v_w_mix_out, v_norm_mlp_g, v_w_up, v_w_down, v_w_ada, v_b_ada, v_final_g):
    S, D = x.shape[1], x.shape[2]
    x0 = x.reshape(S, D)
    tgt = loss_target.reshape(S, D)
    me = 4 * lax.axis_index("x") + 2 * lax.axis_index("y") + lax.axis_index("c")
    wts = dict(w_in=w_in, w_ret_out=w_ret_out, w_sb_out=w_sb_out, w_mix_out=w_mix_out, w_up=w_up, w_down=w_down)
    mts = dict(w_in=m_w_in, w_ret_out=m_w_ret_out, w_sb_out=m_w_sb_out, w_mix_out=m_w_mix_out, w_up=m_w_up, w_down=m_w_down)
    vts = dict(w_in=v_w_in, w_ret_out=v_w_ret_out, w_sb_out=v_w_sb_out, w_mix_out=v_w_mix_out, w_up=v_w_up, w_down=v_w_down)
    rc = _ret_consts(S)
    tri = _tri()

    (cact,) = _ew(lambda t: (_silu(t),), [jnp.pad(c, ((0, 7), (0, 0)))], outs=[('row', D, F32)], name="silu_c")
    (cact_all,) = _all_gather_lead([cact[0:1]], "gather_c")
    cact_all = cact_all.reshape(N_DEV, D)
    cact16 = jnp.pad(cact_all, ((0, 8), (0, 0)))
    n_ada = w_ada.shape[2]
    b_loc = lax.dynamic_slice_in_dim(b_ada, me * n_ada, n_ada, axis=1)
    mods = [_mod_partial(cact16, w_ada[l], b_loc[l:l + 1])[:N_DEV] for l in range(DEPTH)]
    modp = jnp.stack(mods, axis=1)
    (modr,) = _all_to_all_lead([modp], "scatter_mod")
    mod_full = jnp.transpose(modr, (1, 0, 2)).reshape(DEPTH, N_MOD, D)

    shards = {}
    for nm in _W_NAMES:
        w = wts[nm]
        (wb,) = _ew(lambda t: (t,), [w.reshape(-1, w.shape[-1])], outs=[('row', w.shape[-1], BF16)], name="cast_bf16")
        shards[nm] = wb.reshape(w.shape)
    rest = tuple(nm for nm in _W_NAMES if nm != "w_in")
    started, layer_groups = [modr], []
    for l in range(DEPTH):
        sh_l = {nm: shards[nm][l] for nm in _W_NAMES}
        groups = [(("w_in",), "%d_in" % l), (rest, "%d_rest" % l)] if l == 0 else [(_W_NAMES, "%d_all" % l)]
        layer_groups.append([])
        for names, tag in groups:
            layer_groups[-1].append((names, _gather_start(sh_l, names, tag, started[-1:])))
            started.append(layer_groups[-1][-1][1]['token'])
    layer_w = [_LayerWeights(g, started) for g in layer_groups]

    xs = x0
    saved = []
    for l in range(DEPTH):
        xs, sv = _layer_fwd(xs, mod_full[l], norm_mix_g[l:l + 1], norm_mlp_g[l:l + 1], layer_w[l], rc, tri)
        sv['W'] = layer_w[l]
        saved.append(sv)

    fg = final_g.reshape(1, D)

    def head(xt, tg, g):
        r = lax.rsqrt(jnp.mean(xt * xt, axis=-1, keepdims=True) + EPS)
        xh = xt * r
        e = xh * g - tg
        dy = e * (1.0 / D)
        dxh = dy * g
        dx = r * (dxh - xh * jnp.mean(dxh * xh, axis=-1, keepdims=True))
        return dx, dy * xh, 0.5 * e * e * (1.0 / D)

    dxs, d_fg, loss_cols = _ew(head, [xs, tgt], vecs=[fg], outs=[('row', D, F32), ('sum', D), ('sum', D)], name="loss_head")

    small = [None] * DEPTH
    pending = []
    for l in reversed(range(DEPTH)):
        sv = saved[l]

        def emit(gw, tag, l=l):
            names = tuple(gw)
            pending.append((l, names, _scatter_start(gw, names, "%d_%s" % (l, tag))))
            return pending[-1][2]['token']

        dxs, d_mod, d_gn1, d_gn2 = _layer_bwd(dxs, sv, sv['W'], rc, tri, emit)
        small[l] = (d_mod, d_gn1, d_gn2)
    grad_x = dxs.reshape(1, S, D)

    pack = jnp.concatenate([small[l][0] for l in range(DEPTH)] + [small[l][1] for l in range(DEPTH)]
                           + [small[l][2] for l in range(DEPTH)] + [d_fg, loss_cols], axis=1)
    (packs,) = _all_gather_lead([pack], "gather_small")
    packs = packs.reshape(N_DEV, -1)
    o = 0
    dmod_all = []
    for l in range(DEPTH):
        dmod_all.append(packs[:, o:o + N_MOD * D]); o += N_MOD * D
    gn1_parts = packs[:, o:o + DEPTH * D].reshape(N_DEV, DEPTH, D); o += DEPTH * D
    gn2_parts = packs[:, o:o + DEPTH * D].reshape(N_DEV, DEPTH, D); o += DEPTH * D
    fg_parts = packs[:, o:o + D].reshape(N_DEV, 1, D); o += D
    loss_parts = packs[:, o:o + D]
    (loss_sum,) = _ew(lambda t: (t,), [loss_parts], outs=[('sum', D)], name="loss_sum")
    loss = jnp.sum(loss_sum)

    res = {}
    after = [dxs, loss_sum]
    for l, names, h in pending:
        for nm, landed in zip(names, _finish(h, after)):
            res[nm] = _adam_layer(landed, wts[nm], mts[nm], vts[nm], l, res.get(nm), "adam_layer")
        after = [res[names[-1]][0]]
    res["norm_mix_g"] = _adam(gn1_parts, norm_mix_g, m_norm_mix_g, v_norm_mix_g, "adam")
    res["norm_mlp_g"] = _adam(gn2_parts, norm_mlp_g, m_norm_mlp_g, v_norm_mlp_g, "adam")
    fgr = _adam(fg_parts, fg, m_final_g.reshape(1, D), v_final_g.reshape(1, D), "adam")
    res["final_g"] = [t.reshape(D) for t in fgr]
    bparts = jnp.stack(dmod_all, axis=1)
    res["b_ada"] = _adam(bparts, b_ada, m_b_ada, v_b_ada, "adam")
    cact_t = cact_all.T
    for l in range(DEPTH):
        dm_loc = lax.dynamic_slice_in_dim(dmod_all[l], me * n_ada, n_ada, axis=1)
        res["w_ada"] = _adam_layer(_ada_grad(cact_t, dm_loc), w_ada, m_w_ada, v_w_ada, l, res.get("w_ada"),
                                   "adam_layer")

    order = ['norm_mix_g', 'w_in', 'w_ret_out', 'w_sb_out', 'w_mix_out', 'norm_mlp_g', 'w_up', 'w_down', 'w_ada', 'b_ada', 'final_g']
    out = [loss, grad_x]
    for i in range(4):
        out += [res[nm][i] for nm in order]
    return tuple(out)
```

```python
import functools
import math

import jax
import jax.numpy as jnp
import numpy as np
from jax import lax
from jax.experimental import pallas as pl
from jax.experimental.pallas import tpu as pltpu

F32 = jnp.float32
BF16 = jnp.bfloat16

N_DEV = 8
D_MODEL = 1024
DEPTH = 2
RET_HEADS = 4
RET_QK = 256
RET_V = 512
RET_CHUNK = 128
ROPE_BASE = 10000.0
SB_HEADS = 16
SB_DIM = 64
D_FF = 4096
N_MOD = 6
EPS = 1e-6
GN_EPS = 1e-5
O_RQ, O_RK, O_RV, O_RG, O_SQ, O_SK, O_SV, O_GA, O_GB = 0, 1024, 2048, 4096, 6144, 7168, 8192, 9216, 10240
IN_W = 11264

ADAM_LR, ADAM_B1, ADAM_B2, ADAM_EPS, ADAM_WD, ADAM_STEP = 0.001, 0.9, 0.999, 1e-08, 0.01, 10

VMEM_LIMIT = 56 * 1024 * 1024
DW_TK = 2048


def _cparams(sem):
    return pltpu.CompilerParams(dimension_semantics=sem, vmem_limit_bytes=VMEM_LIMIT)


def _mm(a, b, *, ta=False, tb=False, tm=1024, tn=1024, tk=None, a_ex=(), pro=None, o_ex=(), epi=None,
        outs=(F32,), cols=0, name, deps=()):
    a_parts = list(a) if isinstance(a, (list, tuple)) else [a]
    b_parts = list(b) if isinstance(b, (list, tuple)) else [b]
    assert not (ta and len(a_parts) > 1) and not (tb and len(b_parts) > 1)
    if ta:
        K, M = a.shape
    else:
        M, K = a_parts[0].shape[0], sum(t.shape[1] for t in a_parts)
    N = b.shape[0] if tb else sum(t.shape[1] for t in b_parts)
    tm, tn, tk = min(tm, M), min(tn, N), K if tk is None else min(tk, K)
    assert M % tm == 0 and N % tn == 0 and K % tk == 0, (name, M, N, K, tm, tn, tk)
    nk = K // tk
    multi = len(a_parts) > 1 or len(b_parts) > 1

    def ranges(parts, t):
        out, o = [], 0
        for arr in parts:
            assert arr.shape[1] % t == 0
            out.append((o, o + arr.shape[1] // t))
            o += arr.shape[1] // t
        return out

    a_rng = ranges(a_parts, tk) if len(a_parts) > 1 else [(0, nk)]
    b_rng = ranges(b_parts, tn) if len(b_parts) > 1 else [(0, N // tn)]
    clip = lambda v, lo, hi: jnp.clip(v - lo, 0, hi - lo - 1)
    in_specs, args = [], []
    for arr, (lo, hi) in zip(a_parts, a_rng):
        in_specs.append(pl.BlockSpec((tk, tm), lambda i, j, k: (k, i)) if ta
                        else pl.BlockSpec((tm, tk), lambda i, j, k, lo=lo, hi=hi: (i, clip(k, lo, hi))))
        args.append(arr)
    for arr, (lo, hi) in zip(b_parts, b_rng):
        in_specs.append(pl.BlockSpec((tn, tk), lambda i, j, k: (j, k)) if tb
                        else pl.BlockSpec((tk, tn), lambda i, j, k, lo=lo, hi=hi: (k, clip(j, lo, hi))))
        args.append(arr)
    npa, npb = len(a_parts), len(b_parts)
    for arr, kind, *off in a_ex:
        off = off[0] if off else 0
        if kind == 'a' and ta:
            assert off % tm == 0
            in_specs.append(pl.BlockSpec((tk, tm), lambda i, j, k, o=off // tm: (k, o + i)))
        elif kind == 'a':
            assert off % tk == 0
            in_specs.append(pl.BlockSpec((tm, tk), lambda i, j, k, o=off // tk: (i, o + k)))
        elif kind == 'k':
            in_specs.append(pl.BlockSpec((tk, 1), lambda i, j, k: (k, 0)) if ta
                            else pl.BlockSpec((1, tk), lambda i, j, k: (0, k)))
        else:
            in_specs.append(pl.BlockSpec((1, tm), lambda i, j, k: (0, i)) if ta
                            else pl.BlockSpec((tm, 1), lambda i, j, k: (i, 0)))
        args.append(arr)
    for arr, kind, *off in o_ex:
        off = off[0] if off else 0
        if kind == 'o':
            assert off % tn == 0
            in_specs.append(pl.BlockSpec((tm, tn), lambda i, j, k, o=off // tn: (i, o + j)))
        elif kind == 'n':
            in_specs.append(pl.BlockSpec((1, tn), lambda i, j, k: (0, j)))
        else:
            in_specs.append(pl.BlockSpec((tm, 1), lambda i, j, k: (i, 0)))
        args.append(arr)
    for arr in deps:
        in_specs.append(pl.BlockSpec(memory_space=pl.ANY))
        args.append(arr)
    assert cols == 0 or N == tn
    na, no, nout, nd = len(a_ex), len(o_ex), len(outs) + cols, len(deps)
    dims = (((0 if ta else 1,), (1 if tb else 0,)), ((), ()))

    def body(*refs):
        a_refs, b_refs = refs[:npa], refs[npa:npa + npb]
        n0 = npa + npb
        aex = refs[n0:n0 + na]
        oex = refs[n0 + na:n0 + na + no]
        out_refs = refs[n0 + na + no + nd:n0 + na + no + nd + nout]

        def product(a_ref, b_ref):
            at = a_ref[...]
            if pro is not None:
                at = pro(at, *[r[...] for r in aex])
            return lax.dot_general(at.astype(BF16), b_ref[...].astype(BF16), dims, preferred_element_type=F32)

        def finish(res):
            vals = epi(res, *[r[...] for r in oex]) if epi is not None else (res,)
            for o_ref, v in zip(out_refs, vals):
                o_ref[...] = v.astype(o_ref.dtype)

        if nk == 1 and not multi:
            finish(product(a_refs[0], b_refs[0]))
            return
        acc = refs[-1]
        j, k = pl.program_id(1), pl.program_id(2)
        if multi:
            @pl.when(k == 0)
            def _():
                acc[...] = jnp.zeros_like(acc)

            for a_ref, (alo, ahi) in zip(a_refs, a_rng):
                for b_ref, (blo, bhi) in zip(b_refs, b_rng):
                    @pl.when((k >= alo) & (k < ahi) & (j >= blo) & (j < bhi))
                    def _():
                        acc[...] += product(a_ref, b_ref)
        else:
            @pl.when(k == 0)
            def _():
                acc[...] = product(a_refs[0], b_refs[0])

            @pl.when(k > 0)
            def _():
                acc[...] += product(a_refs[0], b_refs[0])

        @pl.when(k == nk - 1)
        def _():
            finish(acc[...])

    res = pl.pallas_call(
        body, name=name, grid=(M // tm, N // tn, nk), in_specs=in_specs,
        out_specs=[pl.BlockSpec((tm, tn), lambda i, j, k: (i, j)) for _ in outs]
        + [pl.BlockSpec((tm, 1), lambda i, j, k: (i, 0))] * cols,
        out_shape=[jax.ShapeDtypeStruct((M, N), dt) for dt in outs] + [jax.ShapeDtypeStruct((M, 1), F32)] * cols,
        scratch_shapes=[pltpu.VMEM((tm, tn), F32)] if nk > 1 or multi else [],
        compiler_params=_cparams(("parallel", "parallel", "arbitrary")),
    )(*args)
    return res


def _ew(fn, rows, vecs=(), cols=(), outs=(), tr=256, name=None):
    S = rows[0].shape[0]
    tr = min(tr, S)
    assert S % tr == 0
    in_specs, args = [], []
    for r in rows:
        in_specs.append(pl.BlockSpec((tr, r.shape[1]), lambda i: (i, 0)))
        args.append(r)
    for v in vecs:
        in_specs.append(pl.BlockSpec((1, v.shape[1]), lambda i: (0, 0)))
        args.append(v)
    for c in cols:
        in_specs.append(pl.BlockSpec((tr, 1), lambda i: (i, 0)))
        args.append(c)
    out_specs, out_shape = [], []
    for o in outs:
        if o[0] == 'row':
            out_specs.append(pl.BlockSpec((tr, o[1]), lambda i: (i, 0)))
            out_shape.append(jax.ShapeDtypeStruct((S, o[1]), o[2]))
        elif o[0] == 'sum':
            out_specs.append(pl.BlockSpec((1, o[1]), lambda i: (0, 0)))
            out_shape.append(jax.ShapeDtypeStruct((1, o[1]), F32))
        else:
            out_specs.append(pl.BlockSpec((tr, 1), lambda i: (i, 0)))
            out_shape.append(jax.ShapeDtypeStruct((S, 1), o[1]))
    nin = len(args)

    def body(*refs):
        i = pl.program_id(0)
        vals = fn(*[r[...] for r in refs[:nin]])
        for o, o_ref, v in zip(outs, refs[nin:], vals):
            if o[0] == 'sum':
                @pl.when(i == 0)
                def _():
                    o_ref[...] = jnp.zeros_like(o_ref)
                o_ref[...] += jnp.sum(v.astype(F32), axis=0, keepdims=True)
            else:
                o_ref[...] = v.astype(o_ref.dtype)

    return pl.pallas_call(
        body, name=name, grid=(S // tr,), in_specs=in_specs, out_specs=out_specs, out_shape=out_shape,
        compiler_params=_cparams(("arbitrary",)),
    )(*args)


def _sigmoid(x):
    return 1.0 / (1.0 + jnp.exp(-x))


def _ret_consts(S):
    h = np.arange(RET_HEADS, dtype=np.float64)
    log_gamma = np.log1p(-np.power(2.0, -5.0 - h))
    idx = np.arange(RET_CHUNK, dtype=np.float64)
    rel = idx[:, None] - idx[None, :]
    decay = np.where(rel >= 0, np.exp(np.maximum(rel, 0.0) * log_gamma[:, None, None]), 0.0)
    xi = np.exp((idx + 1.0) * log_gamma[:, None])[:, :, None]
    zeta = np.exp((RET_CHUNK - 1.0 - idx) * log_gamma[:, None])[:, :, None]
    gamma_c = np.exp(RET_CHUNK * log_gamma)[:, None, None]
    half = RET_QK // 2
    inv_freq = np.power(ROPE_BASE, -np.arange(half, dtype=np.float64) / half).astype(np.float32)
    ang = np.arange(S, dtype=np.float32)[:, None] * inv_freq[None, :]
    f = lambda t: jnp.asarray(t, F32)
    return dict(decay=f(decay), xi=f(xi), zeta=f(zeta), gc=f(gamma_c), cos=f(np.cos(ang)), sin=f(np.sin(ang)))


def _rot(t, cos, sin):
    half = RET_QK // 2
    t1, t2 = t[:, :half], t[:, half:]
    return jnp.concatenate([t1 * cos - t2 * sin, t1 * sin + t2 * cos], axis=-1)


def _rot_inv(t, cos, sin):
    half = RET_QK // 2
    t1, t2 = t[:, :half], t[:, half:]
    return jnp.concatenate([t1 * cos + t2 * sin, t2 * cos - t1 * sin], axis=-1)


_NT = (((1,), (1,)), ((), ()))
_TN = (((0,), (0,)), ((), ()))


def _dot(a, b):
    return jnp.dot(a, b, preferred_element_type=F32)


def _dot_nt(a, b):
    return lax.dot_general(a, b, _NT, preferred_element_type=F32)


def _dot_tn(a, b):
    return lax.dot_general(a, b, _TN, preferred_element_type=F32)


_QW, _VW = RET_HEADS * RET_QK, RET_HEADS * RET_V
_HEADS = range(RET_HEADS)


def _ret_in_specs(C, rev, NC):
    n_of = (lambda n: NC - 1 - n) if rev else (lambda n: n)
    whole3 = lambda n: (0, 0, 0)
    return [
        pl.BlockSpec((C, _QW), lambda n: (n_of(n), O_RQ // _QW)),
        pl.BlockSpec((C, _QW), lambda n: (n_of(n), O_RK // _QW)),
        pl.BlockSpec((C, _VW), lambda n: (n_of(n), O_RV // _VW)),
        pl.BlockSpec((C, RET_QK // 2), lambda n: (n_of(n), 0)),
        pl.BlockSpec((C, RET_QK // 2), lambda n: (n_of(n), 0)),
        pl.BlockSpec((RET_HEADS, C, C), whole3),
        pl.BlockSpec((RET_HEADS, C, 1), whole3),
        pl.BlockSpec((RET_HEADS, C, 1), whole3),
        pl.BlockSpec((RET_HEADS, 1, 1), whole3),
    ]


def _qk_heads(q_ref, k_ref, cos, sin):
    qs, kfs = [], []
    for h in _HEADS:
        cols = slice(h * RET_QK, (h + 1) * RET_QK)
        qs.append(_rot(q_ref[:, cols].astype(F32), cos, sin).astype(BF16))
        kfs.append(_rot(k_ref[:, cols].astype(F32), cos, sin) * (RET_QK ** -0.5))
    return qs, kfs


def _ret_fwd(p, rc):
    S = p.shape[0]
    C = RET_CHUNK
    NC = S // C

    def body(q_ref, k_ref, v_ref, cos_ref, sin_ref, dec_ref, xi_ref, zeta_ref, gc_ref, y_ref, rs_ref, r_acc):
        n = pl.program_id(0)

        @pl.when(n == 0)
        def _():
            r_acc[...] = jnp.zeros_like(r_acc)

        cos, sin = cos_ref[...], sin_ref[...]
        qs, kfs = _qk_heads(q_ref, k_ref, cos, sin)
        vs = [v_ref[:, h * RET_V:(h + 1) * RET_V] for h in _HEADS]
        rbs = [r_acc[h].astype(BF16) for h in _HEADS]
        for h in _HEADS:
            rs_ref[h, 0] = rbs[h]
        ss = [(_dot_nt(qs[h], kfs[h].astype(BF16)) * dec_ref[h]).astype(BF16) for h in _HEADS]
        os = [_dot(ss[h], vs[h]) + _dot(qs[h], rbs[h]) * xi_ref[h] for h in _HEADS]
        for h in _HEADS:
            o = os[h]
            mu = jnp.mean(o, axis=-1, keepdims=True)
            var = jnp.mean(jnp.square(o - mu), axis=-1, keepdims=True)
            y_ref[:, h * RET_V:(h + 1) * RET_V] = ((o - mu) * lax.rsqrt(var + GN_EPS)).astype(y_ref.dtype)
        for h in _HEADS:
            kz = (kfs[h] * zeta_ref[h]).astype(BF16)
            r_acc[h] = r_acc[h] * gc_ref[h] + _dot_tn(kz, vs[h])

    return pl.pallas_call(
        body, name="ret_fwd", grid=(NC,), in_specs=_ret_in_specs(C, False, NC),
        out_specs=[pl.BlockSpec((C, _VW), lambda n: (n, 0)),
                   pl.BlockSpec((RET_HEADS, 1, RET_QK, RET_V), lambda n: (0, n, 0, 0))],
        out_shape=[jax.ShapeDtypeStruct((S, _VW), BF16),
                   jax.ShapeDtypeStruct((RET_HEADS, NC, RET_QK, RET_V), BF16)],
        scratch_shapes=[pltpu.VMEM((RET_HEADS, RET_QK, RET_V), F32)],
        compiler_params=_cparams(("arbitrary",)),
    )(p, p, p, rc['cos'], rc['sin'], rc['decay'], rc['xi'], rc['zeta'], rc['gc'])


def _ret_bwd(p, rstate, dy, rc):
    S = p.shape[0]
    C = RET_CHUNK
    NC = S // C

    def body(q_ref, k_ref, v_ref, cos_ref, sin_ref, dec_ref, xi_ref, zeta_ref, gc_ref, rs_ref, dy_ref,
             dq_ref, dk_ref, dv_ref, dr_acc):
        t = pl.program_id(0)

        @pl.when(t == 0)
        def _():
            dr_acc[...] = jnp.zeros_like(dr_acc)

        cos, sin = cos_ref[...], sin_ref[...]
        qs, kfs = _qk_heads(q_ref, k_ref, cos, sin)
        ks = [kf.astype(BF16) for kf in kfs]
        vs = [v_ref[:, h * RET_V:(h + 1) * RET_V] for h in _HEADS]
        rbs = [rs_ref[h, 0] for h in _HEADS]
        ss = [(_dot_nt(qs[h], ks[h]) * dec_ref[h]).astype(BF16) for h in _HEADS]
        os = [_dot(ss[h], vs[h]) + _dot(qs[h], rbs[h]) * xi_ref[h] for h in _HEADS]
        dobs, doxis = [], []
        for h in _HEADS:
            o = os[h]
            mu = jnp.mean(o, axis=-1, keepdims=True)
            var = jnp.mean(jnp.square(o - mu), axis=-1, keepdims=True)
            rstd = lax.rsqrt(var + GN_EPS)
            yh = (o - mu) * rstd
            dyf = dy_ref[:, h * RET_V:(h + 1) * RET_V].astype(F32)
            do = (dyf - jnp.mean(dyf, axis=-1, keepdims=True)
                  - yh * jnp.mean(dyf * yh, axis=-1, keepdims=True)) * rstd
            dobs.append(do.astype(BF16))
            doxis.append((do * xi_ref[h]).astype(BF16))
        drbs = [dr_acc[h].astype(BF16) for h in _HEADS]
        dss = [(_dot_nt(dobs[h], vs[h]) * dec_ref[h]).astype(BF16) for h in _HEADS]
        for h in _HEADS:
            dq = _dot(dss[h], ks[h]) + _dot_nt(doxis[h], rbs[h])
            dq_ref[:, h * RET_QK:(h + 1) * RET_QK] = _rot_inv(dq, cos, sin).astype(dq_ref.dtype)
        for h in _HEADS:
            dk = _dot_tn(dss[h], qs[h]) + _dot_nt(vs[h], drbs[h]) * zeta_ref[h]
            dk_ref[:, h * RET_QK:(h + 1) * RET_QK] = (_rot_inv(dk, cos, sin) * (RET_QK ** -0.5)).astype(dk_ref.dtype)
        for h in _HEADS:
            kz = (kfs[h] * zeta_ref[h]).astype(BF16)
            dv = _dot_tn(ss[h], dobs[h]) + _dot(kz, drbs[h])
            dv_ref[:, h * RET_V:(h + 1) * RET_V] = dv.astype(dv_ref.dtype)
        for h in _HEADS:
            dr_acc[h] = dr_acc[h] * gc_ref[h] + _dot_tn(qs[h], doxis[h])

    rn = lambda n: NC - 1 - n
    in_specs = _ret_in_specs(C, True, NC) + [
        pl.BlockSpec((RET_HEADS, 1, RET_QK, RET_V), lambda n: (0, rn(n), 0, 0)),
        pl.BlockSpec((C, _VW), lambda n: (rn(n), 0)),
    ]
    return pl.pallas_call(
        body, name="ret_bwd", grid=(NC,), in_specs=in_specs,
        out_specs=[pl.BlockSpec((C, _QW), lambda n: (rn(n), 0)),
                   pl.BlockSpec((C, _QW), lambda n: (rn(n), 0)),
                   pl.BlockSpec((C, _VW), lambda n: (rn(n), 0))],
        out_shape=[jax.ShapeDtypeStruct((S, _QW), BF16),
                   jax.ShapeDtypeStruct((S, _QW), BF16),
                   jax.ShapeDtypeStruct((S, _VW), BF16)],
        scratch_shapes=[pltpu.VMEM((RET_HEADS, RET_QK, RET_V), F32)],
        compiler_params=_cparams(("arbitrary",)),
    )(p, p, p, rc['cos'], rc['sin'], rc['decay'], rc['xi'], rc['zeta'], rc['gc'], rstate, dy)


SB_T = 256
SB_SCALE = SB_DIM ** -0.5


def _tri():
    j = np.arange(SB_T)
    after = (j[:, None] > j[None, :]).astype(np.float32)
    upto = (j[:, None] <= j[None, :]).astype(np.float32)
    return jnp.asarray(np.stack([after, upto]), BF16)


def _softplus_parts(z):
    neg_abs = lax.bitcast_convert_type(lax.bitcast_convert_type(z, jnp.uint32) | jnp.uint32(0x80000000), F32)
    e = jnp.exp(neg_abs)
    return jnp.maximum(z, 0.0) + jnp.log(1.0 + e), e


def _sb_fwd(p, tri):
    S = p.shape[0]
    T = min(SB_T, S)
    NQ = S // T
    assert NQ <= 128
    qb, kb, vb = O_SQ // 128, O_SK // 128, O_SV // 128

    def body(q_ref, k_ref, v_ref, tri_ref, o_ref, cs_ref, o_acc, run, zbuf, abuf):
        i = pl.program_id(1)
        lane = lax.broadcasted_iota(jnp.int32, (1, 128), 1)
        tri_after = tri_ref[0]
        qs = [jnp.where((lane >= 64) if hh else (lane < 64), q_ref[...], jnp.zeros_like(q_ref[...]))
              * jnp.asarray(SB_SCALE, BF16) for hh in range(2)]
        cs_ref[...] = jnp.zeros_like(cs_ref)
        o_acc[...] = jnp.zeros_like(o_acc)
        run[...] = jnp.zeros_like(run)

        def kv(ref, j):
            return ref[pl.ds(pl.multiple_of(j * T, T), T), :]

        for hh in range(2):
            zbuf[hh] = _dot_nt(qs[hh], kv(k_ref, i))

        def block(t, diagonal):
            j = i - t
            if diagonal:
                msk = lax.broadcasted_iota(jnp.int32, (T, T), 1) < lax.broadcasted_iota(jnp.int32, (T, T), 0)
            if not diagonal:
                av = [_dot(abuf[hh], kv(v_ref, j + 1)) for hh in range(2)]
            lss, exs, tot, zn = [], [], [], []
            for hh in range(2):
                z = zbuf[hh]
                sp, _ = _softplus_parts(z)
                lss.append(z - sp)
                if diagonal:
                    sp = jnp.where(msk, sp, 0.0)
                exs.append(_dot(sp.astype(BF16), tri_after))
                tot.append(sp[:, 0:1])
                zn.append(_dot_nt(qs[hh], kv(k_ref, jnp.maximum(j - 1, 0))))
            for hh in range(2):
                csl = slice(hh * 128, (hh + 1) * 128)
                cs = run[hh]
                a = jnp.exp(lss[hh] - exs[hh] - cs)
                if diagonal:
                    a = jnp.where(msk, a, 0.0)
                abuf[hh] = a.astype(BF16)
                cs_ref[:, csl] = jnp.where(lane == j, cs, cs_ref[:, csl])
                run[hh] = cs + exs[hh][:, 0:1] + tot[hh]
            for hh in range(2):
                if not diagonal:
                    o_acc[hh] += av[hh]
                zbuf[hh] = zn[hh]

        block(0, True)

        def step(t, carry):
            block(t, False)
            return carry

        lax.fori_loop(1, i + 1, step, 0)
        o_ref[...] = jnp.where(lane < 64, o_acc[0] + _dot(abuf[0], kv(v_ref, 0)),
                               o_acc[1] + _dot(abuf[1], kv(v_ref, 0))).astype(o_ref.dtype)

    return pl.pallas_call(
        body, name="sb_fwd", grid=(SB_HEADS // 2, NQ),
        scratch_shapes=[pltpu.VMEM((2, T, 128), F32), pltpu.VMEM((2, T, 1), F32), pltpu.VMEM((2, T, T), F32),
                        pltpu.VMEM((2, T, T), BF16)],
        in_specs=[pl.BlockSpec((T, 128), lambda h, i: (i, qb + h)),
                  pl.BlockSpec((S, 128), lambda h, i: (0, kb + h)),
                  pl.BlockSpec((S, 128), lambda h, i: (0, vb + h)),
                  pl.BlockSpec((1, T, T), lambda h, i: (0, 0, 0))],
        out_specs=[pl.BlockSpec((T, 128), lambda h, i: (i, h)),
                   pl.BlockSpec((T, 256), lambda h, i: (i, h))],
        out_shape=[jax.ShapeDtypeStruct((S, SB_HEADS * SB_DIM), BF16),
                   jax.ShapeDtypeStruct((S, SB_HEADS * 128), F32)],
        compiler_params=_cparams(("parallel", "arbitrary")),
    )(p, p, p, tri)


def _sb_bwd(p, carries, dy, tri):
    S = p.shape[0]
    T = min(SB_T, S)
    NQ = S // T
    qb, kb, vb = O_SQ // 128, O_SK // 128, O_SV // 128

    def body(q_ref, k_ref, v_ref, cs_ref, dy_ref, tri_ref, dq_ref, dk_ref, dv_ref, dk_acc, dv_acc, dq_acc, run,
             zbuf, dabuf, dzbuf, abuf):
        i = pl.program_id(1)

        @pl.when(i == 0)
        def _():
            dk_acc[...] = jnp.zeros_like(dk_acc)
            dv_acc[...] = jnp.zeros_like(dv_acc)

        lane = lax.broadcasted_iota(jnp.int32, (1, 128), 1)
        tri_after, tri_upto = tri_ref[0], tri_ref[1]
        hms = [(lane >= 64) if hh else (lane < 64) for hh in range(2)]
        qs = [jnp.where(hm, q_ref[...], jnp.zeros_like(q_ref[...])) * jnp.asarray(SB_SCALE, BF16) for hm in hms]
        dos = [jnp.where(hm, dy_ref[...], jnp.zeros_like(dy_ref[...])) for hm in hms]
        qst = [t.T for t in qs]
        dost = [t.T for t in dos]
        dq_acc[...] = jnp.zeros_like(dq_acc)
        run[...] = jnp.zeros_like(run)
        dzbuf[...] = jnp.zeros_like(dzbuf)
        abuf[...] = jnp.zeros_like(abuf)

        def kv(ref, j):
            return ref[pl.ds(pl.multiple_of(j * T, T), T), :]

        def flush(jp):
            kp = kv(k_ref, jp)
            dq_add = [_dot(dzbuf[hh], kp) for hh in range(2)]
            dk_add = _dot(qst[0], dzbuf[0]) + _dot(qst[1], dzbuf[1])
            dv_add = _dot(dost[0], abuf[0]) + _dot(dost[1], abuf[1])
            return dq_add, dk_add, dv_add

        def apply(jp, adds):
            dq_add, dk_add, dv_add = adds
            cols = pl.ds(pl.multiple_of(jp * T, T), T)
            for hh in range(2):
                dq_acc[hh] += dq_add[hh]
            dk_acc[:, cols] += dk_add
            dv_acc[:, cols] += dv_add

        for hh in range(2):
            zbuf[hh] = _dot_nt(qs[hh], kv(k_ref, 0))
            dabuf[hh] = _dot_nt(dos[hh], kv(v_ref, 0))

        def block(j, diagonal):
            jp = jnp.maximum(j - 1, 0)
            if diagonal:
                msk = lax.broadcasted_iota(jnp.int32, (T, T), 1) < lax.broadcasted_iota(jnp.int32, (T, T), 0)
            kp = kv(k_ref, jp)
            dq_add = [_dot(dzbuf[hh], kp) for hh in range(2)]
            sigs, lss, exs, zn, dan, dk_part, dv_part = [], [], [], [], [], [], []
            for hh in range(2):
                z = zbuf[hh]
                sp, _ = _softplus_parts(z)
                lss.append(z - sp)
                sigs.append(jnp.exp(lss[hh]))
                if diagonal:
                    sp = jnp.where(msk, sp, 0.0)
                exs.append(_dot(sp.astype(BF16), tri_after))
                if not diagonal:
                    zn.append(_dot_nt(qs[hh], kv(k_ref, j + 1)))
                dk_part.append(_dot(qst[hh], dzbuf[hh]))
            pgs, gs = [], []
            for hh in range(2):
                csl = slice(hh * 128, (hh + 1) * 128)
                cs = jnp.sum(jnp.where(lane == j, cs_ref[:, csl], 0.0), axis=-1, keepdims=True)
                a = jnp.exp(lss[hh] - exs[hh] - cs)
                if diagonal:
                    a = jnp.where(msk, a, 0.0)
                abuf_new = a.astype(BF16)
                g = a * dabuf[hh]
                gs.append((g, abuf_new))
                pgs.append(_dot(g.astype(BF16), tri_upto))
                if not diagonal:
                    dan.append(_dot_nt(dos[hh], kv(v_ref, j + 1)))
                dv_part.append(_dot(dost[hh], abuf[hh]))
            adds = (dq_add, dk_part[0] + dk_part[1], dv_part[0] + dv_part[1])
            for hh in range(2):
                g, abuf_new = gs[hh]
                cg = run[hh]
                dz = g - sigs[hh] * (cg + pgs[hh])
                if diagonal:
                    dz = jnp.where(msk, dz, 0.0)
                run[hh] = cg + pgs[hh][:, T - 1:T]
                dzbuf[hh] = dz.astype(BF16)
                abuf[hh] = abuf_new
            apply(jp, adds)
            if not diagonal:
                for hh in range(2):
                    zbuf[hh] = zn[hh]
                    dabuf[hh] = dan[hh]

        def step(j, carry):
            block(j, False)
            return carry

        lax.fori_loop(0, i, step, 0)
        block(i, True)
        apply(i, flush(i))
        dq_ref[...] = (jnp.where(lane < 64, dq_acc[0], dq_acc[1]) * SB_SCALE).astype(dq_ref.dtype)

        @pl.when(i == NQ - 1)
        def _():
            dk_ref[...] = dk_acc[...].T.astype(dk_ref.dtype)
            dv_ref[...] = dv_acc[...].T.astype(dv_ref.dtype)

    W = SB_HEADS * SB_DIM
    return pl.pallas_call(
        body, name="sb_bwd", grid=(SB_HEADS // 2, NQ),
        in_specs=[pl.BlockSpec((T, 128), lambda h, i: (i, qb + h)),
                  pl.BlockSpec((S, 128), lambda h, i: (0, kb + h)),
                  pl.BlockSpec((S, 128), lambda h, i: (0, vb + h)),
                  pl.BlockSpec((T, 256), lambda h, i: (i, h)),
                  pl.BlockSpec((T, 128), lambda h, i: (i, h)),
                  pl.BlockSpec((2, T, T), lambda h, i: (0, 0, 0))],
        out_specs=[pl.BlockSpec((T, 128), lambda h, i: (i, h)),
                   pl.BlockSpec((S, 128), lambda h, i: (0, h)),
                   pl.BlockSpec((S, 128), lambda h, i: (0, h))],
        out_shape=[jax.ShapeDtypeStruct((S, W), BF16)] * 3,
        scratch_shapes=[pltpu.VMEM((128, S), F32), pltpu.VMEM((128, S), F32), pltpu.VMEM((2, T, 128), F32),
                        pltpu.VMEM((2, T, 1), F32), pltpu.VMEM((2, T, T), F32), pltpu.VMEM((2, T, T), F32),
                        pltpu.VMEM((2, T, T), BF16), pltpu.VMEM((2, T, T), BF16)],
        compiler_params=_cparams(("parallel", "arbitrary")),
    )(p, p, p, carries, dy, tri)


def _exchange(srcs, out_shapes, src_slice, dst_slice, name):
    n = len(srcs)

    def body(*refs):
        ins, outs = refs[:n], refs[n:2 * n]
        send_sems, recv_sems, loc_sems = refs[2 * n:]
        x, y, c = lax.axis_index("x"), lax.axis_index("y"), lax.axis_index("c")
        me = 4 * x + 2 * y + c
        local = [pltpu.make_async_copy(src_slice(t, ins[t], me), dst_slice(t, outs[t], me), loc_sems.at[t])
                 for t in range(n)]
        for cp in local:
            cp.start()
        sends, recvs = [], []
        for k in (1, 2, 4, 6, 3, 5, 7):
            px = 1 - x if k & 4 else x
            py = 1 - y if k & 2 else y
            pc = 1 - c if k & 1 else c
            peer = 4 * px + 2 * py + pc
            for t in range(n):
                s = t * 7 + k - 1
                sends.append(pltpu.make_async_remote_copy(
                    src_ref=src_slice(t, ins[t], peer), dst_ref=dst_slice(t, outs[t], me),
                    send_sem=send_sems.at[s], recv_sem=recv_sems.at[s],
                    device_id=(px, py, pc), device_id_type=pl.DeviceIdType.MESH))
                recvs.append(pltpu.make_async_remote_copy(
                    src_ref=src_slice(t, ins[t], me), dst_ref=dst_slice(t, outs[t], peer),
                    send_sem=send_sems.at[s], recv_sem=recv_sems.at[s],
                    device_id=(px, py, pc), device_id_type=pl.DeviceIdType.MESH))
        for cp in sends:
            cp.start()
        for cp in recvs:
            cp.wait_recv()
        for cp in sends:
            cp.wait_send()
        for cp in local:
            cp.wait()

    anyspec = pl.BlockSpec(memory_space=pl.ANY)
    return pl.pallas_call(
        body, name=name, in_specs=[anyspec] * n, out_specs=[anyspec] * n,
        out_shape=[jax.ShapeDtypeStruct(s, d) for s, d in out_shapes],
        scratch_shapes=[pltpu.SemaphoreType.DMA((7 * n,)), pltpu.SemaphoreType.DMA((7 * n,)),
                        pltpu.SemaphoreType.DMA((n,))],
    )(*srcs)


def _all_gather_lead(xs, name):
    return _exchange(
        xs, [((N_DEV,) + x.shape, x.dtype) for x in xs],
        lambda t, ref, peer: ref, lambda t, ref, who: ref.at[who], name)


def _all_to_all_lead(xs, name):
    return _exchange(
        xs, [(x.shape, x.dtype) for x in xs],
        lambda t, ref, peer: ref.at[peer], lambda t, ref, who: ref.at[who], name)


_W_AXIS = {"w_in": 1, "w_ret_out": 0, "w_sb_out": 0, "w_mix_out": 0, "w_up": 1, "w_down": 0}
_W_NAMES = tuple(_W_AXIS)


def _window(ref, axis, who, width, count=1):
    start = pl.multiple_of(who * width, width)
    return ref.at[pl.ds(start, count * width), :] if axis == 0 else ref.at[:, pl.ds(start, count * width)]


_HBM = pl.BlockSpec(memory_space=pltpu.HBM)
_SEM = pl.BlockSpec(memory_space=pltpu.SEMAPHORE)
_EFFECT = pltpu.SideEffectType.DATAFLOW_SIDE_EFFECTING


def _exchange_start(srcs, shapes, src_slice, dst_slice, name, deps=()):
    n, nd = len(srcs), len(deps)
    lands = [pltpu.with_memory_space_constraint(lax.empty(s, d), pltpu.HBM) for s, d in shapes]

    def body(*refs):
        ins, lnd = refs[:n], refs[n:2 * n]
        sems = refs[2 * n + nd:4 * n + nd]
        token = refs[6 * n + nd]
        x, y, c = lax.axis_index("x"), lax.axis_index("y"), lax.axis_index("c")
        me = 4 * x + 2 * y + c
        for k in (0, 1, 2, 4, 6, 3, 5, 7):
            px = 1 - x if k & 4 else x
            py = 1 - y if k & 2 else y
            pc = 1 - c if k & 1 else c
            peer = 4 * px + 2 * py + pc
            for t in range(n):
                pltpu.make_async_remote_copy(
                    src_ref=src_slice(t, ins[t], peer), dst_ref=dst_slice(t, lnd[t], me),
                    send_sem=sems[2 * t], recv_sem=sems[2 * t + 1],
                    device_id=(px, py, pc), device_id_type=pl.DeviceIdType.MESH).start()
        token[...] = jnp.zeros_like(token)

    res = pl.pallas_call(
        body, name=name, in_specs=[_HBM] * (2 * n) + [pl.BlockSpec(memory_space=pl.ANY)] * nd,
        out_specs=[_SEM] * (2 * n) + [_HBM] * (2 * n) + [pl.BlockSpec(memory_space=pltpu.VMEM)],
        out_shape=[pltpu.SemaphoreType.DMA(())] * (2 * n) + [pltpu.HBM(s.shape, s.dtype) for s in srcs]
        + [pltpu.HBM(s.shape, s.dtype) for s in lands] + [jax.ShapeDtypeStruct((8, 128), F32)],
        input_output_aliases={t: 2 * n + t for t in range(2 * n)},
        compiler_params=pltpu.CompilerParams(has_side_effects=_EFFECT),
    )(*[pltpu.with_memory_space_constraint(s, pltpu.HBM) for s in srcs], *lands, *deps)
    return dict(n=n, sems=res[:2 * n], srcs=res[2 * n:3 * n], lands=res[3 * n:4 * n], token=res[4 * n])


def _exchange_wait(h, after, name):
    n = h['n']

    def body(*refs):
        lnd = refs[n:2 * n]
        sems = refs[2 * n:4 * n]
        x, y, c = lax.axis_index("x"), lax.axis_index("y"), lax.axis_index("c")
        for t in range(n):
            w = lnd[t]
            cp = pltpu.make_async_remote_copy(src_ref=w, dst_ref=w, send_sem=sems[2 * t], recv_sem=sems[2 * t + 1],
                                              device_id=(x, y, 1 - c), device_id_type=pl.DeviceIdType.MESH)
            cp.wait_send()
            cp.wait_recv()

    after = list(after)
    res = pl.pallas_call(
        body, name=name,
        in_specs=[_HBM] * (2 * n) + [_SEM] * (2 * n) + [pl.BlockSpec(memory_space=pl.ANY)] * len(after),
        out_specs=[_HBM] * (2 * n),
        out_shape=[pltpu.HBM(s.shape, s.dtype) for s in h['srcs']] + [pltpu.HBM(s.shape, s.dtype) for s in h['lands']],
        input_output_aliases={t: t for t in range(2 * n)},
        compiler_params=pltpu.CompilerParams(has_side_effects=_EFFECT),
    )(*h['srcs'], *h['lands'], *h['sems'], *after)
    return list(res[n:])


def _gather_start(shards, names, tag, deps=()):
    xs = [shards[nm] for nm in names]
    axes = [_W_AXIS[nm] for nm in names]
    widths = [x.shape[ax] for x, ax in zip(xs, axes)]
    shapes = [(tuple(d * (N_DEV if a == ax else 1) for a, d in enumerate(x.shape)), x.dtype) for x, ax in zip(xs, axes)]
    src = lambda t, ref, peer: ref
    dst = lambda t, ref, who: _window(ref, axes[t], who, widths[t])
    h = _exchange_start(xs, shapes, src, dst, "gw_start_" + tag, deps)
    h['tag'] = "gw_wait_" + tag
    return h


def _scatter_start(grads, names, tag):
    xs = [grads[nm] for nm in names]
    axes = [_W_AXIS[nm] for nm in names]
    widths = [x.shape[ax] // N_DEV for x, ax in zip(xs, axes)]
    shapes = [((N_DEV,) + tuple(d // (N_DEV if a == ax else 1) for a, d in enumerate(x.shape)), x.dtype)
              for x, ax in zip(xs, axes)]
    src = lambda t, ref, peer: _window(ref, axes[t], peer, widths[t])
    dst = lambda t, ref, who: ref.at[who]
    h = _exchange_start(xs, shapes, src, dst, "sg_start_" + tag)
    h['tag'] = "sg_wait_" + tag
    return h


def _finish(h, after):
    return _exchange_wait(h, after, h['tag'])


class _LayerWeights:
    def __init__(self, groups, started):
        self.groups = groups
        self.started = started
        self.got = {}
        self.after = None

    def __getitem__(self, nm):
        if nm not in self.got:
            for names, h in self.groups:
                if nm in names:
                    self.got.update(zip(names, _finish(h, list(self.after) + self.started)))
        return self.got[nm]


def _adam_math(p_ref, w, m, v):
    g = p_ref[0].astype(F32)
    for s in range(1, p_ref.shape[0]):
        g = g + p_ref[s].astype(F32)
    bc1 = 1.0 / (1.0 - ADAM_B1 ** ADAM_STEP)
    bc2 = 1.0 / (1.0 - ADAM_B2 ** ADAM_STEP)
    mm = ADAM_B1 * m + (1.0 - ADAM_B1) * g
    vv = ADAM_B2 * v + (1.0 - ADAM_B2) * jnp.square(g)
    return g, -ADAM_LR * ((mm * bc1) / (jnp.sqrt(vv * bc2) + ADAM_EPS) + ADAM_WD * w), mm, vv


def _adam(parts, w, m, v, name, tr=256):
    P, R, C = parts.shape
    tr = min(tr, R)
    assert R % tr == 0

    def body(p_ref, w_ref, m_ref, v_ref, *outs):
        for o_ref, val in zip(outs, _adam_math(p_ref, w_ref[...], m_ref[...], v_ref[...])):
            o_ref[...] = val

    spec = pl.BlockSpec((tr, C), lambda i: (i, 0))
    return pl.pallas_call(
        body, name=name, grid=(R // tr,),
        in_specs=[pl.BlockSpec((P, tr, C), lambda i: (0, i, 0)), spec, spec, spec],
        out_specs=[spec] * 4, out_shape=[jax.ShapeDtypeStruct((R, C), F32)] * 4,
        compiler_params=_cparams(("parallel",)),
    )(parts, w, m, v)


def _adam_layer(parts, w, m, v, l, prev, name, tr=256):
    P, R, C = parts.shape
    tr = min(tr, R)
    assert R % tr == 0 and w.shape == (DEPTH, R, C)
    npv = 0 if prev is None else 4

    def body(p_ref, w_ref, m_ref, v_ref, *rest):
        for o_ref, val in zip(rest[npv:], _adam_math(p_ref, w_ref[0], m_ref[0], v_ref[0])):
            o_ref[0] = val

    spec = pl.BlockSpec((1, tr, C), lambda i: (l, i, 0))
    return pl.pallas_call(
        body, name=name, grid=(R // tr,),
        in_specs=[pl.BlockSpec((P, tr, C), lambda i: (0, i, 0)), spec, spec, spec]
        + [pl.BlockSpec(memory_space=pl.ANY)] * npv,
        out_specs=[spec] * 4, out_shape=[jax.ShapeDtypeStruct((DEPTH, R, C), F32)] * 4,
        input_output_aliases={4 + t: t for t in range(npv)},
        compiler_params=_cparams(("parallel",)),
    )(parts, w, m, v, *([] if prev is None else prev))


def _mod_partial(cact_all, w_ada_l, b_ada_l):
    def body(c_ref, w_ref, b_ref, o_ref):
        o_ref[...] = _dot(c_ref[...].astype(BF16), w_ref[...].astype(BF16)) + b_ref[...]

    return pl.pallas_call(
        body, name="mod_partial", out_shape=jax.ShapeDtypeStruct((cact_all.shape[0], w_ada_l.shape[1]), F32),
        compiler_params=pltpu.CompilerParams(vmem_limit_bytes=VMEM_LIMIT),
    )(cact_all, w_ada_l, b_ada_l)


def _ada_grad(cact_t, dmod):
    D, n = cact_t.shape[0], dmod.shape[1]

    def body(c_ref, d_ref, o_ref):
        ct = c_ref[...].astype(BF16).astype(F32)
        dm = d_ref[...].astype(BF16).astype(F32)
        acc = ct[:, 0:1] * dm[0:1, :]
        for b in range(1, N_DEV):
            acc = acc + ct[:, b:b + 1] * dm[b:b + 1, :]
        o_ref[0] = acc

    return pl.pallas_call(
        body, name="ada_grad", out_shape=jax.ShapeDtypeStruct((1, D, n), F32),
        compiler_params=pltpu.CompilerParams(vmem_limit_bytes=VMEM_LIMIT),
    )(cact_t, dmod)


def _norm_mod(x, r, gv, sh):
    return x * r * gv + sh


def _silu(x):
    return x * _sigmoid(x)


def _rstd(x):
    return lax.rsqrt(jnp.mean(x * x, axis=-1, keepdims=True) + EPS)


def _residual_epi(acc, x, g):
    xn = x + g * acc
    return acc, xn, _rstd(xn)


def _layer_fwd(x0, r1, mod, gn1, gn2, W, rc, tri):
    S = x0.shape[0]
    sh1, sc1, g1m, sh2, sc2, g2m = [mod[i:i + 1] for i in range(N_MOD)]
    gv1 = gn1 * (1.0 + sc1)
    gv2 = gn2 * (1.0 + sc2)
    W.after = [r1, gv1]
    (p,) = _mm(x0, W["w_in"], a_ex=[(r1, 'm'), (gv1, 'k'), (sh1, 'k')], pro=_norm_mod, outs=(BF16,), name="mm_in")
    yret, rstate = _ret_fwd(p, rc)
    ysb, sbc = _sb_fwd(p, tri)
    W.after = [ysb]
    (ya,) = _mm(yret, W["w_ret_out"], tm=512, a_ex=[(p, 'a', O_RG)],
                pro=lambda yr, g: _silu(g.astype(F32)) * yr.astype(F32), outs=(BF16,), name="mm_ret_out")
    yb, mg = _mm(ysb, W["w_sb_out"], o_ex=[(ya, 'o'), (p, 'o', O_GA), (p, 'o', O_GB)],
                 epi=lambda acc, a, ga, gb: (acc, _sigmoid(ga.astype(F32)) * a.astype(F32)
                                             + _sigmoid(gb.astype(F32)) * acc),
                 outs=(BF16, BF16), name="mm_sb_out")
    mo, x1, r2 = _mm(mg, W["w_mix_out"], o_ex=[(x0, 'o'), (g1m, 'n')], epi=_residual_epi, outs=(BF16, F32), cols=1,
                     name="mm_mix_out")
    (act,) = _mm(x1, W["w_up"], a_ex=[(r2, 'm'), (gv2, 'k'), (sh2, 'k')], pro=_norm_mod,
                 epi=lambda acc: (jnp.maximum(acc, 0.0),), outs=(BF16,), name="mm_up")
    dn, x2, r_out = _mm(act, W["w_down"], tm=512, pro=lambda a: a * a, o_ex=[(x1, 'o'), (g2m, 'n')],
                        epi=_residual_epi, outs=(BF16, F32), cols=1, name="mm_down")
    saved = dict(x0=x0, r1=r1, p=p, yret=yret, rstate=rstate, ysb=ysb, sbc=sbc, ya=ya, yb=yb, mg=mg, mo=mo, x1=x1, r2=r2,
                 act=act, dn=dn, gv1=gv1, gv2=gv2, mod=mod, gn1=gn1, gn2=gn2)
    return x2, r_out, saved


def _norm_bwd(dh, x, r, dres, gv, gn, extra_rows=(), extra_vecs=(), extra_fn=None, extra_outs=(), name="norm_bwd"):
    D = x.shape[1]
    ne = len(extra_rows)

    def fn(dh_t, x_t, dres_t, *rest):
        er, rest = rest[:ne], rest[ne:]
        gv_t = rest[0]
        ev, r_t = rest[1:-1], rest[-1]
        xh = x_t * r_t
        dxh = dh_t * gv_t
        dx = r_t * (dxh - xh * jnp.mean(dxh * xh, axis=-1, keepdims=True)) + dres_t
        base = (dx, dh_t, dh_t * xh)
        if extra_fn is None:
            return base
        return base + tuple(extra_fn(dx, *er, *ev))

    return _ew(fn, [dh, x, dres] + list(extra_rows), vecs=[gv] + list(extra_vecs), cols=[r],
               outs=[('row', D, F32), ('sum', D), ('sum', D)] + list(extra_outs), name=name)


def _gate_bwd(dx, dn, g):
    return dx * dn.astype(F32), dx * g


_GATE_OUTS = [('sum', D_MODEL), ('row', D_MODEL, BF16)]


def _layer_bwd(dx2, d_g2m, d_dn, sv, below, W, rc, tri, emit):
    mod = sv['mod']
    sh1, sc1, g1m, sh2, sc2, g2m = [mod[i:i + 1] for i in range(N_MOD)]
    D = D_MODEL
    p = sv['p']
    (d_up,) = _mm(d_dn, W["w_down"], tb=True, o_ex=[(sv['act'], 'o')],
                  epi=lambda acc, a: (acc * 2.0 * a.astype(F32),), outs=(BF16,), name="mm_down_dx")
    (gw_down,) = _mm(sv['act'], d_dn, ta=True, tk=DW_TK, pro=lambda a: a * a, outs=(BF16,), name="mm_down_dw")
    (gw_up,) = _mm(sv['x1'], d_up, ta=True, tm=512, tk=DW_TK,
                   a_ex=[(sv['r2'], 'k'), (sv['gv2'].reshape(1, D), 'm'), (sh2, 'm')], pro=_norm_mod,
                   outs=(BF16,), name="mm_up_dw")
    tok = emit(dict(w_down=gw_down, w_up=gw_up), "mlp")
    (d_h2,) = _mm(d_up, W["w_up"], tb=True, tk=2048, outs=(F32,), name="mm_up_dx", deps=[tok])
    dx1, d_sh2, s_h2, d_g1m, d_mo = _norm_bwd(
        d_h2, sv['x1'], sv['r2'], dx2, sv['gv2'], sv['gn2'],
        extra_rows=[sv['mo']], extra_vecs=[g1m],
        extra_fn=lambda dx, mo, g: (dx * mo.astype(F32), dx * g),
        extra_outs=[('sum', D), ('row', D, BF16)], name="norm_bwd_mlp")
    d_sc2 = sv['gn2'] * s_h2
    d_gn2 = (1.0 + sc2) * s_h2
    def mix_epi(acc, ya, yb, ga, gb):
        sa, sb = _sigmoid(ga.astype(F32)), _sigmoid(gb.astype(F32))
        return (acc * sa, acc * sb, acc * ya.astype(F32) * sa * (1.0 - sa), acc * yb.astype(F32) * sb * (1.0 - sb))

    d_ya, d_yb, d_ga, d_gb = _mm(d_mo, W["w_mix_out"], tb=True, tm=512,
                                 o_ex=[(sv['ya'], 'o'), (sv['yb'], 'o'), (p, 'o', O_GA), (p, 'o', O_GB)], epi=mix_epi,
                                 outs=(BF16,) * 4, name="mm_mix_dx")
    (gw_mix,) = _mm(sv['mg'], d_mo, ta=True, tk=DW_TK, outs=(BF16,), name="mm_mix_dw")

    def ro_epi(acc, g, yr):
        gf = g.astype(F32)
        s = _sigmoid(gf)
        return (acc * yr.astype(F32) * s * (1.0 + gf * (1.0 - s)), acc * gf * s)

    d_rg, d_yret = _mm(d_ya, W["w_ret_out"], tb=True, tm=512, o_ex=[(p, 'o', O_RG), (sv['yret'], 'o')], epi=ro_epi,
                       outs=(BF16, BF16), name="mm_ret_dx")
    (gw_ro,) = _mm(sv['yret'], d_ya, ta=True, tm=512, tk=DW_TK, a_ex=[(p, 'a', O_RG)],
                   pro=lambda yr, g: _silu(g.astype(F32)) * yr.astype(F32), outs=(BF16,), name="mm_ret_dw")
    (gw_so,) = _mm(sv['ysb'], d_yb, ta=True, tk=DW_TK, outs=(BF16,), name="mm_sb_dw")
    tok = emit(dict(w_mix_out=gw_mix, w_ret_out=gw_ro, w_sb_out=gw_so), "mix")
    (d_ysb,) = _mm(d_yb, W["w_sb_out"], tb=True, outs=(BF16,), name="mm_sb_dx", deps=[tok])
    d_sq, d_sk, d_sv = _sb_bwd(p, sv['sbc'], d_ysb, tri)
    d_rq, d_rk, d_rv = _ret_bwd(p, sv['rstate'], d_yret, rc)
    dp = [d_rq, d_rk, d_rv, d_rg, d_sq, d_sk, d_sv, d_ga, d_gb]
    (gw_in,) = _mm(sv['x0'], dp, ta=True, tm=512, tk=512,
                   a_ex=[(sv['r1'], 'k'), (sv['gv1'].reshape(1, D), 'm'), (sh1, 'm')], pro=_norm_mod,
                   outs=(BF16,), name="mm_in_dw")
    tok = emit(dict(w_in=gw_in), "in")
    (d_h,) = _mm(dp, W["w_in"], tb=True, tm=512, tk=1024, outs=(F32,), name="mm_in_dx", deps=[tok])
    if below is None:
        dx0, d_sh1, s_h1 = _norm_bwd(d_h, sv['x0'], sv['r1'], dx1, sv['gv1'], sv['gn1'], name="norm_bwd_mix")
        gate_below = (None, None)
    else:
        dx0, d_sh1, s_h1, *gate_below = _norm_bwd(
            d_h, sv['x0'], sv['r1'], dx1, sv['gv1'], sv['gn1'], extra_rows=[below['dn']],
            extra_vecs=[below['mod'][N_MOD - 1:N_MOD]], extra_fn=_gate_bwd, extra_outs=_GATE_OUTS,
            name="norm_bwd_mix_gate")
    d_sc1 = sv['gn1'] * s_h1
    d_gn1 = (1.0 + sc1) * s_h1
    d_mod = jnp.concatenate([d_sh1, d_sc1, d_g1m, d_sh2, d_sc2, d_g2m], axis=1)
    return dx0, gate_below, d_mod, d_gn1, d_gn2


def kernel(x, c, norm_mix_g, w_in, w_ret_out, w_sb_out, w_mix_out, norm_mlp_g, w_up, w_down, w_ada, b_ada, final_g, loss_target, m_norm_mix_g, m_w_in, m_w_ret_out, m_w_sb_out, m_w_mix_out, m_norm_mlp_g, m_w_up, m_w_down, m_w_ada, m_b_ada, m_final_g, v_norm_mix_g, v_w_in, v_w_ret_out, v_w_sb_out, v_w_mix_out, v_norm_mlp_g, v_w_up, v_w_down, v_w_ada, v_b_ada, v_final_g):
    S, D = x.shape[1], x.shape[2]
    x0 = x.reshape(S, D)
    tgt = loss_target.reshape(S, D)
    me = 4 * lax.axis_index("x") + 2 * lax.axis_index("y") + lax.axis_index("c")
    wts = dict(w_in=w_in, w_ret_out=w_ret_out, w_sb_out=w_sb_out, w_mix_out=w_mix_out, w_up=w_up, w_down=w_down)
    mts = dict(w_in=m_w_in, w_ret_out=m_w_ret_out, w_sb_out=m_w_sb_out, w_mix_out=m_w_mix_out, w_up=m_w_up, w_down=m_w_down)
    vts = dict(w_in=v_w_in, w_ret_out=v_w_ret_out, w_sb_out=v_w_sb_out, w_mix_out=v_w_mix_out, w_up=v_w_up, w_down=v_w_down)
    rc = _ret_consts(S)
    tri = _tri()

    (cact,) = _ew(lambda t: (_silu(t),), [jnp.pad(c, ((0, 7), (0, 0)))], outs=[('row', D, F32)], name="silu_c")
    (cact_all,) = _all_gather_lead([cact[0:1]], "gather_c")
    cact_all = cact_all.reshape(N_DEV, D)
    cact16 = jnp.pad(cact_all, ((0, 8), (0, 0)))
    n_ada = w_ada.shape[2]
    b_loc = lax.dynamic_slice_in_dim(b_ada, me * n_ada, n_ada, axis=1)
    mods = [_mod_partial(cact16, w_ada[l], b_loc[l:l + 1])[:N_DEV] for l in range(DEPTH)]
    modp = jnp.stack(mods, axis=1)
    (modr,) = _all_to_all_lead([modp], "scatter_mod")
    mod_full = jnp.transpose(modr, (1, 0, 2)).reshape(DEPTH, N_MOD, D)

    shards = {}
    for nm in _W_NAMES:
        w = wts[nm]
        (wb,) = _ew(lambda t: (t,), [w.reshape(-1, w.shape[-1])], outs=[('row', w.shape[-1], BF16)], name="cast_bf16")
        shards[nm] = wb.reshape(w.shape)
    rest = tuple(nm for nm in _W_NAMES if nm != "w_in")
    started, layer_groups = [modr], []
    for l in range(DEPTH):
        sh_l = {nm: shards[nm][l] for nm in _W_NAMES}
        groups = [(("w_in",), "%d_in" % l), (rest, "%d_rest" % l)] if l == 0 else [(_W_NAMES, "%d_all" % l)]
        layer_groups.append([])
        for names, tag in groups:
            layer_groups[-1].append((names, _gather_start(sh_l, names, tag, started[-1:])))
            started.append(layer_groups[-1][-1][1]['token'])
    layer_w = [_LayerWeights(g, started) for g in layer_groups]

    xs = x0
    (rs,) = _ew(lambda t: (_rstd(t),), [x0], outs=[('col', F32)], name="row_rstd")
    saved = []
    for l in range(DEPTH):
        xs, rs, sv = _layer_fwd(xs, rs, mod_full[l], norm_mix_g[l:l + 1], norm_mlp_g[l:l + 1], layer_w[l], rc, tri)
        sv['W'] = layer_w[l]
        saved.append(sv)

    fg = final_g.reshape(1, D)

    def head(xt, tg, dn, g, g2m, r):
        xh = xt * r
        e = xh * g - tg
        dy = e * (1.0 / D)
        dxh = dy * g
        dx = r * (dxh - xh * jnp.mean(dxh * xh, axis=-1, keepdims=True))
        return (dx, dy * xh, 0.5 * e * e * (1.0 / D)) + _gate_bwd(dx, dn, g2m)

    top = saved[DEPTH - 1]
    dxs, d_fg, loss_cols, *gate = _ew(head, [xs, tgt, top['dn']], vecs=[fg, top['mod'][N_MOD - 1:N_MOD]], cols=[rs],
                                      outs=[('row', D, F32), ('sum', D), ('sum', D)] + _GATE_OUTS, name="loss_head")

    small = [None] * DEPTH
    pending = []
    for l in reversed(range(DEPTH)):
        sv = saved[l]

        def emit(gw, tag, l=l):
            names = tuple(gw)
            pending.append((l, names, _scatter_start(gw, names, "%d_%s" % (l, tag))))
            return pending[-1][2]['token']

        dxs, gate, d_mod, d_gn1, d_gn2 = _layer_bwd(dxs, gate[0], gate[1], sv, saved[l - 1] if l else None,
                                                    sv['W'], rc, tri, emit)
        small[l] = (d_mod, d_gn1, d_gn2)
    grad_x = dxs.reshape(1, S, D)

    pack = jnp.concatenate([small[l][0] for l in range(DEPTH)] + [small[l][1] for l in range(DEPTH)]
                           + [small[l][2] for l in range(DEPTH)] + [d_fg, loss_cols], axis=1)
    (packs,) = _all_gather_lead([pack], "gather_small")
    packs = packs.reshape(N_DEV, -1)
    o = 0
    dmod_all = []
    for l in range(DEPTH):
        dmod_all.append(packs[:, o:o + N_MOD * D]); o += N_MOD * D
    gn1_parts = packs[:, o:o + DEPTH * D].reshape(N_DEV, DEPTH, D); o += DEPTH * D
    gn2_parts = packs[:, o:o + DEPTH * D].reshape(N_DEV, DEPTH, D); o += DEPTH * D
    fg_parts = packs[:, o:o + D].reshape(N_DEV, 1, D); o += D
    loss_parts = packs[:, o:o + D]
    (loss_sum,) = _ew(lambda t: (t,), [loss_parts], outs=[('sum', D)], name="loss_sum")
    loss = jnp.sum(loss_sum)

    res = {}
    after = [dxs, loss_sum]
    for l, names, h in pending:
        for nm, landed in zip(names, _finish(h, after)):
            res[nm] = _adam_layer(landed, wts[nm], mts[nm], vts[nm], l, res.get(nm), "adam_layer")
        after = [res[names[-1]][0]]
    res["norm_mix_g"] = _adam(gn1_parts, norm_mix_g, m_norm_mix_g, v_norm_mix_g, "adam")
    res["norm_mlp_g"] = _adam(gn2_parts, norm_mlp_g, m_norm_mlp_g, v_norm_mlp_g, "adam")
    fgr = _adam(fg_parts, fg, m_final_g.reshape(1, D), v_final_g.reshape(1, D), "adam")
    res["final_g"] = [t.reshape(D) for t in fgr]
    bparts = jnp.stack(dmod_all, axis=1)
    res["b_ada"] = _adam(bparts, b_ada, m_b_ada, v_b_ada, "adam")
    cact_t = cact_all.T
    for l in range(DEPTH):
        dm_loc = lax.dynamic_slice_in_dim(dmod_all[l], me * n_ada, n_ada, axis=1)
        res["w_ada"] = _adam_layer(_ada_grad(cact_t, dm_loc), w_ada, m_w_ada, v_w_ada, l, res.get("w_ada"),
                                   "adam_layer")

    order = ['norm_mix_g', 'w_in', 'w_ret_out', 'w_sb_out', 'w_mix_out', 'norm_mlp_g', 'w_up', 'w_down', 'w_ada', 'b_ada', 'final_g']
    out = [loss, grad_x]
    for i in range(4):
        out += [res[nm][i] for nm in order]
    return tuple(out)
```

```python
import functools
import math

import jax
import jax.numpy as jnp
import numpy as np
from jax import lax
from jax.experimental import pallas as pl
from jax.experimental.pallas import tpu as pltpu

F32 = jnp.float32
BF16 = jnp.bfloat16

N_DEV = 8
D_MODEL = 1024
DEPTH = 2
RET_HEADS = 4
RET_QK = 256
RET_V = 512
RET_CHUNK = 128
ROPE_BASE = 10000.0
SB_HEADS = 16
SB_DIM = 64
D_FF = 4096
N_MOD = 6
EPS = 1e-6
GN_EPS = 1e-5
O_RQ, O_RK, O_RV, O_RG, O_SQ, O_SK, O_SV, O_GA, O_GB = 0, 1024, 2048, 4096, 6144, 7168, 8192, 9216, 10240
IN_W = 11264

ADAM_LR, ADAM_B1, ADAM_B2, ADAM_EPS, ADAM_WD, ADAM_STEP = 0.001, 0.9, 0.999, 1e-08, 0.01, 10

VMEM_LIMIT = 56 * 1024 * 1024
DW_TK = 2048


def _cparams(sem):
    return pltpu.CompilerParams(dimension_semantics=sem, vmem_limit_bytes=VMEM_LIMIT)


def _mm(a, b, *, ta=False, tb=False, tm=1024, tn=1024, tk=None, a_ex=(), pro=None, o_ex=(), epi=None,
        outs=(F32,), cols=0, name, deps=()):
    a_parts = list(a) if isinstance(a, (list, tuple)) else [a]
    b_parts = list(b) if isinstance(b, (list, tuple)) else [b]
    assert not (ta and len(a_parts) > 1) and not (tb and len(b_parts) > 1)
    if ta:
        K, M = a.shape
    else:
        M, K = a_parts[0].shape[0], sum(t.shape[1] for t in a_parts)
    N = b.shape[0] if tb else sum(t.shape[1] for t in b_parts)
    tm, tn, tk = min(tm, M), min(tn, N), K if tk is None else min(tk, K)
    assert M % tm == 0 and N % tn == 0 and K % tk == 0, (name, M, N, K, tm, tn, tk)
    nk = K // tk
    multi = len(a_parts) > 1 or len(b_parts) > 1

    def ranges(parts, t):
        out, o = [], 0
        for arr in parts:
            assert arr.shape[1] % t == 0
            out.append((o, o + arr.shape[1] // t))
            o += arr.shape[1] // t
        return out

    a_rng = ranges(a_parts, tk) if len(a_parts) > 1 else [(0, nk)]
    b_rng = ranges(b_parts, tn) if len(b_parts) > 1 else [(0, N // tn)]
    clip = lambda v, lo, hi: jnp.clip(v - lo, 0, hi - lo - 1)
    mine = lambda v, lo, hi, w: jnp.where((v >= lo) & (v < hi), w, 0)
    in_specs, args = [], []
    for arr, (lo, hi) in zip(a_parts, a_rng):
        in_specs.append(pl.BlockSpec((tk, tm), lambda i, j, k: (k, i)) if ta
                        else pl.BlockSpec((tm, tk), lambda i, j, k, lo=lo, hi=hi: (i, clip(k, lo, hi))))
        args.append(arr)
    for arr, (lo, hi) in zip(b_parts, b_rng):
        in_specs.append(pl.BlockSpec((tn, tk), lambda i, j, k: (j, k)) if tb
                        else pl.BlockSpec((tk, tn), lambda i, j, k, lo=lo, hi=hi: (mine(j, lo, hi, k), clip(j, lo, hi))))
        args.append(arr)
    npa, npb = len(a_parts), len(b_parts)
    for arr, kind, *off in a_ex:
        off = off[0] if off else 0
        if kind == 'a' and ta:
            assert off % tm == 0
            in_specs.append(pl.BlockSpec((tk, tm), lambda i, j, k, o=off // tm: (k, o + i)))
        elif kind == 'a':
            assert off % tk == 0
            in_specs.append(pl.BlockSpec((tm, tk), lambda i, j, k, o=off // tk: (i, o + k)))
        elif kind == 'k':
            in_specs.append(pl.BlockSpec((tk, 1), lambda i, j, k: (k, 0)) if ta
                            else pl.BlockSpec((1, tk), lambda i, j, k: (0, k)))
        else:
            in_specs.append(pl.BlockSpec((1, tm), lambda i, j, k: (0, i)) if ta
                            else pl.BlockSpec((tm, 1), lambda i, j, k: (i, 0)))
        args.append(arr)
    for arr, kind, *off in o_ex:
        off = off[0] if off else 0
        if kind == 'o':
            assert off % tn == 0
            in_specs.append(pl.BlockSpec((tm, tn), lambda i, j, k, o=off // tn: (i, o + j)))
        elif kind == 'n':
            in_specs.append(pl.BlockSpec((1, tn), lambda i, j, k: (0, j)))
        else:
            in_specs.append(pl.BlockSpec((tm, 1), lambda i, j, k: (i, 0)))
        args.append(arr)
    for arr in deps:
        in_specs.append(pl.BlockSpec(memory_space=pl.ANY))
        args.append(arr)
    assert cols == 0 or N == tn
    na, no, nout, nd = len(a_ex), len(o_ex), len(outs) + cols, len(deps)
    dims = (((0 if ta else 1,), (1 if tb else 0,)), ((), ()))

    def body(*refs):
        a_refs, b_refs = refs[:npa], refs[npa:npa + npb]
        n0 = npa + npb
        aex = refs[n0:n0 + na]
        oex = refs[n0 + na:n0 + na + no]
        out_refs = refs[n0 + na + no + nd:n0 + na + no + nd + nout]

        def product(a_ref, b_ref):
            at = a_ref[...]
            if pro is not None:
                at = pro(at, *[r[...] for r in aex])
            return lax.dot_general(at.astype(BF16), b_ref[...].astype(BF16), dims, preferred_element_type=F32)

        def finish(res):
            vals = epi(res, *[r[...] for r in oex]) if epi is not None else (res,)
            for o_ref, v in zip(out_refs, vals):
                o_ref[...] = v.astype(o_ref.dtype)

        if nk == 1 and not multi:
            finish(product(a_refs[0], b_refs[0]))
            return
        acc = refs[-1]
        j, k = pl.program_id(1), pl.program_id(2)
        if multi:
            @pl.when(k == 0)
            def _():
                acc[...] = jnp.zeros_like(acc)

            for a_ref, (alo, ahi) in zip(a_refs, a_rng):
                for b_ref, (blo, bhi) in zip(b_refs, b_rng):
                    @pl.when((k >= alo) & (k < ahi) & (j >= blo) & (j < bhi))
                    def _():
                        acc[...] += product(a_ref, b_ref)
        else:
            @pl.when(k == 0)
            def _():
                acc[...] = product(a_refs[0], b_refs[0])

            @pl.when(k > 0)
            def _():
                acc[...] += product(a_refs[0], b_refs[0])

        @pl.when(k == nk - 1)
        def _():
            finish(acc[...])

    res = pl.pallas_call(
        body, name=name, grid=(M // tm, N // tn, nk), in_specs=in_specs,
        out_specs=[pl.BlockSpec((tm, tn), lambda i, j, k: (i, j)) for _ in outs]
        + [pl.BlockSpec((tm, 1), lambda i, j, k: (i, 0))] * cols,
        out_shape=[jax.ShapeDtypeStruct((M, N), dt) for dt in outs] + [jax.ShapeDtypeStruct((M, 1), F32)] * cols,
        scratch_shapes=[pltpu.VMEM((tm, tn), F32)] if nk > 1 or multi else [],
        compiler_params=_cparams(("parallel", "parallel", "arbitrary")),
    )(*args)
    return res


def _ew(fn, rows, vecs=(), cols=(), outs=(), tr=256, name=None):
    S = rows[0].shape[0]
    tr = min(tr, S)
    assert S % tr == 0
    in_specs, args = [], []
    for r in rows:
        in_specs.append(pl.BlockSpec((tr, r.shape[1]), lambda i: (i, 0)))
        args.append(r)
    for v in vecs:
        in_specs.append(pl.BlockSpec((1, v.shape[1]), lambda i: (0, 0)))
        args.append(v)
    for c in cols:
        in_specs.append(pl.BlockSpec((tr, 1), lambda i: (i, 0)))
        args.append(c)
    out_specs, out_shape = [], []
    for o in outs:
        if o[0] == 'row':
            out_specs.append(pl.BlockSpec((tr, o[1]), lambda i: (i, 0)))
            out_shape.append(jax.ShapeDtypeStruct((S, o[1]), o[2]))
        elif o[0] == 'sum':
            out_specs.append(pl.BlockSpec((1, o[1]), lambda i: (0, 0)))
            out_shape.append(jax.ShapeDtypeStruct((1, o[1]), F32))
        else:
            out_specs.append(pl.BlockSpec((tr, 1), lambda i: (i, 0)))
            out_shape.append(jax.ShapeDtypeStruct((S, 1), o[1]))
    nin = len(args)

    def body(*refs):
        i = pl.program_id(0)
        vals = fn(*[r[...] for r in refs[:nin]])
        for o, o_ref, v in zip(outs, refs[nin:], vals):
            if o[0] == 'sum':
                @pl.when(i == 0)
                def _():
                    o_ref[...] = jnp.zeros_like(o_ref)
                o_ref[...] += jnp.sum(v.astype(F32), axis=0, keepdims=True)
            else:
                o_ref[...] = v.astype(o_ref.dtype)

    return pl.pallas_call(
        body, name=name, grid=(S // tr,), in_specs=in_specs, out_specs=out_specs, out_shape=out_shape,
        compiler_params=_cparams(("arbitrary",)),
    )(*args)


def _sigmoid(x):
    return 1.0 / (1.0 + jnp.exp(-x))


def _ret_consts(S):
    h = np.arange(RET_HEADS, dtype=np.float64)
    log_gamma = np.log1p(-np.power(2.0, -5.0 - h))
    idx = np.arange(RET_CHUNK, dtype=np.float64)
    rel = idx[:, None] - idx[None, :]
    decay = np.where(rel >= 0, np.exp(np.maximum(rel, 0.0) * log_gamma[:, None, None]), 0.0)
    xi = np.exp((idx + 1.0) * log_gamma[:, None])[:, :, None]
    zeta = np.exp((RET_CHUNK - 1.0 - idx) * log_gamma[:, None])[:, :, None]
    gamma_c = np.exp(RET_CHUNK * log_gamma)[:, None, None]
    half = RET_QK // 2
    inv_freq = np.power(ROPE_BASE, -np.arange(half, dtype=np.float64) / half).astype(np.float32)
    ang = np.arange(S, dtype=np.float32)[:, None] * inv_freq[None, :]
    f = lambda t: jnp.asarray(t, F32)
    return dict(decay=f(decay), xi=f(xi), zeta=f(zeta), gc=f(gamma_c), cos=f(np.cos(ang)), sin=f(np.sin(ang)))


def _rot(t, cos, sin):
    half = RET_QK // 2
    t1, t2 = t[:, :half], t[:, half:]
    return jnp.concatenate([t1 * cos - t2 * sin, t1 * sin + t2 * cos], axis=-1)


def _rot_inv(t, cos, sin):
    half = RET_QK // 2
    t1, t2 = t[:, :half], t[:, half:]
    return jnp.concatenate([t1 * cos + t2 * sin, t2 * cos - t1 * sin], axis=-1)


_NT = (((1,), (1,)), ((), ()))
_TN = (((0,), (0,)), ((), ()))


def _dot(a, b):
    return jnp.dot(a, b, preferred_element_type=F32)


def _dot_nt(a, b):
    return lax.dot_general(a, b, _NT, preferred_element_type=F32)


def _dot_tn(a, b):
    return lax.dot_general(a, b, _TN, preferred_element_type=F32)


_QW, _VW = RET_HEADS * RET_QK, RET_HEADS * RET_V
_HEADS = range(RET_HEADS)


def _ret_in_specs(C, rev, NC):
    n_of = (lambda n: NC - 1 - n) if rev else (lambda n: n)
    whole3 = lambda n: (0, 0, 0)
    return [
        pl.BlockSpec((C, _QW), lambda n: (n_of(n), O_RQ // _QW)),
        pl.BlockSpec((C, _QW), lambda n: (n_of(n), O_RK // _QW)),
        pl.BlockSpec((C, _VW), lambda n: (n_of(n), O_RV // _VW)),
        pl.BlockSpec((C, RET_QK // 2), lambda n: (n_of(n), 0)),
        pl.BlockSpec((C, RET_QK // 2), lambda n: (n_of(n), 0)),
        pl.BlockSpec((RET_HEADS, C, C), whole3),
        pl.BlockSpec((RET_HEADS, C, 1), whole3),
        pl.BlockSpec((RET_HEADS, C, 1), whole3),
        pl.BlockSpec((RET_HEADS, 1, 1), whole3),
    ]


def _qk_heads(q_ref, k_ref, cos, sin):
    qs, kfs = [], []
    for h in _HEADS:
        cols = slice(h * RET_QK, (h + 1) * RET_QK)
        qs.append(_rot(q_ref[:, cols].astype(F32), cos, sin).astype(BF16))
        kfs.append(_rot(k_ref[:, cols].astype(F32), cos, sin) * (RET_QK ** -0.5))
    return qs, kfs


def _ret_fwd(p, rc):
    S = p.shape[0]
    C = RET_CHUNK
    NC = S // C

    def body(q_ref, k_ref, v_ref, cos_ref, sin_ref, dec_ref, xi_ref, zeta_ref, gc_ref, y_ref, rs_ref, r_acc):
        n = pl.program_id(0)

        @pl.when(n == 0)
        def _():
            r_acc[...] = jnp.zeros_like(r_acc)

        cos, sin = cos_ref[...], sin_ref[...]
        qs, kfs = _qk_heads(q_ref, k_ref, cos, sin)
        vs = [v_ref[:, h * RET_V:(h + 1) * RET_V] for h in _HEADS]
        rbs = [r_acc[h].astype(BF16) for h in _HEADS]
        for h in _HEADS:
            rs_ref[h, 0] = rbs[h]
        ss = [(_dot_nt(qs[h], kfs[h].astype(BF16)) * dec_ref[h]).astype(BF16) for h in _HEADS]
        os = [_dot(ss[h], vs[h]) + _dot(qs[h], rbs[h]) * xi_ref[h] for h in _HEADS]
        for h in _HEADS:
            o = os[h]
            mu = jnp.mean(o, axis=-1, keepdims=True)
            var = jnp.mean(jnp.square(o - mu), axis=-1, keepdims=True)
            y_ref[:, h * RET_V:(h + 1) * RET_V] = ((o - mu) * lax.rsqrt(var + GN_EPS)).astype(y_ref.dtype)
        for h in _HEADS:
            kz = (kfs[h] * zeta_ref[h]).astype(BF16)
            r_acc[h] = r_acc[h] * gc_ref[h] + _dot_tn(kz, vs[h])

    return pl.pallas_call(
        body, name="ret_fwd", grid=(NC,), in_specs=_ret_in_specs(C, False, NC),
        out_specs=[pl.BlockSpec((C, _VW), lambda n: (n, 0)),
                   pl.BlockSpec((RET_HEADS, 1, RET_QK, RET_V), lambda n: (0, n, 0, 0))],
        out_shape=[jax.ShapeDtypeStruct((S, _VW), BF16),
                   jax.ShapeDtypeStruct((RET_HEADS, NC, RET_QK, RET_V), BF16)],
        scratch_shapes=[pltpu.VMEM((RET_HEADS, RET_QK, RET_V), F32)],
        compiler_params=_cparams(("arbitrary",)),
    )(p, p, p, rc['cos'], rc['sin'], rc['decay'], rc['xi'], rc['zeta'], rc['gc'])


def _ret_bwd(p, rstate, dy, rc):
    S = p.shape[0]
    C = RET_CHUNK
    NC = S // C

    def body(q_ref, k_ref, v_ref, cos_ref, sin_ref, dec_ref, xi_ref, zeta_ref, gc_ref, rs_ref, dy_ref,
             d_ref, dr_acc):
        dq_ref, dk_ref, dv_ref = d_ref.at[:, 0:_QW], d_ref.at[:, _QW:2 * _QW], d_ref.at[:, 2 * _QW:2 * _QW + _VW]
        t = pl.program_id(0)

        @pl.when(t == 0)
        def _():
            dr_acc[...] = jnp.zeros_like(dr_acc)

        cos, sin = cos_ref[...], sin_ref[...]
        qs, kfs = _qk_heads(q_ref, k_ref, cos, sin)
        ks = [kf.astype(BF16) for kf in kfs]
        vs = [v_ref[:, h * RET_V:(h + 1) * RET_V] for h in _HEADS]
        rbs = [rs_ref[h, 0] for h in _HEADS]
        ss = [(_dot_nt(qs[h], ks[h]) * dec_ref[h]).astype(BF16) for h in _HEADS]
        os = [_dot(ss[h], vs[h]) + _dot(qs[h], rbs[h]) * xi_ref[h] for h in _HEADS]
        dobs, doxis = [], []
        for h in _HEADS:
            o = os[h]
            mu = jnp.mean(o, axis=-1, keepdims=True)
            var = jnp.mean(jnp.square(o - mu), axis=-1, keepdims=True)
            rstd = lax.rsqrt(var + GN_EPS)
            yh = (o - mu) * rstd
            dyf = dy_ref[:, h * RET_V:(h + 1) * RET_V].astype(F32)
            do = (dyf - jnp.mean(dyf, axis=-1, keepdims=True)
                  - yh * jnp.mean(dyf * yh, axis=-1, keepdims=True)) * rstd
            dobs.append(do.astype(BF16))
            doxis.append((do * xi_ref[h]).astype(BF16))
        drbs = [dr_acc[h].astype(BF16) for h in _HEADS]
        dss = [(_dot_nt(dobs[h], vs[h]) * dec_ref[h]).astype(BF16) for h in _HEADS]
        for h in _HEADS:
            dq = _dot(dss[h], ks[h]) + _dot_nt(doxis[h], rbs[h])
            dq_ref[:, h * RET_QK:(h + 1) * RET_QK] = _rot_inv(dq, cos, sin).astype(dq_ref.dtype)
        for h in _HEADS:
            dk = _dot_tn(dss[h], qs[h]) + _dot_nt(vs[h], drbs[h]) * zeta_ref[h]
            dk_ref[:, h * RET_QK:(h + 1) * RET_QK] = (_rot_inv(dk, cos, sin) * (RET_QK ** -0.5)).astype(dk_ref.dtype)
        for h in _HEADS:
            kz = (kfs[h] * zeta_ref[h]).astype(BF16)
            dv = _dot_tn(ss[h], dobs[h]) + _dot(kz, drbs[h])
            dv_ref[:, h * RET_V:(h + 1) * RET_V] = dv.astype(dv_ref.dtype)
        for h in _HEADS:
            dr_acc[h] = dr_acc[h] * gc_ref[h] + _dot_tn(qs[h], doxis[h])

    rn = lambda n: NC - 1 - n
    in_specs = _ret_in_specs(C, True, NC) + [
        pl.BlockSpec((RET_HEADS, 1, RET_QK, RET_V), lambda n: (0, rn(n), 0, 0)),
        pl.BlockSpec((C, _VW), lambda n: (rn(n), 0)),
    ]
    return pl.pallas_call(
        body, name="ret_bwd", grid=(NC,), in_specs=in_specs,
        out_specs=pl.BlockSpec((C, 2 * _QW + _VW), lambda n: (rn(n), 0)),
        out_shape=jax.ShapeDtypeStruct((S, 2 * _QW + _VW), BF16),
        scratch_shapes=[pltpu.VMEM((RET_HEADS, RET_QK, RET_V), F32)],
        compiler_params=_cparams(("arbitrary",)),
    )(p, p, p, rc['cos'], rc['sin'], rc['decay'], rc['xi'], rc['zeta'], rc['gc'], rstate, dy)


SB_T = 256
SB_SCALE = SB_DIM ** -0.5


def _tri():
    j = np.arange(SB_T)
    after = (j[:, None] > j[None, :]).astype(np.float32)
    upto = (j[:, None] <= j[None, :]).astype(np.float32)
    return jnp.asarray(np.stack([after, upto]), BF16)


def _softplus_parts(z):
    neg_abs = lax.bitcast_convert_type(lax.bitcast_convert_type(z, jnp.uint32) | jnp.uint32(0x80000000), F32)
    e = jnp.exp(neg_abs)
    return jnp.maximum(z, 0.0) + jnp.log(1.0 + e), e


def _sb_fwd(p, tri):
    S = p.shape[0]
    T = min(SB_T, S)
    NQ = S // T
    assert NQ <= 128
    qb, kb, vb = O_SQ // 128, O_SK // 128, O_SV // 128

    def body(q_ref, k_ref, v_ref, tri_ref, o_ref, cs_ref, o_acc, run, zbuf, abuf):
        i = pl.program_id(1)
        lane = lax.broadcasted_iota(jnp.int32, (1, 128), 1)
        tri_after = tri_ref[0]
        qs = [jnp.where((lane >= 64) if hh else (lane < 64), q_ref[...], jnp.zeros_like(q_ref[...]))
              * jnp.asarray(SB_SCALE, BF16) for hh in range(2)]
        cs_ref[...] = jnp.zeros_like(cs_ref)
        o_acc[...] = jnp.zeros_like(o_acc)
        run[...] = jnp.zeros_like(run)

        def kv(ref, j):
            return ref[pl.ds(pl.multiple_of(j * T, T), T), :]

        for hh in range(2):
            zbuf[hh] = _dot_nt(qs[hh], kv(k_ref, i))

        def block(t, diagonal):
            j = i - t
            if diagonal:
                msk = lax.broadcasted_iota(jnp.int32, (T, T), 1) < lax.broadcasted_iota(jnp.int32, (T, T), 0)
            if not diagonal:
                av = [_dot(abuf[hh], kv(v_ref, j + 1)) for hh in range(2)]
            lss, exs, tot, zn = [], [], [], []
            for hh in range(2):
                z = zbuf[hh]
                sp, _ = _softplus_parts(z)
                lss.append(z - sp)
                if diagonal:
                    sp = jnp.where(msk, sp, 0.0)
                exs.append(_dot(sp.astype(BF16), tri_after))
                tot.append(sp[:, 0:1])
                zn.append(_dot_nt(qs[hh], kv(k_ref, jnp.maximum(j - 1, 0))))
            for hh in range(2):
                csl = slice(hh * 128, (hh + 1) * 128)
                cs = run[hh]
                a = jnp.exp(lss[hh] - exs[hh] - cs)
                if diagonal:
                    a = jnp.where(msk, a, 0.0)
                abuf[hh] = a.astype(BF16)
                cs_ref[:, csl] = jnp.where(lane == j, cs, cs_ref[:, csl])
                run[hh] = cs + exs[hh][:, 0:1] + tot[hh]
            for hh in range(2):
                if not diagonal:
                    o_acc[hh] += av[hh]
                zbuf[hh] = zn[hh]

        block(0, True)

        def step(t, carry):
            block(t, False)
            return carry

        lax.fori_loop(1, i + 1, step, 0)
        o_ref[...] = jnp.where(lane < 64, o_acc[0] + _dot(abuf[0], kv(v_ref, 0)),
                               o_acc[1] + _dot(abuf[1], kv(v_ref, 0))).astype(o_ref.dtype)

    return pl.pallas_call(
        body, name="sb_fwd", grid=(SB_HEADS // 2, NQ),
        scratch_shapes=[pltpu.VMEM((2, T, 128), F32), pltpu.VMEM((2, T, 1), F32), pltpu.VMEM((2, T, T), F32),
                        pltpu.VMEM((2, T, T), BF16)],
        in_specs=[pl.BlockSpec((T, 128), lambda h, i: (i, qb + h)),
                  pl.BlockSpec((S, 128), lambda h, i: (0, kb + h)),
                  pl.BlockSpec((S, 128), lambda h, i: (0, vb + h)),
                  pl.BlockSpec((1, T, T), lambda h, i: (0, 0, 0))],
        out_specs=[pl.BlockSpec((T, 128), lambda h, i: (i, h)),
                   pl.BlockSpec((T, 256), lambda h, i: (i, h))],
        out_shape=[jax.ShapeDtypeStruct((S, SB_HEADS * SB_DIM), BF16),
                   jax.ShapeDtypeStruct((S, SB_HEADS * 128), F32)],
        compiler_params=_cparams(("parallel", "arbitrary")),
    )(p, p, p, tri)


def _sb_bwd(p, carries, dy, tri):
    S = p.shape[0]
    T = min(SB_T, S)
    NQ = S // T
    qb, kb, vb = O_SQ // 128, O_SK // 128, O_SV // 128

    def body(q_ref, k_ref, v_ref, cs_ref, dy_ref, tri_ref, dq_ref, dk_ref, dv_ref, dk_acc, dv_acc, dq_acc, run,
             zbuf, dabuf, dzbuf, abuf):
        i = pl.program_id(1)

        @pl.when(i == 0)
        def _():
            dk_acc[...] = jnp.zeros_like(dk_acc)
            dv_acc[...] = jnp.zeros_like(dv_acc)

        lane = lax.broadcasted_iota(jnp.int32, (1, 128), 1)
        tri_after, tri_upto = tri_ref[0], tri_ref[1]
        hms = [(lane >= 64) if hh else (lane < 64) for hh in range(2)]
        qs = [jnp.where(hm, q_ref[...], jnp.zeros_like(q_ref[...])) * jnp.asarray(SB_SCALE, BF16) for hm in hms]
        dos = [jnp.where(hm, dy_ref[...], jnp.zeros_like(dy_ref[...])) for hm in hms]
        qst = [t.T for t in qs]
        dost = [t.T for t in dos]
        dq_acc[...] = jnp.zeros_like(dq_acc)
        run[...] = jnp.zeros_like(run)
        dzbuf[...] = jnp.zeros_like(dzbuf)
        abuf[...] = jnp.zeros_like(abuf)

        def kv(ref, j):
            return ref[pl.ds(pl.multiple_of(j * T, T), T), :]

        def flush(jp):
            kp = kv(k_ref, jp)
            dq_add = [_dot(dzbuf[hh], kp) for hh in range(2)]
            dk_add = _dot(qst[0], dzbuf[0]) + _dot(qst[1], dzbuf[1])
            dv_add = _dot(dost[0], abuf[0]) + _dot(dost[1], abuf[1])
            return dq_add, dk_add, dv_add

        def apply(jp, adds):
            dq_add, dk_add, dv_add = adds
            cols = pl.ds(pl.multiple_of(jp * T, T), T)
            for hh in range(2):
                dq_acc[hh] += dq_add[hh]
            dk_acc[:, cols] += dk_add
            dv_acc[:, cols] += dv_add

        for hh in range(2):
            zbuf[hh] = _dot_nt(qs[hh], kv(k_ref, 0))
            dabuf[hh] = _dot_nt(dos[hh], kv(v_ref, 0))

        def block(j, diagonal):
            jp = jnp.maximum(j - 1, 0)
            if diagonal:
                msk = lax.broadcasted_iota(jnp.int32, (T, T), 1) < lax.broadcasted_iota(jnp.int32, (T, T), 0)
            kp = kv(k_ref, jp)
            dq_add = [_dot(dzbuf[hh], kp) for hh in range(2)]
            sigs, lss, exs, zn, dan, dk_part, dv_part = [], [], [], [], [], [], []
            for hh in range(2):
                z = zbuf[hh]
                sp, _ = _softplus_parts(z)
                lss.append(z - sp)
                sigs.append(jnp.exp(lss[hh]))
                if diagonal:
                    sp = jnp.where(msk, sp, 0.0)
                exs.append(_dot(sp.astype(BF16), tri_after))
                if not diagonal:
                    zn.append(_dot_nt(qs[hh], kv(k_ref, j + 1)))
                dk_part.append(_dot(qst[hh], dzbuf[hh]))
            pgs, gs = [], []
            for hh in range(2):
                csl = slice(hh * 128, (hh + 1) * 128)
                cs = jnp.sum(jnp.where(lane == j, cs_ref[:, csl], 0.0), axis=-1, keepdims=True)
                a = jnp.exp(lss[hh] - exs[hh] - cs)
                if diagonal:
                    a = jnp.where(msk, a, 0.0)
                abuf_new = a.astype(BF16)
                g = a * dabuf[hh]
                gs.append((g, abuf_new))
                pgs.append(_dot(g.astype(BF16), tri_upto))
                if not diagonal:
                    dan.append(_dot_nt(dos[hh], kv(v_ref, j + 1)))
                dv_part.append(_dot(dost[hh], abuf[hh]))
            adds = (dq_add, dk_part[0] + dk_part[1], dv_part[0] + dv_part[1])
            for hh in range(2):
                g, abuf_new = gs[hh]
                cg = run[hh]
                dz = g - sigs[hh] * (cg + pgs[hh])
                if diagonal:
                    dz = jnp.where(msk, dz, 0.0)
                run[hh] = cg + pgs[hh][:, T - 1:T]
                dzbuf[hh] = dz.astype(BF16)
                abuf[hh] = abuf_new
            apply(jp, adds)
            if not diagonal:
                for hh in range(2):
                    zbuf[hh] = zn[hh]
                    dabuf[hh] = dan[hh]

        def step(j, carry):
            block(j, False)
            return carry

        lax.fori_loop(0, i, step, 0)
        block(i, True)
        apply(i, flush(i))
        dq_ref[...] = (jnp.where(lane < 64, dq_acc[0], dq_acc[1]) * SB_SCALE).astype(dq_ref.dtype)

        @pl.when(i == NQ - 1)
        def _():
            dk_ref[...] = dk_acc[...].T.astype(dk_ref.dtype)
            dv_ref[...] = dv_acc[...].T.astype(dv_ref.dtype)

    W = SB_HEADS * SB_DIM
    return pl.pallas_call(
        body, name="sb_bwd", grid=(SB_HEADS // 2, NQ),
        in_specs=[pl.BlockSpec((T, 128), lambda h, i: (i, qb + h)),
                  pl.BlockSpec((S, 128), lambda h, i: (0, kb + h)),
                  pl.BlockSpec((S, 128), lambda h, i: (0, vb + h)),
                  pl.BlockSpec((T, 256), lambda h, i: (i, h)),
                  pl.BlockSpec((T, 128), lambda h, i: (i, h)),
                  pl.BlockSpec((2, T, T), lambda h, i: (0, 0, 0))],
        out_specs=[pl.BlockSpec((T, 128), lambda h, i: (i, h)),
                   pl.BlockSpec((S, 128), lambda h, i: (0, h)),
                   pl.BlockSpec((S, 128), lambda h, i: (0, h))],
        out_shape=[jax.ShapeDtypeStruct((S, W), BF16)] * 3,
        scratch_shapes=[pltpu.VMEM((128, S), F32), pltpu.VMEM((128, S), F32), pltpu.VMEM((2, T, 128), F32),
                        pltpu.VMEM((2, T, 1), F32), pltpu.VMEM((2, T, T), F32), pltpu.VMEM((2, T, T), F32),
                        pltpu.VMEM((2, T, T), BF16), pltpu.VMEM((2, T, T), BF16)],
        compiler_params=_cparams(("parallel", "arbitrary")),
    )(p, p, p, carries, dy, tri)


def _exchange(srcs, out_shapes, src_slice, dst_slice, name):
    n = len(srcs)

    def body(*refs):
        ins, outs = refs[:n], refs[n:2 * n]
        send_sems, recv_sems, loc_sems = refs[2 * n:]
        x, y, c = lax.axis_index("x"), lax.axis_index("y"), lax.axis_index("c")
        me = 4 * x + 2 * y + c
        local = [pltpu.make_async_copy(src_slice(t, ins[t], me), dst_slice(t, outs[t], me), loc_sems.at[t])
                 for t in range(n)]
        for cp in local:
            cp.start()
        sends, recvs = [], []
        for k in (1, 2, 4, 6, 3, 5, 7):
            px = 1 - x if k & 4 else x
            py = 1 - y if k & 2 else y
            pc = 1 - c if k & 1 else c
            peer = 4 * px + 2 * py + pc
            for t in range(n):
                s = t * 7 + k - 1
                sends.append(pltpu.make_async_remote_copy(
                    src_ref=src_slice(t, ins[t], peer), dst_ref=dst_slice(t, outs[t], me),
                    send_sem=send_sems.at[s], recv_sem=recv_sems.at[s],
                    device_id=(px, py, pc), device_id_type=pl.DeviceIdType.MESH))
                recvs.append(pltpu.make_async_remote_copy(
                    src_ref=src_slice(t, ins[t], me), dst_ref=dst_slice(t, outs[t], peer),
                    send_sem=send_sems.at[s], recv_sem=recv_sems.at[s],
                    device_id=(px, py, pc), device_id_type=pl.DeviceIdType.MESH))
        for cp in sends:
            cp.start()
        for cp in recvs:
            cp.wait_recv()
        for cp in sends:
            cp.wait_send()
        for cp in local:
            cp.wait()

    anyspec = pl.BlockSpec(memory_space=pl.ANY)
    return pl.pallas_call(
        body, name=name, in_specs=[anyspec] * n, out_specs=[anyspec] * n,
        out_shape=[jax.ShapeDtypeStruct(s, d) for s, d in out_shapes],
        scratch_shapes=[pltpu.SemaphoreType.DMA((7 * n,)), pltpu.SemaphoreType.DMA((7 * n,)),
                        pltpu.SemaphoreType.DMA((n,))],
    )(*srcs)


def _all_gather_lead(xs, name):
    return _exchange(
        xs, [((N_DEV,) + x.shape, x.dtype) for x in xs],
        lambda t, ref, peer: ref, lambda t, ref, who: ref.at[who], name)


def _all_to_all_lead(xs, name):
    return _exchange(
        xs, [(x.shape, x.dtype) for x in xs],
        lambda t, ref, peer: ref.at[peer], lambda t, ref, who: ref.at[who], name)


_W_AXIS = {"w_in": 1, "w_ret_out": 0, "w_sb_out": 0, "w_mix_out": 0, "w_up": 1, "w_down": 0}
_W_NAMES = tuple(_W_AXIS)


def _window(ref, axis, who, width, count=1):
    start = pl.multiple_of(who * width, width)
    return ref.at[pl.ds(start, count * width), :] if axis == 0 else ref.at[:, pl.ds(start, count * width)]


_HBM = pl.BlockSpec(memory_space=pltpu.HBM)
_SEM = pl.BlockSpec(memory_space=pltpu.SEMAPHORE)
_EFFECT = pltpu.SideEffectType.DATAFLOW_SIDE_EFFECTING


def _exchange_start(srcs, shapes, src_slice, dst_slice, name, deps=()):
    n, nd = len(srcs), len(deps)
    lands = [pltpu.with_memory_space_constraint(lax.empty(s, d), pltpu.HBM) for s, d in shapes]

    def body(*refs):
        ins, lnd = refs[:n], refs[n:2 * n]
        sems = refs[2 * n + nd:4 * n + nd]
        token = refs[6 * n + nd]
        x, y, c = lax.axis_index("x"), lax.axis_index("y"), lax.axis_index("c")
        me = 4 * x + 2 * y + c
        for k in (0, 1, 2, 4, 6, 3, 5, 7):
            px = 1 - x if k & 4 else x
            py = 1 - y if k & 2 else y
            pc = 1 - c if k & 1 else c
            peer = 4 * px + 2 * py + pc
            for t in range(n):
                pltpu.make_async_remote_copy(
                    src_ref=src_slice(t, ins[t], peer), dst_ref=dst_slice(t, lnd[t], me),
                    send_sem=sems[2 * t], recv_sem=sems[2 * t + 1],
                    device_id=(px, py, pc), device_id_type=pl.DeviceIdType.MESH).start()
        token[...] = jnp.zeros_like(token)

    res = pl.pallas_call(
        body, name=name, in_specs=[_HBM] * (2 * n) + [pl.BlockSpec(memory_space=pl.ANY)] * nd,
        out_specs=[_SEM] * (2 * n) + [_HBM] * (2 * n) + [pl.BlockSpec(memory_space=pltpu.VMEM)],
        out_shape=[pltpu.SemaphoreType.DMA(())] * (2 * n) + [pltpu.HBM(s.shape, s.dtype) for s in srcs]
        + [pltpu.HBM(s.shape, s.dtype) for s in lands] + [jax.ShapeDtypeStruct((8, 128), F32)],
        input_output_aliases={t: 2 * n + t for t in range(2 * n)},
        compiler_params=pltpu.CompilerParams(has_side_effects=_EFFECT),
    )(*[pltpu.with_memory_space_constraint(s, pltpu.HBM) for s in srcs], *lands, *deps)
    return dict(n=n, sems=res[:2 * n], srcs=res[2 * n:3 * n], lands=res[3 * n:4 * n], token=res[4 * n])


def _exchange_wait(h, after, name):
    n = h['n']

    def body(*refs):
        lnd = refs[n:2 * n]
        sems = refs[2 * n:4 * n]
        x, y, c = lax.axis_index("x"), lax.axis_index("y"), lax.axis_index("c")
        for t in range(n):
            w = lnd[t]
            cp = pltpu.make_async_remote_copy(src_ref=w, dst_ref=w, send_sem=sems[2 * t], recv_sem=sems[2 * t + 1],
                                              device_id=(x, y, 1 - c), device_id_type=pl.DeviceIdType.MESH)
            cp.wait_send()
            cp.wait_recv()

    after = list(after)
    res = pl.pallas_call(
        body, name=name,
        in_specs=[_HBM] * (2 * n) + [_SEM] * (2 * n) + [pl.BlockSpec(memory_space=pl.ANY)] * len(after),
        out_specs=[_HBM] * (2 * n),
        out_shape=[pltpu.HBM(s.shape, s.dtype) for s in h['srcs']] + [pltpu.HBM(s.shape, s.dtype) for s in h['lands']],
        input_output_aliases={t: t for t in range(2 * n)},
        compiler_params=pltpu.CompilerParams(has_side_effects=_EFFECT),
    )(*h['srcs'], *h['lands'], *h['sems'], *after)
    return list(res[n:])


def _gather_start(shards, names, tag, deps=()):
    xs = [shards[nm] for nm in names]
    axes = [_W_AXIS[nm] for nm in names]
    widths = [x.shape[ax] for x, ax in zip(xs, axes)]
    shapes = [(tuple(d * (N_DEV if a == ax else 1) for a, d in enumerate(x.shape)), x.dtype) for x, ax in zip(xs, axes)]
    src = lambda t, ref, peer: ref
    dst = lambda t, ref, who: _window(ref, axes[t], who, widths[t])
    h = _exchange_start(xs, shapes, src, dst, "gw_start_" + tag, deps)
    h['tag'] = "gw_wait_" + tag
    return h


def _scatter_start(grads, names, tag):
    xs = [grads[nm] for nm in names]
    axes = [_W_AXIS[nm] for nm in names]
    widths = [x.shape[ax] // N_DEV for x, ax in zip(xs, axes)]
    shapes = [((N_DEV,) + tuple(d // (N_DEV if a == ax else 1) for a, d in enumerate(x.shape)), x.dtype)
              for x, ax in zip(xs, axes)]
    src = lambda t, ref, peer: _window(ref, axes[t], peer, widths[t])
    dst = lambda t, ref, who: ref.at[who]
    h = _exchange_start(xs, shapes, src, dst, "sg_start_" + tag)
    h['tag'] = "sg_wait_" + tag
    return h


def _finish(h, after):
    return _exchange_wait(h, after, h['tag'])


class _LayerWeights:
    def __init__(self, groups, started):
        self.groups = groups
        self.started = started
        self.got = {}
        self.after = None

    def __getitem__(self, nm):
        if nm not in self.got:
            for names, h in self.groups:
                if nm in names:
                    self.got.update(zip(names, _finish(h, list(self.after) + self.started)))
        return self.got[nm]


def _adam_math(p_ref, w, m, v):
    g = p_ref[0].astype(F32)
    for s in range(1, p_ref.shape[0]):
        g = g + p_ref[s].astype(F32)
    bc1 = 1.0 / (1.0 - ADAM_B1 ** ADAM_STEP)
    bc2 = 1.0 / (1.0 - ADAM_B2 ** ADAM_STEP)
    mm = ADAM_B1 * m + (1.0 - ADAM_B1) * g
    vv = ADAM_B2 * v + (1.0 - ADAM_B2) * jnp.square(g)
    return g, -ADAM_LR * ((mm * bc1) / (jnp.sqrt(vv * bc2) + ADAM_EPS) + ADAM_WD * w), mm, vv


def _adam(parts, w, m, v, name, tr=256):
    P, R, C = parts.shape
    tr = min(tr, R)
    assert R % tr == 0

    def body(p_ref, w_ref, m_ref, v_ref, *outs):
        for o_ref, val in zip(outs, _adam_math(p_ref, w_ref[...], m_ref[...], v_ref[...])):
            o_ref[...] = val

    spec = pl.BlockSpec((tr, C), lambda i: (i, 0))
    return pl.pallas_call(
        body, name=name, grid=(R // tr,),
        in_specs=[pl.BlockSpec((P, tr, C), lambda i: (0, i, 0)), spec, spec, spec],
        out_specs=[spec] * 4, out_shape=[jax.ShapeDtypeStruct((R, C), F32)] * 4,
        compiler_params=_cparams(("parallel",)),
    )(parts, w, m, v)


def _adam_layer(parts, w, m, v, l, prev, name, tr=256):
    P, R, C = parts.shape
    tr = min(tr, R)
    assert R % tr == 0 and w.shape == (DEPTH, R, C)
    npv = 0 if prev is None else 4

    def body(p_ref, w_ref, m_ref, v_ref, *rest):
        for o_ref, val in zip(rest[npv:], _adam_math(p_ref, w_ref[0], m_ref[0], v_ref[0])):
            o_ref[0] = val

    spec = pl.BlockSpec((1, tr, C), lambda i: (l, i, 0))
    return pl.pallas_call(
        body, name=name, grid=(R // tr,),
        in_specs=[pl.BlockSpec((P, tr, C), lambda i: (0, i, 0)), spec, spec, spec]
        + [pl.BlockSpec(memory_space=pl.ANY)] * npv,
        out_specs=[spec] * 4, out_shape=[jax.ShapeDtypeStruct((DEPTH, R, C), F32)] * 4,
        input_output_aliases={4 + t: t for t in range(npv)},
        compiler_params=_cparams(("parallel",)),
    )(parts, w, m, v, *([] if prev is None else prev))


def _mod_partial(cact_all, w_ada_l, b_ada_l):
    def body(c_ref, w_ref, b_ref, o_ref):
        o_ref[...] = _dot(c_ref[...].astype(BF16), w_ref[...].astype(BF16)) + b_ref[...]

    return pl.pallas_call(
        body, name="mod_partial", out_shape=jax.ShapeDtypeStruct((cact_all.shape[0], w_ada_l.shape[1]), F32),
        compiler_params=pltpu.CompilerParams(vmem_limit_bytes=VMEM_LIMIT),
    )(cact_all, w_ada_l, b_ada_l)


def _ada_grad(cact_t, dmod):
    D, n = cact_t.shape[0], dmod.shape[1]

    def body(c_ref, d_ref, o_ref):
        ct = c_ref[...].astype(BF16).astype(F32)
        dm = d_ref[...].astype(BF16).astype(F32)
        acc = ct[:, 0:1] * dm[0:1, :]
        for b in range(1, N_DEV):
            acc = acc + ct[:, b:b + 1] * dm[b:b + 1, :]
        o_ref[0] = acc

    return pl.pallas_call(
        body, name="ada_grad", out_shape=jax.ShapeDtypeStruct((1, D, n), F32),
        compiler_params=pltpu.CompilerParams(vmem_limit_bytes=VMEM_LIMIT),
    )(cact_t, dmod)


def _norm_mod(x, r, gv, sh):
    return x * r * gv + sh


def _silu(x):
    return x * _sigmoid(x)


def _rstd(x):
    return lax.rsqrt(jnp.mean(x * x, axis=-1, keepdims=True) + EPS)


def _residual_epi(acc, x, g):
    xn = x + g * acc
    return acc, xn, _rstd(xn)


def _layer_fwd(x0, r1, mod, gn1, gn2, W, rc, tri):
    S = x0.shape[0]
    sh1, sc1, g1m, sh2, sc2, g2m = [mod[i:i + 1] for i in range(N_MOD)]
    gv1 = gn1 * (1.0 + sc1)
    gv2 = gn2 * (1.0 + sc2)
    W.after = [r1, gv1]
    (p,) = _mm(x0, W["w_in"], a_ex=[(r1, 'm'), (gv1, 'k'), (sh1, 'k')], pro=_norm_mod, outs=(BF16,), name="mm_in")
    yret, rstate = _ret_fwd(p, rc)
    ysb, sbc = _sb_fwd(p, tri)
    W.after = [ysb]
    (ya,) = _mm(yret, W["w_ret_out"], tm=512, a_ex=[(p, 'a', O_RG)],
                pro=lambda yr, g: _silu(g.astype(F32)) * yr.astype(F32), outs=(BF16,), name="mm_ret_out")
    yb, mg = _mm(ysb, W["w_sb_out"], o_ex=[(ya, 'o'), (p, 'o', O_GA), (p, 'o', O_GB)],
                 epi=lambda acc, a, ga, gb: (acc, _sigmoid(ga.astype(F32)) * a.astype(F32)
                                             + _sigmoid(gb.astype(F32)) * acc),
                 outs=(BF16, BF16), name="mm_sb_out")
    mo, x1, r2 = _mm(mg, W["w_mix_out"], o_ex=[(x0, 'o'), (g1m, 'n')], epi=_residual_epi, outs=(BF16, F32), cols=1,
                     name="mm_mix_out")
    (act,) = _mm(x1, W["w_up"], a_ex=[(r2, 'm'), (gv2, 'k'), (sh2, 'k')], pro=_norm_mod,
                 epi=lambda acc: (jnp.maximum(acc, 0.0),), outs=(BF16,), name="mm_up")
    dn, x2, r_out = _mm(act, W["w_down"], tm=512, pro=lambda a: a * a, o_ex=[(x1, 'o'), (g2m, 'n')],
                        epi=_residual_epi, outs=(BF16, F32), cols=1, name="mm_down")
    saved = dict(x0=x0, r1=r1, p=p, yret=yret, rstate=rstate, ysb=ysb, sbc=sbc, ya=ya, yb=yb, mg=mg, mo=mo, x1=x1, r2=r2,
                 act=act, dn=dn, gv1=gv1, gv2=gv2, mod=mod, gn1=gn1, gn2=gn2)
    return x2, r_out, saved


def _norm_bwd(dh, x, r, dres, gv, gn, extra_rows=(), extra_vecs=(), extra_fn=None, extra_outs=(), name="norm_bwd"):
    D = x.shape[1]
    ne = len(extra_rows)

    def fn(dh_t, x_t, dres_t, *rest):
        er, rest = rest[:ne], rest[ne:]
        gv_t = rest[0]
        ev, r_t = rest[1:-1], rest[-1]
        xh = x_t * r_t
        dxh = dh_t * gv_t
        dx = r_t * (dxh - xh * jnp.mean(dxh * xh, axis=-1, keepdims=True)) + dres_t
        base = (dx, dh_t, dh_t * xh)
        if extra_fn is None:
            return base
        return base + tuple(extra_fn(dx, *er, *ev))

    return _ew(fn, [dh, x, dres] + list(extra_rows), vecs=[gv] + list(extra_vecs), cols=[r],
               outs=[('row', D, F32), ('sum', D), ('sum', D)] + list(extra_outs), name=name)


def _gate_bwd(dx, dn, g):
    return dx * dn.astype(F32), dx * g


_GATE_OUTS = [('sum', D_MODEL), ('row', D_MODEL, BF16)]


def _layer_bwd(dx2, d_g2m, d_dn, sv, below, W, rc, tri, emit):
    mod = sv['mod']
    sh1, sc1, g1m, sh2, sc2, g2m = [mod[i:i + 1] for i in range(N_MOD)]
    D = D_MODEL
    p = sv['p']
    (d_up,) = _mm(d_dn, W["w_down"], tb=True, o_ex=[(sv['act'], 'o')],
                  epi=lambda acc, a: (acc * 2.0 * a.astype(F32),), outs=(BF16,), name="mm_down_dx")
    (gw_down,) = _mm(sv['act'], d_dn, ta=True, tk=DW_TK, pro=lambda a: a * a, outs=(BF16,), name="mm_down_dw")
    (gw_up,) = _mm(sv['x1'], d_up, ta=True, tm=512, tk=DW_TK,
                   a_ex=[(sv['r2'], 'k'), (sv['gv2'].reshape(1, D), 'm'), (sh2, 'm')], pro=_norm_mod,
                   outs=(BF16,), name="mm_up_dw")
    tok = emit(dict(w_down=gw_down, w_up=gw_up), "mlp")
    (d_h2,) = _mm(d_up, W["w_up"], tb=True, tk=2048, outs=(F32,), name="mm_up_dx", deps=[tok])
    dx1, d_sh2, s_h2, d_g1m, d_mo = _norm_bwd(
        d_h2, sv['x1'], sv['r2'], dx2, sv['gv2'], sv['gn2'],
        extra_rows=[sv['mo']], extra_vecs=[g1m],
        extra_fn=lambda dx, mo, g: (dx * mo.astype(F32), dx * g),
        extra_outs=[('sum', D), ('row', D, BF16)], name="norm_bwd_mlp")
    d_sc2 = sv['gn2'] * s_h2
    d_gn2 = (1.0 + sc2) * s_h2
    def mix_epi(acc, ya, yb, ga, gb):
        sa, sb = _sigmoid(ga.astype(F32)), _sigmoid(gb.astype(F32))
        return (acc * sa, acc * sb, acc * ya.astype(F32) * sa * (1.0 - sa), acc * yb.astype(F32) * sb * (1.0 - sb))

    d_ya, d_yb, d_ga, d_gb = _mm(d_mo, W["w_mix_out"], tb=True, tm=512,
                                 o_ex=[(sv['ya'], 'o'), (sv['yb'], 'o'), (p, 'o', O_GA), (p, 'o', O_GB)], epi=mix_epi,
                                 outs=(BF16,) * 4, name="mm_mix_dx")
    (gw_mix,) = _mm(sv['mg'], d_mo, ta=True, tk=DW_TK, outs=(BF16,), name="mm_mix_dw")

    def ro_epi(acc, g, yr):
        gf = g.astype(F32)
        s = _sigmoid(gf)
        return (acc * yr.astype(F32) * s * (1.0 + gf * (1.0 - s)), acc * gf * s)

    d_rg, d_yret = _mm(d_ya, W["w_ret_out"], tb=True, tm=512, o_ex=[(p, 'o', O_RG), (sv['yret'], 'o')], epi=ro_epi,
                       outs=(BF16, BF16), name="mm_ret_dx")
    (gw_ro,) = _mm(sv['yret'], d_ya, ta=True, tm=512, tk=DW_TK, a_ex=[(p, 'a', O_RG)],
                   pro=lambda yr, g: _silu(g.astype(F32)) * yr.astype(F32), outs=(BF16,), name="mm_ret_dw")
    (gw_so,) = _mm(sv['ysb'], d_yb, ta=True, tk=DW_TK, outs=(BF16,), name="mm_sb_dw")
    tok = emit(dict(w_mix_out=gw_mix, w_ret_out=gw_ro, w_sb_out=gw_so), "mix")
    (d_ysb,) = _mm(d_yb, W["w_sb_out"], tb=True, outs=(BF16,), name="mm_sb_dx", deps=[tok])
    d_sq, d_sk, d_sv = _sb_bwd(p, sv['sbc'], d_ysb, tri)
    d_ret = _ret_bwd(p, sv['rstate'], d_yret, rc)
    dp = [d_ret, d_rg, d_sq, d_sk, d_sv, d_ga, d_gb]
    (gw_in,) = _mm(sv['x0'], dp, ta=True, tm=512, tk=1024,
                   a_ex=[(sv['r1'], 'k'), (sv['gv1'].reshape(1, D), 'm'), (sh1, 'm')], pro=_norm_mod,
                   outs=(BF16,), name="mm_in_dw")
    tok = emit(dict(w_in=gw_in), "in")
    (d_h,) = _mm(dp, W["w_in"], tb=True, tk=1024, outs=(F32,), name="mm_in_dx", deps=[tok])
    if below is None:
        dx0, d_sh1, s_h1 = _norm_bwd(d_h, sv['x0'], sv['r1'], dx1, sv['gv1'], sv['gn1'], name="norm_bwd_mix")
        gate_below = (None, None)
    else:
        dx0, d_sh1, s_h1, *gate_below = _norm_bwd(
            d_h, sv['x0'], sv['r1'], dx1, sv['gv1'], sv['gn1'], extra_rows=[below['dn']],
            extra_vecs=[below['mod'][N_MOD - 1:N_MOD]], extra_fn=_gate_bwd, extra_outs=_GATE_OUTS,
            name="norm_bwd_mix_gate")
    d_sc1 = sv['gn1'] * s_h1
    d_gn1 = (1.0 + sc1) * s_h1
    d_mod = jnp.concatenate([d_sh1, d_sc1, d_g1m, d_sh2, d_sc2, d_g2m], axis=1)
    return dx0, gate_below, d_mod, d_gn1, d_gn2


def kernel(x, c, norm_mix_g, w_in, w_ret_out, w_sb_out, w_mix_out, norm_mlp_g, w_up, w_down, w_ada, b_ada, final_g, loss_target, m_norm_mix_g, m_w_in, m_w_ret_out, m_w_sb_out, m_w_mix_out, m_norm_mlp_g, m_w_up, m_w_down, m_w_ada, m_b_ada, m_final_g, v_norm_mix_g, v_w_in, v_w_ret_out, v_w_sb_out, v_w_mix_out, v_norm_mlp_g, v_w_up, v_w_down, v_w_ada, v_b_ada, v_final_g):
    S, D = x.shape[1], x.shape[2]
    x0 = x.reshape(S, D)
    tgt = loss_target.reshape(S, D)
    me = 4 * lax.axis_index("x") + 2 * lax.axis_index("y") + lax.axis_index("c")
    wts = dict(w_in=w_in, w_ret_out=w_ret_out, w_sb_out=w_sb_out, w_mix_out=w_mix_out, w_up=w_up, w_down=w_down)
    mts = dict(w_in=m_w_in, w_ret_out=m_w_ret_out, w_sb_out=m_w_sb_out, w_mix_out=m_w_mix_out, w_up=m_w_up, w_down=m_w_down)
    vts = dict(w_in=v_w_in, w_ret_out=v_w_ret_out, w_sb_out=v_w_sb_out, w_mix_out=v_w_mix_out, w_up=v_w_up, w_down=v_w_down)
    rc = _ret_consts(S)
    tri = _tri()

    (cact,) = _ew(lambda t: (_silu(t),), [jnp.pad(c, ((0, 7), (0, 0)))], outs=[('row', D, F32)], name="silu_c")
    (cact_all,) = _all_gather_lead([cact[0:1]], "gather_c")
    cact_all = cact_all.reshape(N_DEV, D)
    cact16 = jnp.pad(cact_all, ((0, 8), (0, 0)))
    n_ada = w_ada.shape[2]
    b_loc = lax.dynamic_slice_in_dim(b_ada, me * n_ada, n_ada, axis=1)
    mods = [_mod_partial(cact16, w_ada[l], b_loc[l:l + 1])[:N_DEV] for l in range(DEPTH)]
    modp = jnp.stack(mods, axis=1)
    (modr,) = _all_to_all_lead([modp], "scatter_mod")
    mod_full = jnp.transpose(modr, (1, 0, 2)).reshape(DEPTH, N_MOD, D)

    shards = {}
    for nm in _W_NAMES:
        w = wts[nm]
        (wb,) = _ew(lambda t: (t,), [w.reshape(-1, w.shape[-1])], outs=[('row', w.shape[-1], BF16)], name="cast_bf16")
        shards[nm] = wb.reshape(w.shape)
    rest = tuple(nm for nm in _W_NAMES if nm != "w_in")
    started, layer_groups = [modr], []
    for l in range(DEPTH):
        sh_l = {nm: shards[nm][l] for nm in _W_NAMES}
        groups = [(("w_in",), "%d_in" % l), (rest, "%d_rest" % l)] if l == 0 else [(_W_NAMES, "%d_all" % l)]
        layer_groups.append([])
        for names, tag in groups:
            layer_groups[-1].append((names, _gather_start(sh_l, names, tag, started[-1:])))
            started.append(layer_groups[-1][-1][1]['token'])
    layer_w = [_LayerWeights(g, started) for g in layer_groups]

    xs = x0
    (rs,) = _ew(lambda t: (_rstd(t),), [x0], outs=[('col', F32)], name="row_rstd")
    saved = []
    for l in range(DEPTH):
        xs, rs, sv = _layer_fwd(xs, rs, mod_full[l], norm_mix_g[l:l + 1], norm_mlp_g[l:l + 1], layer_w[l], rc, tri)
        sv['W'] = layer_w[l]
        saved.append(sv)

    fg = final_g.reshape(1, D)

    def head(xt, tg, dn, g, g2m, r):
        xh = xt * r
        e = xh * g - tg
        dy = e * (1.0 / D)
        dxh = dy * g
        dx = r * (dxh - xh * jnp.mean(dxh * xh, axis=-1, keepdims=True))
        return (dx, dy * xh, 0.5 * e * e * (1.0 / D)) + _gate_bwd(dx, dn, g2m)

    top = saved[DEPTH - 1]
    dxs, d_fg, loss_cols, *gate = _ew(head, [xs, tgt, top['dn']], vecs=[fg, top['mod'][N_MOD - 1:N_MOD]], cols=[rs],
                                      outs=[('row', D, F32), ('sum', D), ('sum', D)] + _GATE_OUTS, name="loss_head")

    small = [None] * DEPTH
    pending = []
    for l in reversed(range(DEPTH)):
        sv = saved[l]

        def emit(gw, tag, l=l):
            names = tuple(gw)
            pending.append((l, names, _scatter_start(gw, names, "%d_%s" % (l, tag))))
            return pending[-1][2]['token']

        dxs, gate, d_mod, d_gn1, d_gn2 = _layer_bwd(dxs, gate[0], gate[1], sv, saved[l - 1] if l else None,
                                                    sv['W'], rc, tri, emit)
        small[l] = (d_mod, d_gn1, d_gn2)
    grad_x = dxs.reshape(1, S, D)

    pack = jnp.concatenate([small[l][0] for l in range(DEPTH)] + [small[l][1] for l in range(DEPTH)]
                           + [small[l][2] for l in range(DEPTH)] + [d_fg, loss_cols], axis=1)
    (packs,) = _all_gather_lead([pack], "gather_small")
    packs = packs.reshape(N_DEV, -1)
    o = 0
    dmod_all = []
    for l in range(DEPTH):
        dmod_all.append(packs[:, o:o + N_MOD * D]); o += N_MOD * D
    gn1_parts = packs[:, o:o + DEPTH * D].reshape(N_DEV, DEPTH, D); o += DEPTH * D
    gn2_parts = packs[:, o:o + DEPTH * D].reshape(N_DEV, DEPTH, D); o += DEPTH * D
    fg_parts = packs[:, o:o + D].reshape(N_DEV, 1, D); o += D
    loss_parts = packs[:, o:o + D]
    (loss_sum,) = _ew(lambda t: (t,), [loss_parts], outs=[('sum', D)], name="loss_sum")
    loss = jnp.sum(loss_sum)

    res = {}
    after = [dxs, loss_sum]
    for l, names, h in pending:
        for nm, landed in zip(names, _finish(h, after)):
            res[nm] = _adam_layer(landed, wts[nm], mts[nm], vts[nm], l, res.get(nm), "adam_layer")
        after = [res[names[-1]][0]]
    res["norm_mix_g"] = _adam(gn1_parts, norm_mix_g, m_norm_mix_g, v_norm_mix_g, "adam")
    res["norm_mlp_g"] = _adam(gn2_parts, norm_mlp_g, m_norm_mlp_g, v_norm_mlp_g, "adam")
    fgr = _adam(fg_parts, fg, m_final_g.reshape(1, D), v_final_g.reshape(1, D), "adam")
    res["final_g"] = [t.reshape(D) for t in fgr]
    bparts = jnp.stack(dmod_all, axis=1)
    res["b_ada"] = _adam(bparts, b_ada, m_b_ada, v_b_ada, "adam")
    cact_t = cact_all.T
    for l in range(DEPTH):
        dm_loc = lax.dynamic_slice_in_dim(dmod_all[l], me * n_ada, n_ada, axis=1)
        res["w_ada"] = _adam_layer(_ada_grad(cact_t, dm_loc), w_ada, m_w_ada, v_w_ada, l, res.get("w_ada"),
                                   "adam_layer")

    order = ['norm_mix_g', 'w_in', 'w_ret_out', 'w_sb_out', 'w_mix_out', 'norm_mlp_g', 'w_up', 'w_down', 'w_ada', 'b_ada', 'final_g']
    out = [loss, grad_x]
    for i in range(4):
        out += [res[nm][i] for nm in order]
    return tuple(out)
```

```python
import functools
import math

import jax
import jax.numpy as jnp
import numpy as np
from jax import lax
from jax.experimental import pallas as pl
from jax.experimental.pallas import tpu as pltpu

F32 = jnp.float32
BF16 = jnp.bfloat16

N_DEV = 8
D_MODEL = 1024
DEPTH = 2
RET_HEADS = 4
RET_QK = 256
RET_V = 512
RET_CHUNK = 128
ROPE_BASE = 10000.0
SB_HEADS = 16
SB_DIM = 64
D_FF = 4096
N_MOD = 6
EPS = 1e-6
GN_EPS = 1e-5
O_RQ, O_RK, O_RV, O_RG, O_SQ, O_SK, O_SV, O_GA, O_GB = 0, 1024, 2048, 4096, 6144, 7168, 8192, 9216, 10240
IN_W = 11264

ADAM_LR, ADAM_B1, ADAM_B2, ADAM_EPS, ADAM_WD, ADAM_STEP = 0.001, 0.9, 0.999, 1e-08, 0.01, 10

VMEM_LIMIT = 56 * 1024 * 1024
DW_TK = 2048


def _cparams(sem):
    return pltpu.CompilerParams(dimension_semantics=sem, vmem_limit_bytes=VMEM_LIMIT)


def _mm(a, b, *, ta=False, tb=False, tm=1024, tn=1024, tk=None, a_ex=(), pro=None, o_ex=(), epi=None,
        outs=(F32,), cols=0, name, deps=()):
    a_parts = list(a) if isinstance(a, (list, tuple)) else [a]
    b_parts = list(b) if isinstance(b, (list, tuple)) else [b]
    assert not (ta and len(a_parts) > 1) and not (tb and len(b_parts) > 1)
    if ta:
        K, M = a.shape
    else:
        M, K = a_parts[0].shape[0], sum(t.shape[1] for t in a_parts)
    N = b.shape[0] if tb else sum(t.shape[1] for t in b_parts)
    tm, tn, tk = min(tm, M), min(tn, N), K if tk is None else min(tk, K)
    assert M % tm == 0 and N % tn == 0 and K % tk == 0, (name, M, N, K, tm, tn, tk)
    nk = K // tk
    multi = len(a_parts) > 1 or len(b_parts) > 1

    def ranges(parts, t):
        out, o = [], 0
        for arr in parts:
            assert arr.shape[1] % t == 0
            out.append((o, o + arr.shape[1] // t))
            o += arr.shape[1] // t
        return out

    a_rng = ranges(a_parts, tk) if len(a_parts) > 1 else [(0, nk)]
    b_rng = ranges(b_parts, tn) if len(b_parts) > 1 else [(0, N // tn)]
    clip = lambda v, lo, hi: jnp.clip(v - lo, 0, hi - lo - 1)
    mine = lambda v, lo, hi, w: jnp.where((v >= lo) & (v < hi), w, 0)
    in_specs, args = [], []
    for arr, (lo, hi) in zip(a_parts, a_rng):
        in_specs.append(pl.BlockSpec((tk, tm), lambda i, j, k: (k, i)) if ta
                        else pl.BlockSpec((tm, tk), lambda i, j, k, lo=lo, hi=hi: (i, clip(k, lo, hi))))
        args.append(arr)
    for arr, (lo, hi) in zip(b_parts, b_rng):
        in_specs.append(pl.BlockSpec((tn, tk), lambda i, j, k: (j, k)) if tb
                        else pl.BlockSpec((tk, tn), lambda i, j, k, lo=lo, hi=hi: (mine(j, lo, hi, k), clip(j, lo, hi))))
        args.append(arr)
    npa, npb = len(a_parts), len(b_parts)
    for arr, kind, *off in a_ex:
        off = off[0] if off else 0
        if kind == 'a' and ta:
            assert off % tm == 0
            in_specs.append(pl.BlockSpec((tk, tm), lambda i, j, k, o=off // tm: (k, o + i)))
        elif kind == 'a':
            assert off % tk == 0
            in_specs.append(pl.BlockSpec((tm, tk), lambda i, j, k, o=off // tk: (i, o + k)))
        elif kind == 'k':
            in_specs.append(pl.BlockSpec((tk, 1), lambda i, j, k: (k, 0)) if ta
                            else pl.BlockSpec((1, tk), lambda i, j, k: (0, k)))
        else:
            in_specs.append(pl.BlockSpec((1, tm), lambda i, j, k: (0, i)) if ta
                            else pl.BlockSpec((tm, 1), lambda i, j, k: (i, 0)))
        args.append(arr)
    for arr, kind, *off in o_ex:
        off = off[0] if off else 0
        if kind == 'o':
            assert off % tn == 0
            in_specs.append(pl.BlockSpec((tm, tn), lambda i, j, k, o=off // tn: (i, o + j)))
        elif kind == 'n':
            in_specs.append(pl.BlockSpec((1, tn), lambda i, j, k: (0, j)))
        else:
            in_specs.append(pl.BlockSpec((tm, 1), lambda i, j, k: (i, 0)))
        args.append(arr)
    for arr in deps:
        in_specs.append(pl.BlockSpec(memory_space=pl.ANY))
        args.append(arr)
    assert cols == 0 or N == tn
    na, no, nout, nd = len(a_ex), len(o_ex), len(outs) + cols, len(deps)
    dims = (((0 if ta else 1,), (1 if tb else 0,)), ((), ()))

    def body(*refs):
        a_refs, b_refs = refs[:npa], refs[npa:npa + npb]
        n0 = npa + npb
        aex = refs[n0:n0 + na]
        oex = refs[n0 + na:n0 + na + no]
        out_refs = refs[n0 + na + no + nd:n0 + na + no + nd + nout]

        def product(a_ref, b_ref):
            at = a_ref[...]
            if pro is not None:
                at = pro(at, *[r[...] for r in aex])
            return lax.dot_general(at.astype(BF16), b_ref[...].astype(BF16), dims, preferred_element_type=F32)

        def finish(res):
            vals = epi(res, *[r[...] for r in oex]) if epi is not None else (res,)
            for o_ref, v in zip(out_refs, vals):
                o_ref[...] = v.astype(o_ref.dtype)

        if nk == 1 and not multi:
            finish(product(a_refs[0], b_refs[0]))
            return
        acc = refs[-1]
        j, k = pl.program_id(1), pl.program_id(2)
        if multi:
            @pl.when(k == 0)
            def _():
                acc[...] = jnp.zeros_like(acc)

            for a_ref, (alo, ahi) in zip(a_refs, a_rng):
                for b_ref, (blo, bhi) in zip(b_refs, b_rng):
                    @pl.when((k >= alo) & (k < ahi) & (j >= blo) & (j < bhi))
                    def _():
                        acc[...] += product(a_ref, b_ref)
        else:
            @pl.when(k == 0)
            def _():
                acc[...] = product(a_refs[0], b_refs[0])

            @pl.when(k > 0)
            def _():
                acc[...] += product(a_refs[0], b_refs[0])

        @pl.when(k == nk - 1)
        def _():
            finish(acc[...])

    res = pl.pallas_call(
        body, name=name, grid=(M // tm, N // tn, nk), in_specs=in_specs,
        out_specs=[pl.BlockSpec((tm, tn), lambda i, j, k: (i, j)) for _ in outs]
        + [pl.BlockSpec((tm, 1), lambda i, j, k: (i, 0))] * cols,
        out_shape=[jax.ShapeDtypeStruct((M, N), dt) for dt in outs] + [jax.ShapeDtypeStruct((M, 1), F32)] * cols,
        scratch_shapes=[pltpu.VMEM((tm, tn), F32)] if nk > 1 or multi else [],
        compiler_params=_cparams(("parallel", "parallel", "arbitrary")),
    )(*args)
    return res


def _ew(fn, rows, vecs=(), cols=(), outs=(), tr=256, name=None):
    S = rows[0].shape[0]
    tr = min(tr, S)
    assert S % tr == 0
    in_specs, args = [], []
    for r in rows:
        in_specs.append(pl.BlockSpec((tr, r.shape[1]), lambda i: (i, 0)))
        args.append(r)
    for v in vecs:
        in_specs.append(pl.BlockSpec((1, v.shape[1]), lambda i: (0, 0)))
        args.append(v)
    for c in cols:
        in_specs.append(pl.BlockSpec((tr, 1), lambda i: (i, 0)))
        args.append(c)
    out_specs, out_shape = [], []
    for o in outs:
        if o[0] == 'row':
            out_specs.append(pl.BlockSpec((tr, o[1]), lambda i: (i, 0)))
            out_shape.append(jax.ShapeDtypeStruct((S, o[1]), o[2]))
        elif o[0] == 'sum':
            out_specs.append(pl.BlockSpec((1, o[1]), lambda i: (0, 0)))
            out_shape.append(jax.ShapeDtypeStruct((1, o[1]), F32))
        else:
            out_specs.append(pl.BlockSpec((tr, 1), lambda i: (i, 0)))
            out_shape.append(jax.ShapeDtypeStruct((S, 1), o[1]))
    nin = len(args)

    def body(*refs):
        i = pl.program_id(0)
        vals = fn(*[r[...] for r in refs[:nin]])
        for o, o_ref, v in zip(outs, refs[nin:], vals):
            if o[0] == 'sum':
                @pl.when(i == 0)
                def _():
                    o_ref[...] = jnp.zeros_like(o_ref)
                o_ref[...] += jnp.sum(v.astype(F32), axis=0, keepdims=True)
            else:
                o_ref[...] = v.astype(o_ref.dtype)

    return pl.pallas_call(
        body, name=name, grid=(S // tr,), in_specs=in_specs, out_specs=out_specs, out_shape=out_shape,
        compiler_params=_cparams(("arbitrary",)),
    )(*args)


def _sigmoid(x):
    return 1.0 / (1.0 + jnp.exp(-x))


def _ret_consts(S):
    h = np.arange(RET_HEADS, dtype=np.float64)
    log_gamma = np.log1p(-np.power(2.0, -5.0 - h))
    idx = np.arange(RET_CHUNK, dtype=np.float64)
    rel = idx[:, None] - idx[None, :]
    decay = np.where(rel >= 0, np.exp(np.maximum(rel, 0.0) * log_gamma[:, None, None]), 0.0)
    xi = np.exp((idx + 1.0) * log_gamma[:, None])[:, :, None]
    zeta = np.exp((RET_CHUNK - 1.0 - idx) * log_gamma[:, None])[:, :, None]
    gamma_c = np.exp(RET_CHUNK * log_gamma)[:, None, None]
    half = RET_QK // 2
    inv_freq = np.power(ROPE_BASE, -np.arange(half, dtype=np.float64) / half).astype(np.float32)
    ang = np.arange(S, dtype=np.float32)[:, None] * inv_freq[None, :]
    f = lambda t: jnp.asarray(t, F32)
    return dict(decay=f(decay), xi=f(xi), zeta=f(zeta), gc=f(gamma_c), cos=f(np.cos(ang)), sin=f(np.sin(ang)))


def _rot(t, cos, sin):
    half = RET_QK // 2
    t1, t2 = t[:, :half], t[:, half:]
    return jnp.concatenate([t1 * cos - t2 * sin, t1 * sin + t2 * cos], axis=-1)


def _rot_inv(t, cos, sin):
    half = RET_QK // 2
    t1, t2 = t[:, :half], t[:, half:]
    return jnp.concatenate([t1 * cos + t2 * sin, t2 * cos - t1 * sin], axis=-1)


_NT = (((1,), (1,)), ((), ()))
_TN = (((0,), (0,)), ((), ()))


def _dot(a, b):
    return jnp.dot(a, b, preferred_element_type=F32)


def _dot_nt(a, b):
    return lax.dot_general(a, b, _NT, preferred_element_type=F32)


def _dot_tn(a, b):
    return lax.dot_general(a, b, _TN, preferred_element_type=F32)


_QW, _VW = RET_HEADS * RET_QK, RET_HEADS * RET_V
_HEADS = range(RET_HEADS)


def _ret_in_specs(C, rev, NC):
    n_of = (lambda n: NC - 1 - n) if rev else (lambda n: n)
    whole3 = lambda n: (0, 0, 0)
    return [
        pl.BlockSpec((C, _QW), lambda n: (n_of(n), O_RQ // _QW)),
        pl.BlockSpec((C, _QW), lambda n: (n_of(n), O_RK // _QW)),
        pl.BlockSpec((C, _VW), lambda n: (n_of(n), O_RV // _VW)),
        pl.BlockSpec((C, RET_QK // 2), lambda n: (n_of(n), 0)),
        pl.BlockSpec((C, RET_QK // 2), lambda n: (n_of(n), 0)),
        pl.BlockSpec((RET_HEADS, C, C), whole3),
        pl.BlockSpec((RET_HEADS, C, 1), whole3),
        pl.BlockSpec((RET_HEADS, C, 1), whole3),
        pl.BlockSpec((RET_HEADS, 1, 1), whole3),
    ]


def _qk_heads(q_ref, k_ref, cos, sin):
    qs, kfs = [], []
    for h in _HEADS:
        cols = slice(h * RET_QK, (h + 1) * RET_QK)
        qs.append(_rot(q_ref[:, cols].astype(F32), cos, sin).astype(BF16))
        kfs.append(_rot(k_ref[:, cols].astype(F32), cos, sin) * (RET_QK ** -0.5))
    return qs, kfs


def _ret_fwd(p, rc):
    S = p.shape[0]
    C = RET_CHUNK
    NC = S // C

    def body(q_ref, k_ref, v_ref, cos_ref, sin_ref, dec_ref, xi_ref, zeta_ref, gc_ref, y_ref, rs_ref, r_acc):
        n = pl.program_id(0)

        @pl.when(n == 0)
        def _():
            r_acc[...] = jnp.zeros_like(r_acc)

        cos, sin = cos_ref[...], sin_ref[...]
        qs, kfs = _qk_heads(q_ref, k_ref, cos, sin)
        vs = [v_ref[:, h * RET_V:(h + 1) * RET_V] for h in _HEADS]
        rbs = [r_acc[h].astype(BF16) for h in _HEADS]
        for h in _HEADS:
            rs_ref[h, 0] = rbs[h]
        ss = [(_dot_nt(qs[h], kfs[h].astype(BF16)) * dec_ref[h]).astype(BF16) for h in _HEADS]
        os = [_dot(ss[h], vs[h]) + _dot(qs[h], rbs[h]) * xi_ref[h] for h in _HEADS]
        for h in _HEADS:
            o = os[h]
            mu = jnp.mean(o, axis=-1, keepdims=True)
            var = jnp.mean(jnp.square(o - mu), axis=-1, keepdims=True)
            y_ref[:, h * RET_V:(h + 1) * RET_V] = ((o - mu) * lax.rsqrt(var + GN_EPS)).astype(y_ref.dtype)
        for h in _HEADS:
            kz = (kfs[h] * zeta_ref[h]).astype(BF16)
            r_acc[h] = r_acc[h] * gc_ref[h] + _dot_tn(kz, vs[h])

    return pl.pallas_call(
        body, name="ret_fwd", grid=(NC,), in_specs=_ret_in_specs(C, False, NC),
        out_specs=[pl.BlockSpec((C, _VW), lambda n: (n, 0)),
                   pl.BlockSpec((RET_HEADS, 1, RET_QK, RET_V), lambda n: (0, n, 0, 0))],
        out_shape=[jax.ShapeDtypeStruct((S, _VW), BF16),
                   jax.ShapeDtypeStruct((RET_HEADS, NC, RET_QK, RET_V), BF16)],
        scratch_shapes=[pltpu.VMEM((RET_HEADS, RET_QK, RET_V), F32)],
        compiler_params=_cparams(("arbitrary",)),
    )(p, p, p, rc['cos'], rc['sin'], rc['decay'], rc['xi'], rc['zeta'], rc['gc'])


def _ret_bwd(p, rstate, dy, rc):
    S = p.shape[0]
    C = RET_CHUNK
    NC = S // C

    def body(q_ref, k_ref, v_ref, cos_ref, sin_ref, dec_ref, xi_ref, zeta_ref, gc_ref, rs_ref, dy_ref,
             d_ref, dr_acc):
        dq_ref, dk_ref, dv_ref = d_ref.at[:, 0:_QW], d_ref.at[:, _QW:2 * _QW], d_ref.at[:, 2 * _QW:2 * _QW + _VW]
        t = pl.program_id(0)

        @pl.when(t == 0)
        def _():
            dr_acc[...] = jnp.zeros_like(dr_acc)

        cos, sin = cos_ref[...], sin_ref[...]
        qs, kfs = _qk_heads(q_ref, k_ref, cos, sin)
        ks = [kf.astype(BF16) for kf in kfs]
        vs = [v_ref[:, h * RET_V:(h + 1) * RET_V] for h in _HEADS]
        rbs = [rs_ref[h, 0] for h in _HEADS]
        ss = [(_dot_nt(qs[h], ks[h]) * dec_ref[h]).astype(BF16) for h in _HEADS]
        os = [_dot(ss[h], vs[h]) + _dot(qs[h], rbs[h]) * xi_ref[h] for h in _HEADS]
        dobs, doxis = [], []
        for h in _HEADS:
            o = os[h]
            mu = jnp.mean(o, axis=-1, keepdims=True)
            var = jnp.mean(jnp.square(o - mu), axis=-1, keepdims=True)
            rstd = lax.rsqrt(var + GN_EPS)
            yh = (o - mu) * rstd
            dyf = dy_ref[:, h * RET_V:(h + 1) * RET_V].astype(F32)
            do = (dyf - jnp.mean(dyf, axis=-1, keepdims=True)
                  - yh * jnp.mean(dyf * yh, axis=-1, keepdims=True)) * rstd
            dobs.append(do.astype(BF16))
            doxis.append((do * xi_ref[h]).astype(BF16))
        drbs = [dr_acc[h].astype(BF16) for h in _HEADS]
        dss = [(_dot_nt(dobs[h], vs[h]) * dec_ref[h]).astype(BF16) for h in _HEADS]
        for h in _HEADS:
            dq = _dot(dss[h], ks[h]) + _dot_nt(doxis[h], rbs[h])
            dq_ref[:, h * RET_QK:(h + 1) * RET_QK] = _rot_inv(dq, cos, sin).astype(dq_ref.dtype)
        for h in _HEADS:
            dk = _dot_tn(dss[h], qs[h]) + _dot_nt(vs[h], drbs[h]) * zeta_ref[h]
            dk_ref[:, h * RET_QK:(h + 1) * RET_QK] = (_rot_inv(dk, cos, sin) * (RET_QK ** -0.5)).astype(dk_ref.dtype)
        for h in _HEADS:
            kz = (kfs[h] * zeta_ref[h]).astype(BF16)
            dv = _dot_tn(ss[h], dobs[h]) + _dot(kz, drbs[h])
            dv_ref[:, h * RET_V:(h + 1) * RET_V] = dv.astype(dv_ref.dtype)
        for h in _HEADS:
            dr_acc[h] = dr_acc[h] * gc_ref[h] + _dot_tn(qs[h], doxis[h])

    rn = lambda n: NC - 1 - n
    in_specs = _ret_in_specs(C, True, NC) + [
        pl.BlockSpec((RET_HEADS, 1, RET_QK, RET_V), lambda n: (0, rn(n), 0, 0)),
        pl.BlockSpec((C, _VW), lambda n: (rn(n), 0)),
    ]
    return pl.pallas_call(
        body, name="ret_bwd", grid=(NC,), in_specs=in_specs,
        out_specs=pl.BlockSpec((C, 2 * _QW + _VW), lambda n: (rn(n), 0)),
        out_shape=jax.ShapeDtypeStruct((S, 2 * _QW + _VW), BF16),
        scratch_shapes=[pltpu.VMEM((RET_HEADS, RET_QK, RET_V), F32)],
        compiler_params=_cparams(("arbitrary",)),
    )(p, p, p, rc['cos'], rc['sin'], rc['decay'], rc['xi'], rc['zeta'], rc['gc'], rstate, dy)


SB_T = 256
SB_SCALE = SB_DIM ** -0.5


def _tri():
    j = np.arange(SB_T)
    after = (j[:, None] > j[None, :]).astype(np.float32)
    upto = (j[:, None] <= j[None, :]).astype(np.float32)
    return jnp.asarray(np.stack([after, upto]), BF16)


def _softplus_parts(z):
    neg_abs = lax.bitcast_convert_type(lax.bitcast_convert_type(z, jnp.uint32) | jnp.uint32(0x80000000), F32)
    e = jnp.exp(neg_abs)
    return jnp.maximum(z, 0.0) + jnp.log(1.0 + e), e


def _sb_fwd(p, tri):
    S = p.shape[0]
    T = min(SB_T, S)
    NQ = S // T
    assert NQ <= 128
    qb, kb, vb = O_SQ // 128, O_SK // 128, O_SV // 128

    def body(q_ref, k_ref, v_ref, tri_ref, o_ref, cs_ref, o_acc, run, zbuf, abuf):
        i = pl.program_id(1)
        lane = lax.broadcasted_iota(jnp.int32, (1, 128), 1)
        tri_after = tri_ref[0]
        qs = [jnp.where((lane >= 64) if hh else (lane < 64), q_ref[...], jnp.zeros_like(q_ref[...]))
              * jnp.asarray(SB_SCALE, BF16) for hh in range(2)]
        cs_ref[...] = jnp.zeros_like(cs_ref)
        o_acc[...] = jnp.zeros_like(o_acc)
        run[...] = jnp.zeros_like(run)

        def kv(ref, j):
            return ref[pl.ds(pl.multiple_of(j * T, T), T), :]

        for hh in range(2):
            zbuf[hh] = _dot_nt(qs[hh], kv(k_ref, i))

        def block(t, diagonal):
            j = i - t
            if diagonal:
                msk = lax.broadcasted_iota(jnp.int32, (T, T), 1) < lax.broadcasted_iota(jnp.int32, (T, T), 0)
            if not diagonal:
                av = [_dot(abuf[hh], kv(v_ref, j + 1)) for hh in range(2)]
            lss, exs, tot, zn = [], [], [], []
            for hh in range(2):
                z = zbuf[hh]
                sp, _ = _softplus_parts(z)
                lss.append(z - sp)
                if diagonal:
                    sp = jnp.where(msk, sp, 0.0)
                exs.append(_dot(sp.astype(BF16), tri_after))
                tot.append(sp[:, 0:1])
                zn.append(_dot_nt(qs[hh], kv(k_ref, jnp.maximum(j - 1, 0))))
            for hh in range(2):
                csl = slice(hh * 128, (hh + 1) * 128)
                cs = run[hh]
                a = jnp.exp(lss[hh] - exs[hh] - cs)
                if diagonal:
                    a = jnp.where(msk, a, 0.0)
                abuf[hh] = a.astype(BF16)
                cs_ref[:, csl] = jnp.where(lane == j, cs, cs_ref[:, csl])
                run[hh] = cs + exs[hh][:, 0:1] + tot[hh]
            for hh in range(2):
                if not diagonal:
                    o_acc[hh] += av[hh]
                zbuf[hh] = zn[hh]

        block(0, True)

        def step(t, carry):
            block(t, False)
            return carry

        lax.fori_loop(1, i + 1, step, 0)
        o_ref[...] = jnp.where(lane < 64, o_acc[0] + _dot(abuf[0], kv(v_ref, 0)),
                               o_acc[1] + _dot(abuf[1], kv(v_ref, 0))).astype(o_ref.dtype)

    return pl.pallas_call(
        body, name="sb_fwd", grid=(SB_HEADS // 2, NQ),
        scratch_shapes=[pltpu.VMEM((2, T, 128), F32), pltpu.VMEM((2, T, 1), F32), pltpu.VMEM((2, T, T), F32),
                        pltpu.VMEM((2, T, T), BF16)],
        in_specs=[pl.BlockSpec((T, 128), lambda h, i: (i, qb + h)),
                  pl.BlockSpec((S, 128), lambda h, i: (0, kb + h)),
                  pl.BlockSpec((S, 128), lambda h, i: (0, vb + h)),
                  pl.BlockSpec((1, T, T), lambda h, i: (0, 0, 0))],
        out_specs=[pl.BlockSpec((T, 128), lambda h, i: (i, h)),
                   pl.BlockSpec((T, 256), lambda h, i: (i, h))],
        out_shape=[jax.ShapeDtypeStruct((S, SB_HEADS * SB_DIM), BF16),
                   jax.ShapeDtypeStruct((S, SB_HEADS * 128), F32)],
        compiler_params=_cparams(("parallel", "arbitrary")),
    )(p, p, p, tri)


def _sb_bwd(p, carries, dy, tri):
    S = p.shape[0]
    T = min(SB_T, S)
    NQ = S // T
    qb, kb, vb = O_SQ // 128, O_SK // 128, O_SV // 128

    def body(q_ref, k_ref, v_ref, cs_ref, dy_ref, tri_ref, dq_ref, dk_ref, dv_ref, dk_acc, dv_acc, dq_acc, run,
             zbuf, dabuf, dzbuf, abuf):
        i = pl.program_id(1)

        @pl.when(i == 0)
        def _():
            dk_acc[...] = jnp.zeros_like(dk_acc)
            dv_acc[...] = jnp.zeros_like(dv_acc)

        lane = lax.broadcasted_iota(jnp.int32, (1, 128), 1)
        tri_after, tri_upto = tri_ref[0], tri_ref[1]
        hms = [(lane >= 64) if hh else (lane < 64) for hh in range(2)]
        qs = [jnp.where(hm, q_ref[...], jnp.zeros_like(q_ref[...])) * jnp.asarray(SB_SCALE, BF16) for hm in hms]
        dos = [jnp.where(hm, dy_ref[...], jnp.zeros_like(dy_ref[...])) for hm in hms]
        qst = [t.T for t in qs]
        dost = [t.T for t in dos]
        dq_acc[...] = jnp.zeros_like(dq_acc)
        run[...] = jnp.zeros_like(run)
        dzbuf[...] = jnp.zeros_like(dzbuf)
        abuf[...] = jnp.zeros_like(abuf)

        def kv(ref, j):
            return ref[pl.ds(pl.multiple_of(j * T, T), T), :]

        def flush(jp):
            kp = kv(k_ref, jp)
            dq_add = [_dot(dzbuf[hh], kp) for hh in range(2)]
            dk_add = _dot(qst[0], dzbuf[0]) + _dot(qst[1], dzbuf[1])
            dv_add = _dot(dost[0], abuf[0]) + _dot(dost[1], abuf[1])
            return dq_add, dk_add, dv_add

        def apply(jp, adds):
            dq_add, dk_add, dv_add = adds
            cols = pl.ds(pl.multiple_of(jp * T, T), T)
            for hh in range(2):
                dq_acc[hh] += dq_add[hh]
            dk_acc[:, cols] += dk_add
            dv_acc[:, cols] += dv_add

        for hh in range(2):
            zbuf[hh] = _dot_nt(qs[hh], kv(k_ref, 0))
            dabuf[hh] = _dot_nt(dos[hh], kv(v_ref, 0))

        def block(j, diagonal):
            jp = jnp.maximum(j - 1, 0)
            if diagonal:
                msk = lax.broadcasted_iota(jnp.int32, (T, T), 1) < lax.broadcasted_iota(jnp.int32, (T, T), 0)
            kp = kv(k_ref, jp)
            dq_add = [_dot(dzbuf[hh], kp) for hh in range(2)]
            sigs, lss, exs, zn, dan, dk_part, dv_part = [], [], [], [], [], [], []
            for hh in range(2):
                z = zbuf[hh]
                sp, _ = _softplus_parts(z)
                lss.append(z - sp)
                sigs.append(jnp.exp(lss[hh]))
                if diagonal:
                    sp = jnp.where(msk, sp, 0.0)
                exs.append(_dot(sp.astype(BF16), tri_after))
                if not diagonal:
                    zn.append(_dot_nt(qs[hh], kv(k_ref, j + 1)))
                dk_part.append(_dot(qst[hh], dzbuf[hh]))
            pgs, gs = [], []
            for hh in range(2):
                csl = slice(hh * 128, (hh + 1) * 128)
                cs = jnp.sum(jnp.where(lane == j, cs_ref[:, csl], 0.0), axis=-1, keepdims=True)
                a = jnp.exp(lss[hh] - exs[hh] - cs)
                if diagonal:
                    a = jnp.where(msk, a, 0.0)
                abuf_new = a.astype(BF16)
                g = a * dabuf[hh]
                gs.append((g, abuf_new))
                pgs.append(_dot(g.astype(BF16), tri_upto))
                if not diagonal:
                    dan.append(_dot_nt(dos[hh], kv(v_ref, j + 1)))
                dv_part.append(_dot(dost[hh], abuf[hh]))
            adds = (dq_add, dk_part[0] + dk_part[1], dv_part[0] + dv_part[1])
            for hh in range(2):
                g, abuf_new = gs[hh]
                cg = run[hh]
                dz = g - sigs[hh] * (cg + pgs[hh])
                if diagonal:
                    dz = jnp.where(msk, dz, 0.0)
                run[hh] = cg + pgs[hh][:, T - 1:T]
                dzbuf[hh] = dz.astype(BF16)
                abuf[hh] = abuf_new
            apply(jp, adds)
            if not diagonal:
                for hh in range(2):
                    zbuf[hh] = zn[hh]
                    dabuf[hh] = dan[hh]

        def step(j, carry):
            block(j, False)
            return carry

        lax.fori_loop(0, i, step, 0)
        block(i, True)
        apply(i, flush(i))
        dq_ref[...] = (jnp.where(lane < 64, dq_acc[0], dq_acc[1]) * SB_SCALE).astype(dq_ref.dtype)

        @pl.when(i == NQ - 1)
        def _():
            dk_ref[...] = dk_acc[...].T.astype(dk_ref.dtype)
            dv_ref[...] = dv_acc[...].T.astype(dv_ref.dtype)

    W = SB_HEADS * SB_DIM
    return pl.pallas_call(
        body, name="sb_bwd", grid=(SB_HEADS // 2, NQ),
        in_specs=[pl.BlockSpec((T, 128), lambda h, i: (i, qb + h)),
                  pl.BlockSpec((S, 128), lambda h, i: (0, kb + h)),
                  pl.BlockSpec((S, 128), lambda h, i: (0, vb + h)),
                  pl.BlockSpec((T, 256), lambda h, i: (i, h)),
                  pl.BlockSpec((T, 128), lambda h, i: (i, h)),
                  pl.BlockSpec((2, T, T), lambda h, i: (0, 0, 0))],
        out_specs=[pl.BlockSpec((T, 128), lambda h, i: (i, h)),
                   pl.BlockSpec((S, 128), lambda h, i: (0, h)),
                   pl.BlockSpec((S, 128), lambda h, i: (0, h))],
        out_shape=[jax.ShapeDtypeStruct((S, W), BF16)] * 3,
        scratch_shapes=[pltpu.VMEM((128, S), F32), pltpu.VMEM((128, S), F32), pltpu.VMEM((2, T, 128), F32),
                        pltpu.VMEM((2, T, 1), F32), pltpu.VMEM((2, T, T), F32), pltpu.VMEM((2, T, T), F32),
                        pltpu.VMEM((2, T, T), BF16), pltpu.VMEM((2, T, T), BF16)],
        compiler_params=_cparams(("parallel", "arbitrary")),
    )(p, p, p, carries, dy, tri)


def _exchange(srcs, out_shapes, src_slice, dst_slice, name, deps=()):
    n, nd = len(srcs), len(deps)

    def body(*refs):
        ins, outs = refs[:n], refs[n + nd:2 * n + nd]
        send_sems, recv_sems, loc_sems = refs[2 * n + nd:]
        x, y, c = lax.axis_index("x"), lax.axis_index("y"), lax.axis_index("c")
        me = 4 * x + 2 * y + c
        local = [pltpu.make_async_copy(src_slice(t, ins[t], me), dst_slice(t, outs[t], me), loc_sems.at[t])
                 for t in range(n)]
        for cp in local:
            cp.start()
        sends, recvs = [], []
        for k in (1, 2, 4, 6, 3, 5, 7):
            px = 1 - x if k & 4 else x
            py = 1 - y if k & 2 else y
            pc = 1 - c if k & 1 else c
            peer = 4 * px + 2 * py + pc
            for t in range(n):
                s = t * 7 + k - 1
                sends.append(pltpu.make_async_remote_copy(
                    src_ref=src_slice(t, ins[t], peer), dst_ref=dst_slice(t, outs[t], me),
                    send_sem=send_sems.at[s], recv_sem=recv_sems.at[s],
                    device_id=(px, py, pc), device_id_type=pl.DeviceIdType.MESH))
                recvs.append(pltpu.make_async_remote_copy(
                    src_ref=src_slice(t, ins[t], me), dst_ref=dst_slice(t, outs[t], peer),
                    send_sem=send_sems.at[s], recv_sem=recv_sems.at[s],
                    device_id=(px, py, pc), device_id_type=pl.DeviceIdType.MESH))
        for cp in sends:
            cp.start()
        for cp in recvs:
            cp.wait_recv()
        for cp in sends:
            cp.wait_send()
        for cp in local:
            cp.wait()

    anyspec = pl.BlockSpec(memory_space=pl.ANY)
    return pl.pallas_call(
        body, name=name, in_specs=[anyspec] * (n + nd), out_specs=[anyspec] * n,
        out_shape=[jax.ShapeDtypeStruct(s, d) for s, d in out_shapes],
        scratch_shapes=[pltpu.SemaphoreType.DMA((7 * n,)), pltpu.SemaphoreType.DMA((7 * n,)),
                        pltpu.SemaphoreType.DMA((n,))],
    )(*srcs, *deps)


def _all_gather_lead(xs, name, deps=()):
    return _exchange(
        xs, [((N_DEV,) + x.shape, x.dtype) for x in xs],
        lambda t, ref, peer: ref, lambda t, ref, who: ref.at[who], name, deps)


def _all_to_all_lead(xs, name):
    return _exchange(
        xs, [(x.shape, x.dtype) for x in xs],
        lambda t, ref, peer: ref.at[peer], lambda t, ref, who: ref.at[who], name)


_W_AXIS = {"w_in": 1, "w_ret_out": 0, "w_sb_out": 0, "w_mix_out": 0, "w_up": 1, "w_down": 0}
_W_NAMES = tuple(_W_AXIS)


def _window(ref, axis, who, width, count=1):
    start = pl.multiple_of(who * width, width)
    return ref.at[pl.ds(start, count * width), :] if axis == 0 else ref.at[:, pl.ds(start, count * width)]


_HBM = pl.BlockSpec(memory_space=pltpu.HBM)
_SEM = pl.BlockSpec(memory_space=pltpu.SEMAPHORE)
_EFFECT = pltpu.SideEffectType.DATAFLOW_SIDE_EFFECTING


def _exchange_start(srcs, shapes, src_slice, dst_slice, name, deps=()):
    n, nd = len(srcs), len(deps)
    lands = [pltpu.with_memory_space_constraint(lax.empty(s, d), pltpu.HBM) for s, d in shapes]

    def body(*refs):
        ins, lnd = refs[:n], refs[n:2 * n]
        sems = refs[2 * n + nd:4 * n + nd]
        token = refs[6 * n + nd]
        x, y, c = lax.axis_index("x"), lax.axis_index("y"), lax.axis_index("c")
        me = 4 * x + 2 * y + c
        for k in (0, 1, 2, 4, 6, 3, 5, 7):
            px = 1 - x if k & 4 else x
            py = 1 - y if k & 2 else y
            pc = 1 - c if k & 1 else c
            peer = 4 * px + 2 * py + pc
            for t in range(n):
                pltpu.make_async_remote_copy(
                    src_ref=src_slice(t, ins[t], peer), dst_ref=dst_slice(t, lnd[t], me),
                    send_sem=sems[2 * t], recv_sem=sems[2 * t + 1],
                    device_id=(px, py, pc), device_id_type=pl.DeviceIdType.MESH).start()
        token[...] = jnp.zeros_like(token)

    res = pl.pallas_call(
        body, name=name, in_specs=[_HBM] * (2 * n) + [pl.BlockSpec(memory_space=pl.ANY)] * nd,
        out_specs=[_SEM] * (2 * n) + [_HBM] * (2 * n) + [pl.BlockSpec(memory_space=pltpu.VMEM)],
        out_shape=[pltpu.SemaphoreType.DMA(())] * (2 * n) + [pltpu.HBM(s.shape, s.dtype) for s in srcs]
        + [pltpu.HBM(s.shape, s.dtype) for s in lands] + [jax.ShapeDtypeStruct((8, 128), F32)],
        input_output_aliases={t: 2 * n + t for t in range(2 * n)},
        compiler_params=pltpu.CompilerParams(has_side_effects=_EFFECT),
    )(*[pltpu.with_memory_space_constraint(s, pltpu.HBM) for s in srcs], *lands, *deps)
    return dict(n=n, sems=res[:2 * n], srcs=res[2 * n:3 * n], lands=res[3 * n:4 * n], token=res[4 * n])


def _exchange_wait(h, after, name):
    n = h['n']

    def body(*refs):
        lnd = refs[n:2 * n]
        sems = refs[2 * n:4 * n]
        x, y, c = lax.axis_index("x"), lax.axis_index("y"), lax.axis_index("c")
        for t in range(n):
            w = lnd[t]
            cp = pltpu.make_async_remote_copy(src_ref=w, dst_ref=w, send_sem=sems[2 * t], recv_sem=sems[2 * t + 1],
                                              device_id=(x, y, 1 - c), device_id_type=pl.DeviceIdType.MESH)
            cp.wait_send()
            cp.wait_recv()

    after = list(after)
    res = pl.pallas_call(
        body, name=name,
        in_specs=[_HBM] * (2 * n) + [_SEM] * (2 * n) + [pl.BlockSpec(memory_space=pl.ANY)] * len(after),
        out_specs=[_HBM] * (2 * n),
        out_shape=[pltpu.HBM(s.shape, s.dtype) for s in h['srcs']] + [pltpu.HBM(s.shape, s.dtype) for s in h['lands']],
        input_output_aliases={t: t for t in range(2 * n)},
        compiler_params=pltpu.CompilerParams(has_side_effects=_EFFECT),
    )(*h['srcs'], *h['lands'], *h['sems'], *after)
    return list(res[n:])


def _gather_start(shards, names, tag, deps=()):
    xs = [shards[nm] for nm in names]
    axes = [_W_AXIS[nm] for nm in names]
    widths = [x.shape[ax] for x, ax in zip(xs, axes)]
    shapes = [(tuple(d * (N_DEV if a == ax else 1) for a, d in enumerate(x.shape)), x.dtype) for x, ax in zip(xs, axes)]
    src = lambda t, ref, peer: ref
    dst = lambda t, ref, who: _window(ref, axes[t], who, widths[t])
    h = _exchange_start(xs, shapes, src, dst, "gw_start_" + tag, deps)
    h['tag'] = "gw_wait_" + tag
    return h


def _scatter_start(grads, names, tag):
    xs = [grads[nm] for nm in names]
    axes = [_W_AXIS[nm] for nm in names]
    widths = [x.shape[ax] // N_DEV for x, ax in zip(xs, axes)]
    shapes = [((N_DEV,) + tuple(d // (N_DEV if a == ax else 1) for a, d in enumerate(x.shape)), x.dtype)
              for x, ax in zip(xs, axes)]
    src = lambda t, ref, peer: _window(ref, axes[t], peer, widths[t])
    dst = lambda t, ref, who: ref.at[who]
    h = _exchange_start(xs, shapes, src, dst, "sg_start_" + tag)
    h['tag'] = "sg_wait_" + tag
    return h


def _finish(h, after):
    return _exchange_wait(h, after, h['tag'])


class _LayerWeights:
    def __init__(self, groups, started):
        self.groups = groups
        self.started = started
        self.got = {}
        self.after = None

    def __getitem__(self, nm):
        if nm not in self.got:
            for names, h in self.groups:
                if nm in names:
                    self.got.update(zip(names, _finish(h, list(self.after) + self.started)))
        return self.got[nm]


def _adam_math(p_ref, w, m, v):
    g = p_ref[0].astype(F32)
    for s in range(1, p_ref.shape[0]):
        g = g + p_ref[s].astype(F32)
    bc1 = 1.0 / (1.0 - ADAM_B1 ** ADAM_STEP)
    bc2 = 1.0 / (1.0 - ADAM_B2 ** ADAM_STEP)
    mm = ADAM_B1 * m + (1.0 - ADAM_B1) * g
    vv = ADAM_B2 * v + (1.0 - ADAM_B2) * jnp.square(g)
    return g, -ADAM_LR * ((mm * bc1) / (jnp.sqrt(vv * bc2) + ADAM_EPS) + ADAM_WD * w), mm, vv


def _adam(parts, w, m, v, name, tr=256):
    P, R, C = parts.shape
    tr = min(tr, R)
    assert R % tr == 0

    def body(p_ref, w_ref, m_ref, v_ref, *outs):
        for o_ref, val in zip(outs, _adam_math(p_ref, w_ref[...], m_ref[...], v_ref[...])):
            o_ref[...] = val

    spec = pl.BlockSpec((tr, C), lambda i: (i, 0))
    return pl.pallas_call(
        body, name=name, grid=(R // tr,),
        in_specs=[pl.BlockSpec((P, tr, C), lambda i: (0, i, 0)), spec, spec, spec],
        out_specs=[spec] * 4, out_shape=[jax.ShapeDtypeStruct((R, C), F32)] * 4,
        compiler_params=_cparams(("parallel",)),
    )(parts, w, m, v)


def _adam_layer(parts, w, m, v, l, prev, name, tr=256):
    P, R, C = parts.shape
    tr = min(tr, R)
    assert R % tr == 0 and w.shape == (DEPTH, R, C)
    npv = 0 if prev is None else 4

    def body(p_ref, w_ref, m_ref, v_ref, *rest):
        for o_ref, val in zip(rest[npv:], _adam_math(p_ref, w_ref[0], m_ref[0], v_ref[0])):
            o_ref[0] = val

    spec = pl.BlockSpec((1, tr, C), lambda i: (l, i, 0))
    return pl.pallas_call(
        body, name=name, grid=(R // tr,),
        in_specs=[pl.BlockSpec((P, tr, C), lambda i: (0, i, 0)), spec, spec, spec]
        + [pl.BlockSpec(memory_space=pl.ANY)] * npv,
        out_specs=[spec] * 4, out_shape=[jax.ShapeDtypeStruct((DEPTH, R, C), F32)] * 4,
        input_output_aliases={4 + t: t for t in range(npv)},
        compiler_params=_cparams(("parallel",)),
    )(parts, w, m, v, *([] if prev is None else prev))


def _mod_partial(cact_all, w_ada_l, b_ada_l):
    def body(c_ref, w_ref, b_ref, o_ref):
        o_ref[...] = _dot(c_ref[...].astype(BF16), w_ref[...].astype(BF16)) + b_ref[...]

    return pl.pallas_call(
        body, name="mod_partial", out_shape=jax.ShapeDtypeStruct((cact_all.shape[0], w_ada_l.shape[1]), F32),
        compiler_params=pltpu.CompilerParams(vmem_limit_bytes=VMEM_LIMIT),
    )(cact_all, w_ada_l, b_ada_l)


def _ada_grad(cact_t, dmod):
    D, n = cact_t.shape[0], dmod.shape[1]

    def body(c_ref, d_ref, o_ref):
        ct = c_ref[...].astype(BF16).astype(F32)
        dm = d_ref[...].astype(BF16).astype(F32)
        acc = ct[:, 0:1] * dm[0:1, :]
        for b in range(1, N_DEV):
            acc = acc + ct[:, b:b + 1] * dm[b:b + 1, :]
        o_ref[0] = acc

    return pl.pallas_call(
        body, name="ada_grad", out_shape=jax.ShapeDtypeStruct((1, D, n), F32),
        compiler_params=pltpu.CompilerParams(vmem_limit_bytes=VMEM_LIMIT),
    )(cact_t, dmod)


def _norm_mod(x, r, gv, sh):
    return x * r * gv + sh


def _silu(x):
    return x * _sigmoid(x)


def _rstd(x):
    return lax.rsqrt(jnp.mean(x * x, axis=-1, keepdims=True) + EPS)


def _residual_epi(acc, x, g):
    xn = x + g * acc
    return acc, xn, _rstd(xn)


def _layer_fwd(x0, r1, mod, gn1, gn2, W, rc, tri):
    S = x0.shape[0]
    sh1, sc1, g1m, sh2, sc2, g2m = [mod[i:i + 1] for i in range(N_MOD)]
    gv1 = gn1 * (1.0 + sc1)
    gv2 = gn2 * (1.0 + sc2)
    W.after = [r1, gv1]
    (p,) = _mm(x0, W["w_in"], a_ex=[(r1, 'm'), (gv1, 'k'), (sh1, 'k')], pro=_norm_mod, outs=(BF16,), name="mm_in")
    yret, rstate = _ret_fwd(p, rc)
    ysb, sbc = _sb_fwd(p, tri)
    W.after = [ysb]
    (ya,) = _mm(yret, W["w_ret_out"], tm=512, a_ex=[(p, 'a', O_RG)],
                pro=lambda yr, g: _silu(g.astype(F32)) * yr.astype(F32), outs=(BF16,), name="mm_ret_out")
    yb, mg = _mm(ysb, W["w_sb_out"], o_ex=[(ya, 'o'), (p, 'o', O_GA), (p, 'o', O_GB)],
                 epi=lambda acc, a, ga, gb: (acc, _sigmoid(ga.astype(F32)) * a.astype(F32)
                                             + _sigmoid(gb.astype(F32)) * acc),
                 outs=(BF16, BF16), name="mm_sb_out")
    mo, x1, r2 = _mm(mg, W["w_mix_out"], o_ex=[(x0, 'o'), (g1m, 'n')], epi=_residual_epi, outs=(BF16, F32), cols=1,
                     name="mm_mix_out")
    (act,) = _mm(x1, W["w_up"], a_ex=[(r2, 'm'), (gv2, 'k'), (sh2, 'k')], pro=_norm_mod,
                 epi=lambda acc: (jnp.maximum(acc, 0.0),), outs=(BF16,), name="mm_up")
    dn, x2, r_out = _mm(act, W["w_down"], tm=512, pro=lambda a: a * a, o_ex=[(x1, 'o'), (g2m, 'n')],
                        epi=_residual_epi, outs=(BF16, F32), cols=1, name="mm_down")
    saved = dict(x0=x0, r1=r1, p=p, yret=yret, rstate=rstate, ysb=ysb, sbc=sbc, ya=ya, yb=yb, mg=mg, mo=mo, x1=x1, r2=r2,
                 act=act, dn=dn, gv1=gv1, gv2=gv2, mod=mod, gn1=gn1, gn2=gn2)
    return x2, r_out, saved


def _norm_bwd(dh, x, r, dres, gv, gn, extra_rows=(), extra_vecs=(), extra_fn=None, extra_outs=(), name="norm_bwd"):
    D = x.shape[1]
    ne = len(extra_rows)

    def fn(dh_t, x_t, dres_t, *rest):
        er, rest = rest[:ne], rest[ne:]
        gv_t = rest[0]
        ev, r_t = rest[1:-1], rest[-1]
        xh = x_t * r_t
        dxh = dh_t * gv_t
        dx = r_t * (dxh - xh * jnp.mean(dxh * xh, axis=-1, keepdims=True)) + dres_t
        base = (dx, dh_t, dh_t * xh)
        if extra_fn is None:
            return base
        return base + tuple(extra_fn(dx, *er, *ev))

    return _ew(fn, [dh, x, dres] + list(extra_rows), vecs=[gv] + list(extra_vecs), cols=[r],
               outs=[('row', D, F32), ('sum', D), ('sum', D)] + list(extra_outs), name=name)


def _gate_bwd(dx, dn, g):
    return dx * dn.astype(F32), dx * g


_GATE_OUTS = [('sum', D_MODEL), ('row', D_MODEL, BF16)]


def _layer_bwd(dx2, d_g2m, d_dn, sv, below, W, rc, tri, emit):
    mod = sv['mod']
    sh1, sc1, g1m, sh2, sc2, g2m = [mod[i:i + 1] for i in range(N_MOD)]
    D = D_MODEL
    p = sv['p']
    (d_up,) = _mm(d_dn, W["w_down"], tb=True, o_ex=[(sv['act'], 'o')],
                  epi=lambda acc, a: (acc * 2.0 * a.astype(F32),), outs=(BF16,), name="mm_down_dx")
    (gw_down,) = _mm(sv['act'], d_dn, ta=True, tk=DW_TK, pro=lambda a: a * a, outs=(BF16,), name="mm_down_dw")
    (gw_up,) = _mm(sv['x1'], d_up, ta=True, tm=512, tk=DW_TK,
                   a_ex=[(sv['r2'], 'k'), (sv['gv2'].reshape(1, D), 'm'), (sh2, 'm')], pro=_norm_mod,
                   outs=(BF16,), name="mm_up_dw")
    tok = emit(dict(w_down=gw_down, w_up=gw_up), "mlp")
    (d_h2,) = _mm(d_up, W["w_up"], tb=True, tk=2048, outs=(F32,), name="mm_up_dx", deps=[tok])
    dx1, d_sh2, s_h2, d_g1m, d_mo = _norm_bwd(
        d_h2, sv['x1'], sv['r2'], dx2, sv['gv2'], sv['gn2'],
        extra_rows=[sv['mo']], extra_vecs=[g1m],
        extra_fn=lambda dx, mo, g: (dx * mo.astype(F32), dx * g),
        extra_outs=[('sum', D), ('row', D, BF16)], name="norm_bwd_mlp")
    d_sc2 = sv['gn2'] * s_h2
    d_gn2 = (1.0 + sc2) * s_h2
    def mix_epi(acc, ya, yb, ga, gb):
        sa, sb = _sigmoid(ga.astype(F32)), _sigmoid(gb.astype(F32))
        return (acc * sa, acc * sb, acc * ya.astype(F32) * sa * (1.0 - sa), acc * yb.astype(F32) * sb * (1.0 - sb))

    d_ya, d_yb, d_ga, d_gb = _mm(d_mo, W["w_mix_out"], tb=True, tm=512,
                                 o_ex=[(sv['ya'], 'o'), (sv['yb'], 'o'), (p, 'o', O_GA), (p, 'o', O_GB)], epi=mix_epi,
                                 outs=(BF16,) * 4, name="mm_mix_dx")
    (gw_mix,) = _mm(sv['mg'], d_mo, ta=True, tk=DW_TK, outs=(BF16,), name="mm_mix_dw")

    def ro_epi(acc, g, yr):
        gf = g.astype(F32)
        s = _sigmoid(gf)
        return (acc * yr.astype(F32) * s * (1.0 + gf * (1.0 - s)), acc * gf * s)

    d_rg, d_yret = _mm(d_ya, W["w_ret_out"], tb=True, tm=512, o_ex=[(p, 'o', O_RG), (sv['yret'], 'o')], epi=ro_epi,
                       outs=(BF16, BF16), name="mm_ret_dx")
    (gw_ro,) = _mm(sv['yret'], d_ya, ta=True, tm=512, tk=DW_TK, a_ex=[(p, 'a', O_RG)],
                   pro=lambda yr, g: _silu(g.astype(F32)) * yr.astype(F32), outs=(BF16,), name="mm_ret_dw")
    (gw_so,) = _mm(sv['ysb'], d_yb, ta=True, tk=DW_TK, outs=(BF16,), name="mm_sb_dw")
    tok = emit(dict(w_mix_out=gw_mix, w_ret_out=gw_ro, w_sb_out=gw_so), "mix")
    (d_ysb,) = _mm(d_yb, W["w_sb_out"], tb=True, outs=(BF16,), name="mm_sb_dx", deps=[tok])
    d_sq, d_sk, d_sv = _sb_bwd(p, sv['sbc'], d_ysb, tri)
    d_ret = _ret_bwd(p, sv['rstate'], d_yret, rc)
    dp = [d_ret, d_rg, d_sq, d_sk, d_sv, d_ga, d_gb]
    (gw_in,) = _mm(sv['x0'], dp, ta=True, tm=512, tk=1024,
                   a_ex=[(sv['r1'], 'k'), (sv['gv1'].reshape(1, D), 'm'), (sh1, 'm')], pro=_norm_mod,
                   outs=(BF16,), name="mm_in_dw")
    tok = emit(dict(w_in=gw_in), "in")
    (d_h,) = _mm(dp, W["w_in"], tb=True, tk=1024, outs=(F32,), name="mm_in_dx", deps=[tok])
    if below is None:
        dx0, d_sh1, s_h1 = _norm_bwd(d_h, sv['x0'], sv['r1'], dx1, sv['gv1'], sv['gn1'], name="norm_bwd_mix")
        gate_below = (None, None)
    else:
        dx0, d_sh1, s_h1, *gate_below = _norm_bwd(
            d_h, sv['x0'], sv['r1'], dx1, sv['gv1'], sv['gn1'], extra_rows=[below['dn']],
            extra_vecs=[below['mod'][N_MOD - 1:N_MOD]], extra_fn=_gate_bwd, extra_outs=_GATE_OUTS,
            name="norm_bwd_mix_gate")
    d_sc1 = sv['gn1'] * s_h1
    d_gn1 = (1.0 + sc1) * s_h1
    d_mod = jnp.concatenate([d_sh1, d_sc1, d_g1m, d_sh2, d_sc2, d_g2m], axis=1)
    return dx0, gate_below, d_mod, d_gn1, d_gn2


def kernel(x, c, norm_mix_g, w_in, w_ret_out, w_sb_out, w_mix_out, norm_mlp_g, w_up, w_down, w_ada, b_ada, final_g, loss_target, m_norm_mix_g, m_w_in, m_w_ret_out, m_w_sb_out, m_w_mix_out, m_norm_mlp_g, m_w_up, m_w_down, m_w_ada, m_b_ada, m_final_g, v_norm_mix_g, v_w_in, v_w_ret_out, v_w_sb_out, v_w_mix_out, v_norm_mlp_g, v_w_up, v_w_down, v_w_ada, v_b_ada, v_final_g):
    S, D = x.shape[1], x.shape[2]
    x0 = x.reshape(S, D)
    tgt = loss_target.reshape(S, D)
    me = 4 * lax.axis_index("x") + 2 * lax.axis_index("y") + lax.axis_index("c")
    wts = dict(w_in=w_in, w_ret_out=w_ret_out, w_sb_out=w_sb_out, w_mix_out=w_mix_out, w_up=w_up, w_down=w_down)
    mts = dict(w_in=m_w_in, w_ret_out=m_w_ret_out, w_sb_out=m_w_sb_out, w_mix_out=m_w_mix_out, w_up=m_w_up, w_down=m_w_down)
    vts = dict(w_in=v_w_in, w_ret_out=v_w_ret_out, w_sb_out=v_w_sb_out, w_mix_out=v_w_mix_out, w_up=v_w_up, w_down=v_w_down)
    rc = _ret_consts(S)
    tri = _tri()

    (cact,) = _ew(lambda t: (_silu(t),), [jnp.pad(c, ((0, 7), (0, 0)))], outs=[('row', D, F32)], name="silu_c")
    (cact_all,) = _all_gather_lead([cact[0:1]], "gather_c")
    cact_all = cact_all.reshape(N_DEV, D)
    cact16 = jnp.pad(cact_all, ((0, 8), (0, 0)))
    n_ada = w_ada.shape[2]
    b_loc = lax.dynamic_slice_in_dim(b_ada, me * n_ada, n_ada, axis=1)
    mods = [_mod_partial(cact16, w_ada[l], b_loc[l:l + 1])[:N_DEV] for l in range(DEPTH)]
    modp = jnp.stack(mods, axis=1)
    (modr,) = _all_to_all_lead([modp], "scatter_mod")
    mod_full = jnp.transpose(modr, (1, 0, 2)).reshape(DEPTH, N_MOD, D)

    shards = {}
    for nm in _W_NAMES:
        w = wts[nm]
        (wb,) = _ew(lambda t: (t,), [w.reshape(-1, w.shape[-1])], outs=[('row', w.shape[-1], BF16)], name="cast_bf16")
        shards[nm] = wb.reshape(w.shape)
    rest = tuple(nm for nm in _W_NAMES if nm != "w_in")
    started, layer_groups = [modr], []
    for l in range(DEPTH):
        sh_l = {nm: shards[nm][l] for nm in _W_NAMES}
        groups = [(("w_in",), "%d_in" % l), (rest, "%d_rest" % l)] if l == 0 else [(_W_NAMES, "%d_all" % l)]
        layer_groups.append([])
        for names, tag in groups:
            layer_groups[-1].append((names, _gather_start(sh_l, names, tag, started[-1:])))
            started.append(layer_groups[-1][-1][1]['token'])
    layer_w = [_LayerWeights(g, started) for g in layer_groups]

    xs = x0
    (rs,) = _ew(lambda t: (_rstd(t),), [x0], outs=[('col', F32)], name="row_rstd")
    saved = []
    for l in range(DEPTH):
        xs, rs, sv = _layer_fwd(xs, rs, mod_full[l], norm_mix_g[l:l + 1], norm_mlp_g[l:l + 1], layer_w[l], rc, tri)
        sv['W'] = layer_w[l]
        saved.append(sv)

    fg = final_g.reshape(1, D)

    def head(xt, tg, dn, g, g2m, r):
        xh = xt * r
        e = xh * g - tg
        dy = e * (1.0 / D)
        dxh = dy * g
        dx = r * (dxh - xh * jnp.mean(dxh * xh, axis=-1, keepdims=True))
        return (dx, dy * xh, 0.5 * e * e * (1.0 / D)) + _gate_bwd(dx, dn, g2m)

    top = saved[DEPTH - 1]
    dxs, d_fg, loss_cols, *gate = _ew(head, [xs, tgt, top['dn']], vecs=[fg, top['mod'][N_MOD - 1:N_MOD]], cols=[rs],
                                      outs=[('row', D, F32), ('sum', D), ('sum', D)] + _GATE_OUTS, name="loss_head")

    small = [None] * DEPTH
    pending = []
    for l in reversed(range(DEPTH)):
        sv = saved[l]

        def emit(gw, tag, l=l):
            names = tuple(gw)
            pending.append((l, names, _scatter_start(gw, names, "%d_%s" % (l, tag))))
            return pending[-1][2]['token']

        dxs, gate, d_mod, d_gn1, d_gn2 = _layer_bwd(dxs, gate[0], gate[1], sv, saved[l - 1] if l else None,
                                                    sv['W'], rc, tri, emit)
        small[l] = (d_mod, d_gn1, d_gn2)
    grad_x = dxs.reshape(1, S, D)

    res = {}
    after = [dxs]
    for l, names, h in pending:
        for nm, landed in zip(names, _finish(h, after)):
            res[nm] = _adam_layer(landed, wts[nm], mts[nm], vts[nm], l, res.get(nm), "adam_layer")
        after = [res[names[-1]][0]]

    pack = jnp.concatenate([small[l][0] for l in range(DEPTH)] + [small[l][1] for l in range(DEPTH)]
                           + [small[l][2] for l in range(DEPTH)] + [d_fg, loss_cols], axis=1)
    (packs,) = _all_gather_lead([pack], "gather_small", deps=after)
    packs = packs.reshape(N_DEV, -1)
    o = 0
    dmod_all = []
    for l in range(DEPTH):
        dmod_all.append(packs[:, o:o + N_MOD * D]); o += N_MOD * D
    gn1_parts = packs[:, o:o + DEPTH * D].reshape(N_DEV, DEPTH, D); o += DEPTH * D
    gn2_parts = packs[:, o:o + DEPTH * D].reshape(N_DEV, DEPTH, D); o += DEPTH * D
    fg_parts = packs[:, o:o + D].reshape(N_DEV, 1, D); o += D
    loss_parts = packs[:, o:o + D]
    (loss_sum,) = _ew(lambda t: (t,), [loss_parts], outs=[('sum', D)], name="loss_sum")
    loss = jnp.sum(loss_sum)

    res["norm_mix_g"] = _adam(gn1_parts, norm_mix_g, m_norm_mix_g, v_norm_mix_g, "adam")
    res["norm_mlp_g"] = _adam(gn2_parts, norm_mlp_g, m_norm_mlp_g, v_norm_mlp_g, "adam")
    fgr = _adam(fg_parts, fg, m_final_g.reshape(1, D), v_final_g.reshape(1, D), "adam")
    res["final_g"] = [t.reshape(D) for t in fgr]
    bparts = jnp.stack(dmod_all, axis=1)
    res["b_ada"] = _adam(bparts, b_ada, m_b_ada, v_b_ada, "adam")
    cact_t = cact_all.T
    for l in range(DEPTH):
        dm_loc = lax.dynamic_slice_in_dim(dmod_all[l], me * n_ada, n_ada, axis=1)
        res["w_ada"] = _adam_layer(_ada_grad(cact_t, dm_loc), w_ada, m_w_ada, v_w_ada, l, res.get("w_ada"),
                                   "adam_layer")

    order = ['norm_mix_g', 'w_in', 'w_ret_out', 'w_sb_out', 'w_mix_out', 'norm_mlp_g', 'w_up', 'w_down', 'w_ada', 'b_ada', 'final_g']
    out = [loss, grad_x]
    for i in range(4):
        out += [res[nm][i] for nm in order]
    return tuple(out)
```

```python
import functools
import math

import jax
import jax.numpy as jnp
import numpy as np
from jax import lax
from jax.experimental import pallas as pl
from jax.experimental.pallas import tpu as pltpu

F32 = jnp.float32
BF16 = jnp.bfloat16

N_DEV = 8
D_MODEL = 1024
DEPTH = 2
RET_HEADS = 4
RET_QK = 256
RET_V = 512
RET_CHUNK = 128
ROPE_BASE = 10000.0
SB_HEADS = 16
SB_DIM = 64
D_FF = 4096
N_MOD = 6
EPS = 1e-6
GN_EPS = 1e-5
O_RQ, O_RK, O_RV, O_RG, O_SQ, O_SK, O_SV, O_GA, O_GB = 0, 1024, 2048, 4096, 6144, 7168, 8192, 9216, 10240
IN_W = 11264

ADAM_LR, ADAM_B1, ADAM_B2, ADAM_EPS, ADAM_WD, ADAM_STEP = 0.001, 0.9, 0.999, 1e-08, 0.01, 10

VMEM_LIMIT = 56 * 1024 * 1024
DW_TK = 2048


def _cparams(sem):
    return pltpu.CompilerParams(dimension_semantics=sem, vmem_limit_bytes=VMEM_LIMIT)


def _mm(a, b, *, ta=False, tb=False, tm=1024, tn=1024, tk=None, a_ex=(), pro=None, o_ex=(), epi=None,
        outs=(F32,), cols=0, name, deps=()):
    a_parts = list(a) if isinstance(a, (list, tuple)) else [a]
    b_parts = list(b) if isinstance(b, (list, tuple)) else [b]
    assert not (ta and len(a_parts) > 1) and not (tb and len(b_parts) > 1)
    if ta:
        K, M = a.shape
    else:
        M, K = a_parts[0].shape[0], sum(t.shape[1] for t in a_parts)
    N = b.shape[0] if tb else sum(t.shape[1] for t in b_parts)
    tm, tn, tk = min(tm, M), min(tn, N), K if tk is None else min(tk, K)
    assert M % tm == 0 and N % tn == 0 and K % tk == 0, (name, M, N, K, tm, tn, tk)
    nk = K // tk
    multi = len(a_parts) > 1 or len(b_parts) > 1

    def ranges(parts, t):
        out, o = [], 0
        for arr in parts:
            assert arr.shape[1] % t == 0
            out.append((o, o + arr.shape[1] // t))
            o += arr.shape[1] // t
        return out

    a_rng = ranges(a_parts, tk) if len(a_parts) > 1 else [(0, nk)]
    b_rng = ranges(b_parts, tn) if len(b_parts) > 1 else [(0, N // tn)]
    clip = lambda v, lo, hi: jnp.clip(v - lo, 0, hi - lo - 1)
    mine = lambda v, lo, hi, w: jnp.where((v >= lo) & (v < hi), w, 0)
    in_specs, args = [], []
    for arr, (lo, hi) in zip(a_parts, a_rng):
        in_specs.append(pl.BlockSpec((tk, tm), lambda i, j, k: (k, i)) if ta
                        else pl.BlockSpec((tm, tk), lambda i, j, k, lo=lo, hi=hi: (i, clip(k, lo, hi))))
        args.append(arr)
    for arr, (lo, hi) in zip(b_parts, b_rng):
        in_specs.append(pl.BlockSpec((tn, tk), lambda i, j, k: (j, k)) if tb
                        else pl.BlockSpec((tk, tn), lambda i, j, k, lo=lo, hi=hi: (mine(j, lo, hi, k), clip(j, lo, hi))))
        args.append(arr)
    npa, npb = len(a_parts), len(b_parts)
    for arr, kind, *off in a_ex:
        off = off[0] if off else 0
        if kind == 'a' and ta:
            assert off % tm == 0
            in_specs.append(pl.BlockSpec((tk, tm), lambda i, j, k, o=off // tm: (k, o + i)))
        elif kind == 'a':
            assert off % tk == 0
            in_specs.append(pl.BlockSpec((tm, tk), lambda i, j, k, o=off // tk: (i, o + k)))
        elif kind == 'k':
            in_specs.append(pl.BlockSpec((tk, 1), lambda i, j, k: (k, 0)) if ta
                            else pl.BlockSpec((1, tk), lambda i, j, k: (0, k)))
        else:
            in_specs.append(pl.BlockSpec((1, tm), lambda i, j, k: (0, i)) if ta
                            else pl.BlockSpec((tm, 1), lambda i, j, k: (i, 0)))
        args.append(arr)
    for arr, kind, *off in o_ex:
        off = off[0] if off else 0
        if kind == 'o':
            assert off % tn == 0
            in_specs.append(pl.BlockSpec((tm, tn), lambda i, j, k, o=off // tn: (i, o + j)))
        elif kind == 'n':
            in_specs.append(pl.BlockSpec((1, tn), lambda i, j, k: (0, j)))
        else:
            in_specs.append(pl.BlockSpec((tm, 1), lambda i, j, k: (i, 0)))
        args.append(arr)
    for arr in deps:
        in_specs.append(pl.BlockSpec(memory_space=pl.ANY))
        args.append(arr)
    assert cols == 0 or N == tn
    na, no, nout, nd = len(a_ex), len(o_ex), len(outs) + cols, len(deps)
    dims = (((0 if ta else 1,), (1 if tb else 0,)), ((), ()))

    def body(*refs):
        a_refs, b_refs = refs[:npa], refs[npa:npa + npb]
        n0 = npa + npb
        aex = refs[n0:n0 + na]
        oex = refs[n0 + na:n0 + na + no]
        out_refs = refs[n0 + na + no + nd:n0 + na + no + nd + nout]

        def product(a_ref, b_ref):
            at = a_ref[...]
            if pro is not None:
                at = pro(at, *[r[...] for r in aex])
            return lax.dot_general(at.astype(BF16), b_ref[...].astype(BF16), dims, preferred_element_type=F32)

        def finish(res):
            vals = epi(res, *[r[...] for r in oex]) if epi is not None else (res,)
            for o_ref, v in zip(out_refs, vals):
                o_ref[...] = v.astype(o_ref.dtype)

        if nk == 1 and not multi:
            finish(product(a_refs[0], b_refs[0]))
            return
        acc = refs[-1]
        j, k = pl.program_id(1), pl.program_id(2)
        if multi:
            @pl.when(k == 0)
            def _():
                acc[...] = jnp.zeros_like(acc)

            for a_ref, (alo, ahi) in zip(a_refs, a_rng):
                for b_ref, (blo, bhi) in zip(b_refs, b_rng):
                    @pl.when((k >= alo) & (k < ahi) & (j >= blo) & (j < bhi))
                    def _():
                        acc[...] += product(a_ref, b_ref)
        else:
            @pl.when(k == 0)
            def _():
                acc[...] = product(a_refs[0], b_refs[0])

            @pl.when(k > 0)
            def _():
                acc[...] += product(a_refs[0], b_refs[0])

        @pl.when(k == nk - 1)
        def _():
            finish(acc[...])

    res = pl.pallas_call(
        body, name=name, grid=(M // tm, N // tn, nk), in_specs=in_specs,
        out_specs=[pl.BlockSpec((tm, tn), lambda i, j, k: (i, j)) for _ in outs]
        + [pl.BlockSpec((tm, 1), lambda i, j, k: (i, 0))] * cols,
        out_shape=[jax.ShapeDtypeStruct((M, N), dt) for dt in outs] + [jax.ShapeDtypeStruct((M, 1), F32)] * cols,
        scratch_shapes=[pltpu.VMEM((tm, tn), F32)] if nk > 1 or multi else [],
        compiler_params=_cparams(("parallel", "parallel", "arbitrary")),
    )(*args)
    return res


def _ew(fn, rows, vecs=(), cols=(), outs=(), tr=256, name=None):
    S = rows[0].shape[0]
    tr = min(tr, S)
    assert S % tr == 0
    in_specs, args = [], []
    for r in rows:
        in_specs.append(pl.BlockSpec((tr, r.shape[1]), lambda i: (i, 0)))
        args.append(r)
    for v in vecs:
        in_specs.append(pl.BlockSpec((1, v.shape[1]), lambda i: (0, 0)))
        args.append(v)
    for c in cols:
        in_specs.append(pl.BlockSpec((tr, 1), lambda i: (i, 0)))
        args.append(c)
    out_specs, out_shape = [], []
    for o in outs:
        if o[0] == 'row':
            out_specs.append(pl.BlockSpec((tr, o[1]), lambda i: (i, 0)))
            out_shape.append(jax.ShapeDtypeStruct((S, o[1]), o[2]))
        elif o[0] == 'sum':
            out_specs.append(pl.BlockSpec((1, o[1]), lambda i: (0, 0)))
            out_shape.append(jax.ShapeDtypeStruct((1, o[1]), F32))
        else:
            out_specs.append(pl.BlockSpec((tr, 1), lambda i: (i, 0)))
            out_shape.append(jax.ShapeDtypeStruct((S, 1), o[1]))
    nin = len(args)

    def body(*refs):
        i = pl.program_id(0)
        vals = fn(*[r[...] for r in refs[:nin]])
        for o, o_ref, v in zip(outs, refs[nin:], vals):
            if o[0] == 'sum':
                @pl.when(i == 0)
                def _():
                    o_ref[...] = jnp.zeros_like(o_ref)
                o_ref[...] += jnp.sum(v.astype(F32), axis=0, keepdims=True)
            else:
                o_ref[...] = v.astype(o_ref.dtype)

    return pl.pallas_call(
        body, name=name, grid=(S // tr,), in_specs=in_specs, out_specs=out_specs, out_shape=out_shape,
        compiler_params=_cparams(("arbitrary",)),
    )(*args)


def _sigmoid(x):
    return 1.0 / (1.0 + jnp.exp(-x))


def _ret_consts(S):
    h = np.arange(RET_HEADS, dtype=np.float64)
    log_gamma = np.log1p(-np.power(2.0, -5.0 - h))
    idx = np.arange(RET_CHUNK, dtype=np.float64)
    rel = idx[:, None] - idx[None, :]
    decay = np.where(rel >= 0, np.exp(np.maximum(rel, 0.0) * log_gamma[:, None, None]), 0.0)
    xi = np.exp((idx + 1.0) * log_gamma[:, None])[:, :, None]
    zeta = np.exp((RET_CHUNK - 1.0 - idx) * log_gamma[:, None])[:, :, None]
    gamma_c = np.exp(RET_CHUNK * log_gamma)[:, None, None]
    half = RET_QK // 2
    inv_freq = np.power(ROPE_BASE, -np.arange(half, dtype=np.float64) / half).astype(np.float32)
    ang = np.arange(S, dtype=np.float32)[:, None] * inv_freq[None, :]
    f = lambda t: jnp.asarray(t, F32)
    return dict(decay=f(decay), xi=f(xi), zeta=f(zeta), gc=f(gamma_c), cos=f(np.cos(ang)), sin=f(np.sin(ang)))


def _rot(t, cos, sin):
    half = RET_QK // 2
    t1, t2 = t[:, :half], t[:, half:]
    return jnp.concatenate([t1 * cos - t2 * sin, t1 * sin + t2 * cos], axis=-1)


def _rot_inv(t, cos, sin):
    half = RET_QK // 2
    t1, t2 = t[:, :half], t[:, half:]
    return jnp.concatenate([t1 * cos + t2 * sin, t2 * cos - t1 * sin], axis=-1)


_NT = (((1,), (1,)), ((), ()))
_TN = (((0,), (0,)), ((), ()))


def _dot(a, b):
    return jnp.dot(a, b, preferred_element_type=F32)


def _dot_nt(a, b):
    return lax.dot_general(a, b, _NT, preferred_element_type=F32)


def _dot_tn(a, b):
    return lax.dot_general(a, b, _TN, preferred_element_type=F32)


_QW, _VW = RET_HEADS * RET_QK, RET_HEADS * RET_V
_HEADS = range(RET_HEADS)


def _ret_in_specs(C, rev, NC):
    n_of = (lambda n: NC - 1 - n) if rev else (lambda n: n)
    whole3 = lambda n: (0, 0, 0)
    return [
        pl.BlockSpec((C, _QW), lambda n: (n_of(n), O_RQ // _QW)),
        pl.BlockSpec((C, _QW), lambda n: (n_of(n), O_RK // _QW)),
        pl.BlockSpec((C, _VW), lambda n: (n_of(n), O_RV // _VW)),
        pl.BlockSpec((C, RET_QK // 2), lambda n: (n_of(n), 0)),
        pl.BlockSpec((C, RET_QK // 2), lambda n: (n_of(n), 0)),
        pl.BlockSpec((RET_HEADS, C, C), whole3),
        pl.BlockSpec((RET_HEADS, C, 1), whole3),
        pl.BlockSpec((RET_HEADS, C, 1), whole3),
        pl.BlockSpec((RET_HEADS, 1, 1), whole3),
    ]


def _qk_heads(q_ref, k_ref, cos, sin):
    qs, kfs = [], []
    for h in _HEADS:
        cols = slice(h * RET_QK, (h + 1) * RET_QK)
        qs.append(_rot(q_ref[:, cols].astype(F32), cos, sin).astype(BF16))
        kfs.append(_rot(k_ref[:, cols].astype(F32), cos, sin) * (RET_QK ** -0.5))
    return qs, kfs


def _ret_fwd(p, rc):
    S = p.shape[0]
    C = RET_CHUNK
    NC = S // C

    def body(q_ref, k_ref, v_ref, cos_ref, sin_ref, dec_ref, xi_ref, zeta_ref, gc_ref, y_ref, rs_ref, r_acc):
        n = pl.program_id(0)

        @pl.when(n == 0)
        def _():
            r_acc[...] = jnp.zeros_like(r_acc)

        cos, sin = cos_ref[...], sin_ref[...]
        qs, kfs = _qk_heads(q_ref, k_ref, cos, sin)
        vs = [v_ref[:, h * RET_V:(h + 1) * RET_V] for h in _HEADS]
        rbs = [r_acc[h].astype(BF16) for h in _HEADS]
        for h in _HEADS:
            rs_ref[h, 0] = rbs[h]
        ss = [(_dot_nt(qs[h], kfs[h].astype(BF16)) * dec_ref[h]).astype(BF16) for h in _HEADS]
        os = [_dot(ss[h], vs[h]) + _dot(qs[h], rbs[h]) * xi_ref[h] for h in _HEADS]
        for h in _HEADS:
            o = os[h]
            mu = jnp.mean(o, axis=-1, keepdims=True)
            var = jnp.mean(jnp.square(o - mu), axis=-1, keepdims=True)
            y_ref[:, h * RET_V:(h + 1) * RET_V] = ((o - mu) * lax.rsqrt(var + GN_EPS)).astype(y_ref.dtype)
        for h in _HEADS:
            kz = (kfs[h] * zeta_ref[h]).astype(BF16)
            r_acc[h] = r_acc[h] * gc_ref[h] + _dot_tn(kz, vs[h])

    return pl.pallas_call(
        body, name="ret_fwd", grid=(NC,), in_specs=_ret_in_specs(C, False, NC),
        out_specs=[pl.BlockSpec((C, _VW), lambda n: (n, 0)),
                   pl.BlockSpec((RET_HEADS, 1, RET_QK, RET_V), lambda n: (0, n, 0, 0))],
        out_shape=[jax.ShapeDtypeStruct((S, _VW), BF16),
                   jax.ShapeDtypeStruct((RET_HEADS, NC, RET_QK, RET_V), BF16)],
        scratch_shapes=[pltpu.VMEM((RET_HEADS, RET_QK, RET_V), F32)],
        compiler_params=_cparams(("arbitrary",)),
    )(p, p, p, rc['cos'], rc['sin'], rc['decay'], rc['xi'], rc['zeta'], rc['gc'])


def _ret_bwd(p, rstate, dy, rc):
    S = p.shape[0]
    C = RET_CHUNK
    NC = S // C

    def body(q_ref, k_ref, v_ref, cos_ref, sin_ref, dec_ref, xi_ref, zeta_ref, gc_ref, rs_ref, dy_ref,
             d_ref, dr_acc):
        dq_ref, dk_ref, dv_ref = d_ref.at[:, 0:_QW], d_ref.at[:, _QW:2 * _QW], d_ref.at[:, 2 * _QW:2 * _QW + _VW]
        t = pl.program_id(0)

        @pl.when(t == 0)
        def _():
            dr_acc[...] = jnp.zeros_like(dr_acc)

        cos, sin = cos_ref[...], sin_ref[...]
        qs, kfs = _qk_heads(q_ref, k_ref, cos, sin)
        ks = [kf.astype(BF16) for kf in kfs]
        vs = [v_ref[:, h * RET_V:(h + 1) * RET_V] for h in _HEADS]
        rbs = [rs_ref[h, 0] for h in _HEADS]
        ss = [(_dot_nt(qs[h], ks[h]) * dec_ref[h]).astype(BF16) for h in _HEADS]
        os = [_dot(ss[h], vs[h]) + _dot(qs[h], rbs[h]) * xi_ref[h] for h in _HEADS]
        dobs, doxis = [], []
        for h in _HEADS:
            o = os[h]
            mu = jnp.mean(o, axis=-1, keepdims=True)
            var = jnp.mean(jnp.square(o - mu), axis=-1, keepdims=True)
            rstd = lax.rsqrt(var + GN_EPS)
            yh = (o - mu) * rstd
            dyf = dy_ref[:, h * RET_V:(h + 1) * RET_V].astype(F32)
            do = (dyf - jnp.mean(dyf, axis=-1, keepdims=True)
                  - yh * jnp.mean(dyf * yh, axis=-1, keepdims=True)) * rstd
            dobs.append(do.astype(BF16))
            doxis.append((do * xi_ref[h]).astype(BF16))
        drbs = [dr_acc[h].astype(BF16) for h in _HEADS]
        dss = [(_dot_nt(dobs[h], vs[h]) * dec_ref[h]).astype(BF16) for h in _HEADS]
        for h in _HEADS:
            dq = _dot(dss[h], ks[h]) + _dot_nt(doxis[h], rbs[h])
            dq_ref[:, h * RET_QK:(h + 1) * RET_QK] = _rot_inv(dq, cos, sin).astype(dq_ref.dtype)
        for h in _HEADS:
            dk = _dot_tn(dss[h], qs[h]) + _dot_nt(vs[h], drbs[h]) * zeta_ref[h]
            dk_ref[:, h * RET_QK:(h + 1) * RET_QK] = (_rot_inv(dk, cos, sin) * (RET_QK ** -0.5)).astype(dk_ref.dtype)
        for h in _HEADS:
            kz = (kfs[h] * zeta_ref[h]).astype(BF16)
            dv = _dot_tn(ss[h], dobs[h]) + _dot(kz, drbs[h])
            dv_ref[:, h * RET_V:(h + 1) * RET_V] = dv.astype(dv_ref.dtype)
        for h in _HEADS:
            dr_acc[h] = dr_acc[h] * gc_ref[h] + _dot_tn(qs[h], doxis[h])

    rn = lambda n: NC - 1 - n
    in_specs = _ret_in_specs(C, True, NC) + [
        pl.BlockSpec((RET_HEADS, 1, RET_QK, RET_V), lambda n: (0, rn(n), 0, 0)),
        pl.BlockSpec((C, _VW), lambda n: (rn(n), 0)),
    ]
    return pl.pallas_call(
        body, name="ret_bwd", grid=(NC,), in_specs=in_specs,
        out_specs=pl.BlockSpec((C, 2 * _QW + _VW), lambda n: (rn(n), 0)),
        out_shape=jax.ShapeDtypeStruct((S, 2 * _QW + _VW), BF16),
        scratch_shapes=[pltpu.VMEM((RET_HEADS, RET_QK, RET_V), F32)],
        compiler_params=_cparams(("arbitrary",)),
    )(p, p, p, rc['cos'], rc['sin'], rc['decay'], rc['xi'], rc['zeta'], rc['gc'], rstate, dy)


SB_T = 256
SB_SCALE = SB_DIM ** -0.5


def _tri():
    j = np.arange(SB_T)
    after = (j[:, None] > j[None, :]).astype(np.float32)
    upto = (j[:, None] <= j[None, :]).astype(np.float32)
    return jnp.asarray(np.stack([after, upto]), BF16)


def _softplus_parts(z):
    neg_abs = lax.bitcast_convert_type(lax.bitcast_convert_type(z, jnp.uint32) | jnp.uint32(0x80000000), F32)
    e = jnp.exp(neg_abs)
    return jnp.maximum(z, 0.0) + jnp.log(1.0 + e), e


def _sb_fwd(p, tri):
    S = p.shape[0]
    T = min(SB_T, S)
    NQ = S // T
    assert NQ <= 128
    qb, kb, vb = O_SQ // 128, O_SK // 128, O_SV // 128

    def body(q_ref, k_ref, v_ref, tri_ref, o_ref, cs_ref, o_acc, run, zbuf, abuf):
        i = pl.program_id(1)
        lane = lax.broadcasted_iota(jnp.int32, (1, 128), 1)
        tri_after = tri_ref[0]
        qs = [jnp.where((lane >= 64) if hh else (lane < 64), q_ref[...], jnp.zeros_like(q_ref[...]))
              * jnp.asarray(SB_SCALE, BF16) for hh in range(2)]
        cs_ref[...] = jnp.zeros_like(cs_ref)
        o_acc[...] = jnp.zeros_like(o_acc)
        run[...] = jnp.zeros_like(run)

        def kv(ref, j):
            return ref[pl.ds(pl.multiple_of(j * T, T), T), :]

        for hh in range(2):
            zbuf[hh] = _dot_nt(qs[hh], kv(k_ref, i))

        def block(t, diagonal):
            j = i - t
            if diagonal:
                msk = lax.broadcasted_iota(jnp.int32, (T, T), 1) < lax.broadcasted_iota(jnp.int32, (T, T), 0)
            if not diagonal:
                av = [_dot(abuf[hh], kv(v_ref, j + 1)) for hh in range(2)]
            lss, exs, tot, zn = [], [], [], []
            for hh in range(2):
                z = zbuf[hh]
                sp, _ = _softplus_parts(z)
                lss.append(z - sp)
                if diagonal:
                    sp = jnp.where(msk, sp, 0.0)
                exs.append(_dot(sp.astype(BF16), tri_after))
                tot.append(sp[:, 0:1])
                zn.append(_dot_nt(qs[hh], kv(k_ref, jnp.maximum(j - 1, 0))))
            for hh in range(2):
                csl = slice(hh * 128, (hh + 1) * 128)
                cs = run[hh]
                a = jnp.exp(lss[hh] - exs[hh] - cs)
                if diagonal:
                    a = jnp.where(msk, a, 0.0)
                abuf[hh] = a.astype(BF16)
                cs_ref[:, csl] = jnp.where(lane == j, cs, cs_ref[:, csl])
                run[hh] = cs + exs[hh][:, 0:1] + tot[hh]
            for hh in range(2):
                if not diagonal:
                    o_acc[hh] += av[hh]
                zbuf[hh] = zn[hh]

        block(0, True)

        def step(t, carry):
            block(t, False)
            return carry

        lax.fori_loop(1, i + 1, step, 0)
        o_ref[...] = jnp.where(lane < 64, o_acc[0] + _dot(abuf[0], kv(v_ref, 0)),
                               o_acc[1] + _dot(abuf[1], kv(v_ref, 0))).astype(o_ref.dtype)

    return pl.pallas_call(
        body, name="sb_fwd", grid=(SB_HEADS // 2, NQ),
        scratch_shapes=[pltpu.VMEM((2, T, 128), F32), pltpu.VMEM((2, T, 1), F32), pltpu.VMEM((2, T, T), F32),
                        pltpu.VMEM((2, T, T), BF16)],
        in_specs=[pl.BlockSpec((T, 128), lambda h, i: (i, qb + h)),
                  pl.BlockSpec((S, 128), lambda h, i: (0, kb + h)),
                  pl.BlockSpec((S, 128), lambda h, i: (0, vb + h)),
                  pl.BlockSpec((1, T, T), lambda h, i: (0, 0, 0))],
        out_specs=[pl.BlockSpec((T, 128), lambda h, i: (i, h)),
                   pl.BlockSpec((T, 256), lambda h, i: (i, h))],
        out_shape=[jax.ShapeDtypeStruct((S, SB_HEADS * SB_DIM), BF16),
                   jax.ShapeDtypeStruct((S, SB_HEADS * 128), F32)],
        compiler_params=_cparams(("parallel", "arbitrary")),
    )(p, p, p, tri)


def _sb_bwd(p, carries, dy, tri):
    S = p.shape[0]
    T = min(SB_T, S)
    NQ = S // T
    qb, kb, vb = O_SQ // 128, O_SK // 128, O_SV // 128

    def body(q_ref, k_ref, v_ref, cs_ref, dy_ref, tri_ref, dq_ref, dk_ref, dv_ref, dk_acc, dv_acc, dq_acc, run,
             zbuf, dabuf, dzbuf, abuf):
        i = pl.program_id(1)

        @pl.when(i == 0)
        def _():
            dk_acc[...] = jnp.zeros_like(dk_acc)
            dv_acc[...] = jnp.zeros_like(dv_acc)

        lane = lax.broadcasted_iota(jnp.int32, (1, 128), 1)
        tri_after, tri_upto = tri_ref[0], tri_ref[1]
        hms = [(lane >= 64) if hh else (lane < 64) for hh in range(2)]
        qs = [jnp.where(hm, q_ref[...], jnp.zeros_like(q_ref[...])) * jnp.asarray(SB_SCALE, BF16) for hm in hms]
        dos = [jnp.where(hm, dy_ref[...], jnp.zeros_like(dy_ref[...])) for hm in hms]
        qst = [t.T for t in qs]
        dost = [t.T for t in dos]
        dq_acc[...] = jnp.zeros_like(dq_acc)
        run[...] = jnp.zeros_like(run)
        dzbuf[...] = jnp.zeros_like(dzbuf)
        abuf[...] = jnp.zeros_like(abuf)

        def kv(ref, j):
            return ref[pl.ds(pl.multiple_of(j * T, T), T), :]

        def flush(jp):
            kp = kv(k_ref, jp)
            dq_add = [_dot(dzbuf[hh], kp) for hh in range(2)]
            dk_add = _dot(qst[0], dzbuf[0]) + _dot(qst[1], dzbuf[1])
            dv_add = _dot(dost[0], abuf[0]) + _dot(dost[1], abuf[1])
            return dq_add, dk_add, dv_add

        def apply(jp, adds):
            dq_add, dk_add, dv_add = adds
            cols = pl.ds(pl.multiple_of(jp * T, T), T)
            for hh in range(2):
                dq_acc[hh] += dq_add[hh]
            dk_acc[:, cols] += dk_add
            dv_acc[:, cols] += dv_add

        for hh in range(2):
            zbuf[hh] = _dot_nt(qs[hh], kv(k_ref, 0))
            dabuf[hh] = _dot_nt(dos[hh], kv(v_ref, 0))

        def block(j, diagonal):
            jp = jnp.maximum(j - 1, 0)
            if diagonal:
                msk = lax.broadcasted_iota(jnp.int32, (T, T), 1) < lax.broadcasted_iota(jnp.int32, (T, T), 0)
            kp = kv(k_ref, jp)
            dq_add = [_dot(dzbuf[hh], kp) for hh in range(2)]
            sigs, lss, exs, zn, dan, dk_part, dv_part = [], [], [], [], [], [], []
            for hh in range(2):
                z = zbuf[hh]
                sp, _ = _softplus_parts(z)
                lss.append(z - sp)
                sigs.append(jnp.exp(lss[hh]))
                if diagonal:
                    sp = jnp.where(msk, sp, 0.0)
                exs.append(_dot(sp.astype(BF16), tri_after))
                if not diagonal:
                    zn.append(_dot_nt(qs[hh], kv(k_ref, j + 1)))
                dk_part.append(_dot(qst[hh], dzbuf[hh]))
            pgs, gs = [], []
            for hh in range(2):
                csl = slice(hh * 128, (hh + 1) * 128)
                cs = jnp.sum(jnp.where(lane == j, cs_ref[:, csl], 0.0), axis=-1, keepdims=True)
                a = jnp.exp(lss[hh] - exs[hh] - cs)
                if diagonal:
                    a = jnp.where(msk, a, 0.0)
                abuf_new = a.astype(BF16)
                g = a * dabuf[hh]
                gs.append((g, abuf_new))
                pgs.append(_dot(g.astype(BF16), tri_upto))
                if not diagonal:
                    dan.append(_dot_nt(dos[hh], kv(v_ref, j + 1)))
                dv_part.append(_dot(dost[hh], abuf[hh]))
            adds = (dq_add, dk_part[0] + dk_part[1], dv_part[0] + dv_part[1])
            for hh in range(2):
                g, abuf_new = gs[hh]
                cg = run[hh]
                dz = g - sigs[hh] * (cg + pgs[hh])
                if diagonal:
                    dz = jnp.where(msk, dz, 0.0)
                run[hh] = cg + pgs[hh][:, T - 1:T]
                dzbuf[hh] = dz.astype(BF16)
                abuf[hh] = abuf_new
            apply(jp, adds)
            if not diagonal:
                for hh in range(2):
                    zbuf[hh] = zn[hh]
                    dabuf[hh] = dan[hh]

        def step(j, carry):
            block(j, False)
            return carry

        lax.fori_loop(0, i, step, 0)
        block(i, True)
        apply(i, flush(i))
        dq_ref[...] = (jnp.where(lane < 64, dq_acc[0], dq_acc[1]) * SB_SCALE).astype(dq_ref.dtype)

        @pl.when(i == NQ - 1)
        def _():
            dk_ref[...] = dk_acc[...].T.astype(dk_ref.dtype)
            dv_ref[...] = dv_acc[...].T.astype(dv_ref.dtype)

    W = SB_HEADS * SB_DIM
    return pl.pallas_call(
        body, name="sb_bwd", grid=(SB_HEADS // 2, NQ),
        in_specs=[pl.BlockSpec((T, 128), lambda h, i: (i, qb + h)),
                  pl.BlockSpec((S, 128), lambda h, i: (0, kb + h)),
                  pl.BlockSpec((S, 128), lambda h, i: (0, vb + h)),
                  pl.BlockSpec((T, 256), lambda h, i: (i, h)),
                  pl.BlockSpec((T, 128), lambda h, i: (i, h)),
                  pl.BlockSpec((2, T, T), lambda h, i: (0, 0, 0))],
        out_specs=[pl.BlockSpec((T, 128), lambda h, i: (i, h)),
                   pl.BlockSpec((S, 128), lambda h, i: (0, h)),
                   pl.BlockSpec((S, 128), lambda h, i: (0, h))],
        out_shape=[jax.ShapeDtypeStruct((S, W), BF16)] * 3,
        scratch_shapes=[pltpu.VMEM((128, S), F32), pltpu.VMEM((128, S), F32), pltpu.VMEM((2, T, 128), F32),
                        pltpu.VMEM((2, T, 1), F32), pltpu.VMEM((2, T, T), F32), pltpu.VMEM((2, T, T), F32),
                        pltpu.VMEM((2, T, T), BF16), pltpu.VMEM((2, T, T), BF16)],
        compiler_params=_cparams(("parallel", "arbitrary")),
    )(p, p, p, carries, dy, tri)


def _exchange(srcs, out_shapes, src_slice, dst_slice, name, deps=()):
    n, nd = len(srcs), len(deps)

    def body(*refs):
        ins, outs = refs[:n], refs[n + nd:2 * n + nd]
        send_sems, recv_sems, loc_sems = refs[2 * n + nd:]
        x, y, c = lax.axis_index("x"), lax.axis_index("y"), lax.axis_index("c")
        me = 4 * x + 2 * y + c
        local = [pltpu.make_async_copy(src_slice(t, ins[t], me), dst_slice(t, outs[t], me), loc_sems.at[t])
                 for t in range(n)]
        for cp in local:
            cp.start()
        sends, recvs = [], []
        for k in (1, 2, 4, 6, 3, 5, 7):
            px = 1 - x if k & 4 else x
            py = 1 - y if k & 2 else y
            pc = 1 - c if k & 1 else c
            peer = 4 * px + 2 * py + pc
            for t in range(n):
                s = t * 7 + k - 1
                sends.append(pltpu.make_async_remote_copy(
                    src_ref=src_slice(t, ins[t], peer), dst_ref=dst_slice(t, outs[t], me),
                    send_sem=send_sems.at[s], recv_sem=recv_sems.at[s],
                    device_id=(px, py, pc), device_id_type=pl.DeviceIdType.MESH))
                recvs.append(pltpu.make_async_remote_copy(
                    src_ref=src_slice(t, ins[t], me), dst_ref=dst_slice(t, outs[t], peer),
                    send_sem=send_sems.at[s], recv_sem=recv_sems.at[s],
                    device_id=(px, py, pc), device_id_type=pl.DeviceIdType.MESH))
        for cp in sends:
            cp.start()
        for cp in recvs:
            cp.wait_recv()
        for cp in sends:
            cp.wait_send()
        for cp in local:
            cp.wait()

    anyspec = pl.BlockSpec(memory_space=pl.ANY)
    return pl.pallas_call(
        body, name=name, in_specs=[anyspec] * (n + nd), out_specs=[anyspec] * n,
        out_shape=[jax.ShapeDtypeStruct(s, d) for s, d in out_shapes],
        scratch_shapes=[pltpu.SemaphoreType.DMA((7 * n,)), pltpu.SemaphoreType.DMA((7 * n,)),
                        pltpu.SemaphoreType.DMA((n,))],
    )(*srcs, *deps)


def _all_gather_lead(xs, name, deps=()):
    return _exchange(
        xs, [((N_DEV,) + x.shape, x.dtype) for x in xs],
        lambda t, ref, peer: ref, lambda t, ref, who: ref.at[who], name, deps)


def _all_to_all_lead(xs, name):
    return _exchange(
        xs, [(x.shape, x.dtype) for x in xs],
        lambda t, ref, peer: ref.at[peer], lambda t, ref, who: ref.at[who], name)


_W_AXIS = {"w_in": 1, "w_ret_out": 0, "w_sb_out": 0, "w_mix_out": 0, "w_up": 1, "w_down": 0}
_W_NAMES = tuple(_W_AXIS)


def _window(ref, axis, who, width, count=1):
    start = pl.multiple_of(who * width, width)
    return ref.at[pl.ds(start, count * width), :] if axis == 0 else ref.at[:, pl.ds(start, count * width)]


_HBM = pl.BlockSpec(memory_space=pltpu.HBM)
_SEM = pl.BlockSpec(memory_space=pltpu.SEMAPHORE)
_EFFECT = pltpu.SideEffectType.DATAFLOW_SIDE_EFFECTING


def _exchange_start(srcs, shapes, src_slice, dst_slice, name, deps=()):
    n, nd = len(srcs), len(deps)
    lands = [pltpu.with_memory_space_constraint(lax.empty(s, d), pltpu.HBM) for s, d in shapes]

    def body(*refs):
        ins, lnd = refs[:n], refs[n:2 * n]
        sems = refs[2 * n + nd:4 * n + nd]
        token = refs[6 * n + nd]
        x, y, c = lax.axis_index("x"), lax.axis_index("y"), lax.axis_index("c")
        me = 4 * x + 2 * y + c
        for k in (0, 1, 2, 4, 6, 3, 5, 7):
            px = 1 - x if k & 4 else x
            py = 1 - y if k & 2 else y
            pc = 1 - c if k & 1 else c
            peer = 4 * px + 2 * py + pc
            for t in range(n):
                pltpu.make_async_remote_copy(
                    src_ref=src_slice(t, ins[t], peer), dst_ref=dst_slice(t, lnd[t], me),
                    send_sem=sems[2 * t], recv_sem=sems[2 * t + 1],
                    device_id=(px, py, pc), device_id_type=pl.DeviceIdType.MESH).start()
        token[...] = jnp.zeros_like(token)

    res = pl.pallas_call(
        body, name=name, in_specs=[_HBM] * (2 * n) + [pl.BlockSpec(memory_space=pl.ANY)] * nd,
        out_specs=[_SEM] * (2 * n) + [_HBM] * (2 * n) + [pl.BlockSpec(memory_space=pltpu.VMEM)],
        out_shape=[pltpu.SemaphoreType.DMA(())] * (2 * n) + [pltpu.HBM(s.shape, s.dtype) for s in srcs]
        + [pltpu.HBM(s.shape, s.dtype) for s in lands] + [jax.ShapeDtypeStruct((8, 128), F32)],
        input_output_aliases={t: 2 * n + t for t in range(2 * n)},
        compiler_params=pltpu.CompilerParams(has_side_effects=_EFFECT),
    )(*[pltpu.with_memory_space_constraint(s, pltpu.HBM) for s in srcs], *lands, *deps)
    return dict(n=n, sems=res[:2 * n], srcs=res[2 * n:3 * n], lands=res[3 * n:4 * n], token=res[4 * n])


def _exchange_wait(h, after, name):
    n = h['n']

    def body(*refs):
        lnd = refs[n:2 * n]
        sems = refs[2 * n:4 * n]
        x, y, c = lax.axis_index("x"), lax.axis_index("y"), lax.axis_index("c")
        for t in range(n):
            w = lnd[t]
            cp = pltpu.make_async_remote_copy(src_ref=w, dst_ref=w, send_sem=sems[2 * t], recv_sem=sems[2 * t + 1],
                                              device_id=(x, y, 1 - c), device_id_type=pl.DeviceIdType.MESH)
            cp.wait_send()
            cp.wait_recv()

    after = list(after)
    res = pl.pallas_call(
        body, name=name,
        in_specs=[_HBM] * (2 * n) + [_SEM] * (2 * n) + [pl.BlockSpec(memory_space=pl.ANY)] * len(after),
        out_specs=[_HBM] * (2 * n),
        out_shape=[pltpu.HBM(s.shape, s.dtype) for s in h['srcs']] + [pltpu.HBM(s.shape, s.dtype) for s in h['lands']],
        input_output_aliases={t: t for t in range(2 * n)},
        compiler_params=pltpu.CompilerParams(has_side_effects=_EFFECT),
    )(*h['srcs'], *h['lands'], *h['sems'], *after)
    return list(res[n:])


def _gather_start(shards, names, tag, deps=()):
    xs = [shards[nm] for nm in names]
    axes = [_W_AXIS[nm] for nm in names]
    widths = [x.shape[ax] for x, ax in zip(xs, axes)]
    shapes = [(tuple(d * (N_DEV if a == ax else 1) for a, d in enumerate(x.shape)), x.dtype) for x, ax in zip(xs, axes)]
    src = lambda t, ref, peer: ref
    dst = lambda t, ref, who: _window(ref, axes[t], who, widths[t])
    h = _exchange_start(xs, shapes, src, dst, "gw_start_" + tag, deps)
    h['tag'] = "gw_wait_" + tag
    return h


def _scatter_start(grads, names, tag):
    xs = [grads[nm] for nm in names]
    axes = [_W_AXIS[nm] for nm in names]
    widths = [x.shape[ax] // N_DEV for x, ax in zip(xs, axes)]
    shapes = [((N_DEV,) + tuple(d // (N_DEV if a == ax else 1) for a, d in enumerate(x.shape)), x.dtype)
              for x, ax in zip(xs, axes)]
    src = lambda t, ref, peer: _window(ref, axes[t], peer, widths[t])
    dst = lambda t, ref, who: ref.at[who]
    h = _exchange_start(xs, shapes, src, dst, "sg_start_" + tag)
    h['tag'] = "sg_wait_" + tag
    return h


def _finish(h, after):
    return _exchange_wait(h, after, h['tag'])


class _LayerWeights:
    def __init__(self, groups, started):
        self.groups = groups
        self.started = started
        self.got = {}
        self.after = None

    def __getitem__(self, nm):
        if nm not in self.got:
            for names, h in self.groups:
                if nm in names:
                    self.got.update(zip(names, _finish(h, list(self.after) + self.started)))
        return self.got[nm]


def _adam_math(p_ref, w, m, v):
    g = p_ref[0].astype(F32)
    for s in range(1, p_ref.shape[0]):
        g = g + p_ref[s].astype(F32)
    bc1 = 1.0 / (1.0 - ADAM_B1 ** ADAM_STEP)
    bc2 = 1.0 / (1.0 - ADAM_B2 ** ADAM_STEP)
    mm = ADAM_B1 * m + (1.0 - ADAM_B1) * g
    vv = ADAM_B2 * v + (1.0 - ADAM_B2) * jnp.square(g)
    return g, -ADAM_LR * ((mm * bc1) / (jnp.sqrt(vv * bc2) + ADAM_EPS) + ADAM_WD * w), mm, vv


def _adam(parts, w, m, v, name, tr=256):
    P, R, C = parts.shape
    tr = min(tr, R)
    assert R % tr == 0

    def body(p_ref, w_ref, m_ref, v_ref, *outs):
        for o_ref, val in zip(outs, _adam_math(p_ref, w_ref[...], m_ref[...], v_ref[...])):
            o_ref[...] = val

    spec = pl.BlockSpec((tr, C), lambda i: (i, 0))
    return pl.pallas_call(
        body, name=name, grid=(R // tr,),
        in_specs=[pl.BlockSpec((P, tr, C), lambda i: (0, i, 0)), spec, spec, spec],
        out_specs=[spec] * 4, out_shape=[jax.ShapeDtypeStruct((R, C), F32)] * 4,
        compiler_params=_cparams(("parallel",)),
    )(parts, w, m, v)


def _adam_layer(parts, w, m, v, l, prev, name, tr=256):
    P, R, C = parts.shape
    tr = min(tr, R)
    assert R % tr == 0 and w.shape == (DEPTH, R, C)
    npv = 0 if prev is None else 4

    def body(p_ref, w_ref, m_ref, v_ref, *rest):
        for o_ref, val in zip(rest[npv:], _adam_math(p_ref, w_ref[0], m_ref[0], v_ref[0])):
            o_ref[0] = val

    spec = pl.BlockSpec((1, tr, C), lambda i: (l, i, 0))
    return pl.pallas_call(
        body, name=name, grid=(R // tr,),
        in_specs=[pl.BlockSpec((P, tr, C), lambda i: (0, i, 0)), spec, spec, spec]
        + [pl.BlockSpec(memory_space=pl.ANY)] * npv,
        out_specs=[spec] * 4, out_shape=[jax.ShapeDtypeStruct((DEPTH, R, C), F32)] * 4,
        input_output_aliases={4 + t: t for t in range(npv)},
        compiler_params=_cparams(("parallel",)),
    )(parts, w, m, v, *([] if prev is None else prev))


def _mod_partial(cact_all, w_ada_l, b_ada_l):
    def body(c_ref, w_ref, b_ref, o_ref):
        o_ref[...] = _dot(c_ref[...].astype(BF16), w_ref[...].astype(BF16)) + b_ref[...]

    return pl.pallas_call(
        body, name="mod_partial", out_shape=jax.ShapeDtypeStruct((cact_all.shape[0], w_ada_l.shape[1]), F32),
        compiler_params=pltpu.CompilerParams(vmem_limit_bytes=VMEM_LIMIT),
    )(cact_all, w_ada_l, b_ada_l)


def _ada_grad(cact_t, dmod):
    D, n = cact_t.shape[0], dmod.shape[1]

    def body(c_ref, d_ref, o_ref):
        ct = c_ref[...].astype(BF16).astype(F32)
        dm = d_ref[...].astype(BF16).astype(F32)
        acc = ct[:, 0:1] * dm[0:1, :]
        for b in range(1, N_DEV):
            acc = acc + ct[:, b:b + 1] * dm[b:b + 1, :]
        o_ref[0] = acc

    return pl.pallas_call(
        body, name="ada_grad", out_shape=jax.ShapeDtypeStruct((1, D, n), F32),
        compiler_params=pltpu.CompilerParams(vmem_limit_bytes=VMEM_LIMIT),
    )(cact_t, dmod)


def _norm_mod(x, r, gv, sh):
    return x * r * gv + sh


def _silu(x):
    return x * _sigmoid(x)


def _rstd(x):
    return lax.rsqrt(jnp.mean(x * x, axis=-1, keepdims=True) + EPS)


def _residual_epi(acc, x, g):
    xn = x + g * acc
    return acc, xn, _rstd(xn)


def _residual_norm_epi(acc, x, g, gv, sh):
    xn = x + g * acc
    r = _rstd(xn)
    return acc, xn, _norm_mod(xn, r, gv, sh), r


def _layer_fwd(x0, r1, h1, mod, gn1, gn2, nxt, W, rc, tri):
    S = x0.shape[0]
    sh1, sc1, g1m, sh2, sc2, g2m = [mod[i:i + 1] for i in range(N_MOD)]
    gv1 = gn1 * (1.0 + sc1)
    gv2 = gn2 * (1.0 + sc2)
    W.after = [h1]
    (p,) = _mm(h1, W["w_in"], outs=(BF16,), name="mm_in")
    yret, rstate = _ret_fwd(p, rc)
    ysb, sbc = _sb_fwd(p, tri)
    W.after = [ysb]
    (ya,) = _mm(yret, W["w_ret_out"], tm=512, a_ex=[(p, 'a', O_RG)],
                pro=lambda yr, g: _silu(g.astype(F32)) * yr.astype(F32), outs=(BF16,), name="mm_ret_out")
    yb, mg = _mm(ysb, W["w_sb_out"], o_ex=[(ya, 'o'), (p, 'o', O_GA), (p, 'o', O_GB)],
                 epi=lambda acc, a, ga, gb: (acc, _sigmoid(ga.astype(F32)) * a.astype(F32)
                                             + _sigmoid(gb.astype(F32)) * acc),
                 outs=(BF16, BF16), name="mm_sb_out")
    mo, x1, h2, r2 = _mm(mg, W["w_mix_out"], tm=512, o_ex=[(x0, 'o'), (g1m, 'n'), (gv2, 'n'), (sh2, 'n')],
                         epi=_residual_norm_epi, outs=(BF16, F32, BF16), cols=1, name="mm_mix_out")
    (act,) = _mm(h2, W["w_up"], epi=lambda acc: (jnp.maximum(acc, 0.0),), outs=(BF16,), name="mm_up")
    if nxt is None:
        dn, x2, r_out = _mm(act, W["w_down"], tm=512, pro=lambda a: a * a, o_ex=[(x1, 'o'), (g2m, 'n')],
                            epi=_residual_epi, outs=(BF16, F32), cols=1, name="mm_down_last")
        h_out = None
    else:
        dn, x2, h_out, r_out = _mm(act, W["w_down"], tm=512, pro=lambda a: a * a,
                                   o_ex=[(x1, 'o'), (g2m, 'n'), (nxt[0], 'n'), (nxt[1], 'n')],
                                   epi=_residual_norm_epi, outs=(BF16, F32, BF16), cols=1, name="mm_down")
    saved = dict(x0=x0, r1=r1, h1=h1, p=p, yret=yret, rstate=rstate, ysb=ysb, sbc=sbc, ya=ya, yb=yb, mg=mg, mo=mo, x1=x1,
                 r2=r2, h2=h2, act=act, dn=dn, gv1=gv1, gv2=gv2, mod=mod, gn1=gn1, gn2=gn2)
    return x2, r_out, h_out, saved


def _norm_bwd(dh, x, r, dres, gv, gn, extra_rows=(), extra_vecs=(), extra_fn=None, extra_outs=(), name="norm_bwd"):
    D = x.shape[1]
    ne = len(extra_rows)

    def fn(dh_t, x_t, dres_t, *rest):
        er, rest = rest[:ne], rest[ne:]
        gv_t = rest[0]
        ev, r_t = rest[1:-1], rest[-1]
        xh = x_t * r_t
        dxh = dh_t * gv_t
        dx = r_t * (dxh - xh * jnp.mean(dxh * xh, axis=-1, keepdims=True)) + dres_t
        base = (dx, dh_t, dh_t * xh)
        if extra_fn is None:
            return base
        return base + tuple(extra_fn(dx, *er, *ev))

    return _ew(fn, [dh, x, dres] + list(extra_rows), vecs=[gv] + list(extra_vecs), cols=[r],
               outs=[('row', D, F32), ('sum', D), ('sum', D)] + list(extra_outs), name=name)


def _gate_bwd(dx, dn, g):
    return dx * dn.astype(F32), dx * g


_GATE_OUTS = [('sum', D_MODEL), ('row', D_MODEL, BF16)]


def _layer_bwd(dx2, d_g2m, d_dn, sv, below, W, rc, tri, emit):
    mod = sv['mod']
    sh1, sc1, g1m, sh2, sc2, g2m = [mod[i:i + 1] for i in range(N_MOD)]
    D = D_MODEL
    p = sv['p']
    (d_up,) = _mm(d_dn, W["w_down"], tb=True, o_ex=[(sv['act'], 'o')],
                  epi=lambda acc, a: (acc * 2.0 * a.astype(F32),), outs=(BF16,), name="mm_down_dx")
    (gw_down,) = _mm(sv['act'], d_dn, ta=True, tk=DW_TK, pro=lambda a: a * a, outs=(BF16,), name="mm_down_dw")
    (gw_up,) = _mm(sv['h2'], d_up, ta=True, tk=DW_TK, outs=(BF16,), name="mm_up_dw")
    tok = emit(dict(w_down=gw_down, w_up=gw_up), "mlp")
    (d_h2,) = _mm(d_up, W["w_up"], tb=True, tk=2048, outs=(F32,), name="mm_up_dx", deps=[tok])
    dx1, d_sh2, s_h2, d_g1m, d_mo = _norm_bwd(
        d_h2, sv['x1'], sv['r2'], dx2, sv['gv2'], sv['gn2'],
        extra_rows=[sv['mo']], extra_vecs=[g1m],
        extra_fn=lambda dx, mo, g: (dx * mo.astype(F32), dx * g),
        extra_outs=[('sum', D), ('row', D, BF16)], name="norm_bwd_mlp")
    d_sc2 = sv['gn2'] * s_h2
    d_gn2 = (1.0 + sc2) * s_h2
    def mix_epi(acc, ya, yb, ga, gb):
        sa, sb = _sigmoid(ga.astype(F32)), _sigmoid(gb.astype(F32))
        return (acc * sa, acc * sb, acc * ya.astype(F32) * sa * (1.0 - sa), acc * yb.astype(F32) * sb * (1.0 - sb))

    d_ya, d_yb, d_ga, d_gb = _mm(d_mo, W["w_mix_out"], tb=True, tm=512,
                                 o_ex=[(sv['ya'], 'o'), (sv['yb'], 'o'), (p, 'o', O_GA), (p, 'o', O_GB)], epi=mix_epi,
                                 outs=(BF16,) * 4, name="mm_mix_dx")
    (gw_mix,) = _mm(sv['mg'], d_mo, ta=True, tk=DW_TK, outs=(BF16,), name="mm_mix_dw")

    def ro_epi(acc, g, yr):
        gf = g.astype(F32)
        s = _sigmoid(gf)
        return (acc * yr.astype(F32) * s * (1.0 + gf * (1.0 - s)), acc * gf * s)

    d_rg, d_yret = _mm(d_ya, W["w_ret_out"], tb=True, tm=512, o_ex=[(p, 'o', O_RG), (sv['yret'], 'o')], epi=ro_epi,
                       outs=(BF16, BF16), name="mm_ret_dx")
    (gw_ro,) = _mm(sv['yret'], d_ya, ta=True, tm=512, tk=DW_TK, a_ex=[(p, 'a', O_RG)],
                   pro=lambda yr, g: _silu(g.astype(F32)) * yr.astype(F32), outs=(BF16,), name="mm_ret_dw")
    (gw_so,) = _mm(sv['ysb'], d_yb, ta=True, tk=DW_TK, outs=(BF16,), name="mm_sb_dw")
    tok = emit(dict(w_mix_out=gw_mix, w_ret_out=gw_ro, w_sb_out=gw_so), "mix")
    (d_ysb,) = _mm(d_yb, W["w_sb_out"], tb=True, outs=(BF16,), name="mm_sb_dx", deps=[tok])
    d_sq, d_sk, d_sv = _sb_bwd(p, sv['sbc'], d_ysb, tri)
    d_ret = _ret_bwd(p, sv['rstate'], d_yret, rc)
    dp = [d_ret, d_rg, d_sq, d_sk, d_sv, d_ga, d_gb]
    (gw_in,) = _mm(sv['h1'], dp, ta=True, tk=1024, outs=(BF16,), name="mm_in_dw")
    tok = emit(dict(w_in=gw_in), "in")
    (d_h,) = _mm(dp, W["w_in"], tb=True, tk=1024, outs=(F32,), name="mm_in_dx", deps=[tok])
    if below is None:
        dx0, d_sh1, s_h1 = _norm_bwd(d_h, sv['x0'], sv['r1'], dx1, sv['gv1'], sv['gn1'], name="norm_bwd_mix")
        gate_below = (None, None)
    else:
        dx0, d_sh1, s_h1, *gate_below = _norm_bwd(
            d_h, sv['x0'], sv['r1'], dx1, sv['gv1'], sv['gn1'], extra_rows=[below['dn']],
            extra_vecs=[below['mod'][N_MOD - 1:N_MOD]], extra_fn=_gate_bwd, extra_outs=_GATE_OUTS,
            name="norm_bwd_mix_gate")
    d_sc1 = sv['gn1'] * s_h1
    d_gn1 = (1.0 + sc1) * s_h1
    d_mod = jnp.concatenate([d_sh1, d_sc1, d_g1m, d_sh2, d_sc2, d_g2m], axis=1)
    return dx0, gate_below, d_mod, d_gn1, d_gn2


def kernel(x, c, norm_mix_g, w_in, w_ret_out, w_sb_out, w_mix_out, norm_mlp_g, w_up, w_down, w_ada, b_ada, final_g, loss_target, m_norm_mix_g, m_w_in, m_w_ret_out, m_w_sb_out, m_w_mix_out, m_norm_mlp_g, m_w_up, m_w_down, m_w_ada, m_b_ada, m_final_g, v_norm_mix_g, v_w_in, v_w_ret_out, v_w_sb_out, v_w_mix_out, v_norm_mlp_g, v_w_up, v_w_down, v_w_ada, v_b_ada, v_final_g):
    S, D = x.shape[1], x.shape[2]
    x0 = x.reshape(S, D)
    tgt = loss_target.reshape(S, D)
    me = 4 * lax.axis_index("x") + 2 * lax.axis_index("y") + lax.axis_index("c")
    wts = dict(w_in=w_in, w_ret_out=w_ret_out, w_sb_out=w_sb_out, w_mix_out=w_mix_out, w_up=w_up, w_down=w_down)
    mts = dict(w_in=m_w_in, w_ret_out=m_w_ret_out, w_sb_out=m_w_sb_out, w_mix_out=m_w_mix_out, w_up=m_w_up, w_down=m_w_down)
    vts = dict(w_in=v_w_in, w_ret_out=v_w_ret_out, w_sb_out=v_w_sb_out, w_mix_out=v_w_mix_out, w_up=v_w_up, w_down=v_w_down)
    rc = _ret_consts(S)
    tri = _tri()

    (cact,) = _ew(lambda t: (_silu(t),), [jnp.pad(c, ((0, 7), (0, 0)))], outs=[('row', D, F32)], name="silu_c")
    (cact_all,) = _all_gather_lead([cact[0:1]], "gather_c")
    cact_all = cact_all.reshape(N_DEV, D)
    cact16 = jnp.pad(cact_all, ((0, 8), (0, 0)))
    n_ada = w_ada.shape[2]
    b_loc = lax.dynamic_slice_in_dim(b_ada, me * n_ada, n_ada, axis=1)
    mods = [_mod_partial(cact16, w_ada[l], b_loc[l:l + 1])[:N_DEV] for l in range(DEPTH)]
    modp = jnp.stack(mods, axis=1)
    (modr,) = _all_to_all_lead([modp], "scatter_mod")
    mod_full = jnp.transpose(modr, (1, 0, 2)).reshape(DEPTH, N_MOD, D)

    shards = {}
    for nm in _W_NAMES:
        w = wts[nm]
        (wb,) = _ew(lambda t: (t,), [w.reshape(-1, w.shape[-1])], outs=[('row', w.shape[-1], BF16)], name="cast_bf16")
        shards[nm] = wb.reshape(w.shape)
    rest = tuple(nm for nm in _W_NAMES if nm != "w_in")
    started, layer_groups = [modr], []
    for l in range(DEPTH):
        sh_l = {nm: shards[nm][l] for nm in _W_NAMES}
        groups = [(("w_in",), "%d_in" % l), (rest, "%d_rest" % l)] if l == 0 else [(_W_NAMES, "%d_all" % l)]
        layer_groups.append([])
        for names, tag in groups:
            layer_groups[-1].append((names, _gather_start(sh_l, names, tag, started[-1:])))
            started.append(layer_groups[-1][-1][1]['token'])
    layer_w = [_LayerWeights(g, started) for g in layer_groups]

    pre = [(norm_mix_g[l:l + 1] * (1.0 + mod_full[l][1:2]), mod_full[l][0:1]) for l in range(DEPTH)]

    def first(t, gv, sh):
        r = _rstd(t)
        return r, _norm_mod(t, r, gv, sh)

    xs = x0
    rs, hs = _ew(first, [x0], vecs=list(pre[0]), outs=[('col', F32), ('row', D, BF16)], name="row_rstd")
    saved = []
    for l in range(DEPTH):
        xs, rs, hs, sv = _layer_fwd(xs, rs, hs, mod_full[l], norm_mix_g[l:l + 1], norm_mlp_g[l:l + 1],
                                    pre[l + 1] if l + 1 < DEPTH else None, layer_w[l], rc, tri)
        sv['W'] = layer_w[l]
        saved.append(sv)

    fg = final_g.reshape(1, D)

    def head(xt, tg, dn, g, g2m, r):
        xh = xt * r
        e = xh * g - tg
        dy = e * (1.0 / D)
        dxh = dy * g
        dx = r * (dxh - xh * jnp.mean(dxh * xh, axis=-1, keepdims=True))
        return (dx, dy * xh, 0.5 * e * e * (1.0 / D)) + _gate_bwd(dx, dn, g2m)

    top = saved[DEPTH - 1]
    dxs, d_fg, loss_cols, *gate = _ew(head, [xs, tgt, top['dn']], vecs=[fg, top['mod'][N_MOD - 1:N_MOD]], cols=[rs],
                                      outs=[('row', D, F32), ('sum', D), ('sum', D)] + _GATE_OUTS, name="loss_head")

    small = [None] * DEPTH
    pending = []
    for l in reversed(range(DEPTH)):
        sv = saved[l]

        def emit(gw, tag, l=l):
            names = tuple(gw)
            pending.append((l, names, _scatter_start(gw, names, "%d_%s" % (l, tag))))
            return pending[-1][2]['token']

        dxs, gate, d_mod, d_gn1, d_gn2 = _layer_bwd(dxs, gate[0], gate[1], sv, saved[l - 1] if l else None,
                                                    sv['W'], rc, tri, emit)
        small[l] = (d_mod, d_gn1, d_gn2)
    grad_x = dxs.reshape(1, S, D)

    res = {}
    after = [dxs]
    for l, names, h in pending:
        for nm, landed in zip(names, _finish(h, after)):
            res[nm] = _adam_layer(landed, wts[nm], mts[nm], vts[nm], l, res.get(nm), "adam_layer")
        after = [res[names[-1]][0]]

    pack = jnp.concatenate([small[l][0] for l in range(DEPTH)] + [small[l][1] for l in range(DEPTH)]
                           + [small[l][2] for l in range(DEPTH)] + [d_fg, loss_cols], axis=1)
    (packs,) = _all_gather_lead([pack], "gather_small", deps=after)
    packs = packs.reshape(N_DEV, -1)
    o = 0
    dmod_all = []
    for l in range(DEPTH):
        dmod_all.append(packs[:, o:o + N_MOD * D]); o += N_MOD * D
    gn1_parts = packs[:, o:o + DEPTH * D].reshape(N_DEV, DEPTH, D); o += DEPTH * D
    gn2_parts = packs[:, o:o + DEPTH * D].reshape(N_DEV, DEPTH, D); o += DEPTH * D
    fg_parts = packs[:, o:o + D].reshape(N_DEV, 1, D); o += D
    loss_parts = packs[:, o:o + D]
    (loss_sum,) = _ew(lambda t: (t,), [loss_parts], outs=[('sum', D)], name="loss_sum")
    loss = jnp.sum(loss_sum)

    res["norm_mix_g"] = _adam(gn1_parts, norm_mix_g, m_norm_mix_g, v_norm_mix_g, "adam")
    res["norm_mlp_g"] = _adam(gn2_parts, norm_mlp_g, m_norm_mlp_g, v_norm_mlp_g, "adam")
    fgr = _adam(fg_parts, fg, m_final_g.reshape(1, D), v_final_g.reshape(1, D), "adam")
    res["final_g"] = [t.reshape(D) for t in fgr]
    bparts = jnp.stack(dmod_all, axis=1)
    res["b_ada"] = _adam(bparts, b_ada, m_b_ada, v_b_ada, "adam")
    cact_t = cact_all.T
    for l in range(DEPTH):
        dm_loc = lax.dynamic_slice_in_dim(dmod_all[l], me * n_ada, n_ada, axis=1)
        res["w_ada"] = _adam_layer(_ada_grad(cact_t, dm_loc), w_ada, m_w_ada, v_w_ada, l, res.get("w_ada"),
                                   "adam_layer")

    order = ['norm_mix_g', 'w_in', 'w_ret_out', 'w_sb_out', 'w_mix_out', 'norm_mlp_g', 'w_up', 'w_down', 'w_ada', 'b_ada', 'final_g']
    out = [loss, grad_x]
    for i in range(4):
        out += [res[nm][i] for nm in order]
    return tuple(out)
```

```python
import functools
import math

import jax
import jax.numpy as jnp
import numpy as np
from jax import lax
from jax.experimental import pallas as pl
from jax.experimental.pallas import tpu as pltpu

F32 = jnp.float32
BF16 = jnp.bfloat16

N_DEV = 8
D_MODEL = 1024
DEPTH = 2
RET_HEADS = 4
RET_QK = 256
RET_V = 512
RET_CHUNK = 128
ROPE_BASE = 10000.0
SB_HEADS = 16
SB_DIM = 64
D_FF = 4096
N_MOD = 6
EPS = 1e-6
GN_EPS = 1e-5
O_RQ, O_RK, O_RV, O_RG, O_SQ, O_SK, O_SV, O_GA, O_GB = 0, 1024, 2048, 4096, 6144, 7168, 8192, 9216, 10240
IN_W = 11264

ADAM_LR, ADAM_B1, ADAM_B2, ADAM_EPS, ADAM_WD, ADAM_STEP = 0.001, 0.9, 0.999, 1e-08, 0.01, 10

VMEM_LIMIT = 56 * 1024 * 1024
DW_TK = 2048


def _cparams(sem):
    return pltpu.CompilerParams(dimension_semantics=sem, vmem_limit_bytes=VMEM_LIMIT)


def _mm(a, b, *, ta=False, tb=False, tm=1024, tn=1024, tk=None, a_ex=(), pro=None, o_ex=(), epi=None,
        outs=(F32,), cols=0, name, deps=()):
    a_parts = list(a) if isinstance(a, (list, tuple)) else [a]
    b_parts = list(b) if isinstance(b, (list, tuple)) else [b]
    assert not (ta and len(a_parts) > 1) and not (tb and len(b_parts) > 1)
    if ta:
        K, M = a.shape
    else:
        M, K = a_parts[0].shape[0], sum(t.shape[1] for t in a_parts)
    N = b.shape[0] if tb else sum(t.shape[1] for t in b_parts)
    tm, tn, tk = min(tm, M), min(tn, N), K if tk is None else min(tk, K)
    assert M % tm == 0 and N % tn == 0 and K % tk == 0, (name, M, N, K, tm, tn, tk)
    nk = K // tk
    multi = len(a_parts) > 1 or len(b_parts) > 1

    def ranges(parts, t):
        out, o = [], 0
        for arr in parts:
            assert arr.shape[1] % t == 0
            out.append((o, o + arr.shape[1] // t))
            o += arr.shape[1] // t
        return out

    a_rng = ranges(a_parts, tk) if len(a_parts) > 1 else [(0, nk)]
    b_rng = ranges(b_parts, tn) if len(b_parts) > 1 else [(0, N // tn)]
    clip = lambda v, lo, hi: jnp.clip(v - lo, 0, hi - lo - 1)
    mine = lambda v, lo, hi, w: jnp.where((v >= lo) & (v < hi), w, 0)
    in_specs, args = [], []
    for arr, (lo, hi) in zip(a_parts, a_rng):
        in_specs.append(pl.BlockSpec((tk, tm), lambda i, j, k: (k, i)) if ta
                        else pl.BlockSpec((tm, tk), lambda i, j, k, lo=lo, hi=hi: (i, clip(k, lo, hi))))
        args.append(arr)
    for arr, (lo, hi) in zip(b_parts, b_rng):
        in_specs.append(pl.BlockSpec((tn, tk), lambda i, j, k: (j, k)) if tb
                        else pl.BlockSpec((tk, tn), lambda i, j, k, lo=lo, hi=hi: (mine(j, lo, hi, k), clip(j, lo, hi))))
        args.append(arr)
    npa, npb = len(a_parts), len(b_parts)
    for arr, kind, *off in a_ex:
        off = off[0] if off else 0
        if kind == 'a' and ta:
            assert off % tm == 0
            in_specs.append(pl.BlockSpec((tk, tm), lambda i, j, k, o=off // tm: (k, o + i)))
        elif kind == 'a':
            assert off % tk == 0
            in_specs.append(pl.BlockSpec((tm, tk), lambda i, j, k, o=off // tk: (i, o + k)))
        elif kind == 'k':
            in_specs.append(pl.BlockSpec((tk, 1), lambda i, j, k: (k, 0)) if ta
                            else pl.BlockSpec((1, tk), lambda i, j, k: (0, k)))
        else:
            in_specs.append(pl.BlockSpec((1, tm), lambda i, j, k: (0, i)) if ta
                            else pl.BlockSpec((tm, 1), lambda i, j, k: (i, 0)))
        args.append(arr)
    for arr, kind, *off in o_ex:
        off = off[0] if off else 0
        if kind == 'o':
            assert off % tn == 0
            in_specs.append(pl.BlockSpec((tm, tn), lambda i, j, k, o=off // tn: (i, o + j)))
        elif kind == 'n':
            in_specs.append(pl.BlockSpec((1, tn), lambda i, j, k: (0, j)))
        else:
            in_specs.append(pl.BlockSpec((tm, 1), lambda i, j, k: (i, 0)))
        args.append(arr)
    for arr in deps:
        in_specs.append(pl.BlockSpec(memory_space=pl.ANY))
        args.append(arr)
    assert cols == 0 or N == tn
    na, no, nout, nd = len(a_ex), len(o_ex), len(outs) + cols, len(deps)
    dims = (((0 if ta else 1,), (1 if tb else 0,)), ((), ()))

    def body(*refs):
        a_refs, b_refs = refs[:npa], refs[npa:npa + npb]
        n0 = npa + npb
        aex = refs[n0:n0 + na]
        oex = refs[n0 + na:n0 + na + no]
        out_refs = refs[n0 + na + no + nd:n0 + na + no + nd + nout]

        def product(a_ref, b_ref):
            at = a_ref[...]
            if pro is not None:
                at = pro(at, *[r[...] for r in aex])
            return lax.dot_general(at.astype(BF16), b_ref[...].astype(BF16), dims, preferred_element_type=F32)

        def finish(res):
            vals = epi(res, *[r[...] for r in oex]) if epi is not None else (res,)
            for o_ref, v in zip(out_refs, vals):
                o_ref[...] = v.astype(o_ref.dtype)

        if nk == 1 and not multi:
            finish(product(a_refs[0], b_refs[0]))
            return
        acc = refs[-1]
        j, k = pl.program_id(1), pl.program_id(2)
        if multi:
            @pl.when(k == 0)
            def _():
                acc[...] = jnp.zeros_like(acc)

            for a_ref, (alo, ahi) in zip(a_refs, a_rng):
                for b_ref, (blo, bhi) in zip(b_refs, b_rng):
                    @pl.when((k >= alo) & (k < ahi) & (j >= blo) & (j < bhi))
                    def _():
                        acc[...] += product(a_ref, b_ref)
        else:
            @pl.when(k == 0)
            def _():
                acc[...] = product(a_refs[0], b_refs[0])

            @pl.when(k > 0)
            def _():
                acc[...] += product(a_refs[0], b_refs[0])

        @pl.when(k == nk - 1)
        def _():
            finish(acc[...])

    res = pl.pallas_call(
        body, name=name, grid=(M // tm, N // tn, nk), in_specs=in_specs,
        out_specs=[pl.BlockSpec((tm, tn), lambda i, j, k: (i, j)) for _ in outs]
        + [pl.BlockSpec((tm, 1), lambda i, j, k: (i, 0))] * cols,
        out_shape=[jax.ShapeDtypeStruct((M, N), dt) for dt in outs] + [jax.ShapeDtypeStruct((M, 1), F32)] * cols,
        scratch_shapes=[pltpu.VMEM((tm, tn), F32)] if nk > 1 or multi else [],
        compiler_params=_cparams(("parallel", "parallel", "arbitrary")),
    )(*args)
    return res


def _ew(fn, rows, vecs=(), cols=(), outs=(), tr=256, name=None):
    S = rows[0].shape[0]
    tr = min(tr, S)
    assert S % tr == 0
    in_specs, args = [], []
    for r in rows:
        in_specs.append(pl.BlockSpec((tr, r.shape[1]), lambda i: (i, 0)))
        args.append(r)
    for v in vecs:
        in_specs.append(pl.BlockSpec((1, v.shape[1]), lambda i: (0, 0)))
        args.append(v)
    for c in cols:
        in_specs.append(pl.BlockSpec((tr, 1), lambda i: (i, 0)))
        args.append(c)
    out_specs, out_shape = [], []
    for o in outs:
        if o[0] == 'row':
            out_specs.append(pl.BlockSpec((tr, o[1]), lambda i: (i, 0)))
            out_shape.append(jax.ShapeDtypeStruct((S, o[1]), o[2]))
        elif o[0] == 'sum':
            out_specs.append(pl.BlockSpec((1, o[1]), lambda i: (0, 0)))
            out_shape.append(jax.ShapeDtypeStruct((1, o[1]), F32))
        else:
            out_specs.append(pl.BlockSpec((tr, 1), lambda i: (i, 0)))
            out_shape.append(jax.ShapeDtypeStruct((S, 1), o[1]))
    nin = len(args)

    def body(*refs):
        i = pl.program_id(0)
        vals = fn(*[r[...] for r in refs[:nin]])
        for o, o_ref, v in zip(outs, refs[nin:], vals):
            if o[0] == 'sum':
                @pl.when(i == 0)
                def _():
                    o_ref[...] = jnp.zeros_like(o_ref)
                o_ref[...] += jnp.sum(v.astype(F32), axis=0, keepdims=True)
            else:
                o_ref[...] = v.astype(o_ref.dtype)

    return pl.pallas_call(
        body, name=name, grid=(S // tr,), in_specs=in_specs, out_specs=out_specs, out_shape=out_shape,
        compiler_params=_cparams(("arbitrary",)),
    )(*args)


def _sigmoid(x):
    return 1.0 / (1.0 + jnp.exp(-x))


def _ret_consts(S):
    h = np.arange(RET_HEADS, dtype=np.float64)
    log_gamma = np.log1p(-np.power(2.0, -5.0 - h))
    idx = np.arange(RET_CHUNK, dtype=np.float64)
    rel = idx[:, None] - idx[None, :]
    decay = np.where(rel >= 0, np.exp(np.maximum(rel, 0.0) * log_gamma[:, None, None]), 0.0)
    xi = np.exp((idx + 1.0) * log_gamma[:, None])[:, :, None]
    zeta = np.exp((RET_CHUNK - 1.0 - idx) * log_gamma[:, None])[:, :, None]
    gamma_c = np.exp(RET_CHUNK * log_gamma)[:, None, None]
    half = RET_QK // 2
    inv_freq = np.power(ROPE_BASE, -np.arange(half, dtype=np.float64) / half).astype(np.float32)
    ang = np.arange(S, dtype=np.float32)[:, None] * inv_freq[None, :]
    f = lambda t: jnp.asarray(t, F32)
    return dict(decay=f(decay), xi=f(xi), zeta=f(zeta), gc=f(gamma_c), cos=f(np.cos(ang)), sin=f(np.sin(ang)))


def _rot(t, cos, sin):
    half = RET_QK // 2
    t1, t2 = t[:, :half], t[:, half:]
    return jnp.concatenate([t1 * cos - t2 * sin, t1 * sin + t2 * cos], axis=-1)


def _rot_inv(t, cos, sin):
    half = RET_QK // 2
    t1, t2 = t[:, :half], t[:, half:]
    return jnp.concatenate([t1 * cos + t2 * sin, t2 * cos - t1 * sin], axis=-1)


_NT = (((1,), (1,)), ((), ()))
_TN = (((0,), (0,)), ((), ()))


def _dot(a, b):
    return jnp.dot(a, b, preferred_element_type=F32)


def _dot_nt(a, b):
    return lax.dot_general(a, b, _NT, preferred_element_type=F32)


def _dot_tn(a, b):
    return lax.dot_general(a, b, _TN, preferred_element_type=F32)


_QW, _VW = RET_HEADS * RET_QK, RET_HEADS * RET_V
_HEADS = range(RET_HEADS)


def _ret_in_specs(C, rev, NC):
    n_of = (lambda n: NC - 1 - n) if rev else (lambda n: n)
    whole3 = lambda n: (0, 0, 0)
    return [
        pl.BlockSpec((C, _QW), lambda n: (n_of(n), O_RQ // _QW)),
        pl.BlockSpec((C, _QW), lambda n: (n_of(n), O_RK // _QW)),
        pl.BlockSpec((C, _VW), lambda n: (n_of(n), O_RV // _VW)),
        pl.BlockSpec((C, RET_QK // 2), lambda n: (n_of(n), 0)),
        pl.BlockSpec((C, RET_QK // 2), lambda n: (n_of(n), 0)),
        pl.BlockSpec((RET_HEADS, C, C), whole3),
        pl.BlockSpec((RET_HEADS, C, 1), whole3),
        pl.BlockSpec((RET_HEADS, C, 1), whole3),
        pl.BlockSpec((RET_HEADS, 1, 1), whole3),
    ]


def _qk_heads(q_ref, k_ref, cos, sin):
    qs, kfs = [], []
    for h in _HEADS:
        cols = slice(h * RET_QK, (h + 1) * RET_QK)
        qs.append(_rot(q_ref[:, cols].astype(F32), cos, sin).astype(BF16))
        kfs.append(_rot(k_ref[:, cols].astype(F32), cos, sin) * (RET_QK ** -0.5))
    return qs, kfs


def _ret_fwd(p, rc):
    S = p.shape[0]
    C = RET_CHUNK
    NC = S // C

    def body(q_ref, k_ref, v_ref, cos_ref, sin_ref, dec_ref, xi_ref, zeta_ref, gc_ref, y_ref, rs_ref, r_acc):
        n = pl.program_id(0)

        @pl.when(n == 0)
        def _():
            r_acc[...] = jnp.zeros_like(r_acc)

        cos, sin = cos_ref[...], sin_ref[...]
        qs, kfs = _qk_heads(q_ref, k_ref, cos, sin)
        vs = [v_ref[:, h * RET_V:(h + 1) * RET_V] for h in _HEADS]
        rbs = [r_acc[h].astype(BF16) for h in _HEADS]
        for h in _HEADS:
            rs_ref[h, 0] = rbs[h]
        ss = [(_dot_nt(qs[h], kfs[h].astype(BF16)) * dec_ref[h]).astype(BF16) for h in _HEADS]
        os = [_dot(ss[h], vs[h]) + _dot(qs[h], rbs[h]) * xi_ref[h] for h in _HEADS]
        for h in _HEADS:
            o = os[h]
            mu = jnp.mean(o, axis=-1, keepdims=True)
            var = jnp.mean(jnp.square(o - mu), axis=-1, keepdims=True)
            y_ref[:, h * RET_V:(h + 1) * RET_V] = ((o - mu) * lax.rsqrt(var + GN_EPS)).astype(y_ref.dtype)
        for h in _HEADS:
            kz = (kfs[h] * zeta_ref[h]).astype(BF16)
            r_acc[h] = r_acc[h] * gc_ref[h] + _dot_tn(kz, vs[h])

    return pl.pallas_call(
        body, name="ret_fwd", grid=(NC,), in_specs=_ret_in_specs(C, False, NC),
        out_specs=[pl.BlockSpec((C, _VW), lambda n: (n, 0)),
                   pl.BlockSpec((RET_HEADS, 1, RET_QK, RET_V), lambda n: (0, n, 0, 0))],
        out_shape=[jax.ShapeDtypeStruct((S, _VW), BF16),
                   jax.ShapeDtypeStruct((RET_HEADS, NC, RET_QK, RET_V), BF16)],
        scratch_shapes=[pltpu.VMEM((RET_HEADS, RET_QK, RET_V), F32)],
        compiler_params=_cparams(("arbitrary",)),
    )(p, p, p, rc['cos'], rc['sin'], rc['decay'], rc['xi'], rc['zeta'], rc['gc'])


def _ret_bwd(p, rstate, dy, rc):
    S = p.shape[0]
    C = RET_CHUNK
    NC = S // C

    def body(q_ref, k_ref, v_ref, cos_ref, sin_ref, dec_ref, xi_ref, zeta_ref, gc_ref, rs_ref, dy_ref,
             d_ref, dr_acc):
        dq_ref, dk_ref, dv_ref = d_ref.at[:, 0:_QW], d_ref.at[:, _QW:2 * _QW], d_ref.at[:, 2 * _QW:2 * _QW + _VW]
        t = pl.program_id(0)

        @pl.when(t == 0)
        def _():
            dr_acc[...] = jnp.zeros_like(dr_acc)

        cos, sin = cos_ref[...], sin_ref[...]
        qs, kfs = _qk_heads(q_ref, k_ref, cos, sin)
        ks = [kf.astype(BF16) for kf in kfs]
        vs = [v_ref[:, h * RET_V:(h + 1) * RET_V] for h in _HEADS]
        rbs = [rs_ref[h, 0] for h in _HEADS]
        ss = [(_dot_nt(qs[h], ks[h]) * dec_ref[h]).astype(BF16) for h in _HEADS]
        os = [_dot(ss[h], vs[h]) + _dot(qs[h], rbs[h]) * xi_ref[h] for h in _HEADS]
        dobs, doxis = [], []
        for h in _HEADS:
            o = os[h]
            mu = jnp.mean(o, axis=-1, keepdims=True)
            var = jnp.mean(jnp.square(o - mu), axis=-1, keepdims=True)
            rstd = lax.rsqrt(var + GN_EPS)
            yh = (o - mu) * rstd
            dyf = dy_ref[:, h * RET_V:(h + 1) * RET_V].astype(F32)
            do = (dyf - jnp.mean(dyf, axis=-1, keepdims=True)
                  - yh * jnp.mean(dyf * yh, axis=-1, keepdims=True)) * rstd
            dobs.append(do.astype(BF16))
            doxis.append((do * xi_ref[h]).astype(BF16))
        drbs = [dr_acc[h].astype(BF16) for h in _HEADS]
        dss = [(_dot_nt(dobs[h], vs[h]) * dec_ref[h]).astype(BF16) for h in _HEADS]
        for h in _HEADS:
            dq = _dot(dss[h], ks[h]) + _dot_nt(doxis[h], rbs[h])
            dq_ref[:, h * RET_QK:(h + 1) * RET_QK] = _rot_inv(dq, cos, sin).astype(dq_ref.dtype)
        for h in _HEADS:
            dk = _dot_tn(dss[h], qs[h]) + _dot_nt(vs[h], drbs[h]) * zeta_ref[h]
            dk_ref[:, h * RET_QK:(h + 1) * RET_QK] = (_rot_inv(dk, cos, sin) * (RET_QK ** -0.5)).astype(dk_ref.dtype)
        for h in _HEADS:
            kz = (kfs[h] * zeta_ref[h]).astype(BF16)
            dv = _dot_tn(ss[h], dobs[h]) + _dot(kz, drbs[h])
            dv_ref[:, h * RET_V:(h + 1) * RET_V] = dv.astype(dv_ref.dtype)
        for h in _HEADS:
            dr_acc[h] = dr_acc[h] * gc_ref[h] + _dot_tn(qs[h], doxis[h])

    rn = lambda n: NC - 1 - n
    in_specs = _ret_in_specs(C, True, NC) + [
        pl.BlockSpec((RET_HEADS, 1, RET_QK, RET_V), lambda n: (0, rn(n), 0, 0)),
        pl.BlockSpec((C, _VW), lambda n: (rn(n), 0)),
    ]
    return pl.pallas_call(
        body, name="ret_bwd", grid=(NC,), in_specs=in_specs,
        out_specs=pl.BlockSpec((C, 2 * _QW + _VW), lambda n: (rn(n), 0)),
        out_shape=jax.ShapeDtypeStruct((S, 2 * _QW + _VW), BF16),
        scratch_shapes=[pltpu.VMEM((RET_HEADS, RET_QK, RET_V), F32)],
        compiler_params=_cparams(("arbitrary",)),
    )(p, p, p, rc['cos'], rc['sin'], rc['decay'], rc['xi'], rc['zeta'], rc['gc'], rstate, dy)


SB_T = 256
SB_SCALE = SB_DIM ** -0.5


def _tri():
    j = np.arange(SB_T)
    after = (j[:, None] > j[None, :]).astype(np.float32)
    upto = (j[:, None] <= j[None, :]).astype(np.float32)
    return jnp.asarray(np.stack([after, upto]), BF16)


def _softplus_parts(z):
    neg_abs = lax.bitcast_convert_type(lax.bitcast_convert_type(z, jnp.uint32) | jnp.uint32(0x80000000), F32)
    e = jnp.exp(neg_abs)
    return jnp.maximum(z, 0.0) + jnp.log(1.0 + e), e


def _sb_fwd(p, tri):
    S = p.shape[0]
    T = min(SB_T, S)
    NQ = S // T
    assert NQ <= 128
    qb, kb, vb = O_SQ // 128, O_SK // 128, O_SV // 128

    def body(q_ref, k_ref, v_ref, tri_ref, o_ref, cs_ref, o_acc, run, zbuf, abuf):
        i = pl.program_id(1)
        lane = lax.broadcasted_iota(jnp.int32, (1, 128), 1)
        tri_after = tri_ref[0]
        qs = [jnp.where((lane >= 64) if hh else (lane < 64), q_ref[...], jnp.zeros_like(q_ref[...]))
              * jnp.asarray(SB_SCALE, BF16) for hh in range(2)]
        cs_ref[...] = jnp.zeros_like(cs_ref)
        o_acc[...] = jnp.zeros_like(o_acc)
        run[...] = jnp.zeros_like(run)

        def kv(ref, j):
            return ref[pl.ds(pl.multiple_of(j * T, T), T), :]

        for hh in range(2):
            zbuf[hh] = _dot_nt(qs[hh], kv(k_ref, i))

        def block(t, diagonal):
            j = i - t
            if diagonal:
                msk = lax.broadcasted_iota(jnp.int32, (T, T), 1) < lax.broadcasted_iota(jnp.int32, (T, T), 0)
            if not diagonal:
                av = [_dot(abuf[hh], kv(v_ref, j + 1)) for hh in range(2)]
            lss, exs, tot, zn = [], [], [], []
            for hh in range(2):
                z = zbuf[hh]
                sp, _ = _softplus_parts(z)
                lss.append(z - sp)
                if diagonal:
                    sp = jnp.where(msk, sp, 0.0)
                exs.append(_dot(sp.astype(BF16), tri_after))
                tot.append(sp[:, 0:1])
                zn.append(_dot_nt(qs[hh], kv(k_ref, jnp.maximum(j - 1, 0))))
            for hh in range(2):
                csl = slice(hh * 128, (hh + 1) * 128)
                cs = run[hh]
                a = jnp.exp(lss[hh] - exs[hh] - cs)
                if diagonal:
                    a = jnp.where(msk, a, 0.0)
                abuf[hh] = a.astype(BF16)
                cs_ref[:, csl] = jnp.where(lane == j, cs, cs_ref[:, csl])
                run[hh] = cs + exs[hh][:, 0:1] + tot[hh]
            for hh in range(2):
                if not diagonal:
                    o_acc[hh] += av[hh]
                zbuf[hh] = zn[hh]

        block(0, True)

        def step(t, carry):
            block(t, False)
            return carry

        lax.fori_loop(1, i + 1, step, 0)
        o_ref[...] = jnp.where(lane < 64, o_acc[0] + _dot(abuf[0], kv(v_ref, 0)),
                               o_acc[1] + _dot(abuf[1], kv(v_ref, 0))).astype(o_ref.dtype)

    return pl.pallas_call(
        body, name="sb_fwd", grid=(SB_HEADS // 2, NQ),
        scratch_shapes=[pltpu.VMEM((2, T, 128), F32), pltpu.VMEM((2, T, 1), F32), pltpu.VMEM((2, T, T), F32),
                        pltpu.VMEM((2, T, T), BF16)],
        in_specs=[pl.BlockSpec((T, 128), lambda h, i: (i, qb + h)),
                  pl.BlockSpec((S, 128), lambda h, i: (0, kb + h)),
                  pl.BlockSpec((S, 128), lambda h, i: (0, vb + h)),
                  pl.BlockSpec((1, T, T), lambda h, i: (0, 0, 0))],
        out_specs=[pl.BlockSpec((T, 128), lambda h, i: (i, h)),
                   pl.BlockSpec((T, 256), lambda h, i: (i, h))],
        out_shape=[jax.ShapeDtypeStruct((S, SB_HEADS * SB_DIM), BF16),
                   jax.ShapeDtypeStruct((S, SB_HEADS * 128), F32)],
        compiler_params=_cparams(("parallel", "arbitrary")),
    )(p, p, p, tri)


def _sb_bwd(p, carries, dy, tri):
    S = p.shape[0]
    T = min(SB_T, S)
    NQ = S // T
    qb, kb, vb = O_SQ // 128, O_SK // 128, O_SV // 128

    def body(q_ref, k_ref, v_ref, cs_ref, dy_ref, tri_ref, dq_ref, dk_ref, dv_ref, dk_acc, dv_acc, dq_acc, run,
             zbuf, dabuf, dzbuf, abuf):
        i = pl.program_id(1)

        @pl.when(i == 0)
        def _():
            dk_acc[...] = jnp.zeros_like(dk_acc)
            dv_acc[...] = jnp.zeros_like(dv_acc)

        lane = lax.broadcasted_iota(jnp.int32, (1, 128), 1)
        tri_after, tri_upto = tri_ref[0], tri_ref[1]
        hms = [(lane >= 64) if hh else (lane < 64) for hh in range(2)]
        qs = [jnp.where(hm, q_ref[...], jnp.zeros_like(q_ref[...])) * jnp.asarray(SB_SCALE, BF16) for hm in hms]
        dos = [jnp.where(hm, dy_ref[...], jnp.zeros_like(dy_ref[...])) for hm in hms]
        qst = [t.T for t in qs]
        dost = [t.T for t in dos]
        dq_acc[...] = jnp.zeros_like(dq_acc)
        run[...] = jnp.zeros_like(run)
        dzbuf[...] = jnp.zeros_like(dzbuf)
        abuf[...] = jnp.zeros_like(abuf)

        def kv(ref, j):
            return ref[pl.ds(pl.multiple_of(j * T, T), T), :]

        def flush(jp):
            kp = kv(k_ref, jp)
            dq_add = [_dot(dzbuf[hh], kp) for hh in range(2)]
            dk_add = _dot(qst[0], dzbuf[0]) + _dot(qst[1], dzbuf[1])
            dv_add = _dot(dost[0], abuf[0]) + _dot(dost[1], abuf[1])
            return dq_add, dk_add, dv_add

        def apply(jp, adds):
            dq_add, dk_add, dv_add = adds
            cols = pl.ds(pl.multiple_of(jp * T, T), T)
            for hh in range(2):
                dq_acc[hh] += dq_add[hh]
            dk_acc[:, cols] += dk_add
            dv_acc[:, cols] += dv_add

        for hh in range(2):
            zbuf[hh] = _dot_nt(qs[hh], kv(k_ref, 0))
            dabuf[hh] = _dot_nt(dos[hh], kv(v_ref, 0))

        def block(j, diagonal):
            jp = jnp.maximum(j - 1, 0)
            if diagonal:
                msk = lax.broadcasted_iota(jnp.int32, (T, T), 1) < lax.broadcasted_iota(jnp.int32, (T, T), 0)
            kp = kv(k_ref, jp)
            dq_add = [_dot(dzbuf[hh], kp) for hh in range(2)]
            sigs, lss, exs, zn, dan, dk_part, dv_part = [], [], [], [], [], [], []
            for hh in range(2):
                z = zbuf[hh]
                sp, _ = _softplus_parts(z)
                lss.append(z - sp)
                sigs.append(jnp.exp(lss[hh]))
                if diagonal:
                    sp = jnp.where(msk, sp, 0.0)
                exs.append(_dot(sp.astype(BF16), tri_after))
                if not diagonal:
                    zn.append(_dot_nt(qs[hh], kv(k_ref, j + 1)))
                dk_part.append(_dot(qst[hh], dzbuf[hh]))
            pgs, gs = [], []
            for hh in range(2):
                csl = slice(hh * 128, (hh + 1) * 128)
                cs = jnp.sum(jnp.where(lane == j, cs_ref[:, csl], 0.0), axis=-1, keepdims=True)
                a = jnp.exp(lss[hh] - exs[hh] - cs)
                if diagonal:
                    a = jnp.where(msk, a, 0.0)
                abuf_new = a.astype(BF16)
                g = a * dabuf[hh]
                gs.append((g, abuf_new))
                pgs.append(_dot(g.astype(BF16), tri_upto))
                if not diagonal:
                    dan.append(_dot_nt(dos[hh], kv(v_ref, j + 1)))
                dv_part.append(_dot(dost[hh], abuf[hh]))
            adds = (dq_add, dk_part[0] + dk_part[1], dv_part[0] + dv_part[1])
            for hh in range(2):
                g, abuf_new = gs[hh]
                cg = run[hh]
                dz = g - sigs[hh] * (cg + pgs[hh])
                if diagonal:
                    dz = jnp.where(msk, dz, 0.0)
                run[hh] = cg + pgs[hh][:, T - 1:T]
                dzbuf[hh] = dz.astype(BF16)
                abuf[hh] = abuf_new
            apply(jp, adds)
            if not diagonal:
                for hh in range(2):
                    zbuf[hh] = zn[hh]
                    dabuf[hh] = dan[hh]

        def step(j, carry):
            block(j, False)
            return carry

        lax.fori_loop(0, i, step, 0)
        block(i, True)
        apply(i, flush(i))
        dq_ref[...] = (jnp.where(lane < 64, dq_acc[0], dq_acc[1]) * SB_SCALE).astype(dq_ref.dtype)

        @pl.when(i == NQ - 1)
        def _():
            dk_ref[...] = dk_acc[...].T.astype(dk_ref.dtype)
            dv_ref[...] = dv_acc[...].T.astype(dv_ref.dtype)

    W = SB_HEADS * SB_DIM
    return pl.pallas_call(
        body, name="sb_bwd", grid=(SB_HEADS // 2, NQ),
        in_specs=[pl.BlockSpec((T, 128), lambda h, i: (i, qb + h)),
                  pl.BlockSpec((S, 128), lambda h, i: (0, kb + h)),
                  pl.BlockSpec((S, 128), lambda h, i: (0, vb + h)),
                  pl.BlockSpec((T, 256), lambda h, i: (i, h)),
                  pl.BlockSpec((T, 128), lambda h, i: (i, h)),
                  pl.BlockSpec((2, T, T), lambda h, i: (0, 0, 0))],
        out_specs=[pl.BlockSpec((T, 128), lambda h, i: (i, h)),
                   pl.BlockSpec((S, 128), lambda h, i: (0, h)),
                   pl.BlockSpec((S, 128), lambda h, i: (0, h))],
        out_shape=[jax.ShapeDtypeStruct((S, W), BF16)] * 3,
        scratch_shapes=[pltpu.VMEM((128, S), F32), pltpu.VMEM((128, S), F32), pltpu.VMEM((2, T, 128), F32),
                        pltpu.VMEM((2, T, 1), F32), pltpu.VMEM((2, T, T), F32), pltpu.VMEM((2, T, T), F32),
                        pltpu.VMEM((2, T, T), BF16), pltpu.VMEM((2, T, T), BF16)],
        compiler_params=_cparams(("parallel", "arbitrary")),
    )(p, p, p, carries, dy, tri)


def _exchange(srcs, out_shapes, src_slice, dst_slice, name, deps=()):
    n, nd = len(srcs), len(deps)

    def body(*refs):
        ins, outs = refs[:n], refs[n + nd:2 * n + nd]
        send_sems, recv_sems, loc_sems = refs[2 * n + nd:]
        x, y, c = lax.axis_index("x"), lax.axis_index("y"), lax.axis_index("c")
        me = 4 * x + 2 * y + c
        local = [pltpu.make_async_copy(src_slice(t, ins[t], me), dst_slice(t, outs[t], me), loc_sems.at[t])
                 for t in range(n)]
        for cp in local:
            cp.start()
        sends, recvs = [], []
        for k in (1, 2, 4, 6, 3, 5, 7):
            px = 1 - x if k & 4 else x
            py = 1 - y if k & 2 else y
            pc = 1 - c if k & 1 else c
            peer = 4 * px + 2 * py + pc
            for t in range(n):
                s = t * 7 + k - 1
                sends.append(pltpu.make_async_remote_copy(
                    src_ref=src_slice(t, ins[t], peer), dst_ref=dst_slice(t, outs[t], me),
                    send_sem=send_sems.at[s], recv_sem=recv_sems.at[s],
                    device_id=(px, py, pc), device_id_type=pl.DeviceIdType.MESH))
                recvs.append(pltpu.make_async_remote_copy(
                    src_ref=src_slice(t, ins[t], me), dst_ref=dst_slice(t, outs[t], peer),
                    send_sem=send_sems.at[s], recv_sem=recv_sems.at[s],
                    device_id=(px, py, pc), device_id_type=pl.DeviceIdType.MESH))
        for cp in sends:
            cp.start()
        for cp in recvs:
            cp.wait_recv()
        for cp in sends:
            cp.wait_send()
        for cp in local:
            cp.wait()

    anyspec = pl.BlockSpec(memory_space=pl.ANY)
    return pl.pallas_call(
        body, name=name, in_specs=[anyspec] * (n + nd), out_specs=[anyspec] * n,
        out_shape=[jax.ShapeDtypeStruct(s, d) for s, d in out_shapes],
        scratch_shapes=[pltpu.SemaphoreType.DMA((7 * n,)), pltpu.SemaphoreType.DMA((7 * n,)),
                        pltpu.SemaphoreType.DMA((n,))],
    )(*srcs, *deps)


def _all_gather_lead(xs, name, deps=()):
    return _exchange(
        xs, [((N_DEV,) + x.shape, x.dtype) for x in xs],
        lambda t, ref, peer: ref, lambda t, ref, who: ref.at[who], name, deps)


def _all_to_all_lead(xs, name):
    return _exchange(
        xs, [(x.shape, x.dtype) for x in xs],
        lambda t, ref, peer: ref.at[peer], lambda t, ref, who: ref.at[who], name)


_W_AXIS = {"w_in": 1, "w_ret_out": 0, "w_sb_out": 0, "w_mix_out": 0, "w_up": 1, "w_down": 0}
_W_NAMES = tuple(_W_AXIS)


def _window(ref, axis, who, width, count=1):
    start = pl.multiple_of(who * width, width)
    return ref.at[pl.ds(start, count * width), :] if axis == 0 else ref.at[:, pl.ds(start, count * width)]


_HBM = pl.BlockSpec(memory_space=pltpu.HBM)
_SEM = pl.BlockSpec(memory_space=pltpu.SEMAPHORE)
_EFFECT = pltpu.SideEffectType.DATAFLOW_SIDE_EFFECTING


_ALL_PEERS = (0, 1, 2, 4, 6, 3, 5, 7)
_SAME_CORE = (0, 2, 4, 6)


def _exchange_start(srcs, shapes, src_slice, dst_slice, name, deps=(), ks=_ALL_PEERS):
    n, nd = len(srcs), len(deps)
    lands = [pltpu.with_memory_space_constraint(lax.empty(s, d), pltpu.HBM) for s, d in shapes]

    def body(*refs):
        ins, lnd = refs[:n], refs[n:2 * n]
        sems = refs[2 * n + nd:4 * n + nd]
        token = refs[6 * n + nd]
        x, y, c = lax.axis_index("x"), lax.axis_index("y"), lax.axis_index("c")
        me = 4 * x + 2 * y + c
        for k in ks:
            px = 1 - x if k & 4 else x
            py = 1 - y if k & 2 else y
            pc = 1 - c if k & 1 else c
            peer = 4 * px + 2 * py + pc
            for t in range(n):
                pltpu.make_async_remote_copy(
                    src_ref=src_slice(t, ins[t], peer), dst_ref=dst_slice(t, lnd[t], me),
                    send_sem=sems[2 * t], recv_sem=sems[2 * t + 1],
                    device_id=(px, py, pc), device_id_type=pl.DeviceIdType.MESH).start()
        token[...] = jnp.zeros_like(token)

    res = pl.pallas_call(
        body, name=name, in_specs=[_HBM] * (2 * n) + [pl.BlockSpec(memory_space=pl.ANY)] * nd,
        out_specs=[_SEM] * (2 * n) + [_HBM] * (2 * n) + [pl.BlockSpec(memory_space=pltpu.VMEM)],
        out_shape=[pltpu.SemaphoreType.DMA(())] * (2 * n) + [pltpu.HBM(s.shape, s.dtype) for s in srcs]
        + [pltpu.HBM(s.shape, s.dtype) for s in lands] + [jax.ShapeDtypeStruct((8, 128), F32)],
        input_output_aliases={t: 2 * n + t for t in range(2 * n)},
        compiler_params=pltpu.CompilerParams(has_side_effects=_EFFECT),
    )(*[pltpu.with_memory_space_constraint(s, pltpu.HBM) for s in srcs], *lands, *deps)
    return dict(n=n, sems=res[:2 * n], srcs=res[2 * n:3 * n], lands=res[3 * n:4 * n], token=res[4 * n])


def _exchange_wait(h, after, name):
    n, ns = h['n'], len(h['srcs'])
    span = h.get('span', lambda t, ref: ref)

    def body(*refs):
        lnd = refs[ns:ns + n]
        sems = refs[ns + n:ns + 3 * n]
        x, y, c = lax.axis_index("x"), lax.axis_index("y"), lax.axis_index("c")
        for t in range(n):
            w = span(t, lnd[t])
            cp = pltpu.make_async_remote_copy(src_ref=w, dst_ref=w, send_sem=sems[2 * t], recv_sem=sems[2 * t + 1],
                                              device_id=(x, y, 1 - c), device_id_type=pl.DeviceIdType.MESH)
            cp.wait_send()
            cp.wait_recv()

    after = list(after)
    res = pl.pallas_call(
        body, name=name,
        in_specs=[_HBM] * (ns + n) + [_SEM] * (2 * n) + [pl.BlockSpec(memory_space=pl.ANY)] * len(after),
        out_specs=[_HBM] * (ns + n),
        out_shape=[pltpu.HBM(s.shape, s.dtype) for s in h['srcs']] + [pltpu.HBM(s.shape, s.dtype) for s in h['lands']],
        input_output_aliases={t: t for t in range(ns + n)},
        compiler_params=pltpu.CompilerParams(has_side_effects=_EFFECT),
    )(*h['srcs'], *h['lands'], *h['sems'], *after)
    return list(res[ns:])


def _sibling_start(lands, win, name):
    n = len(lands)

    def body(*refs):
        lnd = refs[:n]
        sems = refs[n:3 * n]
        token = refs[4 * n]
        x, y, c = lax.axis_index("x"), lax.axis_index("y"), lax.axis_index("c")
        for ox in (x, 1 - x):
            for oy in (y, 1 - y):
                owner = 4 * ox + 2 * oy + c
                for t in range(n):
                    w = win(t, lnd[t], owner)
                    pltpu.make_async_remote_copy(
                        src_ref=w, dst_ref=w, send_sem=sems[2 * t], recv_sem=sems[2 * t + 1],
                        device_id=(x, y, 1 - c), device_id_type=pl.DeviceIdType.MESH).start()
        token[...] = jnp.zeros_like(token)

    res = pl.pallas_call(
        body, name=name, in_specs=[_HBM] * n,
        out_specs=[_SEM] * (2 * n) + [_HBM] * n + [pl.BlockSpec(memory_space=pltpu.VMEM)],
        out_shape=[pltpu.SemaphoreType.DMA(())] * (2 * n) + [pltpu.HBM(s.shape, s.dtype) for s in lands]
        + [jax.ShapeDtypeStruct((8, 128), F32)],
        input_output_aliases={t: 2 * n + t for t in range(n)},
        compiler_params=pltpu.CompilerParams(has_side_effects=_EFFECT),
    )(*lands)
    return dict(n=n, sems=res[:2 * n], srcs=[], lands=res[2 * n:3 * n], token=res[3 * n])


def _gather_start(shards, names, tag, deps=(), two_level=False):
    xs = [shards[nm] for nm in names]
    axes = [_W_AXIS[nm] for nm in names]
    widths = [x.shape[ax] for x, ax in zip(xs, axes)]
    shapes = [(tuple(d * (N_DEV if a == ax else 1) for a, d in enumerate(x.shape)), x.dtype) for x, ax in zip(xs, axes)]
    src = lambda t, ref, peer: ref
    dst = lambda t, ref, who: _window(ref, axes[t], who, widths[t])
    h = _exchange_start(xs, shapes, src, dst, "gw_start_" + tag, deps, _SAME_CORE if two_level else _ALL_PEERS)
    h['tag'] = "gw_wait_" + tag
    if two_level:
        h['span'] = lambda t, ref: _window(ref, axes[t], 0, widths[t], len(_SAME_CORE))
        h['second'] = (dst, "gw_pass_" + tag, "gw_passed_" + tag)
    return h


def _scatter_start(grads, names, tag):
    xs = [grads[nm] for nm in names]
    axes = [_W_AXIS[nm] for nm in names]
    widths = [x.shape[ax] // N_DEV for x, ax in zip(xs, axes)]
    shapes = [((N_DEV,) + tuple(d // (N_DEV if a == ax else 1) for a, d in enumerate(x.shape)), x.dtype)
              for x, ax in zip(xs, axes)]
    src = lambda t, ref, peer: _window(ref, axes[t], peer, widths[t])
    dst = lambda t, ref, who: ref.at[who]
    h = _exchange_start(xs, shapes, src, dst, "sg_start_" + tag)
    h['tag'] = "sg_wait_" + tag
    return h


def _finish(h, after):
    return _exchange_wait(h, after, h['tag'])


class _LayerWeights:
    def __init__(self, groups, started):
        self.groups = groups
        self.started = started
        self.got = {}
        self.after = None

    def __getitem__(self, nm):
        if nm not in self.got:
            for names, h in self.groups:
                if nm in names:
                    self.got.update(zip(names, _finish(h, list(self.after) + self.started)))
        return self.got[nm]


def _adam_math(p_ref, w, m, v):
    g = p_ref[0].astype(F32)
    for s in range(1, p_ref.shape[0]):
        g = g + p_ref[s].astype(F32)
    bc1 = 1.0 / (1.0 - ADAM_B1 ** ADAM_STEP)
    bc2 = 1.0 / (1.0 - ADAM_B2 ** ADAM_STEP)
    mm = ADAM_B1 * m + (1.0 - ADAM_B1) * g
    vv = ADAM_B2 * v + (1.0 - ADAM_B2) * jnp.square(g)
    return g, -ADAM_LR * ((mm * bc1) / (jnp.sqrt(vv * bc2) + ADAM_EPS) + ADAM_WD * w), mm, vv


def _adam(parts, w, m, v, name, tr=256):
    P, R, C = parts.shape
    tr = min(tr, R)
    assert R % tr == 0

    def body(p_ref, w_ref, m_ref, v_ref, *outs):
        for o_ref, val in zip(outs, _adam_math(p_ref, w_ref[...], m_ref[...], v_ref[...])):
            o_ref[...] = val

    spec = pl.BlockSpec((tr, C), lambda i: (i, 0))
    return pl.pallas_call(
        body, name=name, grid=(R // tr,),
        in_specs=[pl.BlockSpec((P, tr, C), lambda i: (0, i, 0)), spec, spec, spec],
        out_specs=[spec] * 4, out_shape=[jax.ShapeDtypeStruct((R, C), F32)] * 4,
        compiler_params=_cparams(("parallel",)),
    )(parts, w, m, v)


def _adam_layer(parts, w, m, v, l, prev, name, tr=256):
    P, R, C = parts.shape
    tr = min(tr, R)
    assert R % tr == 0 and w.shape == (DEPTH, R, C)
    npv = 0 if prev is None else 4

    def body(p_ref, w_ref, m_ref, v_ref, *rest):
        for o_ref, val in zip(rest[npv:], _adam_math(p_ref, w_ref[0], m_ref[0], v_ref[0])):
            o_ref[0] = val

    spec = pl.BlockSpec((1, tr, C), lambda i: (l, i, 0))
    return pl.pallas_call(
        body, name=name, grid=(R // tr,),
        in_specs=[pl.BlockSpec((P, tr, C), lambda i: (0, i, 0)), spec, spec, spec]
        + [pl.BlockSpec(memory_space=pl.ANY)] * npv,
        out_specs=[spec] * 4, out_shape=[jax.ShapeDtypeStruct((DEPTH, R, C), F32)] * 4,
        input_output_aliases={4 + t: t for t in range(npv)},
        compiler_params=_cparams(("parallel",)),
    )(parts, w, m, v, *([] if prev is None else prev))


def _mod_partial(cact_all, w_ada_l, b_ada_l):
    def body(c_ref, w_ref, b_ref, o_ref):
        o_ref[...] = _dot(c_ref[...].astype(BF16), w_ref[...].astype(BF16)) + b_ref[...]

    return pl.pallas_call(
        body, name="mod_partial", out_shape=jax.ShapeDtypeStruct((cact_all.shape[0], w_ada_l.shape[1]), F32),
        compiler_params=pltpu.CompilerParams(vmem_limit_bytes=VMEM_LIMIT),
    )(cact_all, w_ada_l, b_ada_l)


def _ada_grad(cact_t, dmod):
    D, n = cact_t.shape[0], dmod.shape[1]

    def body(c_ref, d_ref, o_ref):
        ct = c_ref[...].astype(BF16).astype(F32)
        dm = d_ref[...].astype(BF16).astype(F32)
        acc = ct[:, 0:1] * dm[0:1, :]
        for b in range(1, N_DEV):
            acc = acc + ct[:, b:b + 1] * dm[b:b + 1, :]
        o_ref[0] = acc

    return pl.pallas_call(
        body, name="ada_grad", out_shape=jax.ShapeDtypeStruct((1, D, n), F32),
        compiler_params=pltpu.CompilerParams(vmem_limit_bytes=VMEM_LIMIT),
    )(cact_t, dmod)


def _norm_mod(x, r, gv, sh):
    return x * r * gv + sh


def _silu(x):
    return x * _sigmoid(x)


def _rstd(x):
    return lax.rsqrt(jnp.mean(x * x, axis=-1, keepdims=True) + EPS)


def _residual_epi(acc, x, g):
    xn = x + g * acc
    return acc, xn, _rstd(xn)


def _residual_norm_epi(acc, x, g, gv, sh):
    xn = x + g * acc
    r = _rstd(xn)
    return acc, xn, _norm_mod(xn, r, gv, sh), r


def _layer_fwd(x0, r1, h1, mod, gn1, gn2, nxt, W, rc, tri):
    S = x0.shape[0]
    sh1, sc1, g1m, sh2, sc2, g2m = [mod[i:i + 1] for i in range(N_MOD)]
    gv1 = gn1 * (1.0 + sc1)
    gv2 = gn2 * (1.0 + sc2)
    W.after = [h1]
    (p,) = _mm(h1, W["w_in"], outs=(BF16,), name="mm_in")
    yret, rstate = _ret_fwd(p, rc)
    ysb, sbc = _sb_fwd(p, tri)
    W.after = [ysb]
    (ya,) = _mm(yret, W["w_ret_out"], tm=512, a_ex=[(p, 'a', O_RG)],
                pro=lambda yr, g: _silu(g.astype(F32)) * yr.astype(F32), outs=(BF16,), name="mm_ret_out")
    yb, mg = _mm(ysb, W["w_sb_out"], o_ex=[(ya, 'o'), (p, 'o', O_GA), (p, 'o', O_GB)],
                 epi=lambda acc, a, ga, gb: (acc, _sigmoid(ga.astype(F32)) * a.astype(F32)
                                             + _sigmoid(gb.astype(F32)) * acc),
                 outs=(BF16, BF16), name="mm_sb_out")
    mo, x1, h2, r2 = _mm(mg, W["w_mix_out"], tm=512, o_ex=[(x0, 'o'), (g1m, 'n'), (gv2, 'n'), (sh2, 'n')],
                         epi=_residual_norm_epi, outs=(BF16, F32, BF16), cols=1, name="mm_mix_out")
    (act,) = _mm(h2, W["w_up"], epi=lambda acc: (jnp.maximum(acc, 0.0),), outs=(BF16,), name="mm_up")
    if nxt is None:
        dn, x2, r_out = _mm(act, W["w_down"], tm=512, pro=lambda a: a * a, o_ex=[(x1, 'o'), (g2m, 'n')],
                            epi=_residual_epi, outs=(BF16, F32), cols=1, name="mm_down_last")
        h_out = None
    else:
        dn, x2, h_out, r_out = _mm(act, W["w_down"], tm=512, pro=lambda a: a * a,
                                   o_ex=[(x1, 'o'), (g2m, 'n'), (nxt[0], 'n'), (nxt[1], 'n')],
                                   epi=_residual_norm_epi, outs=(BF16, F32, BF16), cols=1, name="mm_down")
    saved = dict(x0=x0, r1=r1, h1=h1, p=p, yret=yret, rstate=rstate, ysb=ysb, sbc=sbc, ya=ya, yb=yb, mg=mg, mo=mo, x1=x1,
                 r2=r2, h2=h2, act=act, dn=dn, gv1=gv1, gv2=gv2, mod=mod, gn1=gn1, gn2=gn2)
    return x2, r_out, h_out, saved


def _norm_bwd(dh, x, r, dres, gv, gn, extra_rows=(), extra_vecs=(), extra_fn=None, extra_outs=(), name="norm_bwd"):
    D = x.shape[1]
    ne = len(extra_rows)

    def fn(dh_t, x_t, dres_t, *rest):
        er, rest = rest[:ne], rest[ne:]
        gv_t = rest[0]
        ev, r_t = rest[1:-1], rest[-1]
        xh = x_t * r_t
        dxh = dh_t * gv_t
        dx = r_t * (dxh - xh * jnp.mean(dxh * xh, axis=-1, keepdims=True)) + dres_t
        base = (dx, dh_t, dh_t * xh)
        if extra_fn is None:
            return base
        return base + tuple(extra_fn(dx, *er, *ev))

    return _ew(fn, [dh, x, dres] + list(extra_rows), vecs=[gv] + list(extra_vecs), cols=[r],
               outs=[('row', D, F32), ('sum', D), ('sum', D)] + list(extra_outs), name=name)


def _gate_bwd(dx, dn, g):
    return dx * dn.astype(F32), dx * g


_GATE_OUTS = [('sum', D_MODEL), ('row', D_MODEL, BF16)]


def _layer_bwd(dx2, d_g2m, d_dn, sv, below, W, rc, tri, emit):
    mod = sv['mod']
    sh1, sc1, g1m, sh2, sc2, g2m = [mod[i:i + 1] for i in range(N_MOD)]
    D = D_MODEL
    p = sv['p']
    (d_up,) = _mm(d_dn, W["w_down"], tb=True, o_ex=[(sv['act'], 'o')],
                  epi=lambda acc, a: (acc * 2.0 * a.astype(F32),), outs=(BF16,), name="mm_down_dx")
    (gw_down,) = _mm(sv['act'], d_dn, ta=True, tk=DW_TK, pro=lambda a: a * a, outs=(BF16,), name="mm_down_dw")
    (gw_up,) = _mm(sv['h2'], d_up, ta=True, tk=DW_TK, outs=(BF16,), name="mm_up_dw")
    tok = emit(dict(w_down=gw_down, w_up=gw_up), "mlp")
    (d_h2,) = _mm(d_up, W["w_up"], tb=True, tk=2048, outs=(F32,), name="mm_up_dx", deps=[tok])
    dx1, d_sh2, s_h2, d_g1m, d_mo = _norm_bwd(
        d_h2, sv['x1'], sv['r2'], dx2, sv['gv2'], sv['gn2'],
        extra_rows=[sv['mo']], extra_vecs=[g1m],
        extra_fn=lambda dx, mo, g: (dx * mo.astype(F32), dx * g),
        extra_outs=[('sum', D), ('row', D, BF16)], name="norm_bwd_mlp")
    d_sc2 = sv['gn2'] * s_h2
    d_gn2 = (1.0 + sc2) * s_h2
    def mix_epi(acc, ya, yb, ga, gb):
        sa, sb = _sigmoid(ga.astype(F32)), _sigmoid(gb.astype(F32))
        return (acc * sa, acc * sb, acc * ya.astype(F32) * sa * (1.0 - sa), acc * yb.astype(F32) * sb * (1.0 - sb))

    d_ya, d_yb, d_ga, d_gb = _mm(d_mo, W["w_mix_out"], tb=True, tm=512,
                                 o_ex=[(sv['ya'], 'o'), (sv['yb'], 'o'), (p, 'o', O_GA), (p, 'o', O_GB)], epi=mix_epi,
                                 outs=(BF16,) * 4, name="mm_mix_dx")
    (gw_mix,) = _mm(sv['mg'], d_mo, ta=True, tk=DW_TK, outs=(BF16,), name="mm_mix_dw")

    def ro_epi(acc, g, yr):
        gf = g.astype(F32)
        s = _sigmoid(gf)
        return (acc * yr.astype(F32) * s * (1.0 + gf * (1.0 - s)), acc * gf * s)

    d_rg, d_yret = _mm(d_ya, W["w_ret_out"], tb=True, tm=512, o_ex=[(p, 'o', O_RG), (sv['yret'], 'o')], epi=ro_epi,
                       outs=(BF16, BF16), name="mm_ret_dx")
    (gw_ro,) = _mm(sv['yret'], d_ya, ta=True, tm=512, tk=DW_TK, a_ex=[(p, 'a', O_RG)],
                   pro=lambda yr, g: _silu(g.astype(F32)) * yr.astype(F32), outs=(BF16,), name="mm_ret_dw")
    (gw_so,) = _mm(sv['ysb'], d_yb, ta=True, tk=DW_TK, outs=(BF16,), name="mm_sb_dw")
    tok = emit(dict(w_mix_out=gw_mix, w_ret_out=gw_ro, w_sb_out=gw_so), "mix")
    (d_ysb,) = _mm(d_yb, W["w_sb_out"], tb=True, outs=(BF16,), name="mm_sb_dx", deps=[tok])
    d_sq, d_sk, d_sv = _sb_bwd(p, sv['sbc'], d_ysb, tri)
    d_ret = _ret_bwd(p, sv['rstate'], d_yret, rc)
    dp = [d_ret, d_rg, d_sq, d_sk, d_sv, d_ga, d_gb]
    (gw_in,) = _mm(sv['h1'], dp, ta=True, tk=1024, outs=(BF16,), name="mm_in_dw")
    tok = emit(dict(w_in=gw_in), "in")
    (d_h,) = _mm(dp, W["w_in"], tb=True, tk=1024, outs=(F32,), name="mm_in_dx", deps=[tok])
    if below is None:
        dx0, d_sh1, s_h1 = _norm_bwd(d_h, sv['x0'], sv['r1'], dx1, sv['gv1'], sv['gn1'], name="norm_bwd_mix")
        gate_below = (None, None)
    else:
        dx0, d_sh1, s_h1, *gate_below = _norm_bwd(
            d_h, sv['x0'], sv['r1'], dx1, sv['gv1'], sv['gn1'], extra_rows=[below['dn']],
            extra_vecs=[below['mod'][N_MOD - 1:N_MOD]], extra_fn=_gate_bwd, extra_outs=_GATE_OUTS,
            name="norm_bwd_mix_gate")
    d_sc1 = sv['gn1'] * s_h1
    d_gn1 = (1.0 + sc1) * s_h1
    d_mod = jnp.concatenate([d_sh1, d_sc1, d_g1m, d_sh2, d_sc2, d_g2m], axis=1)
    return dx0, gate_below, d_mod, d_gn1, d_gn2


def kernel(x, c, norm_mix_g, w_in, w_ret_out, w_sb_out, w_mix_out, norm_mlp_g, w_up, w_down, w_ada, b_ada, final_g, loss_target, m_norm_mix_g, m_w_in, m_w_ret_out, m_w_sb_out, m_w_mix_out, m_norm_mlp_g, m_w_up, m_w_down, m_w_ada, m_b_ada, m_final_g, v_norm_mix_g, v_w_in, v_w_ret_out, v_w_sb_out, v_w_mix_out, v_norm_mlp_g, v_w_up, v_w_down, v_w_ada, v_b_ada, v_final_g):
    S, D = x.shape[1], x.shape[2]
    x0 = x.reshape(S, D)
    tgt = loss_target.reshape(S, D)
    me = 4 * lax.axis_index("x") + 2 * lax.axis_index("y") + lax.axis_index("c")
    wts = dict(w_in=w_in, w_ret_out=w_ret_out, w_sb_out=w_sb_out, w_mix_out=w_mix_out, w_up=w_up, w_down=w_down)
    mts = dict(w_in=m_w_in, w_ret_out=m_w_ret_out, w_sb_out=m_w_sb_out, w_mix_out=m_w_mix_out, w_up=m_w_up, w_down=m_w_down)
    vts = dict(w_in=v_w_in, w_ret_out=v_w_ret_out, w_sb_out=v_w_sb_out, w_mix_out=v_w_mix_out, w_up=v_w_up, w_down=v_w_down)
    rc = _ret_consts(S)
    tri = _tri()

    (cact,) = _ew(lambda t: (_silu(t),), [jnp.pad(c, ((0, 7), (0, 0)))], outs=[('row', D, F32)], name="silu_c")
    (cact_all,) = _all_gather_lead([cact[0:1]], "gather_c")
    cact_all = cact_all.reshape(N_DEV, D)
    cact16 = jnp.pad(cact_all, ((0, 8), (0, 0)))
    n_ada = w_ada.shape[2]
    b_loc = lax.dynamic_slice_in_dim(b_ada, me * n_ada, n_ada, axis=1)
    mods = [_mod_partial(cact16, w_ada[l], b_loc[l:l + 1])[:N_DEV] for l in range(DEPTH)]
    modp = jnp.stack(mods, axis=1)
    (modr,) = _all_to_all_lead([modp], "scatter_mod")
    mod_full = jnp.transpose(modr, (1, 0, 2)).reshape(DEPTH, N_MOD, D)

    shards = {}
    for nm in _W_NAMES:
        w = wts[nm]
        (wb,) = _ew(lambda t: (t,), [w.reshape(-1, w.shape[-1])], outs=[('row', w.shape[-1], BF16)], name="cast_bf16")
        shards[nm] = wb.reshape(w.shape)
    rest = tuple(nm for nm in _W_NAMES if nm != "w_in")
    sh = [{nm: shards[nm][l] for nm in _W_NAMES} for l in range(DEPTH)]
    h_in = _gather_start(sh[0], ("w_in",), "0_in", [modr], two_level=True)

    pre = [(norm_mix_g[l:l + 1] * (1.0 + mod_full[l][1:2]), mod_full[l][0:1]) for l in range(DEPTH)]

    def first(t, gv, sh):
        r = _rstd(t)
        return r, _norm_mod(t, r, gv, sh)

    xs = x0
    rs, hs = _ew(first, [x0], vecs=list(pre[0]), outs=[('col', F32), ('row', D, BF16)], name="row_rstd")

    win, pass_name, passed_name = h_in['second']
    h_pass = _sibling_start(_exchange_wait(h_in, [hs], h_in['tag']), win, pass_name)
    h_pass['span'] = h_in['span']
    started = [h_pass['token']]
    layer_groups = []
    for l, groups in enumerate([[(rest, "0_rest")]] + [[(_W_NAMES, "%d_all" % l)] for l in range(1, DEPTH)]):
        layer_groups.append([])
        for names, tag in groups:
            layer_groups[-1].append((names, _gather_start(sh[l], names, tag, started[-1:])))
            started.append(layer_groups[-1][-1][1]['token'])
    (w_in0,) = _exchange_wait(h_pass, started, passed_name)
    layer_w = [_LayerWeights(g, started) for g in layer_groups]
    layer_w[0].got["w_in"] = w_in0
    saved = []
    for l in range(DEPTH):
        xs, rs, hs, sv = _layer_fwd(xs, rs, hs, mod_full[l], norm_mix_g[l:l + 1], norm_mlp_g[l:l + 1],
                                    pre[l + 1] if l + 1 < DEPTH else None, layer_w[l], rc, tri)
        sv['W'] = layer_w[l]
        saved.append(sv)

    fg = final_g.reshape(1, D)

    def head(xt, tg, dn, g, g2m, r):
        xh = xt * r
        e = xh * g - tg
        dy = e * (1.0 / D)
        dxh = dy * g
        dx = r * (dxh - xh * jnp.mean(dxh * xh, axis=-1, keepdims=True))
        return (dx, dy * xh, 0.5 * e * e * (1.0 / D)) + _gate_bwd(dx, dn, g2m)

    top = saved[DEPTH - 1]
    dxs, d_fg, loss_cols, *gate = _ew(head, [xs, tgt, top['dn']], vecs=[fg, top['mod'][N_MOD - 1:N_MOD]], cols=[rs],
                                      outs=[('row', D, F32), ('sum', D), ('sum', D)] + _GATE_OUTS, name="loss_head")

    small = [None] * DEPTH
    pending = []
    for l in reversed(range(DEPTH)):
        sv = saved[l]

        def emit(gw, tag, l=l):
            names = tuple(gw)
            pending.append((l, names, _scatter_start(gw, names, "%d_%s" % (l, tag))))
            return pending[-1][2]['token']

        dxs, gate, d_mod, d_gn1, d_gn2 = _layer_bwd(dxs, gate[0], gate[1], sv, saved[l - 1] if l else None,
                                                    sv['W'], rc, tri, emit)
        small[l] = (d_mod, d_gn1, d_gn2)
    grad_x = dxs.reshape(1, S, D)

    res = {}
    after = [dxs]
    for l, names, h in pending:
        for nm, landed in zip(names, _finish(h, after)):
            res[nm] = _adam_layer(landed, wts[nm], mts[nm], vts[nm], l, res.get(nm), "adam_layer")
        after = [res[names[-1]][0]]

    pack = jnp.concatenate([small[l][0] for l in range(DEPTH)] + [small[l][1] for l in range(DEPTH)]
                           + [small[l][2] for l in range(DEPTH)] + [d_fg, loss_cols], axis=1)
    (packs,) = _all_gather_lead([pack], "gather_small", deps=after)
    packs = packs.reshape(N_DEV, -1)
    o = 0
    dmod_all = []
    for l in range(DEPTH):
        dmod_all.append(packs[:, o:o + N_MOD * D]); o += N_MOD * D
    gn1_parts = packs[:, o:o + DEPTH * D].reshape(N_DEV, DEPTH, D); o += DEPTH * D
    gn2_parts = packs[:, o:o + DEPTH * D].reshape(N_DEV, DEPTH, D); o += DEPTH * D
    fg_parts = packs[:, o:o + D].reshape(N_DEV, 1, D); o += D
    loss_parts = packs[:, o:o + D]
    (loss_sum,) = _ew(lambda t: (t,), [loss_parts], outs=[('sum', D)], name="loss_sum")
    loss = jnp.sum(loss_sum)

    res["norm_mix_g"] = _adam(gn1_parts, norm_mix_g, m_norm_mix_g, v_norm_mix_g, "adam")
    res["norm_mlp_g"] = _adam(gn2_parts, norm_mlp_g, m_norm_mlp_g, v_norm_mlp_g, "adam")
    fgr = _adam(fg_parts, fg, m_final_g.reshape(1, D), v_final_g.reshape(1, D), "adam")
    res["final_g"] = [t.reshape(D) for t in fgr]
    bparts = jnp.stack(dmod_all, axis=1)
    res["b_ada"] = _adam(bparts, b_ada, m_b_ada, v_b_ada, "adam")
    cact_t = cact_all.T
    for l in range(DEPTH):
        dm_loc = lax.dynamic_slice_in_dim(dmod_all[l], me * n_ada, n_ada, axis=1)
        res["w_ada"] = _adam_layer(_ada_grad(cact_t, dm_loc), w_ada, m_w_ada, v_w_ada, l, res.get("w_ada"),
                                   "adam_layer")

    order = ['norm_mix_g', 'w_in', 'w_ret_out', 'w_sb_out', 'w_mix_out', 'norm_mlp_g', 'w_up', 'w_down', 'w_ada', 'b_ada', 'final_g']
    out = [loss, grad_x]
    for i in range(4):
        out += [res[nm][i] for nm in order]
    return tuple(out)
```

```python
import functools
import math

import jax
import jax.numpy as jnp
import numpy as np
from jax import lax
from jax.experimental import pallas as pl
from jax.experimental.pallas import tpu as pltpu

F32 = jnp.float32
BF16 = jnp.bfloat16

N_DEV = 8
D_MODEL = 1024
DEPTH = 2
RET_HEADS = 4
RET_QK = 256
RET_V = 512
RET_CHUNK = 256
ROPE_BASE = 10000.0
SB_HEADS = 16
SB_DIM = 64
D_FF = 4096
N_MOD = 6
EPS = 1e-6
GN_EPS = 1e-5
O_RQ, O_RK, O_RV, O_RG, O_SQ, O_SK, O_SV, O_GA, O_GB = 0, 1024, 2048, 4096, 6144, 7168, 8192, 9216, 10240
IN_W = 11264

ADAM_LR, ADAM_B1, ADAM_B2, ADAM_EPS, ADAM_WD, ADAM_STEP = 0.001, 0.9, 0.999, 1e-08, 0.01, 10

VMEM_LIMIT = 56 * 1024 * 1024
DW_TK = 2048


def _cparams(sem):
    return pltpu.CompilerParams(dimension_semantics=sem, vmem_limit_bytes=VMEM_LIMIT)


def _mm(a, b, *, ta=False, tb=False, tm=1024, tn=1024, tk=None, a_ex=(), pro=None, o_ex=(), epi=None,
        outs=(F32,), cols=0, name, deps=()):
    a_parts = list(a) if isinstance(a, (list, tuple)) else [a]
    b_parts = list(b) if isinstance(b, (list, tuple)) else [b]
    assert not (ta and len(a_parts) > 1) and not (tb and len(b_parts) > 1)
    if ta:
        K, M = a.shape
    else:
        M, K = a_parts[0].shape[0], sum(t.shape[1] for t in a_parts)
    N = b.shape[0] if tb else sum(t.shape[1] for t in b_parts)
    tm, tn, tk = min(tm, M), min(tn, N), K if tk is None else min(tk, K)
    assert M % tm == 0 and N % tn == 0 and K % tk == 0, (name, M, N, K, tm, tn, tk)
    nk = K // tk
    multi = len(a_parts) > 1 or len(b_parts) > 1

    def ranges(parts, t):
        out, o = [], 0
        for arr in parts:
            assert arr.shape[1] % t == 0
            out.append((o, o + arr.shape[1] // t))
            o += arr.shape[1] // t
        return out

    a_rng = ranges(a_parts, tk) if len(a_parts) > 1 else [(0, nk)]
    b_rng = ranges(b_parts, tn) if len(b_parts) > 1 else [(0, N // tn)]
    clip = lambda v, lo, hi: jnp.clip(v - lo, 0, hi - lo - 1)
    mine = lambda v, lo, hi, w: jnp.where((v >= lo) & (v < hi), w, 0)
    in_specs, args = [], []
    for arr, (lo, hi) in zip(a_parts, a_rng):
        in_specs.append(pl.BlockSpec((tk, tm), lambda i, j, k: (k, i)) if ta
                        else pl.BlockSpec((tm, tk), lambda i, j, k, lo=lo, hi=hi: (i, clip(k, lo, hi))))
        args.append(arr)
    for arr, (lo, hi) in zip(b_parts, b_rng):
        in_specs.append(pl.BlockSpec((tn, tk), lambda i, j, k: (j, k)) if tb
                        else pl.BlockSpec((tk, tn), lambda i, j, k, lo=lo, hi=hi: (mine(j, lo, hi, k), clip(j, lo, hi))))
        args.append(arr)
    npa, npb = len(a_parts), len(b_parts)
    for arr, kind, *off in a_ex:
        off = off[0] if off else 0
        if kind == 'a' and ta:
            assert off % tm == 0
            in_specs.append(pl.BlockSpec((tk, tm), lambda i, j, k, o=off // tm: (k, o + i)))
        elif kind == 'a':
            assert off % tk == 0
            in_specs.append(pl.BlockSpec((tm, tk), lambda i, j, k, o=off // tk: (i, o + k)))
        elif kind == 'k':
            in_specs.append(pl.BlockSpec((tk, 1), lambda i, j, k: (k, 0)) if ta
                            else pl.BlockSpec((1, tk), lambda i, j, k: (0, k)))
        else:
            in_specs.append(pl.BlockSpec((1, tm), lambda i, j, k: (0, i)) if ta
                            else pl.BlockSpec((tm, 1), lambda i, j, k: (i, 0)))
        args.append(arr)
    for arr, kind, *off in o_ex:
        off = off[0] if off else 0
        if kind == 'o':
            assert off % tn == 0
            in_specs.append(pl.BlockSpec((tm, tn), lambda i, j, k, o=off // tn: (i, o + j)))
        elif kind == 'n':
            in_specs.append(pl.BlockSpec((1, tn), lambda i, j, k: (0, j)))
        else:
            in_specs.append(pl.BlockSpec((tm, 1), lambda i, j, k: (i, 0)))
        args.append(arr)
    for arr in deps:
        in_specs.append(pl.BlockSpec(memory_space=pl.ANY))
        args.append(arr)
    assert cols == 0 or N == tn
    na, no, nout, nd = len(a_ex), len(o_ex), len(outs) + cols, len(deps)
    dims = (((0 if ta else 1,), (1 if tb else 0,)), ((), ()))

    def body(*refs):
        a_refs, b_refs = refs[:npa], refs[npa:npa + npb]
        n0 = npa + npb
        aex = refs[n0:n0 + na]
        oex = refs[n0 + na:n0 + na + no]
        out_refs = refs[n0 + na + no + nd:n0 + na + no + nd + nout]

        def product(a_ref, b_ref):
            at = a_ref[...]
            if pro is not None:
                at = pro(at, *[r[...] for r in aex])
            return lax.dot_general(at.astype(BF16), b_ref[...].astype(BF16), dims, preferred_element_type=F32)

        def finish(res):
            vals = epi(res, *[r[...] for r in oex]) if epi is not None else (res,)
            for o_ref, v in zip(out_refs, vals):
                o_ref[...] = v.astype(o_ref.dtype)

        if nk == 1 and not multi:
            finish(product(a_refs[0], b_refs[0]))
            return
        acc = refs[-1]
        j, k = pl.program_id(1), pl.program_id(2)
        if multi:
            @pl.when(k == 0)
            def _():
                acc[...] = jnp.zeros_like(acc)

            for a_ref, (alo, ahi) in zip(a_refs, a_rng):
                for b_ref, (blo, bhi) in zip(b_refs, b_rng):
                    @pl.when((k >= alo) & (k < ahi) & (j >= blo) & (j < bhi))
                    def _():
                        acc[...] += product(a_ref, b_ref)
        else:
            @pl.when(k == 0)
            def _():
                acc[...] = product(a_refs[0], b_refs[0])

            @pl.when(k > 0)
            def _():
                acc[...] += product(a_refs[0], b_refs[0])

        @pl.when(k == nk - 1)
        def _():
            finish(acc[...])

    res = pl.pallas_call(
        body, name=name, grid=(M // tm, N // tn, nk), in_specs=in_specs,
        out_specs=[pl.BlockSpec((tm, tn), lambda i, j, k: (i, j)) for _ in outs]
        + [pl.BlockSpec((tm, 1), lambda i, j, k: (i, 0))] * cols,
        out_shape=[jax.ShapeDtypeStruct((M, N), dt) for dt in outs] + [jax.ShapeDtypeStruct((M, 1), F32)] * cols,
        scratch_shapes=[pltpu.VMEM((tm, tn), F32)] if nk > 1 or multi else [],
        compiler_params=_cparams(("parallel", "parallel", "arbitrary")),
    )(*args)
    return res


def _ew(fn, rows, vecs=(), cols=(), outs=(), tr=256, name=None):
    S = rows[0].shape[0]
    tr = min(tr, S)
    assert S % tr == 0
    in_specs, args = [], []
    for r in rows:
        in_specs.append(pl.BlockSpec((tr, r.shape[1]), lambda i: (i, 0)))
        args.append(r)
    for v in vecs:
        in_specs.append(pl.BlockSpec((1, v.shape[1]), lambda i: (0, 0)))
        args.append(v)
    for c in cols:
        in_specs.append(pl.BlockSpec((tr, 1), lambda i: (i, 0)))
        args.append(c)
    out_specs, out_shape = [], []
    for o in outs:
        if o[0] == 'row':
            out_specs.append(pl.BlockSpec((tr, o[1]), lambda i: (i, 0)))
            out_shape.append(jax.ShapeDtypeStruct((S, o[1]), o[2]))
        elif o[0] == 'sum':
            out_specs.append(pl.BlockSpec((1, o[1]), lambda i: (0, 0)))
            out_shape.append(jax.ShapeDtypeStruct((1, o[1]), F32))
        else:
            out_specs.append(pl.BlockSpec((tr, 1), lambda i: (i, 0)))
            out_shape.append(jax.ShapeDtypeStruct((S, 1), o[1]))
    nin = len(args)

    def body(*refs):
        i = pl.program_id(0)
        vals = fn(*[r[...] for r in refs[:nin]])
        for o, o_ref, v in zip(outs, refs[nin:], vals):
            if o[0] == 'sum':
                @pl.when(i == 0)
                def _():
                    o_ref[...] = jnp.zeros_like(o_ref)
                o_ref[...] += jnp.sum(v.astype(F32), axis=0, keepdims=True)
            else:
                o_ref[...] = v.astype(o_ref.dtype)

    return pl.pallas_call(
        body, name=name, grid=(S // tr,), in_specs=in_specs, out_specs=out_specs, out_shape=out_shape,
        compiler_params=_cparams(("arbitrary",)),
    )(*args)


def _sigmoid(x):
    return 1.0 / (1.0 + jnp.exp(-x))


def _ret_consts(S):
    h = np.arange(RET_HEADS, dtype=np.float64)
    log_gamma = np.log1p(-np.power(2.0, -5.0 - h))
    idx = np.arange(RET_CHUNK, dtype=np.float64)
    rel = idx[:, None] - idx[None, :]
    decay = np.where(rel >= 0, np.exp(np.maximum(rel, 0.0) * log_gamma[:, None, None]), 0.0)
    xi = np.exp((idx + 1.0) * log_gamma[:, None])[:, :, None]
    zeta = np.exp((RET_CHUNK - 1.0 - idx) * log_gamma[:, None])[:, :, None]
    gamma_c = np.exp(RET_CHUNK * log_gamma)[:, None, None]
    half = RET_QK // 2
    inv_freq = np.power(ROPE_BASE, -np.arange(half, dtype=np.float64) / half).astype(np.float32)
    ang = np.arange(S, dtype=np.float32)[:, None] * inv_freq[None, :]
    f = lambda t: jnp.asarray(t, F32)
    return dict(decay=f(decay), xi=f(xi), zeta=f(zeta), gc=f(gamma_c), cos=f(np.cos(ang)), sin=f(np.sin(ang)))


def _rot(t, cos, sin):
    half = RET_QK // 2
    t1, t2 = t[:, :half], t[:, half:]
    return jnp.concatenate([t1 * cos - t2 * sin, t1 * sin + t2 * cos], axis=-1)


def _rot_inv(t, cos, sin):
    half = RET_QK // 2
    t1, t2 = t[:, :half], t[:, half:]
    return jnp.concatenate([t1 * cos + t2 * sin, t2 * cos - t1 * sin], axis=-1)


_NT = (((1,), (1,)), ((), ()))
_TN = (((0,), (0,)), ((), ()))


def _dot(a, b):
    return jnp.dot(a, b, preferred_element_type=F32)


def _dot_nt(a, b):
    return lax.dot_general(a, b, _NT, preferred_element_type=F32)


def _dot_tn(a, b):
    return lax.dot_general(a, b, _TN, preferred_element_type=F32)


_QW, _VW = RET_HEADS * RET_QK, RET_HEADS * RET_V
_HEADS = range(RET_HEADS)


def _ret_in_specs(C, rev, NC):
    n_of = (lambda n: NC - 1 - n) if rev else (lambda n: n)
    whole3 = lambda n: (0, 0, 0)
    return [
        pl.BlockSpec((C, _QW), lambda n: (n_of(n), O_RQ // _QW)),
        pl.BlockSpec((C, _QW), lambda n: (n_of(n), O_RK // _QW)),
        pl.BlockSpec((C, _VW), lambda n: (n_of(n), O_RV // _VW)),
        pl.BlockSpec((C, RET_QK // 2), lambda n: (n_of(n), 0)),
        pl.BlockSpec((C, RET_QK // 2), lambda n: (n_of(n), 0)),
        pl.BlockSpec((RET_HEADS, C, C), whole3),
        pl.BlockSpec((RET_HEADS, C, 1), whole3),
        pl.BlockSpec((RET_HEADS, C, 1), whole3),
        pl.BlockSpec((RET_HEADS, 1, 1), whole3),
    ]


def _qk_heads(q_ref, k_ref, cos, sin):
    qs, kfs = [], []
    for h in _HEADS:
        cols = slice(h * RET_QK, (h + 1) * RET_QK)
        qs.append(_rot(q_ref[:, cols].astype(F32), cos, sin).astype(BF16))
        kfs.append(_rot(k_ref[:, cols].astype(F32), cos, sin) * (RET_QK ** -0.5))
    return qs, kfs


def _ret_fwd(p, rc):
    S = p.shape[0]
    C = RET_CHUNK
    NC = S // C

    def body(q_ref, k_ref, v_ref, cos_ref, sin_ref, dec_ref, xi_ref, zeta_ref, gc_ref, y_ref, rs_ref, r_acc):
        n = pl.program_id(0)

        @pl.when(n == 0)
        def _():
            r_acc[...] = jnp.zeros_like(r_acc)

        cos, sin = cos_ref[...], sin_ref[...]
        qs, kfs = _qk_heads(q_ref, k_ref, cos, sin)
        vs = [v_ref[:, h * RET_V:(h + 1) * RET_V] for h in _HEADS]
        rbs = [r_acc[h].astype(BF16) for h in _HEADS]
        for h in _HEADS:
            rs_ref[h, 0] = rbs[h]
        ss = [(_dot_nt(qs[h], kfs[h].astype(BF16)) * dec_ref[h]).astype(BF16) for h in _HEADS]
        os = [_dot(ss[h], vs[h]) + _dot(qs[h], rbs[h]) * xi_ref[h] for h in _HEADS]
        for h in _HEADS:
            o = os[h]
            mu = jnp.mean(o, axis=-1, keepdims=True)
            var = jnp.mean(jnp.square(o - mu), axis=-1, keepdims=True)
            y_ref[:, h * RET_V:(h + 1) * RET_V] = ((o - mu) * lax.rsqrt(var + GN_EPS)).astype(y_ref.dtype)
        for h in _HEADS:
            kz = (kfs[h] * zeta_ref[h]).astype(BF16)
            r_acc[h] = r_acc[h] * gc_ref[h] + _dot_tn(kz, vs[h])

    return pl.pallas_call(
        body, name="ret_fwd", grid=(NC,), in_specs=_ret_in_specs(C, False, NC),
        out_specs=[pl.BlockSpec((C, _VW), lambda n: (n, 0)),
                   pl.BlockSpec((RET_HEADS, 1, RET_QK, RET_V), lambda n: (0, n, 0, 0))],
        out_shape=[jax.ShapeDtypeStruct((S, _VW), BF16),
                   jax.ShapeDtypeStruct((RET_HEADS, NC, RET_QK, RET_V), BF16)],
        scratch_shapes=[pltpu.VMEM((RET_HEADS, RET_QK, RET_V), F32)],
        compiler_params=_cparams(("arbitrary",)),
    )(p, p, p, rc['cos'], rc['sin'], rc['decay'], rc['xi'], rc['zeta'], rc['gc'])


def _ret_bwd(p, rstate, dy, rc):
    S = p.shape[0]
    C = RET_CHUNK
    NC = S // C

    def body(q_ref, k_ref, v_ref, cos_ref, sin_ref, dec_ref, xi_ref, zeta_ref, gc_ref, rs_ref, dy_ref,
             d_ref, dr_acc):
        dq_ref, dk_ref, dv_ref = d_ref.at[:, 0:_QW], d_ref.at[:, _QW:2 * _QW], d_ref.at[:, 2 * _QW:2 * _QW + _VW]
        t = pl.program_id(0)

        @pl.when(t == 0)
        def _():
            dr_acc[...] = jnp.zeros_like(dr_acc)

        cos, sin = cos_ref[...], sin_ref[...]
        qs, kfs = _qk_heads(q_ref, k_ref, cos, sin)
        ks = [kf.astype(BF16) for kf in kfs]
        vs = [v_ref[:, h * RET_V:(h + 1) * RET_V] for h in _HEADS]
        rbs = [rs_ref[h, 0] for h in _HEADS]
        ss = [(_dot_nt(qs[h], ks[h]) * dec_ref[h]).astype(BF16) for h in _HEADS]
        os = [_dot(ss[h], vs[h]) + _dot(qs[h], rbs[h]) * xi_ref[h] for h in _HEADS]
        dobs, doxis = [], []
        for h in _HEADS:
            o = os[h]
            mu = jnp.mean(o, axis=-1, keepdims=True)
            var = jnp.mean(jnp.square(o - mu), axis=-1, keepdims=True)
            rstd = lax.rsqrt(var + GN_EPS)
            yh = (o - mu) * rstd
            dyf = dy_ref[:, h * RET_V:(h + 1) * RET_V].astype(F32)
            do = (dyf - jnp.mean(dyf, axis=-1, keepdims=True)
                  - yh * jnp.mean(dyf * yh, axis=-1, keepdims=True)) * rstd
            dobs.append(do.astype(BF16))
            doxis.append((do * xi_ref[h]).astype(BF16))
        drbs = [dr_acc[h].astype(BF16) for h in _HEADS]
        dss = [(_dot_nt(dobs[h], vs[h]) * dec_ref[h]).astype(BF16) for h in _HEADS]
        for h in _HEADS:
            dq = _dot(dss[h], ks[h]) + _dot_nt(doxis[h], rbs[h])
            dq_ref[:, h * RET_QK:(h + 1) * RET_QK] = _rot_inv(dq, cos, sin).astype(dq_ref.dtype)
        for h in _HEADS:
            dk = _dot_tn(dss[h], qs[h]) + _dot_nt(vs[h], drbs[h]) * zeta_ref[h]
            dk_ref[:, h * RET_QK:(h + 1) * RET_QK] = (_rot_inv(dk, cos, sin) * (RET_QK ** -0.5)).astype(dk_ref.dtype)
        for h in _HEADS:
            kz = (kfs[h] * zeta_ref[h]).astype(BF16)
            dv = _dot_tn(ss[h], dobs[h]) + _dot(kz, drbs[h])
            dv_ref[:, h * RET_V:(h + 1) * RET_V] = dv.astype(dv_ref.dtype)
        for h in _HEADS:
            dr_acc[h] = dr_acc[h] * gc_ref[h] + _dot_tn(qs[h], doxis[h])

    rn = lambda n: NC - 1 - n
    in_specs = _ret_in_specs(C, True, NC) + [
        pl.BlockSpec((RET_HEADS, 1, RET_QK, RET_V), lambda n: (0, rn(n), 0, 0)),
        pl.BlockSpec((C, _VW), lambda n: (rn(n), 0)),
    ]
    return pl.pallas_call(
        body, name="ret_bwd", grid=(NC,), in_specs=in_specs,
        out_specs=pl.BlockSpec((C, 2 * _QW + _VW), lambda n: (rn(n), 0)),
        out_shape=jax.ShapeDtypeStruct((S, 2 * _QW + _VW), BF16),
        scratch_shapes=[pltpu.VMEM((RET_HEADS, RET_QK, RET_V), F32)],
        compiler_params=_cparams(("arbitrary",)),
    )(p, p, p, rc['cos'], rc['sin'], rc['decay'], rc['xi'], rc['zeta'], rc['gc'], rstate, dy)


SB_T = 256
SB_SCALE = SB_DIM ** -0.5


def _tri():
    j = np.arange(SB_T)
    after = (j[:, None] > j[None, :]).astype(np.float32)
    upto = (j[:, None] <= j[None, :]).astype(np.float32)
    return jnp.asarray(np.stack([after, upto]), BF16)


def _softplus_parts(z):
    neg_abs = lax.bitcast_convert_type(lax.bitcast_convert_type(z, jnp.uint32) | jnp.uint32(0x80000000), F32)
    e = jnp.exp(neg_abs)
    return jnp.maximum(z, 0.0) + jnp.log(1.0 + e), e


def _sb_fwd(p, tri):
    S = p.shape[0]
    T = min(SB_T, S)
    NQ = S // T
    assert NQ <= 128
    qb, kb, vb = O_SQ // 128, O_SK // 128, O_SV // 128

    def body(q_ref, k_ref, v_ref, tri_ref, o_ref, cs_ref, o_acc, run, zbuf, abuf):
        i = pl.program_id(1)
        lane = lax.broadcasted_iota(jnp.int32, (1, 128), 1)
        tri_after = tri_ref[0]
        qs = [jnp.where((lane >= 64) if hh else (lane < 64), q_ref[...], jnp.zeros_like(q_ref[...]))
              * jnp.asarray(SB_SCALE, BF16) for hh in range(2)]
        cs_ref[...] = jnp.zeros_like(cs_ref)
        o_acc[...] = jnp.zeros_like(o_acc)
        run[...] = jnp.zeros_like(run)

        def kv(ref, j):
            return ref[pl.ds(pl.multiple_of(j * T, T), T), :]

        for hh in range(2):
            zbuf[hh] = _dot_nt(qs[hh], kv(k_ref, i))

        def block(t, diagonal):
            j = i - t
            if diagonal:
                msk = lax.broadcasted_iota(jnp.int32, (T, T), 1) < lax.broadcasted_iota(jnp.int32, (T, T), 0)
            if not diagonal:
                av = [_dot(abuf[hh], kv(v_ref, j + 1)) for hh in range(2)]
            lss, exs, tot, zn = [], [], [], []
            for hh in range(2):
                z = zbuf[hh]
                sp, _ = _softplus_parts(z)
                lss.append(z - sp)
                if diagonal:
                    sp = jnp.where(msk, sp, 0.0)
                exs.append(_dot(sp.astype(BF16), tri_after))
                tot.append(sp[:, 0:1])
                zn.append(_dot_nt(qs[hh], kv(k_ref, jnp.maximum(j - 1, 0))))
            for hh in range(2):
                csl = slice(hh * 128, (hh + 1) * 128)
                cs = run[hh]
                a = jnp.exp(lss[hh] - exs[hh] - cs)
                if diagonal:
                    a = jnp.where(msk, a, 0.0)
                abuf[hh] = a.astype(BF16)
                cs_ref[:, csl] = jnp.where(lane == j, cs, cs_ref[:, csl])
                run[hh] = cs + exs[hh][:, 0:1] + tot[hh]
            for hh in range(2):
                if not diagonal:
                    o_acc[hh] += av[hh]
                zbuf[hh] = zn[hh]

        block(0, True)

        def step(t, carry):
            block(t, False)
            return carry

        lax.fori_loop(1, i + 1, step, 0)
        o_ref[...] = jnp.where(lane < 64, o_acc[0] + _dot(abuf[0], kv(v_ref, 0)),
                               o_acc[1] + _dot(abuf[1], kv(v_ref, 0))).astype(o_ref.dtype)

    return pl.pallas_call(
        body, name="sb_fwd", grid=(SB_HEADS // 2, NQ),
        scratch_shapes=[pltpu.VMEM((2, T, 128), F32), pltpu.VMEM((2, T, 1), F32), pltpu.VMEM((2, T, T), F32),
                        pltpu.VMEM((2, T, T), BF16)],
        in_specs=[pl.BlockSpec((T, 128), lambda h, i: (i, qb + h)),
                  pl.BlockSpec((S, 128), lambda h, i: (0, kb + h)),
                  pl.BlockSpec((S, 128), lambda h, i: (0, vb + h)),
                  pl.BlockSpec((1, T, T), lambda h, i: (0, 0, 0))],
        out_specs=[pl.BlockSpec((T, 128), lambda h, i: (i, h)),
                   pl.BlockSpec((T, 256), lambda h, i: (i, h))],
        out_shape=[jax.ShapeDtypeStruct((S, SB_HEADS * SB_DIM), BF16),
                   jax.ShapeDtypeStruct((S, SB_HEADS * 128), F32)],
        compiler_params=_cparams(("parallel", "arbitrary")),
    )(p, p, p, tri)


def _sb_bwd(p, carries, dy, tri):
    S = p.shape[0]
    T = min(SB_T, S)
    NQ = S // T
    qb, kb, vb = O_SQ // 128, O_SK // 128, O_SV // 128

    def body(q_ref, k_ref, v_ref, cs_ref, dy_ref, tri_ref, dq_ref, dk_ref, dv_ref, dk_acc, dv_acc, dq_acc, run,
             zbuf, dabuf, dzbuf, abuf):
        i = pl.program_id(1)

        @pl.when(i == 0)
        def _():
            dk_acc[...] = jnp.zeros_like(dk_acc)
            dv_acc[...] = jnp.zeros_like(dv_acc)

        lane = lax.broadcasted_iota(jnp.int32, (1, 128), 1)
        tri_after, tri_upto = tri_ref[0], tri_ref[1]
        hms = [(lane >= 64) if hh else (lane < 64) for hh in range(2)]
        qs = [jnp.where(hm, q_ref[...], jnp.zeros_like(q_ref[...])) * jnp.asarray(SB_SCALE, BF16) for hm in hms]
        dos = [jnp.where(hm, dy_ref[...], jnp.zeros_like(dy_ref[...])) for hm in hms]
        qst = [t.T for t in qs]
        dost = [t.T for t in dos]
        dq_acc[...] = jnp.zeros_like(dq_acc)
        run[...] = jnp.zeros_like(run)
        dzbuf[...] = jnp.zeros_like(dzbuf)
        abuf[...] = jnp.zeros_like(abuf)

        def kv(ref, j):
            return ref[pl.ds(pl.multiple_of(j * T, T), T), :]

        def flush(jp):
            kp = kv(k_ref, jp)
            dq_add = [_dot(dzbuf[hh], kp) for hh in range(2)]
            dk_add = _dot(qst[0], dzbuf[0]) + _dot(qst[1], dzbuf[1])
            dv_add = _dot(dost[0], abuf[0]) + _dot(dost[1], abuf[1])
            return dq_add, dk_add, dv_add

        def apply(jp, adds):
            dq_add, dk_add, dv_add = adds
            cols = pl.ds(pl.multiple_of(jp * T, T), T)
            for hh in range(2):
                dq_acc[hh] += dq_add[hh]
            dk_acc[:, cols] += dk_add
            dv_acc[:, cols] += dv_add

        for hh in range(2):
            zbuf[hh] = _dot_nt(qs[hh], kv(k_ref, 0))
            dabuf[hh] = _dot_nt(dos[hh], kv(v_ref, 0))

        def block(j, diagonal):
            jp = jnp.maximum(j - 1, 0)
            if diagonal:
                msk = lax.broadcasted_iota(jnp.int32, (T, T), 1) < lax.broadcasted_iota(jnp.int32, (T, T), 0)
            kp = kv(k_ref, jp)
            dq_add = [_dot(dzbuf[hh], kp) for hh in range(2)]
            sigs, lss, exs, zn, dan, dk_part, dv_part = [], [], [], [], [], [], []
            for hh in range(2):
                z = zbuf[hh]
                sp, _ = _softplus_parts(z)
                lss.append(z - sp)
                sigs.append(jnp.exp(lss[hh]))
                if diagonal:
                    sp = jnp.where(msk, sp, 0.0)
                exs.append(_dot(sp.astype(BF16), tri_after))
                if not diagonal:
                    zn.append(_dot_nt(qs[hh], kv(k_ref, j + 1)))
                dk_part.append(_dot(qst[hh], dzbuf[hh]))
            pgs, gs = [], []
            for hh in range(2):
                csl = slice(hh * 128, (hh + 1) * 128)
                cs = jnp.sum(jnp.where(lane == j, cs_ref[:, csl], 0.0), axis=-1, keepdims=True)
                a = jnp.exp(lss[hh] - exs[hh] - cs)
                if diagonal:
                    a = jnp.where(msk, a, 0.0)
                abuf_new = a.astype(BF16)
                g = a * dabuf[hh]
                gs.append((g, abuf_new))
                pgs.append(_dot(g.astype(BF16), tri_upto))
                if not diagonal:
                    dan.append(_dot_nt(dos[hh], kv(v_ref, j + 1)))
                dv_part.append(_dot(dost[hh], abuf[hh]))
            adds = (dq_add, dk_part[0] + dk_part[1], dv_part[0] + dv_part[1])
            for hh in range(2):
                g, abuf_new = gs[hh]
                cg = run[hh]
                dz = g - sigs[hh] * (cg + pgs[hh])
                if diagonal:
                    dz = jnp.where(msk, dz, 0.0)
                run[hh] = cg + pgs[hh][:, T - 1:T]
                dzbuf[hh] = dz.astype(BF16)
                abuf[hh] = abuf_new
            apply(jp, adds)
            if not diagonal:
                for hh in range(2):
                    zbuf[hh] = zn[hh]
                    dabuf[hh] = dan[hh]

        def step(j, carry):
            block(j, False)
            return carry

        lax.fori_loop(0, i, step, 0)
        block(i, True)
        apply(i, flush(i))
        dq_ref[...] = (jnp.where(lane < 64, dq_acc[0], dq_acc[1]) * SB_SCALE).astype(dq_ref.dtype)

        @pl.when(i == NQ - 1)
        def _():
            dk_ref[...] = dk_acc[...].T.astype(dk_ref.dtype)
            dv_ref[...] = dv_acc[...].T.astype(dv_ref.dtype)

    W = SB_HEADS * SB_DIM
    return pl.pallas_call(
        body, name="sb_bwd", grid=(SB_HEADS // 2, NQ),
        in_specs=[pl.BlockSpec((T, 128), lambda h, i: (i, qb + h)),
                  pl.BlockSpec((S, 128), lambda h, i: (0, kb + h)),
                  pl.BlockSpec((S, 128), lambda h, i: (0, vb + h)),
                  pl.BlockSpec((T, 256), lambda h, i: (i, h)),
                  pl.BlockSpec((T, 128), lambda h, i: (i, h)),
                  pl.BlockSpec((2, T, T), lambda h, i: (0, 0, 0))],
        out_specs=[pl.BlockSpec((T, 128), lambda h, i: (i, h)),
                   pl.BlockSpec((S, 128), lambda h, i: (0, h)),
                   pl.BlockSpec((S, 128), lambda h, i: (0, h))],
        out_shape=[jax.ShapeDtypeStruct((S, W), BF16)] * 3,
        scratch_shapes=[pltpu.VMEM((128, S), F32), pltpu.VMEM((128, S), F32), pltpu.VMEM((2, T, 128), F32),
                        pltpu.VMEM((2, T, 1), F32), pltpu.VMEM((2, T, T), F32), pltpu.VMEM((2, T, T), F32),
                        pltpu.VMEM((2, T, T), BF16), pltpu.VMEM((2, T, T), BF16)],
        compiler_params=_cparams(("parallel", "arbitrary")),
    )(p, p, p, carries, dy, tri)


def _exchange(srcs, out_shapes, src_slice, dst_slice, name, deps=()):
    n, nd = len(srcs), len(deps)

    def body(*refs):
        ins, outs = refs[:n], refs[n + nd:2 * n + nd]
        send_sems, recv_sems, loc_sems = refs[2 * n + nd:]
        x, y, c = lax.axis_index("x"), lax.axis_index("y"), lax.axis_index("c")
        me = 4 * x + 2 * y + c
        local = [pltpu.make_async_copy(src_slice(t, ins[t], me), dst_slice(t, outs[t], me), loc_sems.at[t])
                 for t in range(n)]
        for cp in local:
            cp.start()
        sends, recvs = [], []
        for k in (1, 2, 4, 6, 3, 5, 7):
            px = 1 - x if k & 4 else x
            py = 1 - y if k & 2 else y
            pc = 1 - c if k & 1 else c
            peer = 4 * px + 2 * py + pc
            for t in range(n):
                s = t * 7 + k - 1
                sends.append(pltpu.make_async_remote_copy(
                    src_ref=src_slice(t, ins[t], peer), dst_ref=dst_slice(t, outs[t], me),
                    send_sem=send_sems.at[s], recv_sem=recv_sems.at[s],
                    device_id=(px, py, pc), device_id_type=pl.DeviceIdType.MESH))
                recvs.append(pltpu.make_async_remote_copy(
                    src_ref=src_slice(t, ins[t], me), dst_ref=dst_slice(t, outs[t], peer),
                    send_sem=send_sems.at[s], recv_sem=recv_sems.at[s],
                    device_id=(px, py, pc), device_id_type=pl.DeviceIdType.MESH))
        for cp in sends:
            cp.start()
        for cp in recvs:
            cp.wait_recv()
        for cp in sends:
            cp.wait_send()
        for cp in local:
            cp.wait()

    anyspec = pl.BlockSpec(memory_space=pl.ANY)
    return pl.pallas_call(
        body, name=name, in_specs=[anyspec] * (n + nd), out_specs=[anyspec] * n,
        out_shape=[jax.ShapeDtypeStruct(s, d) for s, d in out_shapes],
        scratch_shapes=[pltpu.SemaphoreType.DMA((7 * n,)), pltpu.SemaphoreType.DMA((7 * n,)),
                        pltpu.SemaphoreType.DMA((n,))],
    )(*srcs, *deps)


def _all_gather_lead(xs, name, deps=()):
    return _exchange(
        xs, [((N_DEV,) + x.shape, x.dtype) for x in xs],
        lambda t, ref, peer: ref, lambda t, ref, who: ref.at[who], name, deps)


def _all_to_all_lead(xs, name):
    return _exchange(
        xs, [(x.shape, x.dtype) for x in xs],
        lambda t, ref, peer: ref.at[peer], lambda t, ref, who: ref.at[who], name)


_W_AXIS = {"w_in": 1, "w_ret_out": 0, "w_sb_out": 0, "w_mix_out": 0, "w_up": 1, "w_down": 0}
_W_NAMES = tuple(_W_AXIS)


def _window(ref, axis, who, width, count=1):
    start = pl.multiple_of(who * width, width)
    return ref.at[pl.ds(start, count * width), :] if axis == 0 else ref.at[:, pl.ds(start, count * width)]


_HBM = pl.BlockSpec(memory_space=pltpu.HBM)
_SEM = pl.BlockSpec(memory_space=pltpu.SEMAPHORE)
_EFFECT = pltpu.SideEffectType.DATAFLOW_SIDE_EFFECTING


_ALL_PEERS = (0, 1, 2, 4, 6, 3, 5, 7)
_SAME_CORE = (0, 2, 4, 6)


def _exchange_start(srcs, shapes, src_slice, dst_slice, name, deps=(), ks=_ALL_PEERS):
    n, nd = len(srcs), len(deps)
    lands = [pltpu.with_memory_space_constraint(lax.empty(s, d), pltpu.HBM) for s, d in shapes]

    def body(*refs):
        ins, lnd = refs[:n], refs[n:2 * n]
        sems = refs[2 * n + nd:4 * n + nd]
        token = refs[6 * n + nd]
        x, y, c = lax.axis_index("x"), lax.axis_index("y"), lax.axis_index("c")
        me = 4 * x + 2 * y + c
        for k in ks:
            px = 1 - x if k & 4 else x
            py = 1 - y if k & 2 else y
            pc = 1 - c if k & 1 else c
            peer = 4 * px + 2 * py + pc
            for t in range(n):
                pltpu.make_async_remote_copy(
                    src_ref=src_slice(t, ins[t], peer), dst_ref=dst_slice(t, lnd[t], me),
                    send_sem=sems[2 * t], recv_sem=sems[2 * t + 1],
                    device_id=(px, py, pc), device_id_type=pl.DeviceIdType.MESH).start()
        token[...] = jnp.zeros_like(token)

    res = pl.pallas_call(
        body, name=name, in_specs=[_HBM] * (2 * n) + [pl.BlockSpec(memory_space=pl.ANY)] * nd,
        out_specs=[_SEM] * (2 * n) + [_HBM] * (2 * n) + [pl.BlockSpec(memory_space=pltpu.VMEM)],
        out_shape=[pltpu.SemaphoreType.DMA(())] * (2 * n) + [pltpu.HBM(s.shape, s.dtype) for s in srcs]
        + [pltpu.HBM(s.shape, s.dtype) for s in lands] + [jax.ShapeDtypeStruct((8, 128), F32)],
        input_output_aliases={t: 2 * n + t for t in range(2 * n)},
        compiler_params=pltpu.CompilerParams(has_side_effects=_EFFECT),
    )(*[pltpu.with_memory_space_constraint(s, pltpu.HBM) for s in srcs], *lands, *deps)
    return dict(n=n, sems=res[:2 * n], srcs=res[2 * n:3 * n], lands=res[3 * n:4 * n], token=res[4 * n])


def _exchange_wait(h, after, name):
    n, ns = h['n'], len(h['srcs'])
    span = h.get('span', lambda t, ref: ref)

    def body(*refs):
        lnd = refs[ns:ns + n]
        sems = refs[ns + n:ns + 3 * n]
        x, y, c = lax.axis_index("x"), lax.axis_index("y"), lax.axis_index("c")
        for t in range(n):
            w = span(t, lnd[t])
            cp = pltpu.make_async_remote_copy(src_ref=w, dst_ref=w, send_sem=sems[2 * t], recv_sem=sems[2 * t + 1],
                                              device_id=(x, y, 1 - c), device_id_type=pl.DeviceIdType.MESH)
            cp.wait_send()
            cp.wait_recv()

    after = list(after)
    res = pl.pallas_call(
        body, name=name,
        in_specs=[_HBM] * (ns + n) + [_SEM] * (2 * n) + [pl.BlockSpec(memory_space=pl.ANY)] * len(after),
        out_specs=[_HBM] * (ns + n),
        out_shape=[pltpu.HBM(s.shape, s.dtype) for s in h['srcs']] + [pltpu.HBM(s.shape, s.dtype) for s in h['lands']],
        input_output_aliases={t: t for t in range(ns + n)},
        compiler_params=pltpu.CompilerParams(has_side_effects=_EFFECT),
    )(*h['srcs'], *h['lands'], *h['sems'], *after)
    return list(res[ns:])


def _sibling_start(lands, win, name):
    n = len(lands)

    def body(*refs):
        lnd = refs[:n]
        sems = refs[n:3 * n]
        token = refs[4 * n]
        x, y, c = lax.axis_index("x"), lax.axis_index("y"), lax.axis_index("c")
        for ox in (x, 1 - x):
            for oy in (y, 1 - y):
                owner = 4 * ox + 2 * oy + c
                for t in range(n):
                    w = win(t, lnd[t], owner)
                    pltpu.make_async_remote_copy(
                        src_ref=w, dst_ref=w, send_sem=sems[2 * t], recv_sem=sems[2 * t + 1],
                        device_id=(x, y, 1 - c), device_id_type=pl.DeviceIdType.MESH).start()
        token[...] = jnp.zeros_like(token)

    res = pl.pallas_call(
        body, name=name, in_specs=[_HBM] * n,
        out_specs=[_SEM] * (2 * n) + [_HBM] * n + [pl.BlockSpec(memory_space=pltpu.VMEM)],
        out_shape=[pltpu.SemaphoreType.DMA(())] * (2 * n) + [pltpu.HBM(s.shape, s.dtype) for s in lands]
        + [jax.ShapeDtypeStruct((8, 128), F32)],
        input_output_aliases={t: 2 * n + t for t in range(n)},
        compiler_params=pltpu.CompilerParams(has_side_effects=_EFFECT),
    )(*lands)
    return dict(n=n, sems=res[:2 * n], srcs=[], lands=res[2 * n:3 * n], token=res[3 * n])


def _gather_start(shards, names, tag, deps=(), two_level=False):
    xs = [shards[nm] for nm in names]
    axes = [_W_AXIS[nm] for nm in names]
    widths = [x.shape[ax] for x, ax in zip(xs, axes)]
    shapes = [(tuple(d * (N_DEV if a == ax else 1) for a, d in enumerate(x.shape)), x.dtype) for x, ax in zip(xs, axes)]
    src = lambda t, ref, peer: ref
    dst = lambda t, ref, who: _window(ref, axes[t], who, widths[t])
    h = _exchange_start(xs, shapes, src, dst, "gw_start_" + tag, deps, _SAME_CORE if two_level else _ALL_PEERS)
    h['tag'] = "gw_wait_" + tag
    if two_level:
        h['span'] = lambda t, ref: _window(ref, axes[t], 0, widths[t], len(_SAME_CORE))
        h['second'] = (dst, "gw_pass_" + tag, "gw_passed_" + tag)
    return h


def _scatter_start(grads, names, tag):
    xs = [grads[nm] for nm in names]
    axes = [_W_AXIS[nm] for nm in names]
    widths = [x.shape[ax] // N_DEV for x, ax in zip(xs, axes)]
    shapes = [((N_DEV,) + tuple(d // (N_DEV if a == ax else 1) for a, d in enumerate(x.shape)), x.dtype)
              for x, ax in zip(xs, axes)]
    src = lambda t, ref, peer: _window(ref, axes[t], peer, widths[t])
    dst = lambda t, ref, who: ref.at[who]
    h = _exchange_start(xs, shapes, src, dst, "sg_start_" + tag)
    h['tag'] = "sg_wait_" + tag
    return h


def _finish(h, after):
    return _exchange_wait(h, after, h['tag'])


class _LayerWeights:
    def __init__(self, groups, started):
        self.groups = groups
        self.started = started
        self.got = {}
        self.after = None

    def __getitem__(self, nm):
        if nm not in self.got:
            for names, h in self.groups:
                if nm in names:
                    self.got.update(zip(names, _finish(h, list(self.after) + self.started)))
        return self.got[nm]


def _adam_math(p_ref, w, m, v):
    g = p_ref[0].astype(F32)
    for s in range(1, p_ref.shape[0]):
        g = g + p_ref[s].astype(F32)
    bc1 = 1.0 / (1.0 - ADAM_B1 ** ADAM_STEP)
    bc2 = 1.0 / (1.0 - ADAM_B2 ** ADAM_STEP)
    mm = ADAM_B1 * m + (1.0 - ADAM_B1) * g
    vv = ADAM_B2 * v + (1.0 - ADAM_B2) * jnp.square(g)
    return g, -ADAM_LR * ((mm * bc1) / (jnp.sqrt(vv * bc2) + ADAM_EPS) + ADAM_WD * w), mm, vv


def _adam(parts, w, m, v, name, tr=256):
    P, R, C = parts.shape
    tr = min(tr, R)
    assert R % tr == 0

    def body(p_ref, w_ref, m_ref, v_ref, *outs):
        for o_ref, val in zip(outs, _adam_math(p_ref, w_ref[...], m_ref[...], v_ref[...])):
            o_ref[...] = val

    spec = pl.BlockSpec((tr, C), lambda i: (i, 0))
    return pl.pallas_call(
        body, name=name, grid=(R // tr,),
        in_specs=[pl.BlockSpec((P, tr, C), lambda i: (0, i, 0)), spec, spec, spec],
        out_specs=[spec] * 4, out_shape=[jax.ShapeDtypeStruct((R, C), F32)] * 4,
        compiler_params=_cparams(("parallel",)),
    )(parts, w, m, v)


def _adam_layer(parts, w, m, v, l, prev, name, tr=256):
    P, R, C = parts.shape
    tr = min(tr, R)
    assert R % tr == 0 and w.shape == (DEPTH, R, C)
    npv = 0 if prev is None else 4

    def body(p_ref, w_ref, m_ref, v_ref, *rest):
        for o_ref, val in zip(rest[npv:], _adam_math(p_ref, w_ref[0], m_ref[0], v_ref[0])):
            o_ref[0] = val

    spec = pl.BlockSpec((1, tr, C), lambda i: (l, i, 0))
    return pl.pallas_call(
        body, name=name, grid=(R // tr,),
        in_specs=[pl.BlockSpec((P, tr, C), lambda i: (0, i, 0)), spec, spec, spec]
        + [pl.BlockSpec(memory_space=pl.ANY)] * npv,
        out_specs=[spec] * 4, out_shape=[jax.ShapeDtypeStruct((DEPTH, R, C), F32)] * 4,
        input_output_aliases={4 + t: t for t in range(npv)},
        compiler_params=_cparams(("parallel",)),
    )(parts, w, m, v, *([] if prev is None else prev))


def _mod_partial(cact_all, w_ada_l, b_ada_l):
    def body(c_ref, w_ref, b_ref, o_ref):
        o_ref[...] = _dot(c_ref[...].astype(BF16), w_ref[...].astype(BF16)) + b_ref[...]

    return pl.pallas_call(
        body, name="mod_partial", out_shape=jax.ShapeDtypeStruct((cact_all.shape[0], w_ada_l.shape[1]), F32),
        compiler_params=pltpu.CompilerParams(vmem_limit_bytes=VMEM_LIMIT),
    )(cact_all, w_ada_l, b_ada_l)


def _ada_grad(cact_t, dmod):
    D, n = cact_t.shape[0], dmod.shape[1]

    def body(c_ref, d_ref, o_ref):
        ct = c_ref[...].astype(BF16).astype(F32)
        dm = d_ref[...].astype(BF16).astype(F32)
        acc = ct[:, 0:1] * dm[0:1, :]
        for b in range(1, N_DEV):
            acc = acc + ct[:, b:b + 1] * dm[b:b + 1, :]
        o_ref[0] = acc

    return pl.pallas_call(
        body, name="ada_grad", out_shape=jax.ShapeDtypeStruct((1, D, n), F32),
        compiler_params=pltpu.CompilerParams(vmem_limit_bytes=VMEM_LIMIT),
    )(cact_t, dmod)


def _norm_mod(x, r, gv, sh):
    return x * r * gv + sh


def _silu(x):
    return x * _sigmoid(x)


def _rstd(x):
    return lax.rsqrt(jnp.mean(x * x, axis=-1, keepdims=True) + EPS)


def _residual_epi(acc, x, g):
    xn = x + g * acc
    return acc, xn, _rstd(xn)


def _residual_norm_epi(acc, x, g, gv, sh):
    xn = x + g * acc
    r = _rstd(xn)
    return acc, xn, _norm_mod(xn, r, gv, sh), r


def _layer_fwd(x0, r1, h1, mod, gn1, gn2, nxt, W, rc, tri):
    S = x0.shape[0]
    sh1, sc1, g1m, sh2, sc2, g2m = [mod[i:i + 1] for i in range(N_MOD)]
    gv1 = gn1 * (1.0 + sc1)
    gv2 = gn2 * (1.0 + sc2)
    W.after = [h1]
    (p,) = _mm(h1, W["w_in"], outs=(BF16,), name="mm_in")
    yret, rstate = _ret_fwd(p, rc)
    ysb, sbc = _sb_fwd(p, tri)
    W.after = [ysb]
    (ya,) = _mm(yret, W["w_ret_out"], tm=512, a_ex=[(p, 'a', O_RG)],
                pro=lambda yr, g: _silu(g.astype(F32)) * yr.astype(F32), outs=(BF16,), name="mm_ret_out")
    yb, mg = _mm(ysb, W["w_sb_out"], o_ex=[(ya, 'o'), (p, 'o', O_GA), (p, 'o', O_GB)],
                 epi=lambda acc, a, ga, gb: (acc, _sigmoid(ga.astype(F32)) * a.astype(F32)
                                             + _sigmoid(gb.astype(F32)) * acc),
                 outs=(BF16, BF16), name="mm_sb_out")
    mo, x1, h2, r2 = _mm(mg, W["w_mix_out"], tm=512, o_ex=[(x0, 'o'), (g1m, 'n'), (gv2, 'n'), (sh2, 'n')],
                         epi=_residual_norm_epi, outs=(BF16, F32, BF16), cols=1, name="mm_mix_out")
    (act,) = _mm(h2, W["w_up"], epi=lambda acc: (jnp.maximum(acc, 0.0),), outs=(BF16,), name="mm_up")
    if nxt is None:
        dn, x2, r_out = _mm(act, W["w_down"], tm=512, pro=lambda a: a * a, o_ex=[(x1, 'o'), (g2m, 'n')],
                            epi=_residual_epi, outs=(BF16, F32), cols=1, name="mm_down_last")
        h_out = None
    else:
        dn, x2, h_out, r_out = _mm(act, W["w_down"], tm=512, pro=lambda a: a * a,
                                   o_ex=[(x1, 'o'), (g2m, 'n'), (nxt[0], 'n'), (nxt[1], 'n')],
                                   epi=_residual_norm_epi, outs=(BF16, F32, BF16), cols=1, name="mm_down")
    saved = dict(x0=x0, r1=r1, h1=h1, p=p, yret=yret, rstate=rstate, ysb=ysb, sbc=sbc, ya=ya, yb=yb, mg=mg, mo=mo, x1=x1,
                 r2=r2, h2=h2, act=act, dn=dn, gv1=gv1, gv2=gv2, mod=mod, gn1=gn1, gn2=gn2)
    return x2, r_out, h_out, saved


def _norm_bwd(dh, x, r, dres, gv, gn, extra_rows=(), extra_vecs=(), extra_fn=None, extra_outs=(), name="norm_bwd"):
    D = x.shape[1]
    ne = len(extra_rows)

    def fn(dh_t, x_t, dres_t, *rest):
        er, rest = rest[:ne], rest[ne:]
        gv_t = rest[0]
        ev, r_t = rest[1:-1], rest[-1]
        xh = x_t * r_t
        dxh = dh_t * gv_t
        dx = r_t * (dxh - xh * jnp.mean(dxh * xh, axis=-1, keepdims=True)) + dres_t
        base = (dx, dh_t, dh_t * xh)
        if extra_fn is None:
            return base
        return base + tuple(extra_fn(dx, *er, *ev))

    return _ew(fn, [dh, x, dres] + list(extra_rows), vecs=[gv] + list(extra_vecs), cols=[r],
               outs=[('row', D, F32), ('sum', D), ('sum', D)] + list(extra_outs), name=name)


def _gate_bwd(dx, dn, g):
    return dx * dn.astype(F32), dx * g


_GATE_OUTS = [('sum', D_MODEL), ('row', D_MODEL, BF16)]


def _layer_bwd(dx2, d_g2m, d_dn, sv, below, W, rc, tri, emit):
    mod = sv['mod']
    sh1, sc1, g1m, sh2, sc2, g2m = [mod[i:i + 1] for i in range(N_MOD)]
    D = D_MODEL
    p = sv['p']
    (d_up,) = _mm(d_dn, W["w_down"], tb=True, o_ex=[(sv['act'], 'o')],
                  epi=lambda acc, a: (acc * 2.0 * a.astype(F32),), outs=(BF16,), name="mm_down_dx")
    (gw_down,) = _mm(sv['act'], d_dn, ta=True, tk=DW_TK, pro=lambda a: a * a, outs=(BF16,), name="mm_down_dw")
    (gw_up,) = _mm(sv['h2'], d_up, ta=True, tk=DW_TK, outs=(BF16,), name="mm_up_dw")
    tok = emit(dict(w_down=gw_down, w_up=gw_up), "mlp")
    (d_h2,) = _mm(d_up, W["w_up"], tb=True, tk=2048, outs=(F32,), name="mm_up_dx", deps=[tok])
    dx1, d_sh2, s_h2, d_g1m, d_mo = _norm_bwd(
        d_h2, sv['x1'], sv['r2'], dx2, sv['gv2'], sv['gn2'],
        extra_rows=[sv['mo']], extra_vecs=[g1m],
        extra_fn=lambda dx, mo, g: (dx * mo.astype(F32), dx * g),
        extra_outs=[('sum', D), ('row', D, BF16)], name="norm_bwd_mlp")
    d_sc2 = sv['gn2'] * s_h2
    d_gn2 = (1.0 + sc2) * s_h2
    def mix_epi(acc, ya, yb, ga, gb):
        sa, sb = _sigmoid(ga.astype(F32)), _sigmoid(gb.astype(F32))
        return (acc * sa, acc * sb, acc * ya.astype(F32) * sa * (1.0 - sa), acc * yb.astype(F32) * sb * (1.0 - sb))

    d_ya, d_yb, d_ga, d_gb = _mm(d_mo, W["w_mix_out"], tb=True, tm=512,
                                 o_ex=[(sv['ya'], 'o'), (sv['yb'], 'o'), (p, 'o', O_GA), (p, 'o', O_GB)], epi=mix_epi,
                                 outs=(BF16,) * 4, name="mm_mix_dx")
    (gw_mix,) = _mm(sv['mg'], d_mo, ta=True, tk=DW_TK, outs=(BF16,), name="mm_mix_dw")

    def ro_epi(acc, g, yr):
        gf = g.astype(F32)
        s = _sigmoid(gf)
        return (acc * yr.astype(F32) * s * (1.0 + gf * (1.0 - s)), acc * gf * s)

    d_rg, d_yret = _mm(d_ya, W["w_ret_out"], tb=True, tm=512, o_ex=[(p, 'o', O_RG), (sv['yret'], 'o')], epi=ro_epi,
                       outs=(BF16, BF16), name="mm_ret_dx")
    (gw_ro,) = _mm(sv['yret'], d_ya, ta=True, tm=512, tk=DW_TK, a_ex=[(p, 'a', O_RG)],
                   pro=lambda yr, g: _silu(g.astype(F32)) * yr.astype(F32), outs=(BF16,), name="mm_ret_dw")
    (gw_so,) = _mm(sv['ysb'], d_yb, ta=True, tk=DW_TK, outs=(BF16,), name="mm_sb_dw")
    tok = emit(dict(w_mix_out=gw_mix, w_ret_out=gw_ro, w_sb_out=gw_so), "mix")
    (d_ysb,) = _mm(d_yb, W["w_sb_out"], tb=True, outs=(BF16,), name="mm_sb_dx", deps=[tok])
    d_sq, d_sk, d_sv = _sb_bwd(p, sv['sbc'], d_ysb, tri)
    d_ret = _ret_bwd(p, sv['rstate'], d_yret, rc)
    dp = [d_ret, d_rg, d_sq, d_sk, d_sv, d_ga, d_gb]
    (gw_in,) = _mm(sv['h1'], dp, ta=True, tk=1024, outs=(BF16,), name="mm_in_dw")
    tok = emit(dict(w_in=gw_in), "in")
    (d_h,) = _mm(dp, W["w_in"], tb=True, tk=1024, outs=(F32,), name="mm_in_dx", deps=[tok])
    if below is None:
        dx0, d_sh1, s_h1 = _norm_bwd(d_h, sv['x0'], sv['r1'], dx1, sv['gv1'], sv['gn1'], name="norm_bwd_mix")
        gate_below = (None, None)
    else:
        dx0, d_sh1, s_h1, *gate_below = _norm_bwd(
            d_h, sv['x0'], sv['r1'], dx1, sv['gv1'], sv['gn1'], extra_rows=[below['dn']],
            extra_vecs=[below['mod'][N_MOD - 1:N_MOD]], extra_fn=_gate_bwd, extra_outs=_GATE_OUTS,
            name="norm_bwd_mix_gate")
    d_sc1 = sv['gn1'] * s_h1
    d_gn1 = (1.0 + sc1) * s_h1
    d_mod = jnp.concatenate([d_sh1, d_sc1, d_g1m, d_sh2, d_sc2, d_g2m], axis=1)
    return dx0, gate_below, d_mod, d_gn1, d_gn2


def kernel(x, c, norm_mix_g, w_in, w_ret_out, w_sb_out, w_mix_out, norm_mlp_g, w_up, w_down, w_ada, b_ada, final_g, loss_target, m_norm_mix_g, m_w_in, m_w_ret_out, m_w_sb_out, m_w_mix_out, m_norm_mlp_g, m_w_up, m_w_down, m_w_ada, m_b_ada, m_final_g, v_norm_mix_g, v_w_in, v_w_ret_out, v_w_sb_out, v_w_mix_out, v_norm_mlp_g, v_w_up, v_w_down, v_w_ada, v_b_ada, v_final_g):
    S, D = x.shape[1], x.shape[2]
    x0 = x.reshape(S, D)
    tgt = loss_target.reshape(S, D)
    me = 4 * lax.axis_index("x") + 2 * lax.axis_index("y") + lax.axis_index("c")
    wts = dict(w_in=w_in, w_ret_out=w_ret_out, w_sb_out=w_sb_out, w_mix_out=w_mix_out, w_up=w_up, w_down=w_down)
    mts = dict(w_in=m_w_in, w_ret_out=m_w_ret_out, w_sb_out=m_w_sb_out, w_mix_out=m_w_mix_out, w_up=m_w_up, w_down=m_w_down)
    vts = dict(w_in=v_w_in, w_ret_out=v_w_ret_out, w_sb_out=v_w_sb_out, w_mix_out=v_w_mix_out, w_up=v_w_up, w_down=v_w_down)
    rc = _ret_consts(S)
    tri = _tri()

    (cact,) = _ew(lambda t: (_silu(t),), [jnp.pad(c, ((0, 7), (0, 0)))], outs=[('row', D, F32)], name="silu_c")
    (cact_all,) = _all_gather_lead([cact[0:1]], "gather_c")
    cact_all = cact_all.reshape(N_DEV, D)
    cact16 = jnp.pad(cact_all, ((0, 8), (0, 0)))
    n_ada = w_ada.shape[2]
    b_loc = lax.dynamic_slice_in_dim(b_ada, me * n_ada, n_ada, axis=1)
    mods = [_mod_partial(cact16, w_ada[l], b_loc[l:l + 1])[:N_DEV] for l in range(DEPTH)]
    modp = jnp.stack(mods, axis=1)
    (modr,) = _all_to_all_lead([modp], "scatter_mod")
    mod_full = jnp.transpose(modr, (1, 0, 2)).reshape(DEPTH, N_MOD, D)

    shards = {}
    for nm in _W_NAMES:
        w = wts[nm]
        (wb,) = _ew(lambda t: (t,), [w.reshape(-1, w.shape[-1])], outs=[('row', w.shape[-1], BF16)], name="cast_bf16")
        shards[nm] = wb.reshape(w.shape)
    rest = tuple(nm for nm in _W_NAMES if nm != "w_in")
    sh = [{nm: shards[nm][l] for nm in _W_NAMES} for l in range(DEPTH)]
    h_in = _gather_start(sh[0], ("w_in",), "0_in", [modr], two_level=True)

    pre = [(norm_mix_g[l:l + 1] * (1.0 + mod_full[l][1:2]), mod_full[l][0:1]) for l in range(DEPTH)]

    def first(t, gv, sh):
        r = _rstd(t)
        return r, _norm_mod(t, r, gv, sh)

    xs = x0
    rs, hs = _ew(first, [x0], vecs=list(pre[0]), outs=[('col', F32), ('row', D, BF16)], name="row_rstd")

    win, pass_name, passed_name = h_in['second']
    h_pass = _sibling_start(_exchange_wait(h_in, [hs], h_in['tag']), win, pass_name)
    h_pass['span'] = h_in['span']
    started = [h_pass['token']]
    layer_groups = []
    for l, groups in enumerate([[(rest, "0_rest")]] + [[(_W_NAMES, "%d_all" % l)] for l in range(1, DEPTH)]):
        layer_groups.append([])
        for names, tag in groups:
            layer_groups[-1].append((names, _gather_start(sh[l], names, tag, started[-1:])))
            started.append(layer_groups[-1][-1][1]['token'])
    (w_in0,) = _exchange_wait(h_pass, started, passed_name)
    layer_w = [_LayerWeights(g, started) for g in layer_groups]
    layer_w[0].got["w_in"] = w_in0
    saved = []
    for l in range(DEPTH):
        xs, rs, hs, sv = _layer_fwd(xs, rs, hs, mod_full[l], norm_mix_g[l:l + 1], norm_mlp_g[l:l + 1],
                                    pre[l + 1] if l + 1 < DEPTH else None, layer_w[l], rc, tri)
        sv['W'] = layer_w[l]
        saved.append(sv)

    fg = final_g.reshape(1, D)

    def head(xt, tg, dn, g, g2m, r):
        xh = xt * r
        e = xh * g - tg
        dy = e * (1.0 / D)
        dxh = dy * g
        dx = r * (dxh - xh * jnp.mean(dxh * xh, axis=-1, keepdims=True))
        return (dx, dy * xh, 0.5 * e * e * (1.0 / D)) + _gate_bwd(dx, dn, g2m)

    top = saved[DEPTH - 1]
    dxs, d_fg, loss_cols, *gate = _ew(head, [xs, tgt, top['dn']], vecs=[fg, top['mod'][N_MOD - 1:N_MOD]], cols=[rs],
                                      outs=[('row', D, F32), ('sum', D), ('sum', D)] + _GATE_OUTS, name="loss_head")

    small = [None] * DEPTH
    pending = []
    for l in reversed(range(DEPTH)):
        sv = saved[l]

        def emit(gw, tag, l=l):
            names = tuple(gw)
            pending.append((l, names, _scatter_start(gw, names, "%d_%s" % (l, tag))))
            return pending[-1][2]['token']

        dxs, gate, d_mod, d_gn1, d_gn2 = _layer_bwd(dxs, gate[0], gate[1], sv, saved[l - 1] if l else None,
                                                    sv['W'], rc, tri, emit)
        small[l] = (d_mod, d_gn1, d_gn2)
    grad_x = dxs.reshape(1, S, D)

    res = {}
    after = [dxs]
    for l, names, h in pending:
        for nm, landed in zip(names, _finish(h, after)):
            res[nm] = _adam_layer(landed, wts[nm], mts[nm], vts[nm], l, res.get(nm), "adam_layer")
        after = [res[names[-1]][0]]

    pack = jnp.concatenate([small[l][0] for l in range(DEPTH)] + [small[l][1] for l in range(DEPTH)]
                           + [small[l][2] for l in range(DEPTH)] + [d_fg, loss_cols], axis=1)
    (packs,) = _all_gather_lead([pack], "gather_small", deps=after)
    packs = packs.reshape(N_DEV, -1)
    o = 0
    dmod_all = []
    for l in range(DEPTH):
        dmod_all.append(packs[:, o:o + N_MOD * D]); o += N_MOD * D
    gn1_parts = packs[:, o:o + DEPTH * D].reshape(N_DEV, DEPTH, D); o += DEPTH * D
    gn2_parts = packs[:, o:o + DEPTH * D].reshape(N_DEV, DEPTH, D); o += DEPTH * D
    fg_parts = packs[:, o:o + D].reshape(N_DEV, 1, D); o += D
    loss_parts = packs[:, o:o + D]
    (loss_sum,) = _ew(lambda t: (t,), [loss_parts], outs=[('sum', D)], name="loss_sum")
    loss = jnp.sum(loss_sum)

    res["norm_mix_g"] = _adam(gn1_parts, norm_mix_g, m_norm_mix_g, v_norm_mix_g, "adam")
    res["norm_mlp_g"] = _adam(gn2_parts, norm_mlp_g, m_norm_mlp_g, v_norm_mlp_g, "adam")
    fgr = _adam(fg_parts, fg, m_final_g.reshape(1, D), v_final_g.reshape(1, D), "adam")
    res["final_g"] = [t.reshape(D) for t in fgr]
    bparts = jnp.stack(dmod_all, axis=1)
    res["b_ada"] = _adam(bparts, b_ada, m_b_ada, v_b_ada, "adam")
    cact_t = cact_all.T
    for l in range(DEPTH):
        dm_loc = lax.dynamic_slice_in_dim(dmod_all[l], me * n_ada, n_ada, axis=1)
        res["w_ada"] = _adam_layer(_ada_grad(cact_t, dm_loc), w_ada, m_w_ada, v_w_ada, l, res.get("w_ada"),
                                   "adam_layer")

    order = ['norm_mix_g', 'w_in', 'w_ret_out', 'w_sb_out', 'w_mix_out', 'norm_mlp_g', 'w_up', 'w_down', 'w_ada', 'b_ada', 'final_g']
    out = [loss, grad_x]
    for i in range(4):
        out += [res[nm][i] for nm in order]
    return tuple(out)
```

```python
import functools
import math

import jax
import jax.numpy as jnp
import numpy as np
from jax import lax
from jax.experimental import pallas as pl
from jax.experimental.pallas import tpu as pltpu

F32 = jnp.float32
BF16 = jnp.bfloat16

N_DEV = 8
D_MODEL = 1024
DEPTH = 2
RET_HEADS = 4
RET_QK = 256
RET_V = 512
RET_CHUNK = 256
ROPE_BASE = 10000.0
SB_HEADS = 16
SB_DIM = 64
D_FF = 4096
N_MOD = 6
EPS = 1e-6
GN_EPS = 1e-5
O_RQ, O_RK, O_RV, O_RG, O_SQ, O_SK, O_SV, O_GA, O_GB = 0, 1024, 2048, 4096, 6144, 7168, 8192, 9216, 10240
IN_W = 11264

ADAM_LR, ADAM_B1, ADAM_B2, ADAM_EPS, ADAM_WD, ADAM_STEP = 0.001, 0.9, 0.999, 1e-08, 0.01, 10

VMEM_LIMIT = 56 * 1024 * 1024
DW_TK = 2048


def _cparams(sem):
    return pltpu.CompilerParams(dimension_semantics=sem, vmem_limit_bytes=VMEM_LIMIT)


def _mm(a, b, *, ta=False, tb=False, tm=1024, tn=1024, tk=None, a_ex=(), pro=None, o_ex=(), epi=None,
        outs=(F32,), cols=0, name, deps=()):
    a_parts = list(a) if isinstance(a, (list, tuple)) else [a]
    b_parts = list(b) if isinstance(b, (list, tuple)) else [b]
    assert not (ta and len(a_parts) > 1) and not (tb and len(b_parts) > 1)
    if ta:
        K, M = a.shape
    else:
        M, K = a_parts[0].shape[0], sum(t.shape[1] for t in a_parts)
    N = b.shape[0] if tb else sum(t.shape[1] for t in b_parts)
    tm, tn, tk = min(tm, M), min(tn, N), K if tk is None else min(tk, K)
    assert M % tm == 0 and N % tn == 0 and K % tk == 0, (name, M, N, K, tm, tn, tk)
    nk = K // tk
    multi = len(a_parts) > 1 or len(b_parts) > 1
    direct = epi is None and tuple(outs) == (F32,) and cols == 0

    def ranges(parts, t):
        out, o = [], 0
        for arr in parts:
            assert arr.shape[1] % t == 0
            out.append((o, o + arr.shape[1] // t))
            o += arr.shape[1] // t
        return out

    a_rng = ranges(a_parts, tk) if len(a_parts) > 1 else [(0, nk)]
    b_rng = ranges(b_parts, tn) if len(b_parts) > 1 else [(0, N // tn)]
    clip = lambda v, lo, hi: jnp.clip(v - lo, 0, hi - lo - 1)
    mine = lambda v, lo, hi, w: jnp.where((v >= lo) & (v < hi), w, 0)
    in_specs, args = [], []
    for arr, (lo, hi) in zip(a_parts, a_rng):
        in_specs.append(pl.BlockSpec((tk, tm), lambda i, j, k: (k, i)) if ta
                        else pl.BlockSpec((tm, tk), lambda i, j, k, lo=lo, hi=hi: (i, clip(k, lo, hi))))
        args.append(arr)
    for arr, (lo, hi) in zip(b_parts, b_rng):
        in_specs.append(pl.BlockSpec((tn, tk), lambda i, j, k: (j, k)) if tb
                        else pl.BlockSpec((tk, tn), lambda i, j, k, lo=lo, hi=hi: (mine(j, lo, hi, k), clip(j, lo, hi))))
        args.append(arr)
    npa, npb = len(a_parts), len(b_parts)
    for arr, kind, *off in a_ex:
        off = off[0] if off else 0
        if kind == 'a' and ta:
            assert off % tm == 0
            in_specs.append(pl.BlockSpec((tk, tm), lambda i, j, k, o=off // tm: (k, o + i)))
        elif kind == 'a':
            assert off % tk == 0
            in_specs.append(pl.BlockSpec((tm, tk), lambda i, j, k, o=off // tk: (i, o + k)))
        elif kind == 'k':
            in_specs.append(pl.BlockSpec((tk, 1), lambda i, j, k: (k, 0)) if ta
                            else pl.BlockSpec((1, tk), lambda i, j, k: (0, k)))
        else:
            in_specs.append(pl.BlockSpec((1, tm), lambda i, j, k: (0, i)) if ta
                            else pl.BlockSpec((tm, 1), lambda i, j, k: (i, 0)))
        args.append(arr)
    for arr, kind, *off in o_ex:
        off = off[0] if off else 0
        if kind == 'o':
            assert off % tn == 0
            in_specs.append(pl.BlockSpec((tm, tn), lambda i, j, k, o=off // tn: (i, o + j)))
        elif kind == 'n':
            in_specs.append(pl.BlockSpec((1, tn), lambda i, j, k: (0, j)))
        else:
            in_specs.append(pl.BlockSpec((tm, 1), lambda i, j, k: (i, 0)))
        args.append(arr)
    for arr in deps:
        in_specs.append(pl.BlockSpec(memory_space=pl.ANY))
        args.append(arr)
    assert cols == 0 or N == tn
    na, no, nout, nd = len(a_ex), len(o_ex), len(outs) + cols, len(deps)
    dims = (((0 if ta else 1,), (1 if tb else 0,)), ((), ()))

    def body(*refs):
        a_refs, b_refs = refs[:npa], refs[npa:npa + npb]
        n0 = npa + npb
        aex = refs[n0:n0 + na]
        oex = refs[n0 + na:n0 + na + no]
        out_refs = refs[n0 + na + no + nd:n0 + na + no + nd + nout]

        def product(a_ref, b_ref):
            at = a_ref[...]
            if pro is not None:
                at = pro(at, *[r[...] for r in aex])
            return lax.dot_general(at.astype(BF16), b_ref[...].astype(BF16), dims, preferred_element_type=F32)

        def finish(res):
            vals = epi(res, *[r[...] for r in oex]) if epi is not None else (res,)
            for o_ref, v in zip(out_refs, vals):
                o_ref[...] = v.astype(o_ref.dtype)

        if nk == 1 and not multi:
            finish(product(a_refs[0], b_refs[0]))
            return
        acc = out_refs[0] if direct else refs[-1]
        j, k = pl.program_id(1), pl.program_id(2)
        if multi:
            @pl.when(k == 0)
            def _():
                acc[...] = jnp.zeros_like(acc)

            for a_ref, (alo, ahi) in zip(a_refs, a_rng):
                for b_ref, (blo, bhi) in zip(b_refs, b_rng):
                    @pl.when((k >= alo) & (k < ahi) & (j >= blo) & (j < bhi))
                    def _():
                        acc[...] += product(a_ref, b_ref)
        else:
            @pl.when(k == 0)
            def _():
                acc[...] = product(a_refs[0], b_refs[0])

            @pl.when(k > 0)
            def _():
                acc[...] += product(a_refs[0], b_refs[0])

        if not direct:
            @pl.when(k == nk - 1)
            def _():
                finish(acc[...])

    res = pl.pallas_call(
        body, name=name, grid=(M // tm, N // tn, nk), in_specs=in_specs,
        out_specs=[pl.BlockSpec((tm, tn), lambda i, j, k: (i, j)) for _ in outs]
        + [pl.BlockSpec((tm, 1), lambda i, j, k: (i, 0))] * cols,
        out_shape=[jax.ShapeDtypeStruct((M, N), dt) for dt in outs] + [jax.ShapeDtypeStruct((M, 1), F32)] * cols,
        scratch_shapes=[pltpu.VMEM((tm, tn), F32)] if (nk > 1 or multi) and not direct else [],
        compiler_params=_cparams(("parallel", "parallel", "arbitrary")),
    )(*args)
    return res


def _ew(fn, rows, vecs=(), cols=(), outs=(), tr=256, name=None):
    S = rows[0].shape[0]
    tr = min(tr, S)
    assert S % tr == 0
    in_specs, args = [], []
    for r in rows:
        in_specs.append(pl.BlockSpec((tr, r.shape[1]), lambda i: (i, 0)))
        args.append(r)
    for v in vecs:
        in_specs.append(pl.BlockSpec((1, v.shape[1]), lambda i: (0, 0)))
        args.append(v)
    for c in cols:
        in_specs.append(pl.BlockSpec((tr, 1), lambda i: (i, 0)))
        args.append(c)
    out_specs, out_shape = [], []
    for o in outs:
        if o[0] == 'row':
            out_specs.append(pl.BlockSpec((tr, o[1]), lambda i: (i, 0)))
            out_shape.append(jax.ShapeDtypeStruct((S, o[1]), o[2]))
        elif o[0] == 'sum':
            out_specs.append(pl.BlockSpec((1, o[1]), lambda i: (0, 0)))
            out_shape.append(jax.ShapeDtypeStruct((1, o[1]), F32))
        else:
            out_specs.append(pl.BlockSpec((tr, 1), lambda i: (i, 0)))
            out_shape.append(jax.ShapeDtypeStruct((S, 1), o[1]))
    nin = len(args)

    def body(*refs):
        i = pl.program_id(0)
        vals = fn(*[r[...] for r in refs[:nin]])
        for o, o_ref, v in zip(outs, refs[nin:], vals):
            if o[0] == 'sum':
                @pl.when(i == 0)
                def _():
                    o_ref[...] = jnp.zeros_like(o_ref)
                o_ref[...] += jnp.sum(v.astype(F32), axis=0, keepdims=True)
            else:
                o_ref[...] = v.astype(o_ref.dtype)

    return pl.pallas_call(
        body, name=name, grid=(S // tr,), in_specs=in_specs, out_specs=out_specs, out_shape=out_shape,
        compiler_params=_cparams(("arbitrary",)),
    )(*args)


def _sigmoid(x):
    return 1.0 / (1.0 + jnp.exp(-x))


def _ret_consts(S):
    h = np.arange(RET_HEADS, dtype=np.float64)
    log_gamma = np.log1p(-np.power(2.0, -5.0 - h))
    idx = np.arange(RET_CHUNK, dtype=np.float64)
    rel = idx[:, None] - idx[None, :]
    decay = np.where(rel >= 0, np.exp(np.maximum(rel, 0.0) * log_gamma[:, None, None]), 0.0)
    xi = np.exp((idx + 1.0) * log_gamma[:, None])[:, :, None]
    zeta = np.exp((RET_CHUNK - 1.0 - idx) * log_gamma[:, None])[:, :, None]
    gamma_c = np.exp(RET_CHUNK * log_gamma)[:, None, None]
    half = RET_QK // 2
    inv_freq = np.power(ROPE_BASE, -np.arange(half, dtype=np.float64) / half).astype(np.float32)
    ang = np.arange(S, dtype=np.float32)[:, None] * inv_freq[None, :]
    f = lambda t: jnp.asarray(t, F32)
    return dict(decay=f(decay), xi=f(xi), zeta=f(zeta), gc=f(gamma_c), cos=f(np.cos(ang)), sin=f(np.sin(ang)))


def _rot(t, cos, sin):
    half = RET_QK // 2
    t1, t2 = t[:, :half], t[:, half:]
    return jnp.concatenate([t1 * cos - t2 * sin, t1 * sin + t2 * cos], axis=-1)


def _rot_inv(t, cos, sin):
    half = RET_QK // 2
    t1, t2 = t[:, :half], t[:, half:]
    return jnp.concatenate([t1 * cos + t2 * sin, t2 * cos - t1 * sin], axis=-1)


_NT = (((1,), (1,)), ((), ()))
_TN = (((0,), (0,)), ((), ()))


def _dot(a, b):
    return jnp.dot(a, b, preferred_element_type=F32)


def _dot_nt(a, b):
    return lax.dot_general(a, b, _NT, preferred_element_type=F32)


def _dot_tn(a, b):
    return lax.dot_general(a, b, _TN, preferred_element_type=F32)


_QW, _VW = RET_HEADS * RET_QK, RET_HEADS * RET_V
_HEADS = range(RET_HEADS)


def _ret_in_specs(C, rev, NC):
    n_of = (lambda n: NC - 1 - n) if rev else (lambda n: n)
    whole3 = lambda n: (0, 0, 0)
    return [
        pl.BlockSpec((C, _QW), lambda n: (n_of(n), O_RQ // _QW)),
        pl.BlockSpec((C, _QW), lambda n: (n_of(n), O_RK // _QW)),
        pl.BlockSpec((C, _VW), lambda n: (n_of(n), O_RV // _VW)),
        pl.BlockSpec((C, RET_QK // 2), lambda n: (n_of(n), 0)),
        pl.BlockSpec((C, RET_QK // 2), lambda n: (n_of(n), 0)),
        pl.BlockSpec((RET_HEADS, C, C), whole3),
        pl.BlockSpec((RET_HEADS, C, 1), whole3),
        pl.BlockSpec((RET_HEADS, C, 1), whole3),
        pl.BlockSpec((RET_HEADS, 1, 1), whole3),
    ]


def _qk_heads(q_ref, k_ref, cos, sin):
    qs, kfs = [], []
    for h in _HEADS:
        cols = slice(h * RET_QK, (h + 1) * RET_QK)
        qs.append(_rot(q_ref[:, cols].astype(F32), cos, sin).astype(BF16))
        kfs.append(_rot(k_ref[:, cols].astype(F32), cos, sin) * (RET_QK ** -0.5))
    return qs, kfs


def _ret_fwd(p, rc):
    S = p.shape[0]
    C = RET_CHUNK
    NC = S // C

    def body(q_ref, k_ref, v_ref, cos_ref, sin_ref, dec_ref, xi_ref, zeta_ref, gc_ref, y_ref, rs_ref, r_acc):
        n = pl.program_id(0)

        @pl.when(n == 0)
        def _():
            r_acc[...] = jnp.zeros_like(r_acc)

        cos, sin = cos_ref[...], sin_ref[...]
        qs, kfs = _qk_heads(q_ref, k_ref, cos, sin)
        vs = [v_ref[:, h * RET_V:(h + 1) * RET_V] for h in _HEADS]
        rbs = [r_acc[h].astype(BF16) for h in _HEADS]
        for h in _HEADS:
            rs_ref[h, 0] = rbs[h]
        ss = [(_dot_nt(qs[h], kfs[h].astype(BF16)) * dec_ref[h]).astype(BF16) for h in _HEADS]
        os = [_dot(ss[h], vs[h]) + _dot(qs[h], rbs[h]) * xi_ref[h] for h in _HEADS]
        for h in _HEADS:
            o = os[h]
            mu = jnp.mean(o, axis=-1, keepdims=True)
            var = jnp.mean(jnp.square(o - mu), axis=-1, keepdims=True)
            y_ref[:, h * RET_V:(h + 1) * RET_V] = ((o - mu) * lax.rsqrt(var + GN_EPS)).astype(y_ref.dtype)
        for h in _HEADS:
            kz = (kfs[h] * zeta_ref[h]).astype(BF16)
            r_acc[h] = r_acc[h] * gc_ref[h] + _dot_tn(kz, vs[h])

    return pl.pallas_call(
        body, name="ret_fwd", grid=(NC,), in_specs=_ret_in_specs(C, False, NC),
        out_specs=[pl.BlockSpec((C, _VW), lambda n: (n, 0)),
                   pl.BlockSpec((RET_HEADS, 1, RET_QK, RET_V), lambda n: (0, n, 0, 0))],
        out_shape=[jax.ShapeDtypeStruct((S, _VW), BF16),
                   jax.ShapeDtypeStruct((RET_HEADS, NC, RET_QK, RET_V), BF16)],
        scratch_shapes=[pltpu.VMEM((RET_HEADS, RET_QK, RET_V), F32)],
        compiler_params=_cparams(("arbitrary",)),
    )(p, p, p, rc['cos'], rc['sin'], rc['decay'], rc['xi'], rc['zeta'], rc['gc'])


def _ret_bwd(p, rstate, dy, rc):
    S = p.shape[0]
    C = RET_CHUNK
    NC = S // C

    def body(q_ref, k_ref, v_ref, cos_ref, sin_ref, dec_ref, xi_ref, zeta_ref, gc_ref, rs_ref, dy_ref,
             d_ref, dr_acc):
        dq_ref, dk_ref, dv_ref = d_ref.at[:, 0:_QW], d_ref.at[:, _QW:2 * _QW], d_ref.at[:, 2 * _QW:2 * _QW + _VW]
        t = pl.program_id(0)

        @pl.when(t == 0)
        def _():
            dr_acc[...] = jnp.zeros_like(dr_acc)

        cos, sin = cos_ref[...], sin_ref[...]
        qs, kfs = _qk_heads(q_ref, k_ref, cos, sin)
        ks = [kf.astype(BF16) for kf in kfs]
        vs = [v_ref[:, h * RET_V:(h + 1) * RET_V] for h in _HEADS]
        rbs = [rs_ref[h, 0] for h in _HEADS]
        ss = [(_dot_nt(qs[h], ks[h]) * dec_ref[h]).astype(BF16) for h in _HEADS]
        os = [_dot(ss[h], vs[h]) + _dot(qs[h], rbs[h]) * xi_ref[h] for h in _HEADS]
        dobs, doxis = [], []
        for h in _HEADS:
            o = os[h]
            mu = jnp.mean(o, axis=-1, keepdims=True)
            var = jnp.mean(jnp.square(o - mu), axis=-1, keepdims=True)
            rstd = lax.rsqrt(var + GN_EPS)
            yh = (o - mu) * rstd
            dyf = dy_ref[:, h * RET_V:(h + 1) * RET_V].astype(F32)
            do = (dyf - jnp.mean(dyf, axis=-1, keepdims=True)
                  - yh * jnp.mean(dyf * yh, axis=-1, keepdims=True)) * rstd
            dobs.append(do.astype(BF16))
            doxis.append((do * xi_ref[h]).astype(BF16))
        drbs = [dr_acc[h].astype(BF16) for h in _HEADS]
        dss = [(_dot_nt(dobs[h], vs[h]) * dec_ref[h]).astype(BF16) for h in _HEADS]
        for h in _HEADS:
            dq = _dot(dss[h], ks[h]) + _dot_nt(doxis[h], rbs[h])
            dq_ref[:, h * RET_QK:(h + 1) * RET_QK] = _rot_inv(dq, cos, sin).astype(dq_ref.dtype)
        for h in _HEADS:
            dk = _dot_tn(dss[h], qs[h]) + _dot_nt(vs[h], drbs[h]) * zeta_ref[h]
            dk_ref[:, h * RET_QK:(h + 1) * RET_QK] = (_rot_inv(dk, cos, sin) * (RET_QK ** -0.5)).astype(dk_ref.dtype)
        for h in _HEADS:
            kz = (kfs[h] * zeta_ref[h]).astype(BF16)
            dv = _dot_tn(ss[h], dobs[h]) + _dot(kz, drbs[h])
            dv_ref[:, h * RET_V:(h + 1) * RET_V] = dv.astype(dv_ref.dtype)
        for h in _HEADS:
            dr_acc[h] = dr_acc[h] * gc_ref[h] + _dot_tn(qs[h], doxis[h])

    rn = lambda n: NC - 1 - n
    in_specs = _ret_in_specs(C, True, NC) + [
        pl.BlockSpec((RET_HEADS, 1, RET_QK, RET_V), lambda n: (0, rn(n), 0, 0)),
        pl.BlockSpec((C, _VW), lambda n: (rn(n), 0)),
    ]
    return pl.pallas_call(
        body, name="ret_bwd", grid=(NC,), in_specs=in_specs,
        out_specs=pl.BlockSpec((C, 2 * _QW + _VW), lambda n: (rn(n), 0)),
        out_shape=jax.ShapeDtypeStruct((S, 2 * _QW + _VW), BF16),
        scratch_shapes=[pltpu.VMEM((RET_HEADS, RET_QK, RET_V), F32)],
        compiler_params=_cparams(("arbitrary",)),
    )(p, p, p, rc['cos'], rc['sin'], rc['decay'], rc['xi'], rc['zeta'], rc['gc'], rstate, dy)


SB_T = 256
SB_SCALE = SB_DIM ** -0.5


def _tri():
    j = np.arange(SB_T)
    after = (j[:, None] > j[None, :]).astype(np.float32)
    upto = (j[:, None] <= j[None, :]).astype(np.float32)
    return jnp.asarray(np.stack([after, upto]), BF16)


def _softplus_parts(z):
    neg_abs = lax.bitcast_convert_type(lax.bitcast_convert_type(z, jnp.uint32) | jnp.uint32(0x80000000), F32)
    e = jnp.exp(neg_abs)
    return jnp.maximum(z, 0.0) + jnp.log(1.0 + e), e


def _sb_fwd(p, tri):
    S = p.shape[0]
    T = min(SB_T, S)
    NQ = S // T
    assert NQ <= 128
    qb, kb, vb = O_SQ // 128, O_SK // 128, O_SV // 128

    def body(q_ref, k_ref, v_ref, tri_ref, o_ref, cs_ref, o_acc, run, zbuf, abuf):
        i = pl.program_id(1)
        lane = lax.broadcasted_iota(jnp.int32, (1, 128), 1)
        tri_after = tri_ref[0]
        qs = [jnp.where((lane >= 64) if hh else (lane < 64), q_ref[...], jnp.zeros_like(q_ref[...]))
              * jnp.asarray(SB_SCALE, BF16) for hh in range(2)]
        cs_ref[...] = jnp.zeros_like(cs_ref)
        o_acc[...] = jnp.zeros_like(o_acc)
        run[...] = jnp.zeros_like(run)

        def kv(ref, j):
            return ref[pl.ds(pl.multiple_of(j * T, T), T), :]

        for hh in range(2):
            zbuf[hh] = _dot_nt(qs[hh], kv(k_ref, i))

        def block(t, diagonal):
            j = i - t
            if diagonal:
                msk = lax.broadcasted_iota(jnp.int32, (T, T), 1) < lax.broadcasted_iota(jnp.int32, (T, T), 0)
            if not diagonal:
                av = [_dot(abuf[hh], kv(v_ref, j + 1)) for hh in range(2)]
            lss, exs, tot, zn = [], [], [], []
            for hh in range(2):
                z = zbuf[hh]
                sp, _ = _softplus_parts(z)
                lss.append(z - sp)
                if diagonal:
                    sp = jnp.where(msk, sp, 0.0)
                exs.append(_dot(sp.astype(BF16), tri_after))
                tot.append(sp[:, 0:1])
                zn.append(_dot_nt(qs[hh], kv(k_ref, jnp.maximum(j - 1, 0))))
            for hh in range(2):
                csl = slice(hh * 128, (hh + 1) * 128)
                cs = run[hh]
                a = jnp.exp(lss[hh] - exs[hh] - cs)
                if diagonal:
                    a = jnp.where(msk, a, 0.0)
                abuf[hh] = a.astype(BF16)
                cs_ref[:, csl] = jnp.where(lane == j, cs, cs_ref[:, csl])
                run[hh] = cs + exs[hh][:, 0:1] + tot[hh]
            for hh in range(2):
                if not diagonal:
                    o_acc[hh] += av[hh]
                zbuf[hh] = zn[hh]

        block(0, True)

        def step(t, carry):
            block(t, False)
            return carry

        lax.fori_loop(1, i + 1, step, 0)
        o_ref[...] = jnp.where(lane < 64, o_acc[0] + _dot(abuf[0], kv(v_ref, 0)),
                               o_acc[1] + _dot(abuf[1], kv(v_ref, 0))).astype(o_ref.dtype)

    return pl.pallas_call(
        body, name="sb_fwd", grid=(SB_HEADS // 2, NQ),
        scratch_shapes=[pltpu.VMEM((2, T, 128), F32), pltpu.VMEM((2, T, 1), F32), pltpu.VMEM((2, T, T), F32),
                        pltpu.VMEM((2, T, T), BF16)],
        in_specs=[pl.BlockSpec((T, 128), lambda h, i: (i, qb + h)),
                  pl.BlockSpec((S, 128), lambda h, i: (0, kb + h)),
                  pl.BlockSpec((S, 128), lambda h, i: (0, vb + h)),
                  pl.BlockSpec((1, T, T), lambda h, i: (0, 0, 0))],
        out_specs=[pl.BlockSpec((T, 128), lambda h, i: (i, h)),
                   pl.BlockSpec((T, 256), lambda h, i: (i, h))],
        out_shape=[jax.ShapeDtypeStruct((S, SB_HEADS * SB_DIM), BF16),
                   jax.ShapeDtypeStruct((S, SB_HEADS * 128), F32)],
        compiler_params=_cparams(("parallel", "arbitrary")),
    )(p, p, p, tri)


def _sb_bwd(p, carries, dy, tri):
    S = p.shape[0]
    T = min(SB_T, S)
    NQ = S // T
    qb, kb, vb = O_SQ // 128, O_SK // 128, O_SV // 128

    def body(q_ref, k_ref, v_ref, cs_ref, dy_ref, tri_ref, dq_ref, dk_ref, dv_ref, dk_acc, dv_acc, dq_acc, run,
             zbuf, dabuf, dzbuf, abuf):
        i = pl.program_id(1)

        @pl.when(i == 0)
        def _():
            dk_acc[...] = jnp.zeros_like(dk_acc)
            dv_acc[...] = jnp.zeros_like(dv_acc)

        lane = lax.broadcasted_iota(jnp.int32, (1, 128), 1)
        tri_after, tri_upto = tri_ref[0], tri_ref[1]
        hms = [(lane >= 64) if hh else (lane < 64) for hh in range(2)]
        qs = [jnp.where(hm, q_ref[...], jnp.zeros_like(q_ref[...])) * jnp.asarray(SB_SCALE, BF16) for hm in hms]
        dos = [jnp.where(hm, dy_ref[...], jnp.zeros_like(dy_ref[...])) for hm in hms]
        qst = [t.T for t in qs]
        dost = [t.T for t in dos]
        dq_acc[...] = jnp.zeros_like(dq_acc)
        run[...] = jnp.zeros_like(run)
        dzbuf[...] = jnp.zeros_like(dzbuf)
        abuf[...] = jnp.zeros_like(abuf)

        def kv(ref, j):
            return ref[pl.ds(pl.multiple_of(j * T, T), T), :]

        def flush(jp):
            kp = kv(k_ref, jp)
            dq_add = [_dot(dzbuf[hh], kp) for hh in range(2)]
            dk_add = _dot(qst[0], dzbuf[0]) + _dot(qst[1], dzbuf[1])
            dv_add = _dot(dost[0], abuf[0]) + _dot(dost[1], abuf[1])
            return dq_add, dk_add, dv_add

        def apply(jp, adds):
            dq_add, dk_add, dv_add = adds
            cols = pl.ds(pl.multiple_of(jp * T, T), T)
            for hh in range(2):
                dq_acc[hh] += dq_add[hh]
            dk_acc[:, cols] += dk_add
            dv_acc[:, cols] += dv_add

        for hh in range(2):
            zbuf[hh] = _dot_nt(qs[hh], kv(k_ref, 0))
            dabuf[hh] = _dot_nt(dos[hh], kv(v_ref, 0))

        def block(j, diagonal):
            jp = jnp.maximum(j - 1, 0)
            if diagonal:
                msk = lax.broadcasted_iota(jnp.int32, (T, T), 1) < lax.broadcasted_iota(jnp.int32, (T, T), 0)
            kp = kv(k_ref, jp)
            dq_add = [_dot(dzbuf[hh], kp) for hh in range(2)]
            sigs, lss, exs, zn, dan, dk_part, dv_part = [], [], [], [], [], [], []
            for hh in range(2):
                z = zbuf[hh]
                sp, _ = _softplus_parts(z)
                lss.append(z - sp)
                sigs.append(jnp.exp(lss[hh]))
                if diagonal:
                    sp = jnp.where(msk, sp, 0.0)
                exs.append(_dot(sp.astype(BF16), tri_after))
                if not diagonal:
                    zn.append(_dot_nt(qs[hh], kv(k_ref, j + 1)))
                dk_part.append(_dot(qst[hh], dzbuf[hh]))
            pgs, gs = [], []
            for hh in range(2):
                csl = slice(hh * 128, (hh + 1) * 128)
                cs = jnp.sum(jnp.where(lane == j, cs_ref[:, csl], 0.0), axis=-1, keepdims=True)
                a = jnp.exp(lss[hh] - exs[hh] - cs)
                if diagonal:
                    a = jnp.where(msk, a, 0.0)
                abuf_new = a.astype(BF16)
                g = a * dabuf[hh]
                gs.append((g, abuf_new))
                pgs.append(_dot(g.astype(BF16), tri_upto))
                if not diagonal:
                    dan.append(_dot_nt(dos[hh], kv(v_ref, j + 1)))
                dv_part.append(_dot(dost[hh], abuf[hh]))
            adds = (dq_add, dk_part[0] + dk_part[1], dv_part[0] + dv_part[1])
            for hh in range(2):
                g, abuf_new = gs[hh]
                cg = run[hh]
                dz = g - sigs[hh] * (cg + pgs[hh])
                if diagonal:
                    dz = jnp.where(msk, dz, 0.0)
                run[hh] = cg + pgs[hh][:, T - 1:T]
                dzbuf[hh] = dz.astype(BF16)
                abuf[hh] = abuf_new
            apply(jp, adds)
            if not diagonal:
                for hh in range(2):
                    zbuf[hh] = zn[hh]
                    dabuf[hh] = dan[hh]

        def step(j, carry):
            block(j, False)
            return carry

        lax.fori_loop(0, i, step, 0)
        block(i, True)
        apply(i, flush(i))
        dq_ref[...] = (jnp.where(lane < 64, dq_acc[0], dq_acc[1]) * SB_SCALE).astype(dq_ref.dtype)

        @pl.when(i == NQ - 1)
        def _():
            dk_ref[...] = dk_acc[...].T.astype(dk_ref.dtype)
            dv_ref[...] = dv_acc[...].T.astype(dv_ref.dtype)

    W = SB_HEADS * SB_DIM
    return pl.pallas_call(
        body, name="sb_bwd", grid=(SB_HEADS // 2, NQ),
        in_specs=[pl.BlockSpec((T, 128), lambda h, i: (i, qb + h)),
                  pl.BlockSpec((S, 128), lambda h, i: (0, kb + h)),
                  pl.BlockSpec((S, 128), lambda h, i: (0, vb + h)),
                  pl.BlockSpec((T, 256), lambda h, i: (i, h)),
                  pl.BlockSpec((T, 128), lambda h, i: (i, h)),
                  pl.BlockSpec((2, T, T), lambda h, i: (0, 0, 0))],
        out_specs=[pl.BlockSpec((T, 128), lambda h, i: (i, h)),
                   pl.BlockSpec((S, 128), lambda h, i: (0, h)),
                   pl.BlockSpec((S, 128), lambda h, i: (0, h))],
        out_shape=[jax.ShapeDtypeStruct((S, W), BF16)] * 3,
        scratch_shapes=[pltpu.VMEM((128, S), F32), pltpu.VMEM((128, S), F32), pltpu.VMEM((2, T, 128), F32),
                        pltpu.VMEM((2, T, 1), F32), pltpu.VMEM((2, T, T), F32), pltpu.VMEM((2, T, T), F32),
                        pltpu.VMEM((2, T, T), BF16), pltpu.VMEM((2, T, T), BF16)],
        compiler_params=_cparams(("parallel", "arbitrary")),
    )(p, p, p, carries, dy, tri)


def _exchange(srcs, out_shapes, src_slice, dst_slice, name, deps=()):
    n, nd = len(srcs), len(deps)

    def body(*refs):
        ins, outs = refs[:n], refs[n + nd:2 * n + nd]
        send_sems, recv_sems, loc_sems = refs[2 * n + nd:]
        x, y, c = lax.axis_index("x"), lax.axis_index("y"), lax.axis_index("c")
        me = 4 * x + 2 * y + c
        local = [pltpu.make_async_copy(src_slice(t, ins[t], me), dst_slice(t, outs[t], me), loc_sems.at[t])
                 for t in range(n)]
        for cp in local:
            cp.start()
        sends, recvs = [], []
        for k in (1, 2, 4, 6, 3, 5, 7):
            px = 1 - x if k & 4 else x
            py = 1 - y if k & 2 else y
            pc = 1 - c if k & 1 else c
            peer = 4 * px + 2 * py + pc
            for t in range(n):
                s = t * 7 + k - 1
                sends.append(pltpu.make_async_remote_copy(
                    src_ref=src_slice(t, ins[t], peer), dst_ref=dst_slice(t, outs[t], me),
                    send_sem=send_sems.at[s], recv_sem=recv_sems.at[s],
                    device_id=(px, py, pc), device_id_type=pl.DeviceIdType.MESH))
                recvs.append(pltpu.make_async_remote_copy(
                    src_ref=src_slice(t, ins[t], me), dst_ref=dst_slice(t, outs[t], peer),
                    send_sem=send_sems.at[s], recv_sem=recv_sems.at[s],
                    device_id=(px, py, pc), device_id_type=pl.DeviceIdType.MESH))
        for cp in sends:
            cp.start()
        for cp in recvs:
            cp.wait_recv()
        for cp in sends:
            cp.wait_send()
        for cp in local:
            cp.wait()

    anyspec = pl.BlockSpec(memory_space=pl.ANY)
    return pl.pallas_call(
        body, name=name, in_specs=[anyspec] * (n + nd), out_specs=[anyspec] * n,
        out_shape=[jax.ShapeDtypeStruct(s, d) for s, d in out_shapes],
        scratch_shapes=[pltpu.SemaphoreType.DMA((7 * n,)), pltpu.SemaphoreType.DMA((7 * n,)),
                        pltpu.SemaphoreType.DMA((n,))],
    )(*srcs, *deps)


def _all_gather_lead(xs, name, deps=()):
    return _exchange(
        xs, [((N_DEV,) + x.shape, x.dtype) for x in xs],
        lambda t, ref, peer: ref, lambda t, ref, who: ref.at[who], name, deps)


def _all_to_all_lead(xs, name):
    return _exchange(
        xs, [(x.shape, x.dtype) for x in xs],
        lambda t, ref, peer: ref.at[peer], lambda t, ref, who: ref.at[who], name)


_W_AXIS = {"w_in": 1, "w_ret_out": 0, "w_sb_out": 0, "w_mix_out": 0, "w_up": 1, "w_down": 0}
_W_NAMES = tuple(_W_AXIS)


def _window(ref, axis, who, width, count=1):
    start = pl.multiple_of(who * width, width)
    return ref.at[pl.ds(start, count * width), :] if axis == 0 else ref.at[:, pl.ds(start, count * width)]


_HBM = pl.BlockSpec(memory_space=pltpu.HBM)
_SEM = pl.BlockSpec(memory_space=pltpu.SEMAPHORE)
_EFFECT = pltpu.SideEffectType.DATAFLOW_SIDE_EFFECTING


_ALL_PEERS = (0, 1, 2, 4, 6, 3, 5, 7)
_SAME_CORE = (0, 2, 4, 6)


def _exchange_start(srcs, shapes, src_slice, dst_slice, name, deps=(), ks=_ALL_PEERS):
    n, nd = len(srcs), len(deps)
    lands = [pltpu.with_memory_space_constraint(lax.empty(s, d), pltpu.HBM) for s, d in shapes]

    def body(*refs):
        ins, lnd = refs[:n], refs[n:2 * n]
        sems = refs[2 * n + nd:4 * n + nd]
        token = refs[6 * n + nd]
        x, y, c = lax.axis_index("x"), lax.axis_index("y"), lax.axis_index("c")
        me = 4 * x + 2 * y + c
        for k in ks:
            px = 1 - x if k & 4 else x
            py = 1 - y if k & 2 else y
            pc = 1 - c if k & 1 else c
            peer = 4 * px + 2 * py + pc
            for t in range(n):
                pltpu.make_async_remote_copy(
                    src_ref=src_slice(t, ins[t], peer), dst_ref=dst_slice(t, lnd[t], me),
                    send_sem=sems[2 * t], recv_sem=sems[2 * t + 1],
                    device_id=(px, py, pc), device_id_type=pl.DeviceIdType.MESH).start()
        token[...] = jnp.zeros_like(token)

    res = pl.pallas_call(
        body, name=name, in_specs=[_HBM] * (2 * n) + [pl.BlockSpec(memory_space=pl.ANY)] * nd,
        out_specs=[_SEM] * (2 * n) + [_HBM] * (2 * n) + [pl.BlockSpec(memory_space=pltpu.VMEM)],
        out_shape=[pltpu.SemaphoreType.DMA(())] * (2 * n) + [pltpu.HBM(s.shape, s.dtype) for s in srcs]
        + [pltpu.HBM(s.shape, s.dtype) for s in lands] + [jax.ShapeDtypeStruct((8, 128), F32)],
        input_output_aliases={t: 2 * n + t for t in range(2 * n)},
        compiler_params=pltpu.CompilerParams(has_side_effects=_EFFECT),
    )(*[pltpu.with_memory_space_constraint(s, pltpu.HBM) for s in srcs], *lands, *deps)
    return dict(n=n, sems=res[:2 * n], srcs=res[2 * n:3 * n], lands=res[3 * n:4 * n], token=res[4 * n])


def _exchange_wait(h, after, name):
    n, ns = h['n'], len(h['srcs'])
    span = h.get('span', lambda t, ref: ref)

    def body(*refs):
        lnd = refs[ns:ns + n]
        sems = refs[ns + n:ns + 3 * n]
        x, y, c = lax.axis_index("x"), lax.axis_index("y"), lax.axis_index("c")
        for t in range(n):
            w = span(t, lnd[t])
            cp = pltpu.make_async_remote_copy(src_ref=w, dst_ref=w, send_sem=sems[2 * t], recv_sem=sems[2 * t + 1],
                                              device_id=(x, y, 1 - c), device_id_type=pl.DeviceIdType.MESH)
            cp.wait_send()
            cp.wait_recv()

    after = list(after)
    res = pl.pallas_call(
        body, name=name,
        in_specs=[_HBM] * (ns + n) + [_SEM] * (2 * n) + [pl.BlockSpec(memory_space=pl.ANY)] * len(after),
        out_specs=[_HBM] * (ns + n),
        out_shape=[pltpu.HBM(s.shape, s.dtype) for s in h['srcs']] + [pltpu.HBM(s.shape, s.dtype) for s in h['lands']],
        input_output_aliases={t: t for t in range(ns + n)},
        compiler_params=pltpu.CompilerParams(has_side_effects=_EFFECT),
    )(*h['srcs'], *h['lands'], *h['sems'], *after)
    return list(res[ns:])


def _sibling_start(lands, win, name):
    n = len(lands)

    def body(*refs):
        lnd = refs[:n]
        sems = refs[n:3 * n]
        token = refs[4 * n]
        x, y, c = lax.axis_index("x"), lax.axis_index("y"), lax.axis_index("c")
        for ox in (x, 1 - x):
            for oy in (y, 1 - y):
                owner = 4 * ox + 2 * oy + c
                for t in range(n):
                    w = win(t, lnd[t], owner)
                    pltpu.make_async_remote_copy(
                        src_ref=w, dst_ref=w, send_sem=sems[2 * t], recv_sem=sems[2 * t + 1],
                        device_id=(x, y, 1 - c), device_id_type=pl.DeviceIdType.MESH).start()
        token[...] = jnp.zeros_like(token)

    res = pl.pallas_call(
        body, name=name, in_specs=[_HBM] * n,
        out_specs=[_SEM] * (2 * n) + [_HBM] * n + [pl.BlockSpec(memory_space=pltpu.VMEM)],
        out_shape=[pltpu.SemaphoreType.DMA(())] * (2 * n) + [pltpu.HBM(s.shape, s.dtype) for s in lands]
        + [jax.ShapeDtypeStruct((8, 128), F32)],
        input_output_aliases={t: 2 * n + t for t in range(n)},
        compiler_params=pltpu.CompilerParams(has_side_effects=_EFFECT),
    )(*lands)
    return dict(n=n, sems=res[:2 * n], srcs=[], lands=res[2 * n:3 * n], token=res[3 * n])


def _gather_start(shards, names, tag, deps=(), two_level=False):
    xs = [shards[nm] for nm in names]
    axes = [_W_AXIS[nm] for nm in names]
    widths = [x.shape[ax] for x, ax in zip(xs, axes)]
    shapes = [(tuple(d * (N_DEV if a == ax else 1) for a, d in enumerate(x.shape)), x.dtype) for x, ax in zip(xs, axes)]
    src = lambda t, ref, peer: ref
    dst = lambda t, ref, who: _window(ref, axes[t], who, widths[t])
    h = _exchange_start(xs, shapes, src, dst, "gw_start_" + tag, deps, _SAME_CORE if two_level else _ALL_PEERS)
    h['tag'] = "gw_wait_" + tag
    if two_level:
        h['span'] = lambda t, ref: _window(ref, axes[t], 0, widths[t], len(_SAME_CORE))
        h['second'] = (dst, "gw_pass_" + tag, "gw_passed_" + tag)
    return h


def _scatter_start(grads, names, tag):
    xs = [grads[nm] for nm in names]
    axes = [_W_AXIS[nm] for nm in names]
    widths = [x.shape[ax] // N_DEV for x, ax in zip(xs, axes)]
    shapes = [((N_DEV,) + tuple(d // (N_DEV if a == ax else 1) for a, d in enumerate(x.shape)), x.dtype)
              for x, ax in zip(xs, axes)]
    src = lambda t, ref, peer: _window(ref, axes[t], peer, widths[t])
    dst = lambda t, ref, who: ref.at[who]
    h = _exchange_start(xs, shapes, src, dst, "sg_start_" + tag)
    h['tag'] = "sg_wait_" + tag
    return h


def _finish(h, after):
    return _exchange_wait(h, after, h['tag'])


class _LayerWeights:
    def __init__(self, groups, started):
        self.groups = groups
        self.started = started
        self.got = {}
        self.after = None

    def __getitem__(self, nm):
        if nm not in self.got:
            for names, h in self.groups:
                if nm in names:
                    self.got.update(zip(names, _finish(h, list(self.after) + self.started)))
        return self.got[nm]


def _adam_math(p_ref, w, m, v):
    g = p_ref[0].astype(F32)
    for s in range(1, p_ref.shape[0]):
        g = g + p_ref[s].astype(F32)
    bc1 = 1.0 / (1.0 - ADAM_B1 ** ADAM_STEP)
    bc2 = 1.0 / (1.0 - ADAM_B2 ** ADAM_STEP)
    mm = ADAM_B1 * m + (1.0 - ADAM_B1) * g
    vv = ADAM_B2 * v + (1.0 - ADAM_B2) * jnp.square(g)
    return g, -ADAM_LR * ((mm * bc1) / (jnp.sqrt(vv * bc2) + ADAM_EPS) + ADAM_WD * w), mm, vv


def _adam(parts, w, m, v, name, tr=256):
    P, R, C = parts.shape
    tr = min(tr, R)
    assert R % tr == 0

    def body(p_ref, w_ref, m_ref, v_ref, *outs):
        for o_ref, val in zip(outs, _adam_math(p_ref, w_ref[...], m_ref[...], v_ref[...])):
            o_ref[...] = val

    spec = pl.BlockSpec((tr, C), lambda i: (i, 0))
    return pl.pallas_call(
        body, name=name, grid=(R // tr,),
        in_specs=[pl.BlockSpec((P, tr, C), lambda i: (0, i, 0)), spec, spec, spec],
        out_specs=[spec] * 4, out_shape=[jax.ShapeDtypeStruct((R, C), F32)] * 4,
        compiler_params=_cparams(("parallel",)),
    )(parts, w, m, v)


def _adam_layer(parts, w, m, v, l, prev, name, tr=256):
    P, R, C = parts.shape
    tr = min(tr, R)
    assert R % tr == 0 and w.shape == (DEPTH, R, C)
    npv = 0 if prev is None else 4

    def body(p_ref, w_ref, m_ref, v_ref, *rest):
        for o_ref, val in zip(rest[npv:], _adam_math(p_ref, w_ref[0], m_ref[0], v_ref[0])):
            o_ref[0] = val

    spec = pl.BlockSpec((1, tr, C), lambda i: (l, i, 0))
    return pl.pallas_call(
        body, name=name, grid=(R // tr,),
        in_specs=[pl.BlockSpec((P, tr, C), lambda i: (0, i, 0)), spec, spec, spec]
        + [pl.BlockSpec(memory_space=pl.ANY)] * npv,
        out_specs=[spec] * 4, out_shape=[jax.ShapeDtypeStruct((DEPTH, R, C), F32)] * 4,
        input_output_aliases={4 + t: t for t in range(npv)},
        compiler_params=_cparams(("parallel",)),
    )(parts, w, m, v, *([] if prev is None else prev))


def _mod_partial(cact_all, w_ada_l, b_ada_l):
    def body(c_ref, w_ref, b_ref, o_ref):
        o_ref[...] = _dot(c_ref[...].astype(BF16), w_ref[...].astype(BF16)) + b_ref[...]

    return pl.pallas_call(
        body, name="mod_partial", out_shape=jax.ShapeDtypeStruct((cact_all.shape[0], w_ada_l.shape[1]), F32),
        compiler_params=pltpu.CompilerParams(vmem_limit_bytes=VMEM_LIMIT),
    )(cact_all, w_ada_l, b_ada_l)


def _ada_grad(cact_t, dmod):
    D, n = cact_t.shape[0], dmod.shape[1]

    def body(c_ref, d_ref, o_ref):
        ct = c_ref[...].astype(BF16).astype(F32)
        dm = d_ref[...].astype(BF16).astype(F32)
        acc = ct[:, 0:1] * dm[0:1, :]
        for b in range(1, N_DEV):
            acc = acc + ct[:, b:b + 1] * dm[b:b + 1, :]
        o_ref[0] = acc

    return pl.pallas_call(
        body, name="ada_grad", out_shape=jax.ShapeDtypeStruct((1, D, n), F32),
        compiler_params=pltpu.CompilerParams(vmem_limit_bytes=VMEM_LIMIT),
    )(cact_t, dmod)


def _norm_mod(x, r, gv, sh):
    return x * r * gv + sh


def _silu(x):
    return x * _sigmoid(x)


def _rstd(x):
    return lax.rsqrt(jnp.mean(x * x, axis=-1, keepdims=True) + EPS)


def _residual_epi(acc, x, g):
    xn = x + g * acc
    return acc, xn, _rstd(xn)


def _residual_norm_epi(acc, x, g, gv, sh):
    xn = x + g * acc
    r = _rstd(xn)
    return acc, xn, _norm_mod(xn, r, gv, sh), r


def _layer_fwd(x0, r1, h1, mod, gn1, gn2, nxt, W, rc, tri):
    S = x0.shape[0]
    sh1, sc1, g1m, sh2, sc2, g2m = [mod[i:i + 1] for i in range(N_MOD)]
    gv1 = gn1 * (1.0 + sc1)
    gv2 = gn2 * (1.0 + sc2)
    W.after = [h1]
    (p,) = _mm(h1, W["w_in"], tm=2048, outs=(BF16,), name="mm_in")
    yret, rstate = _ret_fwd(p, rc)
    ysb, sbc = _sb_fwd(p, tri)
    W.after = [ysb]
    (ya,) = _mm(yret, W["w_ret_out"], tm=512, a_ex=[(p, 'a', O_RG)],
                pro=lambda yr, g: _silu(g.astype(F32)) * yr.astype(F32), outs=(BF16,), name="mm_ret_out")
    yb, mg = _mm(ysb, W["w_sb_out"], o_ex=[(ya, 'o'), (p, 'o', O_GA), (p, 'o', O_GB)],
                 epi=lambda acc, a, ga, gb: (acc, _sigmoid(ga.astype(F32)) * a.astype(F32)
                                             + _sigmoid(gb.astype(F32)) * acc),
                 outs=(BF16, BF16), name="mm_sb_out")
    mo, x1, h2, r2 = _mm(mg, W["w_mix_out"], tm=512, o_ex=[(x0, 'o'), (g1m, 'n'), (gv2, 'n'), (sh2, 'n')],
                         epi=_residual_norm_epi, outs=(BF16, F32, BF16), cols=1, name="mm_mix_out")
    (act,) = _mm(h2, W["w_up"], epi=lambda acc: (jnp.maximum(acc, 0.0),), outs=(BF16,), name="mm_up")
    if nxt is None:
        dn, x2, r_out = _mm(act, W["w_down"], tm=512, pro=lambda a: a * a, o_ex=[(x1, 'o'), (g2m, 'n')],
                            epi=_residual_epi, outs=(BF16, F32), cols=1, name="mm_down_last")
        h_out = None
    else:
        dn, x2, h_out, r_out = _mm(act, W["w_down"], tm=512, pro=lambda a: a * a,
                                   o_ex=[(x1, 'o'), (g2m, 'n'), (nxt[0], 'n'), (nxt[1], 'n')],
                                   epi=_residual_norm_epi, outs=(BF16, F32, BF16), cols=1, name="mm_down")
    saved = dict(x0=x0, r1=r1, h1=h1, p=p, yret=yret, rstate=rstate, ysb=ysb, sbc=sbc, ya=ya, yb=yb, mg=mg, mo=mo, x1=x1,
                 r2=r2, h2=h2, act=act, dn=dn, gv1=gv1, gv2=gv2, mod=mod, gn1=gn1, gn2=gn2)
    return x2, r_out, h_out, saved


def _norm_bwd(dh, x, r, dres, gv, gn, extra_rows=(), extra_vecs=(), extra_fn=None, extra_outs=(), name="norm_bwd"):
    D = x.shape[1]
    ne = len(extra_rows)

    def fn(dh_t, x_t, dres_t, *rest):
        er, rest = rest[:ne], rest[ne:]
        gv_t = rest[0]
        ev, r_t = rest[1:-1], rest[-1]
        xh = x_t * r_t
        dxh = dh_t * gv_t
        dx = r_t * (dxh - xh * jnp.mean(dxh * xh, axis=-1, keepdims=True)) + dres_t
        base = (dx, dh_t, dh_t * xh)
        if extra_fn is None:
            return base
        return base + tuple(extra_fn(dx, *er, *ev))

    return _ew(fn, [dh, x, dres] + list(extra_rows), vecs=[gv] + list(extra_vecs), cols=[r],
               outs=[('row', D, F32), ('sum', D), ('sum', D)] + list(extra_outs), name=name)


def _gate_bwd(dx, dn, g):
    return dx * dn.astype(F32), dx * g


_GATE_OUTS = [('sum', D_MODEL), ('row', D_MODEL, BF16)]


def _layer_bwd(dx2, d_g2m, d_dn, sv, below, W, rc, tri, emit):
    mod = sv['mod']
    sh1, sc1, g1m, sh2, sc2, g2m = [mod[i:i + 1] for i in range(N_MOD)]
    D = D_MODEL
    p = sv['p']
    (d_up,) = _mm(d_dn, W["w_down"], tb=True, o_ex=[(sv['act'], 'o')],
                  epi=lambda acc, a: (acc * 2.0 * a.astype(F32),), outs=(BF16,), name="mm_down_dx")
    (gw_down,) = _mm(sv['act'], d_dn, ta=True, tk=DW_TK, pro=lambda a: a * a, outs=(BF16,), name="mm_down_dw")
    (gw_up,) = _mm(sv['h2'], d_up, ta=True, tk=DW_TK, outs=(BF16,), name="mm_up_dw")
    tok = emit(dict(w_down=gw_down, w_up=gw_up), "mlp")
    (d_h2,) = _mm(d_up, W["w_up"], tb=True, tk=2048, outs=(F32,), name="mm_up_dx", deps=[tok])
    dx1, d_sh2, s_h2, d_g1m, d_mo = _norm_bwd(
        d_h2, sv['x1'], sv['r2'], dx2, sv['gv2'], sv['gn2'],
        extra_rows=[sv['mo']], extra_vecs=[g1m],
        extra_fn=lambda dx, mo, g: (dx * mo.astype(F32), dx * g),
        extra_outs=[('sum', D), ('row', D, BF16)], name="norm_bwd_mlp")
    d_sc2 = sv['gn2'] * s_h2
    d_gn2 = (1.0 + sc2) * s_h2
    def mix_epi(acc, ya, yb, ga, gb):
        sa, sb = _sigmoid(ga.astype(F32)), _sigmoid(gb.astype(F32))
        return (acc * sa, acc * sb, acc * ya.astype(F32) * sa * (1.0 - sa), acc * yb.astype(F32) * sb * (1.0 - sb))

    d_ya, d_yb, d_ga, d_gb = _mm(d_mo, W["w_mix_out"], tb=True, tm=512,
                                 o_ex=[(sv['ya'], 'o'), (sv['yb'], 'o'), (p, 'o', O_GA), (p, 'o', O_GB)], epi=mix_epi,
                                 outs=(BF16,) * 4, name="mm_mix_dx")
    (gw_mix,) = _mm(sv['mg'], d_mo, ta=True, tk=DW_TK, outs=(BF16,), name="mm_mix_dw")

    def ro_epi(acc, g, yr):
        gf = g.astype(F32)
        s = _sigmoid(gf)
        return (acc * yr.astype(F32) * s * (1.0 + gf * (1.0 - s)), acc * gf * s)

    d_rg, d_yret = _mm(d_ya, W["w_ret_out"], tb=True, tm=512, o_ex=[(p, 'o', O_RG), (sv['yret'], 'o')], epi=ro_epi,
                       outs=(BF16, BF16), name="mm_ret_dx")
    (gw_ro,) = _mm(sv['yret'], d_ya, ta=True, tm=512, tk=DW_TK, a_ex=[(p, 'a', O_RG)],
                   pro=lambda yr, g: _silu(g.astype(F32)) * yr.astype(F32), outs=(BF16,), name="mm_ret_dw")
    (gw_so,) = _mm(sv['ysb'], d_yb, ta=True, tk=DW_TK, outs=(BF16,), name="mm_sb_dw")
    tok = emit(dict(w_mix_out=gw_mix, w_ret_out=gw_ro, w_sb_out=gw_so), "mix")
    (d_ysb,) = _mm(d_yb, W["w_sb_out"], tb=True, outs=(BF16,), name="mm_sb_dx", deps=[tok])
    d_sq, d_sk, d_sv = _sb_bwd(p, sv['sbc'], d_ysb, tri)
    d_ret = _ret_bwd(p, sv['rstate'], d_yret, rc)
    dp = [d_ret, d_rg, d_sq, d_sk, d_sv, d_ga, d_gb]
    (gw_in,) = _mm(sv['h1'], dp, ta=True, tk=1024, outs=(BF16,), name="mm_in_dw")
    tok = emit(dict(w_in=gw_in), "in")
    (d_h,) = _mm(dp, W["w_in"], tb=True, tm=2048, tk=512, outs=(F32,), name="mm_in_dx", deps=[tok])
    if below is None:
        dx0, d_sh1, s_h1 = _norm_bwd(d_h, sv['x0'], sv['r1'], dx1, sv['gv1'], sv['gn1'], name="norm_bwd_mix")
        gate_below = (None, None)
    else:
        dx0, d_sh1, s_h1, *gate_below = _norm_bwd(
            d_h, sv['x0'], sv['r1'], dx1, sv['gv1'], sv['gn1'], extra_rows=[below['dn']],
            extra_vecs=[below['mod'][N_MOD - 1:N_MOD]], extra_fn=_gate_bwd, extra_outs=_GATE_OUTS,
            name="norm_bwd_mix_gate")
    d_sc1 = sv['gn1'] * s_h1
    d_gn1 = (1.0 + sc1) * s_h1
    d_mod = jnp.concatenate([d_sh1, d_sc1, d_g1m, d_sh2, d_sc2, d_g2m], axis=1)
    return dx0, gate_below, d_mod, d_gn1, d_gn2


def kernel(x, c, norm_mix_g, w_in, w_ret_out, w_sb_out, w_mix_out, norm_mlp_g, w_up, w_down, w_ada, b_ada, final_g, loss_target, m_norm_mix_g, m_w_in, m_w_ret_out, m_w_sb_out, m_w_mix_out, m_norm_mlp_g, m_w_up, m_w_down, m_w_ada, m_b_ada, m_final_g, v_norm_mix_g, v_w_in, v_w_ret_out, v_w_sb_out, v_w_mix_out, v_norm_mlp_g, v_w_up, v_w_down, v_w_ada, v_b_ada, v_final_g):
    S, D = x.shape[1], x.shape[2]
    x0 = x.reshape(S, D)
    tgt = loss_target.reshape(S, D)
    me = 4 * lax.axis_index("x") + 2 * lax.axis_index("y") + lax.axis_index("c")
    wts = dict(w_in=w_in, w_ret_out=w_ret_out, w_sb_out=w_sb_out, w_mix_out=w_mix_out, w_up=w_up, w_down=w_down)
    mts = dict(w_in=m_w_in, w_ret_out=m_w_ret_out, w_sb_out=m_w_sb_out, w_mix_out=m_w_mix_out, w_up=m_w_up, w_down=m_w_down)
    vts = dict(w_in=v_w_in, w_ret_out=v_w_ret_out, w_sb_out=v_w_sb_out, w_mix_out=v_w_mix_out, w_up=v_w_up, w_down=v_w_down)
    rc = _ret_consts(S)
    tri = _tri()

    (cact,) = _ew(lambda t: (_silu(t),), [jnp.pad(c, ((0, 7), (0, 0)))], outs=[('row', D, F32)], name="silu_c")
    (cact_all,) = _all_gather_lead([cact[0:1]], "gather_c")
    cact_all = cact_all.reshape(N_DEV, D)
    cact16 = jnp.pad(cact_all, ((0, 8), (0, 0)))
    n_ada = w_ada.shape[2]
    b_loc = lax.dynamic_slice_in_dim(b_ada, me * n_ada, n_ada, axis=1)
    mods = [_mod_partial(cact16, w_ada[l], b_loc[l:l + 1])[:N_DEV] for l in range(DEPTH)]
    modp = jnp.stack(mods, axis=1)
    (modr,) = _all_to_all_lead([modp], "scatter_mod")
    mod_full = jnp.transpose(modr, (1, 0, 2)).reshape(DEPTH, N_MOD, D)

    shards = {}
    for nm in _W_NAMES:
        w = wts[nm]
        (wb,) = _ew(lambda t: (t,), [w.reshape(-1, w.shape[-1])], outs=[('row', w.shape[-1], BF16)], name="cast_bf16")
        shards[nm] = wb.reshape(w.shape)
    rest = tuple(nm for nm in _W_NAMES if nm != "w_in")
    sh = [{nm: shards[nm][l] for nm in _W_NAMES} for l in range(DEPTH)]
    h_in = _gather_start(sh[0], ("w_in",), "0_in", [modr], two_level=True)

    pre = [(norm_mix_g[l:l + 1] * (1.0 + mod_full[l][1:2]), mod_full[l][0:1]) for l in range(DEPTH)]

    def first(t, gv, sh):
        r = _rstd(t)
        return r, _norm_mod(t, r, gv, sh)

    xs = x0
    rs, hs = _ew(first, [x0], vecs=list(pre[0]), outs=[('col', F32), ('row', D, BF16)], name="row_rstd")

    win, pass_name, passed_name = h_in['second']
    h_pass = _sibling_start(_exchange_wait(h_in, [hs], h_in['tag']), win, pass_name)
    h_pass['span'] = h_in['span']
    started = [h_pass['token']]
    layer_groups = []
    for l, groups in enumerate([[(rest, "0_rest")]] + [[(_W_NAMES, "%d_all" % l)] for l in range(1, DEPTH)]):
        layer_groups.append([])
        for names, tag in groups:
            layer_groups[-1].append((names, _gather_start(sh[l], names, tag, started[-1:])))
            started.append(layer_groups[-1][-1][1]['token'])
    (w_in0,) = _exchange_wait(h_pass, started, passed_name)
    layer_w = [_LayerWeights(g, started) for g in layer_groups]
    layer_w[0].got["w_in"] = w_in0
    saved = []
    for l in range(DEPTH):
        xs, rs, hs, sv = _layer_fwd(xs, rs, hs, mod_full[l], norm_mix_g[l:l + 1], norm_mlp_g[l:l + 1],
                                    pre[l + 1] if l + 1 < DEPTH else None, layer_w[l], rc, tri)
        sv['W'] = layer_w[l]
        saved.append(sv)

    fg = final_g.reshape(1, D)

    def head(xt, tg, dn, g, g2m, r):
        xh = xt * r
        e = xh * g - tg
        dy = e * (1.0 / D)
        dxh = dy * g
        dx = r * (dxh - xh * jnp.mean(dxh * xh, axis=-1, keepdims=True))
        return (dx, dy * xh, 0.5 * e * e * (1.0 / D)) + _gate_bwd(dx, dn, g2m)

    top = saved[DEPTH - 1]
    dxs, d_fg, loss_cols, *gate = _ew(head, [xs, tgt, top['dn']], vecs=[fg, top['mod'][N_MOD - 1:N_MOD]], cols=[rs],
                                      outs=[('row', D, F32), ('sum', D), ('sum', D)] + _GATE_OUTS, name="loss_head")

    small = [None] * DEPTH
    pending = []
    for l in reversed(range(DEPTH)):
        sv = saved[l]

        def emit(gw, tag, l=l):
            names = tuple(gw)
            pending.append((l, names, _scatter_start(gw, names, "%d_%s" % (l, tag))))
            return pending[-1][2]['token']

        dxs, gate, d_mod, d_gn1, d_gn2 = _layer_bwd(dxs, gate[0], gate[1], sv, saved[l - 1] if l else None,
                                                    sv['W'], rc, tri, emit)
        small[l] = (d_mod, d_gn1, d_gn2)
    grad_x = dxs.reshape(1, S, D)

    res = {}
    after = [dxs]
    for l, names, h in pending:
        for nm, landed in zip(names, _finish(h, after)):
            res[nm] = _adam_layer(landed, wts[nm], mts[nm], vts[nm], l, res.get(nm), "adam_layer")
        after = [res[names[-1]][0]]

    pack = jnp.concatenate([small[l][0] for l in range(DEPTH)] + [small[l][1] for l in range(DEPTH)]
                           + [small[l][2] for l in range(DEPTH)] + [d_fg, loss_cols], axis=1)
    (packs,) = _all_gather_lead([pack], "gather_small", deps=after)
    packs = packs.reshape(N_DEV, -1)
    o = 0
    dmod_all = []
    for l in range(DEPTH):
        dmod_all.append(packs[:, o:o + N_MOD * D]); o += N_MOD * D
    gn1_parts = packs[:, o:o + DEPTH * D].reshape(N_DEV, DEPTH, D); o += DEPTH * D
    gn2_parts = packs[:, o:o + DEPTH * D].reshape(N_DEV, DEPTH, D); o += DEPTH * D
    fg_parts = packs[:, o:o + D].reshape(N_DEV, 1, D); o += D
    loss_parts = packs[:, o:o + D]
    (loss_sum,) = _ew(lambda t: (t,), [loss_parts], outs=[('sum', D)], name="loss_sum")
    loss = jnp.sum(loss_sum)

    res["norm_mix_g"] = _adam(gn1_parts, norm_mix_g, m_norm_mix_g, v_norm_mix_g, "adam")
    res["norm_mlp_g"] = _adam(gn2_parts, norm_mlp_g, m_norm_mlp_g, v_norm_mlp_g, "adam")
    fgr = _adam(fg_parts, fg, m_final_g.reshape(1, D), v_final_g.reshape(1, D), "adam")
    res["final_g"] = [t.reshape(D) for t in fgr]
    bparts = jnp.stack(dmod_all, axis=1)
    res["b_ada"] = _adam(bparts, b_ada, m_b_ada, v_b_ada, "adam")
    cact_t = cact_all.T
    for l in range(DEPTH):
        dm_loc = lax.dynamic_slice_in_dim(dmod_all[l], me * n_ada, n_ada, axis=1)
        res["w_ada"] = _adam_layer(_ada_grad(cact_t, dm_loc), w_ada, m_w_ada, v_w_ada, l, res.get("w_ada"),
                                   "adam_layer")

    order = ['norm_mix_g', 'w_in', 'w_ret_out', 'w_sb_out', 'w_mix_out', 'norm_mlp_g', 'w_up', 'w_down', 'w_ada', 'b_ada', 'final_g']
    out = [loss, grad_x]
    for i in range(4):
        out += [res[nm][i] for nm in order]
    return tuple(out)
```

```python
import functools
import math

import jax
import jax.numpy as jnp
import numpy as np
from jax import lax
from jax.experimental import pallas as pl
from jax.experimental.pallas import tpu as pltpu

F32 = jnp.float32
BF16 = jnp.bfloat16

N_DEV = 8
D_MODEL = 1024
DEPTH = 2
RET_HEADS = 4
RET_QK = 256
RET_V = 512
RET_CHUNK = 256
ROPE_BASE = 10000.0
SB_HEADS = 16
SB_DIM = 64
D_FF = 4096
N_MOD = 6
EPS = 1e-6
GN_EPS = 1e-5
O_RQ, O_RK, O_RV, O_RG, O_SQ, O_SK, O_SV, O_GA, O_GB = 0, 1024, 2048, 4096, 6144, 7168, 8192, 9216, 10240
IN_W = 11264

ADAM_LR, ADAM_B1, ADAM_B2, ADAM_EPS, ADAM_WD, ADAM_STEP = 0.001, 0.9, 0.999, 1e-08, 0.01, 10

VMEM_LIMIT = 56 * 1024 * 1024
DW_TK = 2048


def _cparams(sem):
    return pltpu.CompilerParams(dimension_semantics=sem, vmem_limit_bytes=VMEM_LIMIT)


def _mm(a, b, *, ta=False, tb=False, tm=1024, tn=1024, tk=None, a_ex=(), pro=None, o_ex=(), epi=None,
        outs=(F32,), cols=0, name, deps=()):
    a_parts = list(a) if isinstance(a, (list, tuple)) else [a]
    b_parts = list(b) if isinstance(b, (list, tuple)) else [b]
    assert not (ta and len(a_parts) > 1) and not (tb and len(b_parts) > 1)
    if ta:
        K, M = a.shape
    else:
        M, K = a_parts[0].shape[0], sum(t.shape[1] for t in a_parts)
    N = b.shape[0] if tb else sum(t.shape[1] for t in b_parts)
    tm, tn, tk = min(tm, M), min(tn, N), K if tk is None else min(tk, K)
    assert M % tm == 0 and N % tn == 0 and K % tk == 0, (name, M, N, K, tm, tn, tk)
    nk = K // tk
    multi = len(a_parts) > 1 or len(b_parts) > 1
    direct = epi is None and tuple(outs) == (F32,) and cols == 0

    def ranges(parts, t):
        out, o = [], 0
        for arr in parts:
            assert arr.shape[1] % t == 0
            out.append((o, o + arr.shape[1] // t))
            o += arr.shape[1] // t
        return out

    a_rng = ranges(a_parts, tk) if len(a_parts) > 1 else [(0, nk)]
    b_rng = ranges(b_parts, tn) if len(b_parts) > 1 else [(0, N // tn)]
    clip = lambda v, lo, hi: jnp.clip(v - lo, 0, hi - lo - 1)
    mine = lambda v, lo, hi, w: jnp.where((v >= lo) & (v < hi), w, 0)
    in_specs, args = [], []
    for arr, (lo, hi) in zip(a_parts, a_rng):
        in_specs.append(pl.BlockSpec((tk, tm), lambda i, j, k: (k, i)) if ta
                        else pl.BlockSpec((tm, tk), lambda i, j, k, lo=lo, hi=hi: (i, clip(k, lo, hi))))
        args.append(arr)
    for arr, (lo, hi) in zip(b_parts, b_rng):
        in_specs.append(pl.BlockSpec((tn, tk), lambda i, j, k: (j, k)) if tb
                        else pl.BlockSpec((tk, tn), lambda i, j, k, lo=lo, hi=hi: (mine(j, lo, hi, k), clip(j, lo, hi))))
        args.append(arr)
    npa, npb = len(a_parts), len(b_parts)
    for arr, kind, *off in a_ex:
        off = off[0] if off else 0
        if kind == 'a' and ta:
            assert off % tm == 0
            in_specs.append(pl.BlockSpec((tk, tm), lambda i, j, k, o=off // tm: (k, o + i)))
        elif kind == 'a':
            assert off % tk == 0
            in_specs.append(pl.BlockSpec((tm, tk), lambda i, j, k, o=off // tk: (i, o + k)))
        elif kind == 'k':
            in_specs.append(pl.BlockSpec((tk, 1), lambda i, j, k: (k, 0)) if ta
                            else pl.BlockSpec((1, tk), lambda i, j, k: (0, k)))
        else:
            in_specs.append(pl.BlockSpec((1, tm), lambda i, j, k: (0, i)) if ta
                            else pl.BlockSpec((tm, 1), lambda i, j, k: (i, 0)))
        args.append(arr)
    for arr, kind, *off in o_ex:
        off = off[0] if off else 0
        if kind == 'o':
            assert off % tn == 0
            in_specs.append(pl.BlockSpec((tm, tn), lambda i, j, k, o=off // tn: (i, o + j)))
        elif kind == 'n':
            in_specs.append(pl.BlockSpec((1, tn), lambda i, j, k: (0, j)))
        else:
            in_specs.append(pl.BlockSpec((tm, 1), lambda i, j, k: (i, 0)))
        args.append(arr)
    for arr in deps:
        in_specs.append(pl.BlockSpec(memory_space=pl.ANY))
        args.append(arr)
    assert cols == 0 or N == tn
    na, no, nout, nd = len(a_ex), len(o_ex), len(outs) + cols, len(deps)
    dims = (((0 if ta else 1,), (1 if tb else 0,)), ((), ()))

    def body(*refs):
        a_refs, b_refs = refs[:npa], refs[npa:npa + npb]
        n0 = npa + npb
        aex = refs[n0:n0 + na]
        oex = refs[n0 + na:n0 + na + no]
        out_refs = refs[n0 + na + no + nd:n0 + na + no + nd + nout]

        def product(a_ref, b_ref):
            at = a_ref[...]
            if pro is not None:
                at = pro(at, *[r[...] for r in aex])
            return lax.dot_general(at.astype(BF16), b_ref[...].astype(BF16), dims, preferred_element_type=F32)

        def finish(res):
            vals = epi(res, *[r[...] for r in oex]) if epi is not None else (res,)
            for o_ref, v in zip(out_refs, vals):
                o_ref[...] = v.astype(o_ref.dtype)

        if nk == 1 and not multi:
            finish(product(a_refs[0], b_refs[0]))
            return
        acc = out_refs[0] if direct else refs[-1]
        j, k = pl.program_id(1), pl.program_id(2)
        if multi:
            @pl.when(k == 0)
            def _():
                acc[...] = jnp.zeros_like(acc)

            for a_ref, (alo, ahi) in zip(a_refs, a_rng):
                for b_ref, (blo, bhi) in zip(b_refs, b_rng):
                    @pl.when((k >= alo) & (k < ahi) & (j >= blo) & (j < bhi))
                    def _():
                        acc[...] += product(a_ref, b_ref)
        else:
            @pl.when(k == 0)
            def _():
                acc[...] = product(a_refs[0], b_refs[0])

            @pl.when(k > 0)
            def _():
                acc[...] += product(a_refs[0], b_refs[0])

        if not direct:
            @pl.when(k == nk - 1)
            def _():
                finish(acc[...])

    res = pl.pallas_call(
        body, name=name, grid=(M // tm, N // tn, nk), in_specs=in_specs,
        out_specs=[pl.BlockSpec((tm, tn), lambda i, j, k: (i, j)) for _ in outs]
        + [pl.BlockSpec((tm, 1), lambda i, j, k: (i, 0))] * cols,
        out_shape=[jax.ShapeDtypeStruct((M, N), dt) for dt in outs] + [jax.ShapeDtypeStruct((M, 1), F32)] * cols,
        scratch_shapes=[pltpu.VMEM((tm, tn), F32)] if (nk > 1 or multi) and not direct else [],
        compiler_params=_cparams(("parallel", "parallel", "arbitrary")),
    )(*args)
    return res


def _ew(fn, rows, vecs=(), cols=(), outs=(), tr=256, name=None, deps=()):
    S = rows[0].shape[0]
    tr = min(tr, S)
    assert S % tr == 0
    in_specs, args = [], []
    for r in rows:
        in_specs.append(pl.BlockSpec((tr, r.shape[1]), lambda i: (i, 0)))
        args.append(r)
    for v in vecs:
        in_specs.append(pl.BlockSpec((1, v.shape[1]), lambda i: (0, 0)))
        args.append(v)
    for c in cols:
        in_specs.append(pl.BlockSpec((tr, 1), lambda i: (i, 0)))
        args.append(c)
    out_specs, out_shape = [], []
    for o in outs:
        if o[0] == 'row':
            out_specs.append(pl.BlockSpec((tr, o[1]), lambda i: (i, 0)))
            out_shape.append(jax.ShapeDtypeStruct((S, o[1]), o[2]))
        elif o[0] == 'sum':
            out_specs.append(pl.BlockSpec((1, o[1]), lambda i: (0, 0)))
            out_shape.append(jax.ShapeDtypeStruct((1, o[1]), F32))
        else:
            out_specs.append(pl.BlockSpec((tr, 1), lambda i: (i, 0)))
            out_shape.append(jax.ShapeDtypeStruct((S, 1), o[1]))
    nin = len(args)
    in_specs += [pl.BlockSpec(memory_space=pl.ANY)] * len(deps)
    args += list(deps)

    def body(*refs):
        i = pl.program_id(0)
        vals = fn(*[r[...] for r in refs[:nin]])
        for o, o_ref, v in zip(outs, refs[nin + len(deps):], vals):
            if o[0] == 'sum':
                @pl.when(i == 0)
                def _():
                    o_ref[...] = jnp.zeros_like(o_ref)
                o_ref[...] += jnp.sum(v.astype(F32), axis=0, keepdims=True)
            else:
                o_ref[...] = v.astype(o_ref.dtype)

    return pl.pallas_call(
        body, name=name, grid=(S // tr,), in_specs=in_specs, out_specs=out_specs, out_shape=out_shape,
        compiler_params=_cparams(("arbitrary",)),
    )(*args)


def _sigmoid(x):
    return 1.0 / (1.0 + jnp.exp(-x))


def _ret_consts(S):
    h = np.arange(RET_HEADS, dtype=np.float64)
    log_gamma = np.log1p(-np.power(2.0, -5.0 - h))
    idx = np.arange(RET_CHUNK, dtype=np.float64)
    rel = idx[:, None] - idx[None, :]
    decay = np.where(rel >= 0, np.exp(np.maximum(rel, 0.0) * log_gamma[:, None, None]), 0.0)
    xi = np.exp((idx + 1.0) * log_gamma[:, None])[:, :, None]
    zeta = np.exp((RET_CHUNK - 1.0 - idx) * log_gamma[:, None])[:, :, None]
    gamma_c = np.exp(RET_CHUNK * log_gamma)[:, None, None]
    half = RET_QK // 2
    inv_freq = np.power(ROPE_BASE, -np.arange(half, dtype=np.float64) / half).astype(np.float32)
    ang = np.arange(S, dtype=np.float32)[:, None] * inv_freq[None, :]
    f = lambda t: jnp.asarray(t, F32)
    return dict(decay=f(decay), xi=f(xi), zeta=f(zeta), gc=f(gamma_c), cos=f(np.cos(ang)), sin=f(np.sin(ang)))


def _rot(t, cos, sin):
    half = RET_QK // 2
    t1, t2 = t[:, :half], t[:, half:]
    return jnp.concatenate([t1 * cos - t2 * sin, t1 * sin + t2 * cos], axis=-1)


def _rot_inv(t, cos, sin):
    half = RET_QK // 2
    t1, t2 = t[:, :half], t[:, half:]
    return jnp.concatenate([t1 * cos + t2 * sin, t2 * cos - t1 * sin], axis=-1)


_NT = (((1,), (1,)), ((), ()))
_TN = (((0,), (0,)), ((), ()))


def _dot(a, b):
    return jnp.dot(a, b, preferred_element_type=F32)


def _dot_nt(a, b):
    return lax.dot_general(a, b, _NT, preferred_element_type=F32)


def _dot_tn(a, b):
    return lax.dot_general(a, b, _TN, preferred_element_type=F32)


_QW, _VW = RET_HEADS * RET_QK, RET_HEADS * RET_V
_HEADS = range(RET_HEADS)


def _ret_in_specs(C, rev, NC):
    n_of = (lambda n: NC - 1 - n) if rev else (lambda n: n)
    whole3 = lambda n: (0, 0, 0)
    return [
        pl.BlockSpec((C, _QW), lambda n: (n_of(n), O_RQ // _QW)),
        pl.BlockSpec((C, _QW), lambda n: (n_of(n), O_RK // _QW)),
        pl.BlockSpec((C, _VW), lambda n: (n_of(n), O_RV // _VW)),
        pl.BlockSpec((C, RET_QK // 2), lambda n: (n_of(n), 0)),
        pl.BlockSpec((C, RET_QK // 2), lambda n: (n_of(n), 0)),
        pl.BlockSpec((RET_HEADS, C, C), whole3),
        pl.BlockSpec((RET_HEADS, C, 1), whole3),
        pl.BlockSpec((RET_HEADS, C, 1), whole3),
        pl.BlockSpec((RET_HEADS, 1, 1), whole3),
    ]


def _qk_heads(q_ref, k_ref, cos, sin):
    qs, kfs = [], []
    for h in _HEADS:
        cols = slice(h * RET_QK, (h + 1) * RET_QK)
        qs.append(_rot(q_ref[:, cols].astype(F32), cos, sin).astype(BF16))
        kfs.append(_rot(k_ref[:, cols].astype(F32), cos, sin) * (RET_QK ** -0.5))
    return qs, kfs


def _ret_fwd(p, rc):
    S = p.shape[0]
    C = RET_CHUNK
    NC = S // C

    def body(q_ref, k_ref, v_ref, cos_ref, sin_ref, dec_ref, xi_ref, zeta_ref, gc_ref, y_ref, rs_ref, r_acc):
        n = pl.program_id(0)

        @pl.when(n == 0)
        def _():
            r_acc[...] = jnp.zeros_like(r_acc)

        cos, sin = cos_ref[...], sin_ref[...]
        qs, kfs = _qk_heads(q_ref, k_ref, cos, sin)
        vs = [v_ref[:, h * RET_V:(h + 1) * RET_V] for h in _HEADS]
        rbs = [r_acc[h].astype(BF16) for h in _HEADS]
        for h in _HEADS:
            rs_ref[h, 0] = rbs[h]
        ss = [(_dot_nt(qs[h], kfs[h].astype(BF16)) * dec_ref[h]).astype(BF16) for h in _HEADS]
        os = [_dot(ss[h], vs[h]) + _dot(qs[h], rbs[h]) * xi_ref[h] for h in _HEADS]
        for h in _HEADS:
            o = os[h]
            mu = jnp.mean(o, axis=-1, keepdims=True)
            var = jnp.mean(jnp.square(o - mu), axis=-1, keepdims=True)
            y_ref[:, h * RET_V:(h + 1) * RET_V] = ((o - mu) * lax.rsqrt(var + GN_EPS)).astype(y_ref.dtype)
        for h in _HEADS:
            kz = (kfs[h] * zeta_ref[h]).astype(BF16)
            r_acc[h] = r_acc[h] * gc_ref[h] + _dot_tn(kz, vs[h])

    return pl.pallas_call(
        body, name="ret_fwd", grid=(NC,), in_specs=_ret_in_specs(C, False, NC),
        out_specs=[pl.BlockSpec((C, _VW), lambda n: (n, 0)),
                   pl.BlockSpec((RET_HEADS, 1, RET_QK, RET_V), lambda n: (0, n, 0, 0))],
        out_shape=[jax.ShapeDtypeStruct((S, _VW), BF16),
                   jax.ShapeDtypeStruct((RET_HEADS, NC, RET_QK, RET_V), BF16)],
        scratch_shapes=[pltpu.VMEM((RET_HEADS, RET_QK, RET_V), F32)],
        compiler_params=_cparams(("arbitrary",)),
    )(p, p, p, rc['cos'], rc['sin'], rc['decay'], rc['xi'], rc['zeta'], rc['gc'])


def _ret_bwd(p, rstate, dy, rc):
    S = p.shape[0]
    C = RET_CHUNK
    NC = S // C

    def body(q_ref, k_ref, v_ref, cos_ref, sin_ref, dec_ref, xi_ref, zeta_ref, gc_ref, rs_ref, dy_ref,
             d_ref, dr_acc):
        dq_ref, dk_ref, dv_ref = d_ref.at[:, 0:_QW], d_ref.at[:, _QW:2 * _QW], d_ref.at[:, 2 * _QW:2 * _QW + _VW]
        t = pl.program_id(0)

        @pl.when(t == 0)
        def _():
            dr_acc[...] = jnp.zeros_like(dr_acc)

        cos, sin = cos_ref[...], sin_ref[...]
        qs, kfs = _qk_heads(q_ref, k_ref, cos, sin)
        ks = [kf.astype(BF16) for kf in kfs]
        vs = [v_ref[:, h * RET_V:(h + 1) * RET_V] for h in _HEADS]
        rbs = [rs_ref[h, 0] for h in _HEADS]
        ss = [(_dot_nt(qs[h], ks[h]) * dec_ref[h]).astype(BF16) for h in _HEADS]
        os = [_dot(ss[h], vs[h]) + _dot(qs[h], rbs[h]) * xi_ref[h] for h in _HEADS]
        dobs, doxis = [], []
        for h in _HEADS:
            o = os[h]
            mu = jnp.mean(o, axis=-1, keepdims=True)
            var = jnp.mean(jnp.square(o - mu), axis=-1, keepdims=True)
            rstd = lax.rsqrt(var + GN_EPS)
            yh = (o - mu) * rstd
            dyf = dy_ref[:, h * RET_V:(h + 1) * RET_V].astype(F32)
            do = (dyf - jnp.mean(dyf, axis=-1, keepdims=True)
                  - yh * jnp.mean(dyf * yh, axis=-1, keepdims=True)) * rstd
            dobs.append(do.astype(BF16))
            doxis.append((do * xi_ref[h]).astype(BF16))
        drbs = [dr_acc[h].astype(BF16) for h in _HEADS]
        dss = [(_dot_nt(dobs[h], vs[h]) * dec_ref[h]).astype(BF16) for h in _HEADS]
        for h in _HEADS:
            dq = _dot(dss[h], ks[h]) + _dot_nt(doxis[h], rbs[h])
            dq_ref[:, h * RET_QK:(h + 1) * RET_QK] = _rot_inv(dq, cos, sin).astype(dq_ref.dtype)
        for h in _HEADS:
            dk = _dot_tn(dss[h], qs[h]) + _dot_nt(vs[h], drbs[h]) * zeta_ref[h]
            dk_ref[:, h * RET_QK:(h + 1) * RET_QK] = (_rot_inv(dk, cos, sin) * (RET_QK ** -0.5)).astype(dk_ref.dtype)
        for h in _HEADS:
            kz = (kfs[h] * zeta_ref[h]).astype(BF16)
            dv = _dot_tn(ss[h], dobs[h]) + _dot(kz, drbs[h])
            dv_ref[:, h * RET_V:(h + 1) * RET_V] = dv.astype(dv_ref.dtype)
        for h in _HEADS:
            dr_acc[h] = dr_acc[h] * gc_ref[h] + _dot_tn(qs[h], doxis[h])

    rn = lambda n: NC - 1 - n
    in_specs = _ret_in_specs(C, True, NC) + [
        pl.BlockSpec((RET_HEADS, 1, RET_QK, RET_V), lambda n: (0, rn(n), 0, 0)),
        pl.BlockSpec((C, _VW), lambda n: (rn(n), 0)),
    ]
    return pl.pallas_call(
        body, name="ret_bwd", grid=(NC,), in_specs=in_specs,
        out_specs=pl.BlockSpec((C, 2 * _QW + _VW), lambda n: (rn(n), 0)),
        out_shape=jax.ShapeDtypeStruct((S, 2 * _QW + _VW), BF16),
        scratch_shapes=[pltpu.VMEM((RET_HEADS, RET_QK, RET_V), F32)],
        compiler_params=_cparams(("arbitrary",)),
    )(p, p, p, rc['cos'], rc['sin'], rc['decay'], rc['xi'], rc['zeta'], rc['gc'], rstate, dy)


SB_T = 256
SB_SCALE = SB_DIM ** -0.5


def _tri():
    j = np.arange(SB_T)
    after = (j[:, None] > j[None, :]).astype(np.float32)
    upto = (j[:, None] <= j[None, :]).astype(np.float32)
    return jnp.asarray(np.stack([after, upto]), BF16)


def _softplus_parts(z):
    neg_abs = lax.bitcast_convert_type(lax.bitcast_convert_type(z, jnp.uint32) | jnp.uint32(0x80000000), F32)
    e = jnp.exp(neg_abs)
    return jnp.maximum(z, 0.0) + jnp.log(1.0 + e), e


def _sb_fwd(p, tri):
    S = p.shape[0]
    T = min(SB_T, S)
    NQ = S // T
    assert NQ <= 128
    qb, kb, vb = O_SQ // 128, O_SK // 128, O_SV // 128

    def body(q_ref, k_ref, v_ref, tri_ref, o_ref, cs_ref, o_acc, run, zbuf, abuf):
        i = pl.program_id(1)
        lane = lax.broadcasted_iota(jnp.int32, (1, 128), 1)
        tri_after = tri_ref[0]
        qs = [jnp.where((lane >= 64) if hh else (lane < 64), q_ref[...], jnp.zeros_like(q_ref[...]))
              * jnp.asarray(SB_SCALE, BF16) for hh in range(2)]
        cs_ref[...] = jnp.zeros_like(cs_ref)
        o_acc[...] = jnp.zeros_like(o_acc)
        run[...] = jnp.zeros_like(run)

        def kv(ref, j):
            return ref[pl.ds(pl.multiple_of(j * T, T), T), :]

        for hh in range(2):
            zbuf[hh] = _dot_nt(qs[hh], kv(k_ref, i))

        def block(t, diagonal):
            j = i - t
            if diagonal:
                msk = lax.broadcasted_iota(jnp.int32, (T, T), 1) < lax.broadcasted_iota(jnp.int32, (T, T), 0)
            if not diagonal:
                av = [_dot(abuf[hh], kv(v_ref, j + 1)) for hh in range(2)]
            lss, exs, tot, zn = [], [], [], []
            for hh in range(2):
                z = zbuf[hh]
                sp, _ = _softplus_parts(z)
                lss.append(z - sp)
                if diagonal:
                    sp = jnp.where(msk, sp, 0.0)
                exs.append(_dot(sp.astype(BF16), tri_after))
                tot.append(sp[:, 0:1])
                zn.append(_dot_nt(qs[hh], kv(k_ref, jnp.maximum(j - 1, 0))))
            for hh in range(2):
                csl = slice(hh * 128, (hh + 1) * 128)
                cs = run[hh]
                a = jnp.exp(lss[hh] - exs[hh] - cs)
                if diagonal:
                    a = jnp.where(msk, a, 0.0)
                abuf[hh] = a.astype(BF16)
                cs_ref[:, csl] = jnp.where(lane == j, cs, cs_ref[:, csl])
                run[hh] = cs + exs[hh][:, 0:1] + tot[hh]
            for hh in range(2):
                if not diagonal:
                    o_acc[hh] += av[hh]
                zbuf[hh] = zn[hh]

        block(0, True)

        def step(t, carry):
            block(t, False)
            return carry

        lax.fori_loop(1, i + 1, step, 0)
        o_ref[...] = jnp.where(lane < 64, o_acc[0] + _dot(abuf[0], kv(v_ref, 0)),
                               o_acc[1] + _dot(abuf[1], kv(v_ref, 0))).astype(o_ref.dtype)

    return pl.pallas_call(
        body, name="sb_fwd", grid=(SB_HEADS // 2, NQ),
        scratch_shapes=[pltpu.VMEM((2, T, 128), F32), pltpu.VMEM((2, T, 1), F32), pltpu.VMEM((2, T, T), F32),
                        pltpu.VMEM((2, T, T), BF16)],
        in_specs=[pl.BlockSpec((T, 128), lambda h, i: (i, qb + h)),
                  pl.BlockSpec((S, 128), lambda h, i: (0, kb + h)),
                  pl.BlockSpec((S, 128), lambda h, i: (0, vb + h)),
                  pl.BlockSpec((1, T, T), lambda h, i: (0, 0, 0))],
        out_specs=[pl.BlockSpec((T, 128), lambda h, i: (i, h)),
                   pl.BlockSpec((T, 256), lambda h, i: (i, h))],
        out_shape=[jax.ShapeDtypeStruct((S, SB_HEADS * SB_DIM), BF16),
                   jax.ShapeDtypeStruct((S, SB_HEADS * 128), F32)],
        compiler_params=_cparams(("parallel", "arbitrary")),
    )(p, p, p, tri)


def _sb_bwd(p, carries, dy, tri):
    S = p.shape[0]
    T = min(SB_T, S)
    NQ = S // T
    qb, kb, vb = O_SQ // 128, O_SK // 128, O_SV // 128

    def body(q_ref, k_ref, v_ref, cs_ref, dy_ref, tri_ref, dq_ref, dk_ref, dv_ref, dk_acc, dv_acc, dq_acc, run,
             zbuf, dabuf, dzbuf, abuf):
        i = pl.program_id(1)

        @pl.when(i == 0)
        def _():
            dk_acc[...] = jnp.zeros_like(dk_acc)
            dv_acc[...] = jnp.zeros_like(dv_acc)

        lane = lax.broadcasted_iota(jnp.int32, (1, 128), 1)
        tri_after, tri_upto = tri_ref[0], tri_ref[1]
        hms = [(lane >= 64) if hh else (lane < 64) for hh in range(2)]
        qs = [jnp.where(hm, q_ref[...], jnp.zeros_like(q_ref[...])) * jnp.asarray(SB_SCALE, BF16) for hm in hms]
        dos = [jnp.where(hm, dy_ref[...], jnp.zeros_like(dy_ref[...])) for hm in hms]
        qst = [t.T for t in qs]
        dost = [t.T for t in dos]
        dq_acc[...] = jnp.zeros_like(dq_acc)
        run[...] = jnp.zeros_like(run)
        dzbuf[...] = jnp.zeros_like(dzbuf)
        abuf[...] = jnp.zeros_like(abuf)

        def kv(ref, j):
            return ref[pl.ds(pl.multiple_of(j * T, T), T), :]

        def flush(jp):
            kp = kv(k_ref, jp)
            dq_add = [_dot(dzbuf[hh], kp) for hh in range(2)]
            dk_add = _dot(qst[0], dzbuf[0]) + _dot(qst[1], dzbuf[1])
            dv_add = _dot(dost[0], abuf[0]) + _dot(dost[1], abuf[1])
            return dq_add, dk_add, dv_add

        def apply(jp, adds):
            dq_add, dk_add, dv_add = adds
            cols = pl.ds(pl.multiple_of(jp * T, T), T)
            for hh in range(2):
                dq_acc[hh] += dq_add[hh]
            dk_acc[:, cols] += dk_add
            dv_acc[:, cols] += dv_add

        for hh in range(2):
            zbuf[hh] = _dot_nt(qs[hh], kv(k_ref, 0))
            dabuf[hh] = _dot_nt(dos[hh], kv(v_ref, 0))

        def block(j, diagonal):
            jp = jnp.maximum(j - 1, 0)
            if diagonal:
                msk = lax.broadcasted_iota(jnp.int32, (T, T), 1) < lax.broadcasted_iota(jnp.int32, (T, T), 0)
            kp = kv(k_ref, jp)
            dq_add = [_dot(dzbuf[hh], kp) for hh in range(2)]
            sigs, lss, exs, zn, dan, dk_part, dv_part = [], [], [], [], [], [], []
            for hh in range(2):
                z = zbuf[hh]
                sp, _ = _softplus_parts(z)
                lss.append(z - sp)
                sigs.append(jnp.exp(lss[hh]))
                if diagonal:
                    sp = jnp.where(msk, sp, 0.0)
                exs.append(_dot(sp.astype(BF16), tri_after))
                if not diagonal:
                    zn.append(_dot_nt(qs[hh], kv(k_ref, j + 1)))
                dk_part.append(_dot(qst[hh], dzbuf[hh]))
            pgs, gs = [], []
            for hh in range(2):
                csl = slice(hh * 128, (hh + 1) * 128)
                cs = jnp.sum(jnp.where(lane == j, cs_ref[:, csl], 0.0), axis=-1, keepdims=True)
                a = jnp.exp(lss[hh] - exs[hh] - cs)
                if diagonal:
                    a = jnp.where(msk, a, 0.0)
                abuf_new = a.astype(BF16)
                g = a * dabuf[hh]
                gs.append((g, abuf_new))
                pgs.append(_dot(g.astype(BF16), tri_upto))
                if not diagonal:
                    dan.append(_dot_nt(dos[hh], kv(v_ref, j + 1)))
                dv_part.append(_dot(dost[hh], abuf[hh]))
            adds = (dq_add, dk_part[0] + dk_part[1], dv_part[0] + dv_part[1])
            for hh in range(2):
                g, abuf_new = gs[hh]
                cg = run[hh]
                dz = g - sigs[hh] * (cg + pgs[hh])
                if diagonal:
                    dz = jnp.where(msk, dz, 0.0)
                run[hh] = cg + pgs[hh][:, T - 1:T]
                dzbuf[hh] = dz.astype(BF16)
                abuf[hh] = abuf_new
            apply(jp, adds)
            if not diagonal:
                for hh in range(2):
                    zbuf[hh] = zn[hh]
                    dabuf[hh] = dan[hh]

        def step(j, carry):
            block(j, False)
            return carry

        lax.fori_loop(0, i, step, 0)
        block(i, True)
        apply(i, flush(i))
        dq_ref[...] = (jnp.where(lane < 64, dq_acc[0], dq_acc[1]) * SB_SCALE).astype(dq_ref.dtype)

        @pl.when(i == NQ - 1)
        def _():
            dk_ref[...] = dk_acc[...].T.astype(dk_ref.dtype)
            dv_ref[...] = dv_acc[...].T.astype(dv_ref.dtype)

    W = SB_HEADS * SB_DIM
    return pl.pallas_call(
        body, name="sb_bwd", grid=(SB_HEADS // 2, NQ),
        in_specs=[pl.BlockSpec((T, 128), lambda h, i: (i, qb + h)),
                  pl.BlockSpec((S, 128), lambda h, i: (0, kb + h)),
                  pl.BlockSpec((S, 128), lambda h, i: (0, vb + h)),
                  pl.BlockSpec((T, 256), lambda h, i: (i, h)),
                  pl.BlockSpec((T, 128), lambda h, i: (i, h)),
                  pl.BlockSpec((2, T, T), lambda h, i: (0, 0, 0))],
        out_specs=[pl.BlockSpec((T, 128), lambda h, i: (i, h)),
                   pl.BlockSpec((S, 128), lambda h, i: (0, h)),
                   pl.BlockSpec((S, 128), lambda h, i: (0, h))],
        out_shape=[jax.ShapeDtypeStruct((S, W), BF16)] * 3,
        scratch_shapes=[pltpu.VMEM((128, S), F32), pltpu.VMEM((128, S), F32), pltpu.VMEM((2, T, 128), F32),
                        pltpu.VMEM((2, T, 1), F32), pltpu.VMEM((2, T, T), F32), pltpu.VMEM((2, T, T), F32),
                        pltpu.VMEM((2, T, T), BF16), pltpu.VMEM((2, T, T), BF16)],
        compiler_params=_cparams(("parallel", "arbitrary")),
    )(p, p, p, carries, dy, tri)


def _exchange(srcs, out_shapes, src_slice, dst_slice, name, deps=()):
    n, nd = len(srcs), len(deps)

    def body(*refs):
        ins, outs = refs[:n], refs[n + nd:2 * n + nd]
        send_sems, recv_sems, loc_sems = refs[2 * n + nd:]
        x, y, c = lax.axis_index("x"), lax.axis_index("y"), lax.axis_index("c")
        me = 4 * x + 2 * y + c
        local = [pltpu.make_async_copy(src_slice(t, ins[t], me), dst_slice(t, outs[t], me), loc_sems.at[t])
                 for t in range(n)]
        for cp in local:
            cp.start()
        sends, recvs = [], []
        for k in (1, 2, 4, 6, 3, 5, 7):
            px = 1 - x if k & 4 else x
            py = 1 - y if k & 2 else y
            pc = 1 - c if k & 1 else c
            peer = 4 * px + 2 * py + pc
            for t in range(n):
                s = t * 7 + k - 1
                sends.append(pltpu.make_async_remote_copy(
                    src_ref=src_slice(t, ins[t], peer), dst_ref=dst_slice(t, outs[t], me),
                    send_sem=send_sems.at[s], recv_sem=recv_sems.at[s],
                    device_id=(px, py, pc), device_id_type=pl.DeviceIdType.MESH))
                recvs.append(pltpu.make_async_remote_copy(
                    src_ref=src_slice(t, ins[t], me), dst_ref=dst_slice(t, outs[t], peer),
                    send_sem=send_sems.at[s], recv_sem=recv_sems.at[s],
                    device_id=(px, py, pc), device_id_type=pl.DeviceIdType.MESH))
        for cp in sends:
            cp.start()
        for cp in recvs:
            cp.wait_recv()
        for cp in sends:
            cp.wait_send()
        for cp in local:
            cp.wait()

    anyspec = pl.BlockSpec(memory_space=pl.ANY)
    return pl.pallas_call(
        body, name=name, in_specs=[anyspec] * (n + nd), out_specs=[anyspec] * n,
        out_shape=[jax.ShapeDtypeStruct(s, d) for s, d in out_shapes],
        scratch_shapes=[pltpu.SemaphoreType.DMA((7 * n,)), pltpu.SemaphoreType.DMA((7 * n,)),
                        pltpu.SemaphoreType.DMA((n,))],
    )(*srcs, *deps)


def _all_gather_lead(xs, name, deps=()):
    return _exchange(
        xs, [((N_DEV,) + x.shape, x.dtype) for x in xs],
        lambda t, ref, peer: ref, lambda t, ref, who: ref.at[who], name, deps)


def _all_to_all_lead(xs, name):
    return _exchange(
        xs, [(x.shape, x.dtype) for x in xs],
        lambda t, ref, peer: ref.at[peer], lambda t, ref, who: ref.at[who], name)


_W_AXIS = {"w_in": 1, "w_ret_out": 0, "w_sb_out": 0, "w_mix_out": 0, "w_up": 1, "w_down": 0}
_W_NAMES = tuple(_W_AXIS)


def _window(ref, axis, who, width, count=1):
    start = pl.multiple_of(who * width, width)
    return ref.at[pl.ds(start, count * width), :] if axis == 0 else ref.at[:, pl.ds(start, count * width)]


_HBM = pl.BlockSpec(memory_space=pltpu.HBM)
_SEM = pl.BlockSpec(memory_space=pltpu.SEMAPHORE)
_EFFECT = pltpu.SideEffectType.DATAFLOW_SIDE_EFFECTING


_ALL_PEERS = (0, 1, 2, 4, 6, 3, 5, 7)
_SAME_CORE = (0, 2, 4, 6)


def _exchange_start(srcs, shapes, src_slice, dst_slice, name, deps=(), ks=_ALL_PEERS):
    n, nd = len(srcs), len(deps)
    lands = [pltpu.with_memory_space_constraint(lax.empty(s, d), pltpu.HBM) for s, d in shapes]

    def body(*refs):
        ins, lnd = refs[:n], refs[n:2 * n]
        sems = refs[2 * n + nd:4 * n + nd]
        token = refs[6 * n + nd]
        x, y, c = lax.axis_index("x"), lax.axis_index("y"), lax.axis_index("c")
        me = 4 * x + 2 * y + c
        for k in ks:
            px = 1 - x if k & 4 else x
            py = 1 - y if k & 2 else y
            pc = 1 - c if k & 1 else c
            peer = 4 * px + 2 * py + pc
            for t in range(n):
                pltpu.make_async_remote_copy(
                    src_ref=src_slice(t, ins[t], peer), dst_ref=dst_slice(t, lnd[t], me),
                    send_sem=sems[2 * t], recv_sem=sems[2 * t + 1],
                    device_id=(px, py, pc), device_id_type=pl.DeviceIdType.MESH).start()
        token[...] = jnp.zeros_like(token)

    res = pl.pallas_call(
        body, name=name, in_specs=[_HBM] * (2 * n) + [pl.BlockSpec(memory_space=pl.ANY)] * nd,
        out_specs=[_SEM] * (2 * n) + [_HBM] * (2 * n) + [pl.BlockSpec(memory_space=pltpu.VMEM)],
        out_shape=[pltpu.SemaphoreType.DMA(())] * (2 * n) + [pltpu.HBM(s.shape, s.dtype) for s in srcs]
        + [pltpu.HBM(s.shape, s.dtype) for s in lands] + [jax.ShapeDtypeStruct((8, 128), F32)],
        input_output_aliases={t: 2 * n + t for t in range(2 * n)},
        compiler_params=pltpu.CompilerParams(has_side_effects=_EFFECT),
    )(*[pltpu.with_memory_space_constraint(s, pltpu.HBM) for s in srcs], *lands, *deps)
    return dict(n=n, sems=res[:2 * n], srcs=res[2 * n:3 * n], lands=res[3 * n:4 * n], token=res[4 * n])


def _exchange_wait(h, after, name):
    n, ns = h['n'], len(h['srcs'])
    span = h.get('span', lambda t, ref: ref)

    def body(*refs):
        lnd = refs[ns:ns + n]
        sems = refs[ns + n:ns + 3 * n]
        x, y, c = lax.axis_index("x"), lax.axis_index("y"), lax.axis_index("c")
        for t in range(n):
            w = span(t, lnd[t])
            cp = pltpu.make_async_remote_copy(src_ref=w, dst_ref=w, send_sem=sems[2 * t], recv_sem=sems[2 * t + 1],
                                              device_id=(x, y, 1 - c), device_id_type=pl.DeviceIdType.MESH)
            cp.wait_send()
            cp.wait_recv()

    after = list(after)
    res = pl.pallas_call(
        body, name=name,
        in_specs=[_HBM] * (ns + n) + [_SEM] * (2 * n) + [pl.BlockSpec(memory_space=pl.ANY)] * len(after),
        out_specs=[_HBM] * (ns + n),
        out_shape=[pltpu.HBM(s.shape, s.dtype) for s in h['srcs']] + [pltpu.HBM(s.shape, s.dtype) for s in h['lands']],
        input_output_aliases={t: t for t in range(ns + n)},
        compiler_params=pltpu.CompilerParams(has_side_effects=_EFFECT),
    )(*h['srcs'], *h['lands'], *h['sems'], *after)
    return list(res[ns:])


def _sibling_start(lands, win, name):
    n = len(lands)

    def body(*refs):
        lnd = refs[:n]
        sems = refs[n:3 * n]
        token = refs[4 * n]
        x, y, c = lax.axis_index("x"), lax.axis_index("y"), lax.axis_index("c")
        for ox in (x, 1 - x):
            for oy in (y, 1 - y):
                owner = 4 * ox + 2 * oy + c
                for t in range(n):
                    w = win(t, lnd[t], owner)
                    pltpu.make_async_remote_copy(
                        src_ref=w, dst_ref=w, send_sem=sems[2 * t], recv_sem=sems[2 * t + 1],
                        device_id=(x, y, 1 - c), device_id_type=pl.DeviceIdType.MESH).start()
        token[...] = jnp.zeros_like(token)

    res = pl.pallas_call(
        body, name=name, in_specs=[_HBM] * n,
        out_specs=[_SEM] * (2 * n) + [_HBM] * n + [pl.BlockSpec(memory_space=pltpu.VMEM)],
        out_shape=[pltpu.SemaphoreType.DMA(())] * (2 * n) + [pltpu.HBM(s.shape, s.dtype) for s in lands]
        + [jax.ShapeDtypeStruct((8, 128), F32)],
        input_output_aliases={t: 2 * n + t for t in range(n)},
        compiler_params=pltpu.CompilerParams(has_side_effects=_EFFECT),
    )(*lands)
    return dict(n=n, sems=res[:2 * n], srcs=[], lands=res[2 * n:3 * n], token=res[3 * n])


def _gather_start(shards, names, tag, deps=(), two_level=False):
    xs = [shards[nm] for nm in names]
    axes = [_W_AXIS[nm] for nm in names]
    widths = [x.shape[ax] for x, ax in zip(xs, axes)]
    shapes = [(tuple(d * (N_DEV if a == ax else 1) for a, d in enumerate(x.shape)), x.dtype) for x, ax in zip(xs, axes)]
    src = lambda t, ref, peer: ref
    dst = lambda t, ref, who: _window(ref, axes[t], who, widths[t])
    h = _exchange_start(xs, shapes, src, dst, "gw_start_" + tag, deps, _SAME_CORE if two_level else _ALL_PEERS)
    h['tag'] = "gw_wait_" + tag
    if two_level:
        h['span'] = lambda t, ref: _window(ref, axes[t], 0, widths[t], len(_SAME_CORE))
        h['second'] = (dst, "gw_pass_" + tag, "gw_passed_" + tag)
    return h


def _scatter_start(grads, names, tag):
    xs = [grads[nm] for nm in names]
    axes = [_W_AXIS[nm] for nm in names]
    widths = [x.shape[ax] // N_DEV for x, ax in zip(xs, axes)]
    shapes = [((N_DEV,) + tuple(d // (N_DEV if a == ax else 1) for a, d in enumerate(x.shape)), x.dtype)
              for x, ax in zip(xs, axes)]
    src = lambda t, ref, peer: _window(ref, axes[t], peer, widths[t])
    dst = lambda t, ref, who: ref.at[who]
    h = _exchange_start(xs, shapes, src, dst, "sg_start_" + tag)
    h['tag'] = "sg_wait_" + tag
    return h


def _finish(h, after):
    return _exchange_wait(h, after, h['tag'])


class _LayerWeights:
    def __init__(self, groups, started):
        self.groups = groups
        self.started = started
        self.got = {}
        self.after = None

    def __getitem__(self, nm):
        if nm not in self.got:
            for names, h in self.groups:
                if nm in names:
                    self.got.update(zip(names, _finish(h, list(self.after) + self.started)))
        return self.got[nm]


def _adam_math(p_ref, w, m, v):
    g = p_ref[0].astype(F32)
    for s in range(1, p_ref.shape[0]):
        g = g + p_ref[s].astype(F32)
    bc1 = 1.0 / (1.0 - ADAM_B1 ** ADAM_STEP)
    bc2 = 1.0 / (1.0 - ADAM_B2 ** ADAM_STEP)
    mm = ADAM_B1 * m + (1.0 - ADAM_B1) * g
    vv = ADAM_B2 * v + (1.0 - ADAM_B2) * jnp.square(g)
    return g, -ADAM_LR * ((mm * bc1) / (jnp.sqrt(vv * bc2) + ADAM_EPS) + ADAM_WD * w), mm, vv


def _adam(parts, w, m, v, name, tr=256):
    P, R, C = parts.shape
    tr = min(tr, R)
    assert R % tr == 0

    def body(p_ref, w_ref, m_ref, v_ref, *outs):
        for o_ref, val in zip(outs, _adam_math(p_ref, w_ref[...], m_ref[...], v_ref[...])):
            o_ref[...] = val

    spec = pl.BlockSpec((tr, C), lambda i: (i, 0))
    return pl.pallas_call(
        body, name=name, grid=(R // tr,),
        in_specs=[pl.BlockSpec((P, tr, C), lambda i: (0, i, 0)), spec, spec, spec],
        out_specs=[spec] * 4, out_shape=[jax.ShapeDtypeStruct((R, C), F32)] * 4,
        compiler_params=_cparams(("parallel",)),
    )(parts, w, m, v)


def _adam_layer(parts, w, m, v, l, prev, name, tr=256):
    P, R, C = parts.shape
    tr = min(tr, R)
    assert R % tr == 0 and w.shape == (DEPTH, R, C)
    npv = 0 if prev is None else 4

    def body(p_ref, w_ref, m_ref, v_ref, *rest):
        for o_ref, val in zip(rest[npv:], _adam_math(p_ref, w_ref[0], m_ref[0], v_ref[0])):
            o_ref[0] = val

    spec = pl.BlockSpec((1, tr, C), lambda i: (l, i, 0))
    return pl.pallas_call(
        body, name=name, grid=(R // tr,),
        in_specs=[pl.BlockSpec((P, tr, C), lambda i: (0, i, 0)), spec, spec, spec]
        + [pl.BlockSpec(memory_space=pl.ANY)] * npv,
        out_specs=[spec] * 4, out_shape=[jax.ShapeDtypeStruct((DEPTH, R, C), F32)] * 4,
        input_output_aliases={4 + t: t for t in range(npv)},
        compiler_params=_cparams(("parallel",)),
    )(parts, w, m, v, *([] if prev is None else prev))


def _mod_partial(cact_all, w_ada_l, b_ada_l):
    def body(c_ref, w_ref, b_ref, o_ref):
        o_ref[...] = _dot(c_ref[...].astype(BF16), w_ref[...].astype(BF16)) + b_ref[...]

    return pl.pallas_call(
        body, name="mod_partial", out_shape=jax.ShapeDtypeStruct((cact_all.shape[0], w_ada_l.shape[1]), F32),
        compiler_params=pltpu.CompilerParams(vmem_limit_bytes=VMEM_LIMIT),
    )(cact_all, w_ada_l, b_ada_l)


def _ada_grad(cact_t, dmod):
    D, n = cact_t.shape[0], dmod.shape[1]

    def body(c_ref, d_ref, o_ref):
        ct = c_ref[...].astype(BF16).astype(F32)
        dm = d_ref[...].astype(BF16).astype(F32)
        acc = ct[:, 0:1] * dm[0:1, :]
        for b in range(1, N_DEV):
            acc = acc + ct[:, b:b + 1] * dm[b:b + 1, :]
        o_ref[0] = acc

    return pl.pallas_call(
        body, name="ada_grad", out_shape=jax.ShapeDtypeStruct((1, D, n), F32),
        compiler_params=pltpu.CompilerParams(vmem_limit_bytes=VMEM_LIMIT),
    )(cact_t, dmod)


def _norm_mod(x, r, gv, sh):
    return x * r * gv + sh


def _silu(x):
    return x * _sigmoid(x)


def _rstd(x):
    return lax.rsqrt(jnp.mean(x * x, axis=-1, keepdims=True) + EPS)


def _residual_epi(acc, x, g):
    xn = x + g * acc
    return acc, xn, _rstd(xn)


def _residual_norm_epi(acc, x, g, gv, sh):
    xn = x + g * acc
    r = _rstd(xn)
    return acc, xn, _norm_mod(xn, r, gv, sh), r


def _layer_fwd(x0, r1, h1, mod, gn1, gn2, nxt, W, rc, tri):
    S = x0.shape[0]
    sh1, sc1, g1m, sh2, sc2, g2m = [mod[i:i + 1] for i in range(N_MOD)]
    gv1 = gn1 * (1.0 + sc1)
    gv2 = gn2 * (1.0 + sc2)
    W.after = [h1]
    (p,) = _mm(h1, W["w_in"], tm=2048, outs=(BF16,), name="mm_in")
    yret, rstate = _ret_fwd(p, rc)
    ysb, sbc = _sb_fwd(p, tri)
    W.after = [ysb]
    (ya,) = _mm(yret, W["w_ret_out"], tm=512, a_ex=[(p, 'a', O_RG)],
                pro=lambda yr, g: _silu(g.astype(F32)) * yr.astype(F32), outs=(BF16,), name="mm_ret_out")
    yb, mg = _mm(ysb, W["w_sb_out"], o_ex=[(ya, 'o'), (p, 'o', O_GA), (p, 'o', O_GB)],
                 epi=lambda acc, a, ga, gb: (acc, _sigmoid(ga.astype(F32)) * a.astype(F32)
                                             + _sigmoid(gb.astype(F32)) * acc),
                 outs=(BF16, BF16), name="mm_sb_out")
    mo, x1, h2, r2 = _mm(mg, W["w_mix_out"], tm=512, o_ex=[(x0, 'o'), (g1m, 'n'), (gv2, 'n'), (sh2, 'n')],
                         epi=_residual_norm_epi, outs=(BF16, F32, BF16), cols=1, name="mm_mix_out")
    (act,) = _mm(h2, W["w_up"], epi=lambda acc: (jnp.maximum(acc, 0.0),), outs=(BF16,), name="mm_up")
    if nxt is None:
        dn, x2, r_out = _mm(act, W["w_down"], tm=512, pro=lambda a: a * a, o_ex=[(x1, 'o'), (g2m, 'n')],
                            epi=_residual_epi, outs=(BF16, F32), cols=1, name="mm_down_last")
        h_out = None
    else:
        dn, x2, h_out, r_out = _mm(act, W["w_down"], tm=512, pro=lambda a: a * a,
                                   o_ex=[(x1, 'o'), (g2m, 'n'), (nxt[0], 'n'), (nxt[1], 'n')],
                                   epi=_residual_norm_epi, outs=(BF16, F32, BF16), cols=1, name="mm_down")
    saved = dict(x0=x0, r1=r1, h1=h1, p=p, yret=yret, rstate=rstate, ysb=ysb, sbc=sbc, ya=ya, yb=yb, mg=mg, mo=mo, x1=x1,
                 r2=r2, h2=h2, act=act, dn=dn, gv1=gv1, gv2=gv2, mod=mod, gn1=gn1, gn2=gn2)
    return x2, r_out, h_out, saved


def _norm_bwd(dh, x, r, dres, gv, gn, extra_rows=(), extra_vecs=(), extra_fn=None, extra_outs=(), name="norm_bwd"):
    D = x.shape[1]
    ne = len(extra_rows)

    def fn(dh_t, x_t, dres_t, *rest):
        er, rest = rest[:ne], rest[ne:]
        gv_t = rest[0]
        ev, r_t = rest[1:-1], rest[-1]
        xh = x_t * r_t
        dxh = dh_t * gv_t
        dx = r_t * (dxh - xh * jnp.mean(dxh * xh, axis=-1, keepdims=True)) + dres_t
        base = (dx, dh_t, dh_t * xh)
        if extra_fn is None:
            return base
        return base + tuple(extra_fn(dx, *er, *ev))

    return _ew(fn, [dh, x, dres] + list(extra_rows), vecs=[gv] + list(extra_vecs), cols=[r],
               outs=[('row', D, F32), ('sum', D), ('sum', D)] + list(extra_outs), name=name)


def _gate_bwd(dx, dn, g):
    return dx * dn.astype(F32), dx * g


_GATE_OUTS = [('sum', D_MODEL), ('row', D_MODEL, BF16)]


def _layer_bwd(dx2, d_g2m, d_dn, sv, below, W, rc, tri, emit):
    mod = sv['mod']
    sh1, sc1, g1m, sh2, sc2, g2m = [mod[i:i + 1] for i in range(N_MOD)]
    D = D_MODEL
    p = sv['p']
    (d_up,) = _mm(d_dn, W["w_down"], tb=True, o_ex=[(sv['act'], 'o')],
                  epi=lambda acc, a: (acc * 2.0 * a.astype(F32),), outs=(BF16,), name="mm_down_dx")
    (gw_down,) = _mm(sv['act'], d_dn, ta=True, tk=DW_TK, pro=lambda a: a * a, outs=(BF16,), name="mm_down_dw")
    (gw_up,) = _mm(sv['h2'], d_up, ta=True, tk=DW_TK, outs=(BF16,), name="mm_up_dw")
    tok = emit(dict(w_down=gw_down, w_up=gw_up), "mlp")
    (d_h2,) = _mm(d_up, W["w_up"], tb=True, tk=2048, outs=(F32,), name="mm_up_dx", deps=[tok])
    dx1, d_sh2, s_h2, d_g1m, d_mo = _norm_bwd(
        d_h2, sv['x1'], sv['r2'], dx2, sv['gv2'], sv['gn2'],
        extra_rows=[sv['mo']], extra_vecs=[g1m],
        extra_fn=lambda dx, mo, g: (dx * mo.astype(F32), dx * g),
        extra_outs=[('sum', D), ('row', D, BF16)], name="norm_bwd_mlp")
    d_sc2 = sv['gn2'] * s_h2
    d_gn2 = (1.0 + sc2) * s_h2
    def mix_epi(acc, ya, yb, ga, gb):
        sa, sb = _sigmoid(ga.astype(F32)), _sigmoid(gb.astype(F32))
        return (acc * sa, acc * sb, acc * ya.astype(F32) * sa * (1.0 - sa), acc * yb.astype(F32) * sb * (1.0 - sb))

    d_ya, d_yb, d_ga, d_gb = _mm(d_mo, W["w_mix_out"], tb=True, tm=512,
                                 o_ex=[(sv['ya'], 'o'), (sv['yb'], 'o'), (p, 'o', O_GA), (p, 'o', O_GB)], epi=mix_epi,
                                 outs=(BF16,) * 4, name="mm_mix_dx")
    (gw_mix,) = _mm(sv['mg'], d_mo, ta=True, tk=DW_TK, outs=(BF16,), name="mm_mix_dw")

    def ro_epi(acc, g, yr):
        gf = g.astype(F32)
        s = _sigmoid(gf)
        return (acc * yr.astype(F32) * s * (1.0 + gf * (1.0 - s)), acc * gf * s)

    d_rg, d_yret = _mm(d_ya, W["w_ret_out"], tb=True, tm=512, o_ex=[(p, 'o', O_RG), (sv['yret'], 'o')], epi=ro_epi,
                       outs=(BF16, BF16), name="mm_ret_dx")
    (gw_ro,) = _mm(sv['yret'], d_ya, ta=True, tm=512, tk=DW_TK, a_ex=[(p, 'a', O_RG)],
                   pro=lambda yr, g: _silu(g.astype(F32)) * yr.astype(F32), outs=(BF16,), name="mm_ret_dw")
    (gw_so,) = _mm(sv['ysb'], d_yb, ta=True, tk=DW_TK, outs=(BF16,), name="mm_sb_dw")
    tok = emit(dict(w_mix_out=gw_mix, w_ret_out=gw_ro, w_sb_out=gw_so), "mix")
    (d_ysb,) = _mm(d_yb, W["w_sb_out"], tb=True, outs=(BF16,), name="mm_sb_dx", deps=[tok])
    d_sq, d_sk, d_sv = _sb_bwd(p, sv['sbc'], d_ysb, tri)
    d_ret = _ret_bwd(p, sv['rstate'], d_yret, rc)
    dp = [d_ret, d_rg, d_sq, d_sk, d_sv, d_ga, d_gb]
    (gw_in,) = _mm(sv['h1'], dp, ta=True, tk=1024, outs=(BF16,), name="mm_in_dw")
    tok = emit(dict(w_in=gw_in), "in")
    (d_h,) = _mm(dp, W["w_in"], tb=True, tm=2048, tk=512, outs=(F32,), name="mm_in_dx", deps=[tok])
    if below is None:
        dx0, d_sh1, s_h1 = _norm_bwd(d_h, sv['x0'], sv['r1'], dx1, sv['gv1'], sv['gn1'], name="norm_bwd_mix")
        gate_below = (None, None)
    else:
        dx0, d_sh1, s_h1, *gate_below = _norm_bwd(
            d_h, sv['x0'], sv['r1'], dx1, sv['gv1'], sv['gn1'], extra_rows=[below['dn']],
            extra_vecs=[below['mod'][N_MOD - 1:N_MOD]], extra_fn=_gate_bwd, extra_outs=_GATE_OUTS,
            name="norm_bwd_mix_gate")
    d_sc1 = sv['gn1'] * s_h1
    d_gn1 = (1.0 + sc1) * s_h1
    d_mod = jnp.concatenate([d_sh1, d_sc1, d_g1m, d_sh2, d_sc2, d_g2m], axis=1)
    return dx0, gate_below, d_mod, d_gn1, d_gn2


def kernel(x, c, norm_mix_g, w_in, w_ret_out, w_sb_out, w_mix_out, norm_mlp_g, w_up, w_down, w_ada, b_ada, final_g, loss_target, m_norm_mix_g, m_w_in, m_w_ret_out, m_w_sb_out, m_w_mix_out, m_norm_mlp_g, m_w_up, m_w_down, m_w_ada, m_b_ada, m_final_g, v_norm_mix_g, v_w_in, v_w_ret_out, v_w_sb_out, v_w_mix_out, v_norm_mlp_g, v_w_up, v_w_down, v_w_ada, v_b_ada, v_final_g):
    S, D = x.shape[1], x.shape[2]
    x0 = x.reshape(S, D)
    tgt = loss_target.reshape(S, D)
    me = 4 * lax.axis_index("x") + 2 * lax.axis_index("y") + lax.axis_index("c")
    wts = dict(w_in=w_in, w_ret_out=w_ret_out, w_sb_out=w_sb_out, w_mix_out=w_mix_out, w_up=w_up, w_down=w_down)
    mts = dict(w_in=m_w_in, w_ret_out=m_w_ret_out, w_sb_out=m_w_sb_out, w_mix_out=m_w_mix_out, w_up=m_w_up, w_down=m_w_down)
    vts = dict(w_in=v_w_in, w_ret_out=v_w_ret_out, w_sb_out=v_w_sb_out, w_mix_out=v_w_mix_out, w_up=v_w_up, w_down=v_w_down)
    rc = _ret_consts(S)
    tri = _tri()

    (cact,) = _ew(lambda t: (_silu(t),), [jnp.pad(c, ((0, 7), (0, 0)))], outs=[('row', D, F32)], name="silu_c")
    (cact_all,) = _all_gather_lead([cact[0:1]], "gather_c")
    cact_all = cact_all.reshape(N_DEV, D)
    cact16 = jnp.pad(cact_all, ((0, 8), (0, 0)))
    n_ada = w_ada.shape[2]
    b_loc = lax.dynamic_slice_in_dim(b_ada, me * n_ada, n_ada, axis=1)
    mods = [_mod_partial(cact16, w_ada[l], b_loc[l:l + 1])[:N_DEV] for l in range(DEPTH)]
    modp = jnp.stack(mods, axis=1)
    (modr,) = _all_to_all_lead([modp], "scatter_mod")
    mod_full = jnp.transpose(modr, (1, 0, 2)).reshape(DEPTH, N_MOD, D)

    def cast(nm, deps=()):
        w = wts[nm]
        (wb,) = _ew(lambda t: (t,), [w.reshape(-1, w.shape[-1])], outs=[('row', w.shape[-1], BF16)],
                    name="cast_bf16", deps=deps)
        return wb.reshape(w.shape)

    rest = tuple(nm for nm in _W_NAMES if nm != "w_in")
    shards = {"w_in": cast("w_in")}
    h_in = _gather_start({"w_in": shards["w_in"][0]}, ("w_in",), "0_in", [modr], two_level=True)
    in_flight = [h_in['token']]
    for nm in rest:
        shards[nm] = cast(nm, in_flight)
    sh = [{nm: shards[nm][l] for nm in _W_NAMES} for l in range(DEPTH)]

    pre = [(norm_mix_g[l:l + 1] * (1.0 + mod_full[l][1:2]), mod_full[l][0:1]) for l in range(DEPTH)]

    def first(t, gv, sh):
        r = _rstd(t)
        return r, _norm_mod(t, r, gv, sh)

    xs = x0
    rs, hs = _ew(first, [x0], vecs=list(pre[0]), outs=[('col', F32), ('row', D, BF16)], name="row_rstd",
                 deps=in_flight)

    win, pass_name, passed_name = h_in['second']
    before_wait = [hs] + [sh[l][nm] for l in range(DEPTH) for nm in _W_NAMES if (l, nm) != (0, "w_in")]
    h_pass = _sibling_start(_exchange_wait(h_in, before_wait, h_in['tag']), win, pass_name)
    h_pass['span'] = h_in['span']
    started = [h_pass['token']]
    layer_groups = []
    for l, groups in enumerate([[(rest, "0_rest")]] + [[(_W_NAMES, "%d_all" % l)] for l in range(1, DEPTH)]):
        layer_groups.append([])
        for names, tag in groups:
            layer_groups[-1].append((names, _gather_start(sh[l], names, tag, started[-1:])))
            started.append(layer_groups[-1][-1][1]['token'])
    (w_in0,) = _exchange_wait(h_pass, started, passed_name)
    layer_w = [_LayerWeights(g, started) for g in layer_groups]
    layer_w[0].got["w_in"] = w_in0
    saved = []
    for l in range(DEPTH):
        xs, rs, hs, sv = _layer_fwd(xs, rs, hs, mod_full[l], norm_mix_g[l:l + 1], norm_mlp_g[l:l + 1],
                                    pre[l + 1] if l + 1 < DEPTH else None, layer_w[l], rc, tri)
        sv['W'] = layer_w[l]
        saved.append(sv)

    fg = final_g.reshape(1, D)

    def head(xt, tg, dn, g, g2m, r):
        xh = xt * r
        e = xh * g - tg
        dy = e * (1.0 / D)
        dxh = dy * g
        dx = r * (dxh - xh * jnp.mean(dxh * xh, axis=-1, keepdims=True))
        return (dx, dy * xh, 0.5 * e * e * (1.0 / D)) + _gate_bwd(dx, dn, g2m)

    top = saved[DEPTH - 1]
    dxs, d_fg, loss_cols, *gate = _ew(head, [xs, tgt, top['dn']], vecs=[fg, top['mod'][N_MOD - 1:N_MOD]], cols=[rs],
                                      outs=[('row', D, F32), ('sum', D), ('sum', D)] + _GATE_OUTS, name="loss_head")

    small = [None] * DEPTH
    pending = []
    for l in reversed(range(DEPTH)):
        sv = saved[l]

        def emit(gw, tag, l=l):
            names = tuple(gw)
            pending.append((l, names, _scatter_start(gw, names, "%d_%s" % (l, tag))))
            return pending[-1][2]['token']

        dxs, gate, d_mod, d_gn1, d_gn2 = _layer_bwd(dxs, gate[0], gate[1], sv, saved[l - 1] if l else None,
                                                    sv['W'], rc, tri, emit)
        small[l] = (d_mod, d_gn1, d_gn2)
    grad_x = dxs.reshape(1, S, D)

    res = {}
    after = [dxs]
    for l, names, h in pending:
        for nm, landed in zip(names, _finish(h, after)):
            res[nm] = _adam_layer(landed, wts[nm], mts[nm], vts[nm], l, res.get(nm), "adam_layer")
        after = [res[names[-1]][0]]

    pack = jnp.concatenate([small[l][0] for l in range(DEPTH)] + [small[l][1] for l in range(DEPTH)]
                           + [small[l][2] for l in range(DEPTH)] + [d_fg, loss_cols], axis=1)
    (packs,) = _all_gather_lead([pack], "gather_small", deps=after)
    packs = packs.reshape(N_DEV, -1)
    o = 0
    dmod_all = []
    for l in range(DEPTH):
        dmod_all.append(packs[:, o:o + N_MOD * D]); o += N_MOD * D
    gn1_parts = packs[:, o:o + DEPTH * D].reshape(N_DEV, DEPTH, D); o += DEPTH * D
    gn2_parts = packs[:, o:o + DEPTH * D].reshape(N_DEV, DEPTH, D); o += DEPTH * D
    fg_parts = packs[:, o:o + D].reshape(N_DEV, 1, D); o += D
    loss_parts = packs[:, o:o + D]
    (loss_sum,) = _ew(lambda t: (t,), [loss_parts], outs=[('sum', D)], name="loss_sum")
    loss = jnp.sum(loss_sum)

    res["norm_mix_g"] = _adam(gn1_parts, norm_mix_g, m_norm_mix_g, v_norm_mix_g, "adam")
    res["norm_mlp_g"] = _adam(gn2_parts, norm_mlp_g, m_norm_mlp_g, v_norm_mlp_g, "adam")
    fgr = _adam(fg_parts, fg, m_final_g.reshape(1, D), v_final_g.reshape(1, D), "adam")
    res["final_g"] = [t.reshape(D) for t in fgr]
    bparts = jnp.stack(dmod_all, axis=1)
    res["b_ada"] = _adam(bparts, b_ada, m_b_ada, v_b_ada, "adam")
    cact_t = cact_all.T
    for l in range(DEPTH):
        dm_loc = lax.dynamic_slice_in_dim(dmod_all[l], me * n_ada, n_ada, axis=1)
        res["w_ada"] = _adam_layer(_ada_grad(cact_t, dm_loc), w_ada, m_w_ada, v_w_ada, l, res.get("w_ada"),
                                   "adam_layer")

    order = ['norm_mix_g', 'w_in', 'w_ret_out', 'w_sb_out', 'w_mix_out', 'norm_mlp_g', 'w_up', 'w_down', 'w_ada', 'b_ada', 'final_g']
    out = [loss, grad_x]
    for i in range(4):
        out += [res[nm][i] for nm in order]
    return tuple(out)
```

```python
import functools
import math

import jax
import jax.numpy as jnp
import numpy as np
from jax import lax
from jax.experimental import pallas as pl
from jax.experimental.pallas import tpu as pltpu

F32 = jnp.float32
BF16 = jnp.bfloat16

N_DEV = 8
D_MODEL = 1024
DEPTH = 2
RET_HEADS = 4
RET_QK = 256
RET_V = 512
RET_CHUNK = 256
ROPE_BASE = 10000.0
SB_HEADS = 16
SB_DIM = 64
D_FF = 4096
N_MOD = 6
EPS = 1e-6
GN_EPS = 1e-5
O_RQ, O_RK, O_RV, O_RG, O_SQ, O_SK, O_SV, O_GA, O_GB = 0, 1024, 2048, 4096, 6144, 7168, 8192, 9216, 10240
IN_W = 11264

ADAM_LR, ADAM_B1, ADAM_B2, ADAM_EPS, ADAM_WD, ADAM_STEP = 0.001, 0.9, 0.999, 1e-08, 0.01, 10

VMEM_LIMIT = 56 * 1024 * 1024
DW_TK = 2048


def _cparams(sem):
    return pltpu.CompilerParams(dimension_semantics=sem, vmem_limit_bytes=VMEM_LIMIT)


def _mm(a, b, *, ta=False, tb=False, tm=1024, tn=1024, tk=None, a_ex=(), pro=None, o_ex=(), epi=None,
        outs=(F32,), cols=0, name, deps=(), a_whole=False):
    a_parts = list(a) if isinstance(a, (list, tuple)) else [a]
    b_parts = list(b) if isinstance(b, (list, tuple)) else [b]
    assert not (ta and len(a_parts) > 1) and not (tb and len(b_parts) > 1)
    if ta:
        K, M = a.shape
    else:
        M, K = a_parts[0].shape[0], sum(t.shape[1] for t in a_parts)
    N = b.shape[0] if tb else sum(t.shape[1] for t in b_parts)
    tm, tn, tk = min(tm, M), min(tn, N), K if tk is None else min(tk, K)
    assert M % tm == 0 and N % tn == 0 and K % tk == 0, (name, M, N, K, tm, tn, tk)
    nk = K // tk
    multi = len(a_parts) > 1 or len(b_parts) > 1
    direct = epi is None and tuple(outs) == (F32,) and cols == 0

    def ranges(parts, t):
        out, o = [], 0
        for arr in parts:
            assert arr.shape[1] % t == 0
            out.append((o, o + arr.shape[1] // t))
            o += arr.shape[1] // t
        return out

    a_rng = ranges(a_parts, tk) if len(a_parts) > 1 else [(0, nk)]
    b_rng = ranges(b_parts, tn) if len(b_parts) > 1 else [(0, N // tn)]
    clip = lambda v, lo, hi: jnp.clip(v - lo, 0, hi - lo - 1)
    mine = lambda v, lo, hi, w: jnp.where((v >= lo) & (v < hi), w, 0)
    in_specs, args = [], []
    assert ta or not a_whole
    for arr, (lo, hi) in zip(a_parts, a_rng):
        if a_whole:
            in_specs.append(pl.BlockSpec((K, tm), lambda i, j, k: (0, i), pipeline_mode=pl.Buffered(1)))
        else:
            in_specs.append(pl.BlockSpec((tk, tm), lambda i, j, k: (k, i)) if ta
                            else pl.BlockSpec((tm, tk), lambda i, j, k, lo=lo, hi=hi: (i, clip(k, lo, hi))))
        args.append(arr)
    for arr, (lo, hi) in zip(b_parts, b_rng):
        in_specs.append(pl.BlockSpec((tn, tk), lambda i, j, k: (j, k)) if tb
                        else pl.BlockSpec((tk, tn), lambda i, j, k, lo=lo, hi=hi: (mine(j, lo, hi, k), clip(j, lo, hi))))
        args.append(arr)
    npa, npb = len(a_parts), len(b_parts)
    for arr, kind, *off in a_ex:
        off = off[0] if off else 0
        if kind == 'a' and ta:
            assert off % tm == 0
            in_specs.append(pl.BlockSpec((tk, tm), lambda i, j, k, o=off // tm: (k, o + i)))
        elif kind == 'a':
            assert off % tk == 0
            in_specs.append(pl.BlockSpec((tm, tk), lambda i, j, k, o=off // tk: (i, o + k)))
        elif kind == 'k':
            in_specs.append(pl.BlockSpec((tk, 1), lambda i, j, k: (k, 0)) if ta
                            else pl.BlockSpec((1, tk), lambda i, j, k: (0, k)))
        else:
            in_specs.append(pl.BlockSpec((1, tm), lambda i, j, k: (0, i)) if ta
                            else pl.BlockSpec((tm, 1), lambda i, j, k: (i, 0)))
        args.append(arr)
    for arr, kind, *off in o_ex:
        off = off[0] if off else 0
        if kind == 'o':
            assert off % tn == 0
            in_specs.append(pl.BlockSpec((tm, tn), lambda i, j, k, o=off // tn: (i, o + j)))
        elif kind == 'n':
            in_specs.append(pl.BlockSpec((1, tn), lambda i, j, k: (0, j)))
        else:
            in_specs.append(pl.BlockSpec((tm, 1), lambda i, j, k: (i, 0)))
        args.append(arr)
    for arr in deps:
        in_specs.append(pl.BlockSpec(memory_space=pl.ANY))
        args.append(arr)
    assert cols == 0 or N == tn
    na, no, nout, nd = len(a_ex), len(o_ex), len(outs) + cols, len(deps)
    dims = (((0 if ta else 1,), (1 if tb else 0,)), ((), ()))

    def body(*refs):
        a_refs, b_refs = refs[:npa], refs[npa:npa + npb]
        n0 = npa + npb
        aex = refs[n0:n0 + na]
        oex = refs[n0 + na:n0 + na + no]
        out_refs = refs[n0 + na + no + nd:n0 + na + no + nd + nout]

        def product(a_ref, b_ref):
            if a_whole:
                at = a_ref[pl.ds(pl.multiple_of(pl.program_id(2) * tk, tk), tk), :]
            else:
                at = a_ref[...]
            if pro is not None:
                at = pro(at, *[r[...] for r in aex])
            return lax.dot_general(at.astype(BF16), b_ref[...].astype(BF16), dims, preferred_element_type=F32)

        def finish(res):
            vals = epi(res, *[r[...] for r in oex]) if epi is not None else (res,)
            for o_ref, v in zip(out_refs, vals):
                o_ref[...] = v.astype(o_ref.dtype)

        if nk == 1 and not multi:
            finish(product(a_refs[0], b_refs[0]))
            return
        acc = out_refs[0] if direct else refs[-1]
        j, k = pl.program_id(1), pl.program_id(2)
        if multi:
            @pl.when(k == 0)
            def _():
                acc[...] = jnp.zeros_like(acc)

            for a_ref, (alo, ahi) in zip(a_refs, a_rng):
                for b_ref, (blo, bhi) in zip(b_refs, b_rng):
                    @pl.when((k >= alo) & (k < ahi) & (j >= blo) & (j < bhi))
                    def _():
                        acc[...] += product(a_ref, b_ref)
        else:
            @pl.when(k == 0)
            def _():
                acc[...] = product(a_refs[0], b_refs[0])

            @pl.when(k > 0)
            def _():
                acc[...] += product(a_refs[0], b_refs[0])

        if not direct:
            @pl.when(k == nk - 1)
            def _():
                finish(acc[...])

    res = pl.pallas_call(
        body, name=name, grid=(M // tm, N // tn, nk), in_specs=in_specs,
        out_specs=[pl.BlockSpec((tm, tn), lambda i, j, k: (i, j)) for _ in outs]
        + [pl.BlockSpec((tm, 1), lambda i, j, k: (i, 0))] * cols,
        out_shape=[jax.ShapeDtypeStruct((M, N), dt) for dt in outs] + [jax.ShapeDtypeStruct((M, 1), F32)] * cols,
        scratch_shapes=[pltpu.VMEM((tm, tn), F32)] if (nk > 1 or multi) and not direct else [],
        compiler_params=_cparams(("parallel", "parallel", "arbitrary")),
    )(*args)
    return res


def _ew(fn, rows, vecs=(), cols=(), outs=(), tr=256, name=None, deps=()):
    S = rows[0].shape[0]
    tr = min(tr, S)
    assert S % tr == 0
    in_specs, args = [], []
    for r in rows:
        in_specs.append(pl.BlockSpec((tr, r.shape[1]), lambda i: (i, 0)))
        args.append(r)
    for v in vecs:
        in_specs.append(pl.BlockSpec((1, v.shape[1]), lambda i: (0, 0)))
        args.append(v)
    for c in cols:
        in_specs.append(pl.BlockSpec((tr, 1), lambda i: (i, 0)))
        args.append(c)
    out_specs, out_shape = [], []
    for o in outs:
        if o[0] == 'row':
            out_specs.append(pl.BlockSpec((tr, o[1]), lambda i: (i, 0)))
            out_shape.append(jax.ShapeDtypeStruct((S, o[1]), o[2]))
        elif o[0] == 'sum':
            out_specs.append(pl.BlockSpec((1, o[1]), lambda i: (0, 0)))
            out_shape.append(jax.ShapeDtypeStruct((1, o[1]), F32))
        else:
            out_specs.append(pl.BlockSpec((tr, 1), lambda i: (i, 0)))
            out_shape.append(jax.ShapeDtypeStruct((S, 1), o[1]))
    nin = len(args)
    in_specs += [pl.BlockSpec(memory_space=pl.ANY)] * len(deps)
    args += list(deps)

    def body(*refs):
        i = pl.program_id(0)
        vals = fn(*[r[...] for r in refs[:nin]])
        for o, o_ref, v in zip(outs, refs[nin + len(deps):], vals):
            if o[0] == 'sum':
                @pl.when(i == 0)
                def _():
                    o_ref[...] = jnp.zeros_like(o_ref)
                o_ref[...] += jnp.sum(v.astype(F32), axis=0, keepdims=True)
            else:
                o_ref[...] = v.astype(o_ref.dtype)

    return pl.pallas_call(
        body, name=name, grid=(S // tr,), in_specs=in_specs, out_specs=out_specs, out_shape=out_shape,
        compiler_params=_cparams(("arbitrary",)),
    )(*args)


def _sigmoid(x):
    return 1.0 / (1.0 + jnp.exp(-x))


def _ret_consts(S):
    h = np.arange(RET_HEADS, dtype=np.float64)
    log_gamma = np.log1p(-np.power(2.0, -5.0 - h))
    idx = np.arange(RET_CHUNK, dtype=np.float64)
    rel = idx[:, None] - idx[None, :]
    decay = np.where(rel >= 0, np.exp(np.maximum(rel, 0.0) * log_gamma[:, None, None]), 0.0)
    xi = np.exp((idx + 1.0) * log_gamma[:, None])[:, :, None]
    zeta = np.exp((RET_CHUNK - 1.0 - idx) * log_gamma[:, None])[:, :, None]
    gamma_c = np.exp(RET_CHUNK * log_gamma)[:, None, None]
    half = RET_QK // 2
    inv_freq = np.power(ROPE_BASE, -np.arange(half, dtype=np.float64) / half).astype(np.float32)
    ang = np.arange(S, dtype=np.float32)[:, None] * inv_freq[None, :]
    f = lambda t: jnp.asarray(t, F32)
    return dict(decay=f(decay), xi=f(xi), zeta=f(zeta), gc=f(gamma_c), cos=f(np.cos(ang)), sin=f(np.sin(ang)))


def _rot(t, cos, sin):
    half = RET_QK // 2
    t1, t2 = t[:, :half], t[:, half:]
    return jnp.concatenate([t1 * cos - t2 * sin, t1 * sin + t2 * cos], axis=-1)


def _rot_inv(t, cos, sin):
    half = RET_QK // 2
    t1, t2 = t[:, :half], t[:, half:]
    return jnp.concatenate([t1 * cos + t2 * sin, t2 * cos - t1 * sin], axis=-1)


_NT = (((1,), (1,)), ((), ()))
_TN = (((0,), (0,)), ((), ()))


def _dot(a, b):
    return jnp.dot(a, b, preferred_element_type=F32)


def _dot_nt(a, b):
    return lax.dot_general(a, b, _NT, preferred_element_type=F32)


def _dot_tn(a, b):
    return lax.dot_general(a, b, _TN, preferred_element_type=F32)


_QW, _VW = RET_HEADS * RET_QK, RET_HEADS * RET_V
_HEADS = range(RET_HEADS)


def _ret_in_specs(C, rev, NC):
    n_of = (lambda n: NC - 1 - n) if rev else (lambda n: n)
    whole3 = lambda n: (0, 0, 0)
    return [
        pl.BlockSpec((C, _QW), lambda n: (n_of(n), O_RQ // _QW)),
        pl.BlockSpec((C, _QW), lambda n: (n_of(n), O_RK // _QW)),
        pl.BlockSpec((C, _VW), lambda n: (n_of(n), O_RV // _VW)),
        pl.BlockSpec((C, RET_QK // 2), lambda n: (n_of(n), 0)),
        pl.BlockSpec((C, RET_QK // 2), lambda n: (n_of(n), 0)),
        pl.BlockSpec((RET_HEADS, C, C), whole3),
        pl.BlockSpec((RET_HEADS, C, 1), whole3),
        pl.BlockSpec((RET_HEADS, C, 1), whole3),
        pl.BlockSpec((RET_HEADS, 1, 1), whole3),
    ]


def _qk_heads(q_ref, k_ref, cos, sin):
    qs, kfs = [], []
    for h in _HEADS:
        cols = slice(h * RET_QK, (h + 1) * RET_QK)
        qs.append(_rot(q_ref[:, cols].astype(F32), cos, sin).astype(BF16))
        kfs.append(_rot(k_ref[:, cols].astype(F32), cos, sin) * (RET_QK ** -0.5))
    return qs, kfs


def _ret_fwd(p, rc):
    S = p.shape[0]
    C = RET_CHUNK
    NC = S // C

    def body(q_ref, k_ref, v_ref, cos_ref, sin_ref, dec_ref, xi_ref, zeta_ref, gc_ref, y_ref, rs_ref, r_acc):
        n = pl.program_id(0)

        @pl.when(n == 0)
        def _():
            r_acc[...] = jnp.zeros_like(r_acc)

        cos, sin = cos_ref[...], sin_ref[...]
        qs, kfs = _qk_heads(q_ref, k_ref, cos, sin)
        vs = [v_ref[:, h * RET_V:(h + 1) * RET_V] for h in _HEADS]
        rbs = [r_acc[h].astype(BF16) for h in _HEADS]
        for h in _HEADS:
            rs_ref[h, 0] = rbs[h]
        ss = [(_dot_nt(qs[h], kfs[h].astype(BF16)) * dec_ref[h]).astype(BF16) for h in _HEADS]
        os = [_dot(ss[h], vs[h]) + _dot(qs[h], rbs[h]) * xi_ref[h] for h in _HEADS]
        for h in _HEADS:
            o = os[h]
            mu = jnp.mean(o, axis=-1, keepdims=True)
            var = jnp.mean(jnp.square(o - mu), axis=-1, keepdims=True)
            y_ref[:, h * RET_V:(h + 1) * RET_V] = ((o - mu) * lax.rsqrt(var + GN_EPS)).astype(y_ref.dtype)
        for h in _HEADS:
            kz = (kfs[h] * zeta_ref[h]).astype(BF16)
            r_acc[h] = r_acc[h] * gc_ref[h] + _dot_tn(kz, vs[h])

    return pl.pallas_call(
        body, name="ret_fwd", grid=(NC,), in_specs=_ret_in_specs(C, False, NC),
        out_specs=[pl.BlockSpec((C, _VW), lambda n: (n, 0)),
                   pl.BlockSpec((RET_HEADS, 1, RET_QK, RET_V), lambda n: (0, n, 0, 0))],
        out_shape=[jax.ShapeDtypeStruct((S, _VW), BF16),
                   jax.ShapeDtypeStruct((RET_HEADS, NC, RET_QK, RET_V), BF16)],
        scratch_shapes=[pltpu.VMEM((RET_HEADS, RET_QK, RET_V), F32)],
        compiler_params=_cparams(("arbitrary",)),
    )(p, p, p, rc['cos'], rc['sin'], rc['decay'], rc['xi'], rc['zeta'], rc['gc'])


def _ret_bwd(p, rstate, dy, rc):
    S = p.shape[0]
    C = RET_CHUNK
    NC = S // C

    def body(q_ref, k_ref, v_ref, cos_ref, sin_ref, dec_ref, xi_ref, zeta_ref, gc_ref, rs_ref, dy_ref,
             d_ref, dr_acc):
        dq_ref, dk_ref, dv_ref = d_ref.at[:, 0:_QW], d_ref.at[:, _QW:2 * _QW], d_ref.at[:, 2 * _QW:2 * _QW + _VW]
        t = pl.program_id(0)

        @pl.when(t == 0)
        def _():
            dr_acc[...] = jnp.zeros_like(dr_acc)

        cos, sin = cos_ref[...], sin_ref[...]
        qs, kfs = _qk_heads(q_ref, k_ref, cos, sin)
        ks = [kf.astype(BF16) for kf in kfs]
        vs = [v_ref[:, h * RET_V:(h + 1) * RET_V] for h in _HEADS]
        rbs = [rs_ref[h, 0] for h in _HEADS]
        ss = [(_dot_nt(qs[h], ks[h]) * dec_ref[h]).astype(BF16) for h in _HEADS]
        os = [_dot(ss[h], vs[h]) + _dot(qs[h], rbs[h]) * xi_ref[h] for h in _HEADS]
        dobs, doxis = [], []
        for h in _HEADS:
            o = os[h]
            mu = jnp.mean(o, axis=-1, keepdims=True)
            var = jnp.mean(jnp.square(o - mu), axis=-1, keepdims=True)
            rstd = lax.rsqrt(var + GN_EPS)
            yh = (o - mu) * rstd
            dyf = dy_ref[:, h * RET_V:(h + 1) * RET_V].astype(F32)
            do = (dyf - jnp.mean(dyf, axis=-1, keepdims=True)
                  - yh * jnp.mean(dyf * yh, axis=-1, keepdims=True)) * rstd
            dobs.append(do.astype(BF16))
            doxis.append((do * xi_ref[h]).astype(BF16))
        drbs = [dr_acc[h].astype(BF16) for h in _HEADS]
        dss = [(_dot_nt(dobs[h], vs[h]) * dec_ref[h]).astype(BF16) for h in _HEADS]
        for h in _HEADS:
            dq = _dot(dss[h], ks[h]) + _dot_nt(doxis[h], rbs[h])
            dq_ref[:, h * RET_QK:(h + 1) * RET_QK] = _rot_inv(dq, cos, sin).astype(dq_ref.dtype)
        for h in _HEADS:
            dk = _dot_tn(dss[h], qs[h]) + _dot_nt(vs[h], drbs[h]) * zeta_ref[h]
            dk_ref[:, h * RET_QK:(h + 1) * RET_QK] = (_rot_inv(dk, cos, sin) * (RET_QK ** -0.5)).astype(dk_ref.dtype)
        for h in _HEADS:
            kz = (kfs[h] * zeta_ref[h]).astype(BF16)
            dv = _dot_tn(ss[h], dobs[h]) + _dot(kz, drbs[h])
            dv_ref[:, h * RET_V:(h + 1) * RET_V] = dv.astype(dv_ref.dtype)
        for h in _HEADS:
            dr_acc[h] = dr_acc[h] * gc_ref[h] + _dot_tn(qs[h], doxis[h])

    rn = lambda n: NC - 1 - n
    in_specs = _ret_in_specs(C, True, NC) + [
        pl.BlockSpec((RET_HEADS, 1, RET_QK, RET_V), lambda n: (0, rn(n), 0, 0)),
        pl.BlockSpec((C, _VW), lambda n: (rn(n), 0)),
    ]
    return pl.pallas_call(
        body, name="ret_bwd", grid=(NC,), in_specs=in_specs,
        out_specs=pl.BlockSpec((C, 2 * _QW + _VW), lambda n: (rn(n), 0)),
        out_shape=jax.ShapeDtypeStruct((S, 2 * _QW + _VW), BF16),
        scratch_shapes=[pltpu.VMEM((RET_HEADS, RET_QK, RET_V), F32)],
        compiler_params=_cparams(("arbitrary",)),
    )(p, p, p, rc['cos'], rc['sin'], rc['decay'], rc['xi'], rc['zeta'], rc['gc'], rstate, dy)


SB_T = 256
SB_SCALE = SB_DIM ** -0.5


def _tri():
    j = np.arange(SB_T)
    after = (j[:, None] > j[None, :]).astype(np.float32)
    upto = (j[:, None] <= j[None, :]).astype(np.float32)
    return jnp.asarray(np.stack([after, upto]), BF16)


def _softplus_parts(z):
    neg_abs = lax.bitcast_convert_type(lax.bitcast_convert_type(z, jnp.uint32) | jnp.uint32(0x80000000), F32)
    e = jnp.exp(neg_abs)
    return jnp.maximum(z, 0.0) + jnp.log(1.0 + e), e


def _sb_fwd(p, tri):
    S = p.shape[0]
    T = min(SB_T, S)
    NQ = S // T
    assert NQ <= 128
    qb, kb, vb = O_SQ // 128, O_SK // 128, O_SV // 128

    def body(q_ref, k_ref, v_ref, tri_ref, o_ref, cs_ref, o_acc, run, zbuf, abuf):
        i = pl.program_id(1)
        lane = lax.broadcasted_iota(jnp.int32, (1, 128), 1)
        tri_after = tri_ref[0]
        qs = [jnp.where((lane >= 64) if hh else (lane < 64), q_ref[...], jnp.zeros_like(q_ref[...]))
              * jnp.asarray(SB_SCALE, BF16) for hh in range(2)]
        cs_ref[...] = jnp.zeros_like(cs_ref)
        o_acc[...] = jnp.zeros_like(o_acc)
        run[...] = jnp.zeros_like(run)

        def kv(ref, j):
            return ref[pl.ds(pl.multiple_of(j * T, T), T), :]

        for hh in range(2):
            zbuf[hh] = _dot_nt(qs[hh], kv(k_ref, i))

        def block(t, diagonal):
            j = i - t
            if diagonal:
                msk = lax.broadcasted_iota(jnp.int32, (T, T), 1) < lax.broadcasted_iota(jnp.int32, (T, T), 0)
            if not diagonal:
                av = [_dot(abuf[hh], kv(v_ref, j + 1)) for hh in range(2)]
            lss, exs, tot, zn = [], [], [], []
            for hh in range(2):
                z = zbuf[hh]
                sp, _ = _softplus_parts(z)
                lss.append(z - sp)
                if diagonal:
                    sp = jnp.where(msk, sp, 0.0)
                exs.append(_dot(sp.astype(BF16), tri_after))
                tot.append(sp[:, 0:1])
                zn.append(_dot_nt(qs[hh], kv(k_ref, jnp.maximum(j - 1, 0))))
            for hh in range(2):
                csl = slice(hh * 128, (hh + 1) * 128)
                cs = run[hh]
                a = jnp.exp(lss[hh] - exs[hh] - cs)
                if diagonal:
                    a = jnp.where(msk, a, 0.0)
                abuf[hh] = a.astype(BF16)
                cs_ref[:, csl] = jnp.where(lane == j, cs, cs_ref[:, csl])
                run[hh] = cs + exs[hh][:, 0:1] + tot[hh]
            for hh in range(2):
                if not diagonal:
                    o_acc[hh] += av[hh]
                zbuf[hh] = zn[hh]

        block(0, True)

        def step(t, carry):
            block(t, False)
            return carry

        lax.fori_loop(1, i + 1, step, 0)
        o_ref[...] = jnp.where(lane < 64, o_acc[0] + _dot(abuf[0], kv(v_ref, 0)),
                               o_acc[1] + _dot(abuf[1], kv(v_ref, 0))).astype(o_ref.dtype)

    return pl.pallas_call(
        body, name="sb_fwd", grid=(SB_HEADS // 2, NQ),
        scratch_shapes=[pltpu.VMEM((2, T, 128), F32), pltpu.VMEM((2, T, 1), F32), pltpu.VMEM((2, T, T), F32),
                        pltpu.VMEM((2, T, T), BF16)],
        in_specs=[pl.BlockSpec((T, 128), lambda h, i: (i, qb + h)),
                  pl.BlockSpec((S, 128), lambda h, i: (0, kb + h)),
                  pl.BlockSpec((S, 128), lambda h, i: (0, vb + h)),
                  pl.BlockSpec((1, T, T), lambda h, i: (0, 0, 0))],
        out_specs=[pl.BlockSpec((T, 128), lambda h, i: (i, h)),
                   pl.BlockSpec((T, 256), lambda h, i: (i, h))],
        out_shape=[jax.ShapeDtypeStruct((S, SB_HEADS * SB_DIM), BF16),
                   jax.ShapeDtypeStruct((S, SB_HEADS * 128), F32)],
        compiler_params=_cparams(("parallel", "arbitrary")),
    )(p, p, p, tri)


def _sb_bwd(p, carries, dy, tri):
    S = p.shape[0]
    T = min(SB_T, S)
    NQ = S // T
    qb, kb, vb = O_SQ // 128, O_SK // 128, O_SV // 128

    def body(q_ref, k_ref, v_ref, cs_ref, dy_ref, tri_ref, dq_ref, dk_ref, dv_ref, dk_acc, dv_acc, dq_acc, run,
             zbuf, dabuf, dzbuf, abuf):
        i = pl.program_id(1)

        @pl.when(i == 0)
        def _():
            dk_acc[...] = jnp.zeros_like(dk_acc)
            dv_acc[...] = jnp.zeros_like(dv_acc)

        lane = lax.broadcasted_iota(jnp.int32, (1, 128), 1)
        tri_after, tri_upto = tri_ref[0], tri_ref[1]
        hms = [(lane >= 64) if hh else (lane < 64) for hh in range(2)]
        qs = [jnp.where(hm, q_ref[...], jnp.zeros_like(q_ref[...])) * jnp.asarray(SB_SCALE, BF16) for hm in hms]
        dos = [jnp.where(hm, dy_ref[...], jnp.zeros_like(dy_ref[...])) for hm in hms]
        qst = [t.T for t in qs]
        dost = [t.T for t in dos]
        dq_acc[...] = jnp.zeros_like(dq_acc)
        run[...] = jnp.zeros_like(run)
        dzbuf[...] = jnp.zeros_like(dzbuf)
        abuf[...] = jnp.zeros_like(abuf)

        def kv(ref, j):
            return ref[pl.ds(pl.multiple_of(j * T, T), T), :]

        def flush(jp):
            kp = kv(k_ref, jp)
            dq_add = [_dot(dzbuf[hh], kp) for hh in range(2)]
            dk_add = _dot(qst[0], dzbuf[0]) + _dot(qst[1], dzbuf[1])
            dv_add = _dot(dost[0], abuf[0]) + _dot(dost[1], abuf[1])
            return dq_add, dk_add, dv_add

        def apply(jp, adds):
            dq_add, dk_add, dv_add = adds
            cols = pl.ds(pl.multiple_of(jp * T, T), T)
            for hh in range(2):
                dq_acc[hh] += dq_add[hh]
            dk_acc[:, cols] += dk_add
            dv_acc[:, cols] += dv_add

        for hh in range(2):
            zbuf[hh] = _dot_nt(qs[hh], kv(k_ref, 0))
            dabuf[hh] = _dot_nt(dos[hh], kv(v_ref, 0))

        def block(j, diagonal):
            jp = jnp.maximum(j - 1, 0)
            if diagonal:
                msk = lax.broadcasted_iota(jnp.int32, (T, T), 1) < lax.broadcasted_iota(jnp.int32, (T, T), 0)
            kp = kv(k_ref, jp)
            dq_add = [_dot(dzbuf[hh], kp) for hh in range(2)]
            sigs, lss, exs, zn, dan, dk_part, dv_part = [], [], [], [], [], [], []
            for hh in range(2):
                z = zbuf[hh]
                sp, _ = _softplus_parts(z)
                lss.append(z - sp)
                sigs.append(jnp.exp(lss[hh]))
                if diagonal:
                    sp = jnp.where(msk, sp, 0.0)
                exs.append(_dot(sp.astype(BF16), tri_after))
                if not diagonal:
                    zn.append(_dot_nt(qs[hh], kv(k_ref, j + 1)))
                dk_part.append(_dot(qst[hh], dzbuf[hh]))
            pgs, gs = [], []
            for hh in range(2):
                csl = slice(hh * 128, (hh + 1) * 128)
                cs = jnp.sum(jnp.where(lane == j, cs_ref[:, csl], 0.0), axis=-1, keepdims=True)
                a = jnp.exp(lss[hh] - exs[hh] - cs)
                if diagonal:
                    a = jnp.where(msk, a, 0.0)
                abuf_new = a.astype(BF16)
                g = a * dabuf[hh]
                gs.append((g, abuf_new))
                pgs.append(_dot(g.astype(BF16), tri_upto))
                if not diagonal:
                    dan.append(_dot_nt(dos[hh], kv(v_ref, j + 1)))
                dv_part.append(_dot(dost[hh], abuf[hh]))
            adds = (dq_add, dk_part[0] + dk_part[1], dv_part[0] + dv_part[1])
            for hh in range(2):
                g, abuf_new = gs[hh]
                cg = run[hh]
                dz = g - sigs[hh] * (cg + pgs[hh])
                if diagonal:
                    dz = jnp.where(msk, dz, 0.0)
                run[hh] = cg + pgs[hh][:, T - 1:T]
                dzbuf[hh] = dz.astype(BF16)
                abuf[hh] = abuf_new
            apply(jp, adds)
            if not diagonal:
                for hh in range(2):
                    zbuf[hh] = zn[hh]
                    dabuf[hh] = dan[hh]

        def step(j, carry):
            block(j, False)
            return carry

        lax.fori_loop(0, i, step, 0)
        block(i, True)
        apply(i, flush(i))
        dq_ref[...] = (jnp.where(lane < 64, dq_acc[0], dq_acc[1]) * SB_SCALE).astype(dq_ref.dtype)

        @pl.when(i == NQ - 1)
        def _():
            dk_ref[...] = dk_acc[...].T.astype(dk_ref.dtype)
            dv_ref[...] = dv_acc[...].T.astype(dv_ref.dtype)

    W = SB_HEADS * SB_DIM
    return pl.pallas_call(
        body, name="sb_bwd", grid=(SB_HEADS // 2, NQ),
        in_specs=[pl.BlockSpec((T, 128), lambda h, i: (i, qb + h)),
                  pl.BlockSpec((S, 128), lambda h, i: (0, kb + h)),
                  pl.BlockSpec((S, 128), lambda h, i: (0, vb + h)),
                  pl.BlockSpec((T, 256), lambda h, i: (i, h)),
                  pl.BlockSpec((T, 128), lambda h, i: (i, h)),
                  pl.BlockSpec((2, T, T), lambda h, i: (0, 0, 0))],
        out_specs=[pl.BlockSpec((T, 128), lambda h, i: (i, h)),
                   pl.BlockSpec((S, 128), lambda h, i: (0, h)),
                   pl.BlockSpec((S, 128), lambda h, i: (0, h))],
        out_shape=[jax.ShapeDtypeStruct((S, W), BF16)] * 3,
        scratch_shapes=[pltpu.VMEM((128, S), F32), pltpu.VMEM((128, S), F32), pltpu.VMEM((2, T, 128), F32),
                        pltpu.VMEM((2, T, 1), F32), pltpu.VMEM((2, T, T), F32), pltpu.VMEM((2, T, T), F32),
                        pltpu.VMEM((2, T, T), BF16), pltpu.VMEM((2, T, T), BF16)],
        compiler_params=_cparams(("parallel", "arbitrary")),
    )(p, p, p, carries, dy, tri)


def _exchange(srcs, out_shapes, src_slice, dst_slice, name, deps=()):
    n, nd = len(srcs), len(deps)

    def body(*refs):
        ins, outs = refs[:n], refs[n + nd:2 * n + nd]
        send_sems, recv_sems, loc_sems = refs[2 * n + nd:]
        x, y, c = lax.axis_index("x"), lax.axis_index("y"), lax.axis_index("c")
        me = 4 * x + 2 * y + c
        local = [pltpu.make_async_copy(src_slice(t, ins[t], me), dst_slice(t, outs[t], me), loc_sems.at[t])
                 for t in range(n)]
        for cp in local:
            cp.start()
        sends, recvs = [], []
        for k in (1, 2, 4, 6, 3, 5, 7):
            px = 1 - x if k & 4 else x
            py = 1 - y if k & 2 else y
            pc = 1 - c if k & 1 else c
            peer = 4 * px + 2 * py + pc
            for t in range(n):
                s = t * 7 + k - 1
                sends.append(pltpu.make_async_remote_copy(
                    src_ref=src_slice(t, ins[t], peer), dst_ref=dst_slice(t, outs[t], me),
                    send_sem=send_sems.at[s], recv_sem=recv_sems.at[s],
                    device_id=(px, py, pc), device_id_type=pl.DeviceIdType.MESH))
                recvs.append(pltpu.make_async_remote_copy(
                    src_ref=src_slice(t, ins[t], me), dst_ref=dst_slice(t, outs[t], peer),
                    send_sem=send_sems.at[s], recv_sem=recv_sems.at[s],
                    device_id=(px, py, pc), device_id_type=pl.DeviceIdType.MESH))
        for cp in sends:
            cp.start()
        for cp in recvs:
            cp.wait_recv()
        for cp in sends:
            cp.wait_send()
        for cp in local:
            cp.wait()

    anyspec = pl.BlockSpec(memory_space=pl.ANY)
    return pl.pallas_call(
        body, name=name, in_specs=[anyspec] * (n + nd), out_specs=[anyspec] * n,
        out_shape=[jax.ShapeDtypeStruct(s, d) for s, d in out_shapes],
        scratch_shapes=[pltpu.SemaphoreType.DMA((7 * n,)), pltpu.SemaphoreType.DMA((7 * n,)),
                        pltpu.SemaphoreType.DMA((n,))],
    )(*srcs, *deps)


def _all_gather_lead(xs, name, deps=()):
    return _exchange(
        xs, [((N_DEV,) + x.shape, x.dtype) for x in xs],
        lambda t, ref, peer: ref, lambda t, ref, who: ref.at[who], name, deps)


def _all_to_all_lead(xs, name):
    return _exchange(
        xs, [(x.shape, x.dtype) for x in xs],
        lambda t, ref, peer: ref.at[peer], lambda t, ref, who: ref.at[who], name)


_W_AXIS = {"w_in": 1, "w_ret_out": 0, "w_sb_out": 0, "w_mix_out": 0, "w_up": 1, "w_down": 0}
_W_NAMES = tuple(_W_AXIS)


def _window(ref, axis, who, width, count=1):
    start = pl.multiple_of(who * width, width)
    return ref.at[pl.ds(start, count * width), :] if axis == 0 else ref.at[:, pl.ds(start, count * width)]


_HBM = pl.BlockSpec(memory_space=pltpu.HBM)
_SEM = pl.BlockSpec(memory_space=pltpu.SEMAPHORE)
_EFFECT = pltpu.SideEffectType.DATAFLOW_SIDE_EFFECTING


_ALL_PEERS = (0, 1, 2, 4, 6, 3, 5, 7)
_SAME_CORE = (0, 2, 4, 6)


def _exchange_start(srcs, shapes, src_slice, dst_slice, name, deps=(), ks=_ALL_PEERS):
    n, nd = len(srcs), len(deps)
    lands = [pltpu.with_memory_space_constraint(lax.empty(s, d), pltpu.HBM) for s, d in shapes]

    def body(*refs):
        ins, lnd = refs[:n], refs[n:2 * n]
        sems = refs[2 * n + nd:4 * n + nd]
        token = refs[6 * n + nd]
        x, y, c = lax.axis_index("x"), lax.axis_index("y"), lax.axis_index("c")
        me = 4 * x + 2 * y + c
        for k in ks:
            px = 1 - x if k & 4 else x
            py = 1 - y if k & 2 else y
            pc = 1 - c if k & 1 else c
            peer = 4 * px + 2 * py + pc
            for t in range(n):
                pltpu.make_async_remote_copy(
                    src_ref=src_slice(t, ins[t], peer), dst_ref=dst_slice(t, lnd[t], me),
                    send_sem=sems[2 * t], recv_sem=sems[2 * t + 1],
                    device_id=(px, py, pc), device_id_type=pl.DeviceIdType.MESH).start()
        token[...] = jnp.zeros_like(token)

    res = pl.pallas_call(
        body, name=name, in_specs=[_HBM] * (2 * n) + [pl.BlockSpec(memory_space=pl.ANY)] * nd,
        out_specs=[_SEM] * (2 * n) + [_HBM] * (2 * n) + [pl.BlockSpec(memory_space=pltpu.VMEM)],
        out_shape=[pltpu.SemaphoreType.DMA(())] * (2 * n) + [pltpu.HBM(s.shape, s.dtype) for s in srcs]
        + [pltpu.HBM(s.shape, s.dtype) for s in lands] + [jax.ShapeDtypeStruct((8, 128), F32)],
        input_output_aliases={t: 2 * n + t for t in range(2 * n)},
        compiler_params=pltpu.CompilerParams(has_side_effects=_EFFECT),
    )(*[pltpu.with_memory_space_constraint(s, pltpu.HBM) for s in srcs], *lands, *deps)
    return dict(n=n, sems=res[:2 * n], srcs=res[2 * n:3 * n], lands=res[3 * n:4 * n], token=res[4 * n])


def _exchange_wait(h, after, name):
    n, ns = h['n'], len(h['srcs'])
    span = h.get('span', lambda t, ref: ref)

    def body(*refs):
        lnd = refs[ns:ns + n]
        sems = refs[ns + n:ns + 3 * n]
        x, y, c = lax.axis_index("x"), lax.axis_index("y"), lax.axis_index("c")
        for t in range(n):
            w = span(t, lnd[t])
            cp = pltpu.make_async_remote_copy(src_ref=w, dst_ref=w, send_sem=sems[2 * t], recv_sem=sems[2 * t + 1],
                                              device_id=(x, y, 1 - c), device_id_type=pl.DeviceIdType.MESH)
            cp.wait_send()
            cp.wait_recv()

    after = list(after)
    res = pl.pallas_call(
        body, name=name,
        in_specs=[_HBM] * (ns + n) + [_SEM] * (2 * n) + [pl.BlockSpec(memory_space=pl.ANY)] * len(after),
        out_specs=[_HBM] * (ns + n),
        out_shape=[pltpu.HBM(s.shape, s.dtype) for s in h['srcs']] + [pltpu.HBM(s.shape, s.dtype) for s in h['lands']],
        input_output_aliases={t: t for t in range(ns + n)},
        compiler_params=pltpu.CompilerParams(has_side_effects=_EFFECT),
    )(*h['srcs'], *h['lands'], *h['sems'], *after)
    return list(res[ns:])


def _sibling_start(lands, win, name):
    n = len(lands)

    def body(*refs):
        lnd = refs[:n]
        sems = refs[n:3 * n]
        token = refs[4 * n]
        x, y, c = lax.axis_index("x"), lax.axis_index("y"), lax.axis_index("c")
        for ox in (x, 1 - x):
            for oy in (y, 1 - y):
                owner = 4 * ox + 2 * oy + c
                for t in range(n):
                    w = win(t, lnd[t], owner)
                    pltpu.make_async_remote_copy(
                        src_ref=w, dst_ref=w, send_sem=sems[2 * t], recv_sem=sems[2 * t + 1],
                        device_id=(x, y, 1 - c), device_id_type=pl.DeviceIdType.MESH).start()
        token[...] = jnp.zeros_like(token)

    res = pl.pallas_call(
        body, name=name, in_specs=[_HBM] * n,
        out_specs=[_SEM] * (2 * n) + [_HBM] * n + [pl.BlockSpec(memory_space=pltpu.VMEM)],
        out_shape=[pltpu.SemaphoreType.DMA(())] * (2 * n) + [pltpu.HBM(s.shape, s.dtype) for s in lands]
        + [jax.ShapeDtypeStruct((8, 128), F32)],
        input_output_aliases={t: 2 * n + t for t in range(n)},
        compiler_params=pltpu.CompilerParams(has_side_effects=_EFFECT),
    )(*lands)
    return dict(n=n, sems=res[:2 * n], srcs=[], lands=res[2 * n:3 * n], token=res[3 * n])


def _gather_start(shards, names, tag, deps=(), two_level=False):
    xs = [shards[nm] for nm in names]
    axes = [_W_AXIS[nm] for nm in names]
    widths = [x.shape[ax] for x, ax in zip(xs, axes)]
    shapes = [(tuple(d * (N_DEV if a == ax else 1) for a, d in enumerate(x.shape)), x.dtype) for x, ax in zip(xs, axes)]
    src = lambda t, ref, peer: ref
    dst = lambda t, ref, who: _window(ref, axes[t], who, widths[t])
    h = _exchange_start(xs, shapes, src, dst, "gw_start_" + tag, deps, _SAME_CORE if two_level else _ALL_PEERS)
    h['tag'] = "gw_wait_" + tag
    if two_level:
        h['span'] = lambda t, ref: _window(ref, axes[t], 0, widths[t], len(_SAME_CORE))
        h['second'] = (dst, "gw_pass_" + tag, "gw_passed_" + tag)
    return h


def _scatter_start(grads, names, tag):
    xs = [grads[nm] for nm in names]
    axes = [_W_AXIS[nm] for nm in names]
    widths = [x.shape[ax] // N_DEV for x, ax in zip(xs, axes)]
    shapes = [((N_DEV,) + tuple(d // (N_DEV if a == ax else 1) for a, d in enumerate(x.shape)), x.dtype)
              for x, ax in zip(xs, axes)]
    src = lambda t, ref, peer: _window(ref, axes[t], peer, widths[t])
    dst = lambda t, ref, who: ref.at[who]
    h = _exchange_start(xs, shapes, src, dst, "sg_start_" + tag)
    h['tag'] = "sg_wait_" + tag
    return h


def _finish(h, after):
    return _exchange_wait(h, after, h['tag'])


class _LayerWeights:
    def __init__(self, groups, started):
        self.groups = groups
        self.started = started
        self.got = {}
        self.after = None

    def __getitem__(self, nm):
        if nm not in self.got:
            for names, h in self.groups:
                if nm in names:
                    self.got.update(zip(names, _finish(h, list(self.after) + self.started)))
        return self.got[nm]


def _adam_math(p_ref, w, m, v):
    g = p_ref[0].astype(F32)
    for s in range(1, p_ref.shape[0]):
        g = g + p_ref[s].astype(F32)
    bc1 = 1.0 / (1.0 - ADAM_B1 ** ADAM_STEP)
    bc2 = 1.0 / (1.0 - ADAM_B2 ** ADAM_STEP)
    mm = ADAM_B1 * m + (1.0 - ADAM_B1) * g
    vv = ADAM_B2 * v + (1.0 - ADAM_B2) * jnp.square(g)
    return g, -ADAM_LR * ((mm * bc1) / (jnp.sqrt(vv * bc2) + ADAM_EPS) + ADAM_WD * w), mm, vv


def _adam(parts, w, m, v, name, tr=256):
    P, R, C = parts.shape
    tr = min(tr, R)
    assert R % tr == 0

    def body(p_ref, w_ref, m_ref, v_ref, *outs):
        for o_ref, val in zip(outs, _adam_math(p_ref, w_ref[...], m_ref[...], v_ref[...])):
            o_ref[...] = val

    spec = pl.BlockSpec((tr, C), lambda i: (i, 0))
    return pl.pallas_call(
        body, name=name, grid=(R // tr,),
        in_specs=[pl.BlockSpec((P, tr, C), lambda i: (0, i, 0)), spec, spec, spec],
        out_specs=[spec] * 4, out_shape=[jax.ShapeDtypeStruct((R, C), F32)] * 4,
        compiler_params=_cparams(("parallel",)),
    )(parts, w, m, v)


def _adam_layer(parts, w, m, v, l, prev, name, tr=256):
    P, R, C = parts.shape
    tr = min(tr, R)
    assert R % tr == 0 and w.shape == (DEPTH, R, C)
    npv = 0 if prev is None else 4

    def body(p_ref, w_ref, m_ref, v_ref, *rest):
        for o_ref, val in zip(rest[npv:], _adam_math(p_ref, w_ref[0], m_ref[0], v_ref[0])):
            o_ref[0] = val

    spec = pl.BlockSpec((1, tr, C), lambda i: (l, i, 0))
    return pl.pallas_call(
        body, name=name, grid=(R // tr,),
        in_specs=[pl.BlockSpec((P, tr, C), lambda i: (0, i, 0)), spec, spec, spec]
        + [pl.BlockSpec(memory_space=pl.ANY)] * npv,
        out_specs=[spec] * 4, out_shape=[jax.ShapeDtypeStruct((DEPTH, R, C), F32)] * 4,
        input_output_aliases={4 + t: t for t in range(npv)},
        compiler_params=_cparams(("parallel",)),
    )(parts, w, m, v, *([] if prev is None else prev))


def _mod_partial(cact_all, w_ada_l, b_ada_l):
    def body(c_ref, w_ref, b_ref, o_ref):
        o_ref[...] = _dot(c_ref[...].astype(BF16), w_ref[...].astype(BF16)) + b_ref[...]

    return pl.pallas_call(
        body, name="mod_partial", out_shape=jax.ShapeDtypeStruct((cact_all.shape[0], w_ada_l.shape[1]), F32),
        compiler_params=pltpu.CompilerParams(vmem_limit_bytes=VMEM_LIMIT),
    )(cact_all, w_ada_l, b_ada_l)


def _ada_grad(cact_t, dmod):
    D, n = cact_t.shape[0], dmod.shape[1]

    def body(c_ref, d_ref, o_ref):
        ct = c_ref[...].astype(BF16).astype(F32)
        dm = d_ref[...].astype(BF16).astype(F32)
        acc = ct[:, 0:1] * dm[0:1, :]
        for b in range(1, N_DEV):
            acc = acc + ct[:, b:b + 1] * dm[b:b + 1, :]
        o_ref[0] = acc

    return pl.pallas_call(
        body, name="ada_grad", out_shape=jax.ShapeDtypeStruct((1, D, n), F32),
        compiler_params=pltpu.CompilerParams(vmem_limit_bytes=VMEM_LIMIT),
    )(cact_t, dmod)


def _norm_mod(x, r, gv, sh):
    return x * r * gv + sh


def _silu(x):
    return x * _sigmoid(x)


def _rstd(x):
    return lax.rsqrt(jnp.mean(x * x, axis=-1, keepdims=True) + EPS)


def _residual_epi(acc, x, g):
    xn = x + g * acc
    return acc, xn, _rstd(xn)


def _residual_norm_epi(acc, x, g, gv, sh):
    xn = x + g * acc
    r = _rstd(xn)
    return acc, xn, _norm_mod(xn, r, gv, sh), r


def _layer_fwd(x0, r1, h1, mod, gn1, gn2, nxt, W, rc, tri):
    S = x0.shape[0]
    sh1, sc1, g1m, sh2, sc2, g2m = [mod[i:i + 1] for i in range(N_MOD)]
    gv1 = gn1 * (1.0 + sc1)
    gv2 = gn2 * (1.0 + sc2)
    W.after = [h1]
    (p,) = _mm(h1, W["w_in"], tm=2048, outs=(BF16,), name="mm_in")
    yret, rstate = _ret_fwd(p, rc)
    ysb, sbc = _sb_fwd(p, tri)
    W.after = [ysb]
    (ya,) = _mm(yret, W["w_ret_out"], tm=512, a_ex=[(p, 'a', O_RG)],
                pro=lambda yr, g: _silu(g.astype(F32)) * yr.astype(F32), outs=(BF16,), name="mm_ret_out")
    yb, mg = _mm(ysb, W["w_sb_out"], o_ex=[(ya, 'o'), (p, 'o', O_GA), (p, 'o', O_GB)],
                 epi=lambda acc, a, ga, gb: (acc, _sigmoid(ga.astype(F32)) * a.astype(F32)
                                             + _sigmoid(gb.astype(F32)) * acc),
                 outs=(BF16, BF16), name="mm_sb_out")
    mo, x1, h2, r2 = _mm(mg, W["w_mix_out"], tm=512, o_ex=[(x0, 'o'), (g1m, 'n'), (gv2, 'n'), (sh2, 'n')],
                         epi=_residual_norm_epi, outs=(BF16, F32, BF16), cols=1, name="mm_mix_out")
    (act,) = _mm(h2, W["w_up"], epi=lambda acc: (jnp.maximum(acc, 0.0),), outs=(BF16,), name="mm_up")
    if nxt is None:
        dn, x2, r_out = _mm(act, W["w_down"], tm=512, pro=lambda a: a * a, o_ex=[(x1, 'o'), (g2m, 'n')],
                            epi=_residual_epi, outs=(BF16, F32), cols=1, name="mm_down_last")
        h_out = None
    else:
        dn, x2, h_out, r_out = _mm(act, W["w_down"], tm=512, pro=lambda a: a * a,
                                   o_ex=[(x1, 'o'), (g2m, 'n'), (nxt[0], 'n'), (nxt[1], 'n')],
                                   epi=_residual_norm_epi, outs=(BF16, F32, BF16), cols=1, name="mm_down")
    saved = dict(x0=x0, r1=r1, h1=h1, p=p, yret=yret, rstate=rstate, ysb=ysb, sbc=sbc, ya=ya, yb=yb, mg=mg, mo=mo, x1=x1,
                 r2=r2, h2=h2, act=act, dn=dn, gv1=gv1, gv2=gv2, mod=mod, gn1=gn1, gn2=gn2)
    return x2, r_out, h_out, saved


def _norm_bwd(dh, x, r, dres, gv, gn, extra_rows=(), extra_vecs=(), extra_fn=None, extra_outs=(), name="norm_bwd"):
    D = x.shape[1]
    ne = len(extra_rows)

    def fn(dh_t, x_t, dres_t, *rest):
        er, rest = rest[:ne], rest[ne:]
        gv_t = rest[0]
        ev, r_t = rest[1:-1], rest[-1]
        xh = x_t * r_t
        dxh = dh_t * gv_t
        dx = r_t * (dxh - xh * jnp.mean(dxh * xh, axis=-1, keepdims=True)) + dres_t
        base = (dx, dh_t, dh_t * xh)
        if extra_fn is None:
            return base
        return base + tuple(extra_fn(dx, *er, *ev))

    return _ew(fn, [dh, x, dres] + list(extra_rows), vecs=[gv] + list(extra_vecs), cols=[r],
               outs=[('row', D, F32), ('sum', D), ('sum', D)] + list(extra_outs), name=name)


def _gate_bwd(dx, dn, g):
    return dx * dn.astype(F32), dx * g


_GATE_OUTS = [('sum', D_MODEL), ('row', D_MODEL, BF16)]


def _layer_bwd(dx2, d_g2m, d_dn, sv, below, W, rc, tri, emit):
    mod = sv['mod']
    sh1, sc1, g1m, sh2, sc2, g2m = [mod[i:i + 1] for i in range(N_MOD)]
    D = D_MODEL
    p = sv['p']
    (d_up,) = _mm(d_dn, W["w_down"], tb=True, o_ex=[(sv['act'], 'o')],
                  epi=lambda acc, a: (acc * 2.0 * a.astype(F32),), outs=(BF16,), name="mm_down_dx")
    (gw_down,) = _mm(sv['act'], d_dn, ta=True, tk=DW_TK, pro=lambda a: a * a, outs=(BF16,), name="mm_down_dw")
    (gw_up,) = _mm(sv['h2'], d_up, ta=True, tk=DW_TK, outs=(BF16,), name="mm_up_dw", a_whole=True)
    tok = emit(dict(w_down=gw_down, w_up=gw_up), "mlp")
    (d_h2,) = _mm(d_up, W["w_up"], tb=True, tk=2048, outs=(F32,), name="mm_up_dx", deps=[tok])
    dx1, d_sh2, s_h2, d_g1m, d_mo = _norm_bwd(
        d_h2, sv['x1'], sv['r2'], dx2, sv['gv2'], sv['gn2'],
        extra_rows=[sv['mo']], extra_vecs=[g1m],
        extra_fn=lambda dx, mo, g: (dx * mo.astype(F32), dx * g),
        extra_outs=[('sum', D), ('row', D, BF16)], name="norm_bwd_mlp")
    d_sc2 = sv['gn2'] * s_h2
    d_gn2 = (1.0 + sc2) * s_h2
    def mix_epi(acc, ya, yb, ga, gb):
        sa, sb = _sigmoid(ga.astype(F32)), _sigmoid(gb.astype(F32))
        return (acc * sa, acc * sb, acc * ya.astype(F32) * sa * (1.0 - sa), acc * yb.astype(F32) * sb * (1.0 - sb))

    d_ya, d_yb, d_ga, d_gb = _mm(d_mo, W["w_mix_out"], tb=True, tm=512,
                                 o_ex=[(sv['ya'], 'o'), (sv['yb'], 'o'), (p, 'o', O_GA), (p, 'o', O_GB)], epi=mix_epi,
                                 outs=(BF16,) * 4, name="mm_mix_dx")
    (gw_mix,) = _mm(sv['mg'], d_mo, ta=True, tk=DW_TK, outs=(BF16,), name="mm_mix_dw")

    def ro_epi(acc, g, yr):
        gf = g.astype(F32)
        s = _sigmoid(gf)
        return (acc * yr.astype(F32) * s * (1.0 + gf * (1.0 - s)), acc * gf * s)

    d_rg, d_yret = _mm(d_ya, W["w_ret_out"], tb=True, tm=512, o_ex=[(p, 'o', O_RG), (sv['yret'], 'o')], epi=ro_epi,
                       outs=(BF16, BF16), name="mm_ret_dx")
    (gw_ro,) = _mm(sv['yret'], d_ya, ta=True, tm=512, tk=DW_TK, a_ex=[(p, 'a', O_RG)],
                   pro=lambda yr, g: _silu(g.astype(F32)) * yr.astype(F32), outs=(BF16,), name="mm_ret_dw")
    (gw_so,) = _mm(sv['ysb'], d_yb, ta=True, tk=DW_TK, outs=(BF16,), name="mm_sb_dw")
    tok = emit(dict(w_mix_out=gw_mix, w_ret_out=gw_ro, w_sb_out=gw_so), "mix")
    (d_ysb,) = _mm(d_yb, W["w_sb_out"], tb=True, outs=(BF16,), name="mm_sb_dx", deps=[tok])
    d_sq, d_sk, d_sv = _sb_bwd(p, sv['sbc'], d_ysb, tri)
    d_ret = _ret_bwd(p, sv['rstate'], d_yret, rc)
    dp = [d_ret, d_rg, d_sq, d_sk, d_sv, d_ga, d_gb]
    (gw_in,) = _mm(sv['h1'], dp, ta=True, tk=1024, outs=(BF16,), name="mm_in_dw", a_whole=True)
    tok = emit(dict(w_in=gw_in), "in")
    (d_h,) = _mm(dp, W["w_in"], tb=True, tm=2048, tk=512, outs=(F32,), name="mm_in_dx", deps=[tok])
    if below is None:
        dx0, d_sh1, s_h1 = _norm_bwd(d_h, sv['x0'], sv['r1'], dx1, sv['gv1'], sv['gn1'], name="norm_bwd_mix")
        gate_below = (None, None)
    else:
        dx0, d_sh1, s_h1, *gate_below = _norm_bwd(
            d_h, sv['x0'], sv['r1'], dx1, sv['gv1'], sv['gn1'], extra_rows=[below['dn']],
            extra_vecs=[below['mod'][N_MOD - 1:N_MOD]], extra_fn=_gate_bwd, extra_outs=_GATE_OUTS,
            name="norm_bwd_mix_gate")
    d_sc1 = sv['gn1'] * s_h1
    d_gn1 = (1.0 + sc1) * s_h1
    d_mod = jnp.concatenate([d_sh1, d_sc1, d_g1m, d_sh2, d_sc2, d_g2m], axis=1)
    return dx0, gate_below, d_mod, d_gn1, d_gn2


def kernel(x, c, norm_mix_g, w_in, w_ret_out, w_sb_out, w_mix_out, norm_mlp_g, w_up, w_down, w_ada, b_ada, final_g, loss_target, m_norm_mix_g, m_w_in, m_w_ret_out, m_w_sb_out, m_w_mix_out, m_norm_mlp_g, m_w_up, m_w_down, m_w_ada, m_b_ada, m_final_g, v_norm_mix_g, v_w_in, v_w_ret_out, v_w_sb_out, v_w_mix_out, v_norm_mlp_g, v_w_up, v_w_down, v_w_ada, v_b_ada, v_final_g):
    S, D = x.shape[1], x.shape[2]
    x0 = x.reshape(S, D)
    tgt = loss_target.reshape(S, D)
    me = 4 * lax.axis_index("x") + 2 * lax.axis_index("y") + lax.axis_index("c")
    wts = dict(w_in=w_in, w_ret_out=w_ret_out, w_sb_out=w_sb_out, w_mix_out=w_mix_out, w_up=w_up, w_down=w_down)
    mts = dict(w_in=m_w_in, w_ret_out=m_w_ret_out, w_sb_out=m_w_sb_out, w_mix_out=m_w_mix_out, w_up=m_w_up, w_down=m_w_down)
    vts = dict(w_in=v_w_in, w_ret_out=v_w_ret_out, w_sb_out=v_w_sb_out, w_mix_out=v_w_mix_out, w_up=v_w_up, w_down=v_w_down)
    rc = _ret_consts(S)
    tri = _tri()

    (cact,) = _ew(lambda t: (_silu(t),), [jnp.pad(c, ((0, 7), (0, 0)))], outs=[('row', D, F32)], name="silu_c")
    (cact_all,) = _all_gather_lead([cact[0:1]], "gather_c")
    cact_all = cact_all.reshape(N_DEV, D)
    cact16 = jnp.pad(cact_all, ((0, 8), (0, 0)))
    n_ada = w_ada.shape[2]
    b_loc = lax.dynamic_slice_in_dim(b_ada, me * n_ada, n_ada, axis=1)
    mods = [_mod_partial(cact16, w_ada[l], b_loc[l:l + 1])[:N_DEV] for l in range(DEPTH)]
    modp = jnp.stack(mods, axis=1)
    (modr,) = _all_to_all_lead([modp], "scatter_mod")
    mod_full = jnp.transpose(modr, (1, 0, 2)).reshape(DEPTH, N_MOD, D)

    def cast(nm, deps=()):
        w = wts[nm]
        (wb,) = _ew(lambda t: (t,), [w.reshape(-1, w.shape[-1])], outs=[('row', w.shape[-1], BF16)],
                    name="cast_bf16", deps=deps)
        return wb.reshape(w.shape)

    rest = tuple(nm for nm in _W_NAMES if nm != "w_in")
    shards = {"w_in": cast("w_in")}
    h_in = _gather_start({"w_in": shards["w_in"][0]}, ("w_in",), "0_in", [modr], two_level=True)
    in_flight = [h_in['token']]
    for nm in rest:
        shards[nm] = cast(nm, in_flight)
    sh = [{nm: shards[nm][l] for nm in _W_NAMES} for l in range(DEPTH)]

    pre = [(norm_mix_g[l:l + 1] * (1.0 + mod_full[l][1:2]), mod_full[l][0:1]) for l in range(DEPTH)]

    def first(t, gv, sh):
        r = _rstd(t)
        return r, _norm_mod(t, r, gv, sh)

    xs = x0
    rs, hs = _ew(first, [x0], vecs=list(pre[0]), outs=[('col', F32), ('row', D, BF16)], name="row_rstd",
                 deps=in_flight)

    win, pass_name, passed_name = h_in['second']
    before_wait = [hs] + [sh[l][nm] for l in range(DEPTH) for nm in _W_NAMES if (l, nm) != (0, "w_in")]
    h_pass = _sibling_start(_exchange_wait(h_in, before_wait, h_in['tag']), win, pass_name)
    h_pass['span'] = h_in['span']
    started = [h_pass['token']]
    layer_groups = []
    for l, groups in enumerate([[(rest, "0_rest")]] + [[(_W_NAMES, "%d_all" % l)] for l in range(1, DEPTH)]):
        layer_groups.append([])
        for names, tag in groups:
            layer_groups[-1].append((names, _gather_start(sh[l], names, tag, started[-1:])))
            started.append(layer_groups[-1][-1][1]['token'])
    (w_in0,) = _exchange_wait(h_pass, started, passed_name)
    layer_w = [_LayerWeights(g, started) for g in layer_groups]
    layer_w[0].got["w_in"] = w_in0
    saved = []
    for l in range(DEPTH):
        xs, rs, hs, sv = _layer_fwd(xs, rs, hs, mod_full[l], norm_mix_g[l:l + 1], norm_mlp_g[l:l + 1],
                                    pre[l + 1] if l + 1 < DEPTH else None, layer_w[l], rc, tri)
        sv['W'] = layer_w[l]
        saved.append(sv)

    fg = final_g.reshape(1, D)

    def head(xt, tg, dn, g, g2m, r):
        xh = xt * r
        e = xh * g - tg
        dy = e * (1.0 / D)
        dxh = dy * g
        dx = r * (dxh - xh * jnp.mean(dxh * xh, axis=-1, keepdims=True))
        return (dx, dy * xh, 0.5 * e * e * (1.0 / D)) + _gate_bwd(dx, dn, g2m)

    top = saved[DEPTH - 1]
    dxs, d_fg, loss_cols, *gate = _ew(head, [xs, tgt, top['dn']], vecs=[fg, top['mod'][N_MOD - 1:N_MOD]], cols=[rs],
                                      outs=[('row', D, F32), ('sum', D), ('sum', D)] + _GATE_OUTS, name="loss_head")

    small = [None] * DEPTH
    pending = []
    for l in reversed(range(DEPTH)):
        sv = saved[l]

        def emit(gw, tag, l=l):
            names = tuple(gw)
            pending.append((l, names, _scatter_start(gw, names, "%d_%s" % (l, tag))))
            return pending[-1][2]['token']

        dxs, gate, d_mod, d_gn1, d_gn2 = _layer_bwd(dxs, gate[0], gate[1], sv, saved[l - 1] if l else None,
                                                    sv['W'], rc, tri, emit)
        small[l] = (d_mod, d_gn1, d_gn2)
    grad_x = dxs.reshape(1, S, D)

    res = {}
    after = [dxs]
    for l, names, h in pending:
        for nm, landed in zip(names, _finish(h, after)):
            res[nm] = _adam_layer(landed, wts[nm], mts[nm], vts[nm], l, res.get(nm), "adam_layer")
        after = [res[names[-1]][0]]

    pack = jnp.concatenate([small[l][0] for l in range(DEPTH)] + [small[l][1] for l in range(DEPTH)]
                           + [small[l][2] for l in range(DEPTH)] + [d_fg, loss_cols], axis=1)
    (packs,) = _all_gather_lead([pack], "gather_small", deps=after)
    packs = packs.reshape(N_DEV, -1)
    o = 0
    dmod_all = []
    for l in range(DEPTH):
        dmod_all.append(packs[:, o:o + N_MOD * D]); o += N_MOD * D
    gn1_parts = packs[:, o:o + DEPTH * D].reshape(N_DEV, DEPTH, D); o += DEPTH * D
    gn2_parts = packs[:, o:o + DEPTH * D].reshape(N_DEV, DEPTH, D); o += DEPTH * D
    fg_parts = packs[:, o:o + D].reshape(N_DEV, 1, D); o += D
    loss_parts = packs[:, o:o + D]
    (loss_sum,) = _ew(lambda t: (t,), [loss_parts], outs=[('sum', D)], name="loss_sum")
    loss = jnp.sum(loss_sum)

    res["norm_mix_g"] = _adam(gn1_parts, norm_mix_g, m_norm_mix_g, v_norm_mix_g, "adam")
    res["norm_mlp_g"] = _adam(gn2_parts, norm_mlp_g, m_norm_mlp_g, v_norm_mlp_g, "adam")
    fgr = _adam(fg_parts, fg, m_final_g.reshape(1, D), v_final_g.reshape(1, D), "adam")
    res["final_g"] = [t.reshape(D) for t in fgr]
    bparts = jnp.stack(dmod_all, axis=1)
    res["b_ada"] = _adam(bparts, b_ada, m_b_ada, v_b_ada, "adam")
    cact_t = cact_all.T
    for l in range(DEPTH):
        dm_loc = lax.dynamic_slice_in_dim(dmod_all[l], me * n_ada, n_ada, axis=1)
        res["w_ada"] = _adam_layer(_ada_grad(cact_t, dm_loc), w_ada, m_w_ada, v_w_ada, l, res.get("w_ada"),
                                   "adam_layer")

    order = ['norm_mix_g', 'w_in', 'w_ret_out', 'w_sb_out', 'w_mix_out', 'norm_mlp_g', 'w_up', 'w_down', 'w_ada', 'b_ada', 'final_g']
    out = [loss, grad_x]
    for i in range(4):
        out += [res[nm][i] for nm in order]
    return tuple(out)
```

```python
import functools
import math

import jax
import jax.numpy as jnp
import numpy as np
from jax import lax
from jax.experimental import pallas as pl
from jax.experimental.pallas import tpu as pltpu

F32 = jnp.float32
BF16 = jnp.bfloat16

N_DEV = 8
D_MODEL = 1024
DEPTH = 2
RET_HEADS = 4
RET_QK = 256
RET_V = 512
RET_CHUNK = 256
ROPE_BASE = 10000.0
SB_HEADS = 16
SB_DIM = 64
D_FF = 4096
N_MOD = 6
EPS = 1e-6
GN_EPS = 1e-5
O_RQ, O_RK, O_RV, O_RG, O_SQ, O_SK, O_SV, O_GA, O_GB = 0, 1024, 2048, 4096, 6144, 7168, 8192, 9216, 10240
IN_W = 11264

ADAM_LR, ADAM_B1, ADAM_B2, ADAM_EPS, ADAM_WD, ADAM_STEP = 0.001, 0.9, 0.999, 1e-08, 0.01, 10

VMEM_LIMIT = 56 * 1024 * 1024
DW_TK = 2048


def _cparams(sem):
    return pltpu.CompilerParams(dimension_semantics=sem, vmem_limit_bytes=VMEM_LIMIT)


def _mm(a, b, *, ta=False, tb=False, tm=1024, tn=1024, tk=None, a_ex=(), pro=None, o_ex=(), epi=None,
        outs=(F32,), cols=0, name, deps=()):
    a_parts = list(a) if isinstance(a, (list, tuple)) else [a]
    b_parts = list(b) if isinstance(b, (list, tuple)) else [b]
    assert not (ta and len(a_parts) > 1) and not (tb and len(b_parts) > 1)
    if ta:
        K, M = a.shape
    else:
        M, K = a_parts[0].shape[0], sum(t.shape[1] for t in a_parts)
    N = b.shape[0] if tb else sum(t.shape[1] for t in b_parts)
    tm, tn, tk = min(tm, M), min(tn, N), K if tk is None else min(tk, K)
    assert M % tm == 0 and N % tn == 0 and K % tk == 0, (name, M, N, K, tm, tn, tk)
    nk = K // tk
    multi = len(a_parts) > 1 or len(b_parts) > 1
    direct = epi is None and tuple(outs) == (F32,) and cols == 0

    def ranges(parts, t):
        out, o = [], 0
        for arr in parts:
            assert arr.shape[1] % t == 0
            out.append((o, o + arr.shape[1] // t))
            o += arr.shape[1] // t
        return out

    a_rng = ranges(a_parts, tk) if len(a_parts) > 1 else [(0, nk)]
    b_rng = ranges(b_parts, tn) if len(b_parts) > 1 else [(0, N // tn)]
    clip = lambda v, lo, hi: jnp.clip(v - lo, 0, hi - lo - 1)
    mine = lambda v, lo, hi, w: jnp.where((v >= lo) & (v < hi), w, 0)
    in_specs, args = [], []
    for arr, (lo, hi) in zip(a_parts, a_rng):
        in_specs.append(pl.BlockSpec((tk, tm), lambda i, j, k: (k, i)) if ta
                        else pl.BlockSpec((tm, tk), lambda i, j, k, lo=lo, hi=hi: (i, clip(k, lo, hi))))
        args.append(arr)
    for arr, (lo, hi) in zip(b_parts, b_rng):
        in_specs.append(pl.BlockSpec((tn, tk), lambda i, j, k: (j, k)) if tb
                        else pl.BlockSpec((tk, tn), lambda i, j, k, lo=lo, hi=hi: (mine(j, lo, hi, k), clip(j, lo, hi))))
        args.append(arr)
    npa, npb = len(a_parts), len(b_parts)
    for arr, kind, *off in a_ex:
        off = off[0] if off else 0
        if kind == 'a' and ta:
            assert off % tm == 0
            in_specs.append(pl.BlockSpec((tk, tm), lambda i, j, k, o=off // tm: (k, o + i)))
        elif kind == 'a':
            assert off % tk == 0
            in_specs.append(pl.BlockSpec((tm, tk), lambda i, j, k, o=off // tk: (i, o + k)))
        elif kind == 'k':
            in_specs.append(pl.BlockSpec((tk, 1), lambda i, j, k: (k, 0)) if ta
                            else pl.BlockSpec((1, tk), lambda i, j, k: (0, k)))
        else:
            in_specs.append(pl.BlockSpec((1, tm), lambda i, j, k: (0, i)) if ta
                            else pl.BlockSpec((tm, 1), lambda i, j, k: (i, 0)))
        args.append(arr)
    for arr, kind, *off in o_ex:
        off = off[0] if off else 0
        if kind == 'o':
            assert off % tn == 0
            in_specs.append(pl.BlockSpec((tm, tn), lambda i, j, k, o=off // tn: (i, o + j)))
        elif kind == 'n':
            in_specs.append(pl.BlockSpec((1, tn), lambda i, j, k: (0, j)))
        else:
            in_specs.append(pl.BlockSpec((tm, 1), lambda i, j, k: (i, 0)))
        args.append(arr)
    for arr in deps:
        in_specs.append(pl.BlockSpec(memory_space=pl.ANY))
        args.append(arr)
    assert cols == 0 or N == tn
    na, no, nout, nd = len(a_ex), len(o_ex), len(outs) + cols, len(deps)
    dims = (((0 if ta else 1,), (1 if tb else 0,)), ((), ()))

    def body(*refs):
        a_refs, b_refs = refs[:npa], refs[npa:npa + npb]
        n0 = npa + npb
        aex = refs[n0:n0 + na]
        oex = refs[n0 + na:n0 + na + no]
        out_refs = refs[n0 + na + no + nd:n0 + na + no + nd + nout]

        def product(a_ref, b_ref):
            at = a_ref[...]
            if pro is not None:
                at = pro(at, *[r[...] for r in aex])
            return lax.dot_general(at.astype(BF16), b_ref[...].astype(BF16), dims, preferred_element_type=F32)

        def finish(res):
            vals = epi(res, *[r[...] for r in oex]) if epi is not None else (res,)
            for o_ref, v in zip(out_refs, vals):
                o_ref[...] = v.astype(o_ref.dtype)

        if nk == 1 and not multi:
            finish(product(a_refs[0], b_refs[0]))
            return
        acc = out_refs[0] if direct else refs[-1]
        j, k = pl.program_id(1), pl.program_id(2)
        if multi:
            @pl.when(k == 0)
            def _():
                acc[...] = jnp.zeros_like(acc)

            for a_ref, (alo, ahi) in zip(a_refs, a_rng):
                for b_ref, (blo, bhi) in zip(b_refs, b_rng):
                    @pl.when((k >= alo) & (k < ahi) & (j >= blo) & (j < bhi))
                    def _():
                        acc[...] += product(a_ref, b_ref)
        else:
            @pl.when(k == 0)
            def _():
                acc[...] = product(a_refs[0], b_refs[0])

            @pl.when(k > 0)
            def _():
                acc[...] += product(a_refs[0], b_refs[0])

        if not direct:
            @pl.when(k == nk - 1)
            def _():
                finish(acc[...])

    res = pl.pallas_call(
        body, name=name, grid=(M // tm, N // tn, nk), in_specs=in_specs,
        out_specs=[pl.BlockSpec((tm, tn), lambda i, j, k: (i, j)) for _ in outs]
        + [pl.BlockSpec((tm, 1), lambda i, j, k: (i, 0))] * cols,
        out_shape=[jax.ShapeDtypeStruct((M, N), dt) for dt in outs] + [jax.ShapeDtypeStruct((M, 1), F32)] * cols,
        scratch_shapes=[pltpu.VMEM((tm, tn), F32)] if (nk > 1 or multi) and not direct else [],
        compiler_params=_cparams(("parallel", "parallel", "arbitrary")),
    )(*args)
    return res


def _ew(fn, rows, vecs=(), cols=(), outs=(), tr=256, name=None, deps=()):
    S = rows[0].shape[0]
    tr = min(tr, S)
    assert S % tr == 0
    in_specs, args = [], []
    for r in rows:
        in_specs.append(pl.BlockSpec((tr, r.shape[1]), lambda i: (i, 0)))
        args.append(r)
    for v in vecs:
        in_specs.append(pl.BlockSpec((1, v.shape[1]), lambda i: (0, 0)))
        args.append(v)
    for c in cols:
        in_specs.append(pl.BlockSpec((tr, 1), lambda i: (i, 0)))
        args.append(c)
    out_specs, out_shape = [], []
    for o in outs:
        if o[0] == 'row':
            out_specs.append(pl.BlockSpec((tr, o[1]), lambda i: (i, 0)))
            out_shape.append(jax.ShapeDtypeStruct((S, o[1]), o[2]))
        elif o[0] == 'sum':
            out_specs.append(pl.BlockSpec((1, o[1]), lambda i: (0, 0)))
            out_shape.append(jax.ShapeDtypeStruct((1, o[1]), F32))
        else:
            out_specs.append(pl.BlockSpec((tr, 1), lambda i: (i, 0)))
            out_shape.append(jax.ShapeDtypeStruct((S, 1), o[1]))
    nin = len(args)
    in_specs += [pl.BlockSpec(memory_space=pl.ANY)] * len(deps)
    args += list(deps)

    def body(*refs):
        i = pl.program_id(0)
        vals = fn(*[r[...] for r in refs[:nin]])
        for o, o_ref, v in zip(outs, refs[nin + len(deps):], vals):
            if o[0] == 'sum':
                @pl.when(i == 0)
                def _():
                    o_ref[...] = jnp.zeros_like(o_ref)
                o_ref[...] += jnp.sum(v.astype(F32), axis=0, keepdims=True)
            else:
                o_ref[...] = v.astype(o_ref.dtype)

    return pl.pallas_call(
        body, name=name, grid=(S // tr,), in_specs=in_specs, out_specs=out_specs, out_shape=out_shape,
        compiler_params=_cparams(("arbitrary",)),
    )(*args)


def _sigmoid(x):
    return 1.0 / (1.0 + jnp.exp(-x))


def _ret_consts(S):
    h = np.arange(RET_HEADS, dtype=np.float64)
    log_gamma = np.log1p(-np.power(2.0, -5.0 - h))
    idx = np.arange(RET_CHUNK, dtype=np.float64)
    rel = idx[:, None] - idx[None, :]
    decay = np.where(rel >= 0, np.exp(np.maximum(rel, 0.0) * log_gamma[:, None, None]), 0.0)
    xi = np.exp((idx + 1.0) * log_gamma[:, None])[:, :, None]
    zeta = np.exp((RET_CHUNK - 1.0 - idx) * log_gamma[:, None])[:, :, None]
    gamma_c = np.exp(RET_CHUNK * log_gamma)[:, None, None]
    half = RET_QK // 2
    inv_freq = np.power(ROPE_BASE, -np.arange(half, dtype=np.float64) / half).astype(np.float32)
    ang = np.arange(S, dtype=np.float32)[:, None] * inv_freq[None, :]
    f = lambda t: jnp.asarray(t, F32)
    return dict(decay=f(decay), xi=f(xi), zeta=f(zeta), gc=f(gamma_c), cos=f(np.cos(ang)), sin=f(np.sin(ang)))


def _rot(t, cos, sin):
    half = RET_QK // 2
    t1, t2 = t[:, :half], t[:, half:]
    return jnp.concatenate([t1 * cos - t2 * sin, t1 * sin + t2 * cos], axis=-1)


def _rot_inv(t, cos, sin):
    half = RET_QK // 2
    t1, t2 = t[:, :half], t[:, half:]
    return jnp.concatenate([t1 * cos + t2 * sin, t2 * cos - t1 * sin], axis=-1)


_NT = (((1,), (1,)), ((), ()))
_TN = (((0,), (0,)), ((), ()))


def _dot(a, b):
    return jnp.dot(a, b, preferred_element_type=F32)


def _dot_nt(a, b):
    return lax.dot_general(a, b, _NT, preferred_element_type=F32)


def _dot_tn(a, b):
    return lax.dot_general(a, b, _TN, preferred_element_type=F32)


_QW, _VW = RET_HEADS * RET_QK, RET_HEADS * RET_V
_HEADS = range(RET_HEADS)


def _ret_in_specs(C, rev, NC):
    n_of = (lambda n: NC - 1 - n) if rev else (lambda n: n)
    whole3 = lambda n: (0, 0, 0)
    return [
        pl.BlockSpec((C, _QW), lambda n: (n_of(n), O_RQ // _QW)),
        pl.BlockSpec((C, _QW), lambda n: (n_of(n), O_RK // _QW)),
        pl.BlockSpec((C, _VW), lambda n: (n_of(n), O_RV // _VW)),
        pl.BlockSpec((C, RET_QK // 2), lambda n: (n_of(n), 0)),
        pl.BlockSpec((C, RET_QK // 2), lambda n: (n_of(n), 0)),
        pl.BlockSpec((RET_HEADS, C, C), whole3),
        pl.BlockSpec((RET_HEADS, C, 1), whole3),
        pl.BlockSpec((RET_HEADS, C, 1), whole3),
        pl.BlockSpec((RET_HEADS, 1, 1), whole3),
    ]


def _qk_heads(q_ref, k_ref, cos, sin):
    qs, kfs = [], []
    for h in _HEADS:
        cols = slice(h * RET_QK, (h + 1) * RET_QK)
        qs.append(_rot(q_ref[:, cols].astype(F32), cos, sin).astype(BF16))
        kfs.append(_rot(k_ref[:, cols].astype(F32), cos, sin) * (RET_QK ** -0.5))
    return qs, kfs


def _ret_fwd(p, rc):
    S = p.shape[0]
    C = RET_CHUNK
    NC = S // C

    def body(q_ref, k_ref, v_ref, cos_ref, sin_ref, dec_ref, xi_ref, zeta_ref, gc_ref, y_ref, rs_ref, r_acc):
        n = pl.program_id(0)

        @pl.when(n == 0)
        def _():
            r_acc[...] = jnp.zeros_like(r_acc)

        cos, sin = cos_ref[...], sin_ref[...]
        qs, kfs = _qk_heads(q_ref, k_ref, cos, sin)
        vs = [v_ref[:, h * RET_V:(h + 1) * RET_V] for h in _HEADS]
        rbs = [r_acc[h].astype(BF16) for h in _HEADS]
        for h in _HEADS:
            rs_ref[h, 0] = rbs[h]
        ss = [(_dot_nt(qs[h], kfs[h].astype(BF16)) * dec_ref[h]).astype(BF16) for h in _HEADS]
        os = [_dot(ss[h], vs[h]) + _dot(qs[h], rbs[h]) * xi_ref[h] for h in _HEADS]
        for h in _HEADS:
            o = os[h]
            mu = jnp.mean(o, axis=-1, keepdims=True)
            var = jnp.mean(jnp.square(o - mu), axis=-1, keepdims=True)
            y_ref[:, h * RET_V:(h + 1) * RET_V] = ((o - mu) * lax.rsqrt(var + GN_EPS)).astype(y_ref.dtype)
        for h in _HEADS:
            kz = (kfs[h] * zeta_ref[h]).astype(BF16)
            r_acc[h] = r_acc[h] * gc_ref[h] + _dot_tn(kz, vs[h])

    return pl.pallas_call(
        body, name="ret_fwd", grid=(NC,), in_specs=_ret_in_specs(C, False, NC),
        out_specs=[pl.BlockSpec((C, _VW), lambda n: (n, 0)),
                   pl.BlockSpec((RET_HEADS, 1, RET_QK, RET_V), lambda n: (0, n, 0, 0))],
        out_shape=[jax.ShapeDtypeStruct((S, _VW), BF16),
                   jax.ShapeDtypeStruct((RET_HEADS, NC, RET_QK, RET_V), BF16)],
        scratch_shapes=[pltpu.VMEM((RET_HEADS, RET_QK, RET_V), F32)],
        compiler_params=_cparams(("arbitrary",)),
    )(p, p, p, rc['cos'], rc['sin'], rc['decay'], rc['xi'], rc['zeta'], rc['gc'])


def _ret_bwd(p, rstate, dy, rc):
    S = p.shape[0]
    C = RET_CHUNK
    NC = S // C

    def body(q_ref, k_ref, v_ref, cos_ref, sin_ref, dec_ref, xi_ref, zeta_ref, gc_ref, rs_ref, dy_ref,
             d_ref, dr_acc):
        dq_ref, dk_ref, dv_ref = d_ref.at[:, 0:_QW], d_ref.at[:, _QW:2 * _QW], d_ref.at[:, 2 * _QW:2 * _QW + _VW]
        t = pl.program_id(0)

        @pl.when(t == 0)
        def _():
            dr_acc[...] = jnp.zeros_like(dr_acc)

        cos, sin = cos_ref[...], sin_ref[...]
        qs, kfs = _qk_heads(q_ref, k_ref, cos, sin)
        ks = [kf.astype(BF16) for kf in kfs]
        vs = [v_ref[:, h * RET_V:(h + 1) * RET_V] for h in _HEADS]
        rbs = [rs_ref[h, 0] for h in _HEADS]
        ss = [(_dot_nt(qs[h], ks[h]) * dec_ref[h]).astype(BF16) for h in _HEADS]
        os = [_dot(ss[h], vs[h]) + _dot(qs[h], rbs[h]) * xi_ref[h] for h in _HEADS]
        dobs, doxis = [], []
        for h in _HEADS:
            o = os[h]
            mu = jnp.mean(o, axis=-1, keepdims=True)
            var = jnp.mean(jnp.square(o - mu), axis=-1, keepdims=True)
            rstd = lax.rsqrt(var + GN_EPS)
            yh = (o - mu) * rstd
            dyf = dy_ref[:, h * RET_V:(h + 1) * RET_V].astype(F32)
            do = (dyf - jnp.mean(dyf, axis=-1, keepdims=True)
                  - yh * jnp.mean(dyf * yh, axis=-1, keepdims=True)) * rstd
            dobs.append(do.astype(BF16))
            doxis.append((do * xi_ref[h]).astype(BF16))
        drbs = [dr_acc[h].astype(BF16) for h in _HEADS]
        dss = [(_dot_nt(dobs[h], vs[h]) * dec_ref[h]).astype(BF16) for h in _HEADS]
        for h in _HEADS:
            dq = _dot(dss[h], ks[h]) + _dot_nt(doxis[h], rbs[h])
            dq_ref[:, h * RET_QK:(h + 1) * RET_QK] = _rot_inv(dq, cos, sin).astype(dq_ref.dtype)
        for h in _HEADS:
            dk = _dot_tn(dss[h], qs[h]) + _dot_nt(vs[h], drbs[h]) * zeta_ref[h]
            dk_ref[:, h * RET_QK:(h + 1) * RET_QK] = (_rot_inv(dk, cos, sin) * (RET_QK ** -0.5)).astype(dk_ref.dtype)
        for h in _HEADS:
            kz = (kfs[h] * zeta_ref[h]).astype(BF16)
            dv = _dot_tn(ss[h], dobs[h]) + _dot(kz, drbs[h])
            dv_ref[:, h * RET_V:(h + 1) * RET_V] = dv.astype(dv_ref.dtype)
        for h in _HEADS:
            dr_acc[h] = dr_acc[h] * gc_ref[h] + _dot_tn(qs[h], doxis[h])

    rn = lambda n: NC - 1 - n
    in_specs = _ret_in_specs(C, True, NC) + [
        pl.BlockSpec((RET_HEADS, 1, RET_QK, RET_V), lambda n: (0, rn(n), 0, 0)),
        pl.BlockSpec((C, _VW), lambda n: (rn(n), 0)),
    ]
    return pl.pallas_call(
        body, name="ret_bwd", grid=(NC,), in_specs=in_specs,
        out_specs=pl.BlockSpec((C, 2 * _QW + _VW), lambda n: (rn(n), 0)),
        out_shape=jax.ShapeDtypeStruct((S, 2 * _QW + _VW), BF16),
        scratch_shapes=[pltpu.VMEM((RET_HEADS, RET_QK, RET_V), F32)],
        compiler_params=_cparams(("arbitrary",)),
    )(p, p, p, rc['cos'], rc['sin'], rc['decay'], rc['xi'], rc['zeta'], rc['gc'], rstate, dy)


SB_T = 256
SB_SCALE = SB_DIM ** -0.5


def _tri():
    j = np.arange(SB_T)
    after = (j[:, None] > j[None, :]).astype(np.float32)
    upto = (j[:, None] <= j[None, :]).astype(np.float32)
    return jnp.asarray(np.stack([after, upto]), BF16)


def _softplus_parts(z):
    neg_abs = lax.bitcast_convert_type(lax.bitcast_convert_type(z, jnp.uint32) | jnp.uint32(0x80000000), F32)
    e = jnp.exp(neg_abs)
    return jnp.maximum(z, 0.0) + jnp.log(1.0 + e), e


def _sb_fwd(p, tri):
    S = p.shape[0]
    T = min(SB_T, S)
    NQ = S // T
    assert NQ <= 128
    qb, kb, vb = O_SQ // 128, O_SK // 128, O_SV // 128

    def body(q_ref, k_ref, v_ref, tri_ref, o_ref, cs_ref, o_acc, run, zbuf, abuf):
        i = pl.program_id(1)
        lane = lax.broadcasted_iota(jnp.int32, (1, 128), 1)
        tri_after = tri_ref[0]
        qs = [jnp.where((lane >= 64) if hh else (lane < 64), q_ref[...], jnp.zeros_like(q_ref[...]))
              * jnp.asarray(SB_SCALE, BF16) for hh in range(2)]
        cs_ref[...] = jnp.zeros_like(cs_ref)
        o_acc[...] = jnp.zeros_like(o_acc)
        run[...] = jnp.zeros_like(run)

        def kv(ref, j):
            return ref[pl.ds(pl.multiple_of(j * T, T), T), :]

        for hh in range(2):
            zbuf[hh] = _dot_nt(qs[hh], kv(k_ref, i))

        def block(t, diagonal):
            j = i - t
            if diagonal:
                msk = lax.broadcasted_iota(jnp.int32, (T, T), 1) < lax.broadcasted_iota(jnp.int32, (T, T), 0)
            if not diagonal:
                av = [_dot(abuf[hh], kv(v_ref, j + 1)) for hh in range(2)]
            lss, exs, tot, zn = [], [], [], []
            for hh in range(2):
                z = zbuf[hh]
                sp, _ = _softplus_parts(z)
                lss.append(z - sp)
                if diagonal:
                    sp = jnp.where(msk, sp, 0.0)
                exs.append(_dot(sp.astype(BF16), tri_after))
                tot.append(sp[:, 0:1])
                zn.append(_dot_nt(qs[hh], kv(k_ref, jnp.maximum(j - 1, 0))))
            for hh in range(2):
                csl = slice(hh * 128, (hh + 1) * 128)
                cs = run[hh]
                a = jnp.exp(lss[hh] - exs[hh] - cs)
                if diagonal:
                    a = jnp.where(msk, a, 0.0)
                abuf[hh] = a.astype(BF16)
                cs_ref[:, csl] = jnp.where(lane == j, cs, cs_ref[:, csl])
                run[hh] = cs + exs[hh][:, 0:1] + tot[hh]
            for hh in range(2):
                if not diagonal:
                    o_acc[hh] += av[hh]
                zbuf[hh] = zn[hh]

        block(0, True)

        def step(t, carry):
            block(t, False)
            return carry

        lax.fori_loop(1, i + 1, step, 0)
        o_ref[...] = jnp.where(lane < 64, o_acc[0] + _dot(abuf[0], kv(v_ref, 0)),
                               o_acc[1] + _dot(abuf[1], kv(v_ref, 0))).astype(o_ref.dtype)

    return pl.pallas_call(
        body, name="sb_fwd", grid=(SB_HEADS // 2, NQ),
        scratch_shapes=[pltpu.VMEM((2, T, 128), F32), pltpu.VMEM((2, T, 1), F32), pltpu.VMEM((2, T, T), F32),
                        pltpu.VMEM((2, T, T), BF16)],
        in_specs=[pl.BlockSpec((T, 128), lambda h, i: (i, qb + h)),
                  pl.BlockSpec((S, 128), lambda h, i: (0, kb + h)),
                  pl.BlockSpec((S, 128), lambda h, i: (0, vb + h)),
                  pl.BlockSpec((1, T, T), lambda h, i: (0, 0, 0))],
        out_specs=[pl.BlockSpec((T, 128), lambda h, i: (i, h)),
                   pl.BlockSpec((T, 256), lambda h, i: (i, h))],
        out_shape=[jax.ShapeDtypeStruct((S, SB_HEADS * SB_DIM), BF16),
                   jax.ShapeDtypeStruct((S, SB_HEADS * 128), F32)],
        compiler_params=_cparams(("parallel", "arbitrary")),
    )(p, p, p, tri)


def _sb_bwd(p, carries, dy, tri):
    S = p.shape[0]
    T = min(SB_T, S)
    NQ = S // T
    qb, kb, vb = O_SQ // 128, O_SK // 128, O_SV // 128

    def body(q_ref, k_ref, v_ref, cs_ref, dy_ref, tri_ref, dq_ref, dk_ref, dv_ref, dk_acc, dv_acc, dq_acc, run,
             zbuf, dabuf, dzbuf, abuf):
        i = pl.program_id(1)

        @pl.when(i == 0)
        def _():
            dk_acc[...] = jnp.zeros_like(dk_acc)
            dv_acc[...] = jnp.zeros_like(dv_acc)

        lane = lax.broadcasted_iota(jnp.int32, (1, 128), 1)
        tri_after, tri_upto = tri_ref[0], tri_ref[1]
        hms = [(lane >= 64) if hh else (lane < 64) for hh in range(2)]
        qs = [jnp.where(hm, q_ref[...], jnp.zeros_like(q_ref[...])) * jnp.asarray(SB_SCALE, BF16) for hm in hms]
        dos = [jnp.where(hm, dy_ref[...], jnp.zeros_like(dy_ref[...])) for hm in hms]
        qst = [t.T for t in qs]
        dost = [t.T for t in dos]
        dq_acc[...] = jnp.zeros_like(dq_acc)
        run[...] = jnp.zeros_like(run)
        dzbuf[...] = jnp.zeros_like(dzbuf)
        abuf[...] = jnp.zeros_like(abuf)

        def kv(ref, j):
            return ref[pl.ds(pl.multiple_of(j * T, T), T), :]

        def flush(jp):
            kp = kv(k_ref, jp)
            dq_add = [_dot(dzbuf[hh], kp) for hh in range(2)]
            dk_add = _dot(qst[0], dzbuf[0]) + _dot(qst[1], dzbuf[1])
            dv_add = _dot(dost[0], abuf[0]) + _dot(dost[1], abuf[1])
            return dq_add, dk_add, dv_add

        def apply(jp, adds):
            dq_add, dk_add, dv_add = adds
            cols = pl.ds(pl.multiple_of(jp * T, T), T)
            for hh in range(2):
                dq_acc[hh] += dq_add[hh]
            dk_acc[:, cols] += dk_add
            dv_acc[:, cols] += dv_add

        for hh in range(2):
            zbuf[hh] = _dot_nt(qs[hh], kv(k_ref, 0))
            dabuf[hh] = _dot_nt(dos[hh], kv(v_ref, 0))

        def block(j, diagonal):
            jp = jnp.maximum(j - 1, 0)
            if diagonal:
                msk = lax.broadcasted_iota(jnp.int32, (T, T), 1) < lax.broadcasted_iota(jnp.int32, (T, T), 0)
            kp = kv(k_ref, jp)
            dq_add = [_dot(dzbuf[hh], kp) for hh in range(2)]
            sigs, lss, exs, zn, dan, dk_part, dv_part = [], [], [], [], [], [], []
            for hh in range(2):
                z = zbuf[hh]
                sp, _ = _softplus_parts(z)
                lss.append(z - sp)
                sigs.append(jnp.exp(lss[hh]))
                if diagonal:
                    sp = jnp.where(msk, sp, 0.0)
                exs.append(_dot(sp.astype(BF16), tri_after))
                if not diagonal:
                    zn.append(_dot_nt(qs[hh], kv(k_ref, j + 1)))
                dk_part.append(_dot(qst[hh], dzbuf[hh]))
            pgs, gs = [], []
            for hh in range(2):
                csl = slice(hh * 128, (hh + 1) * 128)
                cs = jnp.sum(jnp.where(lane == j, cs_ref[:, csl], 0.0), axis=-1, keepdims=True)
                a = jnp.exp(lss[hh] - exs[hh] - cs)
                if diagonal:
                    a = jnp.where(msk, a, 0.0)
                abuf_new = a.astype(BF16)
                g = a * dabuf[hh]
                gs.append((g, abuf_new))
                pgs.append(_dot(g.astype(BF16), tri_upto))
                if not diagonal:
                    dan.append(_dot_nt(dos[hh], kv(v_ref, j + 1)))
                dv_part.append(_dot(dost[hh], abuf[hh]))
            adds = (dq_add, dk_part[0] + dk_part[1], dv_part[0] + dv_part[1])
            for hh in range(2):
                g, abuf_new = gs[hh]
                cg = run[hh]
                dz = g - sigs[hh] * (cg + pgs[hh])
                if diagonal:
                    dz = jnp.where(msk, dz, 0.0)
                run[hh] = cg + pgs[hh][:, T - 1:T]
                dzbuf[hh] = dz.astype(BF16)
                abuf[hh] = abuf_new
            apply(jp, adds)
            if not diagonal:
                for hh in range(2):
                    zbuf[hh] = zn[hh]
                    dabuf[hh] = dan[hh]

        def step(j, carry):
            block(j, False)
            return carry

        lax.fori_loop(0, i, step, 0)
        block(i, True)
        apply(i, flush(i))
        dq_ref[...] = (jnp.where(lane < 64, dq_acc[0], dq_acc[1]) * SB_SCALE).astype(dq_ref.dtype)

        @pl.when(i == NQ - 1)
        def _():
            dk_ref[...] = dk_acc[...].T.astype(dk_ref.dtype)
            dv_ref[...] = dv_acc[...].T.astype(dv_ref.dtype)

    W = SB_HEADS * SB_DIM
    return pl.pallas_call(
        body, name="sb_bwd", grid=(SB_HEADS // 2, NQ),
        in_specs=[pl.BlockSpec((T, 128), lambda h, i: (i, qb + h)),
                  pl.BlockSpec((S, 128), lambda h, i: (0, kb + h)),
                  pl.BlockSpec((S, 128), lambda h, i: (0, vb + h)),
                  pl.BlockSpec((T, 256), lambda h, i: (i, h)),
                  pl.BlockSpec((T, 128), lambda h, i: (i, h)),
                  pl.BlockSpec((2, T, T), lambda h, i: (0, 0, 0))],
        out_specs=[pl.BlockSpec((T, 128), lambda h, i: (i, h)),
                   pl.BlockSpec((S, 128), lambda h, i: (0, h)),
                   pl.BlockSpec((S, 128), lambda h, i: (0, h))],
        out_shape=[jax.ShapeDtypeStruct((S, W), BF16)] * 3,
        scratch_shapes=[pltpu.VMEM((128, S), F32), pltpu.VMEM((128, S), F32), pltpu.VMEM((2, T, 128), F32),
                        pltpu.VMEM((2, T, 1), F32), pltpu.VMEM((2, T, T), F32), pltpu.VMEM((2, T, T), F32),
                        pltpu.VMEM((2, T, T), BF16), pltpu.VMEM((2, T, T), BF16)],
        compiler_params=_cparams(("parallel", "arbitrary")),
    )(p, p, p, carries, dy, tri)


def _exchange(srcs, out_shapes, src_slice, dst_slice, name, deps=()):
    n, nd = len(srcs), len(deps)

    def body(*refs):
        ins, outs = refs[:n], refs[n + nd:2 * n + nd]
        send_sems, recv_sems, loc_sems = refs[2 * n + nd:]
        x, y, c = lax.axis_index("x"), lax.axis_index("y"), lax.axis_index("c")
        me = 4 * x + 2 * y + c
        local = [pltpu.make_async_copy(src_slice(t, ins[t], me), dst_slice(t, outs[t], me), loc_sems.at[t])
                 for t in range(n)]
        for cp in local:
            cp.start()
        sends, recvs = [], []
        for k in (1, 2, 4, 6, 3, 5, 7):
            px = 1 - x if k & 4 else x
            py = 1 - y if k & 2 else y
            pc = 1 - c if k & 1 else c
            peer = 4 * px + 2 * py + pc
            for t in range(n):
                s = t * 7 + k - 1
                sends.append(pltpu.make_async_remote_copy(
                    src_ref=src_slice(t, ins[t], peer), dst_ref=dst_slice(t, outs[t], me),
                    send_sem=send_sems.at[s], recv_sem=recv_sems.at[s],
                    device_id=(px, py, pc), device_id_type=pl.DeviceIdType.MESH))
                recvs.append(pltpu.make_async_remote_copy(
                    src_ref=src_slice(t, ins[t], me), dst_ref=dst_slice(t, outs[t], peer),
                    send_sem=send_sems.at[s], recv_sem=recv_sems.at[s],
                    device_id=(px, py, pc), device_id_type=pl.DeviceIdType.MESH))
        for cp in sends:
            cp.start()
        for cp in recvs:
            cp.wait_recv()
        for cp in sends:
            cp.wait_send()
        for cp in local:
            cp.wait()

    anyspec = pl.BlockSpec(memory_space=pl.ANY)
    return pl.pallas_call(
        body, name=name, in_specs=[anyspec] * (n + nd), out_specs=[anyspec] * n,
        out_shape=[jax.ShapeDtypeStruct(s, d) for s, d in out_shapes],
        scratch_shapes=[pltpu.SemaphoreType.DMA((7 * n,)), pltpu.SemaphoreType.DMA((7 * n,)),
                        pltpu.SemaphoreType.DMA((n,))],
    )(*srcs, *deps)


def _all_gather_lead(xs, name, deps=()):
    return _exchange(
        xs, [((N_DEV,) + x.shape, x.dtype) for x in xs],
        lambda t, ref, peer: ref, lambda t, ref, who: ref.at[who], name, deps)


def _all_to_all_lead(xs, name):
    return _exchange(
        xs, [(x.shape, x.dtype) for x in xs],
        lambda t, ref, peer: ref.at[peer], lambda t, ref, who: ref.at[who], name)


_W_AXIS = {"w_in": 1, "w_ret_out": 0, "w_sb_out": 0, "w_mix_out": 0, "w_up": 1, "w_down": 0}
_W_NAMES = tuple(_W_AXIS)


def _window(ref, axis, who, width, count=1):
    start = pl.multiple_of(who * width, width)
    return ref.at[pl.ds(start, count * width), :] if axis == 0 else ref.at[:, pl.ds(start, count * width)]


_HBM = pl.BlockSpec(memory_space=pltpu.HBM)
_SEM = pl.BlockSpec(memory_space=pltpu.SEMAPHORE)
_EFFECT = pltpu.SideEffectType.DATAFLOW_SIDE_EFFECTING


_ALL_PEERS = (0, 1, 2, 4, 6, 3, 5, 7)
_SAME_CORE = (0, 2, 4, 6)


def _exchange_start(srcs, shapes, src_slice, dst_slice, name, deps=(), ks=_ALL_PEERS):
    n, nd = len(srcs), len(deps)
    lands = [pltpu.with_memory_space_constraint(lax.empty(s, d), pltpu.HBM) for s, d in shapes]

    def body(*refs):
        ins, lnd = refs[:n], refs[n:2 * n]
        sems = refs[2 * n + nd:4 * n + nd]
        token = refs[6 * n + nd]
        x, y, c = lax.axis_index("x"), lax.axis_index("y"), lax.axis_index("c")
        me = 4 * x + 2 * y + c
        for k in ks:
            px = 1 - x if k & 4 else x
            py = 1 - y if k & 2 else y
            pc = 1 - c if k & 1 else c
            peer = 4 * px + 2 * py + pc
            for t in range(n):
                pltpu.make_async_remote_copy(
                    src_ref=src_slice(t, ins[t], peer), dst_ref=dst_slice(t, lnd[t], me),
                    send_sem=sems[2 * t], recv_sem=sems[2 * t + 1],
                    device_id=(px, py, pc), device_id_type=pl.DeviceIdType.MESH).start()
        token[...] = jnp.zeros_like(token)

    res = pl.pallas_call(
        body, name=name, in_specs=[_HBM] * (2 * n) + [pl.BlockSpec(memory_space=pl.ANY)] * nd,
        out_specs=[_SEM] * (2 * n) + [_HBM] * (2 * n) + [pl.BlockSpec(memory_space=pltpu.VMEM)],
        out_shape=[pltpu.SemaphoreType.DMA(())] * (2 * n) + [pltpu.HBM(s.shape, s.dtype) for s in srcs]
        + [pltpu.HBM(s.shape, s.dtype) for s in lands] + [jax.ShapeDtypeStruct((8, 128), F32)],
        input_output_aliases={t: 2 * n + t for t in range(2 * n)},
        compiler_params=pltpu.CompilerParams(has_side_effects=_EFFECT),
    )(*[pltpu.with_memory_space_constraint(s, pltpu.HBM) for s in srcs], *lands, *deps)
    return dict(n=n, sems=res[:2 * n], srcs=res[2 * n:3 * n], lands=res[3 * n:4 * n], token=res[4 * n])


def _exchange_wait(h, after, name):
    n, ns = h['n'], len(h['srcs'])
    span = h.get('span', lambda t, ref: ref)

    def body(*refs):
        lnd = refs[ns:ns + n]
        sems = refs[ns + n:ns + 3 * n]
        x, y, c = lax.axis_index("x"), lax.axis_index("y"), lax.axis_index("c")
        for t in range(n):
            w = span(t, lnd[t])
            cp = pltpu.make_async_remote_copy(src_ref=w, dst_ref=w, send_sem=sems[2 * t], recv_sem=sems[2 * t + 1],
                                              device_id=(x, y, 1 - c), device_id_type=pl.DeviceIdType.MESH)
            cp.wait_send()
            cp.wait_recv()

    after = list(after)
    res = pl.pallas_call(
        body, name=name,
        in_specs=[_HBM] * (ns + n) + [_SEM] * (2 * n) + [pl.BlockSpec(memory_space=pl.ANY)] * len(after),
        out_specs=[_HBM] * (ns + n),
        out_shape=[pltpu.HBM(s.shape, s.dtype) for s in h['srcs']] + [pltpu.HBM(s.shape, s.dtype) for s in h['lands']],
        input_output_aliases={t: t for t in range(ns + n)},
        compiler_params=pltpu.CompilerParams(has_side_effects=_EFFECT),
    )(*h['srcs'], *h['lands'], *h['sems'], *after)
    return list(res[ns:])


def _sibling_start(lands, win, name):
    n = len(lands)

    def body(*refs):
        lnd = refs[:n]
        sems = refs[n:3 * n]
        token = refs[4 * n]
        x, y, c = lax.axis_index("x"), lax.axis_index("y"), lax.axis_index("c")
        for ox in (x, 1 - x):
            for oy in (y, 1 - y):
                owner = 4 * ox + 2 * oy + c
                for t in range(n):
                    w = win(t, lnd[t], owner)
                    pltpu.make_async_remote_copy(
                        src_ref=w, dst_ref=w, send_sem=sems[2 * t], recv_sem=sems[2 * t + 1],
                        device_id=(x, y, 1 - c), device_id_type=pl.DeviceIdType.MESH).start()
        token[...] = jnp.zeros_like(token)

    res = pl.pallas_call(
        body, name=name, in_specs=[_HBM] * n,
        out_specs=[_SEM] * (2 * n) + [_HBM] * n + [pl.BlockSpec(memory_space=pltpu.VMEM)],
        out_shape=[pltpu.SemaphoreType.DMA(())] * (2 * n) + [pltpu.HBM(s.shape, s.dtype) for s in lands]
        + [jax.ShapeDtypeStruct((8, 128), F32)],
        input_output_aliases={t: 2 * n + t for t in range(n)},
        compiler_params=pltpu.CompilerParams(has_side_effects=_EFFECT),
    )(*lands)
    return dict(n=n, sems=res[:2 * n], srcs=[], lands=res[2 * n:3 * n], token=res[3 * n])


def _gather_start(shards, names, tag, deps=(), two_level=False):
    xs = [shards[nm] for nm in names]
    axes = [_W_AXIS[nm] for nm in names]
    widths = [x.shape[ax] for x, ax in zip(xs, axes)]
    shapes = [(tuple(d * (N_DEV if a == ax else 1) for a, d in enumerate(x.shape)), x.dtype) for x, ax in zip(xs, axes)]
    src = lambda t, ref, peer: ref
    dst = lambda t, ref, who: _window(ref, axes[t], who, widths[t])
    h = _exchange_start(xs, shapes, src, dst, "gw_start_" + tag, deps, _SAME_CORE if two_level else _ALL_PEERS)
    h['tag'] = "gw_wait_" + tag
    if two_level:
        h['span'] = lambda t, ref: _window(ref, axes[t], 0, widths[t], len(_SAME_CORE))
        h['second'] = (dst, "gw_pass_" + tag, "gw_passed_" + tag)
    return h


def _scatter_start(grads, names, tag):
    xs = [grads[nm] for nm in names]
    axes = [_W_AXIS[nm] for nm in names]
    widths = [x.shape[ax] // N_DEV for x, ax in zip(xs, axes)]
    shapes = [((N_DEV,) + tuple(d // (N_DEV if a == ax else 1) for a, d in enumerate(x.shape)), x.dtype)
              for x, ax in zip(xs, axes)]
    src = lambda t, ref, peer: _window(ref, axes[t], peer, widths[t])
    dst = lambda t, ref, who: ref.at[who]
    h = _exchange_start(xs, shapes, src, dst, "sg_start_" + tag)
    h['tag'] = "sg_wait_" + tag
    return h


def _finish(h, after):
    return _exchange_wait(h, after, h['tag'])


class _LayerWeights:
    def __init__(self, groups, started):
        self.groups = groups
        self.started = started
        self.got = {}
        self.after = None

    def __getitem__(self, nm):
        if nm not in self.got:
            for names, h in self.groups:
                if nm in names:
                    self.got.update(zip(names, _finish(h, list(self.after) + self.started)))
        return self.got[nm]


def _adam_math(p_ref, w, m, v):
    g = p_ref[0].astype(F32)
    for s in range(1, p_ref.shape[0]):
        g = g + p_ref[s].astype(F32)
    bc1 = 1.0 / (1.0 - ADAM_B1 ** ADAM_STEP)
    bc2 = 1.0 / (1.0 - ADAM_B2 ** ADAM_STEP)
    mm = ADAM_B1 * m + (1.0 - ADAM_B1) * g
    vv = ADAM_B2 * v + (1.0 - ADAM_B2) * jnp.square(g)
    return g, -ADAM_LR * ((mm * bc1) / (jnp.sqrt(vv * bc2) + ADAM_EPS) + ADAM_WD * w), mm, vv


def _adam(parts, w, m, v, name, tr=256):
    P, R, C = parts.shape
    tr = min(tr, R)
    assert R % tr == 0

    def body(p_ref, w_ref, m_ref, v_ref, *outs):
        for o_ref, val in zip(outs, _adam_math(p_ref, w_ref[...], m_ref[...], v_ref[...])):
            o_ref[...] = val

    spec = pl.BlockSpec((tr, C), lambda i: (i, 0))
    return pl.pallas_call(
        body, name=name, grid=(R // tr,),
        in_specs=[pl.BlockSpec((P, tr, C), lambda i: (0, i, 0)), spec, spec, spec],
        out_specs=[spec] * 4, out_shape=[jax.ShapeDtypeStruct((R, C), F32)] * 4,
        compiler_params=_cparams(("parallel",)),
    )(parts, w, m, v)


def _adam_layer(parts, w, m, v, l, prev, name, tr=256):
    P, R, C = parts.shape
    tr = min(tr, R)
    assert R % tr == 0 and w.shape == (DEPTH, R, C)
    npv = 0 if prev is None else 4

    def body(p_ref, w_ref, m_ref, v_ref, *rest):
        for o_ref, val in zip(rest[npv:], _adam_math(p_ref, w_ref[0], m_ref[0], v_ref[0])):
            o_ref[0] = val

    spec = pl.BlockSpec((1, tr, C), lambda i: (l, i, 0))
    return pl.pallas_call(
        body, name=name, grid=(R // tr,),
        in_specs=[pl.BlockSpec((P, tr, C), lambda i: (0, i, 0)), spec, spec, spec]
        + [pl.BlockSpec(memory_space=pl.ANY)] * npv,
        out_specs=[spec] * 4, out_shape=[jax.ShapeDtypeStruct((DEPTH, R, C), F32)] * 4,
        input_output_aliases={4 + t: t for t in range(npv)},
        compiler_params=_cparams(("parallel",)),
    )(parts, w, m, v, *([] if prev is None else prev))


def _mod_partial(cact_all, w_ada_l, b_ada_l):
    def body(c_ref, w_ref, b_ref, o_ref):
        o_ref[...] = _dot(c_ref[...].astype(BF16), w_ref[...].astype(BF16)) + b_ref[...]

    return pl.pallas_call(
        body, name="mod_partial", out_shape=jax.ShapeDtypeStruct((cact_all.shape[0], w_ada_l.shape[1]), F32),
        compiler_params=pltpu.CompilerParams(vmem_limit_bytes=VMEM_LIMIT),
    )(cact_all, w_ada_l, b_ada_l)


def _ada_grad(cact_t, dmod):
    D, n = cact_t.shape[0], dmod.shape[1]

    def body(c_ref, d_ref, o_ref):
        ct = c_ref[...].astype(BF16).astype(F32)
        dm = d_ref[...].astype(BF16).astype(F32)
        acc = ct[:, 0:1] * dm[0:1, :]
        for b in range(1, N_DEV):
            acc = acc + ct[:, b:b + 1] * dm[b:b + 1, :]
        o_ref[0] = acc

    return pl.pallas_call(
        body, name="ada_grad", out_shape=jax.ShapeDtypeStruct((1, D, n), F32),
        compiler_params=pltpu.CompilerParams(vmem_limit_bytes=VMEM_LIMIT),
    )(cact_t, dmod)


def _norm_mod(x, r, gv, sh):
    return x * r * gv + sh


def _silu(x):
    return x * _sigmoid(x)


def _rstd(x):
    return lax.rsqrt(jnp.mean(x * x, axis=-1, keepdims=True) + EPS)


def _residual_epi(acc, x, g):
    xn = x + g * acc
    return acc, xn, _rstd(xn)


def _residual_norm_epi(acc, x, g, gv, sh):
    xn = x + g * acc
    r = _rstd(xn)
    return acc, xn, _norm_mod(xn, r, gv, sh), r


def _layer_fwd(x0, r1, h1, mod, gn1, gn2, nxt, W, rc, tri):
    S = x0.shape[0]
    sh1, sc1, g1m, sh2, sc2, g2m = [mod[i:i + 1] for i in range(N_MOD)]
    gv1 = gn1 * (1.0 + sc1)
    gv2 = gn2 * (1.0 + sc2)
    W.after = [h1]
    (p,) = _mm(h1, W["w_in"], tm=2048, outs=(BF16,), name="mm_in")
    yret, rstate = _ret_fwd(p, rc)
    ysb, sbc = _sb_fwd(p, tri)
    W.after = [ysb]
    (ya,) = _mm(yret, W["w_ret_out"], tm=512, a_ex=[(p, 'a', O_RG)],
                pro=lambda yr, g: _silu(g.astype(F32)) * yr.astype(F32), outs=(BF16,), name="mm_ret_out")
    yb, mg = _mm(ysb, W["w_sb_out"], o_ex=[(ya, 'o'), (p, 'o', O_GA), (p, 'o', O_GB)],
                 epi=lambda acc, a, ga, gb: (acc, _sigmoid(ga.astype(F32)) * a.astype(F32)
                                             + _sigmoid(gb.astype(F32)) * acc),
                 outs=(BF16, BF16), name="mm_sb_out")
    mo, x1, h2, r2 = _mm(mg, W["w_mix_out"], tm=512, o_ex=[(x0, 'o'), (g1m, 'n'), (gv2, 'n'), (sh2, 'n')],
                         epi=_residual_norm_epi, outs=(BF16, F32, BF16), cols=1, name="mm_mix_out")
    (act,) = _mm(h2, W["w_up"], tm=2048, epi=lambda acc: (jnp.maximum(acc, 0.0),), outs=(BF16,), name="mm_up")
    if nxt is None:
        dn, x2, r_out = _mm(act, W["w_down"], tm=512, pro=lambda a: a * a, o_ex=[(x1, 'o'), (g2m, 'n')],
                            epi=_residual_epi, outs=(BF16, F32), cols=1, name="mm_down_last")
        h_out = None
    else:
        dn, x2, h_out, r_out = _mm(act, W["w_down"], tm=512, pro=lambda a: a * a,
                                   o_ex=[(x1, 'o'), (g2m, 'n'), (nxt[0], 'n'), (nxt[1], 'n')],
                                   epi=_residual_norm_epi, outs=(BF16, F32, BF16), cols=1, name="mm_down")
    saved = dict(x0=x0, r1=r1, h1=h1, p=p, yret=yret, rstate=rstate, ysb=ysb, sbc=sbc, ya=ya, yb=yb, mg=mg, mo=mo, x1=x1,
                 r2=r2, h2=h2, act=act, dn=dn, gv1=gv1, gv2=gv2, mod=mod, gn1=gn1, gn2=gn2)
    return x2, r_out, h_out, saved


def _norm_bwd(dh, x, r, dres, gv, gn, extra_rows=(), extra_vecs=(), extra_fn=None, extra_outs=(), name="norm_bwd"):
    D = x.shape[1]
    ne = len(extra_rows)

    def fn(dh_t, x_t, dres_t, *rest):
        er, rest = rest[:ne], rest[ne:]
        gv_t = rest[0]
        ev, r_t = rest[1:-1], rest[-1]
        xh = x_t * r_t
        dxh = dh_t * gv_t
        dx = r_t * (dxh - xh * jnp.mean(dxh * xh, axis=-1, keepdims=True)) + dres_t
        base = (dx, dh_t, dh_t * xh)
        if extra_fn is None:
            return base
        return base + tuple(extra_fn(dx, *er, *ev))

    return _ew(fn, [dh, x, dres] + list(extra_rows), vecs=[gv] + list(extra_vecs), cols=[r],
               outs=[('row', D, F32), ('sum', D), ('sum', D)] + list(extra_outs), name=name)


def _gate_bwd(dx, dn, g):
    return dx * dn.astype(F32), dx * g


_GATE_OUTS = [('sum', D_MODEL), ('row', D_MODEL, BF16)]


def _layer_bwd(dx2, d_g2m, d_dn, sv, below, W, rc, tri, emit):
    mod = sv['mod']
    sh1, sc1, g1m, sh2, sc2, g2m = [mod[i:i + 1] for i in range(N_MOD)]
    D = D_MODEL
    p = sv['p']
    (d_up,) = _mm(d_dn, W["w_down"], tb=True, tm=2048, o_ex=[(sv['act'], 'o')],
                  epi=lambda acc, a: (acc * 2.0 * a.astype(F32),), outs=(BF16,), name="mm_down_dx")
    (gw_down,) = _mm(sv['act'], d_dn, ta=True, tk=DW_TK, pro=lambda a: a * a, outs=(BF16,), name="mm_down_dw")
    (gw_up,) = _mm(sv['h2'], d_up, ta=True, tk=DW_TK, outs=(BF16,), name="mm_up_dw")
    tok = emit(dict(w_down=gw_down, w_up=gw_up), "mlp")
    (d_h2,) = _mm(d_up, W["w_up"], tb=True, tm=2048, tk=2048, outs=(F32,), name="mm_up_dx", deps=[tok])
    dx1, d_sh2, s_h2, d_g1m, d_mo = _norm_bwd(
        d_h2, sv['x1'], sv['r2'], dx2, sv['gv2'], sv['gn2'],
        extra_rows=[sv['mo']], extra_vecs=[g1m],
        extra_fn=lambda dx, mo, g: (dx * mo.astype(F32), dx * g),
        extra_outs=[('sum', D), ('row', D, BF16)], name="norm_bwd_mlp")
    d_sc2 = sv['gn2'] * s_h2
    d_gn2 = (1.0 + sc2) * s_h2
    def mix_epi(acc, ya, yb, ga, gb):
        sa, sb = _sigmoid(ga.astype(F32)), _sigmoid(gb.astype(F32))
        return (acc * sa, acc * sb, acc * ya.astype(F32) * sa * (1.0 - sa), acc * yb.astype(F32) * sb * (1.0 - sb))

    d_ya, d_yb, d_ga, d_gb = _mm(d_mo, W["w_mix_out"], tb=True, tm=512,
                                 o_ex=[(sv['ya'], 'o'), (sv['yb'], 'o'), (p, 'o', O_GA), (p, 'o', O_GB)], epi=mix_epi,
                                 outs=(BF16,) * 4, name="mm_mix_dx")
    (gw_mix,) = _mm(sv['mg'], d_mo, ta=True, tk=DW_TK, outs=(BF16,), name="mm_mix_dw")

    def ro_epi(acc, g, yr):
        gf = g.astype(F32)
        s = _sigmoid(gf)
        return (acc * yr.astype(F32) * s * (1.0 + gf * (1.0 - s)), acc * gf * s)

    d_rg, d_yret = _mm(d_ya, W["w_ret_out"], tb=True, tm=512, o_ex=[(p, 'o', O_RG), (sv['yret'], 'o')], epi=ro_epi,
                       outs=(BF16, BF16), name="mm_ret_dx")
    (gw_ro,) = _mm(sv['yret'], d_ya, ta=True, tm=512, tk=DW_TK, a_ex=[(p, 'a', O_RG)],
                   pro=lambda yr, g: _silu(g.astype(F32)) * yr.astype(F32), outs=(BF16,), name="mm_ret_dw")
    (gw_so,) = _mm(sv['ysb'], d_yb, ta=True, tk=DW_TK, outs=(BF16,), name="mm_sb_dw")
    tok = emit(dict(w_mix_out=gw_mix, w_ret_out=gw_ro, w_sb_out=gw_so), "mix")
    (d_ysb,) = _mm(d_yb, W["w_sb_out"], tb=True, outs=(BF16,), name="mm_sb_dx", deps=[tok])
    d_sq, d_sk, d_sv = _sb_bwd(p, sv['sbc'], d_ysb, tri)
    d_ret = _ret_bwd(p, sv['rstate'], d_yret, rc)
    dp = [d_ret, d_rg, d_sq, d_sk, d_sv, d_ga, d_gb]
    (gw_in,) = _mm(sv['h1'], dp, ta=True, tk=1024, outs=(BF16,), name="mm_in_dw")
    tok = emit(dict(w_in=gw_in), "in")
    (d_h,) = _mm(dp, W["w_in"], tb=True, tm=2048, tk=512, outs=(F32,), name="mm_in_dx", deps=[tok])
    if below is None:
        dx0, d_sh1, s_h1 = _norm_bwd(d_h, sv['x0'], sv['r1'], dx1, sv['gv1'], sv['gn1'], name="norm_bwd_mix")
        gate_below = (None, None)
    else:
        dx0, d_sh1, s_h1, *gate_below = _norm_bwd(
            d_h, sv['x0'], sv['r1'], dx1, sv['gv1'], sv['gn1'], extra_rows=[below['dn']],
            extra_vecs=[below['mod'][N_MOD - 1:N_MOD]], extra_fn=_gate_bwd, extra_outs=_GATE_OUTS,
            name="norm_bwd_mix_gate")
    d_sc1 = sv['gn1'] * s_h1
    d_gn1 = (1.0 + sc1) * s_h1
    d_mod = jnp.concatenate([d_sh1, d_sc1, d_g1m, d_sh2, d_sc2, d_g2m], axis=1)
    return dx0, gate_below, d_mod, d_gn1, d_gn2


def kernel(x, c, norm_mix_g, w_in, w_ret_out, w_sb_out, w_mix_out, norm_mlp_g, w_up, w_down, w_ada, b_ada, final_g, loss_target, m_norm_mix_g, m_w_in, m_w_ret_out, m_w_sb_out, m_w_mix_out, m_norm_mlp_g, m_w_up, m_w_down, m_w_ada, m_b_ada, m_final_g, v_norm_mix_g, v_w_in, v_w_ret_out, v_w_sb_out, v_w_mix_out, v_norm_mlp_g, v_w_up, v_w_down, v_w_ada, v_b_ada, v_final_g):
    S, D = x.shape[1], x.shape[2]
    x0 = x.reshape(S, D)
    tgt = loss_target.reshape(S, D)
    me = 4 * lax.axis_index("x") + 2 * lax.axis_index("y") + lax.axis_index("c")
    wts = dict(w_in=w_in, w_ret_out=w_ret_out, w_sb_out=w_sb_out, w_mix_out=w_mix_out, w_up=w_up, w_down=w_down)
    mts = dict(w_in=m_w_in, w_ret_out=m_w_ret_out, w_sb_out=m_w_sb_out, w_mix_out=m_w_mix_out, w_up=m_w_up, w_down=m_w_down)
    vts = dict(w_in=v_w_in, w_ret_out=v_w_ret_out, w_sb_out=v_w_sb_out, w_mix_out=v_w_mix_out, w_up=v_w_up, w_down=v_w_down)
    rc = _ret_consts(S)
    tri = _tri()

    (cact,) = _ew(lambda t: (_silu(t),), [jnp.pad(c, ((0, 7), (0, 0)))], outs=[('row', D, F32)], name="silu_c")
    (cact_all,) = _all_gather_lead([cact[0:1]], "gather_c")
    cact_all = cact_all.reshape(N_DEV, D)
    cact16 = jnp.pad(cact_all, ((0, 8), (0, 0)))
    n_ada = w_ada.shape[2]
    b_loc = lax.dynamic_slice_in_dim(b_ada, me * n_ada, n_ada, axis=1)
    mods = [_mod_partial(cact16, w_ada[l], b_loc[l:l + 1])[:N_DEV] for l in range(DEPTH)]
    modp = jnp.stack(mods, axis=1)
    (modr,) = _all_to_all_lead([modp], "scatter_mod")
    mod_full = jnp.transpose(modr, (1, 0, 2)).reshape(DEPTH, N_MOD, D)

    def cast(nm, deps=()):
        w = wts[nm]
        (wb,) = _ew(lambda t: (t,), [w.reshape(-1, w.shape[-1])], outs=[('row', w.shape[-1], BF16)],
                    name="cast_bf16", deps=deps)
        return wb.reshape(w.shape)

    rest = tuple(nm for nm in _W_NAMES if nm != "w_in")
    shards = {"w_in": cast("w_in")}
    h_in = _gather_start({"w_in": shards["w_in"][0]}, ("w_in",), "0_in", [modr], two_level=True)
    in_flight = [h_in['token']]
    for nm in rest:
        shards[nm] = cast(nm, in_flight)
    sh = [{nm: shards[nm][l] for nm in _W_NAMES} for l in range(DEPTH)]

    pre = [(norm_mix_g[l:l + 1] * (1.0 + mod_full[l][1:2]), mod_full[l][0:1]) for l in range(DEPTH)]

    def first(t, gv, sh):
        r = _rstd(t)
        return r, _norm_mod(t, r, gv, sh)

    xs = x0
    rs, hs = _ew(first, [x0], vecs=list(pre[0]), outs=[('col', F32), ('row', D, BF16)], name="row_rstd",
                 deps=in_flight)

    win, pass_name, passed_name = h_in['second']
    before_wait = [hs] + [sh[l][nm] for l in range(DEPTH) for nm in _W_NAMES if (l, nm) != (0, "w_in")]
    h_pass = _sibling_start(_exchange_wait(h_in, before_wait, h_in['tag']), win, pass_name)
    h_pass['span'] = h_in['span']
    started = [h_pass['token']]
    layer_groups = []
    for l, groups in enumerate([[(rest, "0_rest")]] + [[(_W_NAMES, "%d_all" % l)] for l in range(1, DEPTH)]):
        layer_groups.append([])
        for names, tag in groups:
            layer_groups[-1].append((names, _gather_start(sh[l], names, tag, started[-1:])))
            started.append(layer_groups[-1][-1][1]['token'])
    (w_in0,) = _exchange_wait(h_pass, started, passed_name)
    layer_w = [_LayerWeights(g, started) for g in layer_groups]
    layer_w[0].got["w_in"] = w_in0
    saved = []
    for l in range(DEPTH):
        xs, rs, hs, sv = _layer_fwd(xs, rs, hs, mod_full[l], norm_mix_g[l:l + 1], norm_mlp_g[l:l + 1],
                                    pre[l + 1] if l + 1 < DEPTH else None, layer_w[l], rc, tri)
        sv['W'] = layer_w[l]
        saved.append(sv)

    fg = final_g.reshape(1, D)

    def head(xt, tg, dn, g, g2m, r):
        xh = xt * r
        e = xh * g - tg
        dy = e * (1.0 / D)
        dxh = dy * g
        dx = r * (dxh - xh * jnp.mean(dxh * xh, axis=-1, keepdims=True))
        return (dx, dy * xh, 0.5 * e * e * (1.0 / D)) + _gate_bwd(dx, dn, g2m)

    top = saved[DEPTH - 1]
    dxs, d_fg, loss_cols, *gate = _ew(head, [xs, tgt, top['dn']], vecs=[fg, top['mod'][N_MOD - 1:N_MOD]], cols=[rs],
                                      outs=[('row', D, F32), ('sum', D), ('sum', D)] + _GATE_OUTS, name="loss_head")

    small = [None] * DEPTH
    pending = []
    for l in reversed(range(DEPTH)):
        sv = saved[l]

        def emit(gw, tag, l=l):
            names = tuple(gw)
            pending.append((l, names, _scatter_start(gw, names, "%d_%s" % (l, tag))))
            return pending[-1][2]['token']

        dxs, gate, d_mod, d_gn1, d_gn2 = _layer_bwd(dxs, gate[0], gate[1], sv, saved[l - 1] if l else None,
                                                    sv['W'], rc, tri, emit)
        small[l] = (d_mod, d_gn1, d_gn2)
    grad_x = dxs.reshape(1, S, D)

    res = {}
    after = [dxs]
    for l, names, h in pending:
        for nm, landed in zip(names, _finish(h, after)):
            res[nm] = _adam_layer(landed, wts[nm], mts[nm], vts[nm], l, res.get(nm), "adam_layer")
        after = [res[names[-1]][0]]

    pack = jnp.concatenate([small[l][0] for l in range(DEPTH)] + [small[l][1] for l in range(DEPTH)]
                           + [small[l][2] for l in range(DEPTH)] + [d_fg, loss_cols], axis=1)
    (packs,) = _all_gather_lead([pack], "gather_small", deps=after)
    packs = packs.reshape(N_DEV, -1)
    o = 0
    dmod_all = []
    for l in range(DEPTH):
        dmod_all.append(packs[:, o:o + N_MOD * D]); o += N_MOD * D
    gn1_parts = packs[:, o:o + DEPTH * D].reshape(N_DEV, DEPTH, D); o += DEPTH * D
    gn2_parts = packs[:, o:o + DEPTH * D].reshape(N_DEV, DEPTH, D); o += DEPTH * D
    fg_parts = packs[:, o:o + D].reshape(N_DEV, 1, D); o += D
    loss_parts = packs[:, o:o + D]
    (loss_sum,) = _ew(lambda t: (t,), [loss_parts], outs=[('sum', D)], name="loss_sum")
    loss = jnp.sum(loss_sum)

    res["norm_mix_g"] = _adam(gn1_parts, norm_mix_g, m_norm_mix_g, v_norm_mix_g, "adam")
    res["norm_mlp_g"] = _adam(gn2_parts, norm_mlp_g, m_norm_mlp_g, v_norm_mlp_g, "adam")
    fgr = _adam(fg_parts, fg, m_final_g.reshape(1, D), v_final_g.reshape(1, D), "adam")
    res["final_g"] = [t.reshape(D) for t in fgr]
    bparts = jnp.stack(dmod_all, axis=1)
    res["b_ada"] = _adam(bparts, b_ada, m_b_ada, v_b_ada, "adam")
    cact_t = cact_all.T
    for l in range(DEPTH):
        dm_loc = lax.dynamic_slice_in_dim(dmod_all[l], me * n_ada, n_ada, axis=1)
        res["w_ada"] = _adam_layer(_ada_grad(cact_t, dm_loc), w_ada, m_w_ada, v_w_ada, l, res.get("w_ada"),
                                   "adam_layer")

    order = ['norm_mix_g', 'w_in', 'w_ret_out', 'w_sb_out', 'w_mix_out', 'norm_mlp_g', 'w_up', 'w_down', 'w_ada', 'b_ada', 'final_g']
    out = [loss, grad_x]
    for i in range(4):
        out += [res[nm][i] for nm in order]
    return tuple(out)
```

```python
import functools
import math

import jax
import jax.numpy as jnp
import numpy as np
from jax import lax
from jax.experimental import pallas as pl
from jax.experimental.pallas import tpu as pltpu

F32 = jnp.float32
BF16 = jnp.bfloat16

N_DEV = 8
D_MODEL = 1024
DEPTH = 2
RET_HEADS = 4
RET_QK = 256
RET_V = 512
RET_CHUNK = 256
ROPE_BASE = 10000.0
SB_HEADS = 16
SB_DIM = 64
D_FF = 4096
N_MOD = 6
EPS = 1e-6
GN_EPS = 1e-5
O_RQ, O_RK, O_RV, O_RG, O_SQ, O_SK, O_SV, O_GA, O_GB = 0, 1024, 2048, 4096, 6144, 7168, 8192, 9216, 10240
IN_W = 11264

ADAM_LR, ADAM_B1, ADAM_B2, ADAM_EPS, ADAM_WD, ADAM_STEP = 0.001, 0.9, 0.999, 1e-08, 0.01, 10

VMEM_LIMIT = 56 * 1024 * 1024
DW_TK = 2048


def _cparams(sem):
    return pltpu.CompilerParams(dimension_semantics=sem, vmem_limit_bytes=VMEM_LIMIT)


def _mm(a, b, *, ta=False, tb=False, tm=1024, tn=1024, tk=None, a_ex=(), pro=None, o_ex=(), epi=None,
        outs=(F32,), cols=0, name, deps=()):
    a_parts = list(a) if isinstance(a, (list, tuple)) else [a]
    b_parts = list(b) if isinstance(b, (list, tuple)) else [b]
    assert not (ta and len(a_parts) > 1) and not (tb and len(b_parts) > 1)
    if ta:
        K, M = a.shape
    else:
        M, K = a_parts[0].shape[0], sum(t.shape[1] for t in a_parts)
    N = b.shape[0] if tb else sum(t.shape[1] for t in b_parts)
    tm, tn, tk = min(tm, M), min(tn, N), K if tk is None else min(tk, K)
    assert M % tm == 0 and N % tn == 0 and K % tk == 0, (name, M, N, K, tm, tn, tk)
    nk = K // tk
    multi = len(a_parts) > 1 or len(b_parts) > 1
    direct = epi is None and tuple(outs) == (F32,) and cols == 0

    def ranges(parts, t):
        out, o = [], 0
        for arr in parts:
            assert arr.shape[1] % t == 0
            out.append((o, o + arr.shape[1] // t))
            o += arr.shape[1] // t
        return out

    a_rng = ranges(a_parts, tk) if len(a_parts) > 1 else [(0, nk)]
    b_rng = ranges(b_parts, tn) if len(b_parts) > 1 else [(0, N // tn)]
    clip = lambda v, lo, hi: jnp.clip(v - lo, 0, hi - lo - 1)
    mine = lambda v, lo, hi, w: jnp.where((v >= lo) & (v < hi), w, 0)
    in_specs, args = [], []
    for arr, (lo, hi) in zip(a_parts, a_rng):
        in_specs.append(pl.BlockSpec((tk, tm), lambda i, j, k: (k, i)) if ta
                        else pl.BlockSpec((tm, tk), lambda i, j, k, lo=lo, hi=hi: (i, clip(k, lo, hi))))
        args.append(arr)
    for arr, (lo, hi) in zip(b_parts, b_rng):
        in_specs.append(pl.BlockSpec((tn, tk), lambda i, j, k: (j, k)) if tb
                        else pl.BlockSpec((tk, tn), lambda i, j, k, lo=lo, hi=hi: (mine(j, lo, hi, k), clip(j, lo, hi))))
        args.append(arr)
    npa, npb = len(a_parts), len(b_parts)
    for arr, kind, *off in a_ex:
        off = off[0] if off else 0
        if kind == 'a' and ta:
            assert off % tm == 0
            in_specs.append(pl.BlockSpec((tk, tm), lambda i, j, k, o=off // tm: (k, o + i)))
        elif kind == 'a':
            assert off % tk == 0
            in_specs.append(pl.BlockSpec((tm, tk), lambda i, j, k, o=off // tk: (i, o + k)))
        elif kind == 'k':
            in_specs.append(pl.BlockSpec((tk, 1), lambda i, j, k: (k, 0)) if ta
                            else pl.BlockSpec((1, tk), lambda i, j, k: (0, k)))
        else:
            in_specs.append(pl.BlockSpec((1, tm), lambda i, j, k: (0, i)) if ta
                            else pl.BlockSpec((tm, 1), lambda i, j, k: (i, 0)))
        args.append(arr)
    for arr, kind, *off in o_ex:
        off = off[0] if off else 0
        if kind == 'o':
            assert off % tn == 0
            in_specs.append(pl.BlockSpec((tm, tn), lambda i, j, k, o=off // tn: (i, o + j)))
        elif kind == 'n':
            in_specs.append(pl.BlockSpec((1, tn), lambda i, j, k: (0, j)))
        else:
            in_specs.append(pl.BlockSpec((tm, 1), lambda i, j, k: (i, 0)))
        args.append(arr)
    for arr in deps:
        in_specs.append(pl.BlockSpec(memory_space=pl.ANY))
        args.append(arr)
    assert cols == 0 or N == tn
    na, no, nout, nd = len(a_ex), len(o_ex), len(outs) + cols, len(deps)
    dims = (((0 if ta else 1,), (1 if tb else 0,)), ((), ()))

    def body(*refs):
        a_refs, b_refs = refs[:npa], refs[npa:npa + npb]
        n0 = npa + npb
        aex = refs[n0:n0 + na]
        oex = refs[n0 + na:n0 + na + no]
        out_refs = refs[n0 + na + no + nd:n0 + na + no + nd + nout]

        def product(a_ref, b_ref):
            at = a_ref[...]
            if pro is not None:
                at = pro(at, *[r[...] for r in aex])
            return lax.dot_general(at.astype(BF16), b_ref[...].astype(BF16), dims, preferred_element_type=F32)

        def finish(res):
            vals = epi(res, *[r[...] for r in oex]) if epi is not None else (res,)
            for o_ref, v in zip(out_refs, vals):
                o_ref[...] = v.astype(o_ref.dtype)

        if nk == 1 and not multi:
            finish(product(a_refs[0], b_refs[0]))
            return
        acc = out_refs[0] if direct else refs[-1]
        j, k = pl.program_id(1), pl.program_id(2)
        if multi:
            @pl.when(k == 0)
            def _():
                acc[...] = jnp.zeros_like(acc)

            for a_ref, (alo, ahi) in zip(a_refs, a_rng):
                for b_ref, (blo, bhi) in zip(b_refs, b_rng):
                    @pl.when((k >= alo) & (k < ahi) & (j >= blo) & (j < bhi))
                    def _():
                        acc[...] += product(a_ref, b_ref)
        else:
            @pl.when(k == 0)
            def _():
                acc[...] = product(a_refs[0], b_refs[0])

            @pl.when(k > 0)
            def _():
                acc[...] += product(a_refs[0], b_refs[0])

        if not direct:
            @pl.when(k == nk - 1)
            def _():
                finish(acc[...])

    res = pl.pallas_call(
        body, name=name, grid=(M // tm, N // tn, nk), in_specs=in_specs,
        out_specs=[pl.BlockSpec((tm, tn), lambda i, j, k: (i, j)) for _ in outs]
        + [pl.BlockSpec((tm, 1), lambda i, j, k: (i, 0))] * cols,
        out_shape=[jax.ShapeDtypeStruct((M, N), dt) for dt in outs] + [jax.ShapeDtypeStruct((M, 1), F32)] * cols,
        scratch_shapes=[pltpu.VMEM((tm, tn), F32)] if (nk > 1 or multi) and not direct else [],
        compiler_params=_cparams(("parallel", "parallel", "arbitrary")),
    )(*args)
    return res


def _ew(fn, rows, vecs=(), cols=(), outs=(), tr=256, name=None, deps=()):
    S = rows[0].shape[0]
    tr = min(tr, S)
    assert S % tr == 0
    in_specs, args = [], []
    for r in rows:
        in_specs.append(pl.BlockSpec((tr, r.shape[1]), lambda i: (i, 0)))
        args.append(r)
    for v in vecs:
        in_specs.append(pl.BlockSpec((1, v.shape[1]), lambda i: (0, 0)))
        args.append(v)
    for c in cols:
        in_specs.append(pl.BlockSpec((tr, 1), lambda i: (i, 0)))
        args.append(c)
    out_specs, out_shape = [], []
    for o in outs:
        if o[0] == 'row':
            out_specs.append(pl.BlockSpec((tr, o[1]), lambda i: (i, 0)))
            out_shape.append(jax.ShapeDtypeStruct((S, o[1]), o[2]))
        elif o[0] == 'sum':
            out_specs.append(pl.BlockSpec((1, o[1]), lambda i: (0, 0)))
            out_shape.append(jax.ShapeDtypeStruct((1, o[1]), F32))
        else:
            out_specs.append(pl.BlockSpec((tr, 1), lambda i: (i, 0)))
            out_shape.append(jax.ShapeDtypeStruct((S, 1), o[1]))
    nin = len(args)
    in_specs += [pl.BlockSpec(memory_space=pl.ANY)] * len(deps)
    args += list(deps)

    def body(*refs):
        i = pl.program_id(0)
        vals = fn(*[r[...] for r in refs[:nin]])
        for o, o_ref, v in zip(outs, refs[nin + len(deps):], vals):
            if o[0] == 'sum':
                @pl.when(i == 0)
                def _():
                    o_ref[...] = jnp.zeros_like(o_ref)
                o_ref[...] += jnp.sum(v.astype(F32), axis=0, keepdims=True)
            else:
                o_ref[...] = v.astype(o_ref.dtype)

    return pl.pallas_call(
        body, name=name, grid=(S // tr,), in_specs=in_specs, out_specs=out_specs, out_shape=out_shape,
        compiler_params=_cparams(("arbitrary",)),
    )(*args)


def _sigmoid(x):
    return 1.0 / (1.0 + jnp.exp(-x))


def _ret_consts(S):
    h = np.arange(RET_HEADS, dtype=np.float64)
    log_gamma = np.log1p(-np.power(2.0, -5.0 - h))
    idx = np.arange(RET_CHUNK, dtype=np.float64)
    rel = idx[:, None] - idx[None, :]
    decay = np.where(rel >= 0, np.exp(np.maximum(rel, 0.0) * log_gamma[:, None, None]), 0.0)
    xi = np.exp((idx + 1.0) * log_gamma[:, None])[:, :, None]
    zeta = np.exp((RET_CHUNK - 1.0 - idx) * log_gamma[:, None])[:, :, None]
    gamma_c = np.exp(RET_CHUNK * log_gamma)[:, None, None]
    half = RET_QK // 2
    inv_freq = np.power(ROPE_BASE, -np.arange(half, dtype=np.float64) / half).astype(np.float32)
    ang = np.arange(S, dtype=np.float32)[:, None] * inv_freq[None, :]
    f = lambda t: jnp.asarray(t, F32)
    return dict(decay=f(decay), xi=f(xi), zeta=f(zeta), gc=f(gamma_c), cos=f(np.cos(ang)), sin=f(np.sin(ang)))


def _rot(t, cos, sin):
    half = RET_QK // 2
    t1, t2 = t[:, :half], t[:, half:]
    return jnp.concatenate([t1 * cos - t2 * sin, t1 * sin + t2 * cos], axis=-1)


def _rot_inv(t, cos, sin):
    half = RET_QK // 2
    t1, t2 = t[:, :half], t[:, half:]
    return jnp.concatenate([t1 * cos + t2 * sin, t2 * cos - t1 * sin], axis=-1)


_NT = (((1,), (1,)), ((), ()))
_TN = (((0,), (0,)), ((), ()))


def _dot(a, b):
    return jnp.dot(a, b, preferred_element_type=F32)


def _dot_nt(a, b):
    return lax.dot_general(a, b, _NT, preferred_element_type=F32)


def _dot_tn(a, b):
    return lax.dot_general(a, b, _TN, preferred_element_type=F32)


_QW, _VW = RET_HEADS * RET_QK, RET_HEADS * RET_V
_HEADS = range(RET_HEADS)


def _ret_in_specs(C, rev, NC):
    n_of = (lambda n: NC - 1 - n) if rev else (lambda n: n)
    whole3 = lambda n: (0, 0, 0)
    return [
        pl.BlockSpec((C, _QW), lambda n: (n_of(n), O_RQ // _QW)),
        pl.BlockSpec((C, _QW), lambda n: (n_of(n), O_RK // _QW)),
        pl.BlockSpec((C, _VW), lambda n: (n_of(n), O_RV // _VW)),
        pl.BlockSpec((C, RET_QK // 2), lambda n: (n_of(n), 0)),
        pl.BlockSpec((C, RET_QK // 2), lambda n: (n_of(n), 0)),
        pl.BlockSpec((RET_HEADS, C, C), whole3),
        pl.BlockSpec((RET_HEADS, C, 1), whole3),
        pl.BlockSpec((RET_HEADS, C, 1), whole3),
        pl.BlockSpec((RET_HEADS, 1, 1), whole3),
    ]


def _qk_heads(q_ref, k_ref, cos, sin):
    qs, kfs = [], []
    for h in _HEADS:
        cols = slice(h * RET_QK, (h + 1) * RET_QK)
        qs.append(_rot(q_ref[:, cols].astype(F32), cos, sin).astype(BF16))
        kfs.append(_rot(k_ref[:, cols].astype(F32), cos, sin) * (RET_QK ** -0.5))
    return qs, kfs


def _ret_fwd(p, rc):
    S = p.shape[0]
    C = RET_CHUNK
    NC = S // C

    def body(q_ref, k_ref, v_ref, cos_ref, sin_ref, dec_ref, xi_ref, zeta_ref, gc_ref, y_ref, rs_ref, r_acc):
        n = pl.program_id(0)

        @pl.when(n == 0)
        def _():
            r_acc[...] = jnp.zeros_like(r_acc)

        cos, sin = cos_ref[...], sin_ref[...]
        qs, kfs = _qk_heads(q_ref, k_ref, cos, sin)
        vs = [v_ref[:, h * RET_V:(h + 1) * RET_V] for h in _HEADS]
        rbs = [r_acc[h].astype(BF16) for h in _HEADS]
        for h in _HEADS:
            rs_ref[h, 0] = rbs[h]
        ss = [(_dot_nt(qs[h], kfs[h].astype(BF16)) * dec_ref[h]).astype(BF16) for h in _HEADS]
        os = [_dot(ss[h], vs[h]) + _dot(qs[h], rbs[h]) * xi_ref[h] for h in _HEADS]
        for h in _HEADS:
            o = os[h]
            mu = jnp.mean(o, axis=-1, keepdims=True)
            var = jnp.mean(jnp.square(o - mu), axis=-1, keepdims=True)
            y_ref[:, h * RET_V:(h + 1) * RET_V] = ((o - mu) * lax.rsqrt(var + GN_EPS)).astype(y_ref.dtype)
        for h in _HEADS:
            kz = (kfs[h] * zeta_ref[h]).astype(BF16)
            r_acc[h] = r_acc[h] * gc_ref[h] + _dot_tn(kz, vs[h])

    return pl.pallas_call(
        body, name="ret_fwd", grid=(NC,), in_specs=_ret_in_specs(C, False, NC),
        out_specs=[pl.BlockSpec((C, _VW), lambda n: (n, 0)),
                   pl.BlockSpec((RET_HEADS, 1, RET_QK, RET_V), lambda n: (0, n, 0, 0))],
        out_shape=[jax.ShapeDtypeStruct((S, _VW), BF16),
                   jax.ShapeDtypeStruct((RET_HEADS, NC, RET_QK, RET_V), BF16)],
        scratch_shapes=[pltpu.VMEM((RET_HEADS, RET_QK, RET_V), F32)],
        compiler_params=_cparams(("arbitrary",)),
    )(p, p, p, rc['cos'], rc['sin'], rc['decay'], rc['xi'], rc['zeta'], rc['gc'])


def _ret_bwd(p, rstate, dy, rc):
    S = p.shape[0]
    C = RET_CHUNK
    NC = S // C

    def body(q_ref, k_ref, v_ref, cos_ref, sin_ref, dec_ref, xi_ref, zeta_ref, gc_ref, rs_ref, dy_ref,
             d_ref, dr_acc):
        dq_ref, dk_ref, dv_ref = d_ref.at[:, 0:_QW], d_ref.at[:, _QW:2 * _QW], d_ref.at[:, 2 * _QW:2 * _QW + _VW]
        t = pl.program_id(0)

        @pl.when(t == 0)
        def _():
            dr_acc[...] = jnp.zeros_like(dr_acc)

        cos, sin = cos_ref[...], sin_ref[...]
        qs, kfs = _qk_heads(q_ref, k_ref, cos, sin)
        ks = [kf.astype(BF16) for kf in kfs]
        vs = [v_ref[:, h * RET_V:(h + 1) * RET_V] for h in _HEADS]
        rbs = [rs_ref[h, 0] for h in _HEADS]
        ss = [(_dot_nt(qs[h], ks[h]) * dec_ref[h]).astype(BF16) for h in _HEADS]
        os = [_dot(ss[h], vs[h]) + _dot(qs[h], rbs[h]) * xi_ref[h] for h in _HEADS]
        dobs, doxis = [], []
        for h in _HEADS:
            o = os[h]
            mu = jnp.mean(o, axis=-1, keepdims=True)
            var = jnp.mean(jnp.square(o - mu), axis=-1, keepdims=True)
            rstd = lax.rsqrt(var + GN_EPS)
            yh = (o - mu) * rstd
            dyf = dy_ref[:, h * RET_V:(h + 1) * RET_V].astype(F32)
            do = (dyf - jnp.mean(dyf, axis=-1, keepdims=True)
                  - yh * jnp.mean(dyf * yh, axis=-1, keepdims=True)) * rstd
            dobs.append(do.astype(BF16))
            doxis.append((do * xi_ref[h]).astype(BF16))
        drbs = [dr_acc[h].astype(BF16) for h in _HEADS]
        dss = [(_dot_nt(dobs[h], vs[h]) * dec_ref[h]).astype(BF16) for h in _HEADS]
        for h in _HEADS:
            dq = _dot(dss[h], ks[h]) + _dot_nt(doxis[h], rbs[h])
            dq_ref[:, h * RET_QK:(h + 1) * RET_QK] = _rot_inv(dq, cos, sin).astype(dq_ref.dtype)
        for h in _HEADS:
            dk = _dot_tn(dss[h], qs[h]) + _dot_nt(vs[h], drbs[h]) * zeta_ref[h]
            dk_ref[:, h * RET_QK:(h + 1) * RET_QK] = (_rot_inv(dk, cos, sin) * (RET_QK ** -0.5)).astype(dk_ref.dtype)
        for h in _HEADS:
            kz = (kfs[h] * zeta_ref[h]).astype(BF16)
            dv = _dot_tn(ss[h], dobs[h]) + _dot(kz, drbs[h])
            dv_ref[:, h * RET_V:(h + 1) * RET_V] = dv.astype(dv_ref.dtype)
        for h in _HEADS:
            dr_acc[h] = dr_acc[h] * gc_ref[h] + _dot_tn(qs[h], doxis[h])

    rn = lambda n: NC - 1 - n
    in_specs = _ret_in_specs(C, True, NC) + [
        pl.BlockSpec((RET_HEADS, 1, RET_QK, RET_V), lambda n: (0, rn(n), 0, 0)),
        pl.BlockSpec((C, _VW), lambda n: (rn(n), 0)),
    ]
    return pl.pallas_call(
        body, name="ret_bwd", grid=(NC,), in_specs=in_specs,
        out_specs=pl.BlockSpec((C, 2 * _QW + _VW), lambda n: (rn(n), 0)),
        out_shape=jax.ShapeDtypeStruct((S, 2 * _QW + _VW), BF16),
        scratch_shapes=[pltpu.VMEM((RET_HEADS, RET_QK, RET_V), F32)],
        compiler_params=_cparams(("arbitrary",)),
    )(p, p, p, rc['cos'], rc['sin'], rc['decay'], rc['xi'], rc['zeta'], rc['gc'], rstate, dy)


SB_T = 256
SB_SCALE = SB_DIM ** -0.5


def _tri():
    j = np.arange(SB_T)
    after = (j[:, None] > j[None, :]).astype(np.float32)
    upto = (j[:, None] <= j[None, :]).astype(np.float32)
    return jnp.asarray(np.stack([after, upto]), BF16)


def _softplus_parts(z):
    neg_abs = lax.bitcast_convert_type(lax.bitcast_convert_type(z, jnp.uint32) | jnp.uint32(0x80000000), F32)
    e = jnp.exp(neg_abs)
    return jnp.maximum(z, 0.0) + jnp.log(1.0 + e), e


def _sb_fwd(p, tri):
    S = p.shape[0]
    T = min(SB_T, S)
    NQ = S // T
    assert NQ <= 128
    qb, kb, vb = O_SQ // 128, O_SK // 128, O_SV // 128

    def body(q_ref, k_ref, v_ref, tri_ref, o_ref, cs_ref, o_acc, run, zbuf, abuf):
        i = pl.program_id(1)
        lane = lax.broadcasted_iota(jnp.int32, (1, 128), 1)
        tri_after = tri_ref[0]
        qs = [jnp.where((lane >= 64) if hh else (lane < 64), q_ref[...], jnp.zeros_like(q_ref[...]))
              * jnp.asarray(SB_SCALE, BF16) for hh in range(2)]
        cs_ref[...] = jnp.zeros_like(cs_ref)
        o_acc[...] = jnp.zeros_like(o_acc)
        run[...] = jnp.zeros_like(run)

        def kv(ref, j):
            return ref[pl.ds(pl.multiple_of(j * T, T), T), :]

        for hh in range(2):
            zbuf[hh] = _dot_nt(qs[hh], kv(k_ref, i))

        def block(t, diagonal):
            j = i - t
            if diagonal:
                msk = lax.broadcasted_iota(jnp.int32, (T, T), 1) < lax.broadcasted_iota(jnp.int32, (T, T), 0)
            if not diagonal:
                av = [_dot(abuf[hh], kv(v_ref, j + 1)) for hh in range(2)]
            lss, exs, tot, zn = [], [], [], []
            for hh in range(2):
                z = zbuf[hh]
                sp, _ = _softplus_parts(z)
                lss.append(z - sp)
                if diagonal:
                    sp = jnp.where(msk, sp, 0.0)
                exs.append(_dot(sp.astype(BF16), tri_after))
                tot.append(sp[:, 0:1])
                zn.append(_dot_nt(qs[hh], kv(k_ref, jnp.maximum(j - 1, 0))))
            for hh in range(2):
                csl = slice(hh * 128, (hh + 1) * 128)
                cs = run[hh]
                a = jnp.exp(lss[hh] - exs[hh] - cs)
                if diagonal:
                    a = jnp.where(msk, a, 0.0)
                abuf[hh] = a.astype(BF16)
                cs_ref[:, csl] = jnp.where(lane == j, cs, cs_ref[:, csl])
                run[hh] = cs + exs[hh][:, 0:1] + tot[hh]
            for hh in range(2):
                if not diagonal:
                    o_acc[hh] += av[hh]
                zbuf[hh] = zn[hh]

        block(0, True)

        def step(t, carry):
            block(t, False)
            return carry

        lax.fori_loop(1, i + 1, step, 0)
        o_ref[...] = jnp.where(lane < 64, o_acc[0] + _dot(abuf[0], kv(v_ref, 0)),
                               o_acc[1] + _dot(abuf[1], kv(v_ref, 0))).astype(o_ref.dtype)

    return pl.pallas_call(
        body, name="sb_fwd", grid=(SB_HEADS // 2, NQ),
        scratch_shapes=[pltpu.VMEM((2, T, 128), F32), pltpu.VMEM((2, T, 1), F32), pltpu.VMEM((2, T, T), F32),
                        pltpu.VMEM((2, T, T), BF16)],
        in_specs=[pl.BlockSpec((T, 128), lambda h, i: (i, qb + h)),
                  pl.BlockSpec((S, 128), lambda h, i: (0, kb + h)),
                  pl.BlockSpec((S, 128), lambda h, i: (0, vb + h)),
                  pl.BlockSpec((1, T, T), lambda h, i: (0, 0, 0))],
        out_specs=[pl.BlockSpec((T, 128), lambda h, i: (i, h)),
                   pl.BlockSpec((T, 256), lambda h, i: (i, h))],
        out_shape=[jax.ShapeDtypeStruct((S, SB_HEADS * SB_DIM), BF16),
                   jax.ShapeDtypeStruct((S, SB_HEADS * 128), F32)],
        compiler_params=_cparams(("parallel", "arbitrary")),
    )(p, p, p, tri)


def _sb_bwd(p, carries, dy, tri):
    S = p.shape[0]
    T = min(SB_T, S)
    NQ = S // T
    qb, kb, vb = O_SQ // 128, O_SK // 128, O_SV // 128

    def body(q_ref, k_ref, v_ref, cs_ref, dy_ref, tri_ref, dq_ref, dk_ref, dv_ref, dk_acc, dv_acc, dq_acc, run,
             zbuf, dabuf, dzbuf, abuf):
        i = pl.program_id(1)

        @pl.when(i == 0)
        def _():
            dk_acc[...] = jnp.zeros_like(dk_acc)
            dv_acc[...] = jnp.zeros_like(dv_acc)

        lane = lax.broadcasted_iota(jnp.int32, (1, 128), 1)
        tri_after, tri_upto = tri_ref[0], tri_ref[1]
        hms = [(lane >= 64) if hh else (lane < 64) for hh in range(2)]
        qs = [jnp.where(hm, q_ref[...], jnp.zeros_like(q_ref[...])) * jnp.asarray(SB_SCALE, BF16) for hm in hms]
        dos = [jnp.where(hm, dy_ref[...], jnp.zeros_like(dy_ref[...])) for hm in hms]
        qst = [t.T for t in qs]
        dost = [t.T for t in dos]
        dq_acc[...] = jnp.zeros_like(dq_acc)
        run[...] = jnp.zeros_like(run)
        dzbuf[...] = jnp.zeros_like(dzbuf)
        abuf[...] = jnp.zeros_like(abuf)

        def kv(ref, j):
            return ref[pl.ds(pl.multiple_of(j * T, T), T), :]

        def flush(jp):
            kp = kv(k_ref, jp)
            dq_add = [_dot(dzbuf[hh], kp) for hh in range(2)]
            dk_add = _dot(qst[0], dzbuf[0]) + _dot(qst[1], dzbuf[1])
            dv_add = _dot(dost[0], abuf[0]) + _dot(dost[1], abuf[1])
            return dq_add, dk_add, dv_add

        def apply(jp, adds):
            dq_add, dk_add, dv_add = adds
            cols = pl.ds(pl.multiple_of(jp * T, T), T)
            for hh in range(2):
                dq_acc[hh] += dq_add[hh]
            dk_acc[:, cols] += dk_add
            dv_acc[:, cols] += dv_add

        for hh in range(2):
            zbuf[hh] = _dot_nt(qs[hh], kv(k_ref, 0))
            dabuf[hh] = _dot_nt(dos[hh], kv(v_ref, 0))

        def block(j, diagonal):
            jp = jnp.maximum(j - 1, 0)
            if diagonal:
                msk = lax.broadcasted_iota(jnp.int32, (T, T), 1) < lax.broadcasted_iota(jnp.int32, (T, T), 0)
            kp = kv(k_ref, jp)
            dq_add = [_dot(dzbuf[hh], kp) for hh in range(2)]
            sigs, lss, exs, zn, dan, dk_part, dv_part = [], [], [], [], [], [], []
            for hh in range(2):
                z = zbuf[hh]
                sp, _ = _softplus_parts(z)
                lss.append(z - sp)
                sigs.append(jnp.exp(lss[hh]))
                if diagonal:
                    sp = jnp.where(msk, sp, 0.0)
                exs.append(_dot(sp.astype(BF16), tri_after))
                if not diagonal:
                    zn.append(_dot_nt(qs[hh], kv(k_ref, j + 1)))
                dk_part.append(_dot(qst[hh], dzbuf[hh]))
            pgs, gs = [], []
            for hh in range(2):
                csl = slice(hh * 128, (hh + 1) * 128)
                cs = jnp.sum(jnp.where(lane == j, cs_ref[:, csl], 0.0), axis=-1, keepdims=True)
                a = jnp.exp(lss[hh] - exs[hh] - cs)
                if diagonal:
                    a = jnp.where(msk, a, 0.0)
                abuf_new = a.astype(BF16)
                g = a * dabuf[hh]
                gs.append((g, abuf_new))
                pgs.append(_dot(g.astype(BF16), tri_upto))
                if not diagonal:
                    dan.append(_dot_nt(dos[hh], kv(v_ref, j + 1)))
                dv_part.append(_dot(dost[hh], abuf[hh]))
            adds = (dq_add, dk_part[0] + dk_part[1], dv_part[0] + dv_part[1])
            for hh in range(2):
                g, abuf_new = gs[hh]
                cg = run[hh]
                dz = g - sigs[hh] * (cg + pgs[hh])
                if diagonal:
                    dz = jnp.where(msk, dz, 0.0)
                run[hh] = cg + pgs[hh][:, T - 1:T]
                dzbuf[hh] = dz.astype(BF16)
                abuf[hh] = abuf_new
            apply(jp, adds)
            if not diagonal:
                for hh in range(2):
                    zbuf[hh] = zn[hh]
                    dabuf[hh] = dan[hh]

        def step(j, carry):
            block(j, False)
            return carry

        lax.fori_loop(0, i, step, 0)
        block(i, True)
        apply(i, flush(i))
        dq_ref[...] = (jnp.where(lane < 64, dq_acc[0], dq_acc[1]) * SB_SCALE).astype(dq_ref.dtype)

        @pl.when(i == NQ - 1)
        def _():
            dk_ref[...] = dk_acc[...].T.astype(dk_ref.dtype)
            dv_ref[...] = dv_acc[...].T.astype(dv_ref.dtype)

    W = SB_HEADS * SB_DIM
    return pl.pallas_call(
        body, name="sb_bwd", grid=(SB_HEADS // 2, NQ),
        in_specs=[pl.BlockSpec((T, 128), lambda h, i: (i, qb + h)),
                  pl.BlockSpec((S, 128), lambda h, i: (0, kb + h)),
                  pl.BlockSpec((S, 128), lambda h, i: (0, vb + h)),
                  pl.BlockSpec((T, 256), lambda h, i: (i, h)),
                  pl.BlockSpec((T, 128), lambda h, i: (i, h)),
                  pl.BlockSpec((2, T, T), lambda h, i: (0, 0, 0))],
        out_specs=[pl.BlockSpec((T, 128), lambda h, i: (i, h)),
                   pl.BlockSpec((S, 128), lambda h, i: (0, h)),
                   pl.BlockSpec((S, 128), lambda h, i: (0, h))],
        out_shape=[jax.ShapeDtypeStruct((S, W), BF16)] * 3,
        scratch_shapes=[pltpu.VMEM((128, S), F32), pltpu.VMEM((128, S), F32), pltpu.VMEM((2, T, 128), F32),
                        pltpu.VMEM((2, T, 1), F32), pltpu.VMEM((2, T, T), F32), pltpu.VMEM((2, T, T), F32),
                        pltpu.VMEM((2, T, T), BF16), pltpu.VMEM((2, T, T), BF16)],
        compiler_params=_cparams(("parallel", "arbitrary")),
    )(p, p, p, carries, dy, tri)


def _exchange(srcs, out_shapes, src_slice, dst_slice, name, deps=()):
    n, nd = len(srcs), len(deps)

    def body(*refs):
        ins, outs = refs[:n], refs[n + nd:2 * n + nd]
        send_sems, recv_sems, loc_sems = refs[2 * n + nd:]
        x, y, c = lax.axis_index("x"), lax.axis_index("y"), lax.axis_index("c")
        me = 4 * x + 2 * y + c
        local = [pltpu.make_async_copy(src_slice(t, ins[t], me), dst_slice(t, outs[t], me), loc_sems.at[t])
                 for t in range(n)]
        for cp in local:
            cp.start()
        sends, recvs = [], []
        for k in (1, 2, 4, 6, 3, 5, 7):
            px = 1 - x if k & 4 else x
            py = 1 - y if k & 2 else y
            pc = 1 - c if k & 1 else c
            peer = 4 * px + 2 * py + pc
            for t in range(n):
                s = t * 7 + k - 1
                sends.append(pltpu.make_async_remote_copy(
                    src_ref=src_slice(t, ins[t], peer), dst_ref=dst_slice(t, outs[t], me),
                    send_sem=send_sems.at[s], recv_sem=recv_sems.at[s],
                    device_id=(px, py, pc), device_id_type=pl.DeviceIdType.MESH))
                recvs.append(pltpu.make_async_remote_copy(
                    src_ref=src_slice(t, ins[t], me), dst_ref=dst_slice(t, outs[t], peer),
                    send_sem=send_sems.at[s], recv_sem=recv_sems.at[s],
                    device_id=(px, py, pc), device_id_type=pl.DeviceIdType.MESH))
        for cp in sends:
            cp.start()
        for cp in recvs:
            cp.wait_recv()
        for cp in sends:
            cp.wait_send()
        for cp in local:
            cp.wait()

    anyspec = pl.BlockSpec(memory_space=pl.ANY)
    return pl.pallas_call(
        body, name=name, in_specs=[anyspec] * (n + nd), out_specs=[anyspec] * n,
        out_shape=[jax.ShapeDtypeStruct(s, d) for s, d in out_shapes],
        scratch_shapes=[pltpu.SemaphoreType.DMA((7 * n,)), pltpu.SemaphoreType.DMA((7 * n,)),
                        pltpu.SemaphoreType.DMA((n,))],
    )(*srcs, *deps)


def _all_gather_lead(xs, name, deps=()):
    return _exchange(
        xs, [((N_DEV,) + x.shape, x.dtype) for x in xs],
        lambda t, ref, peer: ref, lambda t, ref, who: ref.at[who], name, deps)


def _all_to_all_lead(xs, name):
    return _exchange(
        xs, [(x.shape, x.dtype) for x in xs],
        lambda t, ref, peer: ref.at[peer], lambda t, ref, who: ref.at[who], name)


_W_AXIS = {"w_in": 1, "w_ret_out": 0, "w_sb_out": 0, "w_mix_out": 0, "w_up": 1, "w_down": 0}
_W_NAMES = tuple(_W_AXIS)


def _window(ref, axis, who, width, count=1):
    start = pl.multiple_of(who * width, width)
    return ref.at[pl.ds(start, count * width), :] if axis == 0 else ref.at[:, pl.ds(start, count * width)]


_HBM = pl.BlockSpec(memory_space=pltpu.HBM)
_SEM = pl.BlockSpec(memory_space=pltpu.SEMAPHORE)
_EFFECT = pltpu.SideEffectType.DATAFLOW_SIDE_EFFECTING


_ALL_PEERS = (0, 1, 2, 4, 6, 3, 5, 7)
_SAME_CORE = (0, 2, 4, 6)


def _exchange_start(srcs, shapes, src_slice, dst_slice, name, deps=(), ks=_ALL_PEERS):
    n, nd = len(srcs), len(deps)
    lands = [pltpu.with_memory_space_constraint(lax.empty(s, d), pltpu.HBM) for s, d in shapes]

    def body(*refs):
        ins, lnd = refs[:n], refs[n:2 * n]
        sems = refs[2 * n + nd:4 * n + nd]
        token = refs[6 * n + nd]
        x, y, c = lax.axis_index("x"), lax.axis_index("y"), lax.axis_index("c")
        me = 4 * x + 2 * y + c
        for k in ks:
            px = 1 - x if k & 4 else x
            py = 1 - y if k & 2 else y
            pc = 1 - c if k & 1 else c
            peer = 4 * px + 2 * py + pc
            for t in range(n):
                pltpu.make_async_remote_copy(
                    src_ref=src_slice(t, ins[t], peer), dst_ref=dst_slice(t, lnd[t], me),
                    send_sem=sems[2 * t], recv_sem=sems[2 * t + 1],
                    device_id=(px, py, pc), device_id_type=pl.DeviceIdType.MESH).start()
        token[...] = jnp.zeros_like(token)

    res = pl.pallas_call(
        body, name=name, in_specs=[_HBM] * (2 * n) + [pl.BlockSpec(memory_space=pl.ANY)] * nd,
        out_specs=[_SEM] * (2 * n) + [_HBM] * (2 * n) + [pl.BlockSpec(memory_space=pltpu.VMEM)],
        out_shape=[pltpu.SemaphoreType.DMA(())] * (2 * n) + [pltpu.HBM(s.shape, s.dtype) for s in srcs]
        + [pltpu.HBM(s.shape, s.dtype) for s in lands] + [jax.ShapeDtypeStruct((8, 128), F32)],
        input_output_aliases={t: 2 * n + t for t in range(2 * n)},
        compiler_params=pltpu.CompilerParams(has_side_effects=_EFFECT),
    )(*[pltpu.with_memory_space_constraint(s, pltpu.HBM) for s in srcs], *lands, *deps)
    return dict(n=n, sems=res[:2 * n], srcs=res[2 * n:3 * n], lands=res[3 * n:4 * n], token=res[4 * n])


def _exchange_wait(h, after, name):
    n, ns = h['n'], len(h['srcs'])
    span = h.get('span', lambda t, ref: ref)

    def body(*refs):
        lnd = refs[ns:ns + n]
        sems = refs[ns + n:ns + 3 * n]
        x, y, c = lax.axis_index("x"), lax.axis_index("y"), lax.axis_index("c")
        for t in range(n):
            w = span(t, lnd[t])
            cp = pltpu.make_async_remote_copy(src_ref=w, dst_ref=w, send_sem=sems[2 * t], recv_sem=sems[2 * t + 1],
                                              device_id=(x, y, 1 - c), device_id_type=pl.DeviceIdType.MESH)
            cp.wait_send()
            cp.wait_recv()

    after = list(after)
    res = pl.pallas_call(
        body, name=name,
        in_specs=[_HBM] * (ns + n) + [_SEM] * (2 * n) + [pl.BlockSpec(memory_space=pl.ANY)] * len(after),
        out_specs=[_HBM] * (ns + n),
        out_shape=[pltpu.HBM(s.shape, s.dtype) for s in h['srcs']] + [pltpu.HBM(s.shape, s.dtype) for s in h['lands']],
        input_output_aliases={t: t for t in range(ns + n)},
        compiler_params=pltpu.CompilerParams(has_side_effects=_EFFECT),
    )(*h['srcs'], *h['lands'], *h['sems'], *after)
    return list(res[ns:])


def _sibling_start(lands, win, name):
    n = len(lands)

    def body(*refs):
        lnd = refs[:n]
        sems = refs[n:3 * n]
        token = refs[4 * n]
        x, y, c = lax.axis_index("x"), lax.axis_index("y"), lax.axis_index("c")
        for ox in (x, 1 - x):
            for oy in (y, 1 - y):
                owner = 4 * ox + 2 * oy + c
                for t in range(n):
                    w = win(t, lnd[t], owner)
                    pltpu.make_async_remote_copy(
                        src_ref=w, dst_ref=w, send_sem=sems[2 * t], recv_sem=sems[2 * t + 1],
                        device_id=(x, y, 1 - c), device_id_type=pl.DeviceIdType.MESH).start()
        token[...] = jnp.zeros_like(token)

    res = pl.pallas_call(
        body, name=name, in_specs=[_HBM] * n,
        out_specs=[_SEM] * (2 * n) + [_HBM] * n + [pl.BlockSpec(memory_space=pltpu.VMEM)],
        out_shape=[pltpu.SemaphoreType.DMA(())] * (2 * n) + [pltpu.HBM(s.shape, s.dtype) for s in lands]
        + [jax.ShapeDtypeStruct((8, 128), F32)],
        input_output_aliases={t: 2 * n + t for t in range(n)},
        compiler_params=pltpu.CompilerParams(has_side_effects=_EFFECT),
    )(*lands)
    return dict(n=n, sems=res[:2 * n], srcs=[], lands=res[2 * n:3 * n], token=res[3 * n])


def _gather_start(shards, names, tag, deps=(), two_level=False):
    xs = [shards[nm] for nm in names]
    axes = [_W_AXIS[nm] for nm in names]
    widths = [x.shape[ax] for x, ax in zip(xs, axes)]
    shapes = [(tuple(d * (N_DEV if a == ax else 1) for a, d in enumerate(x.shape)), x.dtype) for x, ax in zip(xs, axes)]
    src = lambda t, ref, peer: ref
    dst = lambda t, ref, who: _window(ref, axes[t], who, widths[t])
    h = _exchange_start(xs, shapes, src, dst, "gw_start_" + tag, deps, _SAME_CORE if two_level else _ALL_PEERS)
    h['tag'] = "gw_wait_" + tag
    if two_level:
        h['span'] = lambda t, ref: _window(ref, axes[t], 0, widths[t], len(_SAME_CORE))
        h['second'] = (dst, "gw_pass_" + tag, "gw_passed_" + tag)
    return h


def _scatter_start(grads, names, tag):
    xs = [grads[nm] for nm in names]
    axes = [_W_AXIS[nm] for nm in names]
    widths = [x.shape[ax] // N_DEV for x, ax in zip(xs, axes)]
    shapes = [((N_DEV,) + tuple(d // (N_DEV if a == ax else 1) for a, d in enumerate(x.shape)), x.dtype)
              for x, ax in zip(xs, axes)]
    src = lambda t, ref, peer: _window(ref, axes[t], peer, widths[t])
    dst = lambda t, ref, who: ref.at[who]
    h = _exchange_start(xs, shapes, src, dst, "sg_start_" + tag)
    h['tag'] = "sg_wait_" + tag
    return h


def _finish(h, after):
    return _exchange_wait(h, after, h['tag'])


class _LayerWeights:
    def __init__(self, groups, started):
        self.groups = groups
        self.started = started
        self.got = {}
        self.after = None

    def __getitem__(self, nm):
        if nm not in self.got:
            for names, h in self.groups:
                if nm in names:
                    self.got.update(zip(names, _finish(h, list(self.after) + self.started)))
        return self.got[nm]


def _adam_math(p_ref, w, m, v):
    g = p_ref[0].astype(F32)
    for s in range(1, p_ref.shape[0]):
        g = g + p_ref[s].astype(F32)
    bc1 = 1.0 / (1.0 - ADAM_B1 ** ADAM_STEP)
    bc2 = 1.0 / (1.0 - ADAM_B2 ** ADAM_STEP)
    mm = ADAM_B1 * m + (1.0 - ADAM_B1) * g
    vv = ADAM_B2 * v + (1.0 - ADAM_B2) * jnp.square(g)
    return g, -ADAM_LR * ((mm * bc1) / (jnp.sqrt(vv * bc2) + ADAM_EPS) + ADAM_WD * w), mm, vv


def _adam(parts, w, m, v, name, tr=256):
    P, R, C = parts.shape
    tr = min(tr, R)
    assert R % tr == 0

    def body(p_ref, w_ref, m_ref, v_ref, *outs):
        for o_ref, val in zip(outs, _adam_math(p_ref, w_ref[...], m_ref[...], v_ref[...])):
            o_ref[...] = val

    spec = pl.BlockSpec((tr, C), lambda i: (i, 0))
    return pl.pallas_call(
        body, name=name, grid=(R // tr,),
        in_specs=[pl.BlockSpec((P, tr, C), lambda i: (0, i, 0)), spec, spec, spec],
        out_specs=[spec] * 4, out_shape=[jax.ShapeDtypeStruct((R, C), F32)] * 4,
        compiler_params=_cparams(("parallel",)),
    )(parts, w, m, v)


def _adam_layer(parts, w, m, v, l, prev, name, tr=256):
    P, R, C = parts.shape
    tr = min(tr, R)
    assert R % tr == 0 and w.shape == (DEPTH, R, C)
    npv = 0 if prev is None else 4

    def body(p_ref, w_ref, m_ref, v_ref, *rest):
        for o_ref, val in zip(rest[npv:], _adam_math(p_ref, w_ref[0], m_ref[0], v_ref[0])):
            o_ref[0] = val

    spec = pl.BlockSpec((1, tr, C), lambda i: (l, i, 0))
    return pl.pallas_call(
        body, name=name, grid=(R // tr,),
        in_specs=[pl.BlockSpec((P, tr, C), lambda i: (0, i, 0)), spec, spec, spec]
        + [pl.BlockSpec(memory_space=pl.ANY)] * npv,
        out_specs=[spec] * 4, out_shape=[jax.ShapeDtypeStruct((DEPTH, R, C), F32)] * 4,
        input_output_aliases={4 + t: t for t in range(npv)},
        compiler_params=_cparams(("parallel",)),
    )(parts, w, m, v, *([] if prev is None else prev))


def _mod_partial(cact_all, w_ada, l, b_ada_l):
    R, (_, D, n) = cact_all.shape[0], w_ada.shape

    def body(c_ref, w_ref, b_ref, o_ref):
        o_ref[...] = _dot(c_ref[...].astype(BF16), w_ref[...].astype(BF16)) + b_ref[...]

    return pl.pallas_call(
        body, name="mod_partial", grid=(1,), out_shape=jax.ShapeDtypeStruct((R, n), F32),
        in_specs=[pl.BlockSpec((R, D), lambda i: (0, 0)), pl.BlockSpec((None, D, n), lambda i: (l, 0, 0)),
                  pl.BlockSpec((1, n), lambda i: (0, 0))],
        out_specs=pl.BlockSpec((R, n), lambda i: (0, 0)),
        compiler_params=pltpu.CompilerParams(vmem_limit_bytes=VMEM_LIMIT),
    )(cact_all, w_ada, b_ada_l)


def _ada_grad(cact_t, dmod):
    D, n = cact_t.shape[0], dmod.shape[1]

    def body(c_ref, d_ref, o_ref):
        ct = c_ref[...].astype(BF16).astype(F32)
        dm = d_ref[...].astype(BF16).astype(F32)
        acc = ct[:, 0:1] * dm[0:1, :]
        for b in range(1, N_DEV):
            acc = acc + ct[:, b:b + 1] * dm[b:b + 1, :]
        o_ref[0] = acc

    return pl.pallas_call(
        body, name="ada_grad", out_shape=jax.ShapeDtypeStruct((1, D, n), F32),
        compiler_params=pltpu.CompilerParams(vmem_limit_bytes=VMEM_LIMIT),
    )(cact_t, dmod)


def _norm_mod(x, r, gv, sh):
    return x * r * gv + sh


def _silu(x):
    return x * _sigmoid(x)


def _rstd(x):
    return lax.rsqrt(jnp.mean(x * x, axis=-1, keepdims=True) + EPS)


def _residual_epi(acc, x, g):
    xn = x + g * acc
    return acc, xn, _rstd(xn)


def _residual_norm_epi(acc, x, g, gv, sh):
    xn = x + g * acc
    r = _rstd(xn)
    return acc, xn, _norm_mod(xn, r, gv, sh), r


def _layer_fwd(x0, r1, h1, mod, gn1, gn2, nxt, W, rc, tri):
    S = x0.shape[0]
    sh1, sc1, g1m, sh2, sc2, g2m = [mod[i:i + 1] for i in range(N_MOD)]
    gv1 = gn1 * (1.0 + sc1)
    gv2 = gn2 * (1.0 + sc2)
    W.after = [h1]
    (p,) = _mm(h1, W["w_in"], tm=2048, outs=(BF16,), name="mm_in")
    yret, rstate = _ret_fwd(p, rc)
    ysb, sbc = _sb_fwd(p, tri)
    W.after = [ysb]
    (ya,) = _mm(yret, W["w_ret_out"], tm=512, a_ex=[(p, 'a', O_RG)],
                pro=lambda yr, g: _silu(g.astype(F32)) * yr.astype(F32), outs=(BF16,), name="mm_ret_out")
    yb, mg = _mm(ysb, W["w_sb_out"], o_ex=[(ya, 'o'), (p, 'o', O_GA), (p, 'o', O_GB)],
                 epi=lambda acc, a, ga, gb: (acc, _sigmoid(ga.astype(F32)) * a.astype(F32)
                                             + _sigmoid(gb.astype(F32)) * acc),
                 outs=(BF16, BF16), name="mm_sb_out")
    mo, x1, h2, r2 = _mm(mg, W["w_mix_out"], tm=512, o_ex=[(x0, 'o'), (g1m, 'n'), (gv2, 'n'), (sh2, 'n')],
                         epi=_residual_norm_epi, outs=(BF16, F32, BF16), cols=1, name="mm_mix_out")
    (act,) = _mm(h2, W["w_up"], tm=2048, epi=lambda acc: (jnp.maximum(acc, 0.0),), outs=(BF16,), name="mm_up")
    if nxt is None:
        dn, x2, r_out = _mm(act, W["w_down"], tm=512, pro=lambda a: a * a, o_ex=[(x1, 'o'), (g2m, 'n')],
                            epi=_residual_epi, outs=(BF16, F32), cols=1, name="mm_down_last")
        h_out = None
    else:
        dn, x2, h_out, r_out = _mm(act, W["w_down"], tm=512, pro=lambda a: a * a,
                                   o_ex=[(x1, 'o'), (g2m, 'n'), (nxt[0], 'n'), (nxt[1], 'n')],
                                   epi=_residual_norm_epi, outs=(BF16, F32, BF16), cols=1, name="mm_down")
    saved = dict(x0=x0, r1=r1, h1=h1, p=p, yret=yret, rstate=rstate, ysb=ysb, sbc=sbc, ya=ya, yb=yb, mg=mg, mo=mo, x1=x1,
                 r2=r2, h2=h2, act=act, dn=dn, gv1=gv1, gv2=gv2, mod=mod, gn1=gn1, gn2=gn2)
    return x2, r_out, h_out, saved


def _norm_bwd(dh, x, r, dres, gv, gn, extra_rows=(), extra_vecs=(), extra_fn=None, extra_outs=(), name="norm_bwd"):
    D = x.shape[1]
    ne = len(extra_rows)

    def fn(dh_t, x_t, dres_t, *rest):
        er, rest = rest[:ne], rest[ne:]
        gv_t = rest[0]
        ev, r_t = rest[1:-1], rest[-1]
        xh = x_t * r_t
        dxh = dh_t * gv_t
        dx = r_t * (dxh - xh * jnp.mean(dxh * xh, axis=-1, keepdims=True)) + dres_t
        base = (dx, dh_t, dh_t * xh)
        if extra_fn is None:
            return base
        return base + tuple(extra_fn(dx, *er, *ev))

    return _ew(fn, [dh, x, dres] + list(extra_rows), vecs=[gv] + list(extra_vecs), cols=[r],
               outs=[('row', D, F32), ('sum', D), ('sum', D)] + list(extra_outs), name=name)


def _gate_bwd(dx, dn, g):
    return dx * dn.astype(F32), dx * g


_GATE_OUTS = [('sum', D_MODEL), ('row', D_MODEL, BF16)]


def _layer_bwd(dx2, d_g2m, d_dn, sv, below, W, rc, tri, emit):
    mod = sv['mod']
    sh1, sc1, g1m, sh2, sc2, g2m = [mod[i:i + 1] for i in range(N_MOD)]
    D = D_MODEL
    p = sv['p']
    (d_up,) = _mm(d_dn, W["w_down"], tb=True, tm=2048, o_ex=[(sv['act'], 'o')],
                  epi=lambda acc, a: (acc * 2.0 * a.astype(F32),), outs=(BF16,), name="mm_down_dx")
    (gw_down,) = _mm(sv['act'], d_dn, ta=True, tk=DW_TK, pro=lambda a: a * a, outs=(BF16,), name="mm_down_dw")
    (gw_up,) = _mm(sv['h2'], d_up, ta=True, tk=DW_TK, outs=(BF16,), name="mm_up_dw")
    tok = emit(dict(w_down=gw_down, w_up=gw_up), "mlp")
    (d_h2,) = _mm(d_up, W["w_up"], tb=True, tm=2048, tk=2048, outs=(F32,), name="mm_up_dx", deps=[tok])
    dx1, d_sh2, s_h2, d_g1m, d_mo = _norm_bwd(
        d_h2, sv['x1'], sv['r2'], dx2, sv['gv2'], sv['gn2'],
        extra_rows=[sv['mo']], extra_vecs=[g1m],
        extra_fn=lambda dx, mo, g: (dx * mo.astype(F32), dx * g),
        extra_outs=[('sum', D), ('row', D, BF16)], name="norm_bwd_mlp")
    d_sc2 = sv['gn2'] * s_h2
    d_gn2 = (1.0 + sc2) * s_h2
    def mix_epi(acc, ya, yb, ga, gb):
        sa, sb = _sigmoid(ga.astype(F32)), _sigmoid(gb.astype(F32))
        return (acc * sa, acc * sb, acc * ya.astype(F32) * sa * (1.0 - sa), acc * yb.astype(F32) * sb * (1.0 - sb))

    d_ya, d_yb, d_ga, d_gb = _mm(d_mo, W["w_mix_out"], tb=True, tm=512,
                                 o_ex=[(sv['ya'], 'o'), (sv['yb'], 'o'), (p, 'o', O_GA), (p, 'o', O_GB)], epi=mix_epi,
                                 outs=(BF16,) * 4, name="mm_mix_dx")
    (gw_mix,) = _mm(sv['mg'], d_mo, ta=True, tk=DW_TK, outs=(BF16,), name="mm_mix_dw")

    def ro_epi(acc, g, yr):
        gf = g.astype(F32)
        s = _sigmoid(gf)
        return (acc * yr.astype(F32) * s * (1.0 + gf * (1.0 - s)), acc * gf * s)

    d_rg, d_yret = _mm(d_ya, W["w_ret_out"], tb=True, tm=512, tn=2048, o_ex=[(p, 'o', O_RG), (sv['yret'], 'o')],
                       epi=ro_epi, outs=(BF16, BF16), name="mm_ret_dx")
    (gw_ro,) = _mm(sv['yret'], d_ya, ta=True, tm=512, tk=DW_TK, a_ex=[(p, 'a', O_RG)],
                   pro=lambda yr, g: _silu(g.astype(F32)) * yr.astype(F32), outs=(BF16,), name="mm_ret_dw")
    (gw_so,) = _mm(sv['ysb'], d_yb, ta=True, tk=DW_TK, outs=(BF16,), name="mm_sb_dw")
    tok = emit(dict(w_mix_out=gw_mix, w_ret_out=gw_ro, w_sb_out=gw_so), "mix")
    (d_ysb,) = _mm(d_yb, W["w_sb_out"], tb=True, outs=(BF16,), name="mm_sb_dx", deps=[tok])
    d_sq, d_sk, d_sv = _sb_bwd(p, sv['sbc'], d_ysb, tri)
    d_ret = _ret_bwd(p, sv['rstate'], d_yret, rc)
    dp = [d_ret, d_rg, d_sq, d_sk, d_sv, d_ga, d_gb]
    (gw_in,) = _mm(sv['h1'], dp, ta=True, tk=1024, outs=(BF16,), name="mm_in_dw")
    tok = emit(dict(w_in=gw_in), "in")
    (d_h,) = _mm(dp, W["w_in"], tb=True, tm=2048, tk=512, outs=(F32,), name="mm_in_dx", deps=[tok])
    if below is None:
        dx0, d_sh1, s_h1 = _norm_bwd(d_h, sv['x0'], sv['r1'], dx1, sv['gv1'], sv['gn1'], name="norm_bwd_mix")
        gate_below = (None, None)
    else:
        dx0, d_sh1, s_h1, *gate_below = _norm_bwd(
            d_h, sv['x0'], sv['r1'], dx1, sv['gv1'], sv['gn1'], extra_rows=[below['dn']],
            extra_vecs=[below['mod'][N_MOD - 1:N_MOD]], extra_fn=_gate_bwd, extra_outs=_GATE_OUTS,
            name="norm_bwd_mix_gate")
    d_sc1 = sv['gn1'] * s_h1
    d_gn1 = (1.0 + sc1) * s_h1
    d_mod = jnp.concatenate([d_sh1, d_sc1, d_g1m, d_sh2, d_sc2, d_g2m], axis=1)
    return dx0, gate_below, d_mod, d_gn1, d_gn2


def kernel(x, c, norm_mix_g, w_in, w_ret_out, w_sb_out, w_mix_out, norm_mlp_g, w_up, w_down, w_ada, b_ada, final_g, loss_target, m_norm_mix_g, m_w_in, m_w_ret_out, m_w_sb_out, m_w_mix_out, m_norm_mlp_g, m_w_up, m_w_down, m_w_ada, m_b_ada, m_final_g, v_norm_mix_g, v_w_in, v_w_ret_out, v_w_sb_out, v_w_mix_out, v_norm_mlp_g, v_w_up, v_w_down, v_w_ada, v_b_ada, v_final_g):
    S, D = x.shape[1], x.shape[2]
    x0 = x.reshape(S, D)
    tgt = loss_target.reshape(S, D)
    me = 4 * lax.axis_index("x") + 2 * lax.axis_index("y") + lax.axis_index("c")
    wts = dict(w_in=w_in, w_ret_out=w_ret_out, w_sb_out=w_sb_out, w_mix_out=w_mix_out, w_up=w_up, w_down=w_down)
    mts = dict(w_in=m_w_in, w_ret_out=m_w_ret_out, w_sb_out=m_w_sb_out, w_mix_out=m_w_mix_out, w_up=m_w_up, w_down=m_w_down)
    vts = dict(w_in=v_w_in, w_ret_out=v_w_ret_out, w_sb_out=v_w_sb_out, w_mix_out=v_w_mix_out, w_up=v_w_up, w_down=v_w_down)
    rc = _ret_consts(S)
    tri = _tri()

    (cact,) = _ew(lambda t: (_silu(t),), [jnp.pad(c, ((0, 7), (0, 0)))], outs=[('row', D, F32)], name="silu_c")
    (cact_all,) = _all_gather_lead([cact[0:1]], "gather_c")
    cact_all = cact_all.reshape(N_DEV, D)
    cact16 = jnp.pad(cact_all, ((0, 8), (0, 0)))
    n_ada = w_ada.shape[2]
    b_loc = lax.dynamic_slice_in_dim(b_ada, me * n_ada, n_ada, axis=1)
    mods = [_mod_partial(cact16, w_ada, l, b_loc[l:l + 1])[:N_DEV] for l in range(DEPTH)]
    modp = jnp.stack(mods, axis=1)
    (modr,) = _all_to_all_lead([modp], "scatter_mod")
    mod_full = jnp.transpose(modr, (1, 0, 2)).reshape(DEPTH, N_MOD, D)

    def cast(nm, deps=()):
        w = wts[nm]
        (wb,) = _ew(lambda t: (t,), [w.reshape(-1, w.shape[-1])], outs=[('row', w.shape[-1], BF16)],
                    name="cast_bf16", deps=deps)
        return wb.reshape(w.shape)

    rest = tuple(nm for nm in _W_NAMES if nm != "w_in")
    shards = {"w_in": cast("w_in")}
    h_in = _gather_start({"w_in": shards["w_in"][0]}, ("w_in",), "0_in", [modr], two_level=True)
    in_flight = [h_in['token']]
    for nm in rest:
        shards[nm] = cast(nm, in_flight)
    sh = [{nm: shards[nm][l] for nm in _W_NAMES} for l in range(DEPTH)]

    pre = [(norm_mix_g[l:l + 1] * (1.0 + mod_full[l][1:2]), mod_full[l][0:1]) for l in range(DEPTH)]

    def first(t, gv, sh):
        r = _rstd(t)
        return r, _norm_mod(t, r, gv, sh)

    xs = x0
    rs, hs = _ew(first, [x0], vecs=list(pre[0]), outs=[('col', F32), ('row', D, BF16)], name="row_rstd",
                 deps=in_flight)

    win, pass_name, passed_name = h_in['second']
    before_wait = [hs] + [sh[l][nm] for l in range(DEPTH) for nm in _W_NAMES if (l, nm) != (0, "w_in")]
    h_pass = _sibling_start(_exchange_wait(h_in, before_wait, h_in['tag']), win, pass_name)
    h_pass['span'] = h_in['span']
    started = [h_pass['token']]
    layer_groups = []
    for l, groups in enumerate([[(rest, "0_rest")]] + [[(_W_NAMES, "%d_all" % l)] for l in range(1, DEPTH)]):
        layer_groups.append([])
        for names, tag in groups:
            layer_groups[-1].append((names, _gather_start(sh[l], names, tag, started[-1:])))
            started.append(layer_groups[-1][-1][1]['token'])
    (w_in0,) = _exchange_wait(h_pass, started, passed_name)
    layer_w = [_LayerWeights(g, started) for g in layer_groups]
    layer_w[0].got["w_in"] = w_in0
    saved = []
    for l in range(DEPTH):
        xs, rs, hs, sv = _layer_fwd(xs, rs, hs, mod_full[l], norm_mix_g[l:l + 1], norm_mlp_g[l:l + 1],
                                    pre[l + 1] if l + 1 < DEPTH else None, layer_w[l], rc, tri)
        sv['W'] = layer_w[l]
        saved.append(sv)

    fg = final_g.reshape(1, D)

    def head(xt, tg, dn, g, g2m, r):
        xh = xt * r
        e = xh * g - tg
        dy = e * (1.0 / D)
        dxh = dy * g
        dx = r * (dxh - xh * jnp.mean(dxh * xh, axis=-1, keepdims=True))
        return (dx, dy * xh, 0.5 * e * e * (1.0 / D)) + _gate_bwd(dx, dn, g2m)

    top = saved[DEPTH - 1]
    dxs, d_fg, loss_cols, *gate = _ew(head, [xs, tgt, top['dn']], vecs=[fg, top['mod'][N_MOD - 1:N_MOD]], cols=[rs],
                                      outs=[('row', D, F32), ('sum', D), ('sum', D)] + _GATE_OUTS, name="loss_head")

    small = [None] * DEPTH
    pending = []
    for l in reversed(range(DEPTH)):
        sv = saved[l]

        def emit(gw, tag, l=l):
            names = tuple(gw)
            pending.append((l, names, _scatter_start(gw, names, "%d_%s" % (l, tag))))
            return pending[-1][2]['token']

        dxs, gate, d_mod, d_gn1, d_gn2 = _layer_bwd(dxs, gate[0], gate[1], sv, saved[l - 1] if l else None,
                                                    sv['W'], rc, tri, emit)
        small[l] = (d_mod, d_gn1, d_gn2)
    grad_x = dxs.reshape(1, S, D)

    res = {}
    after = [dxs]
    for l, names, h in pending:
        for nm, landed in zip(names, _finish(h, after)):
            res[nm] = _adam_layer(landed, wts[nm], mts[nm], vts[nm], l, res.get(nm), "adam_layer")
        after = [res[names[-1]][0]]

    pack = jnp.concatenate([small[l][0] for l in range(DEPTH)] + [small[l][1] for l in range(DEPTH)]
                           + [small[l][2] for l in range(DEPTH)] + [d_fg, loss_cols], axis=1)
    (packs,) = _all_gather_lead([pack], "gather_small", deps=after)
    packs = packs.reshape(N_DEV, -1)
    o = 0
    dmod_all = []
    for l in range(DEPTH):
        dmod_all.append(packs[:, o:o + N_MOD * D]); o += N_MOD * D
    gn1_parts = packs[:, o:o + DEPTH * D].reshape(N_DEV, DEPTH, D); o += DEPTH * D
    gn2_parts = packs[:, o:o + DEPTH * D].reshape(N_DEV, DEPTH, D); o += DEPTH * D
    fg_parts = packs[:, o:o + D].reshape(N_DEV, 1, D); o += D
    loss_parts = packs[:, o:o + D]
    (loss_sum,) = _ew(lambda t: (t,), [loss_parts], outs=[('sum', D)], name="loss_sum")
    loss = jnp.sum(loss_sum)

    res["norm_mix_g"] = _adam(gn1_parts, norm_mix_g, m_norm_mix_g, v_norm_mix_g, "adam")
    res["norm_mlp_g"] = _adam(gn2_parts, norm_mlp_g, m_norm_mlp_g, v_norm_mlp_g, "adam")
    fgr = _adam(fg_parts, fg, m_final_g.reshape(1, D), v_final_g.reshape(1, D), "adam")
    res["final_g"] = [t.reshape(D) for t in fgr]
    bparts = jnp.stack(dmod_all, axis=1)
    res["b_ada"] = _adam(bparts, b_ada, m_b_ada, v_b_ada, "adam")
    cact_t = cact_all.T
    for l in range(DEPTH):
        dm_loc = lax.dynamic_slice_in_dim(dmod_all[l], me * n_ada, n_ada, axis=1)
        res["w_ada"] = _adam_layer(_ada_grad(cact_t, dm_loc), w_ada, m_w_ada, v_w_ada, l, res.get("w_ada"),
                                   "adam_layer")

    order = ['norm_mix_g', 'w_in', 'w_ret_out', 'w_sb_out', 'w_mix_out', 'norm_mlp_g', 'w_up', 'w_down', 'w_ada', 'b_ada', 'final_g']
    out = [loss, grad_x]
    for i in range(4):
        out += [res[nm][i] for nm in order]
    return tuple(out)
```

```python
import functools
import math

import jax
import jax.numpy as jnp
import numpy as np
from jax import lax
from jax.experimental import pallas as pl
from jax.experimental.pallas import tpu as pltpu

F32 = jnp.float32
BF16 = jnp.bfloat16

N_DEV = 8
D_MODEL = 1024
DEPTH = 2
RET_HEADS = 4
RET_QK = 256
RET_V = 512
RET_CHUNK = 256
ROPE_BASE = 10000.0
SB_HEADS = 16
SB_DIM = 64
D_FF = 4096
N_MOD = 6
EPS = 1e-6
GN_EPS = 1e-5
O_RQ, O_RK, O_RV, O_RG, O_SQ, O_SK, O_SV, O_GA, O_GB = 0, 1024, 2048, 4096, 6144, 7168, 8192, 9216, 10240
IN_W = 11264

ADAM_LR, ADAM_B1, ADAM_B2, ADAM_EPS, ADAM_WD, ADAM_STEP = 0.001, 0.9, 0.999, 1e-08, 0.01, 10

VMEM_LIMIT = 56 * 1024 * 1024
DW_TK = 2048


def _cparams(sem):
    return pltpu.CompilerParams(dimension_semantics=sem, vmem_limit_bytes=VMEM_LIMIT)


def _mm(a, b, *, ta=False, tb=False, tm=1024, tn=1024, tk=None, a_ex=(), pro=None, o_ex=(), epi=None,
        outs=(F32,), cols=0, name, deps=()):
    a_parts = list(a) if isinstance(a, (list, tuple)) else [a]
    b_parts = list(b) if isinstance(b, (list, tuple)) else [b]
    assert not (ta and len(a_parts) > 1) and not (tb and len(b_parts) > 1)
    if ta:
        K, M = a.shape
    else:
        M, K = a_parts[0].shape[0], sum(t.shape[1] for t in a_parts)
    N = b.shape[0] if tb else sum(t.shape[1] for t in b_parts)
    tm, tn, tk = min(tm, M), min(tn, N), K if tk is None else min(tk, K)
    assert M % tm == 0 and N % tn == 0 and K % tk == 0, (name, M, N, K, tm, tn, tk)
    nk = K // tk
    multi = len(a_parts) > 1 or len(b_parts) > 1
    direct = epi is None and tuple(outs) == (F32,) and cols == 0

    def ranges(parts, t):
        out, o = [], 0
        for arr in parts:
            assert arr.shape[1] % t == 0
            out.append((o, o + arr.shape[1] // t))
            o += arr.shape[1] // t
        return out

    a_rng = ranges(a_parts, tk) if len(a_parts) > 1 else [(0, nk)]
    b_rng = ranges(b_parts, tn) if len(b_parts) > 1 else [(0, N // tn)]
    clip = lambda v, lo, hi: jnp.clip(v - lo, 0, hi - lo - 1)
    mine = lambda v, lo, hi, w: jnp.where((v >= lo) & (v < hi), w, 0)
    in_specs, args = [], []
    for arr, (lo, hi) in zip(a_parts, a_rng):
        in_specs.append(pl.BlockSpec((tk, tm), lambda i, j, k: (k, i)) if ta
                        else pl.BlockSpec((tm, tk), lambda i, j, k, lo=lo, hi=hi: (i, clip(k, lo, hi))))
        args.append(arr)
    for arr, (lo, hi) in zip(b_parts, b_rng):
        in_specs.append(pl.BlockSpec((tn, tk), lambda i, j, k: (j, k)) if tb
                        else pl.BlockSpec((tk, tn), lambda i, j, k, lo=lo, hi=hi: (mine(j, lo, hi, k), clip(j, lo, hi))))
        args.append(arr)
    npa, npb = len(a_parts), len(b_parts)
    for arr, kind, *off in a_ex:
        off = off[0] if off else 0
        if kind == 'a' and ta:
            assert off % tm == 0
            in_specs.append(pl.BlockSpec((tk, tm), lambda i, j, k, o=off // tm: (k, o + i)))
        elif kind == 'a':
            assert off % tk == 0
            in_specs.append(pl.BlockSpec((tm, tk), lambda i, j, k, o=off // tk: (i, o + k)))
        elif kind == 'k':
            in_specs.append(pl.BlockSpec((tk, 1), lambda i, j, k: (k, 0)) if ta
                            else pl.BlockSpec((1, tk), lambda i, j, k: (0, k)))
        else:
            in_specs.append(pl.BlockSpec((1, tm), lambda i, j, k: (0, i)) if ta
                            else pl.BlockSpec((tm, 1), lambda i, j, k: (i, 0)))
        args.append(arr)
    for arr, kind, *off in o_ex:
        off = off[0] if off else 0
        if kind == 'o':
            assert off % tn == 0
            in_specs.append(pl.BlockSpec((tm, tn), lambda i, j, k, o=off // tn: (i, o + j)))
        elif kind == 'n':
            in_specs.append(pl.BlockSpec((1, tn), lambda i, j, k: (0, j)))
        else:
            in_specs.append(pl.BlockSpec((tm, 1), lambda i, j, k: (i, 0)))
        args.append(arr)
    for arr in deps:
        in_specs.append(pl.BlockSpec(memory_space=pl.ANY))
        args.append(arr)
    assert cols == 0 or N == tn
    na, no, nout, nd = len(a_ex), len(o_ex), len(outs) + cols, len(deps)
    dims = (((0 if ta else 1,), (1 if tb else 0,)), ((), ()))

    def body(*refs):
        a_refs, b_refs = refs[:npa], refs[npa:npa + npb]
        n0 = npa + npb
        aex = refs[n0:n0 + na]
        oex = refs[n0 + na:n0 + na + no]
        out_refs = refs[n0 + na + no + nd:n0 + na + no + nd + nout]

        def product(a_ref, b_ref):
            at = a_ref[...]
            if pro is not None:
                at = pro(at, *[r[...] for r in aex])
            return lax.dot_general(at.astype(BF16), b_ref[...].astype(BF16), dims, preferred_element_type=F32)

        def finish(res):
            vals = epi(res, *[r[...] for r in oex]) if epi is not None else (res,)
            for o_ref, v in zip(out_refs, vals):
                o_ref[...] = v.astype(o_ref.dtype)

        if nk == 1 and not multi:
            finish(product(a_refs[0], b_refs[0]))
            return
        acc = out_refs[0] if direct else refs[-1]
        j, k = pl.program_id(1), pl.program_id(2)
        if multi:
            @pl.when(k == 0)
            def _():
                acc[...] = jnp.zeros_like(acc)

            for a_ref, (alo, ahi) in zip(a_refs, a_rng):
                for b_ref, (blo, bhi) in zip(b_refs, b_rng):
                    @pl.when((k >= alo) & (k < ahi) & (j >= blo) & (j < bhi))
                    def _():
                        acc[...] += product(a_ref, b_ref)
        else:
            @pl.when(k == 0)
            def _():
                acc[...] = product(a_refs[0], b_refs[0])

            @pl.when(k > 0)
            def _():
                acc[...] += product(a_refs[0], b_refs[0])

        if not direct:
            @pl.when(k == nk - 1)
            def _():
                finish(acc[...])

    res = pl.pallas_call(
        body, name=name, grid=(M // tm, N // tn, nk), in_specs=in_specs,
        out_specs=[pl.BlockSpec((tm, tn), lambda i, j, k: (i, j)) for _ in outs]
        + [pl.BlockSpec((tm, 1), lambda i, j, k: (i, 0))] * cols,
        out_shape=[jax.ShapeDtypeStruct((M, N), dt) for dt in outs] + [jax.ShapeDtypeStruct((M, 1), F32)] * cols,
        scratch_shapes=[pltpu.VMEM((tm, tn), F32)] if (nk > 1 or multi) and not direct else [],
        compiler_params=_cparams(("parallel", "parallel", "arbitrary")),
    )(*args)
    return res


def _ew(fn, rows, vecs=(), cols=(), outs=(), tr=256, name=None, deps=()):
    S = rows[0].shape[0]
    tr = min(tr, S)
    assert S % tr == 0
    in_specs, args = [], []
    for r in rows:
        in_specs.append(pl.BlockSpec((tr, r.shape[1]), lambda i: (i, 0)))
        args.append(r)
    for v in vecs:
        in_specs.append(pl.BlockSpec((1, v.shape[1]), lambda i: (0, 0)))
        args.append(v)
    for c in cols:
        in_specs.append(pl.BlockSpec((tr, 1), lambda i: (i, 0)))
        args.append(c)
    out_specs, out_shape = [], []
    for o in outs:
        if o[0] == 'row':
            out_specs.append(pl.BlockSpec((tr, o[1]), lambda i: (i, 0)))
            out_shape.append(jax.ShapeDtypeStruct((S, o[1]), o[2]))
        elif o[0] == 'sum':
            out_specs.append(pl.BlockSpec((1, o[1]), lambda i: (0, 0)))
            out_shape.append(jax.ShapeDtypeStruct((1, o[1]), F32))
        else:
            out_specs.append(pl.BlockSpec((tr, 1), lambda i: (i, 0)))
            out_shape.append(jax.ShapeDtypeStruct((S, 1), o[1]))
    nin = len(args)
    in_specs += [pl.BlockSpec(memory_space=pl.ANY)] * len(deps)
    args += list(deps)

    def body(*refs):
        i = pl.program_id(0)
        vals = fn(*[r[...] for r in refs[:nin]])
        for o, o_ref, v in zip(outs, refs[nin + len(deps):], vals):
            if o[0] == 'sum':
                @pl.when(i == 0)
                def _():
                    o_ref[...] = jnp.zeros_like(o_ref)
                o_ref[...] += jnp.sum(v.astype(F32), axis=0, keepdims=True)
            else:
                o_ref[...] = v.astype(o_ref.dtype)

    return pl.pallas_call(
        body, name=name, grid=(S // tr,), in_specs=in_specs, out_specs=out_specs, out_shape=out_shape,
        compiler_params=_cparams(("arbitrary",)),
    )(*args)


def _sigmoid(x):
    return 1.0 / (1.0 + jnp.exp(-x))


def _ret_consts(S):
    h = np.arange(RET_HEADS, dtype=np.float64)
    log_gamma = np.log1p(-np.power(2.0, -5.0 - h))
    idx = np.arange(RET_CHUNK, dtype=np.float64)
    rel = idx[:, None] - idx[None, :]
    decay = np.where(rel >= 0, np.exp(np.maximum(rel, 0.0) * log_gamma[:, None, None]), 0.0)
    xi = np.exp((idx + 1.0) * log_gamma[:, None])[:, :, None]
    zeta = np.exp((RET_CHUNK - 1.0 - idx) * log_gamma[:, None])[:, :, None]
    gamma_c = np.exp(RET_CHUNK * log_gamma)[:, None, None]
    half = RET_QK // 2
    inv_freq = np.power(ROPE_BASE, -np.arange(half, dtype=np.float64) / half).astype(np.float32)
    ang = np.arange(S, dtype=np.float32)[:, None] * inv_freq[None, :]
    f = lambda t: jnp.asarray(t, F32)
    return dict(decay=f(decay), xi=f(xi), zeta=f(zeta), gc=f(gamma_c), cos=f(np.cos(ang)), sin=f(np.sin(ang)))


def _rot(t, cos, sin):
    half = RET_QK // 2
    t1, t2 = t[:, :half], t[:, half:]
    return jnp.concatenate([t1 * cos - t2 * sin, t1 * sin + t2 * cos], axis=-1)


def _rot_inv(t, cos, sin):
    half = RET_QK // 2
    t1, t2 = t[:, :half], t[:, half:]
    return jnp.concatenate([t1 * cos + t2 * sin, t2 * cos - t1 * sin], axis=-1)


_NT = (((1,), (1,)), ((), ()))
_TN = (((0,), (0,)), ((), ()))


def _dot(a, b):
    return jnp.dot(a, b, preferred_element_type=F32)


def _dot_nt(a, b):
    return lax.dot_general(a, b, _NT, preferred_element_type=F32)


def _dot_tn(a, b):
    return lax.dot_general(a, b, _TN, preferred_element_type=F32)


_QW, _VW = RET_HEADS * RET_QK, RET_HEADS * RET_V
_HEADS = range(RET_HEADS)


def _ret_in_specs(C, rev, NC):
    n_of = (lambda n: NC - 1 - n) if rev else (lambda n: n)
    whole3 = lambda n: (0, 0, 0)
    return [
        pl.BlockSpec((C, _QW), lambda n: (n_of(n), O_RQ // _QW)),
        pl.BlockSpec((C, _QW), lambda n: (n_of(n), O_RK // _QW)),
        pl.BlockSpec((C, _VW), lambda n: (n_of(n), O_RV // _VW)),
        pl.BlockSpec((C, RET_QK // 2), lambda n: (n_of(n), 0)),
        pl.BlockSpec((C, RET_QK // 2), lambda n: (n_of(n), 0)),
        pl.BlockSpec((RET_HEADS, C, C), whole3),
        pl.BlockSpec((RET_HEADS, C, 1), whole3),
        pl.BlockSpec((RET_HEADS, C, 1), whole3),
        pl.BlockSpec((RET_HEADS, 1, 1), whole3),
    ]


def _qk_heads(q_ref, k_ref, cos, sin):
    qs, kfs = [], []
    for h in _HEADS:
        cols = slice(h * RET_QK, (h + 1) * RET_QK)
        qs.append(_rot(q_ref[:, cols].astype(F32), cos, sin).astype(BF16))
        kfs.append(_rot(k_ref[:, cols].astype(F32), cos, sin) * (RET_QK ** -0.5))
    return qs, kfs


def _ret_fwd(p, rc):
    S = p.shape[0]
    C = RET_CHUNK
    NC = S // C

    def body(q_ref, k_ref, v_ref, cos_ref, sin_ref, dec_ref, xi_ref, zeta_ref, gc_ref, y_ref, rs_ref, r_acc):
        n = pl.program_id(0)

        @pl.when(n == 0)
        def _():
            r_acc[...] = jnp.zeros_like(r_acc)

        cos, sin = cos_ref[...], sin_ref[...]
        qs, kfs = _qk_heads(q_ref, k_ref, cos, sin)
        vs = [v_ref[:, h * RET_V:(h + 1) * RET_V] for h in _HEADS]
        rbs = [r_acc[h].astype(BF16) for h in _HEADS]
        for h in _HEADS:
            rs_ref[h, 0] = rbs[h]
        ss = [(_dot_nt(qs[h], kfs[h].astype(BF16)) * dec_ref[h]).astype(BF16) for h in _HEADS]
        os = [_dot(ss[h], vs[h]) + _dot(qs[h], rbs[h]) * xi_ref[h] for h in _HEADS]
        for h in _HEADS:
            o = os[h]
            mu = jnp.mean(o, axis=-1, keepdims=True)
            var = jnp.mean(jnp.square(o - mu), axis=-1, keepdims=True)
            y_ref[:, h * RET_V:(h + 1) * RET_V] = ((o - mu) * lax.rsqrt(var + GN_EPS)).astype(y_ref.dtype)
        for h in _HEADS:
            kz = (kfs[h] * zeta_ref[h]).astype(BF16)
            r_acc[h] = r_acc[h] * gc_ref[h] + _dot_tn(kz, vs[h])

    return pl.pallas_call(
        body, name="ret_fwd", grid=(NC,), in_specs=_ret_in_specs(C, False, NC),
        out_specs=[pl.BlockSpec((C, _VW), lambda n: (n, 0)),
                   pl.BlockSpec((RET_HEADS, 1, RET_QK, RET_V), lambda n: (0, n, 0, 0))],
        out_shape=[jax.ShapeDtypeStruct((S, _VW), BF16),
                   jax.ShapeDtypeStruct((RET_HEADS, NC, RET_QK, RET_V), BF16)],
        scratch_shapes=[pltpu.VMEM((RET_HEADS, RET_QK, RET_V), F32)],
        compiler_params=_cparams(("arbitrary",)),
    )(p, p, p, rc['cos'], rc['sin'], rc['decay'], rc['xi'], rc['zeta'], rc['gc'])


def _ret_bwd(p, rstate, dy, rc):
    S = p.shape[0]
    C = RET_CHUNK
    NC = S // C

    def body(q_ref, k_ref, v_ref, cos_ref, sin_ref, dec_ref, xi_ref, zeta_ref, gc_ref, rs_ref, dy_ref,
             d_ref, dr_acc):
        dq_ref, dk_ref, dv_ref = d_ref.at[:, 0:_QW], d_ref.at[:, _QW:2 * _QW], d_ref.at[:, 2 * _QW:2 * _QW + _VW]
        t = pl.program_id(0)

        @pl.when(t == 0)
        def _():
            dr_acc[...] = jnp.zeros_like(dr_acc)

        cos, sin = cos_ref[...], sin_ref[...]
        qs, kfs = _qk_heads(q_ref, k_ref, cos, sin)
        ks = [kf.astype(BF16) for kf in kfs]
        vs = [v_ref[:, h * RET_V:(h + 1) * RET_V] for h in _HEADS]
        rbs = [rs_ref[h, 0] for h in _HEADS]
        ss = [(_dot_nt(qs[h], ks[h]) * dec_ref[h]).astype(BF16) for h in _HEADS]
        os = [_dot(ss[h], vs[h]) + _dot(qs[h], rbs[h]) * xi_ref[h] for h in _HEADS]
        dobs, doxis = [], []
        for h in _HEADS:
            o = os[h]
            mu = jnp.mean(o, axis=-1, keepdims=True)
            var = jnp.mean(jnp.square(o - mu), axis=-1, keepdims=True)
            rstd = lax.rsqrt(var + GN_EPS)
            yh = (o - mu) * rstd
            dyf = dy_ref[:, h * RET_V:(h + 1) * RET_V].astype(F32)
            do = (dyf - jnp.mean(dyf, axis=-1, keepdims=True)
                  - yh * jnp.mean(dyf * yh, axis=-1, keepdims=True)) * rstd
            dobs.append(do.astype(BF16))
            doxis.append((do * xi_ref[h]).astype(BF16))
        drbs = [dr_acc[h].astype(BF16) for h in _HEADS]
        dss = [(_dot_nt(dobs[h], vs[h]) * dec_ref[h]).astype(BF16) for h in _HEADS]
        for h in _HEADS:
            dq = _dot(dss[h], ks[h]) + _dot_nt(doxis[h], rbs[h])
            dq_ref[:, h * RET_QK:(h + 1) * RET_QK] = _rot_inv(dq, cos, sin).astype(dq_ref.dtype)
        for h in _HEADS:
            dk = _dot_tn(dss[h], qs[h]) + _dot_nt(vs[h], drbs[h]) * zeta_ref[h]
            dk_ref[:, h * RET_QK:(h + 1) * RET_QK] = (_rot_inv(dk, cos, sin) * (RET_QK ** -0.5)).astype(dk_ref.dtype)
        for h in _HEADS:
            kz = (kfs[h] * zeta_ref[h]).astype(BF16)
            dv = _dot_tn(ss[h], dobs[h]) + _dot(kz, drbs[h])
            dv_ref[:, h * RET_V:(h + 1) * RET_V] = dv.astype(dv_ref.dtype)
        for h in _HEADS:
            dr_acc[h] = dr_acc[h] * gc_ref[h] + _dot_tn(qs[h], doxis[h])

    rn = lambda n: NC - 1 - n
    in_specs = _ret_in_specs(C, True, NC) + [
        pl.BlockSpec((RET_HEADS, 1, RET_QK, RET_V), lambda n: (0, rn(n), 0, 0)),
        pl.BlockSpec((C, _VW), lambda n: (rn(n), 0)),
    ]
    return pl.pallas_call(
        body, name="ret_bwd", grid=(NC,), in_specs=in_specs,
        out_specs=pl.BlockSpec((C, 2 * _QW + _VW), lambda n: (rn(n), 0)),
        out_shape=jax.ShapeDtypeStruct((S, 2 * _QW + _VW), BF16),
        scratch_shapes=[pltpu.VMEM((RET_HEADS, RET_QK, RET_V), F32)],
        compiler_params=_cparams(("arbitrary",)),
    )(p, p, p, rc['cos'], rc['sin'], rc['decay'], rc['xi'], rc['zeta'], rc['gc'], rstate, dy)


SB_T = 256
SB_SCALE = SB_DIM ** -0.5


def _tri():
    j = np.arange(SB_T)
    after = (j[:, None] > j[None, :]).astype(np.float32)
    upto = (j[:, None] <= j[None, :]).astype(np.float32)
    return jnp.asarray(np.stack([after, upto]), BF16)


def _softplus_parts(z):
    neg_abs = lax.bitcast_convert_type(lax.bitcast_convert_type(z, jnp.uint32) | jnp.uint32(0x80000000), F32)
    e = jnp.exp(neg_abs)
    return jnp.maximum(z, 0.0) + jnp.log(1.0 + e), e


def _sb_fwd(p, tri):
    S = p.shape[0]
    T = min(SB_T, S)
    NQ = S // T
    assert NQ <= 128
    qb, kb, vb = O_SQ // 128, O_SK // 128, O_SV // 128

    def body(q_ref, k_ref, v_ref, tri_ref, o_ref, cs_ref, o_acc, run, zbuf, abuf):
        i = pl.program_id(1)
        lane = lax.broadcasted_iota(jnp.int32, (1, 128), 1)
        tri_after = tri_ref[0]
        qs = [jnp.where((lane >= 64) if hh else (lane < 64), q_ref[...], jnp.zeros_like(q_ref[...]))
              * jnp.asarray(SB_SCALE, BF16) for hh in range(2)]
        cs_ref[...] = jnp.zeros_like(cs_ref)
        o_acc[...] = jnp.zeros_like(o_acc)
        run[...] = jnp.zeros_like(run)

        def kv(ref, j):
            return ref[pl.ds(pl.multiple_of(j * T, T), T), :]

        for hh in range(2):
            zbuf[hh] = _dot_nt(qs[hh], kv(k_ref, i))

        def block(t, diagonal):
            j = i - t
            if diagonal:
                msk = lax.broadcasted_iota(jnp.int32, (T, T), 1) < lax.broadcasted_iota(jnp.int32, (T, T), 0)
            if not diagonal:
                av = [_dot(abuf[hh], kv(v_ref, j + 1)) for hh in range(2)]
            lss, exs, tot, zn = [], [], [], []
            for hh in range(2):
                z = zbuf[hh]
                sp, _ = _softplus_parts(z)
                lss.append(z - sp)
                if diagonal:
                    sp = jnp.where(msk, sp, 0.0)
                exs.append(_dot(sp.astype(BF16), tri_after))
                tot.append(sp[:, 0:1])
                zn.append(_dot_nt(qs[hh], kv(k_ref, jnp.maximum(j - 1, 0))))
            for hh in range(2):
                csl = slice(hh * 128, (hh + 1) * 128)
                cs = run[hh]
                a = jnp.exp(lss[hh] - exs[hh] - cs)
                if diagonal:
                    a = jnp.where(msk, a, 0.0)
                abuf[hh] = a.astype(BF16)
                cs_ref[:, csl] = jnp.where(lane == j, cs, cs_ref[:, csl])
                run[hh] = cs + exs[hh][:, 0:1] + tot[hh]
            for hh in range(2):
                if not diagonal:
                    o_acc[hh] += av[hh]
                zbuf[hh] = zn[hh]

        block(0, True)

        def step(t, carry):
            block(t, False)
            return carry

        lax.fori_loop(1, i + 1, step, 0)
        o_ref[...] = jnp.where(lane < 64, o_acc[0] + _dot(abuf[0], kv(v_ref, 0)),
                               o_acc[1] + _dot(abuf[1], kv(v_ref, 0))).astype(o_ref.dtype)

    return pl.pallas_call(
        body, name="sb_fwd", grid=(SB_HEADS // 2, NQ),
        scratch_shapes=[pltpu.VMEM((2, T, 128), F32), pltpu.VMEM((2, T, 1), F32), pltpu.VMEM((2, T, T), F32),
                        pltpu.VMEM((2, T, T), BF16)],
        in_specs=[pl.BlockSpec((T, 128), lambda h, i: (i, qb + h)),
                  pl.BlockSpec((S, 128), lambda h, i: (0, kb + h)),
                  pl.BlockSpec((S, 128), lambda h, i: (0, vb + h)),
                  pl.BlockSpec((1, T, T), lambda h, i: (0, 0, 0))],
        out_specs=[pl.BlockSpec((T, 128), lambda h, i: (i, h)),
                   pl.BlockSpec((T, 256), lambda h, i: (i, h))],
        out_shape=[jax.ShapeDtypeStruct((S, SB_HEADS * SB_DIM), BF16),
                   jax.ShapeDtypeStruct((S, SB_HEADS * 128), F32)],
        compiler_params=_cparams(("parallel", "arbitrary")),
    )(p, p, p, tri)


def _sb_bwd(p, carries, dy, tri):
    S = p.shape[0]
    T = min(SB_T, S)
    NQ = S // T
    qb, kb, vb = O_SQ // 128, O_SK // 128, O_SV // 128

    def body(q_ref, k_ref, v_ref, cs_ref, dy_ref, tri_ref, dq_ref, dk_ref, dv_ref, dk_acc, dv_acc, dq_acc, run,
             zbuf, dabuf, dzbuf, abuf):
        i = pl.program_id(1)

        @pl.when(i == 0)
        def _():
            dk_acc[...] = jnp.zeros_like(dk_acc)
            dv_acc[...] = jnp.zeros_like(dv_acc)

        lane = lax.broadcasted_iota(jnp.int32, (1, 128), 1)
        tri_after, tri_upto = tri_ref[0], tri_ref[1]
        hms = [(lane >= 64) if hh else (lane < 64) for hh in range(2)]
        qs = [jnp.where(hm, q_ref[...], jnp.zeros_like(q_ref[...])) * jnp.asarray(SB_SCALE, BF16) for hm in hms]
        dos = [jnp.where(hm, dy_ref[...], jnp.zeros_like(dy_ref[...])) for hm in hms]
        qst = [t.T for t in qs]
        dost = [t.T for t in dos]
        dq_acc[...] = jnp.zeros_like(dq_acc)
        run[...] = jnp.zeros_like(run)
        dzbuf[...] = jnp.zeros_like(dzbuf)
        abuf[...] = jnp.zeros_like(abuf)

        def kv(ref, j):
            return ref[pl.ds(pl.multiple_of(j * T, T), T), :]

        def flush(jp):
            kp = kv(k_ref, jp)
            dq_add = [_dot(dzbuf[hh], kp) for hh in range(2)]
            dk_add = _dot(qst[0], dzbuf[0]) + _dot(qst[1], dzbuf[1])
            dv_add = _dot(dost[0], abuf[0]) + _dot(dost[1], abuf[1])
            return dq_add, dk_add, dv_add

        def apply(jp, adds):
            dq_add, dk_add, dv_add = adds
            cols = pl.ds(pl.multiple_of(jp * T, T), T)
            for hh in range(2):
                dq_acc[hh] += dq_add[hh]
            dk_acc[:, cols] += dk_add
            dv_acc[:, cols] += dv_add

        for hh in range(2):
            zbuf[hh] = _dot_nt(qs[hh], kv(k_ref, 0))
            dabuf[hh] = _dot_nt(dos[hh], kv(v_ref, 0))

        def block(j, diagonal):
            jp = jnp.maximum(j - 1, 0)
            if diagonal:
                msk = lax.broadcasted_iota(jnp.int32, (T, T), 1) < lax.broadcasted_iota(jnp.int32, (T, T), 0)
            kp = kv(k_ref, jp)
            dq_add = [_dot(dzbuf[hh], kp) for hh in range(2)]
            sigs, lss, exs, zn, dan, dk_part, dv_part = [], [], [], [], [], [], []
            for hh in range(2):
                z = zbuf[hh]
                sp, _ = _softplus_parts(z)
                lss.append(z - sp)
                sigs.append(jnp.exp(lss[hh]))
                if diagonal:
                    sp = jnp.where(msk, sp, 0.0)
                exs.append(_dot(sp.astype(BF16), tri_after))
                if not diagonal:
                    zn.append(_dot_nt(qs[hh], kv(k_ref, j + 1)))
                dk_part.append(_dot(qst[hh], dzbuf[hh]))
            pgs, gs = [], []
            for hh in range(2):
                csl = slice(hh * 128, (hh + 1) * 128)
                cs = jnp.sum(jnp.where(lane == j, cs_ref[:, csl], 0.0), axis=-1, keepdims=True)
                a = jnp.exp(lss[hh] - exs[hh] - cs)
                if diagonal:
                    a = jnp.where(msk, a, 0.0)
                abuf_new = a.astype(BF16)
                g = a * dabuf[hh]
                gs.append((g, abuf_new))
                pgs.append(_dot(g.astype(BF16), tri_upto))
                if not diagonal:
                    dan.append(_dot_nt(dos[hh], kv(v_ref, j + 1)))
                dv_part.append(_dot(dost[hh], abuf[hh]))
            adds = (dq_add, dk_part[0] + dk_part[1], dv_part[0] + dv_part[1])
            for hh in range(2):
                g, abuf_new = gs[hh]
                cg = run[hh]
                dz = g - sigs[hh] * (cg + pgs[hh])
                if diagonal:
                    dz = jnp.where(msk, dz, 0.0)
                run[hh] = cg + pgs[hh][:, T - 1:T]
                dzbuf[hh] = dz.astype(BF16)
                abuf[hh] = abuf_new
            apply(jp, adds)
            if not diagonal:
                for hh in range(2):
                    zbuf[hh] = zn[hh]
                    dabuf[hh] = dan[hh]

        def step(j, carry):
            block(j, False)
            return carry

        lax.fori_loop(0, i, step, 0)
        block(i, True)
        apply(i, flush(i))
        dq_ref[...] = (jnp.where(lane < 64, dq_acc[0], dq_acc[1]) * SB_SCALE).astype(dq_ref.dtype)

        @pl.when(i == NQ - 1)
        def _():
            dk_ref[...] = dk_acc[...].T.astype(dk_ref.dtype)
            dv_ref[...] = dv_acc[...].T.astype(dv_ref.dtype)

    W = SB_HEADS * SB_DIM
    return pl.pallas_call(
        body, name="sb_bwd", grid=(SB_HEADS // 2, NQ),
        in_specs=[pl.BlockSpec((T, 128), lambda h, i: (i, qb + h)),
                  pl.BlockSpec((S, 128), lambda h, i: (0, kb + h)),
                  pl.BlockSpec((S, 128), lambda h, i: (0, vb + h)),
                  pl.BlockSpec((T, 256), lambda h, i: (i, h)),
                  pl.BlockSpec((T, 128), lambda h, i: (i, h)),
                  pl.BlockSpec((2, T, T), lambda h, i: (0, 0, 0))],
        out_specs=[pl.BlockSpec((T, 128), lambda h, i: (i, h)),
                   pl.BlockSpec((S, 128), lambda h, i: (0, h)),
                   pl.BlockSpec((S, 128), lambda h, i: (0, h))],
        out_shape=[jax.ShapeDtypeStruct((S, W), BF16)] * 3,
        scratch_shapes=[pltpu.VMEM((128, S), F32), pltpu.VMEM((128, S), F32), pltpu.VMEM((2, T, 128), F32),
                        pltpu.VMEM((2, T, 1), F32), pltpu.VMEM((2, T, T), F32), pltpu.VMEM((2, T, T), F32),
                        pltpu.VMEM((2, T, T), BF16), pltpu.VMEM((2, T, T), BF16)],
        compiler_params=_cparams(("parallel", "arbitrary")),
    )(p, p, p, carries, dy, tri)


def _exchange(srcs, out_shapes, src_slice, dst_slice, name, deps=()):
    n, nd = len(srcs), len(deps)

    def body(*refs):
        ins, outs = refs[:n], refs[n + nd:2 * n + nd]
        send_sems, recv_sems, loc_sems = refs[2 * n + nd:]
        x, y, c = lax.axis_index("x"), lax.axis_index("y"), lax.axis_index("c")
        me = 4 * x + 2 * y + c
        local = [pltpu.make_async_copy(src_slice(t, ins[t], me), dst_slice(t, outs[t], me), loc_sems.at[t])
                 for t in range(n)]
        for cp in local:
            cp.start()
        sends, recvs = [], []
        for k in (1, 2, 4, 6, 3, 5, 7):
            px = 1 - x if k & 4 else x
            py = 1 - y if k & 2 else y
            pc = 1 - c if k & 1 else c
            peer = 4 * px + 2 * py + pc
            for t in range(n):
                s = t * 7 + k - 1
                sends.append(pltpu.make_async_remote_copy(
                    src_ref=src_slice(t, ins[t], peer), dst_ref=dst_slice(t, outs[t], me),
                    send_sem=send_sems.at[s], recv_sem=recv_sems.at[s],
                    device_id=(px, py, pc), device_id_type=pl.DeviceIdType.MESH))
                recvs.append(pltpu.make_async_remote_copy(
                    src_ref=src_slice(t, ins[t], me), dst_ref=dst_slice(t, outs[t], peer),
                    send_sem=send_sems.at[s], recv_sem=recv_sems.at[s],
                    device_id=(px, py, pc), device_id_type=pl.DeviceIdType.MESH))
        for cp in sends:
            cp.start()
        for cp in recvs:
            cp.wait_recv()
        for cp in sends:
            cp.wait_send()
        for cp in local:
            cp.wait()

    anyspec = pl.BlockSpec(memory_space=pl.ANY)
    return pl.pallas_call(
        body, name=name, in_specs=[anyspec] * (n + nd), out_specs=[anyspec] * n,
        out_shape=[jax.ShapeDtypeStruct(s, d) for s, d in out_shapes],
        scratch_shapes=[pltpu.SemaphoreType.DMA((7 * n,)), pltpu.SemaphoreType.DMA((7 * n,)),
                        pltpu.SemaphoreType.DMA((n,))],
    )(*srcs, *deps)


def _all_gather_lead(xs, name, deps=()):
    return _exchange(
        xs, [((N_DEV,) + x.shape, x.dtype) for x in xs],
        lambda t, ref, peer: ref, lambda t, ref, who: ref.at[who], name, deps)


def _all_to_all_lead(xs, name):
    return _exchange(
        xs, [(x.shape, x.dtype) for x in xs],
        lambda t, ref, peer: ref.at[peer], lambda t, ref, who: ref.at[who], name)


_W_AXIS = {"w_in": 1, "w_ret_out": 0, "w_sb_out": 0, "w_mix_out": 0, "w_up": 1, "w_down": 0}
_W_NAMES = tuple(_W_AXIS)


def _window(ref, axis, who, width, count=1):
    start = pl.multiple_of(who * width, width)
    return ref.at[pl.ds(start, count * width), :] if axis == 0 else ref.at[:, pl.ds(start, count * width)]


_HBM = pl.BlockSpec(memory_space=pltpu.HBM)
_SEM = pl.BlockSpec(memory_space=pltpu.SEMAPHORE)
_EFFECT = pltpu.SideEffectType.DATAFLOW_SIDE_EFFECTING


_ALL_PEERS = (0, 1, 2, 4, 6, 3, 5, 7)
_SAME_CORE = (0, 2, 4, 6)


def _exchange_start(srcs, shapes, src_slice, dst_slice, name, deps=(), ks=_ALL_PEERS):
    n, nd = len(srcs), len(deps)
    lands = [pltpu.with_memory_space_constraint(lax.empty(s, d), pltpu.HBM) for s, d in shapes]

    def body(*refs):
        ins, lnd = refs[:n], refs[n:2 * n]
        sems = refs[2 * n + nd:4 * n + nd]
        token = refs[6 * n + nd]
        x, y, c = lax.axis_index("x"), lax.axis_index("y"), lax.axis_index("c")
        me = 4 * x + 2 * y + c
        for k in ks:
            px = 1 - x if k & 4 else x
            py = 1 - y if k & 2 else y
            pc = 1 - c if k & 1 else c
            peer = 4 * px + 2 * py + pc
            for t in range(n):
                pltpu.make_async_remote_copy(
                    src_ref=src_slice(t, ins[t], peer), dst_ref=dst_slice(t, lnd[t], me),
                    send_sem=sems[2 * t], recv_sem=sems[2 * t + 1],
                    device_id=(px, py, pc), device_id_type=pl.DeviceIdType.MESH).start()
        token[...] = jnp.zeros_like(token)

    res = pl.pallas_call(
        body, name=name, in_specs=[_HBM] * (2 * n) + [pl.BlockSpec(memory_space=pl.ANY)] * nd,
        out_specs=[_SEM] * (2 * n) + [_HBM] * (2 * n) + [pl.BlockSpec(memory_space=pltpu.VMEM)],
        out_shape=[pltpu.SemaphoreType.DMA(())] * (2 * n) + [pltpu.HBM(s.shape, s.dtype) for s in srcs]
        + [pltpu.HBM(s.shape, s.dtype) for s in lands] + [jax.ShapeDtypeStruct((8, 128), F32)],
        input_output_aliases={t: 2 * n + t for t in range(2 * n)},
        compiler_params=pltpu.CompilerParams(has_side_effects=_EFFECT),
    )(*[pltpu.with_memory_space_constraint(s, pltpu.HBM) for s in srcs], *lands, *deps)
    return dict(n=n, sems=res[:2 * n], srcs=res[2 * n:3 * n], lands=res[3 * n:4 * n], token=res[4 * n])


def _exchange_wait(h, after, name):
    n, ns = h['n'], len(h['srcs'])
    span = h.get('span', lambda t, ref: ref)

    def body(*refs):
        lnd = refs[ns:ns + n]
        sems = refs[ns + n:ns + 3 * n]
        x, y, c = lax.axis_index("x"), lax.axis_index("y"), lax.axis_index("c")
        for t in range(n):
            w = span(t, lnd[t])
            cp = pltpu.make_async_remote_copy(src_ref=w, dst_ref=w, send_sem=sems[2 * t], recv_sem=sems[2 * t + 1],
                                              device_id=(x, y, 1 - c), device_id_type=pl.DeviceIdType.MESH)
            cp.wait_send()
            cp.wait_recv()

    after = list(after)
    res = pl.pallas_call(
        body, name=name,
        in_specs=[_HBM] * (ns + n) + [_SEM] * (2 * n) + [pl.BlockSpec(memory_space=pl.ANY)] * len(after),
        out_specs=[_HBM] * (ns + n),
        out_shape=[pltpu.HBM(s.shape, s.dtype) for s in h['srcs']] + [pltpu.HBM(s.shape, s.dtype) for s in h['lands']],
        input_output_aliases={t: t for t in range(ns + n)},
        compiler_params=pltpu.CompilerParams(has_side_effects=_EFFECT),
    )(*h['srcs'], *h['lands'], *h['sems'], *after)
    return list(res[ns:])


def _sibling_start(lands, win, name):
    n = len(lands)

    def body(*refs):
        lnd = refs[:n]
        sems = refs[n:3 * n]
        token = refs[4 * n]
        x, y, c = lax.axis_index("x"), lax.axis_index("y"), lax.axis_index("c")
        for ox in (x, 1 - x):
            for oy in (y, 1 - y):
                owner = 4 * ox + 2 * oy + c
                for t in range(n):
                    w = win(t, lnd[t], owner)
                    pltpu.make_async_remote_copy(
                        src_ref=w, dst_ref=w, send_sem=sems[2 * t], recv_sem=sems[2 * t + 1],
                        device_id=(x, y, 1 - c), device_id_type=pl.DeviceIdType.MESH).start()
        token[...] = jnp.zeros_like(token)

    res = pl.pallas_call(
        body, name=name, in_specs=[_HBM] * n,
        out_specs=[_SEM] * (2 * n) + [_HBM] * n + [pl.BlockSpec(memory_space=pltpu.VMEM)],
        out_shape=[pltpu.SemaphoreType.DMA(())] * (2 * n) + [pltpu.HBM(s.shape, s.dtype) for s in lands]
        + [jax.ShapeDtypeStruct((8, 128), F32)],
        input_output_aliases={t: 2 * n + t for t in range(n)},
        compiler_params=pltpu.CompilerParams(has_side_effects=_EFFECT),
    )(*lands)
    return dict(n=n, sems=res[:2 * n], srcs=[], lands=res[2 * n:3 * n], token=res[3 * n])


def _gather_start(shards, names, tag, deps=(), two_level=False):
    xs = [shards[nm] for nm in names]
    axes = [_W_AXIS[nm] for nm in names]
    widths = [x.shape[ax] for x, ax in zip(xs, axes)]
    shapes = [(tuple(d * (N_DEV if a == ax else 1) for a, d in enumerate(x.shape)), x.dtype) for x, ax in zip(xs, axes)]
    src = lambda t, ref, peer: ref
    dst = lambda t, ref, who: _window(ref, axes[t], who, widths[t])
    h = _exchange_start(xs, shapes, src, dst, "gw_start_" + tag, deps, _SAME_CORE if two_level else _ALL_PEERS)
    h['tag'] = "gw_wait_" + tag
    if two_level:
        h['span'] = lambda t, ref: _window(ref, axes[t], 0, widths[t], len(_SAME_CORE))
        h['second'] = (dst, "gw_pass_" + tag, "gw_passed_" + tag)
    return h


def _scatter_start(grads, names, tag):
    xs = [grads[nm] for nm in names]
    axes = [_W_AXIS[nm] for nm in names]
    widths = [x.shape[ax] // N_DEV for x, ax in zip(xs, axes)]
    shapes = [((N_DEV,) + tuple(d // (N_DEV if a == ax else 1) for a, d in enumerate(x.shape)), x.dtype)
              for x, ax in zip(xs, axes)]
    src = lambda t, ref, peer: _window(ref, axes[t], peer, widths[t])
    dst = lambda t, ref, who: ref.at[who]
    h = _exchange_start(xs, shapes, src, dst, "sg_start_" + tag)
    h['tag'] = "sg_wait_" + tag
    return h


def _finish(h, after):
    return _exchange_wait(h, after, h['tag'])


class _LayerWeights:
    def __init__(self, groups, started):
        self.groups = groups
        self.started = started
        self.got = {}
        self.after = None

    def __getitem__(self, nm):
        if nm not in self.got:
            for names, h in self.groups:
                if nm in names:
                    self.got.update(zip(names, _finish(h, list(self.after) + self.started)))
        return self.got[nm]


def _adam_math(p_ref, w, m, v):
    g = p_ref[0].astype(F32)
    for s in range(1, p_ref.shape[0]):
        g = g + p_ref[s].astype(F32)
    bc1 = 1.0 / (1.0 - ADAM_B1 ** ADAM_STEP)
    bc2 = 1.0 / (1.0 - ADAM_B2 ** ADAM_STEP)
    mm = ADAM_B1 * m + (1.0 - ADAM_B1) * g
    vv = ADAM_B2 * v + (1.0 - ADAM_B2) * jnp.square(g)
    return g, -ADAM_LR * ((mm * bc1) / (jnp.sqrt(vv * bc2) + ADAM_EPS) + ADAM_WD * w), mm, vv


def _adam(parts, w, m, v, name, tr=256):
    P, R, C = parts.shape
    tr = min(tr, R)
    assert R % tr == 0

    def body(p_ref, w_ref, m_ref, v_ref, *outs):
        for o_ref, val in zip(outs, _adam_math(p_ref, w_ref[...], m_ref[...], v_ref[...])):
            o_ref[...] = val

    spec = pl.BlockSpec((tr, C), lambda i: (i, 0))
    return pl.pallas_call(
        body, name=name, grid=(R // tr,),
        in_specs=[pl.BlockSpec((P, tr, C), lambda i: (0, i, 0)), spec, spec, spec],
        out_specs=[spec] * 4, out_shape=[jax.ShapeDtypeStruct((R, C), F32)] * 4,
        compiler_params=_cparams(("parallel",)),
    )(parts, w, m, v)


def _adam_layer(parts, w, m, v, l, prev, name, tr=256):
    P, R, C = parts.shape
    tr = min(tr, R)
    assert R % tr == 0 and w.shape == (DEPTH, R, C)
    npv = 0 if prev is None else 4

    def body(p_ref, w_ref, m_ref, v_ref, *rest):
        for o_ref, val in zip(rest[npv:], _adam_math(p_ref, w_ref[0], m_ref[0], v_ref[0])):
            o_ref[0] = val

    spec = pl.BlockSpec((1, tr, C), lambda i: (l, i, 0))
    return pl.pallas_call(
        body, name=name, grid=(R // tr,),
        in_specs=[pl.BlockSpec((P, tr, C), lambda i: (0, i, 0)), spec, spec, spec]
        + [pl.BlockSpec(memory_space=pl.ANY)] * npv,
        out_specs=[spec] * 4, out_shape=[jax.ShapeDtypeStruct((DEPTH, R, C), F32)] * 4,
        input_output_aliases={4 + t: t for t in range(npv)},
        compiler_params=_cparams(("parallel",)),
    )(parts, w, m, v, *([] if prev is None else prev))


def _mod_partial(cact_all, w_ada, l, b_ada_l):
    R, (_, D, n) = cact_all.shape[0], w_ada.shape

    def body(c_ref, w_ref, b_ref, o_ref):
        o_ref[...] = _dot(c_ref[...].astype(BF16), w_ref[...].astype(BF16)) + b_ref[...]

    return pl.pallas_call(
        body, name="mod_partial", grid=(1,), out_shape=jax.ShapeDtypeStruct((R, n), F32),
        in_specs=[pl.BlockSpec((R, D), lambda i: (0, 0)), pl.BlockSpec((None, D, n), lambda i: (l, 0, 0)),
                  pl.BlockSpec((1, n), lambda i: (0, 0))],
        out_specs=pl.BlockSpec((R, n), lambda i: (0, 0)),
        compiler_params=pltpu.CompilerParams(vmem_limit_bytes=VMEM_LIMIT),
    )(cact_all, w_ada, b_ada_l)


def _ada_grad(cact_t, dmod):
    D, n = cact_t.shape[0], dmod.shape[1]

    def body(c_ref, d_ref, o_ref):
        ct = c_ref[...].astype(BF16).astype(F32)
        dm = d_ref[...].astype(BF16).astype(F32)
        acc = ct[:, 0:1] * dm[0:1, :]
        for b in range(1, N_DEV):
            acc = acc + ct[:, b:b + 1] * dm[b:b + 1, :]
        o_ref[0] = acc

    return pl.pallas_call(
        body, name="ada_grad", out_shape=jax.ShapeDtypeStruct((1, D, n), F32),
        compiler_params=pltpu.CompilerParams(vmem_limit_bytes=VMEM_LIMIT),
    )(cact_t, dmod)


def _norm_mod(x, r, gv, sh):
    return x * r * gv + sh


def _silu(x):
    return x * _sigmoid(x)


def _rstd(x):
    return lax.rsqrt(jnp.mean(x * x, axis=-1, keepdims=True) + EPS)


def _residual_epi(acc, x, g):
    xn = x + g * acc
    return acc, xn, _rstd(xn)


def _residual_norm_epi(acc, x, g, gv, sh):
    xn = x + g * acc
    r = _rstd(xn)
    return acc, xn, _norm_mod(xn, r, gv, sh), r


def _layer_fwd(x0, r1, h1, mod, gn1, gn2, nxt, W, rc, tri):
    S = x0.shape[0]
    sh1, sc1, g1m, sh2, sc2, g2m = [mod[i:i + 1] for i in range(N_MOD)]
    gv1 = gn1 * (1.0 + sc1)
    gv2 = gn2 * (1.0 + sc2)
    W.after = [h1]
    (p,) = _mm(h1, W["w_in"], tm=2048, outs=(BF16,), name="mm_in")
    yret, rstate = _ret_fwd(p, rc)
    ysb, sbc = _sb_fwd(p, tri)
    W.after = [ysb]
    (ya,) = _mm(yret, W["w_ret_out"], tm=512, a_ex=[(p, 'a', O_RG)],
                pro=lambda yr, g: _silu(g.astype(F32)) * yr.astype(F32), outs=(BF16,), name="mm_ret_out")
    yb, mg = _mm(ysb, W["w_sb_out"], o_ex=[(ya, 'o'), (p, 'o', O_GA), (p, 'o', O_GB)],
                 epi=lambda acc, a, ga, gb: (acc, _sigmoid(ga.astype(F32)) * a.astype(F32)
                                             + _sigmoid(gb.astype(F32)) * acc),
                 outs=(BF16, BF16), name="mm_sb_out")
    mo, x1, h2, r2 = _mm(mg, W["w_mix_out"], tm=512, o_ex=[(x0, 'o'), (g1m, 'n'), (gv2, 'n'), (sh2, 'n')],
                         epi=_residual_norm_epi, outs=(BF16, F32, BF16), cols=1, name="mm_mix_out")
    (act,) = _mm(h2, W["w_up"], tm=2048, epi=lambda acc: (jnp.maximum(acc, 0.0),), outs=(BF16,), name="mm_up")
    if nxt is None:
        dn, x2, r_out = _mm(act, W["w_down"], tm=512, pro=lambda a: a * a, o_ex=[(x1, 'o'), (g2m, 'n')],
                            epi=_residual_epi, outs=(BF16, F32), cols=1, name="mm_down_last")
        h_out = None
    else:
        dn, x2, h_out, r_out = _mm(act, W["w_down"], tm=512, pro=lambda a: a * a,
                                   o_ex=[(x1, 'o'), (g2m, 'n'), (nxt[0], 'n'), (nxt[1], 'n')],
                                   epi=_residual_norm_epi, outs=(BF16, F32, BF16), cols=1, name="mm_down")
    saved = dict(x0=x0, r1=r1, h1=h1, p=p, yret=yret, rstate=rstate, ysb=ysb, sbc=sbc, ya=ya, yb=yb, mg=mg, mo=mo, x1=x1,
                 r2=r2, h2=h2, act=act, dn=dn, gv1=gv1, gv2=gv2, mod=mod, gn1=gn1, gn2=gn2)
    return x2, r_out, h_out, saved


def _norm_bwd(dh, x, r, dres, gv, gn, extra_rows=(), extra_vecs=(), extra_fn=None, extra_outs=(), name="norm_bwd"):
    D = x.shape[1]
    ne = len(extra_rows)

    def fn(dh_t, x_t, dres_t, *rest):
        er, rest = rest[:ne], rest[ne:]
        gv_t = rest[0]
        ev, r_t = rest[1:-1], rest[-1]
        xh = x_t * r_t
        dxh = dh_t * gv_t
        dx = r_t * (dxh - xh * jnp.mean(dxh * xh, axis=-1, keepdims=True)) + dres_t
        base = (dx, dh_t, dh_t * xh)
        if extra_fn is None:
            return base
        return base + tuple(extra_fn(dx, *er, *ev))

    return _ew(fn, [dh, x, dres] + list(extra_rows), vecs=[gv] + list(extra_vecs), cols=[r],
               outs=[('row', D, F32), ('sum', D), ('sum', D)] + list(extra_outs), name=name)


def _gate_bwd(dx, dn, g):
    return dx * dn.astype(F32), dx * g


_GATE_OUTS = [('sum', D_MODEL), ('row', D_MODEL, BF16)]


def _layer_bwd(dx2, d_g2m, d_dn, sv, below, W, rc, tri, emit):
    mod = sv['mod']
    sh1, sc1, g1m, sh2, sc2, g2m = [mod[i:i + 1] for i in range(N_MOD)]
    D = D_MODEL
    p = sv['p']
    (d_up,) = _mm(d_dn, W["w_down"], tb=True, tm=2048, o_ex=[(sv['act'], 'o')],
                  epi=lambda acc, a: (acc * 2.0 * a.astype(F32),), outs=(BF16,), name="mm_down_dx")
    (gw_down,) = _mm(sv['act'], d_dn, ta=True, tm=2048, tk=1024, pro=lambda a: a * a, outs=(BF16,), name="mm_down_dw")
    (gw_up,) = _mm(sv['h2'], d_up, ta=True, tn=2048, tk=1024, outs=(BF16,), name="mm_up_dw")
    tok = emit(dict(w_down=gw_down, w_up=gw_up), "mlp")
    (d_h2,) = _mm(d_up, W["w_up"], tb=True, tm=2048, tk=2048, outs=(F32,), name="mm_up_dx", deps=[tok])
    dx1, d_sh2, s_h2, d_g1m, d_mo = _norm_bwd(
        d_h2, sv['x1'], sv['r2'], dx2, sv['gv2'], sv['gn2'],
        extra_rows=[sv['mo']], extra_vecs=[g1m],
        extra_fn=lambda dx, mo, g: (dx * mo.astype(F32), dx * g),
        extra_outs=[('sum', D), ('row', D, BF16)], name="norm_bwd_mlp")
    d_sc2 = sv['gn2'] * s_h2
    d_gn2 = (1.0 + sc2) * s_h2
    def mix_epi(acc, ya, yb, ga, gb):
        sa, sb = _sigmoid(ga.astype(F32)), _sigmoid(gb.astype(F32))
        return (acc * sa, acc * sb, acc * ya.astype(F32) * sa * (1.0 - sa), acc * yb.astype(F32) * sb * (1.0 - sb))

    d_ya, d_yb, d_ga, d_gb = _mm(d_mo, W["w_mix_out"], tb=True, tm=512,
                                 o_ex=[(sv['ya'], 'o'), (sv['yb'], 'o'), (p, 'o', O_GA), (p, 'o', O_GB)], epi=mix_epi,
                                 outs=(BF16,) * 4, name="mm_mix_dx")
    (gw_mix,) = _mm(sv['mg'], d_mo, ta=True, tk=DW_TK, outs=(BF16,), name="mm_mix_dw")

    def ro_epi(acc, g, yr):
        gf = g.astype(F32)
        s = _sigmoid(gf)
        return (acc * yr.astype(F32) * s * (1.0 + gf * (1.0 - s)), acc * gf * s)

    d_rg, d_yret = _mm(d_ya, W["w_ret_out"], tb=True, tm=512, tn=2048, o_ex=[(p, 'o', O_RG), (sv['yret'], 'o')],
                       epi=ro_epi, outs=(BF16, BF16), name="mm_ret_dx")
    (gw_ro,) = _mm(sv['yret'], d_ya, ta=True, tk=1024, a_ex=[(p, 'a', O_RG)],
                   pro=lambda yr, g: _silu(g.astype(F32)) * yr.astype(F32), outs=(BF16,), name="mm_ret_dw")
    (gw_so,) = _mm(sv['ysb'], d_yb, ta=True, tk=DW_TK, outs=(BF16,), name="mm_sb_dw")
    tok = emit(dict(w_mix_out=gw_mix, w_ret_out=gw_ro, w_sb_out=gw_so), "mix")
    (d_ysb,) = _mm(d_yb, W["w_sb_out"], tb=True, outs=(BF16,), name="mm_sb_dx", deps=[tok])
    d_sq, d_sk, d_sv = _sb_bwd(p, sv['sbc'], d_ysb, tri)
    d_ret = _ret_bwd(p, sv['rstate'], d_yret, rc)
    dp = [d_ret, d_rg, d_sq, d_sk, d_sv, d_ga, d_gb]
    (gw_in,) = _mm(sv['h1'], dp, ta=True, tk=1024, outs=(BF16,), name="mm_in_dw")
    tok = emit(dict(w_in=gw_in), "in")
    (d_h,) = _mm(dp, W["w_in"], tb=True, tm=2048, tk=512, outs=(F32,), name="mm_in_dx", deps=[tok])
    if below is None:
        dx0, d_sh1, s_h1 = _norm_bwd(d_h, sv['x0'], sv['r1'], dx1, sv['gv1'], sv['gn1'], name="norm_bwd_mix")
        gate_below = (None, None)
    else:
        dx0, d_sh1, s_h1, *gate_below = _norm_bwd(
            d_h, sv['x0'], sv['r1'], dx1, sv['gv1'], sv['gn1'], extra_rows=[below['dn']],
            extra_vecs=[below['mod'][N_MOD - 1:N_MOD]], extra_fn=_gate_bwd, extra_outs=_GATE_OUTS,
            name="norm_bwd_mix_gate")
    d_sc1 = sv['gn1'] * s_h1
    d_gn1 = (1.0 + sc1) * s_h1
    d_mod = jnp.concatenate([d_sh1, d_sc1, d_g1m, d_sh2, d_sc2, d_g2m], axis=1)
    return dx0, gate_below, d_mod, d_gn1, d_gn2


def kernel(x, c, norm_mix_g, w_in, w_ret_out, w_sb_out, w_mix_out, norm_mlp_g, w_up, w_down, w_ada, b_ada, final_g, loss_target, m_norm_mix_g, m_w_in, m_w_ret_out, m_w_sb_out, m_w_mix_out, m_norm_mlp_g, m_w_up, m_w_down, m_w_ada, m_b_ada, m_final_g, v_norm_mix_g, v_w_in, v_w_ret_out, v_w_sb_out, v_w_mix_out, v_norm_mlp_g, v_w_up, v_w_down, v_w_ada, v_b_ada, v_final_g):
    S, D = x.shape[1], x.shape[2]
    x0 = x.reshape(S, D)
    tgt = loss_target.reshape(S, D)
    me = 4 * lax.axis_index("x") + 2 * lax.axis_index("y") + lax.axis_index("c")
    wts = dict(w_in=w_in, w_ret_out=w_ret_out, w_sb_out=w_sb_out, w_mix_out=w_mix_out, w_up=w_up, w_down=w_down)
    mts = dict(w_in=m_w_in, w_ret_out=m_w_ret_out, w_sb_out=m_w_sb_out, w_mix_out=m_w_mix_out, w_up=m_w_up, w_down=m_w_down)
    vts = dict(w_in=v_w_in, w_ret_out=v_w_ret_out, w_sb_out=v_w_sb_out, w_mix_out=v_w_mix_out, w_up=v_w_up, w_down=v_w_down)
    rc = _ret_consts(S)
    tri = _tri()

    (cact,) = _ew(lambda t: (_silu(t),), [jnp.pad(c, ((0, 7), (0, 0)))], outs=[('row', D, F32)], name="silu_c")
    (cact_all,) = _all_gather_lead([cact[0:1]], "gather_c")
    cact_all = cact_all.reshape(N_DEV, D)
    cact16 = jnp.pad(cact_all, ((0, 8), (0, 0)))
    n_ada = w_ada.shape[2]
    b_loc = lax.dynamic_slice_in_dim(b_ada, me * n_ada, n_ada, axis=1)
    mods = [_mod_partial(cact16, w_ada, l, b_loc[l:l + 1])[:N_DEV] for l in range(DEPTH)]
    modp = jnp.stack(mods, axis=1)
    (modr,) = _all_to_all_lead([modp], "scatter_mod")
    mod_full = jnp.transpose(modr, (1, 0, 2)).reshape(DEPTH, N_MOD, D)

    def cast(nm, deps=()):
        w = wts[nm]
        (wb,) = _ew(lambda t: (t,), [w.reshape(-1, w.shape[-1])], outs=[('row', w.shape[-1], BF16)],
                    name="cast_bf16", deps=deps)
        return wb.reshape(w.shape)

    rest = tuple(nm for nm in _W_NAMES if nm != "w_in")
    shards = {"w_in": cast("w_in")}
    h_in = _gather_start({"w_in": shards["w_in"][0]}, ("w_in",), "0_in", [modr], two_level=True)
    in_flight = [h_in['token']]
    for nm in rest:
        shards[nm] = cast(nm, in_flight)
    sh = [{nm: shards[nm][l] for nm in _W_NAMES} for l in range(DEPTH)]

    pre = [(norm_mix_g[l:l + 1] * (1.0 + mod_full[l][1:2]), mod_full[l][0:1]) for l in range(DEPTH)]

    def first(t, gv, sh):
        r = _rstd(t)
        return r, _norm_mod(t, r, gv, sh)

    xs = x0
    rs, hs = _ew(first, [x0], vecs=list(pre[0]), outs=[('col', F32), ('row', D, BF16)], name="row_rstd",
                 deps=in_flight)

    win, pass_name, passed_name = h_in['second']
    before_wait = [hs] + [sh[l][nm] for l in range(DEPTH) for nm in _W_NAMES if (l, nm) != (0, "w_in")]
    h_pass = _sibling_start(_exchange_wait(h_in, before_wait, h_in['tag']), win, pass_name)
    h_pass['span'] = h_in['span']
    started = [h_pass['token']]
    layer_groups = []
    for l, groups in enumerate([[(rest, "0_rest")]] + [[(_W_NAMES, "%d_all" % l)] for l in range(1, DEPTH)]):
        layer_groups.append([])
        for names, tag in groups:
            layer_groups[-1].append((names, _gather_start(sh[l], names, tag, started[-1:])))
            started.append(layer_groups[-1][-1][1]['token'])
    (w_in0,) = _exchange_wait(h_pass, started, passed_name)
    layer_w = [_LayerWeights(g, started) for g in layer_groups]
    layer_w[0].got["w_in"] = w_in0
    saved = []
    for l in range(DEPTH):
        xs, rs, hs, sv = _layer_fwd(xs, rs, hs, mod_full[l], norm_mix_g[l:l + 1], norm_mlp_g[l:l + 1],
                                    pre[l + 1] if l + 1 < DEPTH else None, layer_w[l], rc, tri)
        sv['W'] = layer_w[l]
        saved.append(sv)

    fg = final_g.reshape(1, D)

    def head(xt, tg, dn, g, g2m, r):
        xh = xt * r
        e = xh * g - tg
        dy = e * (1.0 / D)
        dxh = dy * g
        dx = r * (dxh - xh * jnp.mean(dxh * xh, axis=-1, keepdims=True))
        return (dx, dy * xh, 0.5 * e * e * (1.0 / D)) + _gate_bwd(dx, dn, g2m)

    top = saved[DEPTH - 1]
    dxs, d_fg, loss_cols, *gate = _ew(head, [xs, tgt, top['dn']], vecs=[fg, top['mod'][N_MOD - 1:N_MOD]], cols=[rs],
                                      outs=[('row', D, F32), ('sum', D), ('sum', D)] + _GATE_OUTS, name="loss_head")

    small = [None] * DEPTH
    pending = []
    for l in reversed(range(DEPTH)):
        sv = saved[l]

        def emit(gw, tag, l=l):
            names = tuple(gw)
            pending.append((l, names, _scatter_start(gw, names, "%d_%s" % (l, tag))))
            return pending[-1][2]['token']

        dxs, gate, d_mod, d_gn1, d_gn2 = _layer_bwd(dxs, gate[0], gate[1], sv, saved[l - 1] if l else None,
                                                    sv['W'], rc, tri, emit)
        small[l] = (d_mod, d_gn1, d_gn2)
    grad_x = dxs.reshape(1, S, D)

    res = {}
    after = [dxs]
    for l, names, h in pending:
        for nm, landed in zip(names, _finish(h, after)):
            res[nm] = _adam_layer(landed, wts[nm], mts[nm], vts[nm], l, res.get(nm), "adam_layer")
        after = [res[names[-1]][0]]

    pack = jnp.concatenate([small[l][0] for l in range(DEPTH)] + [small[l][1] for l in range(DEPTH)]
                           + [small[l][2] for l in range(DEPTH)] + [d_fg, loss_cols], axis=1)
    (packs,) = _all_gather_lead([pack], "gather_small", deps=after)
    packs = packs.reshape(N_DEV, -1)
    o = 0
    dmod_all = []
    for l in range(DEPTH):
        dmod_all.append(packs[:, o:o + N_MOD * D]); o += N_MOD * D
    gn1_parts = packs[:, o:o + DEPTH * D].reshape(N_DEV, DEPTH, D); o += DEPTH * D
    gn2_parts = packs[:, o:o + DEPTH * D].reshape(N_DEV, DEPTH, D); o += DEPTH * D
    fg_parts = packs[:, o:o + D].reshape(N_DEV, 1, D); o += D
    loss_parts = packs[:, o:o + D]
    (loss_sum,) = _ew(lambda t: (t,), [loss_parts], outs=[('sum', D)], name="loss_sum")
    loss = jnp.sum(loss_sum)

    res["norm_mix_g"] = _adam(gn1_parts, norm_mix_g, m_norm_mix_g, v_norm_mix_g, "adam")
    res["norm_mlp_g"] = _adam(gn2_parts, norm_mlp_g, m_norm_mlp_g, v_norm_mlp_g, "adam")
    fgr = _adam(fg_parts, fg, m_final_g.reshape(1, D), v_final_g.reshape(1, D), "adam")
    res["final_g"] = [t.reshape(D) for t in fgr]
    bparts = jnp.stack(dmod_all, axis=1)
    res["b_ada"] = _adam(bparts, b_ada, m_b_ada, v_b_ada, "adam")
    cact_t = cact_all.T
    for l in range(DEPTH):
        dm_loc = lax.dynamic_slice_in_dim(dmod_all[l], me * n_ada, n_ada, axis=1)
        res["w_ada"] = _adam_layer(_ada_grad(cact_t, dm_loc), w_ada, m_w_ada, v_w_ada, l, res.get("w_ada"),
                                   "adam_layer")

    order = ['norm_mix_g', 'w_in', 'w_ret_out', 'w_sb_out', 'w_mix_out', 'norm_mlp_g', 'w_up', 'w_down', 'w_ada', 'b_ada', 'final_g']
    out = [loss, grad_x]
    for i in range(4):
        out += [res[nm][i] for nm in order]
    return tuple(out)
```

```python
import functools
import math

import jax
import jax.numpy as jnp
import numpy as np
from jax import lax
from jax.experimental import pallas as pl
from jax.experimental.pallas import tpu as pltpu

F32 = jnp.float32
BF16 = jnp.bfloat16

N_DEV = 8
D_MODEL = 1024
DEPTH = 2
RET_HEADS = 4
RET_QK = 256
RET_V = 512
RET_CHUNK = 256
ROPE_BASE = 10000.0
SB_HEADS = 16
SB_DIM = 64
D_FF = 4096
N_MOD = 6
EPS = 1e-6
GN_EPS = 1e-5
O_RQ, O_RK, O_RV, O_RG, O_SQ, O_SK, O_SV, O_GA, O_GB = 0, 1024, 2048, 4096, 6144, 7168, 8192, 9216, 10240
IN_W = 11264

ADAM_LR, ADAM_B1, ADAM_B2, ADAM_EPS, ADAM_WD, ADAM_STEP = 0.001, 0.9, 0.999, 1e-08, 0.01, 10

VMEM_LIMIT = 56 * 1024 * 1024
DW_TK = 2048


def _cparams(sem):
    return pltpu.CompilerParams(dimension_semantics=sem, vmem_limit_bytes=VMEM_LIMIT)


def _mm(a, b, *, ta=False, tb=False, tm=1024, tn=1024, tk=None, a_ex=(), pro=None, o_ex=(), epi=None,
        outs=(F32,), cols=0, name, deps=()):
    a_parts = list(a) if isinstance(a, (list, tuple)) else [a]
    b_parts = list(b) if isinstance(b, (list, tuple)) else [b]
    assert not (ta and len(a_parts) > 1) and not (tb and len(b_parts) > 1)
    if ta:
        K, M = a.shape
    else:
        M, K = a_parts[0].shape[0], sum(t.shape[1] for t in a_parts)
    N = b.shape[0] if tb else sum(t.shape[1] for t in b_parts)
    tm, tn, tk = min(tm, M), min(tn, N), K if tk is None else min(tk, K)
    assert M % tm == 0 and N % tn == 0 and K % tk == 0, (name, M, N, K, tm, tn, tk)
    nk = K // tk
    multi = len(a_parts) > 1 or len(b_parts) > 1
    direct = epi is None and tuple(outs) == (F32,) and cols == 0

    def ranges(parts, t):
        out, o = [], 0
        for arr in parts:
            assert arr.shape[1] % t == 0
            out.append((o, o + arr.shape[1] // t))
            o += arr.shape[1] // t
        return out

    a_rng = ranges(a_parts, tk) if len(a_parts) > 1 else [(0, nk)]
    b_rng = ranges(b_parts, tn) if len(b_parts) > 1 else [(0, N // tn)]
    clip = lambda v, lo, hi: jnp.clip(v - lo, 0, hi - lo - 1)
    mine = lambda v, lo, hi, w: jnp.where((v >= lo) & (v < hi), w, 0)
    in_specs, args = [], []
    for arr, (lo, hi) in zip(a_parts, a_rng):
        in_specs.append(pl.BlockSpec((tk, tm), lambda i, j, k: (k, i)) if ta
                        else pl.BlockSpec((tm, tk), lambda i, j, k, lo=lo, hi=hi: (i, clip(k, lo, hi))))
        args.append(arr)
    for arr, (lo, hi) in zip(b_parts, b_rng):
        in_specs.append(pl.BlockSpec((tn, tk), lambda i, j, k: (j, k)) if tb
                        else pl.BlockSpec((tk, tn), lambda i, j, k, lo=lo, hi=hi: (mine(j, lo, hi, k), clip(j, lo, hi))))
        args.append(arr)
    npa, npb = len(a_parts), len(b_parts)
    for arr, kind, *off in a_ex:
        off = off[0] if off else 0
        if kind == 'a' and ta:
            assert off % tm == 0
            in_specs.append(pl.BlockSpec((tk, tm), lambda i, j, k, o=off // tm: (k, o + i)))
        elif kind == 'a':
            assert off % tk == 0
            in_specs.append(pl.BlockSpec((tm, tk), lambda i, j, k, o=off // tk: (i, o + k)))
        elif kind == 'k':
            in_specs.append(pl.BlockSpec((tk, 1), lambda i, j, k: (k, 0)) if ta
                            else pl.BlockSpec((1, tk), lambda i, j, k: (0, k)))
        else:
            in_specs.append(pl.BlockSpec((1, tm), lambda i, j, k: (0, i)) if ta
                            else pl.BlockSpec((tm, 1), lambda i, j, k: (i, 0)))
        args.append(arr)
    for arr, kind, *off in o_ex:
        off = off[0] if off else 0
        if kind == 'o':
            assert off % tn == 0
            in_specs.append(pl.BlockSpec((tm, tn), lambda i, j, k, o=off // tn: (i, o + j)))
        elif kind == 'n':
            in_specs.append(pl.BlockSpec((1, tn), lambda i, j, k: (0, j)))
        else:
            in_specs.append(pl.BlockSpec((tm, 1), lambda i, j, k: (i, 0)))
        args.append(arr)
    for arr in deps:
        in_specs.append(pl.BlockSpec(memory_space=pl.ANY))
        args.append(arr)
    assert cols == 0 or N == tn
    na, no, nout, nd = len(a_ex), len(o_ex), len(outs) + cols, len(deps)
    dims = (((0 if ta else 1,), (1 if tb else 0,)), ((), ()))

    def body(*refs):
        a_refs, b_refs = refs[:npa], refs[npa:npa + npb]
        n0 = npa + npb
        aex = refs[n0:n0 + na]
        oex = refs[n0 + na:n0 + na + no]
        out_refs = refs[n0 + na + no + nd:n0 + na + no + nd + nout]

        def product(a_ref, b_ref):
            at = a_ref[...]
            if pro is not None:
                at = pro(at, *[r[...] for r in aex])
            return lax.dot_general(at.astype(BF16), b_ref[...].astype(BF16), dims, preferred_element_type=F32)

        def finish(res):
            vals = epi(res, *[r[...] for r in oex]) if epi is not None else (res,)
            for o_ref, v in zip(out_refs, vals):
                o_ref[...] = v.astype(o_ref.dtype)

        if nk == 1 and not multi:
            finish(product(a_refs[0], b_refs[0]))
            return
        acc = out_refs[0] if direct else refs[-1]
        j, k = pl.program_id(1), pl.program_id(2)
        if multi:
            @pl.when(k == 0)
            def _():
                acc[...] = jnp.zeros_like(acc)

            for a_ref, (alo, ahi) in zip(a_refs, a_rng):
                for b_ref, (blo, bhi) in zip(b_refs, b_rng):
                    @pl.when((k >= alo) & (k < ahi) & (j >= blo) & (j < bhi))
                    def _():
                        acc[...] += product(a_ref, b_ref)
        else:
            @pl.when(k == 0)
            def _():
                acc[...] = product(a_refs[0], b_refs[0])

            @pl.when(k > 0)
            def _():
                acc[...] += product(a_refs[0], b_refs[0])

        if not direct:
            @pl.when(k == nk - 1)
            def _():
                finish(acc[...])

    res = pl.pallas_call(
        body, name=name, grid=(M // tm, N // tn, nk), in_specs=in_specs,
        out_specs=[pl.BlockSpec((tm, tn), lambda i, j, k: (i, j)) for _ in outs]
        + [pl.BlockSpec((tm, 1), lambda i, j, k: (i, 0))] * cols,
        out_shape=[jax.ShapeDtypeStruct((M, N), dt) for dt in outs] + [jax.ShapeDtypeStruct((M, 1), F32)] * cols,
        scratch_shapes=[pltpu.VMEM((tm, tn), F32)] if (nk > 1 or multi) and not direct else [],
        compiler_params=_cparams(("parallel", "parallel", "arbitrary")),
    )(*args)
    return res


def _ew(fn, rows, vecs=(), cols=(), outs=(), tr=256, name=None, deps=()):
    S = rows[0].shape[0]
    tr = min(tr, S)
    assert S % tr == 0
    in_specs, args = [], []
    for r in rows:
        in_specs.append(pl.BlockSpec((tr, r.shape[1]), lambda i: (i, 0)))
        args.append(r)
    for v in vecs:
        in_specs.append(pl.BlockSpec((1, v.shape[1]), lambda i: (0, 0)))
        args.append(v)
    for c in cols:
        in_specs.append(pl.BlockSpec((tr, 1), lambda i: (i, 0)))
        args.append(c)
    out_specs, out_shape = [], []
    for o in outs:
        if o[0] == 'row':
            out_specs.append(pl.BlockSpec((tr, o[1]), lambda i: (i, 0)))
            out_shape.append(jax.ShapeDtypeStruct((S, o[1]), o[2]))
        elif o[0] == 'sum':
            out_specs.append(pl.BlockSpec((1, o[1]), lambda i: (0, 0)))
            out_shape.append(jax.ShapeDtypeStruct((1, o[1]), F32))
        else:
            out_specs.append(pl.BlockSpec((tr, 1), lambda i: (i, 0)))
            out_shape.append(jax.ShapeDtypeStruct((S, 1), o[1]))
    nin = len(args)
    in_specs += [pl.BlockSpec(memory_space=pl.ANY)] * len(deps)
    args += list(deps)

    def body(*refs):
        i = pl.program_id(0)
        vals = fn(*[r[...] for r in refs[:nin]])
        for o, o_ref, v in zip(outs, refs[nin + len(deps):], vals):
            if o[0] == 'sum':
                @pl.when(i == 0)
                def _():
                    o_ref[...] = jnp.zeros_like(o_ref)
                o_ref[...] += jnp.sum(v.astype(F32), axis=0, keepdims=True)
            else:
                o_ref[...] = v.astype(o_ref.dtype)

    return pl.pallas_call(
        body, name=name, grid=(S // tr,), in_specs=in_specs, out_specs=out_specs, out_shape=out_shape,
        compiler_params=_cparams(("arbitrary",)),
    )(*args)


def _sigmoid(x):
    return 1.0 / (1.0 + jnp.exp(-x))


def _ret_consts(S):
    h = np.arange(RET_HEADS, dtype=np.float64)
    log_gamma = np.log1p(-np.power(2.0, -5.0 - h))
    idx = np.arange(RET_CHUNK, dtype=np.float64)
    rel = idx[:, None] - idx[None, :]
    decay = np.where(rel >= 0, np.exp(np.maximum(rel, 0.0) * log_gamma[:, None, None]), 0.0)
    xi = np.exp((idx + 1.0) * log_gamma[:, None])[:, :, None]
    zeta = np.exp((RET_CHUNK - 1.0 - idx) * log_gamma[:, None])[:, :, None]
    gamma_c = np.exp(RET_CHUNK * log_gamma)[:, None, None]
    half = RET_QK // 2
    inv_freq = np.power(ROPE_BASE, -np.arange(half, dtype=np.float64) / half).astype(np.float32)
    ang = np.arange(S, dtype=np.float32)[:, None] * inv_freq[None, :]
    f = lambda t: jnp.asarray(t, F32)
    return dict(decay=f(decay), xi=f(xi), zeta=f(zeta), gc=f(gamma_c), cos=f(np.cos(ang)), sin=f(np.sin(ang)))


def _rot(t, cos, sin):
    half = RET_QK // 2
    t1, t2 = t[:, :half], t[:, half:]
    return jnp.concatenate([t1 * cos - t2 * sin, t1 * sin + t2 * cos], axis=-1)


def _rot_inv(t, cos, sin):
    half = RET_QK // 2
    t1, t2 = t[:, :half], t[:, half:]
    return jnp.concatenate([t1 * cos + t2 * sin, t2 * cos - t1 * sin], axis=-1)


_NT = (((1,), (1,)), ((), ()))
_TN = (((0,), (0,)), ((), ()))


def _dot(a, b):
    return jnp.dot(a, b, preferred_element_type=F32)


def _dot_nt(a, b):
    return lax.dot_general(a, b, _NT, preferred_element_type=F32)


def _dot_tn(a, b):
    return lax.dot_general(a, b, _TN, preferred_element_type=F32)


_QW, _VW = RET_HEADS * RET_QK, RET_HEADS * RET_V
_HEADS = range(RET_HEADS)


def _ret_in_specs(C, rev, NC):
    n_of = (lambda n: NC - 1 - n) if rev else (lambda n: n)
    whole3 = lambda n: (0, 0, 0)
    return [
        pl.BlockSpec((C, _QW), lambda n: (n_of(n), O_RQ // _QW)),
        pl.BlockSpec((C, _QW), lambda n: (n_of(n), O_RK // _QW)),
        pl.BlockSpec((C, _VW), lambda n: (n_of(n), O_RV // _VW)),
        pl.BlockSpec((C, RET_QK // 2), lambda n: (n_of(n), 0)),
        pl.BlockSpec((C, RET_QK // 2), lambda n: (n_of(n), 0)),
        pl.BlockSpec((RET_HEADS, C, C), whole3),
        pl.BlockSpec((RET_HEADS, C, 1), whole3),
        pl.BlockSpec((RET_HEADS, C, 1), whole3),
        pl.BlockSpec((RET_HEADS, 1, 1), whole3),
    ]


def _qk_heads(q_ref, k_ref, cos, sin):
    qs, kfs = [], []
    for h in _HEADS:
        cols = slice(h * RET_QK, (h + 1) * RET_QK)
        qs.append(_rot(q_ref[:, cols].astype(F32), cos, sin).astype(BF16))
        kfs.append(_rot(k_ref[:, cols].astype(F32), cos, sin) * (RET_QK ** -0.5))
    return qs, kfs


def _ret_fwd(p, rc):
    S = p.shape[0]
    C = RET_CHUNK
    NC = S // C

    def body(q_ref, k_ref, v_ref, cos_ref, sin_ref, dec_ref, xi_ref, zeta_ref, gc_ref, y_ref, rs_ref, r_acc):
        n = pl.program_id(0)

        @pl.when(n == 0)
        def _():
            r_acc[...] = jnp.zeros_like(r_acc)

        cos, sin = cos_ref[...], sin_ref[...]
        qs, kfs = _qk_heads(q_ref, k_ref, cos, sin)
        vs = [v_ref[:, h * RET_V:(h + 1) * RET_V] for h in _HEADS]
        rbs = [r_acc[h].astype(BF16) for h in _HEADS]
        for h in _HEADS:
            rs_ref[h, 0] = rbs[h]
        ss = [(_dot_nt(qs[h], kfs[h].astype(BF16)) * dec_ref[h]).astype(BF16) for h in _HEADS]
        os = [_dot(ss[h], vs[h]) + _dot(qs[h], rbs[h]) * xi_ref[h] for h in _HEADS]
        for h in _HEADS:
            o = os[h]
            mu = jnp.mean(o, axis=-1, keepdims=True)
            var = jnp.mean(jnp.square(o - mu), axis=-1, keepdims=True)
            y_ref[:, h * RET_V:(h + 1) * RET_V] = ((o - mu) * lax.rsqrt(var + GN_EPS)).astype(y_ref.dtype)
        for h in _HEADS:
            kz = (kfs[h] * zeta_ref[h]).astype(BF16)
            r_acc[h] = r_acc[h] * gc_ref[h] + _dot_tn(kz, vs[h])

    return pl.pallas_call(
        body, name="ret_fwd", grid=(NC,), in_specs=_ret_in_specs(C, False, NC),
        out_specs=[pl.BlockSpec((C, _VW), lambda n: (n, 0)),
                   pl.BlockSpec((RET_HEADS, 1, RET_QK, RET_V), lambda n: (0, n, 0, 0))],
        out_shape=[jax.ShapeDtypeStruct((S, _VW), BF16),
                   jax.ShapeDtypeStruct((RET_HEADS, NC, RET_QK, RET_V), BF16)],
        scratch_shapes=[pltpu.VMEM((RET_HEADS, RET_QK, RET_V), F32)],
        compiler_params=_cparams(("arbitrary",)),
    )(p, p, p, rc['cos'], rc['sin'], rc['decay'], rc['xi'], rc['zeta'], rc['gc'])


def _ret_bwd(p, rstate, dy, rc):
    S = p.shape[0]
    C = RET_CHUNK
    NC = S // C

    def body(q_ref, k_ref, v_ref, cos_ref, sin_ref, dec_ref, xi_ref, zeta_ref, gc_ref, rs_ref, dy_ref,
             d_ref, dr_acc):
        dq_ref, dk_ref, dv_ref = d_ref.at[:, 0:_QW], d_ref.at[:, _QW:2 * _QW], d_ref.at[:, 2 * _QW:2 * _QW + _VW]
        t = pl.program_id(0)

        @pl.when(t == 0)
        def _():
            dr_acc[...] = jnp.zeros_like(dr_acc)

        cos, sin = cos_ref[...], sin_ref[...]
        qs, kfs = _qk_heads(q_ref, k_ref, cos, sin)
        ks = [kf.astype(BF16) for kf in kfs]
        vs = [v_ref[:, h * RET_V:(h + 1) * RET_V] for h in _HEADS]
        rbs = [rs_ref[h, 0] for h in _HEADS]
        ss = [(_dot_nt(qs[h], ks[h]) * dec_ref[h]).astype(BF16) for h in _HEADS]
        os = [_dot(ss[h], vs[h]) + _dot(qs[h], rbs[h]) * xi_ref[h] for h in _HEADS]
        dobs, doxis = [], []
        for h in _HEADS:
            o = os[h]
            mu = jnp.mean(o, axis=-1, keepdims=True)
            var = jnp.mean(jnp.square(o - mu), axis=-1, keepdims=True)
            rstd = lax.rsqrt(var + GN_EPS)
            yh = (o - mu) * rstd
            dyf = dy_ref[:, h * RET_V:(h + 1) * RET_V].astype(F32)
            do = (dyf - jnp.mean(dyf, axis=-1, keepdims=True)
                  - yh * jnp.mean(dyf * yh, axis=-1, keepdims=True)) * rstd
            dobs.append(do.astype(BF16))
            doxis.append((do * xi_ref[h]).astype(BF16))
        drbs = [dr_acc[h].astype(BF16) for h in _HEADS]
        dss = [(_dot_nt(dobs[h], vs[h]) * dec_ref[h]).astype(BF16) for h in _HEADS]
        for h in _HEADS:
            dq = _dot(dss[h], ks[h]) + _dot_nt(doxis[h], rbs[h])
            dq_ref[:, h * RET_QK:(h + 1) * RET_QK] = _rot_inv(dq, cos, sin).astype(dq_ref.dtype)
        for h in _HEADS:
            dk = _dot_tn(dss[h], qs[h]) + _dot_nt(vs[h], drbs[h]) * zeta_ref[h]
            dk_ref[:, h * RET_QK:(h + 1) * RET_QK] = (_rot_inv(dk, cos, sin) * (RET_QK ** -0.5)).astype(dk_ref.dtype)
        for h in _HEADS:
            kz = (kfs[h] * zeta_ref[h]).astype(BF16)
            dv = _dot_tn(ss[h], dobs[h]) + _dot(kz, drbs[h])
            dv_ref[:, h * RET_V:(h + 1) * RET_V] = dv.astype(dv_ref.dtype)
        for h in _HEADS:
            dr_acc[h] = dr_acc[h] * gc_ref[h] + _dot_tn(qs[h], doxis[h])

    rn = lambda n: NC - 1 - n
    in_specs = _ret_in_specs(C, True, NC) + [
        pl.BlockSpec((RET_HEADS, 1, RET_QK, RET_V), lambda n: (0, rn(n), 0, 0)),
        pl.BlockSpec((C, _VW), lambda n: (rn(n), 0)),
    ]
    return pl.pallas_call(
        body, name="ret_bwd", grid=(NC,), in_specs=in_specs,
        out_specs=pl.BlockSpec((C, 2 * _QW + _VW), lambda n: (rn(n), 0)),
        out_shape=jax.ShapeDtypeStruct((S, 2 * _QW + _VW), BF16),
        scratch_shapes=[pltpu.VMEM((RET_HEADS, RET_QK, RET_V), F32)],
        compiler_params=_cparams(("arbitrary",)),
    )(p, p, p, rc['cos'], rc['sin'], rc['decay'], rc['xi'], rc['zeta'], rc['gc'], rstate, dy)


SB_T = 256
SB_SCALE = SB_DIM ** -0.5


def _tri():
    j = np.arange(SB_T)
    after = (j[:, None] > j[None, :]).astype(np.float32)
    upto = (j[:, None] <= j[None, :]).astype(np.float32)
    return jnp.asarray(np.stack([after, upto]), BF16)


def _softplus_parts(z):
    neg_abs = lax.bitcast_convert_type(lax.bitcast_convert_type(z, jnp.uint32) | jnp.uint32(0x80000000), F32)
    e = jnp.exp(neg_abs)
    return jnp.maximum(z, 0.0) + jnp.log(1.0 + e), e


def _sb_fwd(p, tri):
    S = p.shape[0]
    T = min(SB_T, S)
    NQ = S // T
    assert NQ <= 128
    qb, kb, vb = O_SQ // 128, O_SK // 128, O_SV // 128

    def body(q_ref, k_ref, v_ref, tri_ref, o_ref, cs_ref, o_acc, run, zbuf, abuf):
        i = pl.program_id(1)
        lane = lax.broadcasted_iota(jnp.int32, (1, 128), 1)
        tri_after = tri_ref[0]
        qs = [jnp.where((lane >= 64) if hh else (lane < 64), q_ref[...], jnp.zeros_like(q_ref[...]))
              * jnp.asarray(SB_SCALE, BF16) for hh in range(2)]
        cs_ref[...] = jnp.zeros_like(cs_ref)
        o_acc[...] = jnp.zeros_like(o_acc)
        run[...] = jnp.zeros_like(run)

        def kv(ref, j):
            return ref[pl.ds(pl.multiple_of(j * T, T), T), :]

        for hh in range(2):
            zbuf[hh] = _dot_nt(qs[hh], kv(k_ref, i))

        def block(t, diagonal):
            j = i - t
            if diagonal:
                msk = lax.broadcasted_iota(jnp.int32, (T, T), 1) < lax.broadcasted_iota(jnp.int32, (T, T), 0)
            if not diagonal:
                av = [_dot(abuf[hh], kv(v_ref, j + 1)) for hh in range(2)]
            lss, exs, tot, zn = [], [], [], []
            for hh in range(2):
                z = zbuf[hh]
                sp, _ = _softplus_parts(z)
                lss.append(z - sp)
                if diagonal:
                    sp = jnp.where(msk, sp, 0.0)
                exs.append(_dot(sp.astype(BF16), tri_after))
                tot.append(sp[:, 0:1])
                zn.append(_dot_nt(qs[hh], kv(k_ref, jnp.maximum(j - 1, 0))))
            for hh in range(2):
                csl = slice(hh * 128, (hh + 1) * 128)
                cs = run[hh]
                a = jnp.exp(lss[hh] - exs[hh] - cs)
                if diagonal:
                    a = jnp.where(msk, a, 0.0)
                abuf[hh] = a.astype(BF16)
                cs_ref[:, csl] = jnp.where(lane == j, cs, cs_ref[:, csl])
                run[hh] = cs + exs[hh][:, 0:1] + tot[hh]
            for hh in range(2):
                if not diagonal:
                    o_acc[hh] += av[hh]
                zbuf[hh] = zn[hh]

        block(0, True)

        def step(t, carry):
            block(t, False)
            return carry

        lax.fori_loop(1, i + 1, step, 0)
        o_ref[...] = jnp.where(lane < 64, o_acc[0] + _dot(abuf[0], kv(v_ref, 0)),
                               o_acc[1] + _dot(abuf[1], kv(v_ref, 0))).astype(o_ref.dtype)

    return pl.pallas_call(
        body, name="sb_fwd", grid=(SB_HEADS // 2, NQ),
        scratch_shapes=[pltpu.VMEM((2, T, 128), F32), pltpu.VMEM((2, T, 1), F32), pltpu.VMEM((2, T, T), F32),
                        pltpu.VMEM((2, T, T), BF16)],
        in_specs=[pl.BlockSpec((T, 128), lambda h, i: (i, qb + h)),
                  pl.BlockSpec((S, 128), lambda h, i: (0, kb + h)),
                  pl.BlockSpec((S, 128), lambda h, i: (0, vb + h)),
                  pl.BlockSpec((1, T, T), lambda h, i: (0, 0, 0))],
        out_specs=[pl.BlockSpec((T, 128), lambda h, i: (i, h)),
                   pl.BlockSpec((T, 256), lambda h, i: (i, h))],
        out_shape=[jax.ShapeDtypeStruct((S, SB_HEADS * SB_DIM), BF16),
                   jax.ShapeDtypeStruct((S, SB_HEADS * 128), F32)],
        compiler_params=_cparams(("parallel", "arbitrary")),
    )(p, p, p, tri)


def _sb_bwd(p, carries, dy, tri):
    S = p.shape[0]
    T = min(SB_T, S)
    NQ = S // T
    qb, kb, vb = O_SQ // 128, O_SK // 128, O_SV // 128

    def body(q_ref, k_ref, v_ref, cs_ref, dy_ref, tri_ref, dq_ref, dk_ref, dv_ref, dk_acc, dv_acc, dq_acc, run,
             zbuf, dabuf, dzbuf, abuf):
        i = pl.program_id(1)

        @pl.when(i == 0)
        def _():
            dk_acc[...] = jnp.zeros_like(dk_acc)
            dv_acc[...] = jnp.zeros_like(dv_acc)

        lane = lax.broadcasted_iota(jnp.int32, (1, 128), 1)
        tri_after, tri_upto = tri_ref[0], tri_ref[1]
        hms = [(lane >= 64) if hh else (lane < 64) for hh in range(2)]
        qs = [jnp.where(hm, q_ref[...], jnp.zeros_like(q_ref[...])) * jnp.asarray(SB_SCALE, BF16) for hm in hms]
        dos = [jnp.where(hm, dy_ref[...], jnp.zeros_like(dy_ref[...])) for hm in hms]
        qst = [t.T for t in qs]
        dost = [t.T for t in dos]
        dq_acc[...] = jnp.zeros_like(dq_acc)
        run[...] = jnp.zeros_like(run)
        dzbuf[...] = jnp.zeros_like(dzbuf)
        abuf[...] = jnp.zeros_like(abuf)

        def kv(ref, j):
            return ref[pl.ds(pl.multiple_of(j * T, T), T), :]

        def flush(jp):
            kp = kv(k_ref, jp)
            dq_add = [_dot(dzbuf[hh], kp) for hh in range(2)]
            dk_add = _dot(qst[0], dzbuf[0]) + _dot(qst[1], dzbuf[1])
            dv_add = _dot(dost[0], abuf[0]) + _dot(dost[1], abuf[1])
            return dq_add, dk_add, dv_add

        def apply(jp, adds):
            dq_add, dk_add, dv_add = adds
            cols = pl.ds(pl.multiple_of(jp * T, T), T)
            for hh in range(2):
                dq_acc[hh] += dq_add[hh]
            dk_acc[:, cols] += dk_add
            dv_acc[:, cols] += dv_add

        for hh in range(2):
            zbuf[hh] = _dot_nt(qs[hh], kv(k_ref, 0))
            dabuf[hh] = _dot_nt(dos[hh], kv(v_ref, 0))

        def block(j, diagonal):
            jp = jnp.maximum(j - 1, 0)
            if diagonal:
                msk = lax.broadcasted_iota(jnp.int32, (T, T), 1) < lax.broadcasted_iota(jnp.int32, (T, T), 0)
            kp = kv(k_ref, jp)
            dq_add = [_dot(dzbuf[hh], kp) for hh in range(2)]
            sigs, lss, exs, zn, dan, dk_part, dv_part = [], [], [], [], [], [], []
            for hh in range(2):
                z = zbuf[hh]
                sp, _ = _softplus_parts(z)
                lss.append(z - sp)
                sigs.append(jnp.exp(lss[hh]))
                if diagonal:
                    sp = jnp.where(msk, sp, 0.0)
                exs.append(_dot(sp.astype(BF16), tri_after))
                if not diagonal:
                    zn.append(_dot_nt(qs[hh], kv(k_ref, j + 1)))
                dk_part.append(_dot(qst[hh], dzbuf[hh]))
            pgs, gs = [], []
            for hh in range(2):
                csl = slice(hh * 128, (hh + 1) * 128)
                cs = jnp.sum(jnp.where(lane == j, cs_ref[:, csl], 0.0), axis=-1, keepdims=True)
                a = jnp.exp(lss[hh] - exs[hh] - cs)
                if diagonal:
                    a = jnp.where(msk, a, 0.0)
                abuf_new = a.astype(BF16)
                g = a * dabuf[hh]
                gs.append((g, abuf_new))
                pgs.append(_dot(g.astype(BF16), tri_upto))
                if not diagonal:
                    dan.append(_dot_nt(dos[hh], kv(v_ref, j + 1)))
                dv_part.append(_dot(dost[hh], abuf[hh]))
            adds = (dq_add, dk_part[0] + dk_part[1], dv_part[0] + dv_part[1])
            for hh in range(2):
                g, abuf_new = gs[hh]
                cg = run[hh]
                dz = g - sigs[hh] * (cg + pgs[hh])
                if diagonal:
                    dz = jnp.where(msk, dz, 0.0)
                run[hh] = cg + pgs[hh][:, T - 1:T]
                dzbuf[hh] = dz.astype(BF16)
                abuf[hh] = abuf_new
            apply(jp, adds)
            if not diagonal:
                for hh in range(2):
                    zbuf[hh] = zn[hh]
                    dabuf[hh] = dan[hh]

        def step(j, carry):
            block(j, False)
            return carry

        lax.fori_loop(0, i, step, 0)
        block(i, True)
        apply(i, flush(i))
        dq_ref[...] = (jnp.where(lane < 64, dq_acc[0], dq_acc[1]) * SB_SCALE).astype(dq_ref.dtype)

        @pl.when(i == NQ - 1)
        def _():
            dk_ref[...] = dk_acc[...].T.astype(dk_ref.dtype)
            dv_ref[...] = dv_acc[...].T.astype(dv_ref.dtype)

    W = SB_HEADS * SB_DIM
    return pl.pallas_call(
        body, name="sb_bwd", grid=(SB_HEADS // 2, NQ),
        in_specs=[pl.BlockSpec((T, 128), lambda h, i: (i, qb + h)),
                  pl.BlockSpec((S, 128), lambda h, i: (0, kb + h)),
                  pl.BlockSpec((S, 128), lambda h, i: (0, vb + h)),
                  pl.BlockSpec((T, 256), lambda h, i: (i, h)),
                  pl.BlockSpec((T, 128), lambda h, i: (i, h)),
                  pl.BlockSpec((2, T, T), lambda h, i: (0, 0, 0))],
        out_specs=[pl.BlockSpec((T, 128), lambda h, i: (i, h)),
                   pl.BlockSpec((S, 128), lambda h, i: (0, h)),
                   pl.BlockSpec((S, 128), lambda h, i: (0, h))],
        out_shape=[jax.ShapeDtypeStruct((S, W), BF16)] * 3,
        scratch_shapes=[pltpu.VMEM((128, S), F32), pltpu.VMEM((128, S), F32), pltpu.VMEM((2, T, 128), F32),
                        pltpu.VMEM((2, T, 1), F32), pltpu.VMEM((2, T, T), F32), pltpu.VMEM((2, T, T), F32),
                        pltpu.VMEM((2, T, T), BF16), pltpu.VMEM((2, T, T), BF16)],
        compiler_params=_cparams(("parallel", "arbitrary")),
    )(p, p, p, carries, dy, tri)


def _exchange(srcs, out_shapes, src_slice, dst_slice, name, deps=()):
    n, nd = len(srcs), len(deps)

    def body(*refs):
        ins, outs = refs[:n], refs[n + nd:2 * n + nd]
        send_sems, recv_sems, loc_sems = refs[2 * n + nd:]
        x, y, c = lax.axis_index("x"), lax.axis_index("y"), lax.axis_index("c")
        me = 4 * x + 2 * y + c
        local = [pltpu.make_async_copy(src_slice(t, ins[t], me), dst_slice(t, outs[t], me), loc_sems.at[t])
                 for t in range(n)]
        for cp in local:
            cp.start()
        sends, recvs = [], []
        for k in (1, 2, 4, 6, 3, 5, 7):
            px = 1 - x if k & 4 else x
            py = 1 - y if k & 2 else y
            pc = 1 - c if k & 1 else c
            peer = 4 * px + 2 * py + pc
            for t in range(n):
                s = t * 7 + k - 1
                sends.append(pltpu.make_async_remote_copy(
                    src_ref=src_slice(t, ins[t], peer), dst_ref=dst_slice(t, outs[t], me),
                    send_sem=send_sems.at[s], recv_sem=recv_sems.at[s],
                    device_id=(px, py, pc), device_id_type=pl.DeviceIdType.MESH))
                recvs.append(pltpu.make_async_remote_copy(
                    src_ref=src_slice(t, ins[t], me), dst_ref=dst_slice(t, outs[t], peer),
                    send_sem=send_sems.at[s], recv_sem=recv_sems.at[s],
                    device_id=(px, py, pc), device_id_type=pl.DeviceIdType.MESH))
        for cp in sends:
            cp.start()
        for cp in recvs:
            cp.wait_recv()
        for cp in sends:
            cp.wait_send()
        for cp in local:
            cp.wait()

    anyspec = pl.BlockSpec(memory_space=pl.ANY)
    return pl.pallas_call(
        body, name=name, in_specs=[anyspec] * (n + nd), out_specs=[anyspec] * n,
        out_shape=[jax.ShapeDtypeStruct(s, d) for s, d in out_shapes],
        scratch_shapes=[pltpu.SemaphoreType.DMA((7 * n,)), pltpu.SemaphoreType.DMA((7 * n,)),
                        pltpu.SemaphoreType.DMA((n,))],
    )(*srcs, *deps)


def _all_gather_lead(xs, name, deps=()):
    return _exchange(
        xs, [((N_DEV,) + x.shape, x.dtype) for x in xs],
        lambda t, ref, peer: ref, lambda t, ref, who: ref.at[who], name, deps)


def _all_to_all_lead(xs, name):
    return _exchange(
        xs, [(x.shape, x.dtype) for x in xs],
        lambda t, ref, peer: ref.at[peer], lambda t, ref, who: ref.at[who], name)


_W_AXIS = {"w_in": 1, "w_ret_out": 0, "w_sb_out": 0, "w_mix_out": 0, "w_up": 1, "w_down": 0}
_W_NAMES = tuple(_W_AXIS)


def _window(ref, axis, who, width, count=1):
    start = pl.multiple_of(who * width, width)
    return ref.at[pl.ds(start, count * width), :] if axis == 0 else ref.at[:, pl.ds(start, count * width)]


_HBM = pl.BlockSpec(memory_space=pltpu.HBM)
_SEM = pl.BlockSpec(memory_space=pltpu.SEMAPHORE)
_EFFECT = pltpu.SideEffectType.DATAFLOW_SIDE_EFFECTING


_ALL_PEERS = (0, 1, 2, 4, 6, 3, 5, 7)
_SAME_CORE = (0, 2, 4, 6)


def _exchange_start(srcs, shapes, src_slice, dst_slice, name, deps=(), ks=_ALL_PEERS):
    n, nd = len(srcs), len(deps)
    lands = [pltpu.with_memory_space_constraint(lax.empty(s, d), pltpu.HBM) for s, d in shapes]

    def body(*refs):
        ins, lnd = refs[:n], refs[n:2 * n]
        sems = refs[2 * n + nd:4 * n + nd]
        token = refs[6 * n + nd]
        x, y, c = lax.axis_index("x"), lax.axis_index("y"), lax.axis_index("c")
        me = 4 * x + 2 * y + c
        for k in ks:
            px = 1 - x if k & 4 else x
            py = 1 - y if k & 2 else y
            pc = 1 - c if k & 1 else c
            peer = 4 * px + 2 * py + pc
            for t in range(n):
                pltpu.make_async_remote_copy(
                    src_ref=src_slice(t, ins[t], peer), dst_ref=dst_slice(t, lnd[t], me),
                    send_sem=sems[2 * t], recv_sem=sems[2 * t + 1],
                    device_id=(px, py, pc), device_id_type=pl.DeviceIdType.MESH).start()
        token[...] = jnp.zeros_like(token)

    res = pl.pallas_call(
        body, name=name, in_specs=[_HBM] * (2 * n) + [pl.BlockSpec(memory_space=pl.ANY)] * nd,
        out_specs=[_SEM] * (2 * n) + [_HBM] * (2 * n) + [pl.BlockSpec(memory_space=pltpu.VMEM)],
        out_shape=[pltpu.SemaphoreType.DMA(())] * (2 * n) + [pltpu.HBM(s.shape, s.dtype) for s in srcs]
        + [pltpu.HBM(s.shape, s.dtype) for s in lands] + [jax.ShapeDtypeStruct((8, 128), F32)],
        input_output_aliases={t: 2 * n + t for t in range(2 * n)},
        compiler_params=pltpu.CompilerParams(has_side_effects=_EFFECT),
    )(*[pltpu.with_memory_space_constraint(s, pltpu.HBM) for s in srcs], *lands, *deps)
    return dict(n=n, sems=res[:2 * n], srcs=res[2 * n:3 * n], lands=res[3 * n:4 * n], token=res[4 * n])


def _exchange_wait(h, after, name):
    n, ns = h['n'], len(h['srcs'])
    span = h.get('span', lambda t, ref: ref)

    def body(*refs):
        lnd = refs[ns:ns + n]
        sems = refs[ns + n:ns + 3 * n]
        x, y, c = lax.axis_index("x"), lax.axis_index("y"), lax.axis_index("c")
        for t in range(n):
            w = span(t, lnd[t])
            cp = pltpu.make_async_remote_copy(src_ref=w, dst_ref=w, send_sem=sems[2 * t], recv_sem=sems[2 * t + 1],
                                              device_id=(x, y, 1 - c), device_id_type=pl.DeviceIdType.MESH)
            cp.wait_send()
            cp.wait_recv()

    after = list(after)
    res = pl.pallas_call(
        body, name=name,
        in_specs=[_HBM] * (ns + n) + [_SEM] * (2 * n) + [pl.BlockSpec(memory_space=pl.ANY)] * len(after),
        out_specs=[_HBM] * (ns + n),
        out_shape=[pltpu.HBM(s.shape, s.dtype) for s in h['srcs']] + [pltpu.HBM(s.shape, s.dtype) for s in h['lands']],
        input_output_aliases={t: t for t in range(ns + n)},
        compiler_params=pltpu.CompilerParams(has_side_effects=_EFFECT),
    )(*h['srcs'], *h['lands'], *h['sems'], *after)
    return list(res[ns:])


def _sibling_start(lands, win, name):
    n = len(lands)

    def body(*refs):
        lnd = refs[:n]
        sems = refs[n:3 * n]
        token = refs[4 * n]
        x, y, c = lax.axis_index("x"), lax.axis_index("y"), lax.axis_index("c")
        for ox in (x, 1 - x):
            for oy in (y, 1 - y):
                owner = 4 * ox + 2 * oy + c
                for t in range(n):
                    w = win(t, lnd[t], owner)
                    pltpu.make_async_remote_copy(
                        src_ref=w, dst_ref=w, send_sem=sems[2 * t], recv_sem=sems[2 * t + 1],
                        device_id=(x, y, 1 - c), device_id_type=pl.DeviceIdType.MESH).start()
        token[...] = jnp.zeros_like(token)

    res = pl.pallas_call(
        body, name=name, in_specs=[_HBM] * n,
        out_specs=[_SEM] * (2 * n) + [_HBM] * n + [pl.BlockSpec(memory_space=pltpu.VMEM)],
        out_shape=[pltpu.SemaphoreType.DMA(())] * (2 * n) + [pltpu.HBM(s.shape, s.dtype) for s in lands]
        + [jax.ShapeDtypeStruct((8, 128), F32)],
        input_output_aliases={t: 2 * n + t for t in range(n)},
        compiler_params=pltpu.CompilerParams(has_side_effects=_EFFECT),
    )(*lands)
    return dict(n=n, sems=res[:2 * n], srcs=[], lands=res[2 * n:3 * n], token=res[3 * n])


def _gather_start(shards, names, tag, deps=(), two_level=False):
    xs = [shards[nm] for nm in names]
    axes = [_W_AXIS[nm] for nm in names]
    widths = [x.shape[ax] for x, ax in zip(xs, axes)]
    shapes = [(tuple(d * (N_DEV if a == ax else 1) for a, d in enumerate(x.shape)), x.dtype) for x, ax in zip(xs, axes)]
    src = lambda t, ref, peer: ref
    dst = lambda t, ref, who: _window(ref, axes[t], who, widths[t])
    h = _exchange_start(xs, shapes, src, dst, "gw_start_" + tag, deps, _SAME_CORE if two_level else _ALL_PEERS)
    h['tag'] = "gw_wait_" + tag
    if two_level:
        h['span'] = lambda t, ref: _window(ref, axes[t], 0, widths[t], len(_SAME_CORE))
        h['second'] = (dst, "gw_pass_" + tag, "gw_passed_" + tag)
    return h


def _scatter_start(grads, names, tag):
    xs = [grads[nm] for nm in names]
    axes = [_W_AXIS[nm] for nm in names]
    widths = [x.shape[ax] // N_DEV for x, ax in zip(xs, axes)]
    shapes = [((N_DEV,) + tuple(d // (N_DEV if a == ax else 1) for a, d in enumerate(x.shape)), x.dtype)
              for x, ax in zip(xs, axes)]
    src = lambda t, ref, peer: _window(ref, axes[t], peer, widths[t])
    dst = lambda t, ref, who: ref.at[who]
    h = _exchange_start(xs, shapes, src, dst, "sg_start_" + tag)
    h['tag'] = "sg_wait_" + tag
    return h


def _finish(h, after):
    return _exchange_wait(h, after, h['tag'])


class _LayerWeights:
    def __init__(self, groups, started):
        self.groups = groups
        self.started = started
        self.got = {}
        self.after = None

    def __getitem__(self, nm):
        if nm not in self.got:
            for names, h in self.groups:
                if nm in names:
                    self.got.update(zip(names, _finish(h, list(self.after) + self.started)))
        return self.got[nm]


def _adam_math(p_ref, w, m, v):
    g = p_ref[0].astype(F32)
    for s in range(1, p_ref.shape[0]):
        g = g + p_ref[s].astype(F32)
    bc1 = 1.0 / (1.0 - ADAM_B1 ** ADAM_STEP)
    bc2 = 1.0 / (1.0 - ADAM_B2 ** ADAM_STEP)
    mm = ADAM_B1 * m + (1.0 - ADAM_B1) * g
    vv = ADAM_B2 * v + (1.0 - ADAM_B2) * jnp.square(g)
    return g, -ADAM_LR * ((mm * bc1) / (jnp.sqrt(vv * bc2) + ADAM_EPS) + ADAM_WD * w), mm, vv


def _adam(parts, w, m, v, name, tr=256):
    P, R, C = parts.shape
    tr = min(tr, R)
    assert R % tr == 0

    def body(p_ref, w_ref, m_ref, v_ref, *outs):
        for o_ref, val in zip(outs, _adam_math(p_ref, w_ref[...], m_ref[...], v_ref[...])):
            o_ref[...] = val

    spec = pl.BlockSpec((tr, C), lambda i: (i, 0))
    return pl.pallas_call(
        body, name=name, grid=(R // tr,),
        in_specs=[pl.BlockSpec((P, tr, C), lambda i: (0, i, 0)), spec, spec, spec],
        out_specs=[spec] * 4, out_shape=[jax.ShapeDtypeStruct((R, C), F32)] * 4,
        compiler_params=_cparams(("parallel",)),
    )(parts, w, m, v)


def _adam_layer(parts, w, m, v, l, prev, name, tr=256):
    P, R, C = parts.shape
    tr = min(tr, R)
    assert R % tr == 0 and w.shape == (DEPTH, R, C)
    npv = 0 if prev is None else 4

    def body(p_ref, w_ref, m_ref, v_ref, *rest):
        for o_ref, val in zip(rest[npv:], _adam_math(p_ref, w_ref[0], m_ref[0], v_ref[0])):
            o_ref[0] = val

    spec = pl.BlockSpec((1, tr, C), lambda i: (l, i, 0))
    return pl.pallas_call(
        body, name=name, grid=(R // tr,),
        in_specs=[pl.BlockSpec((P, tr, C), lambda i: (0, i, 0)), spec, spec, spec]
        + [pl.BlockSpec(memory_space=pl.ANY)] * npv,
        out_specs=[spec] * 4, out_shape=[jax.ShapeDtypeStruct((DEPTH, R, C), F32)] * 4,
        input_output_aliases={4 + t: t for t in range(npv)},
        compiler_params=_cparams(("parallel",)),
    )(parts, w, m, v, *([] if prev is None else prev))


def _mod_partial(cact_all, w_ada, l, b_ada_l):
    R, (_, D, n) = cact_all.shape[0], w_ada.shape

    def body(c_ref, w_ref, b_ref, o_ref):
        o_ref[...] = _dot(c_ref[...].astype(BF16), w_ref[...].astype(BF16)) + b_ref[...]

    return pl.pallas_call(
        body, name="mod_partial", grid=(1,), out_shape=jax.ShapeDtypeStruct((R, n), F32),
        in_specs=[pl.BlockSpec((R, D), lambda i: (0, 0)), pl.BlockSpec((None, D, n), lambda i: (l, 0, 0)),
                  pl.BlockSpec((1, n), lambda i: (0, 0))],
        out_specs=pl.BlockSpec((R, n), lambda i: (0, 0)),
        compiler_params=pltpu.CompilerParams(vmem_limit_bytes=VMEM_LIMIT),
    )(cact_all, w_ada, b_ada_l)


def _ada_grad(cact_t, dmod):
    D, n = cact_t.shape[0], dmod.shape[1]

    def body(c_ref, d_ref, o_ref):
        ct = c_ref[...].astype(BF16).astype(F32)
        dm = d_ref[...].astype(BF16).astype(F32)
        acc = ct[:, 0:1] * dm[0:1, :]
        for b in range(1, N_DEV):
            acc = acc + ct[:, b:b + 1] * dm[b:b + 1, :]
        o_ref[0] = acc

    return pl.pallas_call(
        body, name="ada_grad", out_shape=jax.ShapeDtypeStruct((1, D, n), F32),
        compiler_params=pltpu.CompilerParams(vmem_limit_bytes=VMEM_LIMIT),
    )(cact_t, dmod)


def _norm_mod(x, r, gv, sh):
    return x * r * gv + sh


def _silu(x):
    return x * _sigmoid(x)


def _rstd(x):
    return lax.rsqrt(jnp.mean(x * x, axis=-1, keepdims=True) + EPS)


def _residual_epi(acc, x, g):
    xn = x + g * acc
    return acc, xn, _rstd(xn)


def _residual_norm_epi(acc, x, g, gv, sh):
    xn = x + g * acc
    r = _rstd(xn)
    return acc, xn, _norm_mod(xn, r, gv, sh), r


def _layer_fwd(x0, r1, h1, mod, gn1, gn2, nxt, W, rc, tri):
    S = x0.shape[0]
    sh1, sc1, g1m, sh2, sc2, g2m = [mod[i:i + 1] for i in range(N_MOD)]
    gv1 = gn1 * (1.0 + sc1)
    gv2 = gn2 * (1.0 + sc2)
    W.after = [h1]
    (p,) = _mm(h1, W["w_in"], tm=2048, outs=(BF16,), name="mm_in")
    yret, rstate = _ret_fwd(p, rc)
    ysb, sbc = _sb_fwd(p, tri)
    W.after = [ysb]
    (ya,) = _mm(yret, W["w_ret_out"], tm=512, a_ex=[(p, 'a', O_RG)],
                pro=lambda yr, g: _silu(g.astype(F32)) * yr.astype(F32), outs=(BF16,), name="mm_ret_out")
    yb, mg = _mm(ysb, W["w_sb_out"], o_ex=[(ya, 'o'), (p, 'o', O_GA), (p, 'o', O_GB)],
                 epi=lambda acc, a, ga, gb: (acc, _sigmoid(ga.astype(F32)) * a.astype(F32)
                                             + _sigmoid(gb.astype(F32)) * acc),
                 outs=(BF16, BF16), name="mm_sb_out")
    mo, x1, h2, r2 = _mm(mg, W["w_mix_out"], tm=512, o_ex=[(x0, 'o'), (g1m, 'n'), (gv2, 'n'), (sh2, 'n')],
                         epi=_residual_norm_epi, outs=(BF16, F32, BF16), cols=1, name="mm_mix_out")
    (act,) = _mm(h2, W["w_up"], tm=2048, epi=lambda acc: (jnp.maximum(acc, 0.0),), outs=(BF16,), name="mm_up")
    if nxt is None:
        dn, x2, r_out = _mm(act, W["w_down"], tm=512, pro=lambda a: a * a, o_ex=[(x1, 'o'), (g2m, 'n')],
                            epi=_residual_epi, outs=(BF16, F32), cols=1, name="mm_down_last")
        h_out = None
    else:
        dn, x2, h_out, r_out = _mm(act, W["w_down"], tm=512, pro=lambda a: a * a,
                                   o_ex=[(x1, 'o'), (g2m, 'n'), (nxt[0], 'n'), (nxt[1], 'n')],
                                   epi=_residual_norm_epi, outs=(BF16, F32, BF16), cols=1, name="mm_down")
    saved = dict(x0=x0, r1=r1, h1=h1, p=p, yret=yret, rstate=rstate, ysb=ysb, sbc=sbc, ya=ya, yb=yb, mg=mg, mo=mo, x1=x1,
                 r2=r2, h2=h2, act=act, dn=dn, gv1=gv1, gv2=gv2, mod=mod, gn1=gn1, gn2=gn2)
    return x2, r_out, h_out, saved


def _norm_bwd(dh, x, r, dres, gv, gn, extra_rows=(), extra_vecs=(), extra_fn=None, extra_outs=(), name="norm_bwd"):
    D = x.shape[1]
    ne = len(extra_rows)

    def fn(dh_t, x_t, dres_t, *rest):
        er, rest = rest[:ne], rest[ne:]
        gv_t = rest[0]
        ev, r_t = rest[1:-1], rest[-1]
        xh = x_t * r_t
        dxh = dh_t * gv_t
        dx = r_t * (dxh - xh * jnp.mean(dxh * xh, axis=-1, keepdims=True)) + dres_t
        base = (dx, dh_t, dh_t * xh)
        if extra_fn is None:
            return base
        return base + tuple(extra_fn(dx, *er, *ev))

    return _ew(fn, [dh, x, dres] + list(extra_rows), vecs=[gv] + list(extra_vecs), cols=[r],
               outs=[('row', D, F32), ('sum', D), ('sum', D)] + list(extra_outs), name=name)


def _gate_bwd(dx, dn, g):
    return dx * dn.astype(F32), dx * g


_GATE_OUTS = [('sum', D_MODEL), ('row', D_MODEL, BF16)]


def _layer_bwd(dx2, d_g2m, d_dn, sv, below, W, rc, tri, emit):
    mod = sv['mod']
    sh1, sc1, g1m, sh2, sc2, g2m = [mod[i:i + 1] for i in range(N_MOD)]
    D = D_MODEL
    p = sv['p']
    (d_up,) = _mm(d_dn, W["w_down"], tb=True, tm=2048, o_ex=[(sv['act'], 'o')],
                  epi=lambda acc, a: (acc * 2.0 * a.astype(F32),), outs=(BF16,), name="mm_down_dx")
    (gw_down,) = _mm(sv['act'], d_dn, ta=True, tk=DW_TK, pro=lambda a: a * a, outs=(BF16,), name="mm_down_dw")
    (gw_up,) = _mm(sv['h2'], d_up, ta=True, tk=DW_TK, outs=(BF16,), name="mm_up_dw")
    tok = emit(dict(w_down=gw_down, w_up=gw_up), "mlp")
    (d_h2,) = _mm(d_up, W["w_up"], tb=True, tm=2048, tk=2048, outs=(F32,), name="mm_up_dx", deps=[tok])
    dx1, d_sh2, s_h2, d_g1m, d_mo = _norm_bwd(
        d_h2, sv['x1'], sv['r2'], dx2, sv['gv2'], sv['gn2'],
        extra_rows=[sv['mo']], extra_vecs=[g1m],
        extra_fn=lambda dx, mo, g: (dx * mo.astype(F32), dx * g),
        extra_outs=[('sum', D), ('row', D, BF16)], name="norm_bwd_mlp")
    d_sc2 = sv['gn2'] * s_h2
    d_gn2 = (1.0 + sc2) * s_h2
    def mix_epi(acc, ya, yb, ga, gb):
        sa, sb = _sigmoid(ga.astype(F32)), _sigmoid(gb.astype(F32))
        return (acc * sa, acc * sb, acc * ya.astype(F32) * sa * (1.0 - sa), acc * yb.astype(F32) * sb * (1.0 - sb))

    d_ya, d_yb, d_ga, d_gb = _mm(d_mo, W["w_mix_out"], tb=True, tm=512,
                                 o_ex=[(sv['ya'], 'o'), (sv['yb'], 'o'), (p, 'o', O_GA), (p, 'o', O_GB)], epi=mix_epi,
                                 outs=(BF16,) * 4, name="mm_mix_dx")
    (gw_mix,) = _mm(sv['mg'], d_mo, ta=True, tk=DW_TK, outs=(BF16,), name="mm_mix_dw")

    def ro_epi(acc, g, yr):
        gf = g.astype(F32)
        s = _sigmoid(gf)
        return (acc * yr.astype(F32) * s * (1.0 + gf * (1.0 - s)), acc * gf * s)

    d_rg, d_yret = _mm(d_ya, W["w_ret_out"], tb=True, tm=512, tn=2048, o_ex=[(p, 'o', O_RG), (sv['yret'], 'o')],
                       epi=ro_epi, outs=(BF16, BF16), name="mm_ret_dx")
    (gw_ro,) = _mm(sv['yret'], d_ya, ta=True, tm=512, tk=DW_TK, a_ex=[(p, 'a', O_RG)],
                   pro=lambda yr, g: _silu(g.astype(F32)) * yr.astype(F32), outs=(BF16,), name="mm_ret_dw")
    (gw_so,) = _mm(sv['ysb'], d_yb, ta=True, tk=DW_TK, outs=(BF16,), name="mm_sb_dw")
    tok = emit(dict(w_mix_out=gw_mix, w_ret_out=gw_ro, w_sb_out=gw_so), "mix")
    (d_ysb,) = _mm(d_yb, W["w_sb_out"], tb=True, outs=(BF16,), name="mm_sb_dx", deps=[tok])
    d_sq, d_sk, d_sv = _sb_bwd(p, sv['sbc'], d_ysb, tri)
    d_ret = _ret_bwd(p, sv['rstate'], d_yret, rc)
    dp = [d_ret, d_rg, d_sq, d_sk, d_sv, d_ga, d_gb]
    (gw_in,) = _mm(sv['h1'], dp, ta=True, tk=1024, outs=(BF16,), name="mm_in_dw")
    tok = emit(dict(w_in=gw_in), "in")
    (d_h,) = _mm(dp, W["w_in"], tb=True, tm=2048, tk=512, outs=(F32,), name="mm_in_dx", deps=[tok])
    if below is None:
        dx0, d_sh1, s_h1 = _norm_bwd(d_h, sv['x0'], sv['r1'], dx1, sv['gv1'], sv['gn1'], name="norm_bwd_mix")
        gate_below = (None, None)
    else:
        dx0, d_sh1, s_h1, *gate_below = _norm_bwd(
            d_h, sv['x0'], sv['r1'], dx1, sv['gv1'], sv['gn1'], extra_rows=[below['dn']],
            extra_vecs=[below['mod'][N_MOD - 1:N_MOD]], extra_fn=_gate_bwd, extra_outs=_GATE_OUTS,
            name="norm_bwd_mix_gate")
    d_sc1 = sv['gn1'] * s_h1
    d_gn1 = (1.0 + sc1) * s_h1
    d_mod = jnp.concatenate([d_sh1, d_sc1, d_g1m, d_sh2, d_sc2, d_g2m], axis=1)
    return dx0, gate_below, d_mod, d_gn1, d_gn2


def kernel(x, c, norm_mix_g, w_in, w_ret_out, w_sb_out, w_mix_out, norm_mlp_g, w_up, w_down, w_ada, b_ada, final_g, loss_target, m_norm_mix_g, m_w_in, m_w_ret_out, m_w_sb_out, m_w_mix_out, m_norm_mlp_g, m_w_up, m_w_down, m_w_ada, m_b_ada, m_final_g, v_norm_mix_g, v_w_in, v_w_ret_out, v_w_sb_out, v_w_mix_out, v_norm_mlp_g, v_w_up, v_w_down, v_w_ada, v_b_ada, v_final_g):
    S, D = x.shape[1], x.shape[2]
    x0 = x.reshape(S, D)
    tgt = loss_target.reshape(S, D)
    me = 4 * lax.axis_index("x") + 2 * lax.axis_index("y") + lax.axis_index("c")
    wts = dict(w_in=w_in, w_ret_out=w_ret_out, w_sb_out=w_sb_out, w_mix_out=w_mix_out, w_up=w_up, w_down=w_down)
    mts = dict(w_in=m_w_in, w_ret_out=m_w_ret_out, w_sb_out=m_w_sb_out, w_mix_out=m_w_mix_out, w_up=m_w_up, w_down=m_w_down)
    vts = dict(w_in=v_w_in, w_ret_out=v_w_ret_out, w_sb_out=v_w_sb_out, w_mix_out=v_w_mix_out, w_up=v_w_up, w_down=v_w_down)
    rc = _ret_consts(S)
    tri = _tri()

    (cact,) = _ew(lambda t: (_silu(t),), [jnp.pad(c, ((0, 7), (0, 0)))], outs=[('row', D, F32)], name="silu_c")
    (cact_all,) = _all_gather_lead([cact[0:1]], "gather_c")
    cact_all = cact_all.reshape(N_DEV, D)
    cact16 = jnp.pad(cact_all, ((0, 8), (0, 0)))
    n_ada = w_ada.shape[2]
    b_loc = lax.dynamic_slice_in_dim(b_ada, me * n_ada, n_ada, axis=1)
    mods = [_mod_partial(cact16, w_ada, l, b_loc[l:l + 1])[:N_DEV] for l in range(DEPTH)]
    modp = jnp.stack(mods, axis=1)
    (modr,) = _all_to_all_lead([modp], "scatter_mod")
    mod_full = jnp.transpose(modr, (1, 0, 2)).reshape(DEPTH, N_MOD, D)

    def cast(nm, l, deps=()):
        w = wts[nm]
        _, R, C = w.shape
        tr = min(256, R)

        def body(w_ref, *rest):
            rest[-1][...] = w_ref[...].astype(BF16)

        return pl.pallas_call(
            body, name="cast_bf16", grid=(R // tr,),
            in_specs=[pl.BlockSpec((None, tr, C), lambda i: (l, i, 0))]
            + [pl.BlockSpec(memory_space=pl.ANY)] * len(deps),
            out_specs=pl.BlockSpec((tr, C), lambda i: (i, 0)), out_shape=jax.ShapeDtypeStruct((R, C), BF16),
            compiler_params=_cparams(("arbitrary",)),
        )(w, *deps)

    rest = tuple(nm for nm in _W_NAMES if nm != "w_in")
    sh = [{} for _ in range(DEPTH)]
    sh[0]["w_in"] = cast("w_in", 0)
    h_in = _gather_start(sh[0], ("w_in",), "0_in", [modr], two_level=True)
    in_flight = [h_in['token']]
    for l in range(DEPTH):
        for nm in _W_NAMES:
            if nm not in sh[l]:
                sh[l][nm] = cast(nm, l, in_flight)

    pre = [(norm_mix_g[l:l + 1] * (1.0 + mod_full[l][1:2]), mod_full[l][0:1]) for l in range(DEPTH)]

    def first(t, gv, sh):
        r = _rstd(t)
        return r, _norm_mod(t, r, gv, sh)

    xs = x0
    rs, hs = _ew(first, [x0], vecs=list(pre[0]), outs=[('col', F32), ('row', D, BF16)], name="row_rstd",
                 deps=in_flight)

    win, pass_name, passed_name = h_in['second']
    before_wait = [hs] + [sh[l][nm] for l in range(DEPTH) for nm in _W_NAMES if (l, nm) != (0, "w_in")]
    h_pass = _sibling_start(_exchange_wait(h_in, before_wait, h_in['tag']), win, pass_name)
    h_pass['span'] = h_in['span']
    started = [h_pass['token']]
    layer_groups = []
    for l, groups in enumerate([[(rest, "0_rest")]] + [[(_W_NAMES, "%d_all" % l)] for l in range(1, DEPTH)]):
        layer_groups.append([])
        for names, tag in groups:
            layer_groups[-1].append((names, _gather_start(sh[l], names, tag, started[-1:])))
            started.append(layer_groups[-1][-1][1]['token'])
    (w_in0,) = _exchange_wait(h_pass, started, passed_name)
    layer_w = [_LayerWeights(g, started) for g in layer_groups]
    layer_w[0].got["w_in"] = w_in0
    saved = []
    for l in range(DEPTH):
        xs, rs, hs, sv = _layer_fwd(xs, rs, hs, mod_full[l], norm_mix_g[l:l + 1], norm_mlp_g[l:l + 1],
                                    pre[l + 1] if l + 1 < DEPTH else None, layer_w[l], rc, tri)
        sv['W'] = layer_w[l]
        saved.append(sv)

    fg = final_g.reshape(1, D)

    def head(xt, tg, dn, g, g2m, r):
        xh = xt * r
        e = xh * g - tg
        dy = e * (1.0 / D)
        dxh = dy * g
        dx = r * (dxh - xh * jnp.mean(dxh * xh, axis=-1, keepdims=True))
        return (dx, dy * xh, 0.5 * e * e * (1.0 / D)) + _gate_bwd(dx, dn, g2m)

    top = saved[DEPTH - 1]
    dxs, d_fg, loss_cols, *gate = _ew(head, [xs, tgt, top['dn']], vecs=[fg, top['mod'][N_MOD - 1:N_MOD]], cols=[rs],
                                      outs=[('row', D, F32), ('sum', D), ('sum', D)] + _GATE_OUTS, name="loss_head")

    small = [None] * DEPTH
    pending = []
    for l in reversed(range(DEPTH)):
        sv = saved[l]

        def emit(gw, tag, l=l):
            names = tuple(gw)
            pending.append((l, names, _scatter_start(gw, names, "%d_%s" % (l, tag))))
            return pending[-1][2]['token']

        dxs, gate, d_mod, d_gn1, d_gn2 = _layer_bwd(dxs, gate[0], gate[1], sv, saved[l - 1] if l else None,
                                                    sv['W'], rc, tri, emit)
        small[l] = (d_mod, d_gn1, d_gn2)
    grad_x = dxs.reshape(1, S, D)

    res = {}
    after = [dxs]
    for l, names, h in pending:
        for nm, landed in zip(names, _finish(h, after)):
            res[nm] = _adam_layer(landed, wts[nm], mts[nm], vts[nm], l, res.get(nm), "adam_layer")
        after = [res[names[-1]][0]]

    pack = jnp.concatenate([small[l][0] for l in range(DEPTH)] + [small[l][1] for l in range(DEPTH)]
                           + [small[l][2] for l in range(DEPTH)] + [d_fg, loss_cols], axis=1)
    (packs,) = _all_gather_lead([pack], "gather_small", deps=after)
    packs = packs.reshape(N_DEV, -1)
    o = 0
    dmod_all = []
    for l in range(DEPTH):
        dmod_all.append(packs[:, o:o + N_MOD * D]); o += N_MOD * D
    gn1_parts = packs[:, o:o + DEPTH * D].reshape(N_DEV, DEPTH, D); o += DEPTH * D
    gn2_parts = packs[:, o:o + DEPTH * D].reshape(N_DEV, DEPTH, D); o += DEPTH * D
    fg_parts = packs[:, o:o + D].reshape(N_DEV, 1, D); o += D
    loss_parts = packs[:, o:o + D]
    (loss_sum,) = _ew(lambda t: (t,), [loss_parts], outs=[('sum', D)], name="loss_sum")
    loss = jnp.sum(loss_sum)

    res["norm_mix_g"] = _adam(gn1_parts, norm_mix_g, m_norm_mix_g, v_norm_mix_g, "adam")
    res["norm_mlp_g"] = _adam(gn2_parts, norm_mlp_g, m_norm_mlp_g, v_norm_mlp_g, "adam")
    fgr = _adam(fg_parts, fg, m_final_g.reshape(1, D), v_final_g.reshape(1, D), "adam")
    res["final_g"] = [t.reshape(D) for t in fgr]
    bparts = jnp.stack(dmod_all, axis=1)
    res["b_ada"] = _adam(bparts, b_ada, m_b_ada, v_b_ada, "adam")
    cact_t = cact_all.T
    for l in range(DEPTH):
        dm_loc = lax.dynamic_slice_in_dim(dmod_all[l], me * n_ada, n_ada, axis=1)
        res["w_ada"] = _adam_layer(_ada_grad(cact_t, dm_loc), w_ada, m_w_ada, v_w_ada, l, res.get("w_ada"),
                                   "adam_layer")

    order = ['norm_mix_g', 'w_in', 'w_ret_out', 'w_sb_out', 'w_mix_out', 'norm_mlp_g', 'w_up', 'w_down', 'w_ada', 'b_ada', 'final_g']
    out = [loss, grad_x]
    for i in range(4):
        out += [res[nm][i] for nm in order]
    return tuple(out)
```

```python
import functools
import math

import jax
import jax.numpy as jnp
import numpy as np
from jax import lax
from jax.experimental import pallas as pl
from jax.experimental.pallas import tpu as pltpu

F32 = jnp.float32
BF16 = jnp.bfloat16

N_DEV = 8
D_MODEL = 1024
DEPTH = 2
RET_HEADS = 4
RET_QK = 256
RET_V = 512
RET_CHUNK = 256
ROPE_BASE = 10000.0
SB_HEADS = 16
SB_DIM = 64
D_FF = 4096
N_MOD = 6
EPS = 1e-6
GN_EPS = 1e-5
O_RQ, O_RK, O_RV, O_RG, O_SQ, O_SK, O_SV, O_GA, O_GB = 0, 1024, 2048, 4096, 6144, 7168, 8192, 9216, 10240
IN_W = 11264

ADAM_LR, ADAM_B1, ADAM_B2, ADAM_EPS, ADAM_WD, ADAM_STEP = 0.001, 0.9, 0.999, 1e-08, 0.01, 10

VMEM_LIMIT = 56 * 1024 * 1024
DW_TK = 2048


def _cparams(sem):
    return pltpu.CompilerParams(dimension_semantics=sem, vmem_limit_bytes=VMEM_LIMIT)


def _mm(a, b, *, ta=False, tb=False, tm=1024, tn=1024, tk=None, a_ex=(), pro=None, o_ex=(), epi=None,
        outs=(F32,), cols=0, name, deps=()):
    a_parts = list(a) if isinstance(a, (list, tuple)) else [a]
    b_parts = list(b) if isinstance(b, (list, tuple)) else [b]
    assert not (ta and len(a_parts) > 1) and not (tb and len(b_parts) > 1)
    if ta:
        K, M = a.shape
    else:
        M, K = a_parts[0].shape[0], sum(t.shape[1] for t in a_parts)
    N = b.shape[0] if tb else sum(t.shape[1] for t in b_parts)
    tm, tn, tk = min(tm, M), min(tn, N), K if tk is None else min(tk, K)
    assert M % tm == 0 and N % tn == 0 and K % tk == 0, (name, M, N, K, tm, tn, tk)
    nk = K // tk
    multi = len(a_parts) > 1 or len(b_parts) > 1
    direct = epi is None and tuple(outs) == (F32,) and cols == 0

    def ranges(parts, t):
        out, o = [], 0
        for arr in parts:
            assert arr.shape[1] % t == 0
            out.append((o, o + arr.shape[1] // t))
            o += arr.shape[1] // t
        return out

    a_rng = ranges(a_parts, tk) if len(a_parts) > 1 else [(0, nk)]
    b_rng = ranges(b_parts, tn) if len(b_parts) > 1 else [(0, N // tn)]
    clip = lambda v, lo, hi: jnp.clip(v - lo, 0, hi - lo - 1)
    mine = lambda v, lo, hi, w: jnp.where((v >= lo) & (v < hi), w, 0)
    in_specs, args = [], []
    for arr, (lo, hi) in zip(a_parts, a_rng):
        in_specs.append(pl.BlockSpec((tk, tm), lambda i, j, k: (k, i)) if ta
                        else pl.BlockSpec((tm, tk), lambda i, j, k, lo=lo, hi=hi: (i, clip(k, lo, hi))))
        args.append(arr)
    for arr, (lo, hi) in zip(b_parts, b_rng):
        in_specs.append(pl.BlockSpec((tn, tk), lambda i, j, k: (j, k)) if tb
                        else pl.BlockSpec((tk, tn), lambda i, j, k, lo=lo, hi=hi: (mine(j, lo, hi, k), clip(j, lo, hi))))
        args.append(arr)
    npa, npb = len(a_parts), len(b_parts)
    for arr, kind, *off in a_ex:
        off = off[0] if off else 0
        if kind == 'a' and ta:
            assert off % tm == 0
            in_specs.append(pl.BlockSpec((tk, tm), lambda i, j, k, o=off // tm: (k, o + i)))
        elif kind == 'a':
            assert off % tk == 0
            in_specs.append(pl.BlockSpec((tm, tk), lambda i, j, k, o=off // tk: (i, o + k)))
        elif kind == 'k':
            in_specs.append(pl.BlockSpec((tk, 1), lambda i, j, k: (k, 0)) if ta
                            else pl.BlockSpec((1, tk), lambda i, j, k: (0, k)))
        else:
            in_specs.append(pl.BlockSpec((1, tm), lambda i, j, k: (0, i)) if ta
                            else pl.BlockSpec((tm, 1), lambda i, j, k: (i, 0)))
        args.append(arr)
    for arr, kind, *off in o_ex:
        off = off[0] if off else 0
        if kind == 'o':
            assert off % tn == 0
            in_specs.append(pl.BlockSpec((tm, tn), lambda i, j, k, o=off // tn: (i, o + j)))
        elif kind == 'n':
            in_specs.append(pl.BlockSpec((1, tn), lambda i, j, k: (0, j)))
        else:
            in_specs.append(pl.BlockSpec((tm, 1), lambda i, j, k: (i, 0)))
        args.append(arr)
    for arr in deps:
        in_specs.append(pl.BlockSpec(memory_space=pl.ANY))
        args.append(arr)
    assert cols == 0 or N == tn
    na, no, nout, nd = len(a_ex), len(o_ex), len(outs) + cols, len(deps)
    dims = (((0 if ta else 1,), (1 if tb else 0,)), ((), ()))

    def body(*refs):
        a_refs, b_refs = refs[:npa], refs[npa:npa + npb]
        n0 = npa + npb
        aex = refs[n0:n0 + na]
        oex = refs[n0 + na:n0 + na + no]
        out_refs = refs[n0 + na + no + nd:n0 + na + no + nd + nout]

        def product(a_ref, b_ref):
            at = a_ref[...]
            if pro is not None:
                at = pro(at, *[r[...] for r in aex])
            return lax.dot_general(at.astype(BF16), b_ref[...].astype(BF16), dims, preferred_element_type=F32)

        def finish(res):
            vals = epi(res, *[r[...] for r in oex]) if epi is not None else (res,)
            for o_ref, v in zip(out_refs, vals):
                o_ref[...] = v.astype(o_ref.dtype)

        if nk == 1 and not multi:
            finish(product(a_refs[0], b_refs[0]))
            return
        acc = out_refs[0] if direct else refs[-1]
        j, k = pl.program_id(1), pl.program_id(2)
        if multi:
            @pl.when(k == 0)
            def _():
                acc[...] = jnp.zeros_like(acc)

            for a_ref, (alo, ahi) in zip(a_refs, a_rng):
                for b_ref, (blo, bhi) in zip(b_refs, b_rng):
                    @pl.when((k >= alo) & (k < ahi) & (j >= blo) & (j < bhi))
                    def _():
                        acc[...] += product(a_ref, b_ref)
        else:
            @pl.when(k == 0)
            def _():
                acc[...] = product(a_refs[0], b_refs[0])

            @pl.when(k > 0)
            def _():
                acc[...] += product(a_refs[0], b_refs[0])

        if not direct:
            @pl.when(k == nk - 1)
            def _():
                finish(acc[...])

    res = pl.pallas_call(
        body, name=name, grid=(M // tm, N // tn, nk), in_specs=in_specs,
        out_specs=[pl.BlockSpec((tm, tn), lambda i, j, k: (i, j)) for _ in outs]
        + [pl.BlockSpec((tm, 1), lambda i, j, k: (i, 0))] * cols,
        out_shape=[jax.ShapeDtypeStruct((M, N), dt) for dt in outs] + [jax.ShapeDtypeStruct((M, 1), F32)] * cols,
        scratch_shapes=[pltpu.VMEM((tm, tn), F32)] if (nk > 1 or multi) and not direct else [],
        compiler_params=_cparams(("parallel", "parallel", "arbitrary")),
    )(*args)
    return res


def _ew(fn, rows, vecs=(), cols=(), outs=(), tr=256, name=None, deps=()):
    S = rows[0].shape[0]
    tr = min(tr, S)
    assert S % tr == 0
    in_specs, args = [], []
    for r in rows:
        in_specs.append(pl.BlockSpec((tr, r.shape[1]), lambda i: (i, 0)))
        args.append(r)
    for v in vecs:
        in_specs.append(pl.BlockSpec((1, v.shape[1]), lambda i: (0, 0)))
        args.append(v)
    for c in cols:
        in_specs.append(pl.BlockSpec((tr, 1), lambda i: (i, 0)))
        args.append(c)
    out_specs, out_shape = [], []
    for o in outs:
        if o[0] == 'row':
            out_specs.append(pl.BlockSpec((tr, o[1]), lambda i: (i, 0)))
            out_shape.append(jax.ShapeDtypeStruct((S, o[1]), o[2]))
        elif o[0] == 'sum':
            out_specs.append(pl.BlockSpec((1, o[1]), lambda i: (0, 0)))
            out_shape.append(jax.ShapeDtypeStruct((1, o[1]), F32))
        else:
            out_specs.append(pl.BlockSpec((tr, 1), lambda i: (i, 0)))
            out_shape.append(jax.ShapeDtypeStruct((S, 1), o[1]))
    nin = len(args)
    in_specs += [pl.BlockSpec(memory_space=pl.ANY)] * len(deps)
    args += list(deps)

    def body(*refs):
        i = pl.program_id(0)
        vals = fn(*[r[...] for r in refs[:nin]])
        for o, o_ref, v in zip(outs, refs[nin + len(deps):], vals):
            if o[0] == 'sum':
                @pl.when(i == 0)
                def _():
                    o_ref[...] = jnp.zeros_like(o_ref)
                o_ref[...] += jnp.sum(v.astype(F32), axis=0, keepdims=True)
            else:
                o_ref[...] = v.astype(o_ref.dtype)

    return pl.pallas_call(
        body, name=name, grid=(S // tr,), in_specs=in_specs, out_specs=out_specs, out_shape=out_shape,
        compiler_params=_cparams(("arbitrary",)),
    )(*args)


def _sigmoid(x):
    return 1.0 / (1.0 + jnp.exp(-x))


def _ret_consts(S):
    h = np.arange(RET_HEADS, dtype=np.float64)
    log_gamma = np.log1p(-np.power(2.0, -5.0 - h))
    idx = np.arange(RET_CHUNK, dtype=np.float64)
    rel = idx[:, None] - idx[None, :]
    decay = np.where(rel >= 0, np.exp(np.maximum(rel, 0.0) * log_gamma[:, None, None]), 0.0)
    xi = np.exp((idx + 1.0) * log_gamma[:, None])[:, :, None]
    zeta = np.exp((RET_CHUNK - 1.0 - idx) * log_gamma[:, None])[:, :, None]
    gamma_c = np.exp(RET_CHUNK * log_gamma)[:, None, None]
    half = RET_QK // 2
    inv_freq = np.power(ROPE_BASE, -np.arange(half, dtype=np.float64) / half).astype(np.float32)
    ang = np.arange(S, dtype=np.float32)[:, None] * inv_freq[None, :]
    f = lambda t: jnp.asarray(t, F32)
    return dict(decay=f(decay), xi=f(xi), zeta=f(zeta), gc=f(gamma_c), cos=f(np.cos(ang)), sin=f(np.sin(ang)))


def _rot(t, cos, sin):
    half = RET_QK // 2
    t1, t2 = t[:, :half], t[:, half:]
    return jnp.concatenate([t1 * cos - t2 * sin, t1 * sin + t2 * cos], axis=-1)


def _rot_inv(t, cos, sin):
    half = RET_QK // 2
    t1, t2 = t[:, :half], t[:, half:]
    return jnp.concatenate([t1 * cos + t2 * sin, t2 * cos - t1 * sin], axis=-1)


_NT = (((1,), (1,)), ((), ()))
_TN = (((0,), (0,)), ((), ()))


def _dot(a, b):
    return jnp.dot(a, b, preferred_element_type=F32)


def _dot_nt(a, b):
    return lax.dot_general(a, b, _NT, preferred_element_type=F32)


def _dot_tn(a, b):
    return lax.dot_general(a, b, _TN, preferred_element_type=F32)


_QW, _VW = RET_HEADS * RET_QK, RET_HEADS * RET_V
_HEADS = range(RET_HEADS)


def _ret_in_specs(C, rev, NC):
    n_of = (lambda n: NC - 1 - n) if rev else (lambda n: n)
    whole3 = lambda n: (0, 0, 0)
    return [
        pl.BlockSpec((C, _QW), lambda n: (n_of(n), O_RQ // _QW)),
        pl.BlockSpec((C, _QW), lambda n: (n_of(n), O_RK // _QW)),
        pl.BlockSpec((C, _VW), lambda n: (n_of(n), O_RV // _VW)),
        pl.BlockSpec((C, RET_QK // 2), lambda n: (n_of(n), 0)),
        pl.BlockSpec((C, RET_QK // 2), lambda n: (n_of(n), 0)),
        pl.BlockSpec((RET_HEADS, C, C), whole3),
        pl.BlockSpec((RET_HEADS, C, 1), whole3),
        pl.BlockSpec((RET_HEADS, C, 1), whole3),
        pl.BlockSpec((RET_HEADS, 1, 1), whole3),
    ]


def _qk_heads(q_ref, k_ref, cos, sin):
    qs, kfs = [], []
    for h in _HEADS:
        cols = slice(h * RET_QK, (h + 1) * RET_QK)
        qs.append(_rot(q_ref[:, cols].astype(F32), cos, sin).astype(BF16))
        kfs.append(_rot(k_ref[:, cols].astype(F32), cos, sin) * (RET_QK ** -0.5))
    return qs, kfs


def _ret_fwd(p, rc):
    S = p.shape[0]
    C = RET_CHUNK
    NC = S // C

    def body(q_ref, k_ref, v_ref, cos_ref, sin_ref, dec_ref, xi_ref, zeta_ref, gc_ref, y_ref, rs_ref, r_acc):
        n = pl.program_id(0)

        @pl.when(n == 0)
        def _():
            r_acc[...] = jnp.zeros_like(r_acc)

        cos, sin = cos_ref[...], sin_ref[...]
        qs, kfs = _qk_heads(q_ref, k_ref, cos, sin)
        vs = [v_ref[:, h * RET_V:(h + 1) * RET_V] for h in _HEADS]
        rbs = [r_acc[h].astype(BF16) for h in _HEADS]
        for h in _HEADS:
            rs_ref[h, 0] = rbs[h]
        ss = [(_dot_nt(qs[h], kfs[h].astype(BF16)) * dec_ref[h]).astype(BF16) for h in _HEADS]
        os = [_dot(ss[h], vs[h]) + _dot(qs[h], rbs[h]) * xi_ref[h] for h in _HEADS]
        for h in _HEADS:
            o = os[h]
            mu = jnp.mean(o, axis=-1, keepdims=True)
            var = jnp.mean(jnp.square(o - mu), axis=-1, keepdims=True)
            y_ref[:, h * RET_V:(h + 1) * RET_V] = ((o - mu) * lax.rsqrt(var + GN_EPS)).astype(y_ref.dtype)
        for h in _HEADS:
            kz = (kfs[h] * zeta_ref[h]).astype(BF16)
            r_acc[h] = r_acc[h] * gc_ref[h] + _dot_tn(kz, vs[h])

    return pl.pallas_call(
        body, name="ret_fwd", grid=(NC,), in_specs=_ret_in_specs(C, False, NC),
        out_specs=[pl.BlockSpec((C, _VW), lambda n: (n, 0)),
                   pl.BlockSpec((RET_HEADS, 1, RET_QK, RET_V), lambda n: (0, n, 0, 0))],
        out_shape=[jax.ShapeDtypeStruct((S, _VW), BF16),
                   jax.ShapeDtypeStruct((RET_HEADS, NC, RET_QK, RET_V), BF16)],
        scratch_shapes=[pltpu.VMEM((RET_HEADS, RET_QK, RET_V), F32)],
        compiler_params=_cparams(("arbitrary",)),
    )(p, p, p, rc['cos'], rc['sin'], rc['decay'], rc['xi'], rc['zeta'], rc['gc'])


def _ret_bwd(p, rstate, dy, rc):
    S = p.shape[0]
    C = RET_CHUNK
    NC = S // C

    def body(q_ref, k_ref, v_ref, cos_ref, sin_ref, dec_ref, xi_ref, zeta_ref, gc_ref, rs_ref, dy_ref,
             d_ref, dr_acc):
        dq_ref, dk_ref, dv_ref = d_ref.at[:, 0:_QW], d_ref.at[:, _QW:2 * _QW], d_ref.at[:, 2 * _QW:2 * _QW + _VW]
        t = pl.program_id(0)

        @pl.when(t == 0)
        def _():
            dr_acc[...] = jnp.zeros_like(dr_acc)

        cos, sin = cos_ref[...], sin_ref[...]
        qs, kfs = _qk_heads(q_ref, k_ref, cos, sin)
        ks = [kf.astype(BF16) for kf in kfs]
        vs = [v_ref[:, h * RET_V:(h + 1) * RET_V] for h in _HEADS]
        rbs = [rs_ref[h, 0] for h in _HEADS]
        ss = [(_dot_nt(qs[h], ks[h]) * dec_ref[h]).astype(BF16) for h in _HEADS]
        os = [_dot(ss[h], vs[h]) + _dot(qs[h], rbs[h]) * xi_ref[h] for h in _HEADS]
        dobs, doxis = [], []
        for h in _HEADS:
            o = os[h]
            mu = jnp.mean(o, axis=-1, keepdims=True)
            var = jnp.mean(jnp.square(o - mu), axis=-1, keepdims=True)
            rstd = lax.rsqrt(var + GN_EPS)
            yh = (o - mu) * rstd
            dyf = dy_ref[:, h * RET_V:(h + 1) * RET_V].astype(F32)
            do = (dyf - jnp.mean(dyf, axis=-1, keepdims=True)
                  - yh * jnp.mean(dyf * yh, axis=-1, keepdims=True)) * rstd
            dobs.append(do.astype(BF16))
            doxis.append((do * xi_ref[h]).astype(BF16))
        drbs = [dr_acc[h].astype(BF16) for h in _HEADS]
        dss = [(_dot_nt(dobs[h], vs[h]) * dec_ref[h]).astype(BF16) for h in _HEADS]
        for h in _HEADS:
            dq = _dot(dss[h], ks[h]) + _dot_nt(doxis[h], rbs[h])
            dq_ref[:, h * RET_QK:(h + 1) * RET_QK] = _rot_inv(dq, cos, sin).astype(dq_ref.dtype)
        for h in _HEADS:
            dk = _dot_tn(dss[h], qs[h]) + _dot_nt(vs[h], drbs[h]) * zeta_ref[h]
            dk_ref[:, h * RET_QK:(h + 1) * RET_QK] = (_rot_inv(dk, cos, sin) * (RET_QK ** -0.5)).astype(dk_ref.dtype)
        for h in _HEADS:
            kz = (kfs[h] * zeta_ref[h]).astype(BF16)
            dv = _dot_tn(ss[h], dobs[h]) + _dot(kz, drbs[h])
            dv_ref[:, h * RET_V:(h + 1) * RET_V] = dv.astype(dv_ref.dtype)
        for h in _HEADS:
            dr_acc[h] = dr_acc[h] * gc_ref[h] + _dot_tn(qs[h], doxis[h])

    rn = lambda n: NC - 1 - n
    in_specs = _ret_in_specs(C, True, NC) + [
        pl.BlockSpec((RET_HEADS, 1, RET_QK, RET_V), lambda n: (0, rn(n), 0, 0)),
        pl.BlockSpec((C, _VW), lambda n: (rn(n), 0)),
    ]
    return pl.pallas_call(
        body, name="ret_bwd", grid=(NC,), in_specs=in_specs,
        out_specs=pl.BlockSpec((C, 2 * _QW + _VW), lambda n: (rn(n), 0)),
        out_shape=jax.ShapeDtypeStruct((S, 2 * _QW + _VW), BF16),
        scratch_shapes=[pltpu.VMEM((RET_HEADS, RET_QK, RET_V), F32)],
        compiler_params=_cparams(("arbitrary",)),
    )(p, p, p, rc['cos'], rc['sin'], rc['decay'], rc['xi'], rc['zeta'], rc['gc'], rstate, dy)


SB_T = 256
SB_SCALE = SB_DIM ** -0.5


def _tri():
    j = np.arange(SB_T)
    after = (j[:, None] > j[None, :]).astype(np.float32)
    upto = (j[:, None] <= j[None, :]).astype(np.float32)
    return jnp.asarray(np.stack([after, upto]), BF16)


def _softplus_parts(z):
    neg_abs = lax.bitcast_convert_type(lax.bitcast_convert_type(z, jnp.uint32) | jnp.uint32(0x80000000), F32)
    e = jnp.exp(neg_abs)
    return jnp.maximum(z, 0.0) + jnp.log(1.0 + e), e


def _sb_fwd(p, tri):
    S = p.shape[0]
    T = min(SB_T, S)
    NQ = S // T
    assert NQ <= 128
    qb, kb, vb = O_SQ // 128, O_SK // 128, O_SV // 128

    def body(q_ref, k_ref, v_ref, tri_ref, o_ref, cs_ref, o_acc, run, zbuf, abuf):
        i = pl.program_id(1)
        lane = lax.broadcasted_iota(jnp.int32, (1, 128), 1)
        tri_after = tri_ref[0]
        qs = [jnp.where((lane >= 64) if hh else (lane < 64), q_ref[...], jnp.zeros_like(q_ref[...]))
              * jnp.asarray(SB_SCALE, BF16) for hh in range(2)]
        cs_ref[...] = jnp.zeros_like(cs_ref)
        o_acc[...] = jnp.zeros_like(o_acc)
        run[...] = jnp.zeros_like(run)

        def kv(ref, j):
            return ref[pl.ds(pl.multiple_of(j * T, T), T), :]

        for hh in range(2):
            zbuf[hh] = _dot_nt(qs[hh], kv(k_ref, i))

        def block(t, diagonal):
            j = i - t
            if diagonal:
                msk = lax.broadcasted_iota(jnp.int32, (T, T), 1) < lax.broadcasted_iota(jnp.int32, (T, T), 0)
            if not diagonal:
                av = [_dot(abuf[hh], kv(v_ref, j + 1)) for hh in range(2)]
            lss, exs, tot, zn = [], [], [], []
            for hh in range(2):
                z = zbuf[hh]
                sp, _ = _softplus_parts(z)
                lss.append(z - sp)
                if diagonal:
                    sp = jnp.where(msk, sp, 0.0)
                exs.append(_dot(sp.astype(BF16), tri_after))
                tot.append(sp[:, 0:1])
                zn.append(_dot_nt(qs[hh], kv(k_ref, jnp.maximum(j - 1, 0))))
            for hh in range(2):
                csl = slice(hh * 128, (hh + 1) * 128)
                cs = run[hh]
                a = jnp.exp(lss[hh] - exs[hh] - cs)
                if diagonal:
                    a = jnp.where(msk, a, 0.0)
                abuf[hh] = a.astype(BF16)
                cs_ref[:, csl] = jnp.where(lane == j, cs, cs_ref[:, csl])
                run[hh] = cs + exs[hh][:, 0:1] + tot[hh]
            for hh in range(2):
                if not diagonal:
                    o_acc[hh] += av[hh]
                zbuf[hh] = zn[hh]

        block(0, True)

        def step(t, carry):
            block(t, False)
            return carry

        lax.fori_loop(1, i + 1, step, 0)
        o_ref[...] = jnp.where(lane < 64, o_acc[0] + _dot(abuf[0], kv(v_ref, 0)),
                               o_acc[1] + _dot(abuf[1], kv(v_ref, 0))).astype(o_ref.dtype)

    return pl.pallas_call(
        body, name="sb_fwd", grid=(SB_HEADS // 2, NQ),
        scratch_shapes=[pltpu.VMEM((2, T, 128), F32), pltpu.VMEM((2, T, 1), F32), pltpu.VMEM((2, T, T), F32),
                        pltpu.VMEM((2, T, T), BF16)],
        in_specs=[pl.BlockSpec((T, 128), lambda h, i: (i, qb + h)),
                  pl.BlockSpec((S, 128), lambda h, i: (0, kb + h)),
                  pl.BlockSpec((S, 128), lambda h, i: (0, vb + h)),
                  pl.BlockSpec((1, T, T), lambda h, i: (0, 0, 0))],
        out_specs=[pl.BlockSpec((T, 128), lambda h, i: (i, h)),
                   pl.BlockSpec((T, 256), lambda h, i: (i, h))],
        out_shape=[jax.ShapeDtypeStruct((S, SB_HEADS * SB_DIM), BF16),
                   jax.ShapeDtypeStruct((S, SB_HEADS * 128), F32)],
        compiler_params=_cparams(("parallel", "arbitrary")),
    )(p, p, p, tri)


def _sb_bwd(p, carries, dy, tri):
    S = p.shape[0]
    T = min(SB_T, S)
    NQ = S // T
    qb, kb, vb = O_SQ // 128, O_SK // 128, O_SV // 128

    def body(q_ref, k_ref, v_ref, cs_ref, dy_ref, tri_ref, dq_ref, dk_ref, dv_ref, dk_acc, dv_acc, dq_acc, run,
             zbuf, dabuf, dzbuf, abuf):
        i = pl.program_id(1)

        @pl.when(i == 0)
        def _():
            dk_acc[...] = jnp.zeros_like(dk_acc)
            dv_acc[...] = jnp.zeros_like(dv_acc)

        lane = lax.broadcasted_iota(jnp.int32, (1, 128), 1)
        tri_after, tri_upto = tri_ref[0], tri_ref[1]
        hms = [(lane >= 64) if hh else (lane < 64) for hh in range(2)]
        qs = [jnp.where(hm, q_ref[...], jnp.zeros_like(q_ref[...])) * jnp.asarray(SB_SCALE, BF16) for hm in hms]
        dos = [jnp.where(hm, dy_ref[...], jnp.zeros_like(dy_ref[...])) for hm in hms]
        qst = [t.T for t in qs]
        dost = [t.T for t in dos]
        dq_acc[...] = jnp.zeros_like(dq_acc)
        run[...] = jnp.zeros_like(run)
        dzbuf[...] = jnp.zeros_like(dzbuf)
        abuf[...] = jnp.zeros_like(abuf)

        def kv(ref, j):
            return ref[pl.ds(pl.multiple_of(j * T, T), T), :]

        def flush(jp):
            kp = kv(k_ref, jp)
            dq_add = [_dot(dzbuf[hh], kp) for hh in range(2)]
            dk_add = _dot(qst[0], dzbuf[0]) + _dot(qst[1], dzbuf[1])
            dv_add = _dot(dost[0], abuf[0]) + _dot(dost[1], abuf[1])
            return dq_add, dk_add, dv_add

        def apply(jp, adds):
            dq_add, dk_add, dv_add = adds
            cols = pl.ds(pl.multiple_of(jp * T, T), T)
            for hh in range(2):
                dq_acc[hh] += dq_add[hh]
            dk_acc[:, cols] += dk_add
            dv_acc[:, cols] += dv_add

        for hh in range(2):
            zbuf[hh] = _dot_nt(qs[hh], kv(k_ref, 0))
            dabuf[hh] = _dot_nt(dos[hh], kv(v_ref, 0))

        def block(j, diagonal):
            jp = jnp.maximum(j - 1, 0)
            if diagonal:
                msk = lax.broadcasted_iota(jnp.int32, (T, T), 1) < lax.broadcasted_iota(jnp.int32, (T, T), 0)
            kp = kv(k_ref, jp)
            dq_add = [_dot(dzbuf[hh], kp) for hh in range(2)]
            sigs, lss, exs, zn, dan, dk_part, dv_part = [], [], [], [], [], [], []
            for hh in range(2):
                z = zbuf[hh]
                sp, _ = _softplus_parts(z)
                lss.append(z - sp)
                sigs.append(jnp.exp(lss[hh]))
                if diagonal:
                    sp = jnp.where(msk, sp, 0.0)
                exs.append(_dot(sp.astype(BF16), tri_after))
                if not diagonal:
                    zn.append(_dot_nt(qs[hh], kv(k_ref, j + 1)))
                dk_part.append(_dot(qst[hh], dzbuf[hh]))
            pgs, gs = [], []
            for hh in range(2):
                csl = slice(hh * 128, (hh + 1) * 128)
                cs = jnp.sum(jnp.where(lane == j, cs_ref[:, csl], 0.0), axis=-1, keepdims=True)
                a = jnp.exp(lss[hh] - exs[hh] - cs)
                if diagonal:
                    a = jnp.where(msk, a, 0.0)
                abuf_new = a.astype(BF16)
                g = a * dabuf[hh]
                gs.append((g, abuf_new))
                pgs.append(_dot(g.astype(BF16), tri_upto))
                if not diagonal:
                    dan.append(_dot_nt(dos[hh], kv(v_ref, j + 1)))
                dv_part.append(_dot(dost[hh], abuf[hh]))
            adds = (dq_add, dk_part[0] + dk_part[1], dv_part[0] + dv_part[1])
            for hh in range(2):
                g, abuf_new = gs[hh]
                cg = run[hh]
                dz = g - sigs[hh] * (cg + pgs[hh])
                if diagonal:
                    dz = jnp.where(msk, dz, 0.0)
                run[hh] = cg + pgs[hh][:, T - 1:T]
                dzbuf[hh] = dz.astype(BF16)
                abuf[hh] = abuf_new
            apply(jp, adds)
            if not diagonal:
                for hh in range(2):
                    zbuf[hh] = zn[hh]
                    dabuf[hh] = dan[hh]

        def step(j, carry):
            block(j, False)
            return carry

        lax.fori_loop(0, i, step, 0)
        block(i, True)
        apply(i, flush(i))
        dq_ref[...] = (jnp.where(lane < 64, dq_acc[0], dq_acc[1]) * SB_SCALE).astype(dq_ref.dtype)

        @pl.when(i == NQ - 1)
        def _():
            dk_ref[...] = dk_acc[...].T.astype(dk_ref.dtype)
            dv_ref[...] = dv_acc[...].T.astype(dv_ref.dtype)

    W = SB_HEADS * SB_DIM
    return pl.pallas_call(
        body, name="sb_bwd", grid=(SB_HEADS // 2, NQ),
        in_specs=[pl.BlockSpec((T, 128), lambda h, i: (i, qb + h)),
                  pl.BlockSpec((S, 128), lambda h, i: (0, kb + h)),
                  pl.BlockSpec((S, 128), lambda h, i: (0, vb + h)),
                  pl.BlockSpec((T, 256), lambda h, i: (i, h)),
                  pl.BlockSpec((T, 128), lambda h, i: (i, h)),
                  pl.BlockSpec((2, T, T), lambda h, i: (0, 0, 0))],
        out_specs=[pl.BlockSpec((T, 128), lambda h, i: (i, h)),
                   pl.BlockSpec((S, 128), lambda h, i: (0, h)),
                   pl.BlockSpec((S, 128), lambda h, i: (0, h))],
        out_shape=[jax.ShapeDtypeStruct((S, W), BF16)] * 3,
        scratch_shapes=[pltpu.VMEM((128, S), F32), pltpu.VMEM((128, S), F32), pltpu.VMEM((2, T, 128), F32),
                        pltpu.VMEM((2, T, 1), F32), pltpu.VMEM((2, T, T), F32), pltpu.VMEM((2, T, T), F32),
                        pltpu.VMEM((2, T, T), BF16), pltpu.VMEM((2, T, T), BF16)],
        compiler_params=_cparams(("parallel", "arbitrary")),
    )(p, p, p, carries, dy, tri)


def _exchange(srcs, out_shapes, src_slice, dst_slice, name, deps=()):
    n, nd = len(srcs), len(deps)

    def body(*refs):
        ins, outs = refs[:n], refs[n + nd:2 * n + nd]
        send_sems, recv_sems, loc_sems = refs[2 * n + nd:]
        x, y, c = lax.axis_index("x"), lax.axis_index("y"), lax.axis_index("c")
        me = 4 * x + 2 * y + c
        local = [pltpu.make_async_copy(src_slice(t, ins[t], me), dst_slice(t, outs[t], me), loc_sems.at[t])
                 for t in range(n)]
        for cp in local:
            cp.start()
        sends, recvs = [], []
        for k in (1, 2, 4, 6, 3, 5, 7):
            px = 1 - x if k & 4 else x
            py = 1 - y if k & 2 else y
            pc = 1 - c if k & 1 else c
            peer = 4 * px + 2 * py + pc
            for t in range(n):
                s = t * 7 + k - 1
                sends.append(pltpu.make_async_remote_copy(
                    src_ref=src_slice(t, ins[t], peer), dst_ref=dst_slice(t, outs[t], me),
                    send_sem=send_sems.at[s], recv_sem=recv_sems.at[s],
                    device_id=(px, py, pc), device_id_type=pl.DeviceIdType.MESH))
                recvs.append(pltpu.make_async_remote_copy(
                    src_ref=src_slice(t, ins[t], me), dst_ref=dst_slice(t, outs[t], peer),
                    send_sem=send_sems.at[s], recv_sem=recv_sems.at[s],
                    device_id=(px, py, pc), device_id_type=pl.DeviceIdType.MESH))
        for cp in sends:
            cp.start()
        for cp in recvs:
            cp.wait_recv()
        for cp in sends:
            cp.wait_send()
        for cp in local:
            cp.wait()

    anyspec = pl.BlockSpec(memory_space=pl.ANY)
    return pl.pallas_call(
        body, name=name, in_specs=[anyspec] * (n + nd), out_specs=[anyspec] * n,
        out_shape=[jax.ShapeDtypeStruct(s, d) for s, d in out_shapes],
        scratch_shapes=[pltpu.SemaphoreType.DMA((7 * n,)), pltpu.SemaphoreType.DMA((7 * n,)),
                        pltpu.SemaphoreType.DMA((n,))],
    )(*srcs, *deps)


def _all_gather_lead(xs, name, deps=()):
    return _exchange(
        xs, [((N_DEV,) + x.shape, x.dtype) for x in xs],
        lambda t, ref, peer: ref, lambda t, ref, who: ref.at[who], name, deps)


def _all_to_all_lead(xs, name):
    return _exchange(
        xs, [(x.shape, x.dtype) for x in xs],
        lambda t, ref, peer: ref.at[peer], lambda t, ref, who: ref.at[who], name)


_W_AXIS = {"w_in": 1, "w_ret_out": 0, "w_sb_out": 0, "w_mix_out": 0, "w_up": 1, "w_down": 0}
_W_NAMES = tuple(_W_AXIS)


def _window(ref, axis, who, width, count=1):
    start = pl.multiple_of(who * width, width)
    return ref.at[pl.ds(start, count * width), :] if axis == 0 else ref.at[:, pl.ds(start, count * width)]


_HBM = pl.BlockSpec(memory_space=pltpu.HBM)
_SEM = pl.BlockSpec(memory_space=pltpu.SEMAPHORE)
_EFFECT = pltpu.SideEffectType.DATAFLOW_SIDE_EFFECTING


_ALL_PEERS = (0, 1, 2, 4, 6, 3, 5, 7)
_SAME_CORE = (0, 2, 4, 6)


def _exchange_start(srcs, shapes, src_slice, dst_slice, name, deps=(), ks=_ALL_PEERS):
    n, nd = len(srcs), len(deps)
    lands = [pltpu.with_memory_space_constraint(lax.empty(s, d), pltpu.HBM) for s, d in shapes]

    def body(*refs):
        ins, lnd = refs[:n], refs[n:2 * n]
        sems = refs[2 * n + nd:4 * n + nd]
        token = refs[6 * n + nd]
        x, y, c = lax.axis_index("x"), lax.axis_index("y"), lax.axis_index("c")
        me = 4 * x + 2 * y + c
        for k in ks:
            px = 1 - x if k & 4 else x
            py = 1 - y if k & 2 else y
            pc = 1 - c if k & 1 else c
            peer = 4 * px + 2 * py + pc
            for t in range(n):
                pltpu.make_async_remote_copy(
                    src_ref=src_slice(t, ins[t], peer), dst_ref=dst_slice(t, lnd[t], me),
                    send_sem=sems[2 * t], recv_sem=sems[2 * t + 1],
                    device_id=(px, py, pc), device_id_type=pl.DeviceIdType.MESH).start()
        token[...] = jnp.zeros_like(token)

    res = pl.pallas_call(
        body, name=name, in_specs=[_HBM] * (2 * n) + [pl.BlockSpec(memory_space=pl.ANY)] * nd,
        out_specs=[_SEM] * (2 * n) + [_HBM] * (2 * n) + [pl.BlockSpec(memory_space=pltpu.VMEM)],
        out_shape=[pltpu.SemaphoreType.DMA(())] * (2 * n) + [pltpu.HBM(s.shape, s.dtype) for s in srcs]
        + [pltpu.HBM(s.shape, s.dtype) for s in lands] + [jax.ShapeDtypeStruct((8, 128), F32)],
        input_output_aliases={t: 2 * n + t for t in range(2 * n)},
        compiler_params=pltpu.CompilerParams(has_side_effects=_EFFECT),
    )(*[pltpu.with_memory_space_constraint(s, pltpu.HBM) for s in srcs], *lands, *deps)
    return dict(n=n, sems=res[:2 * n], srcs=res[2 * n:3 * n], lands=res[3 * n:4 * n], token=res[4 * n])


def _exchange_wait(h, after, name):
    n, ns = h['n'], len(h['srcs'])
    span = h.get('span', lambda t, ref: ref)

    def body(*refs):
        lnd = refs[ns:ns + n]
        sems = refs[ns + n:ns + 3 * n]
        x, y, c = lax.axis_index("x"), lax.axis_index("y"), lax.axis_index("c")
        for t in range(n):
            w = span(t, lnd[t])
            cp = pltpu.make_async_remote_copy(src_ref=w, dst_ref=w, send_sem=sems[2 * t], recv_sem=sems[2 * t + 1],
                                              device_id=(x, y, 1 - c), device_id_type=pl.DeviceIdType.MESH)
            cp.wait_send()
            cp.wait_recv()

    after = list(after)
    res = pl.pallas_call(
        body, name=name,
        in_specs=[_HBM] * (ns + n) + [_SEM] * (2 * n) + [pl.BlockSpec(memory_space=pl.ANY)] * len(after),
        out_specs=[_HBM] * (ns + n),
        out_shape=[pltpu.HBM(s.shape, s.dtype) for s in h['srcs']] + [pltpu.HBM(s.shape, s.dtype) for s in h['lands']],
        input_output_aliases={t: t for t in range(ns + n)},
        compiler_params=pltpu.CompilerParams(has_side_effects=_EFFECT),
    )(*h['srcs'], *h['lands'], *h['sems'], *after)
    return list(res[ns:])


def _sibling_start(lands, win, name):
    n = len(lands)

    def body(*refs):
        lnd = refs[:n]
        sems = refs[n:3 * n]
        token = refs[4 * n]
        x, y, c = lax.axis_index("x"), lax.axis_index("y"), lax.axis_index("c")
        for ox in (x, 1 - x):
            for oy in (y, 1 - y):
                owner = 4 * ox + 2 * oy + c
                for t in range(n):
                    w = win(t, lnd[t], owner)
                    pltpu.make_async_remote_copy(
                        src_ref=w, dst_ref=w, send_sem=sems[2 * t], recv_sem=sems[2 * t + 1],
                        device_id=(x, y, 1 - c), device_id_type=pl.DeviceIdType.MESH).start()
        token[...] = jnp.zeros_like(token)

    res = pl.pallas_call(
        body, name=name, in_specs=[_HBM] * n,
        out_specs=[_SEM] * (2 * n) + [_HBM] * n + [pl.BlockSpec(memory_space=pltpu.VMEM)],
        out_shape=[pltpu.SemaphoreType.DMA(())] * (2 * n) + [pltpu.HBM(s.shape, s.dtype) for s in lands]
        + [jax.ShapeDtypeStruct((8, 128), F32)],
        input_output_aliases={t: 2 * n + t for t in range(n)},
        compiler_params=pltpu.CompilerParams(has_side_effects=_EFFECT),
    )(*lands)
    return dict(n=n, sems=res[:2 * n], srcs=[], lands=res[2 * n:3 * n], token=res[3 * n])


def _gather_start(shards, names, tag, deps=(), two_level=False):
    xs = [shards[nm] for nm in names]
    axes = [_W_AXIS[nm] for nm in names]
    widths = [x.shape[ax] for x, ax in zip(xs, axes)]
    shapes = [(tuple(d * (N_DEV if a == ax else 1) for a, d in enumerate(x.shape)), x.dtype) for x, ax in zip(xs, axes)]
    src = lambda t, ref, peer: ref
    dst = lambda t, ref, who: _window(ref, axes[t], who, widths[t])
    h = _exchange_start(xs, shapes, src, dst, "gw_start_" + tag, deps, _SAME_CORE if two_level else _ALL_PEERS)
    h['tag'] = "gw_wait_" + tag
    if two_level:
        h['span'] = lambda t, ref: _window(ref, axes[t], 0, widths[t], len(_SAME_CORE))
        h['second'] = (dst, "gw_pass_" + tag, "gw_passed_" + tag)
    return h


def _scatter_start(grads, names, tag):
    xs = [grads[nm] for nm in names]
    axes = [_W_AXIS[nm] for nm in names]
    widths = [x.shape[ax] // N_DEV for x, ax in zip(xs, axes)]
    shapes = [((N_DEV,) + tuple(d // (N_DEV if a == ax else 1) for a, d in enumerate(x.shape)), x.dtype)
              for x, ax in zip(xs, axes)]
    src = lambda t, ref, peer: _window(ref, axes[t], peer, widths[t])
    dst = lambda t, ref, who: ref.at[who]
    h = _exchange_start(xs, shapes, src, dst, "sg_start_" + tag)
    h['tag'] = "sg_wait_" + tag
    return h


def _finish(h, after):
    return _exchange_wait(h, after, h['tag'])


class _LayerWeights:
    def __init__(self, groups, started):
        self.groups = groups
        self.started = started
        self.got = {}
        self.after = None

    def __getitem__(self, nm):
        if nm not in self.got:
            for names, h in self.groups:
                if nm in names:
                    self.got.update(zip(names, _finish(h, list(self.after) + self.started)))
        return self.got[nm]


def _adam_math(p_ref, w, m, v):
    g = p_ref[0].astype(F32)
    for s in range(1, p_ref.shape[0]):
        g = g + p_ref[s].astype(F32)
    bc1 = 1.0 / (1.0 - ADAM_B1 ** ADAM_STEP)
    bc2 = 1.0 / (1.0 - ADAM_B2 ** ADAM_STEP)
    mm = ADAM_B1 * m + (1.0 - ADAM_B1) * g
    vv = ADAM_B2 * v + (1.0 - ADAM_B2) * jnp.square(g)
    return g, -ADAM_LR * ((mm * bc1) / (jnp.sqrt(vv * bc2) + ADAM_EPS) + ADAM_WD * w), mm, vv


def _adam(parts, w, m, v, name, tr=256):
    P, R, C = parts.shape
    tr = min(tr, R)
    assert R % tr == 0

    def body(p_ref, w_ref, m_ref, v_ref, *outs):
        for o_ref, val in zip(outs, _adam_math(p_ref, w_ref[...], m_ref[...], v_ref[...])):
            o_ref[...] = val

    spec = pl.BlockSpec((tr, C), lambda i: (i, 0))
    return pl.pallas_call(
        body, name=name, grid=(R // tr,),
        in_specs=[pl.BlockSpec((P, tr, C), lambda i: (0, i, 0)), spec, spec, spec],
        out_specs=[spec] * 4, out_shape=[jax.ShapeDtypeStruct((R, C), F32)] * 4,
        compiler_params=_cparams(("parallel",)),
    )(parts, w, m, v)


def _adam_layer(parts, w, m, v, l, prev, name, tr=256):
    P, R, C = parts.shape
    tr = min(tr, R)
    assert R % tr == 0 and w.shape == (DEPTH, R, C)
    npv = 0 if prev is None else 4

    def body(p_ref, w_ref, m_ref, v_ref, *rest):
        for o_ref, val in zip(rest[npv:], _adam_math(p_ref, w_ref[0], m_ref[0], v_ref[0])):
            o_ref[0] = val

    spec = pl.BlockSpec((1, tr, C), lambda i: (l, i, 0))
    return pl.pallas_call(
        body, name=name, grid=(R // tr,),
        in_specs=[pl.BlockSpec((P, tr, C), lambda i: (0, i, 0)), spec, spec, spec]
        + [pl.BlockSpec(memory_space=pl.ANY)] * npv,
        out_specs=[spec] * 4, out_shape=[jax.ShapeDtypeStruct((DEPTH, R, C), F32)] * 4,
        input_output_aliases={4 + t: t for t in range(npv)},
        compiler_params=_cparams(("parallel",)),
    )(parts, w, m, v, *([] if prev is None else prev))


def _mod_partial(cact_all, w_ada, l, b_ada_l):
    R, (_, D, n) = cact_all.shape[0], w_ada.shape

    def body(c_ref, w_ref, b_ref, o_ref):
        o_ref[...] = _dot(c_ref[...].astype(BF16), w_ref[...].astype(BF16)) + b_ref[...]

    return pl.pallas_call(
        body, name="mod_partial", grid=(1,), out_shape=jax.ShapeDtypeStruct((R, n), F32),
        in_specs=[pl.BlockSpec((R, D), lambda i: (0, 0)), pl.BlockSpec((None, D, n), lambda i: (l, 0, 0)),
                  pl.BlockSpec((1, n), lambda i: (0, 0))],
        out_specs=pl.BlockSpec((R, n), lambda i: (0, 0)),
        compiler_params=pltpu.CompilerParams(vmem_limit_bytes=VMEM_LIMIT),
    )(cact_all, w_ada, b_ada_l)


def _ada_grad(cact_t, dmod):
    D, n = cact_t.shape[0], dmod.shape[1]

    def body(c_ref, d_ref, o_ref):
        ct = c_ref[...].astype(BF16).astype(F32)
        dm = d_ref[...].astype(BF16).astype(F32)
        acc = ct[:, 0:1] * dm[0:1, :]
        for b in range(1, N_DEV):
            acc = acc + ct[:, b:b + 1] * dm[b:b + 1, :]
        o_ref[0] = acc

    return pl.pallas_call(
        body, name="ada_grad", out_shape=jax.ShapeDtypeStruct((1, D, n), F32),
        compiler_params=pltpu.CompilerParams(vmem_limit_bytes=VMEM_LIMIT),
    )(cact_t, dmod)


def _norm_mod(x, r, gv, sh):
    return x * r * gv + sh


def _silu(x):
    return x * _sigmoid(x)


def _rstd(x):
    return lax.rsqrt(jnp.mean(x * x, axis=-1, keepdims=True) + EPS)


def _residual_epi(acc, x, g):
    xn = x + g * acc
    return acc, xn, _rstd(xn)


def _residual_norm_epi(acc, x, g, gv, sh):
    xn = x + g * acc
    r = _rstd(xn)
    return acc, xn, _norm_mod(xn, r, gv, sh), r


def _layer_fwd(x0, r1, h1, mod, gn1, gn2, nxt, W, rc, tri):
    S = x0.shape[0]
    sh1, sc1, g1m, sh2, sc2, g2m = [mod[i:i + 1] for i in range(N_MOD)]
    gv1 = gn1 * (1.0 + sc1)
    gv2 = gn2 * (1.0 + sc2)
    W.after = [h1]
    (p,) = _mm(h1, W["w_in"], tm=2048, outs=(BF16,), name="mm_in")
    yret, rstate = _ret_fwd(p, rc)
    ysb, sbc = _sb_fwd(p, tri)
    W.after = [ysb]
    (ya,) = _mm(yret, W["w_ret_out"], tm=512, a_ex=[(p, 'a', O_RG)],
                pro=lambda yr, g: _silu(g.astype(F32)) * yr.astype(F32), outs=(BF16,), name="mm_ret_out")
    yb, mg = _mm(ysb, W["w_sb_out"], o_ex=[(ya, 'o'), (p, 'o', O_GA), (p, 'o', O_GB)],
                 epi=lambda acc, a, ga, gb: (acc, _sigmoid(ga.astype(F32)) * a.astype(F32)
                                             + _sigmoid(gb.astype(F32)) * acc),
                 outs=(BF16, BF16), name="mm_sb_out")
    mo, x1, h2, r2 = _mm(mg, W["w_mix_out"], tm=512, o_ex=[(x0, 'o'), (g1m, 'n'), (gv2, 'n'), (sh2, 'n')],
                         epi=_residual_norm_epi, outs=(BF16, F32, BF16), cols=1, name="mm_mix_out")
    (act,) = _mm(h2, W["w_up"], tm=2048, epi=lambda acc: (jnp.maximum(acc, 0.0),), outs=(BF16,), name="mm_up")
    if nxt is None:
        dn, x2, r_out = _mm(act, W["w_down"], tm=512, pro=lambda a: a * a, o_ex=[(x1, 'o'), (g2m, 'n')],
                            epi=_residual_epi, outs=(BF16, F32), cols=1, name="mm_down_last")
        h_out = None
    else:
        dn, x2, h_out, r_out = _mm(act, W["w_down"], tm=512, pro=lambda a: a * a,
                                   o_ex=[(x1, 'o'), (g2m, 'n'), (nxt[0], 'n'), (nxt[1], 'n')],
                                   epi=_residual_norm_epi, outs=(BF16, F32, BF16), cols=1, name="mm_down")
    saved = dict(x0=x0, r1=r1, h1=h1, p=p, yret=yret, rstate=rstate, ysb=ysb, sbc=sbc, ya=ya, yb=yb, mg=mg, mo=mo, x1=x1,
                 r2=r2, h2=h2, act=act, dn=dn, gv1=gv1, gv2=gv2, mod=mod, gn1=gn1, gn2=gn2)
    return x2, r_out, h_out, saved


def _norm_bwd(dh, x, r, dres, gv, gn, extra_rows=(), extra_vecs=(), extra_fn=None, extra_outs=(), name="norm_bwd"):
    D = x.shape[1]
    ne = len(extra_rows)

    def fn(dh_t, x_t, dres_t, *rest):
        er, rest = rest[:ne], rest[ne:]
        gv_t = rest[0]
        ev, r_t = rest[1:-1], rest[-1]
        xh = x_t * r_t
        dxh = dh_t * gv_t
        dx = r_t * (dxh - xh * jnp.mean(dxh * xh, axis=-1, keepdims=True)) + dres_t
        base = (dx, dh_t, dh_t * xh)
        if extra_fn is None:
            return base
        return base + tuple(extra_fn(dx, *er, *ev))

    return _ew(fn, [dh, x, dres] + list(extra_rows), vecs=[gv] + list(extra_vecs), cols=[r],
               outs=[('row', D, F32), ('sum', D), ('sum', D)] + list(extra_outs), name=name)


def _gate_bwd(dx, dn, g):
    return dx * dn.astype(F32), dx * g


_GATE_OUTS = [('sum', D_MODEL), ('row', D_MODEL, BF16)]


def _layer_bwd(dx2, d_g2m, d_dn, sv, below, W, rc, tri, emit):
    mod = sv['mod']
    sh1, sc1, g1m, sh2, sc2, g2m = [mod[i:i + 1] for i in range(N_MOD)]
    D = D_MODEL
    p = sv['p']
    (d_up,) = _mm(d_dn, W["w_down"], tb=True, tm=2048, o_ex=[(sv['act'], 'o')],
                  epi=lambda acc, a: (acc * 2.0 * a.astype(F32),), outs=(BF16,), name="mm_down_dx")
    (gw_down,) = _mm(sv['act'], d_dn, ta=True, tk=DW_TK, pro=lambda a: a * a, outs=(BF16,), name="mm_down_dw")
    (gw_up,) = _mm(sv['h2'], d_up, ta=True, tk=DW_TK, outs=(BF16,), name="mm_up_dw")
    tok = emit(dict(w_down=gw_down, w_up=gw_up), "mlp")
    (d_h2,) = _mm(d_up, W["w_up"], tb=True, tm=2048, tk=2048, outs=(F32,), name="mm_up_dx", deps=[tok])
    dx1, d_sh2, s_h2, d_g1m, d_mo = _norm_bwd(
        d_h2, sv['x1'], sv['r2'], dx2, sv['gv2'], sv['gn2'],
        extra_rows=[sv['mo']], extra_vecs=[g1m],
        extra_fn=lambda dx, mo, g: (dx * mo.astype(F32), dx * g),
        extra_outs=[('sum', D), ('row', D, BF16)], name="norm_bwd_mlp")
    d_sc2 = sv['gn2'] * s_h2
    d_gn2 = (1.0 + sc2) * s_h2
    def mix_epi(acc, ya, yb, ga, gb):
        sa, sb = _sigmoid(ga.astype(F32)), _sigmoid(gb.astype(F32))
        return (acc * sa, acc * sb, acc * ya.astype(F32) * sa * (1.0 - sa), acc * yb.astype(F32) * sb * (1.0 - sb))

    d_ya, d_yb, d_ga, d_gb = _mm(d_mo, W["w_mix_out"], tb=True, tm=512,
                                 o_ex=[(sv['ya'], 'o'), (sv['yb'], 'o'), (p, 'o', O_GA), (p, 'o', O_GB)], epi=mix_epi,
                                 outs=(BF16,) * 4, name="mm_mix_dx")
    (gw_mix,) = _mm(sv['mg'], d_mo, ta=True, tk=DW_TK, outs=(BF16,), name="mm_mix_dw")

    def ro_epi(acc, g, yr):
        gf = g.astype(F32)
        s = _sigmoid(gf)
        return (acc * yr.astype(F32) * s * (1.0 + gf * (1.0 - s)), acc * gf * s)

    d_rg, d_yret = _mm(d_ya, W["w_ret_out"], tb=True, tm=512, tn=2048, o_ex=[(p, 'o', O_RG), (sv['yret'], 'o')],
                       epi=ro_epi, outs=(BF16, BF16), name="mm_ret_dx")
    (gw_ro,) = _mm(sv['yret'], d_ya, ta=True, tm=512, tk=DW_TK, a_ex=[(p, 'a', O_RG)],
                   pro=lambda yr, g: _silu(g.astype(F32)) * yr.astype(F32), outs=(BF16,), name="mm_ret_dw")
    (gw_so,) = _mm(sv['ysb'], d_yb, ta=True, tk=DW_TK, outs=(BF16,), name="mm_sb_dw")
    tok = emit(dict(w_mix_out=gw_mix, w_ret_out=gw_ro, w_sb_out=gw_so), "mix")
    (d_ysb,) = _mm(d_yb, W["w_sb_out"], tb=True, outs=(BF16,), name="mm_sb_dx", deps=[tok])
    d_sq, d_sk, d_sv = _sb_bwd(p, sv['sbc'], d_ysb, tri)
    d_ret = _ret_bwd(p, sv['rstate'], d_yret, rc)
    dp = [d_ret, d_rg, d_sq, d_sk, d_sv, d_ga, d_gb]
    (gw_in,) = _mm(sv['h1'], dp, ta=True, tk=1024, outs=(BF16,), name="mm_in_dw")
    tok = emit(dict(w_in=gw_in), "in")
    (d_h,) = _mm(dp, W["w_in"], tb=True, tm=2048, tk=512, outs=(F32,), name="mm_in_dx", deps=[tok])
    if below is None:
        dx0, d_sh1, s_h1 = _norm_bwd(d_h, sv['x0'], sv['r1'], dx1, sv['gv1'], sv['gn1'], name="norm_bwd_mix")
        gate_below = (None, None)
    else:
        dx0, d_sh1, s_h1, *gate_below = _norm_bwd(
            d_h, sv['x0'], sv['r1'], dx1, sv['gv1'], sv['gn1'], extra_rows=[below['dn']],
            extra_vecs=[below['mod'][N_MOD - 1:N_MOD]], extra_fn=_gate_bwd, extra_outs=_GATE_OUTS,
            name="norm_bwd_mix_gate")
    d_sc1 = sv['gn1'] * s_h1
    d_gn1 = (1.0 + sc1) * s_h1
    d_mod = jnp.concatenate([d_sh1, d_sc1, d_g1m, d_sh2, d_sc2, d_g2m], axis=1)
    return dx0, gate_below, d_mod, d_gn1, d_gn2


def kernel(x, c, norm_mix_g, w_in, w_ret_out, w_sb_out, w_mix_out, norm_mlp_g, w_up, w_down, w_ada, b_ada, final_g, loss_target, m_norm_mix_g, m_w_in, m_w_ret_out, m_w_sb_out, m_w_mix_out, m_norm_mlp_g, m_w_up, m_w_down, m_w_ada, m_b_ada, m_final_g, v_norm_mix_g, v_w_in, v_w_ret_out, v_w_sb_out, v_w_mix_out, v_norm_mlp_g, v_w_up, v_w_down, v_w_ada, v_b_ada, v_final_g):
    S, D = x.shape[1], x.shape[2]
    x0 = x.reshape(S, D)
    tgt = loss_target.reshape(S, D)
    me = 4 * lax.axis_index("x") + 2 * lax.axis_index("y") + lax.axis_index("c")
    wts = dict(w_in=w_in, w_ret_out=w_ret_out, w_sb_out=w_sb_out, w_mix_out=w_mix_out, w_up=w_up, w_down=w_down)
    mts = dict(w_in=m_w_in, w_ret_out=m_w_ret_out, w_sb_out=m_w_sb_out, w_mix_out=m_w_mix_out, w_up=m_w_up, w_down=m_w_down)
    vts = dict(w_in=v_w_in, w_ret_out=v_w_ret_out, w_sb_out=v_w_sb_out, w_mix_out=v_w_mix_out, w_up=v_w_up, w_down=v_w_down)
    rc = _ret_consts(S)
    tri = _tri()

    (cact,) = _ew(lambda t: (_silu(t),), [jnp.pad(c, ((0, 7), (0, 0)))], outs=[('row', D, F32)], name="silu_c")
    (cact_all,) = _all_gather_lead([cact[0:1]], "gather_c")
    cact_all = cact_all.reshape(N_DEV, D)
    cact16 = jnp.pad(cact_all, ((0, 8), (0, 0)))
    n_ada = w_ada.shape[2]
    b_loc = lax.dynamic_slice_in_dim(b_ada, me * n_ada, n_ada, axis=1)
    mods = [_mod_partial(cact16, w_ada, l, b_loc[l:l + 1])[:N_DEV] for l in range(DEPTH)]
    modp = jnp.stack(mods, axis=1)
    (modr,) = _all_to_all_lead([modp], "scatter_mod")
    mod_full = jnp.transpose(modr, (1, 0, 2)).reshape(DEPTH, N_MOD, D)

    def cast(nm, l, deps=()):
        w = wts[nm]
        _, R, C = w.shape
        tr = min(256, R)

        def body(w_ref, *rest):
            rest[-1][...] = w_ref[...].astype(BF16)

        return pl.pallas_call(
            body, name="cast_bf16", grid=(R // tr,),
            in_specs=[pl.BlockSpec((None, tr, C), lambda i: (l, i, 0))]
            + [pl.BlockSpec(memory_space=pl.ANY)] * len(deps),
            out_specs=pl.BlockSpec((tr, C), lambda i: (i, 0)), out_shape=jax.ShapeDtypeStruct((R, C), BF16),
            compiler_params=_cparams(("arbitrary",)),
        )(w, *deps)

    rest = tuple(nm for nm in _W_NAMES if nm != "w_in")
    sh = [{} for _ in range(DEPTH)]
    sh[0]["w_in"] = cast("w_in", 0)
    h_in = _gather_start(sh[0], ("w_in",), "0_in", [modr], two_level=True)
    in_flight = [h_in['token']]
    for l in range(DEPTH):
        for nm in _W_NAMES:
            if nm not in sh[l]:
                sh[l][nm] = cast(nm, l, in_flight)

    pre = [(norm_mix_g[l:l + 1] * (1.0 + mod_full[l][1:2]), mod_full[l][0:1]) for l in range(DEPTH)]

    def first(t, gv, sh):
        r = _rstd(t)
        return r, _norm_mod(t, r, gv, sh)

    xs = x0
    rs, hs = _ew(first, [x0], vecs=list(pre[0]), outs=[('col', F32), ('row', D, BF16)], name="row_rstd",
                 deps=in_flight)

    win, pass_name, passed_name = h_in['second']
    before_wait = [hs] + [sh[l][nm] for l in range(DEPTH) for nm in _W_NAMES if (l, nm) != (0, "w_in")]
    h_pass = _sibling_start(_exchange_wait(h_in, before_wait, h_in['tag']), win, pass_name)
    h_pass['span'] = h_in['span']
    started = [h_pass['token']]
    layer_groups = []
    for l, groups in enumerate([[(rest, "0_rest")]] + [[(_W_NAMES, "%d_all" % l)] for l in range(1, DEPTH)]):
        layer_groups.append([])
        for names, tag in groups:
            layer_groups[-1].append((names, _gather_start(sh[l], names, tag, started[-1:])))
            started.append(layer_groups[-1][-1][1]['token'])
    (w_in0,) = _exchange_wait(h_pass, started, passed_name)
    layer_w = [_LayerWeights(g, started) for g in layer_groups]
    layer_w[0].got["w_in"] = w_in0
    saved = []
    for l in range(DEPTH):
        xs, rs, hs, sv = _layer_fwd(xs, rs, hs, mod_full[l], norm_mix_g[l:l + 1], norm_mlp_g[l:l + 1],
                                    pre[l + 1] if l + 1 < DEPTH else None, layer_w[l], rc, tri)
        sv['W'] = layer_w[l]
        saved.append(sv)

    fg = final_g.reshape(1, D)

    def head(xt, tg, dn, g, g2m, r):
        xh = xt * r
        e = xh * g - tg
        dy = e * (1.0 / D)
        dxh = dy * g
        dx = r * (dxh - xh * jnp.mean(dxh * xh, axis=-1, keepdims=True))
        return (dx, dy * xh, 0.5 * e * e * (1.0 / D)) + _gate_bwd(dx, dn, g2m)

    top = saved[DEPTH - 1]
    dxs, d_fg, loss_cols, *gate = _ew(head, [xs, tgt, top['dn']], vecs=[fg, top['mod'][N_MOD - 1:N_MOD]], cols=[rs],
                                      outs=[('row', D, F32), ('sum', D), ('sum', D)] + _GATE_OUTS, name="loss_head")

    small = [None] * DEPTH
    pending = []
    for l in reversed(range(DEPTH)):
        sv = saved[l]

        def emit(gw, tag, l=l):
            names = tuple(gw)
            pending.append((l, names, _scatter_start(gw, names, "%d_%s" % (l, tag))))
            return pending[-1][2]['token']

        dxs, gate, d_mod, d_gn1, d_gn2 = _layer_bwd(dxs, gate[0], gate[1], sv, saved[l - 1] if l else None,
                                                    sv['W'], rc, tri, emit)
        small[l] = (d_mod, d_gn1, d_gn2)
    grad_x = dxs.reshape(1, S, D)

    res = {}
    after = [dxs]
    for l, names, h in pending:
        for nm, landed in zip(names, _finish(h, after)):
            res[nm] = _adam_layer(landed, wts[nm], mts[nm], vts[nm], l, res.get(nm), "adam_layer")
        after = [res[nm][0] for nm in names]

    pack = jnp.concatenate([small[l][0] for l in range(DEPTH)] + [small[l][1] for l in range(DEPTH)]
                           + [small[l][2] for l in range(DEPTH)] + [d_fg, loss_cols], axis=1)
    (packs,) = _all_gather_lead([pack], "gather_small", deps=after)
    packs = packs.reshape(N_DEV, -1)
    o = 0
    dmod_all = []
    for l in range(DEPTH):
        dmod_all.append(packs[:, o:o + N_MOD * D]); o += N_MOD * D
    gn1_parts = packs[:, o:o + DEPTH * D].reshape(N_DEV, DEPTH, D); o += DEPTH * D
    gn2_parts = packs[:, o:o + DEPTH * D].reshape(N_DEV, DEPTH, D); o += DEPTH * D
    fg_parts = packs[:, o:o + D].reshape(N_DEV, 1, D); o += D
    loss_parts = packs[:, o:o + D]
    (loss_sum,) = _ew(lambda t: (t,), [loss_parts], outs=[('sum', D)], name="loss_sum")
    loss = jnp.sum(loss_sum)

    res["norm_mix_g"] = _adam(gn1_parts, norm_mix_g, m_norm_mix_g, v_norm_mix_g, "adam")
    res["norm_mlp_g"] = _adam(gn2_parts, norm_mlp_g, m_norm_mlp_g, v_norm_mlp_g, "adam")
    fgr = _adam(fg_parts, fg, m_final_g.reshape(1, D), v_final_g.reshape(1, D), "adam")
    res["final_g"] = [t.reshape(D) for t in fgr]
    bparts = jnp.stack(dmod_all, axis=1)
    res["b_ada"] = _adam(bparts, b_ada, m_b_ada, v_b_ada, "adam")
    cact_t = cact_all.T
    for l in range(DEPTH):
        dm_loc = lax.dynamic_slice_in_dim(dmod_all[l], me * n_ada, n_ada, axis=1)
        res["w_ada"] = _adam_layer(_ada_grad(cact_t, dm_loc), w_ada, m_w_ada, v_w_ada, l, res.get("w_ada"),
                                   "adam_layer")

    order = ['norm_mix_g', 'w_in', 'w_ret_out', 'w_sb_out', 'w_mix_out', 'norm_mlp_g', 'w_up', 'w_down', 'w_ada', 'b_ada', 'final_g']
    out = [loss, grad_x]
    for i in range(4):
        out += [res[nm][i] for nm in order]
    return tuple(out)
```

```python
import functools
import math

import jax
import jax.numpy as jnp
import numpy as np
from jax import lax
from jax.experimental import pallas as pl
from jax.experimental.pallas import tpu as pltpu

F32 = jnp.float32
BF16 = jnp.bfloat16

N_DEV = 8
D_MODEL = 1024
DEPTH = 2
RET_HEADS = 4
RET_QK = 256
RET_V = 512
RET_CHUNK = 256
ROPE_BASE = 10000.0
SB_HEADS = 16
SB_DIM = 64
D_FF = 4096
N_MOD = 6
EPS = 1e-6
GN_EPS = 1e-5
O_RQ, O_RK, O_RV, O_RG, O_SQ, O_SK, O_SV, O_GA, O_GB = 0, 1024, 2048, 4096, 6144, 7168, 8192, 9216, 10240
IN_W = 11264

ADAM_LR, ADAM_B1, ADAM_B2, ADAM_EPS, ADAM_WD, ADAM_STEP = 0.001, 0.9, 0.999, 1e-08, 0.01, 10

VMEM_LIMIT = 56 * 1024 * 1024
DW_TK = 2048


def _cparams(sem):
    return pltpu.CompilerParams(dimension_semantics=sem, vmem_limit_bytes=VMEM_LIMIT)


def _mm(a, b, *, ta=False, tb=False, tm=1024, tn=1024, tk=None, a_ex=(), pro=None, o_ex=(), epi=None,
        outs=(F32,), cols=0, name, deps=()):
    a_parts = list(a) if isinstance(a, (list, tuple)) else [a]
    b_parts = list(b) if isinstance(b, (list, tuple)) else [b]
    assert not (ta and len(a_parts) > 1) and not (tb and len(b_parts) > 1)
    if ta:
        K, M = a.shape
    else:
        M, K = a_parts[0].shape[0], sum(t.shape[1] for t in a_parts)
    N = b.shape[0] if tb else sum(t.shape[1] for t in b_parts)
    tm, tn, tk = min(tm, M), min(tn, N), K if tk is None else min(tk, K)
    assert M % tm == 0 and N % tn == 0 and K % tk == 0, (name, M, N, K, tm, tn, tk)
    nk = K // tk
    multi = len(a_parts) > 1 or len(b_parts) > 1
    direct = epi is None and tuple(outs) == (F32,) and cols == 0

    def ranges(parts, t):
        out, o = [], 0
        for arr in parts:
            assert arr.shape[1] % t == 0
            out.append((o, o + arr.shape[1] // t))
            o += arr.shape[1] // t
        return out

    a_rng = ranges(a_parts, tk) if len(a_parts) > 1 else [(0, nk)]
    b_rng = ranges(b_parts, tn) if len(b_parts) > 1 else [(0, N // tn)]
    clip = lambda v, lo, hi: jnp.clip(v - lo, 0, hi - lo - 1)
    mine = lambda v, lo, hi, w: jnp.where((v >= lo) & (v < hi), w, 0)
    in_specs, args = [], []
    for arr, (lo, hi) in zip(a_parts, a_rng):
        in_specs.append(pl.BlockSpec((tk, tm), lambda i, j, k: (k, i)) if ta
                        else pl.BlockSpec((tm, tk), lambda i, j, k, lo=lo, hi=hi: (i, clip(k, lo, hi))))
        args.append(arr)
    for arr, (lo, hi) in zip(b_parts, b_rng):
        in_specs.append(pl.BlockSpec((tn, tk), lambda i, j, k: (j, k)) if tb
                        else pl.BlockSpec((tk, tn), lambda i, j, k, lo=lo, hi=hi: (mine(j, lo, hi, k), clip(j, lo, hi))))
        args.append(arr)
    npa, npb = len(a_parts), len(b_parts)
    for arr, kind, *off in a_ex:
        off = off[0] if off else 0
        if kind == 'a' and ta:
            assert off % tm == 0
            in_specs.append(pl.BlockSpec((tk, tm), lambda i, j, k, o=off // tm: (k, o + i)))
        elif kind == 'a':
            assert off % tk == 0
            in_specs.append(pl.BlockSpec((tm, tk), lambda i, j, k, o=off // tk: (i, o + k)))
        elif kind == 'k':
            in_specs.append(pl.BlockSpec((tk, 1), lambda i, j, k: (k, 0)) if ta
                            else pl.BlockSpec((1, tk), lambda i, j, k: (0, k)))
        else:
            in_specs.append(pl.BlockSpec((1, tm), lambda i, j, k: (0, i)) if ta
                            else pl.BlockSpec((tm, 1), lambda i, j, k: (i, 0)))
        args.append(arr)
    for arr, kind, *off in o_ex:
        off = off[0] if off else 0
        if kind == 'o':
            assert off % tn == 0
            in_specs.append(pl.BlockSpec((tm, tn), lambda i, j, k, o=off // tn: (i, o + j)))
        elif kind == 'n':
            in_specs.append(pl.BlockSpec((1, tn), lambda i, j, k: (0, j)))
        else:
            in_specs.append(pl.BlockSpec((tm, 1), lambda i, j, k: (i, 0)))
        args.append(arr)
    for arr in deps:
        in_specs.append(pl.BlockSpec(memory_space=pl.ANY))
        args.append(arr)
    assert cols == 0 or N == tn
    na, no, nout, nd = len(a_ex), len(o_ex), len(outs) + cols, len(deps)
    dims = (((0 if ta else 1,), (1 if tb else 0,)), ((), ()))

    def body(*refs):
        a_refs, b_refs = refs[:npa], refs[npa:npa + npb]
        n0 = npa + npb
        aex = refs[n0:n0 + na]
        oex = refs[n0 + na:n0 + na + no]
        out_refs = refs[n0 + na + no + nd:n0 + na + no + nd + nout]

        def product(a_ref, b_ref):
            at = a_ref[...]
            if pro is not None:
                at = pro(at, *[r[...] for r in aex])
            return lax.dot_general(at.astype(BF16), b_ref[...].astype(BF16), dims, preferred_element_type=F32)

        def finish(res):
            vals = epi(res, *[r[...] for r in oex]) if epi is not None else (res,)
            for o_ref, v in zip(out_refs, vals):
                o_ref[...] = v.astype(o_ref.dtype)

        if nk == 1 and not multi:
            finish(product(a_refs[0], b_refs[0]))
            return
        acc = out_refs[0] if direct else refs[-1]
        j, k = pl.program_id(1), pl.program_id(2)
        if multi:
            @pl.when(k == 0)
            def _():
                acc[...] = jnp.zeros_like(acc)

            for a_ref, (alo, ahi) in zip(a_refs, a_rng):
                for b_ref, (blo, bhi) in zip(b_refs, b_rng):
                    @pl.when((k >= alo) & (k < ahi) & (j >= blo) & (j < bhi))
                    def _():
                        acc[...] += product(a_ref, b_ref)
        else:
            @pl.when(k == 0)
            def _():
                acc[...] = product(a_refs[0], b_refs[0])

            @pl.when(k > 0)
            def _():
                acc[...] += product(a_refs[0], b_refs[0])

        if not direct:
            @pl.when(k == nk - 1)
            def _():
                finish(acc[...])

    res = pl.pallas_call(
        body, name=name, grid=(M // tm, N // tn, nk), in_specs=in_specs,
        out_specs=[pl.BlockSpec((tm, tn), lambda i, j, k: (i, j)) for _ in outs]
        + [pl.BlockSpec((tm, 1), lambda i, j, k: (i, 0))] * cols,
        out_shape=[jax.ShapeDtypeStruct((M, N), dt) for dt in outs] + [jax.ShapeDtypeStruct((M, 1), F32)] * cols,
        scratch_shapes=[pltpu.VMEM((tm, tn), F32)] if (nk > 1 or multi) and not direct else [],
        compiler_params=_cparams(("parallel", "parallel", "arbitrary")),
    )(*args)
    return res


def _mm_parts_ring(parts, w, *, tm=2048, tk=1024, name, deps=()):
    S = parts[0].shape[0]
    N, K = w.shape
    rng, o = [], 0
    for a in parts:
        assert a.dtype == BF16 and a.shape[0] == S and a.shape[1] % tk == 0
        rng.append((o, o + a.shape[1] // tk))
        o += a.shape[1] // tk
    tm = min(tm, S)
    nk, nm, npart, nd = o, S // tm, len(parts), len(deps)
    T = nm * nk
    assert nk * tk == K and S % tm == 0 and T >= 2

    def body(*refs):
        a_refs, b_ref, o_ref = refs[:npart], refs[npart], refs[npart + 1 + nd]
        abuf, sems = refs[-2], refs[-1]
        k = pl.program_id(1)
        t = pl.program_id(0) * nk + k

        def fetch(tt):
            ii, kk, slot = tt // nk, tt % nk, tt % 3
            for a_ref, (lo, hi) in zip(a_refs, rng):
                @pl.when((kk >= lo) & (kk < hi))
                def _():
                    pltpu.make_async_copy(
                        a_ref.at[pl.ds(pl.multiple_of(ii * tm, tm), tm), pl.ds(pl.multiple_of((kk - lo) * tk, tk), tk)],
                        abuf.at[slot], sems.at[slot]).start()

        @pl.when(t == 0)
        def _():
            fetch(t)
            fetch(t + 1)

        @pl.when(t + 2 < T)
        def _():
            fetch(t + 2)

        slot = t % 3
        pltpu.make_async_copy(a_refs[0].at[pl.ds(0, tm), pl.ds(0, tk)], abuf.at[slot], sems.at[slot]).wait()

        def product():
            return lax.dot_general(abuf[slot], b_ref[...], (((1,), (1,)), ((), ())), preferred_element_type=F32)

        @pl.when(k == 0)
        def _():
            o_ref[...] = product()

        @pl.when(k > 0)
        def _():
            o_ref[...] += product()

    (res,) = pl.pallas_call(
        body, name=name, grid=(nm, nk),
        in_specs=[pl.BlockSpec(memory_space=pl.ANY)] * npart + [pl.BlockSpec((N, tk), lambda i, k: (0, k))]
        + [pl.BlockSpec(memory_space=pl.ANY)] * nd,
        out_specs=[pl.BlockSpec((tm, N), lambda i, k: (i, 0))],
        out_shape=[jax.ShapeDtypeStruct((S, N), F32)],
        scratch_shapes=[pltpu.VMEM((3, tm, tk), BF16), pltpu.SemaphoreType.DMA((3,))],
        compiler_params=_cparams(("arbitrary", "arbitrary")),
    )(*parts, w, *deps)
    return res


def _ew(fn, rows, vecs=(), cols=(), outs=(), tr=256, name=None, deps=()):
    S = rows[0].shape[0]
    tr = min(tr, S)
    assert S % tr == 0
    in_specs, args = [], []
    for r in rows:
        in_specs.append(pl.BlockSpec((tr, r.shape[1]), lambda i: (i, 0)))
        args.append(r)
    for v in vecs:
        in_specs.append(pl.BlockSpec((1, v.shape[1]), lambda i: (0, 0)))
        args.append(v)
    for c in cols:
        in_specs.append(pl.BlockSpec((tr, 1), lambda i: (i, 0)))
        args.append(c)
    out_specs, out_shape = [], []
    for o in outs:
        if o[0] == 'row':
            out_specs.append(pl.BlockSpec((tr, o[1]), lambda i: (i, 0)))
            out_shape.append(jax.ShapeDtypeStruct((S, o[1]), o[2]))
        elif o[0] == 'sum':
            out_specs.append(pl.BlockSpec((1, o[1]), lambda i: (0, 0)))
            out_shape.append(jax.ShapeDtypeStruct((1, o[1]), F32))
        else:
            out_specs.append(pl.BlockSpec((tr, 1), lambda i: (i, 0)))
            out_shape.append(jax.ShapeDtypeStruct((S, 1), o[1]))
    nin = len(args)
    in_specs += [pl.BlockSpec(memory_space=pl.ANY)] * len(deps)
    args += list(deps)

    def body(*refs):
        i = pl.program_id(0)
        vals = fn(*[r[...] for r in refs[:nin]])
        for o, o_ref, v in zip(outs, refs[nin + len(deps):], vals):
            if o[0] == 'sum':
                @pl.when(i == 0)
                def _():
                    o_ref[...] = jnp.zeros_like(o_ref)
                o_ref[...] += jnp.sum(v.astype(F32), axis=0, keepdims=True)
            else:
                o_ref[...] = v.astype(o_ref.dtype)

    return pl.pallas_call(
        body, name=name, grid=(S // tr,), in_specs=in_specs, out_specs=out_specs, out_shape=out_shape,
        compiler_params=_cparams(("arbitrary",)),
    )(*args)


def _sigmoid(x):
    return 1.0 / (1.0 + jnp.exp(-x))


def _ret_consts(S):
    h = np.arange(RET_HEADS, dtype=np.float64)
    log_gamma = np.log1p(-np.power(2.0, -5.0 - h))
    idx = np.arange(RET_CHUNK, dtype=np.float64)
    rel = idx[:, None] - idx[None, :]
    decay = np.where(rel >= 0, np.exp(np.maximum(rel, 0.0) * log_gamma[:, None, None]), 0.0)
    xi = np.exp((idx + 1.0) * log_gamma[:, None])[:, :, None]
    zeta = np.exp((RET_CHUNK - 1.0 - idx) * log_gamma[:, None])[:, :, None]
    gamma_c = np.exp(RET_CHUNK * log_gamma)[:, None, None]
    half = RET_QK // 2
    inv_freq = np.power(ROPE_BASE, -np.arange(half, dtype=np.float64) / half).astype(np.float32)
    ang = np.arange(S, dtype=np.float32)[:, None] * inv_freq[None, :]
    f = lambda t: jnp.asarray(t, F32)
    return dict(decay=f(decay), xi=f(xi), zeta=f(zeta), gc=f(gamma_c), cos=f(np.cos(ang)), sin=f(np.sin(ang)))


def _rot(t, cos, sin):
    half = RET_QK // 2
    t1, t2 = t[:, :half], t[:, half:]
    return jnp.concatenate([t1 * cos - t2 * sin, t1 * sin + t2 * cos], axis=-1)


def _rot_inv(t, cos, sin):
    half = RET_QK // 2
    t1, t2 = t[:, :half], t[:, half:]
    return jnp.concatenate([t1 * cos + t2 * sin, t2 * cos - t1 * sin], axis=-1)


_NT = (((1,), (1,)), ((), ()))
_TN = (((0,), (0,)), ((), ()))


def _dot(a, b):
    return jnp.dot(a, b, preferred_element_type=F32)


def _dot_nt(a, b):
    return lax.dot_general(a, b, _NT, preferred_element_type=F32)


def _dot_tn(a, b):
    return lax.dot_general(a, b, _TN, preferred_element_type=F32)


_QW, _VW = RET_HEADS * RET_QK, RET_HEADS * RET_V
_HEADS = range(RET_HEADS)


def _ret_in_specs(C, rev, NC):
    n_of = (lambda n: NC - 1 - n) if rev else (lambda n: n)
    whole3 = lambda n: (0, 0, 0)
    return [
        pl.BlockSpec((C, _QW), lambda n: (n_of(n), O_RQ // _QW)),
        pl.BlockSpec((C, _QW), lambda n: (n_of(n), O_RK // _QW)),
        pl.BlockSpec((C, _VW), lambda n: (n_of(n), O_RV // _VW)),
        pl.BlockSpec((C, RET_QK // 2), lambda n: (n_of(n), 0)),
        pl.BlockSpec((C, RET_QK // 2), lambda n: (n_of(n), 0)),
        pl.BlockSpec((RET_HEADS, C, C), whole3),
        pl.BlockSpec((RET_HEADS, C, 1), whole3),
        pl.BlockSpec((RET_HEADS, C, 1), whole3),
        pl.BlockSpec((RET_HEADS, 1, 1), whole3),
    ]


def _qk_heads(q_ref, k_ref, cos, sin):
    qs, kfs = [], []
    for h in _HEADS:
        cols = slice(h * RET_QK, (h + 1) * RET_QK)
        qs.append(_rot(q_ref[:, cols].astype(F32), cos, sin).astype(BF16))
        kfs.append(_rot(k_ref[:, cols].astype(F32), cos, sin) * (RET_QK ** -0.5))
    return qs, kfs


def _ret_fwd(p, rc):
    S = p.shape[0]
    C = RET_CHUNK
    NC = S // C

    def body(q_ref, k_ref, v_ref, cos_ref, sin_ref, dec_ref, xi_ref, zeta_ref, gc_ref, y_ref, rs_ref, r_acc):
        n = pl.program_id(0)

        @pl.when(n == 0)
        def _():
            r_acc[...] = jnp.zeros_like(r_acc)

        cos, sin = cos_ref[...], sin_ref[...]
        qs, kfs = _qk_heads(q_ref, k_ref, cos, sin)
        vs = [v_ref[:, h * RET_V:(h + 1) * RET_V] for h in _HEADS]
        rbs = [r_acc[h].astype(BF16) for h in _HEADS]
        for h in _HEADS:
            rs_ref[h, 0] = rbs[h]
        ss = [(_dot_nt(qs[h], kfs[h].astype(BF16)) * dec_ref[h]).astype(BF16) for h in _HEADS]
        os = [_dot(ss[h], vs[h]) + _dot(qs[h], rbs[h]) * xi_ref[h] for h in _HEADS]
        for h in _HEADS:
            o = os[h]
            mu = jnp.mean(o, axis=-1, keepdims=True)
            var = jnp.mean(jnp.square(o - mu), axis=-1, keepdims=True)
            y_ref[:, h * RET_V:(h + 1) * RET_V] = ((o - mu) * lax.rsqrt(var + GN_EPS)).astype(y_ref.dtype)
        for h in _HEADS:
            kz = (kfs[h] * zeta_ref[h]).astype(BF16)
            r_acc[h] = r_acc[h] * gc_ref[h] + _dot_tn(kz, vs[h])

    return pl.pallas_call(
        body, name="ret_fwd", grid=(NC,), in_specs=_ret_in_specs(C, False, NC),
        out_specs=[pl.BlockSpec((C, _VW), lambda n: (n, 0)),
                   pl.BlockSpec((RET_HEADS, 1, RET_QK, RET_V), lambda n: (0, n, 0, 0))],
        out_shape=[jax.ShapeDtypeStruct((S, _VW), BF16),
                   jax.ShapeDtypeStruct((RET_HEADS, NC, RET_QK, RET_V), BF16)],
        scratch_shapes=[pltpu.VMEM((RET_HEADS, RET_QK, RET_V), F32)],
        compiler_params=_cparams(("arbitrary",)),
    )(p, p, p, rc['cos'], rc['sin'], rc['decay'], rc['xi'], rc['zeta'], rc['gc'])


def _ret_bwd(p, rstate, dy, rc):
    S = p.shape[0]
    C = RET_CHUNK
    NC = S // C

    def body(q_ref, k_ref, v_ref, cos_ref, sin_ref, dec_ref, xi_ref, zeta_ref, gc_ref, rs_ref, dy_ref,
             d_ref, dr_acc):
        dq_ref, dk_ref, dv_ref = d_ref.at[:, 0:_QW], d_ref.at[:, _QW:2 * _QW], d_ref.at[:, 2 * _QW:2 * _QW + _VW]
        t = pl.program_id(0)

        @pl.when(t == 0)
        def _():
            dr_acc[...] = jnp.zeros_like(dr_acc)

        cos, sin = cos_ref[...], sin_ref[...]
        qs, kfs = _qk_heads(q_ref, k_ref, cos, sin)
        ks = [kf.astype(BF16) for kf in kfs]
        vs = [v_ref[:, h * RET_V:(h + 1) * RET_V] for h in _HEADS]
        rbs = [rs_ref[h, 0] for h in _HEADS]
        ss = [(_dot_nt(qs[h], ks[h]) * dec_ref[h]).astype(BF16) for h in _HEADS]
        os = [_dot(ss[h], vs[h]) + _dot(qs[h], rbs[h]) * xi_ref[h] for h in _HEADS]
        dobs, doxis = [], []
        for h in _HEADS:
            o = os[h]
            mu = jnp.mean(o, axis=-1, keepdims=True)
            var = jnp.mean(jnp.square(o - mu), axis=-1, keepdims=True)
            rstd = lax.rsqrt(var + GN_EPS)
            yh = (o - mu) * rstd
            dyf = dy_ref[:, h * RET_V:(h + 1) * RET_V].astype(F32)
            do = (dyf - jnp.mean(dyf, axis=-1, keepdims=True)
                  - yh * jnp.mean(dyf * yh, axis=-1, keepdims=True)) * rstd
            dobs.append(do.astype(BF16))
            doxis.append((do * xi_ref[h]).astype(BF16))
        drbs = [dr_acc[h].astype(BF16) for h in _HEADS]
        dss = [(_dot_nt(dobs[h], vs[h]) * dec_ref[h]).astype(BF16) for h in _HEADS]
        for h in _HEADS:
            dq = _dot(dss[h], ks[h]) + _dot_nt(doxis[h], rbs[h])
            dq_ref[:, h * RET_QK:(h + 1) * RET_QK] = _rot_inv(dq, cos, sin).astype(dq_ref.dtype)
        for h in _HEADS:
            dk = _dot_tn(dss[h], qs[h]) + _dot_nt(vs[h], drbs[h]) * zeta_ref[h]
            dk_ref[:, h * RET_QK:(h + 1) * RET_QK] = (_rot_inv(dk, cos, sin) * (RET_QK ** -0.5)).astype(dk_ref.dtype)
        for h in _HEADS:
            kz = (kfs[h] * zeta_ref[h]).astype(BF16)
            dv = _dot_tn(ss[h], dobs[h]) + _dot(kz, drbs[h])
            dv_ref[:, h * RET_V:(h + 1) * RET_V] = dv.astype(dv_ref.dtype)
        for h in _HEADS:
            dr_acc[h] = dr_acc[h] * gc_ref[h] + _dot_tn(qs[h], doxis[h])

    rn = lambda n: NC - 1 - n
    in_specs = _ret_in_specs(C, True, NC) + [
        pl.BlockSpec((RET_HEADS, 1, RET_QK, RET_V), lambda n: (0, rn(n), 0, 0)),
        pl.BlockSpec((C, _VW), lambda n: (rn(n), 0)),
    ]
    return pl.pallas_call(
        body, name="ret_bwd", grid=(NC,), in_specs=in_specs,
        out_specs=pl.BlockSpec((C, 2 * _QW + _VW), lambda n: (rn(n), 0)),
        out_shape=jax.ShapeDtypeStruct((S, 2 * _QW + _VW), BF16),
        scratch_shapes=[pltpu.VMEM((RET_HEADS, RET_QK, RET_V), F32)],
        compiler_params=_cparams(("arbitrary",)),
    )(p, p, p, rc['cos'], rc['sin'], rc['decay'], rc['xi'], rc['zeta'], rc['gc'], rstate, dy)


SB_T = 256
SB_SCALE = SB_DIM ** -0.5


def _tri():
    j = np.arange(SB_T)
    after = (j[:, None] > j[None, :]).astype(np.float32)
    upto = (j[:, None] <= j[None, :]).astype(np.float32)
    return jnp.asarray(np.stack([after, upto]), BF16)


def _softplus_parts(z):
    neg_abs = lax.bitcast_convert_type(lax.bitcast_convert_type(z, jnp.uint32) | jnp.uint32(0x80000000), F32)
    e = jnp.exp(neg_abs)
    return jnp.maximum(z, 0.0) + jnp.log(1.0 + e), e


def _sb_fwd(p, tri):
    S = p.shape[0]
    T = min(SB_T, S)
    NQ = S // T
    assert NQ <= 128
    qb, kb, vb = O_SQ // 128, O_SK // 128, O_SV // 128

    def body(q_ref, k_ref, v_ref, tri_ref, o_ref, cs_ref, o_acc, run, zbuf, abuf):
        i = pl.program_id(1)
        lane = lax.broadcasted_iota(jnp.int32, (1, 128), 1)
        tri_after = tri_ref[0]
        qs = [jnp.where((lane >= 64) if hh else (lane < 64), q_ref[...], jnp.zeros_like(q_ref[...]))
              * jnp.asarray(SB_SCALE, BF16) for hh in range(2)]
        cs_ref[...] = jnp.zeros_like(cs_ref)
        o_acc[...] = jnp.zeros_like(o_acc)
        run[...] = jnp.zeros_like(run)

        def kv(ref, j):
            return ref[pl.ds(pl.multiple_of(j * T, T), T), :]

        for hh in range(2):
            zbuf[hh] = _dot_nt(qs[hh], kv(k_ref, i))

        def block(t, diagonal):
            j = i - t
            if diagonal:
                msk = lax.broadcasted_iota(jnp.int32, (T, T), 1) < lax.broadcasted_iota(jnp.int32, (T, T), 0)
            if not diagonal:
                av = [_dot(abuf[hh], kv(v_ref, j + 1)) for hh in range(2)]
            lss, exs, tot, zn = [], [], [], []
            for hh in range(2):
                z = zbuf[hh]
                sp, _ = _softplus_parts(z)
                lss.append(z - sp)
                if diagonal:
                    sp = jnp.where(msk, sp, 0.0)
                exs.append(_dot(sp.astype(BF16), tri_after))
                tot.append(sp[:, 0:1])
                zn.append(_dot_nt(qs[hh], kv(k_ref, jnp.maximum(j - 1, 0))))
            for hh in range(2):
                csl = slice(hh * 128, (hh + 1) * 128)
                cs = run[hh]
                a = jnp.exp(lss[hh] - exs[hh] - cs)
                if diagonal:
                    a = jnp.where(msk, a, 0.0)
                abuf[hh] = a.astype(BF16)
                cs_ref[:, csl] = jnp.where(lane == j, cs, cs_ref[:, csl])
                run[hh] = cs + exs[hh][:, 0:1] + tot[hh]
            for hh in range(2):
                if not diagonal:
                    o_acc[hh] += av[hh]
                zbuf[hh] = zn[hh]

        block(0, True)

        def step(t, carry):
            block(t, False)
            return carry

        lax.fori_loop(1, i + 1, step, 0)
        o_ref[...] = jnp.where(lane < 64, o_acc[0] + _dot(abuf[0], kv(v_ref, 0)),
                               o_acc[1] + _dot(abuf[1], kv(v_ref, 0))).astype(o_ref.dtype)

    return pl.pallas_call(
        body, name="sb_fwd", grid=(SB_HEADS // 2, NQ),
        scratch_shapes=[pltpu.VMEM((2, T, 128), F32), pltpu.VMEM((2, T, 1), F32), pltpu.VMEM((2, T, T), F32),
                        pltpu.VMEM((2, T, T), BF16)],
        in_specs=[pl.BlockSpec((T, 128), lambda h, i: (i, qb + h)),
                  pl.BlockSpec((S, 128), lambda h, i: (0, kb + h)),
                  pl.BlockSpec((S, 128), lambda h, i: (0, vb + h)),
                  pl.BlockSpec((1, T, T), lambda h, i: (0, 0, 0))],
        out_specs=[pl.BlockSpec((T, 128), lambda h, i: (i, h)),
                   pl.BlockSpec((T, 256), lambda h, i: (i, h))],
        out_shape=[jax.ShapeDtypeStruct((S, SB_HEADS * SB_DIM), BF16),
                   jax.ShapeDtypeStruct((S, SB_HEADS * 128), F32)],
        compiler_params=_cparams(("parallel", "arbitrary")),
    )(p, p, p, tri)


def _sb_bwd(p, carries, dy, tri):
    S = p.shape[0]
    T = min(SB_T, S)
    NQ = S // T
    qb, kb, vb = O_SQ // 128, O_SK // 128, O_SV // 128

    def body(q_ref, k_ref, v_ref, cs_ref, dy_ref, tri_ref, dq_ref, dk_ref, dv_ref, dk_acc, dv_acc, dq_acc, run,
             zbuf, dabuf, dzbuf, abuf):
        i = pl.program_id(1)

        @pl.when(i == 0)
        def _():
            dk_acc[...] = jnp.zeros_like(dk_acc)
            dv_acc[...] = jnp.zeros_like(dv_acc)

        lane = lax.broadcasted_iota(jnp.int32, (1, 128), 1)
        tri_after, tri_upto = tri_ref[0], tri_ref[1]
        hms = [(lane >= 64) if hh else (lane < 64) for hh in range(2)]
        qs = [jnp.where(hm, q_ref[...], jnp.zeros_like(q_ref[...])) * jnp.asarray(SB_SCALE, BF16) for hm in hms]
        dos = [jnp.where(hm, dy_ref[...], jnp.zeros_like(dy_ref[...])) for hm in hms]
        qst = [t.T for t in qs]
        dost = [t.T for t in dos]
        dq_acc[...] = jnp.zeros_like(dq_acc)
        run[...] = jnp.zeros_like(run)
        dzbuf[...] = jnp.zeros_like(dzbuf)
        abuf[...] = jnp.zeros_like(abuf)

        def kv(ref, j):
            return ref[pl.ds(pl.multiple_of(j * T, T), T), :]

        def flush(jp):
            kp = kv(k_ref, jp)
            dq_add = [_dot(dzbuf[hh], kp) for hh in range(2)]
            dk_add = _dot(qst[0], dzbuf[0]) + _dot(qst[1], dzbuf[1])
            dv_add = _dot(dost[0], abuf[0]) + _dot(dost[1], abuf[1])
            return dq_add, dk_add, dv_add

        def apply(jp, adds):
            dq_add, dk_add, dv_add = adds
            cols = pl.ds(pl.multiple_of(jp * T, T), T)
            for hh in range(2):
                dq_acc[hh] += dq_add[hh]
            dk_acc[:, cols] += dk_add
            dv_acc[:, cols] += dv_add

        for hh in range(2):
            zbuf[hh] = _dot_nt(qs[hh], kv(k_ref, 0))
            dabuf[hh] = _dot_nt(dos[hh], kv(v_ref, 0))

        def block(j, diagonal):
            jp = jnp.maximum(j - 1, 0)
            if diagonal:
                msk = lax.broadcasted_iota(jnp.int32, (T, T), 1) < lax.broadcasted_iota(jnp.int32, (T, T), 0)
            kp = kv(k_ref, jp)
            dq_add = [_dot(dzbuf[hh], kp) for hh in range(2)]
            sigs, lss, exs, zn, dan, dk_part, dv_part = [], [], [], [], [], [], []
            for hh in range(2):
                z = zbuf[hh]
                sp, _ = _softplus_parts(z)
                lss.append(z - sp)
                sigs.append(jnp.exp(lss[hh]))
                if diagonal:
                    sp = jnp.where(msk, sp, 0.0)
                exs.append(_dot(sp.astype(BF16), tri_after))
                if not diagonal:
                    zn.append(_dot_nt(qs[hh], kv(k_ref, j + 1)))
                dk_part.append(_dot(qst[hh], dzbuf[hh]))
            pgs, gs = [], []
            for hh in range(2):
                csl = slice(hh * 128, (hh + 1) * 128)
                cs = jnp.sum(jnp.where(lane == j, cs_ref[:, csl], 0.0), axis=-1, keepdims=True)
                a = jnp.exp(lss[hh] - exs[hh] - cs)
                if diagonal:
                    a = jnp.where(msk, a, 0.0)
                abuf_new = a.astype(BF16)
                g = a * dabuf[hh]
                gs.append((g, abuf_new))
                pgs.append(_dot(g.astype(BF16), tri_upto))
                if not diagonal:
                    dan.append(_dot_nt(dos[hh], kv(v_ref, j + 1)))
                dv_part.append(_dot(dost[hh], abuf[hh]))
            adds = (dq_add, dk_part[0] + dk_part[1], dv_part[0] + dv_part[1])
            for hh in range(2):
                g, abuf_new = gs[hh]
                cg = run[hh]
                dz = g - sigs[hh] * (cg + pgs[hh])
                if diagonal:
                    dz = jnp.where(msk, dz, 0.0)
                run[hh] = cg + pgs[hh][:, T - 1:T]
                dzbuf[hh] = dz.astype(BF16)
                abuf[hh] = abuf_new
            apply(jp, adds)
            if not diagonal:
                for hh in range(2):
                    zbuf[hh] = zn[hh]
                    dabuf[hh] = dan[hh]

        def step(j, carry):
            block(j, False)
            return carry

        lax.fori_loop(0, i, step, 0)
        block(i, True)
        apply(i, flush(i))
        dq_ref[...] = (jnp.where(lane < 64, dq_acc[0], dq_acc[1]) * SB_SCALE).astype(dq_ref.dtype)

        @pl.when(i == NQ - 1)
        def _():
            dk_ref[...] = dk_acc[...].T.astype(dk_ref.dtype)
            dv_ref[...] = dv_acc[...].T.astype(dv_ref.dtype)

    W = SB_HEADS * SB_DIM
    return pl.pallas_call(
        body, name="sb_bwd", grid=(SB_HEADS // 2, NQ),
        in_specs=[pl.BlockSpec((T, 128), lambda h, i: (i, qb + h)),
                  pl.BlockSpec((S, 128), lambda h, i: (0, kb + h)),
                  pl.BlockSpec((S, 128), lambda h, i: (0, vb + h)),
                  pl.BlockSpec((T, 256), lambda h, i: (i, h)),
                  pl.BlockSpec((T, 128), lambda h, i: (i, h)),
                  pl.BlockSpec((2, T, T), lambda h, i: (0, 0, 0))],
        out_specs=[pl.BlockSpec((T, 128), lambda h, i: (i, h)),
                   pl.BlockSpec((S, 128), lambda h, i: (0, h)),
                   pl.BlockSpec((S, 128), lambda h, i: (0, h))],
        out_shape=[jax.ShapeDtypeStruct((S, W), BF16)] * 3,
        scratch_shapes=[pltpu.VMEM((128, S), F32), pltpu.VMEM((128, S), F32), pltpu.VMEM((2, T, 128), F32),
                        pltpu.VMEM((2, T, 1), F32), pltpu.VMEM((2, T, T), F32), pltpu.VMEM((2, T, T), F32),
                        pltpu.VMEM((2, T, T), BF16), pltpu.VMEM((2, T, T), BF16)],
        compiler_params=_cparams(("parallel", "arbitrary")),
    )(p, p, p, carries, dy, tri)


def _exchange(srcs, out_shapes, src_slice, dst_slice, name, deps=()):
    n, nd = len(srcs), len(deps)

    def body(*refs):
        ins, outs = refs[:n], refs[n + nd:2 * n + nd]
        send_sems, recv_sems, loc_sems = refs[2 * n + nd:]
        x, y, c = lax.axis_index("x"), lax.axis_index("y"), lax.axis_index("c")
        me = 4 * x + 2 * y + c
        local = [pltpu.make_async_copy(src_slice(t, ins[t], me), dst_slice(t, outs[t], me), loc_sems.at[t])
                 for t in range(n)]
        for cp in local:
            cp.start()
        sends, recvs = [], []
        for k in (1, 2, 4, 6, 3, 5, 7):
            px = 1 - x if k & 4 else x
            py = 1 - y if k & 2 else y
            pc = 1 - c if k & 1 else c
            peer = 4 * px + 2 * py + pc
            for t in range(n):
                s = t * 7 + k - 1
                sends.append(pltpu.make_async_remote_copy(
                    src_ref=src_slice(t, ins[t], peer), dst_ref=dst_slice(t, outs[t], me),
                    send_sem=send_sems.at[s], recv_sem=recv_sems.at[s],
                    device_id=(px, py, pc), device_id_type=pl.DeviceIdType.MESH))
                recvs.append(pltpu.make_async_remote_copy(
                    src_ref=src_slice(t, ins[t], me), dst_ref=dst_slice(t, outs[t], peer),
                    send_sem=send_sems.at[s], recv_sem=recv_sems.at[s],
                    device_id=(px, py, pc), device_id_type=pl.DeviceIdType.MESH))
        for cp in sends:
            cp.start()
        for cp in recvs:
            cp.wait_recv()
        for cp in sends:
            cp.wait_send()
        for cp in local:
            cp.wait()

    anyspec = pl.BlockSpec(memory_space=pl.ANY)
    return pl.pallas_call(
        body, name=name, in_specs=[anyspec] * (n + nd), out_specs=[anyspec] * n,
        out_shape=[jax.ShapeDtypeStruct(s, d) for s, d in out_shapes],
        scratch_shapes=[pltpu.SemaphoreType.DMA((7 * n,)), pltpu.SemaphoreType.DMA((7 * n,)),
                        pltpu.SemaphoreType.DMA((n,))],
    )(*srcs, *deps)


def _all_gather_lead(xs, name, deps=()):
    return _exchange(
        xs, [((N_DEV,) + x.shape, x.dtype) for x in xs],
        lambda t, ref, peer: ref, lambda t, ref, who: ref.at[who], name, deps)


def _all_to_all_lead(xs, name):
    return _exchange(
        xs, [(x.shape, x.dtype) for x in xs],
        lambda t, ref, peer: ref.at[peer], lambda t, ref, who: ref.at[who], name)


_W_AXIS = {"w_in": 1, "w_ret_out": 0, "w_sb_out": 0, "w_mix_out": 0, "w_up": 1, "w_down": 0}
_W_NAMES = tuple(_W_AXIS)


def _window(ref, axis, who, width, count=1):
    start = pl.multiple_of(who * width, width)
    return ref.at[pl.ds(start, count * width), :] if axis == 0 else ref.at[:, pl.ds(start, count * width)]


_HBM = pl.BlockSpec(memory_space=pltpu.HBM)
_SEM = pl.BlockSpec(memory_space=pltpu.SEMAPHORE)
_EFFECT = pltpu.SideEffectType.DATAFLOW_SIDE_EFFECTING


_ALL_PEERS = (0, 1, 2, 4, 6, 3, 5, 7)
_SAME_CORE = (0, 2, 4, 6)


def _exchange_start(srcs, shapes, src_slice, dst_slice, name, deps=(), ks=_ALL_PEERS):
    n, nd = len(srcs), len(deps)
    lands = [pltpu.with_memory_space_constraint(lax.empty(s, d), pltpu.HBM) for s, d in shapes]

    def body(*refs):
        ins, lnd = refs[:n], refs[n:2 * n]
        sems = refs[2 * n + nd:4 * n + nd]
        token = refs[6 * n + nd]
        x, y, c = lax.axis_index("x"), lax.axis_index("y"), lax.axis_index("c")
        me = 4 * x + 2 * y + c
        for k in ks:
            px = 1 - x if k & 4 else x
            py = 1 - y if k & 2 else y
            pc = 1 - c if k & 1 else c
            peer = 4 * px + 2 * py + pc
            for t in range(n):
                pltpu.make_async_remote_copy(
                    src_ref=src_slice(t, ins[t], peer), dst_ref=dst_slice(t, lnd[t], me),
                    send_sem=sems[2 * t], recv_sem=sems[2 * t + 1],
                    device_id=(px, py, pc), device_id_type=pl.DeviceIdType.MESH).start()
        token[...] = jnp.zeros_like(token)

    res = pl.pallas_call(
        body, name=name, in_specs=[_HBM] * (2 * n) + [pl.BlockSpec(memory_space=pl.ANY)] * nd,
        out_specs=[_SEM] * (2 * n) + [_HBM] * (2 * n) + [pl.BlockSpec(memory_space=pltpu.VMEM)],
        out_shape=[pltpu.SemaphoreType.DMA(())] * (2 * n) + [pltpu.HBM(s.shape, s.dtype) for s in srcs]
        + [pltpu.HBM(s.shape, s.dtype) for s in lands] + [jax.ShapeDtypeStruct((8, 128), F32)],
        input_output_aliases={t: 2 * n + t for t in range(2 * n)},
        compiler_params=pltpu.CompilerParams(has_side_effects=_EFFECT),
    )(*[pltpu.with_memory_space_constraint(s, pltpu.HBM) for s in srcs], *lands, *deps)
    return dict(n=n, sems=res[:2 * n], srcs=res[2 * n:3 * n], lands=res[3 * n:4 * n], token=res[4 * n])


def _exchange_wait(h, after, name):
    n, ns = h['n'], len(h['srcs'])
    span = h.get('span', lambda t, ref: ref)

    def body(*refs):
        lnd = refs[ns:ns + n]
        sems = refs[ns + n:ns + 3 * n]
        x, y, c = lax.axis_index("x"), lax.axis_index("y"), lax.axis_index("c")
        for t in range(n):
            w = span(t, lnd[t])
            cp = pltpu.make_async_remote_copy(src_ref=w, dst_ref=w, send_sem=sems[2 * t], recv_sem=sems[2 * t + 1],
                                              device_id=(x, y, 1 - c), device_id_type=pl.DeviceIdType.MESH)
            cp.wait_send()
            cp.wait_recv()

    after = list(after)
    res = pl.pallas_call(
        body, name=name,
        in_specs=[_HBM] * (ns + n) + [_SEM] * (2 * n) + [pl.BlockSpec(memory_space=pl.ANY)] * len(after),
        out_specs=[_HBM] * (ns + n),
        out_shape=[pltpu.HBM(s.shape, s.dtype) for s in h['srcs']] + [pltpu.HBM(s.shape, s.dtype) for s in h['lands']],
        input_output_aliases={t: t for t in range(ns + n)},
        compiler_params=pltpu.CompilerParams(has_side_effects=_EFFECT),
    )(*h['srcs'], *h['lands'], *h['sems'], *after)
    return list(res[ns:])


def _sibling_start(lands, win, name):
    n = len(lands)

    def body(*refs):
        lnd = refs[:n]
        sems = refs[n:3 * n]
        token = refs[4 * n]
        x, y, c = lax.axis_index("x"), lax.axis_index("y"), lax.axis_index("c")
        for ox in (x, 1 - x):
            for oy in (y, 1 - y):
                owner = 4 * ox + 2 * oy + c
                for t in range(n):
                    w = win(t, lnd[t], owner)
                    pltpu.make_async_remote_copy(
                        src_ref=w, dst_ref=w, send_sem=sems[2 * t], recv_sem=sems[2 * t + 1],
                        device_id=(x, y, 1 - c), device_id_type=pl.DeviceIdType.MESH).start()
        token[...] = jnp.zeros_like(token)

    res = pl.pallas_call(
        body, name=name, in_specs=[_HBM] * n,
        out_specs=[_SEM] * (2 * n) + [_HBM] * n + [pl.BlockSpec(memory_space=pltpu.VMEM)],
        out_shape=[pltpu.SemaphoreType.DMA(())] * (2 * n) + [pltpu.HBM(s.shape, s.dtype) for s in lands]
        + [jax.ShapeDtypeStruct((8, 128), F32)],
        input_output_aliases={t: 2 * n + t for t in range(n)},
        compiler_params=pltpu.CompilerParams(has_side_effects=_EFFECT),
    )(*lands)
    return dict(n=n, sems=res[:2 * n], srcs=[], lands=res[2 * n:3 * n], token=res[3 * n])


def _gather_start(shards, names, tag, deps=(), two_level=False):
    xs = [shards[nm] for nm in names]
    axes = [_W_AXIS[nm] for nm in names]
    widths = [x.shape[ax] for x, ax in zip(xs, axes)]
    shapes = [(tuple(d * (N_DEV if a == ax else 1) for a, d in enumerate(x.shape)), x.dtype) for x, ax in zip(xs, axes)]
    src = lambda t, ref, peer: ref
    dst = lambda t, ref, who: _window(ref, axes[t], who, widths[t])
    h = _exchange_start(xs, shapes, src, dst, "gw_start_" + tag, deps, _SAME_CORE if two_level else _ALL_PEERS)
    h['tag'] = "gw_wait_" + tag
    if two_level:
        h['span'] = lambda t, ref: _window(ref, axes[t], 0, widths[t], len(_SAME_CORE))
        h['second'] = (dst, "gw_pass_" + tag, "gw_passed_" + tag)
    return h


def _scatter_start(grads, names, tag):
    xs = [grads[nm] for nm in names]
    axes = [_W_AXIS[nm] for nm in names]
    widths = [x.shape[ax] // N_DEV for x, ax in zip(xs, axes)]
    shapes = [((N_DEV,) + tuple(d // (N_DEV if a == ax else 1) for a, d in enumerate(x.shape)), x.dtype)
              for x, ax in zip(xs, axes)]
    src = lambda t, ref, peer: _window(ref, axes[t], peer, widths[t])
    dst = lambda t, ref, who: ref.at[who]
    h = _exchange_start(xs, shapes, src, dst, "sg_start_" + tag)
    h['tag'] = "sg_wait_" + tag
    return h


def _finish(h, after):
    return _exchange_wait(h, after, h['tag'])


class _LayerWeights:
    def __init__(self, groups, started):
        self.groups = groups
        self.started = started
        self.got = {}
        self.after = None

    def __getitem__(self, nm):
        if nm not in self.got:
            for names, h in self.groups:
                if nm in names:
                    self.got.update(zip(names, _finish(h, list(self.after) + self.started)))
        return self.got[nm]


def _adam_math(p_ref, w, m, v):
    g = p_ref[0].astype(F32)
    for s in range(1, p_ref.shape[0]):
        g = g + p_ref[s].astype(F32)
    bc1 = 1.0 / (1.0 - ADAM_B1 ** ADAM_STEP)
    bc2 = 1.0 / (1.0 - ADAM_B2 ** ADAM_STEP)
    mm = ADAM_B1 * m + (1.0 - ADAM_B1) * g
    vv = ADAM_B2 * v + (1.0 - ADAM_B2) * jnp.square(g)
    return g, -ADAM_LR * ((mm * bc1) / (jnp.sqrt(vv * bc2) + ADAM_EPS) + ADAM_WD * w), mm, vv


def _adam(parts, w, m, v, name, tr=256):
    P, R, C = parts.shape
    tr = min(tr, R)
    assert R % tr == 0

    def body(p_ref, w_ref, m_ref, v_ref, *outs):
        for o_ref, val in zip(outs, _adam_math(p_ref, w_ref[...], m_ref[...], v_ref[...])):
            o_ref[...] = val

    spec = pl.BlockSpec((tr, C), lambda i: (i, 0))
    return pl.pallas_call(
        body, name=name, grid=(R // tr,),
        in_specs=[pl.BlockSpec((P, tr, C), lambda i: (0, i, 0)), spec, spec, spec],
        out_specs=[spec] * 4, out_shape=[jax.ShapeDtypeStruct((R, C), F32)] * 4,
        compiler_params=_cparams(("parallel",)),
    )(parts, w, m, v)


def _adam_layer(parts, w, m, v, l, prev, name, tr=256):
    P, R, C = parts.shape
    tr = min(tr, R)
    assert R % tr == 0 and w.shape == (DEPTH, R, C)
    npv = 0 if prev is None else 4

    def body(p_ref, w_ref, m_ref, v_ref, *rest):
        for o_ref, val in zip(rest[npv:], _adam_math(p_ref, w_ref[0], m_ref[0], v_ref[0])):
            o_ref[0] = val

    spec = pl.BlockSpec((1, tr, C), lambda i: (l, i, 0))
    return pl.pallas_call(
        body, name=name, grid=(R // tr,),
        in_specs=[pl.BlockSpec((P, tr, C), lambda i: (0, i, 0)), spec, spec, spec]
        + [pl.BlockSpec(memory_space=pl.ANY)] * npv,
        out_specs=[spec] * 4, out_shape=[jax.ShapeDtypeStruct((DEPTH, R, C), F32)] * 4,
        input_output_aliases={4 + t: t for t in range(npv)},
        compiler_params=_cparams(("parallel",)),
    )(parts, w, m, v, *([] if prev is None else prev))


def _mod_partial(cact_all, w_ada, l, b_ada_l):
    R, (_, D, n) = cact_all.shape[0], w_ada.shape

    def body(c_ref, w_ref, b_ref, o_ref):
        o_ref[...] = _dot(c_ref[...].astype(BF16), w_ref[...].astype(BF16)) + b_ref[...]

    return pl.pallas_call(
        body, name="mod_partial", grid=(1,), out_shape=jax.ShapeDtypeStruct((R, n), F32),
        in_specs=[pl.BlockSpec((R, D), lambda i: (0, 0)), pl.BlockSpec((None, D, n), lambda i: (l, 0, 0)),
                  pl.BlockSpec((1, n), lambda i: (0, 0))],
        out_specs=pl.BlockSpec((R, n), lambda i: (0, 0)),
        compiler_params=pltpu.CompilerParams(vmem_limit_bytes=VMEM_LIMIT),
    )(cact_all, w_ada, b_ada_l)


def _ada_grad(cact_t, dmod):
    D, n = cact_t.shape[0], dmod.shape[1]

    def body(c_ref, d_ref, o_ref):
        ct = c_ref[...].astype(BF16).astype(F32)
        dm = d_ref[...].astype(BF16).astype(F32)
        acc = ct[:, 0:1] * dm[0:1, :]
        for b in range(1, N_DEV):
            acc = acc + ct[:, b:b + 1] * dm[b:b + 1, :]
        o_ref[0] = acc

    return pl.pallas_call(
        body, name="ada_grad", out_shape=jax.ShapeDtypeStruct((1, D, n), F32),
        compiler_params=pltpu.CompilerParams(vmem_limit_bytes=VMEM_LIMIT),
    )(cact_t, dmod)


def _norm_mod(x, r, gv, sh):
    return x * r * gv + sh


def _silu(x):
    return x * _sigmoid(x)


def _rstd(x):
    return lax.rsqrt(jnp.mean(x * x, axis=-1, keepdims=True) + EPS)


def _residual_epi(acc, x, g):
    xn = x + g * acc
    return acc, xn, _rstd(xn)


def _residual_norm_epi(acc, x, g, gv, sh):
    xn = x + g * acc
    r = _rstd(xn)
    return acc, xn, _norm_mod(xn, r, gv, sh), r


def _layer_fwd(x0, r1, h1, mod, gn1, gn2, nxt, W, rc, tri):
    S = x0.shape[0]
    sh1, sc1, g1m, sh2, sc2, g2m = [mod[i:i + 1] for i in range(N_MOD)]
    gv1 = gn1 * (1.0 + sc1)
    gv2 = gn2 * (1.0 + sc2)
    W.after = [h1]
    (p,) = _mm(h1, W["w_in"], tm=2048, outs=(BF16,), name="mm_in")
    yret, rstate = _ret_fwd(p, rc)
    ysb, sbc = _sb_fwd(p, tri)
    W.after = [ysb]
    (ya,) = _mm(yret, W["w_ret_out"], tm=512, a_ex=[(p, 'a', O_RG)],
                pro=lambda yr, g: _silu(g.astype(F32)) * yr.astype(F32), outs=(BF16,), name="mm_ret_out")
    yb, mg = _mm(ysb, W["w_sb_out"], o_ex=[(ya, 'o'), (p, 'o', O_GA), (p, 'o', O_GB)],
                 epi=lambda acc, a, ga, gb: (acc, _sigmoid(ga.astype(F32)) * a.astype(F32)
                                             + _sigmoid(gb.astype(F32)) * acc),
                 outs=(BF16, BF16), name="mm_sb_out")
    mo, x1, h2, r2 = _mm(mg, W["w_mix_out"], tm=512, o_ex=[(x0, 'o'), (g1m, 'n'), (gv2, 'n'), (sh2, 'n')],
                         epi=_residual_norm_epi, outs=(BF16, F32, BF16), cols=1, name="mm_mix_out")
    (act,) = _mm(h2, W["w_up"], tm=2048, epi=lambda acc: (jnp.maximum(acc, 0.0),), outs=(BF16,), name="mm_up")
    if nxt is None:
        dn, x2, r_out = _mm(act, W["w_down"], tm=512, pro=lambda a: a * a, o_ex=[(x1, 'o'), (g2m, 'n')],
                            epi=_residual_epi, outs=(BF16, F32), cols=1, name="mm_down_last")
        h_out = None
    else:
        dn, x2, h_out, r_out = _mm(act, W["w_down"], tm=512, pro=lambda a: a * a,
                                   o_ex=[(x1, 'o'), (g2m, 'n'), (nxt[0], 'n'), (nxt[1], 'n')],
                                   epi=_residual_norm_epi, outs=(BF16, F32, BF16), cols=1, name="mm_down")
    saved = dict(x0=x0, r1=r1, h1=h1, p=p, yret=yret, rstate=rstate, ysb=ysb, sbc=sbc, ya=ya, yb=yb, mg=mg, mo=mo, x1=x1,
                 r2=r2, h2=h2, act=act, dn=dn, gv1=gv1, gv2=gv2, mod=mod, gn1=gn1, gn2=gn2)
    return x2, r_out, h_out, saved


def _norm_bwd(dh, x, r, dres, gv, gn, extra_rows=(), extra_vecs=(), extra_fn=None, extra_outs=(), name="norm_bwd"):
    D = x.shape[1]
    ne = len(extra_rows)

    def fn(dh_t, x_t, dres_t, *rest):
        er, rest = rest[:ne], rest[ne:]
        gv_t = rest[0]
        ev, r_t = rest[1:-1], rest[-1]
        xh = x_t * r_t
        dxh = dh_t * gv_t
        dx = r_t * (dxh - xh * jnp.mean(dxh * xh, axis=-1, keepdims=True)) + dres_t
        base = (dx, dh_t, dh_t * xh)
        if extra_fn is None:
            return base
        return base + tuple(extra_fn(dx, *er, *ev))

    return _ew(fn, [dh, x, dres] + list(extra_rows), vecs=[gv] + list(extra_vecs), cols=[r],
               outs=[('row', D, F32), ('sum', D), ('sum', D)] + list(extra_outs), name=name)


def _gate_bwd(dx, dn, g):
    return dx * dn.astype(F32), dx * g


_GATE_OUTS = [('sum', D_MODEL), ('row', D_MODEL, BF16)]


def _layer_bwd(dx2, d_g2m, d_dn, sv, below, W, rc, tri, emit):
    mod = sv['mod']
    sh1, sc1, g1m, sh2, sc2, g2m = [mod[i:i + 1] for i in range(N_MOD)]
    D = D_MODEL
    p = sv['p']
    (d_up,) = _mm(d_dn, W["w_down"], tb=True, tm=2048, o_ex=[(sv['act'], 'o')],
                  epi=lambda acc, a: (acc * 2.0 * a.astype(F32),), outs=(BF16,), name="mm_down_dx")
    (gw_down,) = _mm(sv['act'], d_dn, ta=True, tk=DW_TK, pro=lambda a: a * a, outs=(BF16,), name="mm_down_dw")
    (gw_up,) = _mm(sv['h2'], d_up, ta=True, tk=DW_TK, outs=(BF16,), name="mm_up_dw")
    tok = emit(dict(w_down=gw_down, w_up=gw_up), "mlp")
    (d_h2,) = _mm(d_up, W["w_up"], tb=True, tm=2048, tk=2048, outs=(F32,), name="mm_up_dx", deps=[tok])
    dx1, d_sh2, s_h2, d_g1m, d_mo = _norm_bwd(
        d_h2, sv['x1'], sv['r2'], dx2, sv['gv2'], sv['gn2'],
        extra_rows=[sv['mo']], extra_vecs=[g1m],
        extra_fn=lambda dx, mo, g: (dx * mo.astype(F32), dx * g),
        extra_outs=[('sum', D), ('row', D, BF16)], name="norm_bwd_mlp")
    d_sc2 = sv['gn2'] * s_h2
    d_gn2 = (1.0 + sc2) * s_h2
    def mix_epi(acc, ya, yb, ga, gb):
        sa, sb = _sigmoid(ga.astype(F32)), _sigmoid(gb.astype(F32))
        return (acc * sa, acc * sb, acc * ya.astype(F32) * sa * (1.0 - sa), acc * yb.astype(F32) * sb * (1.0 - sb))

    d_ya, d_yb, d_ga, d_gb = _mm(d_mo, W["w_mix_out"], tb=True, tm=512,
                                 o_ex=[(sv['ya'], 'o'), (sv['yb'], 'o'), (p, 'o', O_GA), (p, 'o', O_GB)], epi=mix_epi,
                                 outs=(BF16,) * 4, name="mm_mix_dx")
    (gw_mix,) = _mm(sv['mg'], d_mo, ta=True, tk=DW_TK, outs=(BF16,), name="mm_mix_dw")

    def ro_epi(acc, g, yr):
        gf = g.astype(F32)
        s = _sigmoid(gf)
        return (acc * yr.astype(F32) * s * (1.0 + gf * (1.0 - s)), acc * gf * s)

    d_rg, d_yret = _mm(d_ya, W["w_ret_out"], tb=True, tm=512, tn=2048, o_ex=[(p, 'o', O_RG), (sv['yret'], 'o')],
                       epi=ro_epi, outs=(BF16, BF16), name="mm_ret_dx")
    (gw_ro,) = _mm(sv['yret'], d_ya, ta=True, tm=512, tk=DW_TK, a_ex=[(p, 'a', O_RG)],
                   pro=lambda yr, g: _silu(g.astype(F32)) * yr.astype(F32), outs=(BF16,), name="mm_ret_dw")
    (gw_so,) = _mm(sv['ysb'], d_yb, ta=True, tk=DW_TK, outs=(BF16,), name="mm_sb_dw")
    tok = emit(dict(w_mix_out=gw_mix, w_ret_out=gw_ro, w_sb_out=gw_so), "mix")
    (d_ysb,) = _mm(d_yb, W["w_sb_out"], tb=True, outs=(BF16,), name="mm_sb_dx", deps=[tok])
    d_sq, d_sk, d_sv = _sb_bwd(p, sv['sbc'], d_ysb, tri)
    d_ret = _ret_bwd(p, sv['rstate'], d_yret, rc)
    dp = [d_ret, d_rg, d_sq, d_sk, d_sv, d_ga, d_gb]
    (gw_in,) = _mm(sv['h1'], dp, ta=True, tk=1024, outs=(BF16,), name="mm_in_dw")
    tok = emit(dict(w_in=gw_in), "in")
    d_h = _mm_parts_ring(dp, W["w_in"], name="mm_in_dx", deps=[tok])
    if below is None:
        dx0, d_sh1, s_h1 = _norm_bwd(d_h, sv['x0'], sv['r1'], dx1, sv['gv1'], sv['gn1'], name="norm_bwd_mix")
        gate_below = (None, None)
    else:
        dx0, d_sh1, s_h1, *gate_below = _norm_bwd(
            d_h, sv['x0'], sv['r1'], dx1, sv['gv1'], sv['gn1'], extra_rows=[below['dn']],
            extra_vecs=[below['mod'][N_MOD - 1:N_MOD]], extra_fn=_gate_bwd, extra_outs=_GATE_OUTS,
            name="norm_bwd_mix_gate")
    d_sc1 = sv['gn1'] * s_h1
    d_gn1 = (1.0 + sc1) * s_h1
    d_mod = jnp.concatenate([d_sh1, d_sc1, d_g1m, d_sh2, d_sc2, d_g2m], axis=1)
    return dx0, gate_below, d_mod, d_gn1, d_gn2


def kernel(x, c, norm_mix_g, w_in, w_ret_out, w_sb_out, w_mix_out, norm_mlp_g, w_up, w_down, w_ada, b_ada, final_g, loss_target, m_norm_mix_g, m_w_in, m_w_ret_out, m_w_sb_out, m_w_mix_out, m_norm_mlp_g, m_w_up, m_w_down, m_w_ada, m_b_ada, m_final_g, v_norm_mix_g, v_w_in, v_w_ret_out, v_w_sb_out, v_w_mix_out, v_norm_mlp_g, v_w_up, v_w_down, v_w_ada, v_b_ada, v_final_g):
    S, D = x.shape[1], x.shape[2]
    x0 = x.reshape(S, D)
    tgt = loss_target.reshape(S, D)
    me = 4 * lax.axis_index("x") + 2 * lax.axis_index("y") + lax.axis_index("c")
    wts = dict(w_in=w_in, w_ret_out=w_ret_out, w_sb_out=w_sb_out, w_mix_out=w_mix_out, w_up=w_up, w_down=w_down)
    mts = dict(w_in=m_w_in, w_ret_out=m_w_ret_out, w_sb_out=m_w_sb_out, w_mix_out=m_w_mix_out, w_up=m_w_up, w_down=m_w_down)
    vts = dict(w_in=v_w_in, w_ret_out=v_w_ret_out, w_sb_out=v_w_sb_out, w_mix_out=v_w_mix_out, w_up=v_w_up, w_down=v_w_down)
    rc = _ret_consts(S)
    tri = _tri()

    (cact,) = _ew(lambda t: (_silu(t),), [jnp.pad(c, ((0, 7), (0, 0)))], outs=[('row', D, F32)], name="silu_c")
    (cact_all,) = _all_gather_lead([cact[0:1]], "gather_c")
    cact_all = cact_all.reshape(N_DEV, D)
    cact16 = jnp.pad(cact_all, ((0, 8), (0, 0)))
    n_ada = w_ada.shape[2]
    b_loc = lax.dynamic_slice_in_dim(b_ada, me * n_ada, n_ada, axis=1)
    mods = [_mod_partial(cact16, w_ada, l, b_loc[l:l + 1])[:N_DEV] for l in range(DEPTH)]
    modp = jnp.stack(mods, axis=1)
    (modr,) = _all_to_all_lead([modp], "scatter_mod")
    mod_full = jnp.transpose(modr, (1, 0, 2)).reshape(DEPTH, N_MOD, D)

    def cast(nm, l, deps=()):
        w = wts[nm]
        _, R, C = w.shape
        tr = min(256, R)

        def body(w_ref, *rest):
            rest[-1][...] = w_ref[...].astype(BF16)

        return pl.pallas_call(
            body, name="cast_bf16", grid=(R // tr,),
            in_specs=[pl.BlockSpec((None, tr, C), lambda i: (l, i, 0))]
            + [pl.BlockSpec(memory_space=pl.ANY)] * len(deps),
            out_specs=pl.BlockSpec((tr, C), lambda i: (i, 0)), out_shape=jax.ShapeDtypeStruct((R, C), BF16),
            compiler_params=_cparams(("arbitrary",)),
        )(w, *deps)

    rest = tuple(nm for nm in _W_NAMES if nm != "w_in")
    sh = [{} for _ in range(DEPTH)]
    sh[0]["w_in"] = cast("w_in", 0)
    h_in = _gather_start(sh[0], ("w_in",), "0_in", [modr], two_level=True)
    in_flight = [h_in['token']]
    for l in range(DEPTH):
        for nm in _W_NAMES:
            if nm not in sh[l]:
                sh[l][nm] = cast(nm, l, in_flight)

    pre = [(norm_mix_g[l:l + 1] * (1.0 + mod_full[l][1:2]), mod_full[l][0:1]) for l in range(DEPTH)]

    def first(t, gv, sh):
        r = _rstd(t)
        return r, _norm_mod(t, r, gv, sh)

    xs = x0
    rs, hs = _ew(first, [x0], vecs=list(pre[0]), outs=[('col', F32), ('row', D, BF16)], name="row_rstd",
                 deps=in_flight)

    win, pass_name, passed_name = h_in['second']
    before_wait = [hs] + [sh[l][nm] for l in range(DEPTH) for nm in _W_NAMES if (l, nm) != (0, "w_in")]
    h_pass = _sibling_start(_exchange_wait(h_in, before_wait, h_in['tag']), win, pass_name)
    h_pass['span'] = h_in['span']
    started = [h_pass['token']]
    layer_groups = []
    for l, groups in enumerate([[(rest, "0_rest")]] + [[(_W_NAMES, "%d_all" % l)] for l in range(1, DEPTH)]):
        layer_groups.append([])
        for names, tag in groups:
            layer_groups[-1].append((names, _gather_start(sh[l], names, tag, started[-1:])))
            started.append(layer_groups[-1][-1][1]['token'])
    (w_in0,) = _exchange_wait(h_pass, started, passed_name)
    layer_w = [_LayerWeights(g, started) for g in layer_groups]
    layer_w[0].got["w_in"] = w_in0
    saved = []
    for l in range(DEPTH):
        xs, rs, hs, sv = _layer_fwd(xs, rs, hs, mod_full[l], norm_mix_g[l:l + 1], norm_mlp_g[l:l + 1],
                                    pre[l + 1] if l + 1 < DEPTH else None, layer_w[l], rc, tri)
        sv['W'] = layer_w[l]
        saved.append(sv)

    fg = final_g.reshape(1, D)

    def head(xt, tg, dn, g, g2m, r):
        xh = xt * r
        e = xh * g - tg
        dy = e * (1.0 / D)
        dxh = dy * g
        dx = r * (dxh - xh * jnp.mean(dxh * xh, axis=-1, keepdims=True))
        return (dx, dy * xh, 0.5 * e * e * (1.0 / D)) + _gate_bwd(dx, dn, g2m)

    top = saved[DEPTH - 1]
    dxs, d_fg, loss_cols, *gate = _ew(head, [xs, tgt, top['dn']], vecs=[fg, top['mod'][N_MOD - 1:N_MOD]], cols=[rs],
                                      outs=[('row', D, F32), ('sum', D), ('sum', D)] + _GATE_OUTS, name="loss_head")

    small = [None] * DEPTH
    pending = []
    for l in reversed(range(DEPTH)):
        sv = saved[l]

        def emit(gw, tag, l=l):
            names = tuple(gw)
            pending.append((l, names, _scatter_start(gw, names, "%d_%s" % (l, tag))))
            return pending[-1][2]['token']

        dxs, gate, d_mod, d_gn1, d_gn2 = _layer_bwd(dxs, gate[0], gate[1], sv, saved[l - 1] if l else None,
                                                    sv['W'], rc, tri, emit)
        small[l] = (d_mod, d_gn1, d_gn2)
    grad_x = dxs.reshape(1, S, D)

    res = {}
    after = [dxs]
    for l, names, h in pending:
        for nm, landed in zip(names, _finish(h, after)):
            res[nm] = _adam_layer(landed, wts[nm], mts[nm], vts[nm], l, res.get(nm), "adam_layer")
        after = [res[nm][0] for nm in names]

    pack = jnp.concatenate([small[l][0] for l in range(DEPTH)] + [small[l][1] for l in range(DEPTH)]
                           + [small[l][2] for l in range(DEPTH)] + [d_fg, loss_cols], axis=1)
    (packs,) = _all_gather_lead([pack], "gather_small", deps=after)
    packs = packs.reshape(N_DEV, -1)
    o = 0
    dmod_all = []
    for l in range(DEPTH):
        dmod_all.append(packs[:, o:o + N_MOD * D]); o += N_MOD * D
    gn1_parts = packs[:, o:o + DEPTH * D].reshape(N_DEV, DEPTH, D); o += DEPTH * D
    gn2_parts = packs[:, o:o + DEPTH * D].reshape(N_DEV, DEPTH, D); o += DEPTH * D
    fg_parts = packs[:, o:o + D].reshape(N_DEV, 1, D); o += D
    loss_parts = packs[:, o:o + D]
    (loss_sum,) = _ew(lambda t: (t,), [loss_parts], outs=[('sum', D)], name="loss_sum")
    loss = jnp.sum(loss_sum)

    res["norm_mix_g"] = _adam(gn1_parts, norm_mix_g, m_norm_mix_g, v_norm_mix_g, "adam")
    res["norm_mlp_g"] = _adam(gn2_parts, norm_mlp_g, m_norm_mlp_g, v_norm_mlp_g, "adam")
    fgr = _adam(fg_parts, fg, m_final_g.reshape(1, D), v_final_g.reshape(1, D), "adam")
    res["final_g"] = [t.reshape(D) for t in fgr]
    bparts = jnp.stack(dmod_all, axis=1)
    res["b_ada"] = _adam(bparts, b_ada, m_b_ada, v_b_ada, "adam")
    cact_t = cact_all.T
    for l in range(DEPTH):
        dm_loc = lax.dynamic_slice_in_dim(dmod_all[l], me * n_ada, n_ada, axis=1)
        res["w_ada"] = _adam_layer(_ada_grad(cact_t, dm_loc), w_ada, m_w_ada, v_w_ada, l, res.get("w_ada"),
                                   "adam_layer")

    order = ['norm_mix_g', 'w_in', 'w_ret_out', 'w_sb_out', 'w_mix_out', 'norm_mlp_g', 'w_up', 'w_down', 'w_ada', 'b_ada', 'final_g']
    out = [loss, grad_x]
    for i in range(4):
        out += [res[nm][i] for nm in order]
    return tuple(out)
```

```python
import functools
import math

import jax
import jax.numpy as jnp
import numpy as np
from jax import lax
from jax.experimental import pallas as pl
from jax.experimental.pallas import tpu as pltpu

F32 = jnp.float32
BF16 = jnp.bfloat16

N_DEV = 8
D_MODEL = 1024
DEPTH = 2
RET_HEADS = 4
RET_QK = 256
RET_V = 512
RET_CHUNK = 256
ROPE_BASE = 10000.0
SB_HEADS = 16
SB_DIM = 64
D_FF = 4096
N_MOD = 6
EPS = 1e-6
GN_EPS = 1e-5
O_RQ, O_RK, O_RV, O_RG, O_SQ, O_SK, O_SV, O_GA, O_GB = 0, 1024, 2048, 4096, 6144, 7168, 8192, 9216, 10240
IN_W = 11264

ADAM_LR, ADAM_B1, ADAM_B2, ADAM_EPS, ADAM_WD, ADAM_STEP = 0.001, 0.9, 0.999, 1e-08, 0.01, 10

VMEM_LIMIT = 56 * 1024 * 1024
DW_TK = 2048


def _cparams(sem):
    return pltpu.CompilerParams(dimension_semantics=sem, vmem_limit_bytes=VMEM_LIMIT)


def _mm(a, b, *, ta=False, tb=False, tm=1024, tn=1024, tk=None, a_ex=(), pro=None, o_ex=(), epi=None,
        outs=(F32,), cols=0, name, deps=()):
    a_parts = list(a) if isinstance(a, (list, tuple)) else [a]
    b_parts = list(b) if isinstance(b, (list, tuple)) else [b]
    assert not (ta and len(a_parts) > 1) and not (tb and len(b_parts) > 1)
    if ta:
        K, M = a.shape
    else:
        M, K = a_parts[0].shape[0], sum(t.shape[1] for t in a_parts)
    N = b.shape[0] if tb else sum(t.shape[1] for t in b_parts)
    tm, tn, tk = min(tm, M), min(tn, N), K if tk is None else min(tk, K)
    assert M % tm == 0 and N % tn == 0 and K % tk == 0, (name, M, N, K, tm, tn, tk)
    nk = K // tk
    multi = len(a_parts) > 1 or len(b_parts) > 1
    direct = epi is None and tuple(outs) == (F32,) and cols == 0

    def ranges(parts, t):
        out, o = [], 0
        for arr in parts:
            assert arr.shape[1] % t == 0
            out.append((o, o + arr.shape[1] // t))
            o += arr.shape[1] // t
        return out

    a_rng = ranges(a_parts, tk) if len(a_parts) > 1 else [(0, nk)]
    b_rng = ranges(b_parts, tn) if len(b_parts) > 1 else [(0, N // tn)]
    clip = lambda v, lo, hi: jnp.clip(v - lo, 0, hi - lo - 1)
    mine = lambda v, lo, hi, w: jnp.where((v >= lo) & (v < hi), w, 0)
    in_specs, args = [], []
    for arr, (lo, hi) in zip(a_parts, a_rng):
        in_specs.append(pl.BlockSpec((tk, tm), lambda i, j, k: (k, i)) if ta
                        else pl.BlockSpec((tm, tk), lambda i, j, k, lo=lo, hi=hi: (i, clip(k, lo, hi))))
        args.append(arr)
    for arr, (lo, hi) in zip(b_parts, b_rng):
        in_specs.append(pl.BlockSpec((tn, tk), lambda i, j, k: (j, k)) if tb
                        else pl.BlockSpec((tk, tn), lambda i, j, k, lo=lo, hi=hi: (mine(j, lo, hi, k), clip(j, lo, hi))))
        args.append(arr)
    npa, npb = len(a_parts), len(b_parts)
    for arr, kind, *off in a_ex:
        off = off[0] if off else 0
        if kind == 'a' and ta:
            assert off % tm == 0
            in_specs.append(pl.BlockSpec((tk, tm), lambda i, j, k, o=off // tm: (k, o + i)))
        elif kind == 'a':
            assert off % tk == 0
            in_specs.append(pl.BlockSpec((tm, tk), lambda i, j, k, o=off // tk: (i, o + k)))
        elif kind == 'k':
            in_specs.append(pl.BlockSpec((tk, 1), lambda i, j, k: (k, 0)) if ta
                            else pl.BlockSpec((1, tk), lambda i, j, k: (0, k)))
        else:
            in_specs.append(pl.BlockSpec((1, tm), lambda i, j, k: (0, i)) if ta
                            else pl.BlockSpec((tm, 1), lambda i, j, k: (i, 0)))
        args.append(arr)
    for arr, kind, *off in o_ex:
        off = off[0] if off else 0
        if kind == 'o':
            assert off % tn == 0
            in_specs.append(pl.BlockSpec((tm, tn), lambda i, j, k, o=off // tn: (i, o + j)))
        elif kind == 'n':
            in_specs.append(pl.BlockSpec((1, tn), lambda i, j, k: (0, j)))
        else:
            in_specs.append(pl.BlockSpec((tm, 1), lambda i, j, k: (i, 0)))
        args.append(arr)
    for arr in deps:
        in_specs.append(pl.BlockSpec(memory_space=pl.ANY))
        args.append(arr)
    assert cols == 0 or N == tn
    na, no, nout, nd = len(a_ex), len(o_ex), len(outs) + cols, len(deps)
    dims = (((0 if ta else 1,), (1 if tb else 0,)), ((), ()))

    def body(*refs):
        a_refs, b_refs = refs[:npa], refs[npa:npa + npb]
        n0 = npa + npb
        aex = refs[n0:n0 + na]
        oex = refs[n0 + na:n0 + na + no]
        out_refs = refs[n0 + na + no + nd:n0 + na + no + nd + nout]

        def product(a_ref, b_ref):
            at = a_ref[...]
            if pro is not None:
                at = pro(at, *[r[...] for r in aex])
            return lax.dot_general(at.astype(BF16), b_ref[...].astype(BF16), dims, preferred_element_type=F32)

        def finish(res):
            vals = epi(res, *[r[...] for r in oex]) if epi is not None else (res,)
            for o_ref, v in zip(out_refs, vals):
                o_ref[...] = v.astype(o_ref.dtype)

        if nk == 1 and not multi:
            finish(product(a_refs[0], b_refs[0]))
            return
        acc = out_refs[0] if direct else refs[-1]
        j, k = pl.program_id(1), pl.program_id(2)
        if multi:
            @pl.when(k == 0)
            def _():
                acc[...] = jnp.zeros_like(acc)

            for a_ref, (alo, ahi) in zip(a_refs, a_rng):
                for b_ref, (blo, bhi) in zip(b_refs, b_rng):
                    @pl.when((k >= alo) & (k < ahi) & (j >= blo) & (j < bhi))
                    def _():
                        acc[...] += product(a_ref, b_ref)
        else:
            @pl.when(k == 0)
            def _():
                acc[...] = product(a_refs[0], b_refs[0])

            @pl.when(k > 0)
            def _():
                acc[...] += product(a_refs[0], b_refs[0])

        if not direct:
            @pl.when(k == nk - 1)
            def _():
                finish(acc[...])

    res = pl.pallas_call(
        body, name=name, grid=(M // tm, N // tn, nk), in_specs=in_specs,
        out_specs=[pl.BlockSpec((tm, tn), lambda i, j, k: (i, j)) for _ in outs]
        + [pl.BlockSpec((tm, 1), lambda i, j, k: (i, 0))] * cols,
        out_shape=[jax.ShapeDtypeStruct((M, N), dt) for dt in outs] + [jax.ShapeDtypeStruct((M, 1), F32)] * cols,
        scratch_shapes=[pltpu.VMEM((tm, tn), F32)] if (nk > 1 or multi) and not direct else [],
        compiler_params=_cparams(("parallel", "parallel", "arbitrary")),
    )(*args)
    return res


def _mm_parts_ring(parts, w, *, tm=2048, tk=1024, name, deps=()):
    S = parts[0].shape[0]
    N, K = w.shape
    rng, o = [], 0
    for a in parts:
        assert a.dtype == BF16 and a.shape[0] == S and a.shape[1] % tk == 0
        rng.append((o, o + a.shape[1] // tk))
        o += a.shape[1] // tk
    tm = min(tm, S)
    nk, nm, npart, nd = o, S // tm, len(parts), len(deps)
    T = nm * nk
    assert nk * tk == K and S % tm == 0 and T >= 2

    def body(*refs):
        a_refs, b_ref, o_ref = refs[:npart], refs[npart], refs[npart + 1 + nd]
        abuf, sems = refs[-2], refs[-1]
        k = pl.program_id(1)
        t = pl.program_id(0) * nk + k

        def fetch(tt):
            ii, kk, slot = tt // nk, tt % nk, tt % 3
            for a_ref, (lo, hi) in zip(a_refs, rng):
                @pl.when((kk >= lo) & (kk < hi))
                def _():
                    pltpu.make_async_copy(
                        a_ref.at[pl.ds(pl.multiple_of(ii * tm, tm), tm), pl.ds(pl.multiple_of((kk - lo) * tk, tk), tk)],
                        abuf.at[slot], sems.at[slot]).start()

        @pl.when(t == 0)
        def _():
            fetch(t)
            fetch(t + 1)

        @pl.when(t + 2 < T)
        def _():
            fetch(t + 2)

        slot = t % 3
        pltpu.make_async_copy(a_refs[0].at[pl.ds(0, tm), pl.ds(0, tk)], abuf.at[slot], sems.at[slot]).wait()

        def product():
            return lax.dot_general(abuf[slot], b_ref[...], (((1,), (1,)), ((), ())), preferred_element_type=F32)

        @pl.when(k == 0)
        def _():
            o_ref[...] = product()

        @pl.when(k > 0)
        def _():
            o_ref[...] += product()

    (res,) = pl.pallas_call(
        body, name=name, grid=(nm, nk),
        in_specs=[pl.BlockSpec(memory_space=pl.ANY)] * npart + [pl.BlockSpec((N, tk), lambda i, k: (0, k))]
        + [pl.BlockSpec(memory_space=pl.ANY)] * nd,
        out_specs=[pl.BlockSpec((tm, N), lambda i, k: (i, 0))],
        out_shape=[jax.ShapeDtypeStruct((S, N), F32)],
        scratch_shapes=[pltpu.VMEM((3, tm, tk), BF16), pltpu.SemaphoreType.DMA((3,))],
        compiler_params=_cparams(("arbitrary", "arbitrary")),
    )(*parts, w, *deps)
    return res


def _mm_t_parts_ring(a, parts, *, tn=1024, tk=1024, name):
    S, M = a.shape
    rng, o = [], 0
    for b in parts:
        assert b.dtype == BF16 and b.shape[0] == S and b.shape[1] % tn == 0
        rng.append((o, o + b.shape[1] // tn))
        o += b.shape[1] // tn
    tk = min(tk, S)
    nj, nk, npart = o, S // tk, len(parts)
    T = nj * nk
    assert S % tk == 0 and T >= 2

    def body(*refs):
        a_ref, b_refs, o_ref = refs[0], refs[1:1 + npart], refs[1 + npart]
        acc, bbuf, sems = refs[-3], refs[-2], refs[-1]
        k = pl.program_id(1)
        t = pl.program_id(0) * nk + k

        def fetch(tt):
            jj, kk, slot = tt // nk, tt % nk, tt % 3
            for b_ref, (lo, hi) in zip(b_refs, rng):
                @pl.when((jj >= lo) & (jj < hi))
                def _():
                    pltpu.make_async_copy(
                        b_ref.at[pl.ds(pl.multiple_of(kk * tk, tk), tk), pl.ds(pl.multiple_of((jj - lo) * tn, tn), tn)],
                        bbuf.at[slot], sems.at[slot]).start()

        @pl.when(t == 0)
        def _():
            fetch(t)
            fetch(t + 1)

        @pl.when(t + 2 < T)
        def _():
            fetch(t + 2)

        slot = t % 3
        pltpu.make_async_copy(b_refs[0].at[pl.ds(0, tk), pl.ds(0, tn)], bbuf.at[slot], sems.at[slot]).wait()

        def product():
            return lax.dot_general(a_ref[...], bbuf[slot], (((0,), (0,)), ((), ())), preferred_element_type=F32)

        @pl.when(k == 0)
        def _():
            acc[...] = product()

        @pl.when(k > 0)
        def _():
            acc[...] += product()

        @pl.when(k == nk - 1)
        def _():
            o_ref[...] = acc[...].astype(o_ref.dtype)

    (res,) = pl.pallas_call(
        body, name=name, grid=(nj, nk),
        in_specs=[pl.BlockSpec((tk, M), lambda j, k: (k, 0))] + [pl.BlockSpec(memory_space=pl.ANY)] * npart,
        out_specs=[pl.BlockSpec((M, tn), lambda j, k: (0, j))],
        out_shape=[jax.ShapeDtypeStruct((M, nj * tn), BF16)],
        scratch_shapes=[pltpu.VMEM((M, tn), F32), pltpu.VMEM((3, tk, tn), BF16), pltpu.SemaphoreType.DMA((3,))],
        compiler_params=_cparams(("arbitrary", "arbitrary")),
    )(a, *parts)
    return res


def _ew(fn, rows, vecs=(), cols=(), outs=(), tr=256, name=None, deps=()):
    S = rows[0].shape[0]
    tr = min(tr, S)
    assert S % tr == 0
    in_specs, args = [], []
    for r in rows:
        in_specs.append(pl.BlockSpec((tr, r.shape[1]), lambda i: (i, 0)))
        args.append(r)
    for v in vecs:
        in_specs.append(pl.BlockSpec((1, v.shape[1]), lambda i: (0, 0)))
        args.append(v)
    for c in cols:
        in_specs.append(pl.BlockSpec((tr, 1), lambda i: (i, 0)))
        args.append(c)
    out_specs, out_shape = [], []
    for o in outs:
        if o[0] == 'row':
            out_specs.append(pl.BlockSpec((tr, o[1]), lambda i: (i, 0)))
            out_shape.append(jax.ShapeDtypeStruct((S, o[1]), o[2]))
        elif o[0] == 'sum':
            out_specs.append(pl.BlockSpec((1, o[1]), lambda i: (0, 0)))
            out_shape.append(jax.ShapeDtypeStruct((1, o[1]), F32))
        else:
            out_specs.append(pl.BlockSpec((tr, 1), lambda i: (i, 0)))
            out_shape.append(jax.ShapeDtypeStruct((S, 1), o[1]))
    nin = len(args)
    in_specs += [pl.BlockSpec(memory_space=pl.ANY)] * len(deps)
    args += list(deps)

    def body(*refs):
        i = pl.program_id(0)
        vals = fn(*[r[...] for r in refs[:nin]])
        for o, o_ref, v in zip(outs, refs[nin + len(deps):], vals):
            if o[0] == 'sum':
                @pl.when(i == 0)
                def _():
                    o_ref[...] = jnp.zeros_like(o_ref)
                o_ref[...] += jnp.sum(v.astype(F32), axis=0, keepdims=True)
            else:
                o_ref[...] = v.astype(o_ref.dtype)

    return pl.pallas_call(
        body, name=name, grid=(S // tr,), in_specs=in_specs, out_specs=out_specs, out_shape=out_shape,
        compiler_params=_cparams(("arbitrary",)),
    )(*args)


def _sigmoid(x):
    return 1.0 / (1.0 + jnp.exp(-x))


def _ret_consts(S):
    h = np.arange(RET_HEADS, dtype=np.float64)
    log_gamma = np.log1p(-np.power(2.0, -5.0 - h))
    idx = np.arange(RET_CHUNK, dtype=np.float64)
    rel = idx[:, None] - idx[None, :]
    decay = np.where(rel >= 0, np.exp(np.maximum(rel, 0.0) * log_gamma[:, None, None]), 0.0)
    xi = np.exp((idx + 1.0) * log_gamma[:, None])[:, :, None]
    zeta = np.exp((RET_CHUNK - 1.0 - idx) * log_gamma[:, None])[:, :, None]
    gamma_c = np.exp(RET_CHUNK * log_gamma)[:, None, None]
    half = RET_QK // 2
    inv_freq = np.power(ROPE_BASE, -np.arange(half, dtype=np.float64) / half).astype(np.float32)
    ang = np.arange(S, dtype=np.float32)[:, None] * inv_freq[None, :]
    f = lambda t: jnp.asarray(t, F32)
    return dict(decay=f(decay), xi=f(xi), zeta=f(zeta), gc=f(gamma_c), cos=f(np.cos(ang)), sin=f(np.sin(ang)))


def _rot(t, cos, sin):
    half = RET_QK // 2
    t1, t2 = t[:, :half], t[:, half:]
    return jnp.concatenate([t1 * cos - t2 * sin, t1 * sin + t2 * cos], axis=-1)


def _rot_inv(t, cos, sin):
    half = RET_QK // 2
    t1, t2 = t[:, :half], t[:, half:]
    return jnp.concatenate([t1 * cos + t2 * sin, t2 * cos - t1 * sin], axis=-1)


_NT = (((1,), (1,)), ((), ()))
_TN = (((0,), (0,)), ((), ()))


def _dot(a, b):
    return jnp.dot(a, b, preferred_element_type=F32)


def _dot_nt(a, b):
    return lax.dot_general(a, b, _NT, preferred_element_type=F32)


def _dot_tn(a, b):
    return lax.dot_general(a, b, _TN, preferred_element_type=F32)


_QW, _VW = RET_HEADS * RET_QK, RET_HEADS * RET_V
_HEADS = range(RET_HEADS)


def _ret_in_specs(C, rev, NC):
    n_of = (lambda n: NC - 1 - n) if rev else (lambda n: n)
    whole3 = lambda n: (0, 0, 0)
    return [
        pl.BlockSpec((C, _QW), lambda n: (n_of(n), O_RQ // _QW)),
        pl.BlockSpec((C, _QW), lambda n: (n_of(n), O_RK // _QW)),
        pl.BlockSpec((C, _VW), lambda n: (n_of(n), O_RV // _VW)),
        pl.BlockSpec((C, RET_QK // 2), lambda n: (n_of(n), 0)),
        pl.BlockSpec((C, RET_QK // 2), lambda n: (n_of(n), 0)),
        pl.BlockSpec((RET_HEADS, C, C), whole3),
        pl.BlockSpec((RET_HEADS, C, 1), whole3),
        pl.BlockSpec((RET_HEADS, C, 1), whole3),
        pl.BlockSpec((RET_HEADS, 1, 1), whole3),
    ]


def _qk_heads(q_ref, k_ref, cos, sin):
    qs, kfs = [], []
    for h in _HEADS:
        cols = slice(h * RET_QK, (h + 1) * RET_QK)
        qs.append(_rot(q_ref[:, cols].astype(F32), cos, sin).astype(BF16))
        kfs.append(_rot(k_ref[:, cols].astype(F32), cos, sin) * (RET_QK ** -0.5))
    return qs, kfs


def _ret_fwd(p, rc):
    S = p.shape[0]
    C = RET_CHUNK
    NC = S // C

    def body(q_ref, k_ref, v_ref, cos_ref, sin_ref, dec_ref, xi_ref, zeta_ref, gc_ref, y_ref, rs_ref, r_acc):
        n = pl.program_id(0)

        @pl.when(n == 0)
        def _():
            r_acc[...] = jnp.zeros_like(r_acc)

        cos, sin = cos_ref[...], sin_ref[...]
        qs, kfs = _qk_heads(q_ref, k_ref, cos, sin)
        vs = [v_ref[:, h * RET_V:(h + 1) * RET_V] for h in _HEADS]
        rbs = [r_acc[h].astype(BF16) for h in _HEADS]
        for h in _HEADS:
            rs_ref[h, 0] = rbs[h]
        ss = [(_dot_nt(qs[h], kfs[h].astype(BF16)) * dec_ref[h]).astype(BF16) for h in _HEADS]
        os = [_dot(ss[h], vs[h]) + _dot(qs[h], rbs[h]) * xi_ref[h] for h in _HEADS]
        for h in _HEADS:
            o = os[h]
            mu = jnp.mean(o, axis=-1, keepdims=True)
            var = jnp.mean(jnp.square(o - mu), axis=-1, keepdims=True)
            y_ref[:, h * RET_V:(h + 1) * RET_V] = ((o - mu) * lax.rsqrt(var + GN_EPS)).astype(y_ref.dtype)
        for h in _HEADS:
            kz = (kfs[h] * zeta_ref[h]).astype(BF16)
            r_acc[h] = r_acc[h] * gc_ref[h] + _dot_tn(kz, vs[h])

    return pl.pallas_call(
        body, name="ret_fwd", grid=(NC,), in_specs=_ret_in_specs(C, False, NC),
        out_specs=[pl.BlockSpec((C, _VW), lambda n: (n, 0)),
                   pl.BlockSpec((RET_HEADS, 1, RET_QK, RET_V), lambda n: (0, n, 0, 0))],
        out_shape=[jax.ShapeDtypeStruct((S, _VW), BF16),
                   jax.ShapeDtypeStruct((RET_HEADS, NC, RET_QK, RET_V), BF16)],
        scratch_shapes=[pltpu.VMEM((RET_HEADS, RET_QK, RET_V), F32)],
        compiler_params=_cparams(("arbitrary",)),
    )(p, p, p, rc['cos'], rc['sin'], rc['decay'], rc['xi'], rc['zeta'], rc['gc'])


def _ret_bwd(p, rstate, dy, rc):
    S = p.shape[0]
    C = RET_CHUNK
    NC = S // C

    def body(q_ref, k_ref, v_ref, cos_ref, sin_ref, dec_ref, xi_ref, zeta_ref, gc_ref, rs_ref, dy_ref,
             d_ref, dr_acc):
        dq_ref, dk_ref, dv_ref = d_ref.at[:, 0:_QW], d_ref.at[:, _QW:2 * _QW], d_ref.at[:, 2 * _QW:2 * _QW + _VW]
        t = pl.program_id(0)

        @pl.when(t == 0)
        def _():
            dr_acc[...] = jnp.zeros_like(dr_acc)

        cos, sin = cos_ref[...], sin_ref[...]
        qs, kfs = _qk_heads(q_ref, k_ref, cos, sin)
        ks = [kf.astype(BF16) for kf in kfs]
        vs = [v_ref[:, h * RET_V:(h + 1) * RET_V] for h in _HEADS]
        rbs = [rs_ref[h, 0] for h in _HEADS]
        ss = [(_dot_nt(qs[h], ks[h]) * dec_ref[h]).astype(BF16) for h in _HEADS]
        os = [_dot(ss[h], vs[h]) + _dot(qs[h], rbs[h]) * xi_ref[h] for h in _HEADS]
        dobs, doxis = [], []
        for h in _HEADS:
            o = os[h]
            mu = jnp.mean(o, axis=-1, keepdims=True)
            var = jnp.mean(jnp.square(o - mu), axis=-1, keepdims=True)
            rstd = lax.rsqrt(var + GN_EPS)
            yh = (o - mu) * rstd
            dyf = dy_ref[:, h * RET_V:(h + 1) * RET_V].astype(F32)
            do = (dyf - jnp.mean(dyf, axis=-1, keepdims=True)
                  - yh * jnp.mean(dyf * yh, axis=-1, keepdims=True)) * rstd
            dobs.append(do.astype(BF16))
            doxis.append((do * xi_ref[h]).astype(BF16))
        drbs = [dr_acc[h].astype(BF16) for h in _HEADS]
        dss = [(_dot_nt(dobs[h], vs[h]) * dec_ref[h]).astype(BF16) for h in _HEADS]
        for h in _HEADS:
            dq = _dot(dss[h], ks[h]) + _dot_nt(doxis[h], rbs[h])
            dq_ref[:, h * RET_QK:(h + 1) * RET_QK] = _rot_inv(dq, cos, sin).astype(dq_ref.dtype)
        for h in _HEADS:
            dk = _dot_tn(dss[h], qs[h]) + _dot_nt(vs[h], drbs[h]) * zeta_ref[h]
            dk_ref[:, h * RET_QK:(h + 1) * RET_QK] = (_rot_inv(dk, cos, sin) * (RET_QK ** -0.5)).astype(dk_ref.dtype)
        for h in _HEADS:
            kz = (kfs[h] * zeta_ref[h]).astype(BF16)
            dv = _dot_tn(ss[h], dobs[h]) + _dot(kz, drbs[h])
            dv_ref[:, h * RET_V:(h + 1) * RET_V] = dv.astype(dv_ref.dtype)
        for h in _HEADS:
            dr_acc[h] = dr_acc[h] * gc_ref[h] + _dot_tn(qs[h], doxis[h])

    rn = lambda n: NC - 1 - n
    in_specs = _ret_in_specs(C, True, NC) + [
        pl.BlockSpec((RET_HEADS, 1, RET_QK, RET_V), lambda n: (0, rn(n), 0, 0)),
        pl.BlockSpec((C, _VW), lambda n: (rn(n), 0)),
    ]
    return pl.pallas_call(
        body, name="ret_bwd", grid=(NC,), in_specs=in_specs,
        out_specs=pl.BlockSpec((C, 2 * _QW + _VW), lambda n: (rn(n), 0)),
        out_shape=jax.ShapeDtypeStruct((S, 2 * _QW + _VW), BF16),
        scratch_shapes=[pltpu.VMEM((RET_HEADS, RET_QK, RET_V), F32)],
        compiler_params=_cparams(("arbitrary",)),
    )(p, p, p, rc['cos'], rc['sin'], rc['decay'], rc['xi'], rc['zeta'], rc['gc'], rstate, dy)


SB_T = 256
SB_SCALE = SB_DIM ** -0.5


def _tri():
    j = np.arange(SB_T)
    after = (j[:, None] > j[None, :]).astype(np.float32)
    upto = (j[:, None] <= j[None, :]).astype(np.float32)
    return jnp.asarray(np.stack([after, upto]), BF16)


def _softplus_parts(z):
    neg_abs = lax.bitcast_convert_type(lax.bitcast_convert_type(z, jnp.uint32) | jnp.uint32(0x80000000), F32)
    e = jnp.exp(neg_abs)
    return jnp.maximum(z, 0.0) + jnp.log(1.0 + e), e


def _sb_fwd(p, tri):
    S = p.shape[0]
    T = min(SB_T, S)
    NQ = S // T
    assert NQ <= 128
    qb, kb, vb = O_SQ // 128, O_SK // 128, O_SV // 128

    def body(q_ref, k_ref, v_ref, tri_ref, o_ref, cs_ref, o_acc, run, zbuf, abuf):
        i = pl.program_id(1)
        lane = lax.broadcasted_iota(jnp.int32, (1, 128), 1)
        tri_after = tri_ref[0]
        qs = [jnp.where((lane >= 64) if hh else (lane < 64), q_ref[...], jnp.zeros_like(q_ref[...]))
              * jnp.asarray(SB_SCALE, BF16) for hh in range(2)]
        cs_ref[...] = jnp.zeros_like(cs_ref)
        o_acc[...] = jnp.zeros_like(o_acc)
        run[...] = jnp.zeros_like(run)

        def kv(ref, j):
            return ref[pl.ds(pl.multiple_of(j * T, T), T), :]

        for hh in range(2):
            zbuf[hh] = _dot_nt(qs[hh], kv(k_ref, i))

        def block(t, diagonal):
            j = i - t
            if diagonal:
                msk = lax.broadcasted_iota(jnp.int32, (T, T), 1) < lax.broadcasted_iota(jnp.int32, (T, T), 0)
            if not diagonal:
                av = [_dot(abuf[hh], kv(v_ref, j + 1)) for hh in range(2)]
            lss, exs, tot, zn = [], [], [], []
            for hh in range(2):
                z = zbuf[hh]
                sp, _ = _softplus_parts(z)
                lss.append(z - sp)
                if diagonal:
                    sp = jnp.where(msk, sp, 0.0)
                exs.append(_dot(sp.astype(BF16), tri_after))
                tot.append(sp[:, 0:1])
                zn.append(_dot_nt(qs[hh], kv(k_ref, jnp.maximum(j - 1, 0))))
            for hh in range(2):
                csl = slice(hh * 128, (hh + 1) * 128)
                cs = run[hh]
                a = jnp.exp(lss[hh] - exs[hh] - cs)
                if diagonal:
                    a = jnp.where(msk, a, 0.0)
                abuf[hh] = a.astype(BF16)
                cs_ref[:, csl] = jnp.where(lane == j, cs, cs_ref[:, csl])
                run[hh] = cs + exs[hh][:, 0:1] + tot[hh]
            for hh in range(2):
                if not diagonal:
                    o_acc[hh] += av[hh]
                zbuf[hh] = zn[hh]

        block(0, True)

        def step(t, carry):
            block(t, False)
            return carry

        lax.fori_loop(1, i + 1, step, 0)
        o_ref[...] = jnp.where(lane < 64, o_acc[0] + _dot(abuf[0], kv(v_ref, 0)),
                               o_acc[1] + _dot(abuf[1], kv(v_ref, 0))).astype(o_ref.dtype)

    return pl.pallas_call(
        body, name="sb_fwd", grid=(SB_HEADS // 2, NQ),
        scratch_shapes=[pltpu.VMEM((2, T, 128), F32), pltpu.VMEM((2, T, 1), F32), pltpu.VMEM((2, T, T), F32),
                        pltpu.VMEM((2, T, T), BF16)],
        in_specs=[pl.BlockSpec((T, 128), lambda h, i: (i, qb + h)),
                  pl.BlockSpec((S, 128), lambda h, i: (0, kb + h)),
                  pl.BlockSpec((S, 128), lambda h, i: (0, vb + h)),
                  pl.BlockSpec((1, T, T), lambda h, i: (0, 0, 0))],
        out_specs=[pl.BlockSpec((T, 128), lambda h, i: (i, h)),
                   pl.BlockSpec((T, 256), lambda h, i: (i, h))],
        out_shape=[jax.ShapeDtypeStruct((S, SB_HEADS * SB_DIM), BF16),
                   jax.ShapeDtypeStruct((S, SB_HEADS * 128), F32)],
        compiler_params=_cparams(("parallel", "arbitrary")),
    )(p, p, p, tri)


def _sb_bwd(p, carries, dy, tri):
    S = p.shape[0]
    T = min(SB_T, S)
    NQ = S // T
    qb, kb, vb = O_SQ // 128, O_SK // 128, O_SV // 128

    def body(q_ref, k_ref, v_ref, cs_ref, dy_ref, tri_ref, dq_ref, dk_ref, dv_ref, dk_acc, dv_acc, dq_acc, run,
             zbuf, dabuf, dzbuf, abuf):
        i = pl.program_id(1)

        @pl.when(i == 0)
        def _():
            dk_acc[...] = jnp.zeros_like(dk_acc)
            dv_acc[...] = jnp.zeros_like(dv_acc)

        lane = lax.broadcasted_iota(jnp.int32, (1, 128), 1)
        tri_after, tri_upto = tri_ref[0], tri_ref[1]
        hms = [(lane >= 64) if hh else (lane < 64) for hh in range(2)]
        qs = [jnp.where(hm, q_ref[...], jnp.zeros_like(q_ref[...])) * jnp.asarray(SB_SCALE, BF16) for hm in hms]
        dos = [jnp.where(hm, dy_ref[...], jnp.zeros_like(dy_ref[...])) for hm in hms]
        qst = [t.T for t in qs]
        dost = [t.T for t in dos]
        dq_acc[...] = jnp.zeros_like(dq_acc)
        run[...] = jnp.zeros_like(run)
        dzbuf[...] = jnp.zeros_like(dzbuf)
        abuf[...] = jnp.zeros_like(abuf)

        def kv(ref, j):
            return ref[pl.ds(pl.multiple_of(j * T, T), T), :]

        def flush(jp):
            kp = kv(k_ref, jp)
            dq_add = [_dot(dzbuf[hh], kp) for hh in range(2)]
            dk_add = _dot(qst[0], dzbuf[0]) + _dot(qst[1], dzbuf[1])
            dv_add = _dot(dost[0], abuf[0]) + _dot(dost[1], abuf[1])
            return dq_add, dk_add, dv_add

        def apply(jp, adds):
            dq_add, dk_add, dv_add = adds
            cols = pl.ds(pl.multiple_of(jp * T, T), T)
            for hh in range(2):
                dq_acc[hh] += dq_add[hh]
            dk_acc[:, cols] += dk_add
            dv_acc[:, cols] += dv_add

        for hh in range(2):
            zbuf[hh] = _dot_nt(qs[hh], kv(k_ref, 0))
            dabuf[hh] = _dot_nt(dos[hh], kv(v_ref, 0))

        def block(j, diagonal):
            jp = jnp.maximum(j - 1, 0)
            if diagonal:
                msk = lax.broadcasted_iota(jnp.int32, (T, T), 1) < lax.broadcasted_iota(jnp.int32, (T, T), 0)
            kp = kv(k_ref, jp)
            dq_add = [_dot(dzbuf[hh], kp) for hh in range(2)]
            sigs, lss, exs, zn, dan, dk_part, dv_part = [], [], [], [], [], [], []
            for hh in range(2):
                z = zbuf[hh]
                sp, _ = _softplus_parts(z)
                lss.append(z - sp)
                sigs.append(jnp.exp(lss[hh]))
                if diagonal:
                    sp = jnp.where(msk, sp, 0.0)
                exs.append(_dot(sp.astype(BF16), tri_after))
                if not diagonal:
                    zn.append(_dot_nt(qs[hh], kv(k_ref, j + 1)))
                dk_part.append(_dot(qst[hh], dzbuf[hh]))
            pgs, gs = [], []
            for hh in range(2):
                csl = slice(hh * 128, (hh + 1) * 128)
                cs = jnp.sum(jnp.where(lane == j, cs_ref[:, csl], 0.0), axis=-1, keepdims=True)
                a = jnp.exp(lss[hh] - exs[hh] - cs)
                if diagonal:
                    a = jnp.where(msk, a, 0.0)
                abuf_new = a.astype(BF16)
                g = a * dabuf[hh]
                gs.append((g, abuf_new))
                pgs.append(_dot(g.astype(BF16), tri_upto))
                if not diagonal:
                    dan.append(_dot_nt(dos[hh], kv(v_ref, j + 1)))
                dv_part.append(_dot(dost[hh], abuf[hh]))
            adds = (dq_add, dk_part[0] + dk_part[1], dv_part[0] + dv_part[1])
            for hh in range(2):
                g, abuf_new = gs[hh]
                cg = run[hh]
                dz = g - sigs[hh] * (cg + pgs[hh])
                if diagonal:
                    dz = jnp.where(msk, dz, 0.0)
                run[hh] = cg + pgs[hh][:, T - 1:T]
                dzbuf[hh] = dz.astype(BF16)
                abuf[hh] = abuf_new
            apply(jp, adds)
            if not diagonal:
                for hh in range(2):
                    zbuf[hh] = zn[hh]
                    dabuf[hh] = dan[hh]

        def step(j, carry):
            block(j, False)
            return carry

        lax.fori_loop(0, i, step, 0)
        block(i, True)
        apply(i, flush(i))
        dq_ref[...] = (jnp.where(lane < 64, dq_acc[0], dq_acc[1]) * SB_SCALE).astype(dq_ref.dtype)

        @pl.when(i == NQ - 1)
        def _():
            dk_ref[...] = dk_acc[...].T.astype(dk_ref.dtype)
            dv_ref[...] = dv_acc[...].T.astype(dv_ref.dtype)

    W = SB_HEADS * SB_DIM
    return pl.pallas_call(
        body, name="sb_bwd", grid=(SB_HEADS // 2, NQ),
        in_specs=[pl.BlockSpec((T, 128), lambda h, i: (i, qb + h)),
                  pl.BlockSpec((S, 128), lambda h, i: (0, kb + h)),
                  pl.BlockSpec((S, 128), lambda h, i: (0, vb + h)),
                  pl.BlockSpec((T, 256), lambda h, i: (i, h)),
                  pl.BlockSpec((T, 128), lambda h, i: (i, h)),
                  pl.BlockSpec((2, T, T), lambda h, i: (0, 0, 0))],
        out_specs=[pl.BlockSpec((T, 128), lambda h, i: (i, h)),
                   pl.BlockSpec((S, 128), lambda h, i: (0, h)),
                   pl.BlockSpec((S, 128), lambda h, i: (0, h))],
        out_shape=[jax.ShapeDtypeStruct((S, W), BF16)] * 3,
        scratch_shapes=[pltpu.VMEM((128, S), F32), pltpu.VMEM((128, S), F32), pltpu.VMEM((2, T, 128), F32),
                        pltpu.VMEM((2, T, 1), F32), pltpu.VMEM((2, T, T), F32), pltpu.VMEM((2, T, T), F32),
                        pltpu.VMEM((2, T, T), BF16), pltpu.VMEM((2, T, T), BF16)],
        compiler_params=_cparams(("parallel", "arbitrary")),
    )(p, p, p, carries, dy, tri)


def _exchange(srcs, out_shapes, src_slice, dst_slice, name, deps=()):
    n, nd = len(srcs), len(deps)

    def body(*refs):
        ins, outs = refs[:n], refs[n + nd:2 * n + nd]
        send_sems, recv_sems, loc_sems = refs[2 * n + nd:]
        x, y, c = lax.axis_index("x"), lax.axis_index("y"), lax.axis_index("c")
        me = 4 * x + 2 * y + c
        local = [pltpu.make_async_copy(src_slice(t, ins[t], me), dst_slice(t, outs[t], me), loc_sems.at[t])
                 for t in range(n)]
        for cp in local:
            cp.start()
        sends, recvs = [], []
        for k in (1, 2, 4, 6, 3, 5, 7):
            px = 1 - x if k & 4 else x
            py = 1 - y if k & 2 else y
            pc = 1 - c if k & 1 else c
            peer = 4 * px + 2 * py + pc
            for t in range(n):
                s = t * 7 + k - 1
                sends.append(pltpu.make_async_remote_copy(
                    src_ref=src_slice(t, ins[t], peer), dst_ref=dst_slice(t, outs[t], me),
                    send_sem=send_sems.at[s], recv_sem=recv_sems.at[s],
                    device_id=(px, py, pc), device_id_type=pl.DeviceIdType.MESH))
                recvs.append(pltpu.make_async_remote_copy(
                    src_ref=src_slice(t, ins[t], me), dst_ref=dst_slice(t, outs[t], peer),
                    send_sem=send_sems.at[s], recv_sem=recv_sems.at[s],
                    device_id=(px, py, pc), device_id_type=pl.DeviceIdType.MESH))
        for cp in sends:
            cp.start()
        for cp in recvs:
            cp.wait_recv()
        for cp in sends:
            cp.wait_send()
        for cp in local:
            cp.wait()

    anyspec = pl.BlockSpec(memory_space=pl.ANY)
    return pl.pallas_call(
        body, name=name, in_specs=[anyspec] * (n + nd), out_specs=[anyspec] * n,
        out_shape=[jax.ShapeDtypeStruct(s, d) for s, d in out_shapes],
        scratch_shapes=[pltpu.SemaphoreType.DMA((7 * n,)), pltpu.SemaphoreType.DMA((7 * n,)),
                        pltpu.SemaphoreType.DMA((n,))],
    )(*srcs, *deps)


def _all_gather_lead(xs, name, deps=()):
    return _exchange(
        xs, [((N_DEV,) + x.shape, x.dtype) for x in xs],
        lambda t, ref, peer: ref, lambda t, ref, who: ref.at[who], name, deps)


def _all_to_all_lead(xs, name):
    return _exchange(
        xs, [(x.shape, x.dtype) for x in xs],
        lambda t, ref, peer: ref.at[peer], lambda t, ref, who: ref.at[who], name)


_W_AXIS = {"w_in": 1, "w_ret_out": 0, "w_sb_out": 0, "w_mix_out": 0, "w_up": 1, "w_down": 0}
_W_NAMES = tuple(_W_AXIS)


def _window(ref, axis, who, width, count=1):
    start = pl.multiple_of(who * width, width)
    return ref.at[pl.ds(start, count * width), :] if axis == 0 else ref.at[:, pl.ds(start, count * width)]


_HBM = pl.BlockSpec(memory_space=pltpu.HBM)
_SEM = pl.BlockSpec(memory_space=pltpu.SEMAPHORE)
_EFFECT = pltpu.SideEffectType.DATAFLOW_SIDE_EFFECTING


_ALL_PEERS = (0, 1, 2, 4, 6, 3, 5, 7)
_SAME_CORE = (0, 2, 4, 6)


def _exchange_start(srcs, shapes, src_slice, dst_slice, name, deps=(), ks=_ALL_PEERS):
    n, nd = len(srcs), len(deps)
    lands = [pltpu.with_memory_space_constraint(lax.empty(s, d), pltpu.HBM) for s, d in shapes]

    def body(*refs):
        ins, lnd = refs[:n], refs[n:2 * n]
        sems = refs[2 * n + nd:4 * n + nd]
        token = refs[6 * n + nd]
        x, y, c = lax.axis_index("x"), lax.axis_index("y"), lax.axis_index("c")
        me = 4 * x + 2 * y + c
        for k in ks:
            px = 1 - x if k & 4 else x
            py = 1 - y if k & 2 else y
            pc = 1 - c if k & 1 else c
            peer = 4 * px + 2 * py + pc
            for t in range(n):
                pltpu.make_async_remote_copy(
                    src_ref=src_slice(t, ins[t], peer), dst_ref=dst_slice(t, lnd[t], me),
                    send_sem=sems[2 * t], recv_sem=sems[2 * t + 1],
                    device_id=(px, py, pc), device_id_type=pl.DeviceIdType.MESH).start()
        token[...] = jnp.zeros_like(token)

    res = pl.pallas_call(
        body, name=name, in_specs=[_HBM] * (2 * n) + [pl.BlockSpec(memory_space=pl.ANY)] * nd,
        out_specs=[_SEM] * (2 * n) + [_HBM] * (2 * n) + [pl.BlockSpec(memory_space=pltpu.VMEM)],
        out_shape=[pltpu.SemaphoreType.DMA(())] * (2 * n) + [pltpu.HBM(s.shape, s.dtype) for s in srcs]
        + [pltpu.HBM(s.shape, s.dtype) for s in lands] + [jax.ShapeDtypeStruct((8, 128), F32)],
        input_output_aliases={t: 2 * n + t for t in range(2 * n)},
        compiler_params=pltpu.CompilerParams(has_side_effects=_EFFECT),
    )(*[pltpu.with_memory_space_constraint(s, pltpu.HBM) for s in srcs], *lands, *deps)
    return dict(n=n, sems=res[:2 * n], srcs=res[2 * n:3 * n], lands=res[3 * n:4 * n], token=res[4 * n])


def _exchange_wait(h, after, name):
    n, ns = h['n'], len(h['srcs'])
    span = h.get('span', lambda t, ref: ref)

    def body(*refs):
        lnd = refs[ns:ns + n]
        sems = refs[ns + n:ns + 3 * n]
        x, y, c = lax.axis_index("x"), lax.axis_index("y"), lax.axis_index("c")
        for t in range(n):
            w = span(t, lnd[t])
            cp = pltpu.make_async_remote_copy(src_ref=w, dst_ref=w, send_sem=sems[2 * t], recv_sem=sems[2 * t + 1],
                                              device_id=(x, y, 1 - c), device_id_type=pl.DeviceIdType.MESH)
            cp.wait_send()
            cp.wait_recv()

    after = list(after)
    res = pl.pallas_call(
        body, name=name,
        in_specs=[_HBM] * (ns + n) + [_SEM] * (2 * n) + [pl.BlockSpec(memory_space=pl.ANY)] * len(after),
        out_specs=[_HBM] * (ns + n),
        out_shape=[pltpu.HBM(s.shape, s.dtype) for s in h['srcs']] + [pltpu.HBM(s.shape, s.dtype) for s in h['lands']],
        input_output_aliases={t: t for t in range(ns + n)},
        compiler_params=pltpu.CompilerParams(has_side_effects=_EFFECT),
    )(*h['srcs'], *h['lands'], *h['sems'], *after)
    return list(res[ns:])


def _sibling_start(lands, win, name):
    n = len(lands)

    def body(*refs):
        lnd = refs[:n]
        sems = refs[n:3 * n]
        token = refs[4 * n]
        x, y, c = lax.axis_index("x"), lax.axis_index("y"), lax.axis_index("c")
        for ox in (x, 1 - x):
            for oy in (y, 1 - y):
                owner = 4 * ox + 2 * oy + c
                for t in range(n):
                    w = win(t, lnd[t], owner)
                    pltpu.make_async_remote_copy(
                        src_ref=w, dst_ref=w, send_sem=sems[2 * t], recv_sem=sems[2 * t + 1],
                        device_id=(x, y, 1 - c), device_id_type=pl.DeviceIdType.MESH).start()
        token[...] = jnp.zeros_like(token)

    res = pl.pallas_call(
        body, name=name, in_specs=[_HBM] * n,
        out_specs=[_SEM] * (2 * n) + [_HBM] * n + [pl.BlockSpec(memory_space=pltpu.VMEM)],
        out_shape=[pltpu.SemaphoreType.DMA(())] * (2 * n) + [pltpu.HBM(s.shape, s.dtype) for s in lands]
        + [jax.ShapeDtypeStruct((8, 128), F32)],
        input_output_aliases={t: 2 * n + t for t in range(n)},
        compiler_params=pltpu.CompilerParams(has_side_effects=_EFFECT),
    )(*lands)
    return dict(n=n, sems=res[:2 * n], srcs=[], lands=res[2 * n:3 * n], token=res[3 * n])


def _gather_start(shards, names, tag, deps=(), two_level=False):
    xs = [shards[nm] for nm in names]
    axes = [_W_AXIS[nm] for nm in names]
    widths = [x.shape[ax] for x, ax in zip(xs, axes)]
    shapes = [(tuple(d * (N_DEV if a == ax else 1) for a, d in enumerate(x.shape)), x.dtype) for x, ax in zip(xs, axes)]
    src = lambda t, ref, peer: ref
    dst = lambda t, ref, who: _window(ref, axes[t], who, widths[t])
    h = _exchange_start(xs, shapes, src, dst, "gw_start_" + tag, deps, _SAME_CORE if two_level else _ALL_PEERS)
    h['tag'] = "gw_wait_" + tag
    if two_level:
        h['span'] = lambda t, ref: _window(ref, axes[t], 0, widths[t], len(_SAME_CORE))
        h['second'] = (dst, "gw_pass_" + tag, "gw_passed_" + tag)
    return h


def _scatter_start(grads, names, tag):
    xs = [grads[nm] for nm in names]
    axes = [_W_AXIS[nm] for nm in names]
    widths = [x.shape[ax] // N_DEV for x, ax in zip(xs, axes)]
    shapes = [((N_DEV,) + tuple(d // (N_DEV if a == ax else 1) for a, d in enumerate(x.shape)), x.dtype)
              for x, ax in zip(xs, axes)]
    src = lambda t, ref, peer: _window(ref, axes[t], peer, widths[t])
    dst = lambda t, ref, who: ref.at[who]
    h = _exchange_start(xs, shapes, src, dst, "sg_start_" + tag)
    h['tag'] = "sg_wait_" + tag
    return h


def _finish(h, after):
    return _exchange_wait(h, after, h['tag'])


class _LayerWeights:
    def __init__(self, groups, started):
        self.groups = groups
        self.started = started
        self.got = {}
        self.after = None

    def __getitem__(self, nm):
        if nm not in self.got:
            for names, h in self.groups:
                if nm in names:
                    self.got.update(zip(names, _finish(h, list(self.after) + self.started)))
        return self.got[nm]


def _adam_math(p_ref, w, m, v):
    g = p_ref[0].astype(F32)
    for s in range(1, p_ref.shape[0]):
        g = g + p_ref[s].astype(F32)
    bc1 = 1.0 / (1.0 - ADAM_B1 ** ADAM_STEP)
    bc2 = 1.0 / (1.0 - ADAM_B2 ** ADAM_STEP)
    mm = ADAM_B1 * m + (1.0 - ADAM_B1) * g
    vv = ADAM_B2 * v + (1.0 - ADAM_B2) * jnp.square(g)
    return g, -ADAM_LR * ((mm * bc1) / (jnp.sqrt(vv * bc2) + ADAM_EPS) + ADAM_WD * w), mm, vv


def _adam(parts, w, m, v, name, tr=256):
    P, R, C = parts.shape
    tr = min(tr, R)
    assert R % tr == 0

    def body(p_ref, w_ref, m_ref, v_ref, *outs):
        for o_ref, val in zip(outs, _adam_math(p_ref, w_ref[...], m_ref[...], v_ref[...])):
            o_ref[...] = val

    spec = pl.BlockSpec((tr, C), lambda i: (i, 0))
    return pl.pallas_call(
        body, name=name, grid=(R // tr,),
        in_specs=[pl.BlockSpec((P, tr, C), lambda i: (0, i, 0)), spec, spec, spec],
        out_specs=[spec] * 4, out_shape=[jax.ShapeDtypeStruct((R, C), F32)] * 4,
        compiler_params=_cparams(("parallel",)),
    )(parts, w, m, v)


def _adam_layer(parts, w, m, v, l, prev, name, tr=256):
    P, R, C = parts.shape
    tr = min(tr, R)
    assert R % tr == 0 and w.shape == (DEPTH, R, C)
    npv = 0 if prev is None else 4

    def body(p_ref, w_ref, m_ref, v_ref, *rest):
        for o_ref, val in zip(rest[npv:], _adam_math(p_ref, w_ref[0], m_ref[0], v_ref[0])):
            o_ref[0] = val

    spec = pl.BlockSpec((1, tr, C), lambda i: (l, i, 0))
    return pl.pallas_call(
        body, name=name, grid=(R // tr,),
        in_specs=[pl.BlockSpec((P, tr, C), lambda i: (0, i, 0)), spec, spec, spec]
        + [pl.BlockSpec(memory_space=pl.ANY)] * npv,
        out_specs=[spec] * 4, out_shape=[jax.ShapeDtypeStruct((DEPTH, R, C), F32)] * 4,
        input_output_aliases={4 + t: t for t in range(npv)},
        compiler_params=_cparams(("parallel",)),
    )(parts, w, m, v, *([] if prev is None else prev))


def _mod_partial(cact_all, w_ada, l, b_ada_l):
    R, (_, D, n) = cact_all.shape[0], w_ada.shape

    def body(c_ref, w_ref, b_ref, o_ref):
        o_ref[...] = _dot(c_ref[...].astype(BF16), w_ref[...].astype(BF16)) + b_ref[...]

    return pl.pallas_call(
        body, name="mod_partial", grid=(1,), out_shape=jax.ShapeDtypeStruct((R, n), F32),
        in_specs=[pl.BlockSpec((R, D), lambda i: (0, 0)), pl.BlockSpec((None, D, n), lambda i: (l, 0, 0)),
                  pl.BlockSpec((1, n), lambda i: (0, 0))],
        out_specs=pl.BlockSpec((R, n), lambda i: (0, 0)),
        compiler_params=pltpu.CompilerParams(vmem_limit_bytes=VMEM_LIMIT),
    )(cact_all, w_ada, b_ada_l)


def _ada_grad(cact_t, dmod):
    D, n = cact_t.shape[0], dmod.shape[1]

    def body(c_ref, d_ref, o_ref):
        ct = c_ref[...].astype(BF16).astype(F32)
        dm = d_ref[...].astype(BF16).astype(F32)
        acc = ct[:, 0:1] * dm[0:1, :]
        for b in range(1, N_DEV):
            acc = acc + ct[:, b:b + 1] * dm[b:b + 1, :]
        o_ref[0] = acc

    return pl.pallas_call(
        body, name="ada_grad", out_shape=jax.ShapeDtypeStruct((1, D, n), F32),
        compiler_params=pltpu.CompilerParams(vmem_limit_bytes=VMEM_LIMIT),
    )(cact_t, dmod)


def _norm_mod(x, r, gv, sh):
    return x * r * gv + sh


def _silu(x):
    return x * _sigmoid(x)


def _rstd(x):
    return lax.rsqrt(jnp.mean(x * x, axis=-1, keepdims=True) + EPS)


def _residual_epi(acc, x, g):
    xn = x + g * acc
    return acc, xn, _rstd(xn)


def _residual_norm_epi(acc, x, g, gv, sh):
    xn = x + g * acc
    r = _rstd(xn)
    return acc, xn, _norm_mod(xn, r, gv, sh), r


def _layer_fwd(x0, r1, h1, mod, gn1, gn2, nxt, W, rc, tri):
    S = x0.shape[0]
    sh1, sc1, g1m, sh2, sc2, g2m = [mod[i:i + 1] for i in range(N_MOD)]
    gv1 = gn1 * (1.0 + sc1)
    gv2 = gn2 * (1.0 + sc2)
    W.after = [h1]
    (p,) = _mm(h1, W["w_in"], tm=2048, outs=(BF16,), name="mm_in")
    yret, rstate = _ret_fwd(p, rc)
    ysb, sbc = _sb_fwd(p, tri)
    W.after = [ysb]
    (ya,) = _mm(yret, W["w_ret_out"], tm=512, a_ex=[(p, 'a', O_RG)],
                pro=lambda yr, g: _silu(g.astype(F32)) * yr.astype(F32), outs=(BF16,), name="mm_ret_out")
    yb, mg = _mm(ysb, W["w_sb_out"], o_ex=[(ya, 'o'), (p, 'o', O_GA), (p, 'o', O_GB)],
                 epi=lambda acc, a, ga, gb: (acc, _sigmoid(ga.astype(F32)) * a.astype(F32)
                                             + _sigmoid(gb.astype(F32)) * acc),
                 outs=(BF16, BF16), name="mm_sb_out")
    mo, x1, h2, r2 = _mm(mg, W["w_mix_out"], tm=512, o_ex=[(x0, 'o'), (g1m, 'n'), (gv2, 'n'), (sh2, 'n')],
                         epi=_residual_norm_epi, outs=(BF16, F32, BF16), cols=1, name="mm_mix_out")
    (act,) = _mm(h2, W["w_up"], tm=2048, epi=lambda acc: (jnp.maximum(acc, 0.0),), outs=(BF16,), name="mm_up")
    if nxt is None:
        dn, x2, r_out = _mm(act, W["w_down"], tm=512, pro=lambda a: a * a, o_ex=[(x1, 'o'), (g2m, 'n')],
                            epi=_residual_epi, outs=(BF16, F32), cols=1, name="mm_down_last")
        h_out = None
    else:
        dn, x2, h_out, r_out = _mm(act, W["w_down"], tm=512, pro=lambda a: a * a,
                                   o_ex=[(x1, 'o'), (g2m, 'n'), (nxt[0], 'n'), (nxt[1], 'n')],
                                   epi=_residual_norm_epi, outs=(BF16, F32, BF16), cols=1, name="mm_down")
    saved = dict(x0=x0, r1=r1, h1=h1, p=p, yret=yret, rstate=rstate, ysb=ysb, sbc=sbc, ya=ya, yb=yb, mg=mg, mo=mo, x1=x1,
                 r2=r2, h2=h2, act=act, dn=dn, gv1=gv1, gv2=gv2, mod=mod, gn1=gn1, gn2=gn2)
    return x2, r_out, h_out, saved


def _norm_bwd(dh, x, r, dres, gv, gn, extra_rows=(), extra_vecs=(), extra_fn=None, extra_outs=(), name="norm_bwd"):
    D = x.shape[1]
    ne = len(extra_rows)

    def fn(dh_t, x_t, dres_t, *rest):
        er, rest = rest[:ne], rest[ne:]
        gv_t = rest[0]
        ev, r_t = rest[1:-1], rest[-1]
        xh = x_t * r_t
        dxh = dh_t * gv_t
        dx = r_t * (dxh - xh * jnp.mean(dxh * xh, axis=-1, keepdims=True)) + dres_t
        base = (dx, dh_t, dh_t * xh)
        if extra_fn is None:
            return base
        return base + tuple(extra_fn(dx, *er, *ev))

    return _ew(fn, [dh, x, dres] + list(extra_rows), vecs=[gv] + list(extra_vecs), cols=[r],
               outs=[('row', D, F32), ('sum', D), ('sum', D)] + list(extra_outs), name=name)


def _gate_bwd(dx, dn, g):
    return dx * dn.astype(F32), dx * g


_GATE_OUTS = [('sum', D_MODEL), ('row', D_MODEL, BF16)]


def _layer_bwd(dx2, d_g2m, d_dn, sv, below, W, rc, tri, emit):
    mod = sv['mod']
    sh1, sc1, g1m, sh2, sc2, g2m = [mod[i:i + 1] for i in range(N_MOD)]
    D = D_MODEL
    p = sv['p']
    (d_up,) = _mm(d_dn, W["w_down"], tb=True, tm=2048, o_ex=[(sv['act'], 'o')],
                  epi=lambda acc, a: (acc * 2.0 * a.astype(F32),), outs=(BF16,), name="mm_down_dx")
    (gw_down,) = _mm(sv['act'], d_dn, ta=True, tk=DW_TK, pro=lambda a: a * a, outs=(BF16,), name="mm_down_dw")
    (gw_up,) = _mm(sv['h2'], d_up, ta=True, tk=DW_TK, outs=(BF16,), name="mm_up_dw")
    tok = emit(dict(w_down=gw_down, w_up=gw_up), "mlp")
    (d_h2,) = _mm(d_up, W["w_up"], tb=True, tm=2048, tk=2048, outs=(F32,), name="mm_up_dx", deps=[tok])
    dx1, d_sh2, s_h2, d_g1m, d_mo = _norm_bwd(
        d_h2, sv['x1'], sv['r2'], dx2, sv['gv2'], sv['gn2'],
        extra_rows=[sv['mo']], extra_vecs=[g1m],
        extra_fn=lambda dx, mo, g: (dx * mo.astype(F32), dx * g),
        extra_outs=[('sum', D), ('row', D, BF16)], name="norm_bwd_mlp")
    d_sc2 = sv['gn2'] * s_h2
    d_gn2 = (1.0 + sc2) * s_h2
    def mix_epi(acc, ya, yb, ga, gb):
        sa, sb = _sigmoid(ga.astype(F32)), _sigmoid(gb.astype(F32))
        return (acc * sa, acc * sb, acc * ya.astype(F32) * sa * (1.0 - sa), acc * yb.astype(F32) * sb * (1.0 - sb))

    d_ya, d_yb, d_ga, d_gb = _mm(d_mo, W["w_mix_out"], tb=True, tm=512,
                                 o_ex=[(sv['ya'], 'o'), (sv['yb'], 'o'), (p, 'o', O_GA), (p, 'o', O_GB)], epi=mix_epi,
                                 outs=(BF16,) * 4, name="mm_mix_dx")
    (gw_mix,) = _mm(sv['mg'], d_mo, ta=True, tk=DW_TK, outs=(BF16,), name="mm_mix_dw")

    def ro_epi(acc, g, yr):
        gf = g.astype(F32)
        s = _sigmoid(gf)
        return (acc * yr.astype(F32) * s * (1.0 + gf * (1.0 - s)), acc * gf * s)

    d_rg, d_yret = _mm(d_ya, W["w_ret_out"], tb=True, tm=512, tn=2048, o_ex=[(p, 'o', O_RG), (sv['yret'], 'o')],
                       epi=ro_epi, outs=(BF16, BF16), name="mm_ret_dx")
    (gw_ro,) = _mm(sv['yret'], d_ya, ta=True, tm=512, tk=DW_TK, a_ex=[(p, 'a', O_RG)],
                   pro=lambda yr, g: _silu(g.astype(F32)) * yr.astype(F32), outs=(BF16,), name="mm_ret_dw")
    (gw_so,) = _mm(sv['ysb'], d_yb, ta=True, tk=DW_TK, outs=(BF16,), name="mm_sb_dw")
    tok = emit(dict(w_mix_out=gw_mix, w_ret_out=gw_ro, w_sb_out=gw_so), "mix")
    (d_ysb,) = _mm(d_yb, W["w_sb_out"], tb=True, outs=(BF16,), name="mm_sb_dx", deps=[tok])
    d_sq, d_sk, d_sv = _sb_bwd(p, sv['sbc'], d_ysb, tri)
    d_ret = _ret_bwd(p, sv['rstate'], d_yret, rc)
    dp = [d_ret, d_rg, d_sq, d_sk, d_sv, d_ga, d_gb]
    gw_in = _mm_t_parts_ring(sv['h1'], dp, name="mm_in_dw")
    tok = emit(dict(w_in=gw_in), "in")
    d_h = _mm_parts_ring(dp, W["w_in"], name="mm_in_dx", deps=[tok])
    if below is None:
        dx0, d_sh1, s_h1 = _norm_bwd(d_h, sv['x0'], sv['r1'], dx1, sv['gv1'], sv['gn1'], name="norm_bwd_mix")
        gate_below = (None, None)
    else:
        dx0, d_sh1, s_h1, *gate_below = _norm_bwd(
            d_h, sv['x0'], sv['r1'], dx1, sv['gv1'], sv['gn1'], extra_rows=[below['dn']],
            extra_vecs=[below['mod'][N_MOD - 1:N_MOD]], extra_fn=_gate_bwd, extra_outs=_GATE_OUTS,
            name="norm_bwd_mix_gate")
    d_sc1 = sv['gn1'] * s_h1
    d_gn1 = (1.0 + sc1) * s_h1
    d_mod = jnp.concatenate([d_sh1, d_sc1, d_g1m, d_sh2, d_sc2, d_g2m], axis=1)
    return dx0, gate_below, d_mod, d_gn1, d_gn2


def kernel(x, c, norm_mix_g, w_in, w_ret_out, w_sb_out, w_mix_out, norm_mlp_g, w_up, w_down, w_ada, b_ada, final_g, loss_target, m_norm_mix_g, m_w_in, m_w_ret_out, m_w_sb_out, m_w_mix_out, m_norm_mlp_g, m_w_up, m_w_down, m_w_ada, m_b_ada, m_final_g, v_norm_mix_g, v_w_in, v_w_ret_out, v_w_sb_out, v_w_mix_out, v_norm_mlp_g, v_w_up, v_w_down, v_w_ada, v_b_ada, v_final_g):
    S, D = x.shape[1], x.shape[2]
    x0 = x.reshape(S, D)
    tgt = loss_target.reshape(S, D)
    me = 4 * lax.axis_index("x") + 2 * lax.axis_index("y") + lax.axis_index("c")
    wts = dict(w_in=w_in, w_ret_out=w_ret_out, w_sb_out=w_sb_out, w_mix_out=w_mix_out, w_up=w_up, w_down=w_down)
    mts = dict(w_in=m_w_in, w_ret_out=m_w_ret_out, w_sb_out=m_w_sb_out, w_mix_out=m_w_mix_out, w_up=m_w_up, w_down=m_w_down)
    vts = dict(w_in=v_w_in, w_ret_out=v_w_ret_out, w_sb_out=v_w_sb_out, w_mix_out=v_w_mix_out, w_up=v_w_up, w_down=v_w_down)
    rc = _ret_consts(S)
    tri = _tri()

    (cact,) = _ew(lambda t: (_silu(t),), [jnp.pad(c, ((0, 7), (0, 0)))], outs=[('row', D, F32)], name="silu_c")
    (cact_all,) = _all_gather_lead([cact[0:1]], "gather_c")
    cact_all = cact_all.reshape(N_DEV, D)
    cact16 = jnp.pad(cact_all, ((0, 8), (0, 0)))
    n_ada = w_ada.shape[2]
    b_loc = lax.dynamic_slice_in_dim(b_ada, me * n_ada, n_ada, axis=1)
    mods = [_mod_partial(cact16, w_ada, l, b_loc[l:l + 1])[:N_DEV] for l in range(DEPTH)]
    modp = jnp.stack(mods, axis=1)
    (modr,) = _all_to_all_lead([modp], "scatter_mod")
    mod_full = jnp.transpose(modr, (1, 0, 2)).reshape(DEPTH, N_MOD, D)

    def cast(nm, l, deps=()):
        w = wts[nm]
        _, R, C = w.shape
        tr = min(256, R)

        def body(w_ref, *rest):
            rest[-1][...] = w_ref[...].astype(BF16)

        return pl.pallas_call(
            body, name="cast_bf16", grid=(R // tr,),
            in_specs=[pl.BlockSpec((None, tr, C), lambda i: (l, i, 0))]
            + [pl.BlockSpec(memory_space=pl.ANY)] * len(deps),
            out_specs=pl.BlockSpec((tr, C), lambda i: (i, 0)), out_shape=jax.ShapeDtypeStruct((R, C), BF16),
            compiler_params=_cparams(("arbitrary",)),
        )(w, *deps)

    rest = tuple(nm for nm in _W_NAMES if nm != "w_in")
    sh = [{} for _ in range(DEPTH)]
    sh[0]["w_in"] = cast("w_in", 0)
    h_in = _gather_start(sh[0], ("w_in",), "0_in", [modr], two_level=True)
    in_flight = [h_in['token']]
    for l in range(DEPTH):
        for nm in _W_NAMES:
            if nm not in sh[l]:
                sh[l][nm] = cast(nm, l, in_flight)

    pre = [(norm_mix_g[l:l + 1] * (1.0 + mod_full[l][1:2]), mod_full[l][0:1]) for l in range(DEPTH)]

    def first(t, gv, sh):
        r = _rstd(t)
        return r, _norm_mod(t, r, gv, sh)

    xs = x0
    rs, hs = _ew(first, [x0], vecs=list(pre[0]), outs=[('col', F32), ('row', D, BF16)], name="row_rstd",
                 deps=in_flight)

    win, pass_name, passed_name = h_in['second']
    before_wait = [hs] + [sh[l][nm] for l in range(DEPTH) for nm in _W_NAMES if (l, nm) != (0, "w_in")]
    h_pass = _sibling_start(_exchange_wait(h_in, before_wait, h_in['tag']), win, pass_name)
    h_pass['span'] = h_in['span']
    started = [h_pass['token']]
    layer_groups = []
    for l, groups in enumerate([[(rest, "0_rest")]] + [[(_W_NAMES, "%d_all" % l)] for l in range(1, DEPTH)]):
        layer_groups.append([])
        for names, tag in groups:
            layer_groups[-1].append((names, _gather_start(sh[l], names, tag, started[-1:])))
            started.append(layer_groups[-1][-1][1]['token'])
    (w_in0,) = _exchange_wait(h_pass, started, passed_name)
    layer_w = [_LayerWeights(g, started) for g in layer_groups]
    layer_w[0].got["w_in"] = w_in0
    saved = []
    for l in range(DEPTH):
        xs, rs, hs, sv = _layer_fwd(xs, rs, hs, mod_full[l], norm_mix_g[l:l + 1], norm_mlp_g[l:l + 1],
                                    pre[l + 1] if l + 1 < DEPTH else None, layer_w[l], rc, tri)
        sv['W'] = layer_w[l]
        saved.append(sv)

    fg = final_g.reshape(1, D)

    def head(xt, tg, dn, g, g2m, r):
        xh = xt * r
        e = xh * g - tg
        dy = e * (1.0 / D)
        dxh = dy * g
        dx = r * (dxh - xh * jnp.mean(dxh * xh, axis=-1, keepdims=True))
        return (dx, dy * xh, 0.5 * e * e * (1.0 / D)) + _gate_bwd(dx, dn, g2m)

    top = saved[DEPTH - 1]
    dxs, d_fg, loss_cols, *gate = _ew(head, [xs, tgt, top['dn']], vecs=[fg, top['mod'][N_MOD - 1:N_MOD]], cols=[rs],
                                      outs=[('row', D, F32), ('sum', D), ('sum', D)] + _GATE_OUTS, name="loss_head")

    small = [None] * DEPTH
    pending = []
    for l in reversed(range(DEPTH)):
        sv = saved[l]

        def emit(gw, tag, l=l):
            names = tuple(gw)
            pending.append((l, names, _scatter_start(gw, names, "%d_%s" % (l, tag))))
            return pending[-1][2]['token']

        dxs, gate, d_mod, d_gn1, d_gn2 = _layer_bwd(dxs, gate[0], gate[1], sv, saved[l - 1] if l else None,
                                                    sv['W'], rc, tri, emit)
        small[l] = (d_mod, d_gn1, d_gn2)
    grad_x = dxs.reshape(1, S, D)

    res = {}
    after = [dxs]
    for l, names, h in pending:
        for nm, landed in zip(names, _finish(h, after)):
            res[nm] = _adam_layer(landed, wts[nm], mts[nm], vts[nm], l, res.get(nm), "adam_layer")
        after = [res[nm][0] for nm in names]

    pack = jnp.concatenate([small[l][0] for l in range(DEPTH)] + [small[l][1] for l in range(DEPTH)]
                           + [small[l][2] for l in range(DEPTH)] + [d_fg, loss_cols], axis=1)
    (packs,) = _all_gather_lead([pack], "gather_small", deps=after)
    packs = packs.reshape(N_DEV, -1)
    o = 0
    dmod_all = []
    for l in range(DEPTH):
        dmod_all.append(packs[:, o:o + N_MOD * D]); o += N_MOD * D
    gn1_parts = packs[:, o:o + DEPTH * D].reshape(N_DEV, DEPTH, D); o += DEPTH * D
    gn2_parts = packs[:, o:o + DEPTH * D].reshape(N_DEV, DEPTH, D); o += DEPTH * D
    fg_parts = packs[:, o:o + D].reshape(N_DEV, 1, D); o += D
    loss_parts = packs[:, o:o + D]
    (loss_sum,) = _ew(lambda t: (t,), [loss_parts], outs=[('sum', D)], name="loss_sum")
    loss = jnp.sum(loss_sum)

    res["norm_mix_g"] = _adam(gn1_parts, norm_mix_g, m_norm_mix_g, v_norm_mix_g, "adam")
    res["norm_mlp_g"] = _adam(gn2_parts, norm_mlp_g, m_norm_mlp_g, v_norm_mlp_g, "adam")
    fgr = _adam(fg_parts, fg, m_final_g.reshape(1, D), v_final_g.reshape(1, D), "adam")
    res["final_g"] = [t.reshape(D) for t in fgr]
    bparts = jnp.stack(dmod_all, axis=1)
    res["b_ada"] = _adam(bparts, b_ada, m_b_ada, v_b_ada, "adam")
    cact_t = cact_all.T
    for l in range(DEPTH):
        dm_loc = lax.dynamic_slice_in_dim(dmod_all[l], me * n_ada, n_ada, axis=1)
        res["w_ada"] = _adam_layer(_ada_grad(cact_t, dm_loc), w_ada, m_w_ada, v_w_ada, l, res.get("w_ada"),
                                   "adam_layer")

    order = ['norm_mix_g', 'w_in', 'w_ret_out', 'w_sb_out', 'w_mix_out', 'norm_mlp_g', 'w_up', 'w_down', 'w_ada', 'b_ada', 'final_g']
    out = [loss, grad_x]
    for i in range(4):
        out += [res[nm][i] for nm in order]
    return tuple(out)
```
